```python
import jax, jax.numpy as jnp
from jax import lax
import numpy as np

D_MODEL = 1024
BATCH = 8
SEQ = 8192
DEPTH = 2

SSD_HEADS = 16
SSD_HEAD_DIM = 64
SSD_INNER = SSD_HEADS * SSD_HEAD_DIM
SSD_GROUPS = 2
SSD_STATE = 64
SSD_CONV = 4
SSD_CHUNK = 128
SSD_XBC = SSD_INNER + 2 * SSD_GROUPS * SSD_STATE
CONF_WIDTH = 512
CONF_KERNEL = 31
SC_WIDTH = 512
SC_KERNEL = 3
N_BRANCH = 3
D_FF = 2816
FFN_KERNEL = 3
EPS = 1e-6

OFF_Z = SSD_INNER
OFF_XBC = OFF_Z + SSD_XBC
OFF_DT = OFF_XBC + SSD_HEADS
OFF_CONF = OFF_DT + 2 * CONF_WIDTH
OFF_SC = OFF_CONF + 3 * SC_WIDTH
N_IN = OFF_SC + N_BRANCH * D_MODEL
IN_SPLITS = (OFF_Z, OFF_XBC, OFF_DT, OFF_CONF, OFF_SC)

kernel_name = "hybrid_ssd_conformer_shortconv_adaln"


def rms_norm(x, g):
    xf = x.astype(jnp.float32)
    y = xf * lax.rsqrt(jnp.mean(xf * xf, axis=-1, keepdims=True) + EPS)
    return (y * g).astype(x.dtype)


def layer_norm(x, g, b):
    xf = x.astype(jnp.float32)
    mu = jnp.mean(xf, axis=-1, keepdims=True)
    xc = xf - mu
    y = xc * lax.rsqrt(jnp.mean(xc * xc, axis=-1, keepdims=True) + EPS)
    return (y * g + b).astype(x.dtype)


def gated_group_rmsnorm(y, z, g):
    v = (y * jax.nn.silu(z)).astype(jnp.float32)
    v = v.reshape(*v.shape[:-1], SSD_GROUPS, -1)
    v = v * lax.rsqrt(jnp.mean(v * v, axis=-1, keepdims=True) + EPS)
    return (v.reshape(y.shape) * g).astype(z.dtype)


def causal_dwconv(x, w, b=None):
    k = w.shape[0]
    y = lax.conv_general_dilated(
        x, w[:, None, :].astype(x.dtype), window_strides=(1,), padding=[(k - 1, 0)],
        dimension_numbers=('NWC', 'WIO', 'NWC'), feature_group_count=x.shape[-1])
    return y if b is None else y + b


def adaln(c, w, b):
    mod = jax.nn.silu(c) @ w + b
    shift, scale, gate = jnp.split(mod[:, None, :], 3, axis=-1)
    return shift, scale, gate


def ssd_chunked(x, dt, a, b_mat, c_mat):
    bsz, l, h, p = x.shape
    g, n = b_mat.shape[-2:]
    r = h // g
    q = SSD_CHUNK
    nc = l // q
    xc = (x * dt[..., None]).reshape(bsz, nc, q, g, r, p)
    a_dt = (dt * a).astype(jnp.float32).reshape(bsz, nc, q, g, r)
    a_cum = jnp.cumsum(jnp.moveaxis(a_dt, 2, -1), axis=-1)
    bc = b_mat.reshape(bsz, nc, q, g, n)
    cc = c_mat.reshape(bsz, nc, q, g, n)
    causal = jnp.tril(jnp.ones((q, q), dtype=bool))
    seg = a_cum[..., :, None] - a_cum[..., None, :]
    decay = jnp.exp(jnp.where(causal, seg, -jnp.inf))
    cb = jnp.einsum('bclgn,bcsgn->bcgls', cc, bc)
    wts = cb[:, :, :, None] * decay
    y_diag = jnp.einsum('bcgrls,bcsgrp->bclgrp', wts, xc)
    decay_states = jnp.exp(a_cum[..., -1:] - a_cum)
    xd = xc * jnp.moveaxis(decay_states, -1, 2)[..., None]
    states = jnp.einsum('bcsgn,bcsgrp->bcgrpn', bc, xd)
    chunk_decay = jnp.exp(a_cum[..., -1])

    def step(hstate, inp):
        s, d = inp
        return hstate * d[..., None, None] + s, hstate

    init = jnp.zeros((bsz, g, r, p, n), dtype=states.dtype)
    _, prev = lax.scan(step, init, (jnp.moveaxis(states, 1, 0), jnp.moveaxis(chunk_decay, 1, 0)))
    prev = jnp.moveaxis(prev, 0, 1)
    decay_out = jnp.exp(jnp.moveaxis(a_cum, -1, 2))[..., None]
    y_off = jnp.einsum('bclgn,bcgrpn->bclgrp', cc, prev) * decay_out
    return (y_diag + y_off).reshape(bsz, l, h, p)


def token_mixers(h, w_in, b_gate, ssd_conv_w, ssd_conv_b, ssd_dt_bias, ssd_a_log, ssd_d,
                 ssd_norm_g, w_ssd_out, conf_conv_w, conf_conv_b, conf_ln_g, conf_ln_b,
                 w_conf_out, sc_conv_w, w_sc_out, w_o):
    bsz, l, _ = h.shape
    proj = h @ w_in
    z, xbc, dt, conf_in, sc_in, gates = jnp.split(proj, IN_SPLITS, axis=-1)
    xbc = jax.nn.silu(causal_dwconv(xbc, ssd_conv_w, ssd_conv_b))
    xs, bm, cm = jnp.split(xbc, [SSD_INNER, SSD_INNER + SSD_GROUPS * SSD_STATE], axis=-1)
    dt = jax.nn.softplus((dt + ssd_dt_bias).astype(jnp.float32))
    a = -jnp.exp(ssd_a_log.astype(jnp.float32))
    xs = xs.reshape(bsz, l, SSD_HEADS, SSD_HEAD_DIM)
    y = ssd_chunked(xs, dt, a,
                    bm.reshape(bsz, l, SSD_GROUPS, SSD_STATE),
                    cm.reshape(bsz, l, SSD_GROUPS, SSD_STATE))
    y = (y + xs * ssd_d[:, None]).reshape(bsz, l, SSD_INNER)
    y_a = gated_group_rmsnorm(y, z, ssd_norm_g) @ w_ssd_out
    u_val, u_gate = jnp.split(conf_in, 2, axis=-1)
    u = u_val * jax.nn.sigmoid(u_gate)
    u = causal_dwconv(u, conf_conv_w, conf_conv_b)
    u = layer_norm(u, conf_ln_g, conf_ln_b)
    y_b = jax.nn.silu(u) @ w_conf_out
    gb, gc, xv = jnp.split(sc_in, 3, axis=-1)
    y_c = (gb * causal_dwconv(gc * xv, sc_conv_w)) @ w_sc_out
    g_a, g_b, g_c = jnp.split(jax.nn.sigmoid(gates + b_gate), 3, axis=-1)
    merged = g_a * y_a + g_b * y_b + g_c * y_c
    return (merged @ w_o).astype(h.dtype)


def conv_ffn(h, w_up, conv_w, conv_b, w_down):
    u = causal_dwconv(h @ w_up, conv_w, conv_b)
    gate, val = jnp.split(u, 2, axis=-1)
    return (jax.nn.silu(gate) * val) @ w_down


def _fwd_setup_inputs(seed: int = 0) -> dict:
    key = jax.random.key(seed)
    ks = iter(jax.random.split(key, 40))

    def nrm(shape, scale):
        return jax.random.normal(next(ks), shape, jnp.float32) * scale

    def gain(shape):
        return 1.0 + nrm(shape, 0.02)

    dt0 = jnp.exp(jax.random.uniform(next(ks), (DEPTH, SSD_HEADS), jnp.float32,
                                     np.float32(np.log(1e-3)), np.float32(np.log(1e-1))))
    return {
        "x": nrm((BATCH, SEQ, D_MODEL), 1.0),
        "c": nrm((BATCH, D_MODEL), 1.0),
        "ada_mix_w": nrm((DEPTH, D_MODEL, 3 * D_MODEL), D_MODEL ** -0.5),
        "ada_mix_b": nrm((DEPTH, 3 * D_MODEL), 0.02),
        "norm_mix_g": gain((DEPTH, D_MODEL)),
        "w_in": nrm((DEPTH, D_MODEL, N_IN), D_MODEL ** -0.5),
        "b_gate": nrm((DEPTH, N_BRANCH * D_MODEL), 0.02),
        "ssd_conv_w": nrm((DEPTH, SSD_CONV, SSD_XBC), SSD_CONV ** -0.5),
        "ssd_conv_b": nrm((DEPTH, SSD_XBC), 0.02),
        "ssd_dt_bias": dt0 + jnp.log(-jnp.expm1(-dt0)),
        "ssd_a_log": jnp.log(jax.random.uniform(next(ks), (DEPTH, SSD_HEADS), jnp.float32, 1.0, 16.0)),
        "ssd_d": gain((DEPTH, SSD_HEADS)),
        "ssd_norm_g": gain((DEPTH, SSD_INNER)),
        "w_ssd_out": nrm((DEPTH, SSD_INNER, D_MODEL), SSD_INNER ** -0.5),
        "conf_conv_w": nrm((DEPTH, CONF_KERNEL, CONF_WIDTH), CONF_KERNEL ** -0.5),
        "conf_conv_b": nrm((DEPTH, CONF_WIDTH), 0.02),
        "conf_ln_g": gain((DEPTH, CONF_WIDTH)),
        "conf_ln_b": nrm((DEPTH, CONF_WIDTH), 0.02),
        "w_conf_out": nrm((DEPTH, CONF_WIDTH, D_MODEL), CONF_WIDTH ** -0.5),
        "sc_conv_w": nrm((DEPTH, SC_KERNEL, SC_WIDTH), SC_KERNEL ** -0.5),
        "w_sc_out": nrm((DEPTH, SC_WIDTH, D_MODEL), SC_WIDTH ** -0.5),
        "w_o": nrm((DEPTH, D_MODEL, D_MODEL), D_MODEL ** -0.5),
        "ada_ffn_w": nrm((DEPTH, D_MODEL, 3 * D_MODEL), D_MODEL ** -0.5),
        "ada_ffn_b": nrm((DEPTH, 3 * D_MODEL), 0.02),
        "norm_ffn_g": gain((DEPTH, D_MODEL)),
        "w_up": nrm((DEPTH, D_MODEL, 2 * D_FF), D_MODEL ** -0.5),
        "ffn_conv_w": nrm((DEPTH, FFN_KERNEL, 2 * D_FF), FFN_KERNEL ** -0.5),
        "ffn_conv_b": nrm((DEPTH, 2 * D_FF), 0.02),
        "w_down": nrm((DEPTH, D_FF, D_MODEL), D_FF ** -0.5),
        "final_norm_g": gain((D_MODEL,)),
    }


def _fwd_reference(x, c, ada_mix_w, ada_mix_b, norm_mix_g, w_in, b_gate, ssd_conv_w, ssd_conv_b,
              ssd_dt_bias, ssd_a_log, ssd_d, ssd_norm_g, w_ssd_out, conf_conv_w, conf_conv_b,
              conf_ln_g, conf_ln_b, w_conf_out, sc_conv_w, w_sc_out, w_o, ada_ffn_w, ada_ffn_b,
              norm_ffn_g, w_up, ffn_conv_w, ffn_conv_b, w_down, final_norm_g):
    for i in range(DEPTH):
        shift, scale, gate = adaln(c, ada_mix_w[i], ada_mix_b[i])
        h = rms_norm(x, norm_mix_g[i]) * (1 + scale) + shift
        mix = token_mixers(h, w_in[i], b_gate[i], ssd_conv_w[i], ssd_conv_b[i], ssd_dt_bias[i],
                           ssd_a_log[i], ssd_d[i], ssd_norm_g[i], w_ssd_out[i], conf_conv_w[i],
                           conf_conv_b[i], conf_ln_g[i], conf_ln_b[i], w_conf_out[i],
                           sc_conv_w[i], w_sc_out[i], w_o[i])
        x = x + (gate * mix).astype(x.dtype)
        shift, scale, gate = adaln(c, ada_ffn_w[i], ada_ffn_b[i])
        h = rms_norm(x, norm_ffn_g[i]) * (1 + scale) + shift
        x = x + (gate * conv_ffn(h, w_up[i], ffn_conv_w[i], ffn_conv_b[i], w_down[i])).astype(x.dtype)
    return rms_norm(x, final_norm_g)


import jax as _jax
import jax.numpy as _jnp

TWIN_FORMAT = 'train_step'
FWD_PARAMS = ['x', 'c', 'ada_mix_w', 'ada_mix_b', 'norm_mix_g', 'w_in', 'b_gate', 'ssd_conv_w', 'ssd_conv_b', 'ssd_dt_bias', 'ssd_a_log', 'ssd_d', 'ssd_norm_g', 'w_ssd_out', 'conf_conv_w', 'conf_conv_b', 'conf_ln_g', 'conf_ln_b', 'w_conf_out', 'sc_conv_w', 'w_sc_out', 'w_o', 'ada_ffn_w', 'ada_ffn_b', 'norm_ffn_g', 'w_up', 'ffn_conv_w', 'ffn_conv_b', 'w_down', 'final_norm_g']
TWIN_WEIGHTS = ['ada_mix_w', 'ada_mix_b', 'norm_mix_g', 'w_in', 'b_gate', 'ssd_conv_w', 'ssd_conv_b', 'ssd_dt_bias', 'ssd_a_log', 'ssd_d', 'ssd_norm_g', 'w_ssd_out', 'conf_conv_w', 'conf_conv_b', 'conf_ln_g', 'conf_ln_b', 'w_conf_out', 'sc_conv_w', 'w_sc_out', 'w_o', 'ada_ffn_w', 'ada_ffn_b', 'norm_ffn_g', 'w_up', 'ffn_conv_w', 'ffn_conv_b', 'w_down', 'final_norm_g']
TWIN_DIFF_INPUT = 'x'
TWIN_INPUTS = ['x', 'c', 'ada_mix_w', 'ada_mix_b', 'norm_mix_g', 'w_in', 'b_gate', 'ssd_conv_w', 'ssd_conv_b', 'ssd_dt_bias', 'ssd_a_log', 'ssd_d', 'ssd_norm_g', 'w_ssd_out', 'conf_conv_w', 'conf_conv_b', 'conf_ln_g', 'conf_ln_b', 'w_conf_out', 'sc_conv_w', 'w_sc_out', 'w_o', 'ada_ffn_w', 'ada_ffn_b', 'norm_ffn_g', 'w_up', 'ffn_conv_w', 'ffn_conv_b', 'w_down', 'final_norm_g', 'loss_target', 'm_ada_mix_w', 'm_ada_mix_b', 'm_norm_mix_g', 'm_w_in', 'm_b_gate', 'm_ssd_conv_w', 'm_ssd_conv_b', 'm_ssd_dt_bias', 'm_ssd_a_log', 'm_ssd_d', 'm_ssd_norm_g', 'm_w_ssd_out', 'm_conf_conv_w', 'm_conf_conv_b', 'm_conf_ln_g', 'm_conf_ln_b', 'm_w_conf_out', 'm_sc_conv_w', 'm_w_sc_out', 'm_w_o', 'm_ada_ffn_w', 'm_ada_ffn_b', 'm_norm_ffn_g', 'm_w_up', 'm_ffn_conv_w', 'm_ffn_conv_b', 'm_w_down', 'm_final_norm_g', 'v_ada_mix_w', 'v_ada_mix_b', 'v_norm_mix_g', 'v_w_in', 'v_b_gate', 'v_ssd_conv_w', 'v_ssd_conv_b', 'v_ssd_dt_bias', 'v_ssd_a_log', 'v_ssd_d', 'v_ssd_norm_g', 'v_w_ssd_out', 'v_conf_conv_w', 'v_conf_conv_b', 'v_conf_ln_g', 'v_conf_ln_b', 'v_w_conf_out', 'v_sc_conv_w', 'v_w_sc_out', 'v_w_o', 'v_ada_ffn_w', 'v_ada_ffn_b', 'v_norm_ffn_g', 'v_w_up', 'v_ffn_conv_w', 'v_ffn_conv_b', 'v_w_down', 'v_final_norm_g']
TWIN_OUTPUTS = ['loss', 'grad_x', 'grad_ada_mix_w', 'grad_ada_mix_b', 'grad_norm_mix_g', 'grad_w_in', 'grad_b_gate', 'grad_ssd_conv_w', 'grad_ssd_conv_b', 'grad_ssd_dt_bias', 'grad_ssd_a_log', 'grad_ssd_d', 'grad_ssd_norm_g', 'grad_w_ssd_out', 'grad_conf_conv_w', 'grad_conf_conv_b', 'grad_conf_ln_g', 'grad_conf_ln_b', 'grad_w_conf_out', 'grad_sc_conv_w', 'grad_w_sc_out', 'grad_w_o', 'grad_ada_ffn_w', 'grad_ada_ffn_b', 'grad_norm_ffn_g', 'grad_w_up', 'grad_ffn_conv_w', 'grad_ffn_conv_b', 'grad_w_down', 'grad_final_norm_g', 'delta_ada_mix_w', 'delta_ada_mix_b', 'delta_norm_mix_g', 'delta_w_in', 'delta_b_gate', 'delta_ssd_conv_w', 'delta_ssd_conv_b', 'delta_ssd_dt_bias', 'delta_ssd_a_log', 'delta_ssd_d', 'delta_ssd_norm_g', 'delta_w_ssd_out', 'delta_conf_conv_w', 'delta_conf_conv_b', 'delta_conf_ln_g', 'delta_conf_ln_b', 'delta_w_conf_out', 'delta_sc_conv_w', 'delta_w_sc_out', 'delta_w_o', 'delta_ada_ffn_w', 'delta_ada_ffn_b', 'delta_norm_ffn_g', 'delta_w_up', 'delta_ffn_conv_w', 'delta_ffn_conv_b', 'delta_w_down', 'delta_final_norm_g', 'new_m_ada_mix_w', 'new_m_ada_mix_b', 'new_m_norm_mix_g', 'new_m_w_in', 'new_m_b_gate', 'new_m_ssd_conv_w', 'new_m_ssd_conv_b', 'new_m_ssd_dt_bias', 'new_m_ssd_a_log', 'new_m_ssd_d', 'new_m_ssd_norm_g', 'new_m_w_ssd_out', 'new_m_conf_conv_w', 'new_m_conf_conv_b', 'new_m_conf_ln_g', 'new_m_conf_ln_b', 'new_m_w_conf_out', 'new_m_sc_conv_w', 'new_m_w_sc_out', 'new_m_w_o', 'new_m_ada_ffn_w', 'new_m_ada_ffn_b', 'new_m_norm_ffn_g', 'new_m_w_up', 'new_m_ffn_conv_w', 'new_m_ffn_conv_b', 'new_m_w_down', 'new_m_final_norm_g', 'new_v_ada_mix_w', 'new_v_ada_mix_b', 'new_v_norm_mix_g', 'new_v_w_in', 'new_v_b_gate', 'new_v_ssd_conv_w', 'new_v_ssd_conv_b', 'new_v_ssd_dt_bias', 'new_v_ssd_a_log', 'new_v_ssd_d', 'new_v_ssd_norm_g', 'new_v_w_ssd_out', 'new_v_conf_conv_w', 'new_v_conf_conv_b', 'new_v_conf_ln_g', 'new_v_conf_ln_b', 'new_v_w_conf_out', 'new_v_sc_conv_w', 'new_v_w_sc_out', 'new_v_w_o', 'new_v_ada_ffn_w', 'new_v_ada_ffn_b', 'new_v_norm_ffn_g', 'new_v_w_up', 'new_v_ffn_conv_w', 'new_v_ffn_conv_b', 'new_v_w_down', 'new_v_final_norm_g']
TWIN_LEAF_KINDS = {'loss': 'loss', 'grad_x': 'grad_x', 'grad_ada_mix_w': 'grad_w', 'grad_ada_mix_b': 'grad_w', 'grad_norm_mix_g': 'grad_w', 'grad_w_in': 'grad_w', 'grad_b_gate': 'grad_w', 'grad_ssd_conv_w': 'grad_w', 'grad_ssd_conv_b': 'grad_w', 'grad_ssd_dt_bias': 'grad_w', 'grad_ssd_a_log': 'grad_w', 'grad_ssd_d': 'grad_w', 'grad_ssd_norm_g': 'grad_w', 'grad_w_ssd_out': 'grad_w', 'grad_conf_conv_w': 'grad_w', 'grad_conf_conv_b': 'grad_w', 'grad_conf_ln_g': 'grad_w', 'grad_conf_ln_b': 'grad_w', 'grad_w_conf_out': 'grad_w', 'grad_sc_conv_w': 'grad_w', 'grad_w_sc_out': 'grad_w', 'grad_w_o': 'grad_w', 'grad_ada_ffn_w': 'grad_w', 'grad_ada_ffn_b': 'grad_w', 'grad_norm_ffn_g': 'grad_w', 'grad_w_up': 'grad_w', 'grad_ffn_conv_w': 'grad_w', 'grad_ffn_conv_b': 'grad_w', 'grad_w_down': 'grad_w', 'grad_final_norm_g': 'grad_w', 'delta_ada_mix_w': 'delta_w', 'delta_ada_mix_b': 'delta_w', 'delta_norm_mix_g': 'delta_w', 'delta_w_in': 'delta_w', 'delta_b_gate': 'delta_w', 'delta_ssd_conv_w': 'delta_w', 'delta_ssd_conv_b': 'delta_w', 'delta_ssd_dt_bias': 'delta_w', 'delta_ssd_a_log': 'delta_w', 'delta_ssd_d': 'delta_w', 'delta_ssd_norm_g': 'delta_w', 'delta_w_ssd_out': 'delta_w', 'delta_conf_conv_w': 'delta_w', 'delta_conf_conv_b': 'delta_w', 'delta_conf_ln_g': 'delta_w', 'delta_conf_ln_b': 'delta_w', 'delta_w_conf_out': 'delta_w', 'delta_sc_conv_w': 'delta_w', 'delta_w_sc_out': 'delta_w', 'delta_w_o': 'delta_w', 'delta_ada_ffn_w': 'delta_w', 'delta_ada_ffn_b': 'delta_w', 'delta_norm_ffn_g': 'delta_w', 'delta_w_up': 'delta_w', 'delta_ffn_conv_w': 'delta_w', 'delta_ffn_conv_b': 'delta_w', 'delta_w_down': 'delta_w', 'delta_final_norm_g': 'delta_w', 'new_m_ada_mix_w': 'new_m', 'new_m_ada_mix_b': 'new_m', 'new_m_norm_mix_g': 'new_m', 'new_m_w_in': 'new_m', 'new_m_b_gate': 'new_m', 'new_m_ssd_conv_w': 'new_m', 'new_m_ssd_conv_b': 'new_m', 'new_m_ssd_dt_bias': 'new_m', 'new_m_ssd_a_log': 'new_m', 'new_m_ssd_d': 'new_m', 'new_m_ssd_norm_g': 'new_m', 'new_m_w_ssd_out': 'new_m', 'new_m_conf_conv_w': 'new_m', 'new_m_conf_conv_b': 'new_m', 'new_m_conf_ln_g': 'new_m', 'new_m_conf_ln_b': 'new_m', 'new_m_w_conf_out': 'new_m', 'new_m_sc_conv_w': 'new_m', 'new_m_w_sc_out': 'new_m', 'new_m_w_o': 'new_m', 'new_m_ada_ffn_w': 'new_m', 'new_m_ada_ffn_b': 'new_m', 'new_m_norm_ffn_g': 'new_m', 'new_m_w_up': 'new_m', 'new_m_ffn_conv_w': 'new_m', 'new_m_ffn_conv_b': 'new_m', 'new_m_w_down': 'new_m', 'new_m_final_norm_g': 'new_m', 'new_v_ada_mix_w': 'new_v', 'new_v_ada_mix_b': 'new_v', 'new_v_norm_mix_g': 'new_v', 'new_v_w_in': 'new_v', 'new_v_b_gate': 'new_v', 'new_v_ssd_conv_w': 'new_v', 'new_v_ssd_conv_b': 'new_v', 'new_v_ssd_dt_bias': 'new_v', 'new_v_ssd_a_log': 'new_v', 'new_v_ssd_d': 'new_v', 'new_v_ssd_norm_g': 'new_v', 'new_v_w_ssd_out': 'new_v', 'new_v_conf_conv_w': 'new_v', 'new_v_conf_conv_b': 'new_v', 'new_v_conf_ln_g': 'new_v', 'new_v_conf_ln_b': 'new_v', 'new_v_w_conf_out': 'new_v', 'new_v_sc_conv_w': 'new_v', 'new_v_w_sc_out': 'new_v', 'new_v_w_o': 'new_v', 'new_v_ada_ffn_w': 'new_v', 'new_v_ada_ffn_b': 'new_v', 'new_v_norm_ffn_g': 'new_v', 'new_v_w_up': 'new_v', 'new_v_ffn_conv_w': 'new_v', 'new_v_ffn_conv_b': 'new_v', 'new_v_w_down': 'new_v', 'new_v_final_norm_g': 'new_v'}


def _forward(args):
    return _fwd_reference(*[args[k] for k in FWD_PARAMS])


def _output_shape():
    def fwd():
        inp = _fwd_setup_inputs(0)
        return _fwd_reference(*[inp[k] for k in FWD_PARAMS])
    out = _jax.eval_shape(fwd)
    return out.shape, out.dtype

N_MICROBATCH = 1
ADAM_LR = 0.001
ADAM_B1 = 0.9
ADAM_B2 = 0.999
ADAM_EPS = 1e-08
ADAM_WD = 0.01
ADAM_STEP = 10
PER_EXAMPLE_BATCH_AXIS = {'x': 0, 'c': 0, 'loss_target': 0}
SHARED_INPUTS = []
_WEIGHT_DTYPES = {'ada_mix_w': _jnp.float32, 'ada_mix_b': _jnp.float32, 'norm_mix_g': _jnp.float32, 'w_in': _jnp.float32, 'b_gate': _jnp.float32, 'ssd_conv_w': _jnp.float32, 'ssd_conv_b': _jnp.float32, 'ssd_dt_bias': _jnp.float32, 'ssd_a_log': _jnp.float32, 'ssd_d': _jnp.float32, 'ssd_norm_g': _jnp.float32, 'w_ssd_out': _jnp.float32, 'conf_conv_w': _jnp.float32, 'conf_conv_b': _jnp.float32, 'conf_ln_g': _jnp.float32, 'conf_ln_b': _jnp.float32, 'w_conf_out': _jnp.float32, 'sc_conv_w': _jnp.float32, 'w_sc_out': _jnp.float32, 'w_o': _jnp.float32, 'ada_ffn_w': _jnp.float32, 'ada_ffn_b': _jnp.float32, 'norm_ffn_g': _jnp.float32, 'w_up': _jnp.float32, 'ffn_conv_w': _jnp.float32, 'ffn_conv_b': _jnp.float32, 'w_down': _jnp.float32, 'final_norm_g': _jnp.float32}
MOMENT_SCALE = {'ada_mix_w': 1.585667e-01, 'ada_mix_b': 2.722072e-01, 'norm_mix_g': 2.633411e-01, 'w_in': 1.049665e-01, 'b_gate': 3.725908e-02, 'ssd_conv_w': 6.974768e-02, 'ssd_conv_b': 6.391299e-02, 'ssd_dt_bias': 1.907958e-01, 'ssd_a_log': 1.753131e-01, 'ssd_d': 4.162823e-01, 'ssd_norm_g': 6.955905e-02, 'w_ssd_out': 6.887899e-02, 'conf_conv_w': 6.133102e-02, 'conf_conv_b': 9.508544e-02, 'conf_ln_g': 8.552311e-02, 'conf_ln_b': 6.713892e-02, 'w_conf_out': 4.182936e-02, 'sc_conv_w': 2.164283e-01, 'w_sc_out': 1.484365e-01, 'w_o': 1.695113e-01, 'ada_ffn_w': 9.987602e-02, 'ada_ffn_b': 1.729586e-01, 'norm_ffn_g': 1.535771e-01, 'w_up': 7.222420e-02, 'ffn_conv_w': 7.199047e-02, 'ffn_conv_b': 5.397615e-02, 'w_down': 1.166505e-01, 'final_norm_g': 6.516113e+01}


def _to_microbatches(a, axis):
    t = _jnp.moveaxis(a, axis, 0)
    t = t.reshape((N_MICROBATCH, t.shape[0] // N_MICROBATCH) + t.shape[1:])
    return _jnp.moveaxis(t, 1, axis + 1)


def setup_inputs(seed: int = 0) -> dict:
    inp = _fwd_setup_inputs(seed)
    key = _jax.random.fold_in(_jax.random.key(seed), 7919)
    shape, _ = _output_shape()
    out = dict(inp)
    out["loss_target"] = _jax.random.normal(_jax.random.fold_in(key, 0), shape, _jnp.float32)
    for i, name in enumerate(TWIN_WEIGHTS):
        w = inp[name].astype(_jnp.float32)
        if MOMENT_SCALE is None:
            s = _jnp.sqrt(_jnp.mean(_jnp.square(w)) + 1e-30)
        else:
            s = MOMENT_SCALE[name]
        km, kv = _jax.random.split(_jax.random.fold_in(key, i + 1))
        out[name] = w
        out["m_" + name] = s * _jax.random.normal(km, w.shape, _jnp.float32)
        out["v_" + name] = (s * s) * _jax.random.uniform(kv, w.shape, _jnp.float32, 0.5, 1.5)
    if N_MICROBATCH > 1:
        for name, axis in PER_EXAMPLE_BATCH_AXIS.items():
            out[name] = _to_microbatches(out[name], axis)
    return {'x': out['x'], 'c': out['c'], 'ada_mix_w': out['ada_mix_w'], 'ada_mix_b': out['ada_mix_b'], 'norm_mix_g': out['norm_mix_g'], 'w_in': out['w_in'], 'b_gate': out['b_gate'], 'ssd_conv_w': out['ssd_conv_w'], 'ssd_conv_b': out['ssd_conv_b'], 'ssd_dt_bias': out['ssd_dt_bias'], 'ssd_a_log': out['ssd_a_log'], 'ssd_d': out['ssd_d'], 'ssd_norm_g': out['ssd_norm_g'], 'w_ssd_out': out['w_ssd_out'], 'conf_conv_w': out['conf_conv_w'], 'conf_conv_b': out['conf_conv_b'], 'conf_ln_g': out['conf_ln_g'], 'conf_ln_b': out['conf_ln_b'], 'w_conf_out': out['w_conf_out'], 'sc_conv_w': out['sc_conv_w'], 'w_sc_out': out['w_sc_out'], 'w_o': out['w_o'], 'ada_ffn_w': out['ada_ffn_w'], 'ada_ffn_b': out['ada_ffn_b'], 'norm_ffn_g': out['norm_ffn_g'], 'w_up': out['w_up'], 'ffn_conv_w': out['ffn_conv_w'], 'ffn_conv_b': out['ffn_conv_b'], 'w_down': out['w_down'], 'final_norm_g': out['final_norm_g'], 'loss_target': out['loss_target'], 'm_ada_mix_w': out['m_ada_mix_w'], 'm_ada_mix_b': out['m_ada_mix_b'], 'm_norm_mix_g': out['m_norm_mix_g'], 'm_w_in': out['m_w_in'], 'm_b_gate': out['m_b_gate'], 'm_ssd_conv_w': out['m_ssd_conv_w'], 'm_ssd_conv_b': out['m_ssd_conv_b'], 'm_ssd_dt_bias': out['m_ssd_dt_bias'], 'm_ssd_a_log': out['m_ssd_a_log'], 'm_ssd_d': out['m_ssd_d'], 'm_ssd_norm_g': out['m_ssd_norm_g'], 'm_w_ssd_out': out['m_w_ssd_out'], 'm_conf_conv_w': out['m_conf_conv_w'], 'm_conf_conv_b': out['m_conf_conv_b'], 'm_conf_ln_g': out['m_conf_ln_g'], 'm_conf_ln_b': out['m_conf_ln_b'], 'm_w_conf_out': out['m_w_conf_out'], 'm_sc_conv_w': out['m_sc_conv_w'], 'm_w_sc_out': out['m_w_sc_out'], 'm_w_o': out['m_w_o'], 'm_ada_ffn_w': out['m_ada_ffn_w'], 'm_ada_ffn_b': out['m_ada_ffn_b'], 'm_norm_ffn_g': out['m_norm_ffn_g'], 'm_w_up': out['m_w_up'], 'm_ffn_conv_w': out['m_ffn_conv_w'], 'm_ffn_conv_b': out['m_ffn_conv_b'], 'm_w_down': out['m_w_down'], 'm_final_norm_g': out['m_final_norm_g'], 'v_ada_mix_w': out['v_ada_mix_w'], 'v_ada_mix_b': out['v_ada_mix_b'], 'v_norm_mix_g': out['v_norm_mix_g'], 'v_w_in': out['v_w_in'], 'v_b_gate': out['v_b_gate'], 'v_ssd_conv_w': out['v_ssd_conv_w'], 'v_ssd_conv_b': out['v_ssd_conv_b'], 'v_ssd_dt_bias': out['v_ssd_dt_bias'], 'v_ssd_a_log': out['v_ssd_a_log'], 'v_ssd_d': out['v_ssd_d'], 'v_ssd_norm_g': out['v_ssd_norm_g'], 'v_w_ssd_out': out['v_w_ssd_out'], 'v_conf_conv_w': out['v_conf_conv_w'], 'v_conf_conv_b': out['v_conf_conv_b'], 'v_conf_ln_g': out['v_conf_ln_g'], 'v_conf_ln_b': out['v_conf_ln_b'], 'v_w_conf_out': out['v_w_conf_out'], 'v_sc_conv_w': out['v_sc_conv_w'], 'v_w_sc_out': out['v_w_sc_out'], 'v_w_o': out['v_w_o'], 'v_ada_ffn_w': out['v_ada_ffn_w'], 'v_ada_ffn_b': out['v_ada_ffn_b'], 'v_norm_ffn_g': out['v_norm_ffn_g'], 'v_w_up': out['v_w_up'], 'v_ffn_conv_w': out['v_ffn_conv_w'], 'v_ffn_conv_b': out['v_ffn_conv_b'], 'v_w_down': out['v_w_down'], 'v_final_norm_g': out['v_final_norm_g']}


def _loss(weights, diff, rest, loss_target):
    with _jax.named_scope("forward"):
        args = {**rest, TWIN_DIFF_INPUT: diff, **{k: w.astype(_WEIGHT_DTYPES[k]) for k, w in weights.items()}}
        y = _forward(args)
    with _jax.named_scope("loss_head"):
        err = _jnp.square(y.astype(_jnp.float32) - loss_target)
        return 0.5 * _jnp.sum(_jnp.mean(err, axis=-1)) if err.ndim else 0.5 * err


def _adamw(w, g, m, v):
    m = ADAM_B1 * m + (1.0 - ADAM_B1) * g
    v = ADAM_B2 * v + (1.0 - ADAM_B2) * _jnp.square(g)
    m_hat = m / (1.0 - ADAM_B1 ** ADAM_STEP)
    v_hat = v / (1.0 - ADAM_B2 ** ADAM_STEP)
    delta = -ADAM_LR * (m_hat / (_jnp.sqrt(v_hat) + ADAM_EPS) + ADAM_WD * w)
    return delta, m, v


def reference(x, c, ada_mix_w, ada_mix_b, norm_mix_g, w_in, b_gate, ssd_conv_w, ssd_conv_b, ssd_dt_bias, ssd_a_log, ssd_d, ssd_norm_g, w_ssd_out, conf_conv_w, conf_conv_b, conf_ln_g, conf_ln_b, w_conf_out, sc_conv_w, w_sc_out, w_o, ada_ffn_w, ada_ffn_b, norm_ffn_g, w_up, ffn_conv_w, ffn_conv_b, w_down, final_norm_g, loss_target, m_ada_mix_w, m_ada_mix_b, m_norm_mix_g, m_w_in, m_b_gate, m_ssd_conv_w, m_ssd_conv_b, m_ssd_dt_bias, m_ssd_a_log, m_ssd_d, m_ssd_norm_g, m_w_ssd_out, m_conf_conv_w, m_conf_conv_b, m_conf_ln_g, m_conf_ln_b, m_w_conf_out, m_sc_conv_w, m_w_sc_out, m_w_o, m_ada_ffn_w, m_ada_ffn_b, m_norm_ffn_g, m_w_up, m_ffn_conv_w, m_ffn_conv_b, m_w_down, m_final_norm_g, v_ada_mix_w, v_ada_mix_b, v_norm_mix_g, v_w_in, v_b_gate, v_ssd_conv_w, v_ssd_conv_b, v_ssd_dt_bias, v_ssd_a_log, v_ssd_d, v_ssd_norm_g, v_w_ssd_out, v_conf_conv_w, v_conf_conv_b, v_conf_ln_g, v_conf_ln_b, v_w_conf_out, v_sc_conv_w, v_w_sc_out, v_w_o, v_ada_ffn_w, v_ada_ffn_b, v_norm_ffn_g, v_w_up, v_ffn_conv_w, v_ffn_conv_b, v_w_down, v_final_norm_g):
    given = dict(x=x, c=c, ada_mix_w=ada_mix_w, ada_mix_b=ada_mix_b, norm_mix_g=norm_mix_g, w_in=w_in, b_gate=b_gate, ssd_conv_w=ssd_conv_w, ssd_conv_b=ssd_conv_b, ssd_dt_bias=ssd_dt_bias, ssd_a_log=ssd_a_log, ssd_d=ssd_d, ssd_norm_g=ssd_norm_g, w_ssd_out=w_ssd_out, conf_conv_w=conf_conv_w, conf_conv_b=conf_conv_b, conf_ln_g=conf_ln_g, conf_ln_b=conf_ln_b, w_conf_out=w_conf_out, sc_conv_w=sc_conv_w, w_sc_out=w_sc_out, w_o=w_o, ada_ffn_w=ada_ffn_w, ada_ffn_b=ada_ffn_b, norm_ffn_g=norm_ffn_g, w_up=w_up, ffn_conv_w=ffn_conv_w, ffn_conv_b=ffn_conv_b, w_down=w_down, final_norm_g=final_norm_g, loss_target=loss_target, m_ada_mix_w=m_ada_mix_w, m_ada_mix_b=m_ada_mix_b, m_norm_mix_g=m_norm_mix_g, m_w_in=m_w_in, m_b_gate=m_b_gate, m_ssd_conv_w=m_ssd_conv_w, m_ssd_conv_b=m_ssd_conv_b, m_ssd_dt_bias=m_ssd_dt_bias, m_ssd_a_log=m_ssd_a_log, m_ssd_d=m_ssd_d, m_ssd_norm_g=m_ssd_norm_g, m_w_ssd_out=m_w_ssd_out, m_conf_conv_w=m_conf_conv_w, m_conf_conv_b=m_conf_conv_b, m_conf_ln_g=m_conf_ln_g, m_conf_ln_b=m_conf_ln_b, m_w_conf_out=m_w_conf_out, m_sc_conv_w=m_sc_conv_w, m_w_sc_out=m_w_sc_out, m_w_o=m_w_o, m_ada_ffn_w=m_ada_ffn_w, m_ada_ffn_b=m_ada_ffn_b, m_norm_ffn_g=m_norm_ffn_g, m_w_up=m_w_up, m_ffn_conv_w=m_ffn_conv_w, m_ffn_conv_b=m_ffn_conv_b, m_w_down=m_w_down, m_final_norm_g=m_final_norm_g, v_ada_mix_w=v_ada_mix_w, v_ada_mix_b=v_ada_mix_b, v_norm_mix_g=v_norm_mix_g, v_w_in=v_w_in, v_b_gate=v_b_gate, v_ssd_conv_w=v_ssd_conv_w, v_ssd_conv_b=v_ssd_conv_b, v_ssd_dt_bias=v_ssd_dt_bias, v_ssd_a_log=v_ssd_a_log, v_ssd_d=v_ssd_d, v_ssd_norm_g=v_ssd_norm_g, v_w_ssd_out=v_w_ssd_out, v_conf_conv_w=v_conf_conv_w, v_conf_conv_b=v_conf_conv_b, v_conf_ln_g=v_conf_ln_g, v_conf_ln_b=v_conf_ln_b, v_w_conf_out=v_w_conf_out, v_sc_conv_w=v_sc_conv_w, v_w_sc_out=v_w_sc_out, v_w_o=v_w_o, v_ada_ffn_w=v_ada_ffn_w, v_ada_ffn_b=v_ada_ffn_b, v_norm_ffn_g=v_norm_ffn_g, v_w_up=v_w_up, v_ffn_conv_w=v_ffn_conv_w, v_ffn_conv_b=v_ffn_conv_b, v_w_down=v_w_down, v_final_norm_g=v_final_norm_g)
    weights = {n: given[n] for n in TWIN_WEIGHTS}
    shared = {n: given[n] for n in SHARED_INPUTS}
    per_example = {n: given[n] for n in ['x', 'c']}
    grad_fn = _jax.value_and_grad(_loss, argnums=(0, 1))

    def one_microbatch(ex, loss_target):
        ex = dict(ex)
        diff = ex.pop(TWIN_DIFF_INPUT)
        return grad_fn(weights, diff, {**shared, **ex}, loss_target)

    if N_MICROBATCH == 1:
        loss, (grad_w, grad_x) = one_microbatch(per_example, given["loss_target"])
    else:
        def body(carry, xs):
            loss_sum, grad_sum = carry
            l_k, (gw_k, gx_k) = one_microbatch(xs[0], xs[1])
            with _jax.named_scope("update"):
                return (loss_sum + l_k, _jax.tree.map(_jnp.add, grad_sum, gw_k)), gx_k

        init = (_jnp.zeros((), _jnp.float32), _jax.tree.map(_jnp.zeros_like, weights))
        (loss, grad_w), grad_x = _jax.lax.scan(body, init, (per_example, given["loss_target"]))
    with _jax.named_scope("update"):
        delta_w, new_m, new_v = {}, {}, {}
        for n in TWIN_WEIGHTS:
            delta_w[n], new_m[n], new_v[n] = _adamw(weights[n], grad_w[n], given["m_" + n], given["v_" + n])
    return (loss, grad_x, *[grad_w[n] for n in TWIN_WEIGHTS], *[delta_w[n] for n in TWIN_WEIGHTS],
            *[new_m[n] for n in TWIN_WEIGHTS], *[new_v[n] for n in TWIN_WEIGHTS])
```

```python
import functools

import jax
import jax.numpy as jnp
from jax import lax
from jax.experimental import pallas as pl
from jax.experimental.pallas import tpu as pltpu

F32 = jnp.float32
BF = jnp.bfloat16
S = jax.ShapeDtypeStruct

D = 1024
HEADS = 16
HEAD_DIM = 64
INNER = HEADS * HEAD_DIM
GROUPS = 2
NSTATE = 64
Q = 128
SSD_K = 4
XBC = INNER + 2 * GROUPS * NSTATE
CONF_W = 512
CONF_K = 31
SC_W = 512
SC_K = 3
DFF = 2816
FFN_K = 3
EPS = 1e-6
DEPTH = 2
R_Z, R_XBC, R_DT, R_CONF, R_SC, R_GATES, N_IN = 0, 1024, 2304, 2320, 3344, 4880, 7952
P_GATES, P_SC, P_XBC, P_Z, P_CONF, PW = 0, 3072, 4608, 6144, 7168, 8192
XBC_PAD = 1536
DT_PAD = 128
P_DT = P_XBC + XBC
N_CHIPS = 4
N_DEV = 8

ADAM_LR, ADAM_B1, ADAM_B2, ADAM_EPS, ADAM_WD, ADAM_STEP = 0.001, 0.9, 0.999, 1e-08, 0.01, 10

VMEM_LIMIT_V7X = 56 * 1024 * 1024
HIGHEST = lax.Precision.HIGHEST


def _cparams(*sem):
    return pltpu.CompilerParams(dimension_semantics=sem, vmem_limit_bytes=VMEM_LIMIT_V7X)


def _full(shape):
    n = len(shape)
    return pl.BlockSpec(shape, lambda *_: (0,) * n)


def _rows(tm, w, cb=0):
    return pl.BlockSpec((tm, w), lambda i: (i, cb))


def _prev_rows(tm, halo, w, cb=0):
    r = tm // halo
    return pl.BlockSpec((halo, w), lambda i: (jnp.maximum(i * r - 1, 0), cb))


def _next_rows(tm, halo, w, nrows, cb=0):
    r = tm // halo
    last = nrows // halo - 1
    return pl.BlockSpec((halo, w), lambda i: (jnp.minimum((i + 1) * r, last), cb))


def _sigmoid(v):
    return 1.0 / (1.0 + jnp.exp(-v))


def _softplus(v):
    return jnp.maximum(v, 0.0) + jnp.log(1.0 + jnp.exp(-jnp.abs(v)))


def _colsum(v):
    return jnp.sum(v, axis=0, keepdims=True)


def _tile(n, want):
    t = min(n, want)
    assert n % t == 0, (n, want)
    return t


NN = (((1,), (0,)), ((), ()))
NT = (((1,), (1,)), ((), ()))
TN = (((0,), (0,)), ((), ()))


def _dot(a, b, dims=NN):
    return lax.dot_general(a, b, dims, preferred_element_type=F32)


def _mm(a, b, *, dims, grid, a_spec, b_spec, o_spec, out_shape, acc_shape, name):
    nk = grid[2]

    def body(a_ref, b_ref, o_ref, acc_ref):
        k = pl.program_id(2)
        part = _dot(a_ref[...], b_ref[...], dims)
        if nk == 1:
            o_ref[...] = part.astype(o_ref.dtype)
        else:
            @pl.when(k == 0)
            def _():
                acc_ref[...] = part

            @pl.when(k > 0)
            def _():
                acc_ref[...] += part

            @pl.when(k == nk - 1)
            def _():
                o_ref[...] = acc_ref[...].astype(o_ref.dtype)

    return pl.pallas_call(
        body, grid=grid, in_specs=[a_spec, b_spec], out_specs=o_spec, out_shape=out_shape,
        scratch_shapes=[pltpu.VMEM(acc_shape if nk > 1 else (8, 128), F32)],
        compiler_params=_cparams("parallel", "parallel", "arbitrary"), name=name)(a, b)


def _mm_nn(a, b, *, out_dtype, tm, tn, name):
    m, k = a.shape
    n = b.shape[1]
    tm, tn = _tile(m, tm), _tile(n, tn)
    return _mm(a, b, dims=NN, grid=(m // tm, n // tn, 1),
               a_spec=pl.BlockSpec((tm, k), lambda i, j, kk: (i, 0)),
               b_spec=pl.BlockSpec((k, tn), lambda i, j, kk: (0, j)),
               o_spec=pl.BlockSpec((tm, tn), lambda i, j, kk: (i, j)),
               out_shape=S((m, n), out_dtype), acc_shape=(tm, tn), name=name)


def _mm_nt(a, b, *, out_dtype, tm, tk, name):
    m, kc = a.shape
    n = b.shape[0]
    tm, tk = _tile(m, tm), _tile(kc, tk)
    return _mm(a, b, dims=NT, grid=(m // tm, 1, kc // tk),
               a_spec=pl.BlockSpec((tm, tk), lambda i, j, kk: (i, kk)),
               b_spec=pl.BlockSpec((n, tk), lambda i, j, kk: (0, kk)),
               o_spec=pl.BlockSpec((tm, n), lambda i, j, kk: (i, 0)),
               out_shape=S((m, n), out_dtype), acc_shape=(tm, n), name=name)


def _mm_tn(a, b, *, tn, tk, name):
    kc, m = a.shape
    n = b.shape[1]
    tn, tk = _tile(n, tn), _tile(kc, tk)
    return _mm(a, b, dims=TN, grid=(1, n // tn, kc // tk),
               a_spec=pl.BlockSpec((tk, m), lambda i, j, kk: (kk, 0)),
               b_spec=pl.BlockSpec((tk, tn), lambda i, j, kk: (kk, j)),
               o_spec=pl.BlockSpec((m, tn), lambda i, j, kk: (0, j)),
               out_shape=S((m, n), F32), acc_shape=(m, tn), name=name)


def _mm_resid(a, b, x, mod, *, tm, name):
    m, k = a.shape
    n = b.shape[1]
    tm = _tile(m, tm)

    def body(a_ref, b_ref, x_ref, mod_ref, o_ref, xn_ref):
        o = _dot(a_ref[...], b_ref[...])
        o_ref[...] = o.astype(o_ref.dtype)
        xn_ref[...] = x_ref[...] + mod_ref[2:3, :] * o

    return pl.pallas_call(
        body, grid=(m // tm,),
        in_specs=[_rows(tm, k), _full((k, n)), _rows(tm, n), _full((3, n))],
        out_specs=[_rows(tm, n), _rows(tm, n)],
        out_shape=[S((m, n), BF), S((m, n), F32)],
        compiler_params=_cparams("parallel"), name=name)(a, b, x, mod)


def _modnorm_fwd(x, gain, mod, *, name):
    n = x.shape[0]
    tm = _tile(n, 512)

    def body(x_ref, g_ref, mod_ref, h_ref):
        xv = x_ref[...]
        r = lax.rsqrt(jnp.mean(xv * xv, axis=-1, keepdims=True) + EPS)
        y = xv * r * g_ref[...]
        h_ref[...] = (y * (1.0 + mod_ref[1:2, :]) + mod_ref[0:1, :]).astype(h_ref.dtype)

    return pl.pallas_call(
        body, grid=(n // tm,), in_specs=[_rows(tm, D), _full((1, D)), _full((3, D))],
        out_specs=_rows(tm, D), out_shape=S((n, D), BF), compiler_params=_cparams("parallel"), name=name)(x, gain, mod)


def _modnorm_bwd(dh, x, dres, gain, mod, *, name):
    n = x.shape[0]
    tm = _tile(n, 512)

    def body(dh_ref, x_ref, dres_ref, g_ref, mod_ref, dx_ref, dg_ref, dsh_ref, dsc_ref):
        i = pl.program_id(0)
        xv = x_ref[...]
        r = lax.rsqrt(jnp.mean(xv * xv, axis=-1, keepdims=True) + EPS)
        xh = xv * r
        dhv = dh_ref[...]
        g = g_ref[...]
        dy = dhv * (1.0 + mod_ref[1:2, :])
        dxh = dy * g
        dx = r * (dxh - xh * jnp.mean(dxh * xh, axis=-1, keepdims=True))
        dx_ref[...] = dres_ref[...] + dx

        @pl.when(i == 0)
        def _():
            dg_ref[...] = jnp.zeros_like(dg_ref)
            dsh_ref[...] = jnp.zeros_like(dsh_ref)
            dsc_ref[...] = jnp.zeros_like(dsc_ref)

        dg_ref[...] += _colsum(dy * xh)
        dsh_ref[...] += _colsum(dhv)
        dsc_ref[...] += _colsum(dhv * xh * g)

    vec = S((1, D), F32)
    return pl.pallas_call(
        body, grid=(n // tm,),
        in_specs=[_rows(tm, D), _rows(tm, D), _rows(tm, D), _full((1, D)), _full((3, D))],
        out_specs=[_rows(tm, D), _full((1, D)), _full((1, D)), _full((1, D))],
        out_shape=[S((n, D), F32), vec, vec, vec],
        compiler_params=_cparams("arbitrary"), name=name)(dh, x, dres, gain, mod)


def _final_loss(x, gain, target, *, name):
    n = x.shape[0]
    tm = _tile(n, 512)

    def body(x_ref, g_ref, t_ref, dx_ref, loss_ref, dg_ref):
        i = pl.program_id(0)
        xv = x_ref[...]
        g = g_ref[...]
        r = lax.rsqrt(jnp.mean(xv * xv, axis=-1, keepdims=True) + EPS)
        xh = xv * r
        err = xh * g - t_ref[...]
        dy = err * (1.0 / D)
        dxh = dy * g
        dx_ref[...] = r * (dxh - xh * jnp.mean(dxh * xh, axis=-1, keepdims=True))

        @pl.when(i == 0)
        def _():
            loss_ref[...] = jnp.zeros_like(loss_ref)
            dg_ref[...] = jnp.zeros_like(dg_ref)

        part = _colsum(jnp.sum(err * err, axis=-1, keepdims=True)) * (0.5 / D)
        loss_ref[...] += jnp.broadcast_to(part, loss_ref.shape)
        dg_ref[...] += _colsum(dy * xh)

    return pl.pallas_call(
        body, grid=(n // tm,),
        in_specs=[_rows(tm, D), _full((1, D)), _rows(tm, D)],
        out_specs=[_rows(tm, D), _full((1, 128)), _full((1, D))],
        out_shape=[S((n, D), F32), S((1, 128), F32), S((1, D), F32)],
        compiler_params=_cparams("arbitrary"), name=name)(x, gain, target)


def _gate_bwd(dx, o, mod, *, name):
    n = dx.shape[0]
    tm = _tile(n, 512)

    def body(dx_ref, o_ref, mod_ref, do_ref, dgt_ref):
        i = pl.program_id(0)
        dxv = dx_ref[...]
        do_ref[...] = (dxv * mod_ref[2:3, :]).astype(do_ref.dtype)

        @pl.when(i == 0)
        def _():
            dgt_ref[...] = jnp.zeros_like(dgt_ref)

        dgt_ref[...] += _colsum(dxv * o_ref[...].astype(F32))

    return pl.pallas_call(
        body, grid=(n // tm,), in_specs=[_rows(tm, D), _rows(tm, D), _full((3, D))],
        out_specs=[_rows(tm, D), _full((1, D))], out_shape=[S((n, D), BF), S((1, D), F32)],
        compiler_params=_cparams("arbitrary"), name=name)(dx, o, mod)


def _conv(buf, w_ref, taps, start, rows, ch):
    acc = None
    for k in range(taps):
        term = buf[pl.ds(start - (taps - 1) + k, rows), 0:ch] * w_ref[k:k + 1, :]
        acc = term if acc is None else acc + term
    return acc


def _conv_t(buf, w_ref, taps, start, rows, ch):
    acc = None
    for k in range(taps):
        term = buf[pl.ds(start + (taps - 1) - k, rows), 0:ch] * w_ref[k:k + 1, :]
        acc = term if acc is None else acc + term
    return acc


def _conv_dw(dw_ref, dy, xbuf, taps, xstart, rows, ch):
    for k in range(taps):
        dw_ref[k:k + 1, :] += _colsum(dy * xbuf[pl.ds(xstart - (taps - 1) + k, rows), 0:ch])


HALO = 16
CONF_HALO = 32


def _ssd_pre_fwd(proj, w, b, *, name):
    n = proj.shape[0]
    tm = _tile(n, 512)
    cb = P_XBC // XBC_PAD

    def body(prev_ref, cur_ref, w_ref, b_ref, o_ref, buf):
        i = pl.program_id(0)
        buf[0:HALO, :] = jnp.where(i == 0, 0.0, prev_ref[:, 0:XBC].astype(F32))
        buf[HALO:HALO + tm, :] = cur_ref[:, 0:XBC].astype(F32)
        c = _conv(buf, w_ref, SSD_K, HALO, tm, XBC) + b_ref[...]
        o_ref[...] = (c * _sigmoid(c)).astype(o_ref.dtype)

    return pl.pallas_call(
        body, grid=(n // tm,),
        in_specs=[_prev_rows(tm, HALO, XBC_PAD, cb), _rows(tm, XBC_PAD, cb), _full((SSD_K, XBC)), _full((1, XBC))],
        out_specs=_rows(tm, XBC), out_shape=S((n, XBC), BF),
        scratch_shapes=[pltpu.VMEM((HALO + tm, XBC), F32)],
        compiler_params=_cparams("parallel"), name=name)(proj, proj, w, b)


def _ssd_pre_bwd(proj, dact, ddt, dproj, w, b, *, name):
    n = proj.shape[0]
    tm = _tile(n, 512)
    nt = n // tm
    cb = P_XBC // XBC_PAD

    def body(xp_ref, xc_ref, xn_ref, dc_ref, dn_ref, ddt_ref, w_ref, b_ref, dproj_in, o_ref, dw_ref, db_ref, xbuf, dbuf):
        del dproj_in
        i = pl.program_id(0)
        xbuf[0:HALO, :] = jnp.where(i == 0, 0.0, xp_ref[:, 0:XBC].astype(F32))
        xbuf[HALO:HALO + tm, :] = xc_ref[:, 0:XBC].astype(F32)
        xbuf[HALO + tm:HALO + tm + HALO, :] = xn_ref[:, 0:XBC].astype(F32)
        c = _conv(xbuf, w_ref, SSD_K, HALO, tm + HALO, XBC) + b_ref[...]
        s = _sigmoid(c)
        dsilu = s * (1.0 + c * (1.0 - s))
        dbuf[0:tm, :] = dc_ref[...].astype(F32) * dsilu[0:tm]
        dbuf[tm:tm + HALO, :] = jnp.where(i == nt - 1, 0.0, dn_ref[...].astype(F32) * dsilu[tm:tm + HALO])
        dx = _conv_t(dbuf, w_ref, SSD_K, 0, tm, XBC)
        o_ref[:, 0:XBC] = dx.astype(o_ref.dtype)
        o_ref[:, XBC:XBC + DT_PAD] = ddt_ref[...]
        o_ref[:, XBC + DT_PAD:XBC_PAD] = jnp.zeros((tm, XBC_PAD - XBC - DT_PAD), o_ref.dtype)

        @pl.when(i == 0)
        def _():
            dw_ref[...] = jnp.zeros_like(dw_ref)
            db_ref[...] = jnp.zeros_like(db_ref)

        dcur = dbuf[0:tm, :]
        db_ref[...] += _colsum(dcur)
        _conv_dw(dw_ref, dcur, xbuf, SSD_K, HALO, tm, XBC)

    return pl.pallas_call(
        body, grid=(nt,),
        in_specs=[_prev_rows(tm, HALO, XBC_PAD, cb), _rows(tm, XBC_PAD, cb), _next_rows(tm, HALO, XBC_PAD, n, cb),
                  _rows(tm, XBC), _next_rows(tm, HALO, XBC, n), _rows(tm, DT_PAD),
                  _full((SSD_K, XBC)), _full((1, XBC)), pl.BlockSpec(memory_space=pl.ANY)],
        out_specs=[_rows(tm, XBC_PAD, cb), _full((SSD_K, XBC)), _full((1, XBC))],
        out_shape=[S(dproj.shape, dproj.dtype), S((SSD_K, XBC), F32), S((1, XBC), F32)],
        scratch_shapes=[pltpu.VMEM((HALO + tm + HALO, XBC), F32), pltpu.VMEM((tm + HALO, XBC), F32)],
        input_output_aliases={8: 0},
        compiler_params=_cparams("arbitrary"), name=name)(proj, proj, proj, dact, dact, ddt, w, b, dproj)


def _sc_fwd(proj, w, *, name):
    n = proj.shape[0]
    tm = _tile(n, 512)
    cb = P_SC // (3 * SC_W)

    def body(prev_ref, cur_ref, w_ref, o_ref, buf):
        i = pl.program_id(0)
        pv = prev_ref[...].astype(F32)
        cv = cur_ref[...].astype(F32)
        buf[0:HALO, :] = jnp.where(i == 0, 0.0, pv[:, SC_W:2 * SC_W] * pv[:, 2 * SC_W:])
        buf[HALO:HALO + tm, :] = cv[:, SC_W:2 * SC_W] * cv[:, 2 * SC_W:]
        q = _conv(buf, w_ref, SC_K, HALO, tm, SC_W)
        o_ref[...] = (cv[:, 0:SC_W] * q).astype(o_ref.dtype)

    return pl.pallas_call(
        body, grid=(n // tm,),
        in_specs=[_prev_rows(tm, HALO, 3 * SC_W, cb), _rows(tm, 3 * SC_W, cb), _full((SC_K, SC_W))],
        out_specs=_rows(tm, SC_W), out_shape=S((n, SC_W), BF),
        scratch_shapes=[pltpu.VMEM((HALO + tm, SC_W), F32)],
        compiler_params=_cparams("parallel"), name=name)(proj, proj, w)


def _sc_bwd(proj, da, dproj, w, *, name):
    n = proj.shape[0]
    tm = _tile(n, 512)
    nt = n // tm
    cb = P_SC // (3 * SC_W)

    def body(xp_ref, xc_ref, xn_ref, dc_ref, dn_ref, w_ref, dproj_in, o_ref, dw_ref, pbuf, dbuf):
        del dproj_in
        i = pl.program_id(0)
        pv = xp_ref[...].astype(F32)
        cv = xc_ref[...].astype(F32)
        nv = xn_ref[...].astype(F32)
        gb, gc, xv = cv[:, 0:SC_W], cv[:, SC_W:2 * SC_W], cv[:, 2 * SC_W:]
        pbuf[0:HALO, :] = jnp.where(i == 0, 0.0, pv[:, SC_W:2 * SC_W] * pv[:, 2 * SC_W:])
        pbuf[HALO:HALO + tm, :] = gc * xv
        q = _conv(pbuf, w_ref, SC_K, HALO, tm, SC_W)
        dav = dc_ref[...].astype(F32)
        dbuf[0:tm, :] = dav * gb
        dbuf[tm:tm + HALO, :] = jnp.where(i == nt - 1, 0.0, dn_ref[...].astype(F32) * nv[:, 0:SC_W])
        dp = _conv_t(dbuf, w_ref, SC_K, 0, tm, SC_W)
        o_ref[:, 0:SC_W] = (dav * q).astype(o_ref.dtype)
        o_ref[:, SC_W:2 * SC_W] = (dp * xv).astype(o_ref.dtype)
        o_ref[:, 2 * SC_W:] = (dp * gc).astype(o_ref.dtype)

        @pl.when(i == 0)
        def _():
            dw_ref[...] = jnp.zeros_like(dw_ref)

        _conv_dw(dw_ref, dbuf[0:tm, :], pbuf, SC_K, HALO, tm, SC_W)

    return pl.pallas_call(
        body, grid=(nt,),
        in_specs=[_prev_rows(tm, HALO, 3 * SC_W, cb), _rows(tm, 3 * SC_W, cb), _next_rows(tm, HALO, 3 * SC_W, n, cb),
                  _rows(tm, SC_W), _next_rows(tm, HALO, SC_W, n), _full((SC_K, SC_W)),
                  pl.BlockSpec(memory_space=pl.ANY)],
        out_specs=[_rows(tm, 3 * SC_W, cb), _full((SC_K, SC_W))],
        out_shape=[S(dproj.shape, dproj.dtype), S((SC_K, SC_W), F32)],
        scratch_shapes=[pltpu.VMEM((HALO + tm, SC_W), F32), pltpu.VMEM((tm + HALO, SC_W), F32)],
        input_output_aliases={6: 0},
        compiler_params=_cparams("arbitrary"), name=name)(proj, proj, proj, da, da, w, dproj)


def _conf_fwd(proj, w, b, ln_g, ln_b, *, name):
    n = proj.shape[0]
    tm = _tile(n, 512)
    cb = P_CONF // (2 * CONF_W)
    h = CONF_HALO

    def body(prev_ref, cur_ref, w_ref, b_ref, g_ref, be_ref, a_ref, uc_ref, buf):
        i = pl.program_id(0)
        pv = prev_ref[...].astype(F32)
        cv = cur_ref[...].astype(F32)
        buf[0:h, :] = jnp.where(i == 0, 0.0, pv[:, 0:CONF_W] * _sigmoid(pv[:, CONF_W:]))
        buf[h:h + tm, :] = cv[:, 0:CONF_W] * _sigmoid(cv[:, CONF_W:])
        uc = _conv(buf, w_ref, CONF_K, h, tm, CONF_W) + b_ref[...]
        uc_ref[...] = uc.astype(uc_ref.dtype)
        mu = jnp.mean(uc, axis=-1, keepdims=True)
        xc = uc - mu
        v = xc * lax.rsqrt(jnp.mean(xc * xc, axis=-1, keepdims=True) + EPS) * g_ref[...] + be_ref[...]
        a_ref[...] = (v * _sigmoid(v)).astype(a_ref.dtype)

    vec = _full((1, CONF_W))
    return pl.pallas_call(
        body, grid=(n // tm,),
        in_specs=[_prev_rows(tm, h, 2 * CONF_W, cb), _rows(tm, 2 * CONF_W, cb), _full((CONF_K, CONF_W)), vec, vec, vec],
        out_specs=[_rows(tm, CONF_W), _rows(tm, CONF_W)],
        out_shape=[S((n, CONF_W), BF), S((n, CONF_W), BF)],
        scratch_shapes=[pltpu.VMEM((h + tm, CONF_W), F32)],
        compiler_params=_cparams("parallel"), name=name)(proj, proj, w, b, ln_g, ln_b)


def _conf_bwd(proj, uc, da, dproj, w, ln_g, ln_b, *, name):
    n = proj.shape[0]
    tm = _tile(n, 512)
    nt = n // tm
    cb = P_CONF // (2 * CONF_W)
    h = CONF_HALO

    def body(xp_ref, xc_ref, ucc_ref, ucn_ref, dac_ref, dan_ref, w_ref, g_ref, be_ref, dproj_in,
             o_ref, dw_ref, db_ref, dg_ref, dbe_ref, ubuf, dbuf):
        del dproj_in
        i = pl.program_id(0)
        pv = xp_ref[...].astype(F32)
        cv = xc_ref[...].astype(F32)
        val, gt = cv[:, 0:CONF_W], cv[:, CONF_W:]
        sg = _sigmoid(gt)
        ubuf[0:h, :] = jnp.where(i == 0, 0.0, pv[:, 0:CONF_W] * _sigmoid(pv[:, CONF_W:]))
        ubuf[h:h + tm, :] = val * sg

        def ln_silu_bwd(ucv, dav):
            mu = jnp.mean(ucv, axis=-1, keepdims=True)
            xc = ucv - mu
            r = lax.rsqrt(jnp.mean(xc * xc, axis=-1, keepdims=True) + EPS)
            xh = xc * r
            v = xh * g_ref[...] + be_ref[...]
            s = _sigmoid(v)
            dv = dav * (s * (1.0 + v * (1.0 - s)))
            dxh = dv * g_ref[...]
            duc = r * (dxh - jnp.mean(dxh, axis=-1, keepdims=True) - xh * jnp.mean(dxh * xh, axis=-1, keepdims=True))
            return duc, dv, xh

        duc, dv, xh = ln_silu_bwd(ucc_ref[...].astype(F32), dac_ref[...].astype(F32))
        dbuf[0:tm, :] = duc
        ducn, _, _ = ln_silu_bwd(ucn_ref[...].astype(F32), dan_ref[...].astype(F32))
        dbuf[tm:tm + h, :] = jnp.where(i == nt - 1, 0.0, ducn)
        du = _conv_t(dbuf, w_ref, CONF_K, 0, tm, CONF_W)
        o_ref[:, 0:CONF_W] = (du * sg).astype(o_ref.dtype)
        o_ref[:, CONF_W:] = (du * val * sg * (1.0 - sg)).astype(o_ref.dtype)

        @pl.when(i == 0)
        def _():
            dw_ref[...] = jnp.zeros_like(dw_ref)
            db_ref[...] = jnp.zeros_like(db_ref)
            dg_ref[...] = jnp.zeros_like(dg_ref)
            dbe_ref[...] = jnp.zeros_like(dbe_ref)

        dg_ref[...] += _colsum(dv * xh)
        dbe_ref[...] += _colsum(dv)
        db_ref[...] += _colsum(duc)
        _conv_dw(dw_ref, duc, ubuf, CONF_K, h, tm, CONF_W)

    vec = _full((1, CONF_W))
    vshape = S((1, CONF_W), F32)
    return pl.pallas_call(
        body, grid=(nt,),
        in_specs=[_prev_rows(tm, h, 2 * CONF_W, cb), _rows(tm, 2 * CONF_W, cb),
                  _rows(tm, CONF_W), _next_rows(tm, h, CONF_W, n), _rows(tm, CONF_W), _next_rows(tm, h, CONF_W, n),
                  _full((CONF_K, CONF_W)), vec, vec, pl.BlockSpec(memory_space=pl.ANY)],
        out_specs=[_rows(tm, 2 * CONF_W, cb), _full((CONF_K, CONF_W)), vec, vec, vec],
        out_shape=[S(dproj.shape, dproj.dtype), S((CONF_K, CONF_W), F32), vshape, vshape, vshape],
        scratch_shapes=[pltpu.VMEM((h + tm, CONF_W), F32), pltpu.VMEM((tm + h, CONF_W), F32)],
        input_output_aliases={9: 0},
        compiler_params=_cparams("arbitrary"), name=name)(proj, proj, uc, uc, da, da, w, ln_g, ln_b, dproj)


def _ffn_act_fwd(up, w, b, *, name):
    n = up.shape[0]
    tm = _tile(n, 256)
    c2 = 2 * DFF

    def body(prev_ref, cur_ref, w_ref, b_ref, o_ref, buf):
        i = pl.program_id(0)
        buf[0:HALO, :] = jnp.where(i == 0, 0.0, prev_ref[...].astype(F32))
        buf[HALO:HALO + tm, :] = cur_ref[...].astype(F32)
        u = _conv(buf, w_ref, FFN_K, HALO, tm, c2) + b_ref[...]
        gate, val = u[:, 0:DFF], u[:, DFF:]
        o_ref[...] = (gate * _sigmoid(gate) * val).astype(o_ref.dtype)

    return pl.pallas_call(
        body, grid=(n // tm,),
        in_specs=[_prev_rows(tm, HALO, c2), _rows(tm, c2), _full((FFN_K, c2)), _full((1, c2))],
        out_specs=_rows(tm, DFF), out_shape=S((n, DFF), BF),
        scratch_shapes=[pltpu.VMEM((HALO + tm, c2), F32)],
        compiler_params=_cparams("parallel"), name=name)(up, up, w, b)


def _ffn_act_bwd(up, dact, w, b, *, name):
    n = up.shape[0]
    tm = _tile(n, 256)
    nt = n // tm
    c2 = 2 * DFF

    def body(xp_ref, xc_ref, xn_ref, dc_ref, dn_ref, w_ref, b_ref, o_ref, dw_ref, db_ref, xbuf, dbuf):
        i = pl.program_id(0)
        xbuf[0:HALO, :] = jnp.where(i == 0, 0.0, xp_ref[...].astype(F32))
        xbuf[HALO:HALO + tm, :] = xc_ref[...].astype(F32)
        xbuf[HALO + tm:HALO + tm + HALO, :] = xn_ref[...].astype(F32)
        u = _conv(xbuf, w_ref, FFN_K, HALO, tm + HALO, c2) + b_ref[...]
        gate, val = u[:, 0:DFF], u[:, DFF:]
        s = _sigmoid(gate)
        tail = jnp.where(i == nt - 1, 0.0, dn_ref[...].astype(F32))
        dgate_c = dc_ref[...].astype(F32)
        dbuf[0:tm, 0:DFF] = dgate_c * val[0:tm] * (s[0:tm] * (1.0 + gate[0:tm] * (1.0 - s[0:tm])))
        dbuf[0:tm, DFF:] = dgate_c * gate[0:tm] * s[0:tm]
        dbuf[tm:tm + HALO, 0:DFF] = tail * val[tm:] * (s[tm:] * (1.0 + gate[tm:] * (1.0 - s[tm:])))
        dbuf[tm:tm + HALO, DFF:] = tail * gate[tm:] * s[tm:]
        o_ref[...] = _conv_t(dbuf, w_ref, FFN_K, 0, tm, c2).astype(o_ref.dtype)

        @pl.when(i == 0)
        def _():
            dw_ref[...] = jnp.zeros_like(dw_ref)
            db_ref[...] = jnp.zeros_like(db_ref)

        du = dbuf[0:tm, :]
        db_ref[...] += _colsum(du)
        _conv_dw(dw_ref, du, xbuf, FFN_K, HALO, tm, c2)

    return pl.pallas_call(
        body, grid=(nt,),
        in_specs=[_prev_rows(tm, HALO, c2), _rows(tm, c2), _next_rows(tm, HALO, c2, n),
                  _rows(tm, DFF), _next_rows(tm, HALO, DFF, n), _full((FFN_K, c2)), _full((1, c2))],
        out_specs=[_rows(tm, c2), _full((FFN_K, c2)), _full((1, c2))],
        out_shape=[S((n, c2), BF), S((FFN_K, c2), F32), S((1, c2), F32)],
        scratch_shapes=[pltpu.VMEM((HALO + tm + HALO, c2), F32), pltpu.VMEM((tm + HALO, c2), F32)],
        compiler_params=_cparams("arbitrary"), name=name)(up, up, up, dact, dact, w, b)


def _head_consts():
    lane = jnp.arange(INNER) // HEAD_DIM
    rep = (jnp.arange(128)[:, None] == lane[None, :]).astype(BF)
    return rep, rep.T


def _split_dot(v, m):
    hi = v.astype(BF)
    lo = (v - hi.astype(F32)).astype(BF)
    return _dot(hi, m) + _dot(lo, m)


def _chunk_decay_terms(dt_raw, dtb, alog, rep):
    row = lax.broadcasted_iota(jnp.int32, (Q, Q), 0)
    col = lax.broadcasted_iota(jnp.int32, (Q, Q), 1)
    lower = row >= col
    upper = col >= row
    dt = _softplus(dt_raw + dtb)
    a = -jnp.exp(alog)
    adt = dt * a
    acum = lax.dot_general(lower.astype(F32), adt, NN, precision=HIGHEST, preferred_element_type=F32)
    acum_t = lax.dot_general(adt, upper.astype(F32), TN, precision=HIGHEST, preferred_element_type=F32)
    alast = acum[Q - 1:Q, :]
    e = jnp.exp(acum)
    f = jnp.exp(alast - acum)
    ex = _split_dot(jnp.concatenate([dt, e, f, jnp.broadcast_to(jnp.exp(alast), (8, 128))], axis=0), rep)
    return dict(lower=lower, upper=upper, dt=dt, a=a, acum=acum, acum_t=acum_t, alast=alast,
                dt_x=ex[0:Q], e_x=ex[Q:2 * Q], f_x=ex[2 * Q:3 * Q], cd_x=ex[3 * Q:3 * Q + 1])


def _block_diag2(v, lo):
    return jnp.concatenate([jnp.where(lo, v, 0.0), jnp.where(lo, 0.0, v)], axis=0).astype(BF)


def _ssd_fwd(xbc_act, proj, dt_bias, a_log, d_x, norm_g, *, name):
    n = xbc_act.shape[0]
    nc = n // Q
    rep, _ = _head_consts()

    def body(xs_ref, bc_ref, dt_ref, z_ref, dtb_ref, alog_ref, dx_ref, ng_ref, rep_ref, y_ref, yn_ref, hp_ref,
             h_scr, y_scr):
        i = pl.program_id(0)

        @pl.when(i == 0)
        def _():
            h_scr[...] = jnp.zeros_like(h_scr)

        hp_ref[...] = h_scr[...]
        t = _chunk_decay_terms(dt_ref[...].astype(F32), dtb_ref[...], alog_ref[...], rep_ref[...])
        xs = xs_ref[...].astype(F32)
        xt = xs * t["dt_x"]
        lo = lax.broadcasted_iota(jnp.int32, (Q, 128), 1) < HEAD_DIM
        gw = INNER // GROUPS
        for g in range(GROUPS):
            bm = bc_ref[:, g * NSTATE:(g + 1) * NSTATE]
            cm = bc_ref[:, GROUPS * NSTATE + g * NSTATE:GROUPS * NSTATE + (g + 1) * NSTATE]
            cb = _dot(cm, bm, NT)
            hg = h_scr[:, g * gw:(g + 1) * gw]
            yoff = _dot(cm, hg.astype(BF))
            for jj in range(gw // 128):
                p = g * (gw // 128) + jj
                sl = slice(p * 128, (p + 1) * 128)
                ws = []
                for hd in (2 * p, 2 * p + 1):
                    seg = t["acum"][:, hd:hd + 1] - t["acum_t"][hd:hd + 1, :]
                    ws.append((cb * jnp.exp(jnp.where(t["lower"], seg, -jnp.inf))).astype(BF))
                ydiag = _dot(jnp.concatenate(ws, axis=1), _block_diag2(xt[:, sl], lo))
                y_scr[:, sl] = ydiag + yoff[:, jj * 128:(jj + 1) * 128] * t["e_x"][:, sl] + dx_ref[:, sl] * xs[:, sl]
            xf = (xt[:, g * gw:(g + 1) * gw] * t["f_x"][:, g * gw:(g + 1) * gw]).astype(BF)
            h_scr[:, g * gw:(g + 1) * gw] = hg * t["cd_x"][:, g * gw:(g + 1) * gw] + _dot(bm, xf, TN)
        y = y_scr[...]
        y_ref[...] = y.astype(y_ref.dtype)
        z = z_ref[...].astype(F32)
        v = y * z * _sigmoid(z)
        for g in range(GROUPS):
            vg = v[:, g * gw:(g + 1) * gw]
            r = lax.rsqrt(jnp.mean(vg * vg, axis=-1, keepdims=True) + EPS)
            yn_ref[:, g * gw:(g + 1) * gw] = (vg * r * ng_ref[:, g * gw:(g + 1) * gw]).astype(yn_ref.dtype)

    vec = _full((1, INNER))
    hv = _full((1, 128))
    return pl.pallas_call(
        body, grid=(nc,),
        in_specs=[_rows(Q, INNER, 0), _rows(Q, 2 * GROUPS * NSTATE, INNER // (2 * GROUPS * NSTATE)),
                  _rows(Q, DT_PAD, P_DT // DT_PAD), _rows(Q, INNER, P_Z // INNER),
                  hv, hv, vec, vec, _full((128, INNER))],
        out_specs=[_rows(Q, INNER), _rows(Q, INNER), pl.BlockSpec((None, NSTATE, INNER), lambda i: (i, 0, 0))],
        out_shape=[S((n, INNER), BF), S((n, INNER), BF), S((nc, NSTATE, INNER), F32)],
        scratch_shapes=[pltpu.VMEM((NSTATE, INNER), F32), pltpu.VMEM((Q, INNER), F32)],
        compiler_params=_cparams("arbitrary"), name=name)(xbc_act, xbc_act, proj, proj, dt_bias, a_log, d_x, norm_g, rep)


def _ssd_bwd(xbc_act, proj, y, dyn, hprev, dproj, dt_bias, a_log, d_x, norm_g, *, name):
    n = xbc_act.shape[0]
    nc = n // Q
    rep, sel = _head_consts()
    gw = INNER // GROUPS

    def rev(w, cb=0):
        return pl.BlockSpec((Q, w), lambda i: (nc - 1 - i, cb))

    def body(xs_ref, bc_ref, dt_ref, z_ref, y_ref, dyn_ref, hp_ref, dtb_ref, alog_ref, dx_ref, ng_ref, rep_ref,
             sel_ref, dproj_in, dz_ref, ddt_ref, dxbc_ref, dng_ref, ddtb_ref, dalog_ref, dd_ref,
             dh_scr, dxt_scr, st_scr, off_scr, rs_scr, cs_scr, dng_acc, ddtb_acc, da_acc, dd_acc):
        del dproj_in
        i = pl.program_id(0)

        @pl.when(i == 0)
        def _():
            for r in (dh_scr, dng_acc, ddtb_acc, da_acc, dd_acc):
                r[...] = jnp.zeros_like(r)

        y = y_ref[...].astype(F32)
        z = z_ref[...].astype(F32)
        sz = _sigmoid(z)
        silu = z * sz
        v = y * silu
        dyn = dyn_ref[...].astype(F32)
        dvs = []
        for g in range(GROUPS):
            gs = slice(g * gw, (g + 1) * gw)
            vg = v[:, gs]
            r = lax.rsqrt(jnp.mean(vg * vg, axis=-1, keepdims=True) + EPS)
            vn = vg * r
            dvn = dyn[:, gs] * ng_ref[:, gs]
            dng_acc[:, gs] += _colsum(dyn[:, gs] * vn)
            dvs.append(r * (dvn - vn * jnp.mean(dvn * vn, axis=-1, keepdims=True)))
        dv = jnp.concatenate(dvs, axis=1)
        dy = dv * silu
        dz_ref[...] = (dv * y * (sz * (1.0 + z * (1.0 - sz)))).astype(dz_ref.dtype)

        dt_raw = dt_ref[...].astype(F32)
        t = _chunk_decay_terms(dt_raw, dtb_ref[...], alog_ref[...], rep_ref[...])
        xs = xs_ref[...].astype(F32)
        dsk = dx_ref[...]
        dd_acc[...] += _colsum(dy * xs)
        xt = xs * t["dt_x"]
        dye = dy * t["e_x"]
        xtf = xt * t["f_x"]
        hp = hp_ref[...]
        dh = dh_scr[...]
        lo = lax.broadcasted_iota(jnp.int32, (Q, 128), 1) < HEAD_DIM
        rs_scr[...] = jnp.zeros_like(rs_scr)
        cs_scr[...] = jnp.zeros_like(cs_scr)
        for g in range(GROUPS):
            gs = slice(g * gw, (g + 1) * gw)
            bm = bc_ref[:, g * NSTATE:(g + 1) * NSTATE]
            cm = bc_ref[:, GROUPS * NSTATE + g * NSTATE:GROUPS * NSTATE + (g + 1) * NSTATE]
            cbt = _dot(bm, cm, NT)
            dhg = dh[:, gs].astype(BF)
            hpg = hp[:, gs].astype(BF)
            dxt_state = _dot(bm, dhg) * t["f_x"][:, gs]
            st_scr[:, gs] = dxt_state
            dye_g = dye[:, gs]
            off_scr[:, gs] = dye_g * _dot(cm, hpg)
            dye_b = dye_g.astype(BF)
            db = _dot(xtf[:, gs].astype(BF), dhg, NT)
            dc = _dot(dye_b, hpg, NT)
            dh_scr[:, gs] = t["cd_x"][:, gs] * dh[:, gs] + _dot(cm, dye_b, TN)
            dcbt = jnp.zeros((Q, Q), F32)
            for jj in range(gw // 128):
                p = g * (gw // 128) + jj
                sl = slice(p * 128, (p + 1) * 128)
                lts, wfs = [], []
                for hd in (2 * p, 2 * p + 1):
                    seg_t = t["acum_t"][hd:hd + 1, :] - t["acum"][:, hd:hd + 1]
                    lt = jnp.exp(jnp.where(t["upper"], seg_t, -jnp.inf))
                    lts.append(lt)
                    wfs.append(cbt * lt)
                dyp = dy[:, sl]
                dxt_diag = _dot(jnp.concatenate([w.astype(BF) for w in wfs], axis=1), _block_diag2(dyp, lo))
                dwt2 = _dot(_block_diag2(xt[:, sl], lo), dyp.astype(BF), NT)
                for k, hd in enumerate((2 * p, 2 * p + 1)):
                    dwt = dwt2[k * Q:(k + 1) * Q]
                    dcbt = dcbt + dwt * lts[k]
                    mt = dwt * wfs[k]
                    rs_scr[hd:hd + 1, :] = _colsum(mt)
                    cs_scr[:, hd:hd + 1] = jnp.sum(mt, axis=1, keepdims=True)
                dxt_scr[:, sl] = dxt_diag + dxt_state[:, jj * 128:(jj + 1) * 128]
            dcbt_b = dcbt.astype(BF)
            db = db + _dot(dcbt_b, cm)
            dc = dc + _dot(dcbt_b, bm, TN)
            dxbc_ref[:, INNER + g * NSTATE:INNER + (g + 1) * NSTATE] = db.astype(dxbc_ref.dtype)
            dxbc_ref[:, INNER + (GROUPS + g) * NSTATE:INNER + (GROUPS + g + 1) * NSTATE] = dc.astype(dxbc_ref.dtype)
        dxt = dxt_scr[...]
        dst = st_scr[...]
        sel_m = sel_ref[...]
        sums = _split_dot(jnp.concatenate([off_scr[...], xs * dst, xs * dxt], axis=0), sel_m)
        r1_off, r3_state, r3 = sums[0:Q], sums[Q:2 * Q], sums[2 * Q:3 * Q]
        t1 = _colsum(xt * dst)
        t2 = _colsum(dh * hp)
        tails = _split_dot(jnp.concatenate([jnp.broadcast_to(t1, (8, INNER)), jnp.broadcast_to(t2, (8, INNER))], axis=0),
                           sel_m)
        extra = tails[0:1] + jnp.exp(t["alast"]) * tails[8:9]
        last_row = lax.broadcasted_iota(jnp.int32, (Q, 128), 0) == Q - 1
        da_cum = (rs_scr[...].T - cs_scr[...]) + r1_off - t["dt"] * r3_state + jnp.where(last_row, extra, 0.0)
        dadt = lax.dot_general(t["upper"].astype(F32), da_cum, NN, precision=HIGHEST, preferred_element_type=F32)
        ddt = r3 + t["a"] * dadt
        da_acc[...] += _colsum(dadt * t["dt"])
        real = lax.broadcasted_iota(jnp.int32, (Q, 128), 1) < HEADS
        ddraw = jnp.where(real, ddt * _sigmoid(dt_raw + dtb_ref[...]), 0.0)
        ddt_ref[...] = ddraw.astype(ddt_ref.dtype)
        ddtb_acc[...] += _colsum(ddraw)
        dxbc_ref[:, 0:INNER] = (dy * dsk + dxt * t["dt_x"]).astype(dxbc_ref.dtype)

        @pl.when(i == nc - 1)
        def _():
            dng_ref[...] = dng_acc[...]
            ddtb_ref[...] = ddtb_acc[...]
            dalog_ref[...] = da_acc[...] * t["a"]
            dd_ref[...] = _split_dot(jnp.broadcast_to(dd_acc[...], (8, INNER)), sel_m)[0:1]

    vec = _full((1, INNER))
    hv = _full((1, 128))
    return pl.pallas_call(
        body, grid=(nc,),
        in_specs=[rev(INNER, 0), rev(2 * GROUPS * NSTATE, INNER // (2 * GROUPS * NSTATE)),
                  rev(DT_PAD, P_DT // DT_PAD), rev(INNER, P_Z // INNER), rev(INNER), rev(INNER),
                  pl.BlockSpec((None, NSTATE, INNER), lambda i: (nc - 1 - i, 0, 0)),
                  hv, hv, vec, vec, _full((128, INNER)), _full((INNER, 128)), pl.BlockSpec(memory_space=pl.ANY)],
        out_specs=[rev(INNER, P_Z // INNER), rev(DT_PAD), rev(XBC), vec, hv, hv, hv],
        out_shape=[S(dproj.shape, dproj.dtype), S((n, DT_PAD), BF), S((n, XBC), BF),
                   S((1, INNER), F32), S((1, 128), F32), S((1, 128), F32), S((1, 128), F32)],
        scratch_shapes=[pltpu.VMEM((NSTATE, INNER), F32), pltpu.VMEM((Q, INNER), F32), pltpu.VMEM((Q, INNER), F32),
                        pltpu.VMEM((Q, INNER), F32), pltpu.VMEM((128, Q), F32), pltpu.VMEM((Q, 128), F32),
                        pltpu.VMEM((1, INNER), F32), pltpu.VMEM((1, 128), F32), pltpu.VMEM((1, 128), F32),
                        pltpu.VMEM((1, INNER), F32)],
        input_output_aliases={13: 0},
        compiler_params=_cparams("arbitrary"), name=name)(
            xbc_act, xbc_act, proj, proj, y, dyn, hprev, dt_bias, a_log, d_x, norm_g, rep, sel, dproj)


def _mixer_out_fwd(yn, a_conf, a_sc, proj, b_gate, w_ssd, w_conf, w_sc, w_o, x, mod, *, name):
    n = x.shape[0]
    tm = _tile(n, 256)

    def body(yn_ref, ac_ref, as_ref, gt_ref, bg_ref, wa_ref, wb_ref, wc_ref, wo_ref, x_ref, mod_ref,
             ya_ref, yb_ref, yc_ref, mg_ref, mix_ref, xn_ref):
        ya = _dot(yn_ref[...], wa_ref[...])
        yb = _dot(ac_ref[...], wb_ref[...])
        yc = _dot(as_ref[...], wc_ref[...])
        ya_ref[...] = ya.astype(ya_ref.dtype)
        yb_ref[...] = yb.astype(yb_ref.dtype)
        yc_ref[...] = yc.astype(yc_ref.dtype)
        g = _sigmoid(gt_ref[...].astype(F32) + bg_ref[...])
        merged = (g[:, 0:D] * ya + g[:, D:2 * D] * yb + g[:, 2 * D:] * yc).astype(mg_ref.dtype)
        mg_ref[...] = merged
        mix = _dot(merged, wo_ref[...])
        mix_ref[...] = mix.astype(mix_ref.dtype)
        xn_ref[...] = x_ref[...] + mod_ref[2:3, :] * mix

    act = S((n, D), BF)
    return pl.pallas_call(
        body, grid=(n // tm,),
        in_specs=[_rows(tm, INNER), _rows(tm, CONF_W), _rows(tm, SC_W), _rows(tm, 3 * D, P_GATES // (3 * D)),
                  _full((1, 3 * D)), _full((INNER, D)), _full((CONF_W, D)), _full((SC_W, D)), _full((D, D)),
                  _rows(tm, D), _full((3, D))],
        out_specs=[_rows(tm, D)] * 6,
        out_shape=[act, act, act, act, act, S((n, D), F32)],
        compiler_params=_cparams("parallel"), name=name)(yn, a_conf, a_sc, proj, b_gate, w_ssd, w_conf, w_sc, w_o, x, mod)


def _mixer_out_bwd(dx, mix, ya, yb, yc, proj, b_gate, w_ssd, w_conf, w_sc, w_o, mod, *, name):
    n = dx.shape[0]
    tm = _tile(n, 256)

    def body(dx_ref, mix_ref, ya_ref, yb_ref, yc_ref, gt_ref, bg_ref, wa_ref, wb_ref, wc_ref, wo_ref, mod_ref,
             do_ref, dya_ref, dyb_ref, dyc_ref, dgt_ref, dyn_ref, dac_ref, das_ref, dgm_ref, dbg_ref):
        i = pl.program_id(0)
        dxv = dx_ref[...]
        do = (dxv * mod_ref[2:3, :]).astype(BF)
        do_ref[...] = do
        dm = _dot(do, wo_ref[...], NT)
        g = _sigmoid(gt_ref[...].astype(F32) + bg_ref[...])
        @pl.when(i == 0)
        def _():
            dgm_ref[...] = jnp.zeros_like(dgm_ref)
            dbg_ref[...] = jnp.zeros_like(dbg_ref)

        dys = []
        for j, (y_ref, o_ref) in enumerate(((ya_ref, dya_ref), (yb_ref, dyb_ref), (yc_ref, dyc_ref))):
            gj = g[:, j * D:(j + 1) * D]
            dyj = (dm * gj).astype(BF)
            o_ref[...] = dyj
            dys.append(dyj)
            dgpre = dm * y_ref[...].astype(F32) * gj * (1.0 - gj)
            dgt_ref[:, j * D:(j + 1) * D] = dgpre.astype(dgt_ref.dtype)
            dbg_ref[:, j * D:(j + 1) * D] += _colsum(dgpre)
        dyn_ref[...] = _dot(dys[0], wa_ref[...], NT).astype(dyn_ref.dtype)
        dac_ref[...] = _dot(dys[1], wb_ref[...], NT).astype(dac_ref.dtype)
        das_ref[...] = _dot(dys[2], wc_ref[...], NT).astype(das_ref.dtype)
        dgm_ref[...] += _colsum(dxv * mix_ref[...].astype(F32))

    act = S((n, D), BF)
    return pl.pallas_call(
        body, grid=(n // tm,),
        in_specs=[_rows(tm, D)] * 5 + [_rows(tm, 3 * D, P_GATES // (3 * D)), _full((1, 3 * D)), _full((INNER, D)),
                                       _full((CONF_W, D)), _full((SC_W, D)), _full((D, D)), _full((3, D))],
        out_specs=[_rows(tm, D)] * 4 + [_rows(tm, 3 * D, P_GATES // (3 * D)), _rows(tm, INNER), _rows(tm, CONF_W),
                                        _rows(tm, SC_W), _full((1, D)), _full((1, 3 * D))],
        out_shape=[act, act, act, act, S((n, PW), BF), S((n, INNER), BF), S((n, CONF_W), BF), S((n, SC_W), BF),
                   S((1, D), F32), S((1, 3 * D), F32)],
        compiler_params=_cparams("arbitrary"), name=name)(dx, mix, ya, yb, yc, proj, b_gate, w_ssd, w_conf, w_sc, w_o, mod)


def _pad_w_in(w):
    zeros = jnp.zeros((w.shape[0], XBC_PAD - XBC - (R_CONF - R_DT)), w.dtype)
    return jnp.concatenate([w[:, R_GATES:], w[:, R_SC:R_GATES], w[:, R_XBC:R_DT], w[:, R_DT:R_CONF], zeros,
                            w[:, R_Z:R_XBC], w[:, R_CONF:R_SC]], axis=1)


def _unpad_w_in(wp):
    return jnp.concatenate([wp[:, P_Z:P_Z + INNER], wp[:, P_XBC:P_XBC + XBC], wp[:, P_DT:P_DT + HEADS],
                            wp[:, P_CONF:P_CONF + 2 * CONF_W], wp[:, P_SC:P_SC + 3 * SC_W], wp[:, P_GATES:P_GATES + 3 * D]],
                           axis=1)


def _row(v):
    return v.reshape(1, -1)


def _head_row(v):
    return jnp.pad(v, (0, 128 - HEADS)).reshape(1, 128)


def _layer_fwd(x, p, mod_mix, mod_ffn, tag):
    sv = {"x0": x}
    h = _modnorm_fwd(x, _row(p["norm_mix_g"]), mod_mix, name=f"modnorm_mix_fwd{tag}")
    proj = _mm_nn(h, p["w_in_pad"], out_dtype=BF, tm=1024, tn=2048, name=f"proj_fwd{tag}")
    xbc_act = _ssd_pre_fwd(proj, p["ssd_conv_w"], _row(p["ssd_conv_b"]), name=f"ssd_pre_fwd{tag}")
    d_x = _row(jnp.repeat(p["ssd_d"], HEAD_DIM))
    y, yn, hprev = _ssd_fwd(xbc_act, proj, _head_row(p["ssd_dt_bias"]), _head_row(p["ssd_a_log"]), d_x,
                            _row(p["ssd_norm_g"]), name=f"ssd_fwd{tag}")
    a_conf, uc = _conf_fwd(proj, p["conf_conv_w"], _row(p["conf_conv_b"]), _row(p["conf_ln_g"]), _row(p["conf_ln_b"]),
                           name=f"conf_fwd{tag}")
    a_sc = _sc_fwd(proj, p["sc_conv_w"], name=f"sc_fwd{tag}")
    ya, yb, yc, merged, mix, x1 = _mixer_out_fwd(yn, a_conf, a_sc, proj, _row(p["b_gate"]), p["w_ssd_out"],
                                                 p["w_conf_out"], p["w_sc_out"], p["w_o"], x, mod_mix,
                                                 name=f"mixer_out_fwd{tag}")
    h2 = _modnorm_fwd(x1, _row(p["norm_ffn_g"]), mod_ffn, name=f"modnorm_ffn_fwd{tag}")
    up = _mm_nn(h2, p["w_up"], out_dtype=BF, tm=1024, tn=1408, name=f"up_fwd{tag}")
    act = _ffn_act_fwd(up, p["ffn_conv_w"], _row(p["ffn_conv_b"]), name=f"ffn_act_fwd{tag}")
    o, x2 = _mm_resid(act, p["w_down"], x1, mod_ffn, tm=512, name=f"down_fwd{tag}")
    sv.update(h=h, proj=proj, xbc_act=xbc_act, d_x=d_x, y=y, yn=yn, hprev=hprev, a_conf=a_conf, uc=uc, a_sc=a_sc,
              ya=ya, yb=yb, yc=yc, merged=merged, mix=mix, x1=x1, h2=h2, up=up, act=act, o=o)
    return x2, sv


def _layer_bwd(dx, p, sv, mod_mix, mod_ffn, tag):
    g = {}
    do2, dgate_ffn = _gate_bwd(dx, sv["o"], mod_ffn, name=f"gate_ffn_bwd{tag}")
    dact = _mm_nt(do2, p["w_down"], out_dtype=BF, tm=1024, tk=D, name=f"down_dx{tag}")
    g["w_down"] = _mm_tn(sv["act"], do2, tn=D, tk=512, name=f"down_dw{tag}")
    dup, g["ffn_conv_w"], dffn_b = _ffn_act_bwd(sv["up"], dact, p["ffn_conv_w"], _row(p["ffn_conv_b"]),
                                               name=f"ffn_act_bwd{tag}")
    g["ffn_conv_b"] = dffn_b[0]
    dh2 = _mm_nt(dup, p["w_up"], out_dtype=F32, tm=1024, tk=1408, name=f"up_dx{tag}")
    g["w_up"] = _mm_tn(sv["h2"], dup, tn=1408, tk=512, name=f"up_dw{tag}")
    dx1, dgn, dsh, dsc = _modnorm_bwd(dh2, sv["x1"], dx, _row(p["norm_ffn_g"]), mod_ffn, name=f"modnorm_ffn_bwd{tag}")
    g["norm_ffn_g"] = dgn[0]
    dmod_ffn = jnp.concatenate([dsh[0], dsc[0], dgate_ffn[0]])

    (do1, dya, dyb, dyc, dproj, dyn, dac, dasc, dgate_mix, dbg) = _mixer_out_bwd(
        dx1, sv["mix"], sv["ya"], sv["yb"], sv["yc"], sv["proj"], _row(p["b_gate"]), p["w_ssd_out"], p["w_conf_out"],
        p["w_sc_out"], p["w_o"], mod_mix, name=f"mixer_out_bwd{tag}")
    g["b_gate"] = dbg[0]
    g["w_o"] = _mm_tn(sv["merged"], do1, tn=D, tk=512, name=f"wo_dw{tag}")
    g["w_ssd_out"] = _mm_tn(sv["yn"], dya, tn=D, tk=512, name=f"wssd_dw{tag}")
    g["w_conf_out"] = _mm_tn(sv["a_conf"], dyb, tn=D, tk=512, name=f"wconf_dw{tag}")
    g["w_sc_out"] = _mm_tn(sv["a_sc"], dyc, tn=D, tk=512, name=f"wsc_dw{tag}")
    dproj, g["conf_conv_w"], dcb, dlg, dlb = _conf_bwd(sv["proj"], sv["uc"], dac, dproj, p["conf_conv_w"],
                                                      _row(p["conf_ln_g"]), _row(p["conf_ln_b"]), name=f"conf_bwd{tag}")
    g["conf_conv_b"], g["conf_ln_g"], g["conf_ln_b"] = dcb[0], dlg[0], dlb[0]
    dproj, g["sc_conv_w"] = _sc_bwd(sv["proj"], dasc, dproj, p["sc_conv_w"], name=f"sc_bwd{tag}")
    dproj, ddt, dxbc_act, dng, ddtb, dalog, ddd = _ssd_bwd(
        sv["xbc_act"], sv["proj"], sv["y"], dyn, sv["hprev"], dproj, _head_row(p["ssd_dt_bias"]),
        _head_row(p["ssd_a_log"]), sv["d_x"], _row(p["ssd_norm_g"]), name=f"ssd_bwd{tag}")
    g["ssd_norm_g"], g["ssd_dt_bias"], g["ssd_a_log"], g["ssd_d"] = dng[0], ddtb[0, :HEADS], dalog[0, :HEADS], ddd[0, :HEADS]
    dproj, g["ssd_conv_w"], dsb = _ssd_pre_bwd(sv["proj"], dxbc_act, ddt, dproj, p["ssd_conv_w"], _row(p["ssd_conv_b"]),
                                               name=f"ssd_pre_bwd{tag}")
    g["ssd_conv_b"] = dsb[0]
    dh = _mm_nt(dproj, p["w_in_pad"], out_dtype=F32, tm=1024, tk=2048, name=f"proj_dx{tag}")
    g["w_in_pad"] = _mm_tn(sv["h"], dproj, tn=2048, tk=512, name=f"proj_dw{tag}")
    dx0, dgn, dsh, dsc = _modnorm_bwd(dh, sv["x0"], dx1, _row(p["norm_mix_g"]), mod_mix, name=f"modnorm_mix_bwd{tag}")
    g["norm_mix_g"] = dgn[0]
    dmod_mix = jnp.concatenate([dsh[0], dsc[0], dgate_mix[0]])
    return dx0, g, dmod_mix, dmod_ffn


def _local_step(x, target, layers, mods, final_norm_g):
    saved = []
    for i, p in enumerate(layers):
        x, sv = _layer_fwd(x, p, mods[i][0], mods[i][1], f"_l{i}")
        saved.append(sv)
    dx, loss, dfg = _final_loss(x, _row(final_norm_g), target, name="final_loss")
    grads, dmods = [None] * len(layers), [None] * len(layers)
    for i in reversed(range(len(layers))):
        dx, grads[i], dmm, dmf = _layer_bwd(dx, layers[i], saved[i], mods[i][0], mods[i][1], f"_l{i}")
        dmods[i] = (dmm, dmf)
    return loss, dx, grads, dmods, dfg[0]


MESH = pl.DeviceIdType.MESH
ANY = pl.BlockSpec(memory_space=pl.ANY)
VMEM = pl.BlockSpec(memory_space=pltpu.VMEM)


def _mesh_pos():
    return lax.axis_index("x"), lax.axis_index("y"), lax.axis_index("c")


def _peer(pos, mask):
    return tuple(1 - v if (mask >> (2 - k)) & 1 else v for k, v in enumerate(pos))


def _lin(pos):
    return 4 * pos[0] + 2 * pos[1] + pos[2]


def _chip(pos):
    return 2 * pos[0] + pos[1]


def _rcopy(src, dst, send_sem, recv_sem, dev):
    return pltpu.make_async_remote_copy(src_ref=src, dst_ref=dst, send_sem=send_sem, recv_sem=recv_sem,
                                        device_id=dev, device_id_type=MESH)


CHIP_MASKS = (2, 4, 6)
SIBLING = 1
ADA_COLS = 3 * D // N_CHIPS
CONV_ROWS, CONV_COLS = 48, 2 * DFF // N_CHIPS
CONV_PACK = {"ffn_conv_w": (0, FFN_K, 2 * DFF // N_CHIPS), "ssd_conv_w": (3, SSD_K, XBC // N_CHIPS),
             "conf_conv_w": (8, CONF_K, CONF_W // N_CHIPS), "sc_conv_w": (40, SC_K, SC_W // N_CHIPS)}


def _ada_exchange(c_blk, ada_mix_w, ada_ffn_w, conv_pack):
    def body(c_ref, wm_ref, wf_ref, cw_ref, mods_ref, sc_ref, cwall_ref,
             call_scr, modp_scr, recv_scr, s1, r1, s3, r3, s4, r4):
        pos = _mesh_pos()
        me, km = _lin(pos), _chip(pos)
        call_scr[me] = c_ref[...]
        cwall_ref[km] = cw_ref[...]
        sends = []
        for m in range(1, N_DEV):
            sends.append(_rcopy(c_ref, call_scr.at[me], s1.at[m - 1], r1.at[m - 1], _peer(pos, m)))
        for j, m in enumerate(CHIP_MASKS):
            sends.append(_rcopy(cw_ref, cwall_ref.at[km], s4.at[j], r4.at[j], _peer(pos, m)))
        for cp in sends:
            cp.start()
        for m in range(1, N_DEV):
            src = _peer(pos, m)
            _rcopy(c_ref, call_scr.at[_lin(src)], s1.at[m - 1], r1.at[m - 1], src).wait_recv()
        cm = jnp.concatenate([call_scr[d, 0:1, :] for d in range(N_DEV)], axis=0)
        sc = cm * _sigmoid(cm)
        sc_ref[...] = sc
        for j, w in enumerate((wm_ref.at[0], wm_ref.at[1], wf_ref.at[0], wf_ref.at[1])):
            modp_scr[:, j * ADA_COLS:(j + 1) * ADA_COLS] = lax.dot_general(
                sc, w[...], NN, precision=HIGHEST, preferred_element_type=F32)
        recv_scr[km] = modp_scr[...]
        sends3 = [_rcopy(modp_scr, recv_scr.at[km], s3.at[j], r3.at[j], _peer(pos, m)) for j, m in enumerate(CHIP_MASKS)]
        for cp in sends3:
            cp.start()
        for j, m in enumerate(CHIP_MASKS):
            src = _peer(pos, m)
            _rcopy(modp_scr, recv_scr.at[_chip(src)], s3.at[j], r3.at[j], src).wait_recv()
            _rcopy(cw_ref, cwall_ref.at[_chip(src)], s4.at[j], r4.at[j], src).wait_recv()
        for k in range(N_CHIPS):
            mods_ref[k:k + 1, :] = recv_scr[k, pl.ds(me, 1), :]
        for cp in sends + sends3:
            cp.wait_send()

    dma = pltpu.SemaphoreType.DMA
    return pl.pallas_call(
        body, in_specs=[VMEM] * 4, out_specs=[VMEM] * 3,
        out_shape=[S((N_CHIPS, 4 * ADA_COLS), F32), S((N_DEV, D), F32), S((N_CHIPS,) + conv_pack.shape, F32)],
        scratch_shapes=[pltpu.VMEM((N_DEV, 8, D), F32), pltpu.VMEM((N_DEV, 4 * ADA_COLS), F32),
                        pltpu.VMEM((N_CHIPS, N_DEV, 4 * ADA_COLS), F32),
                        dma((N_DEV - 1,)), dma((N_DEV - 1,)), dma((3,)), dma((3,)), dma((3,)), dma((3,))],
        compiler_params=pltpu.CompilerParams(vmem_limit_bytes=VMEM_LIMIT_V7X), name="ada_exchange")(
            c_blk, ada_mix_w, ada_ffn_w, conv_pack)


def _gather_weights(shards):
    na = len(shards)

    def body(*refs):
        ins, outs = refs[:na], refs[na:2 * na]
        ssem, rsem, fsend, frecv, lsem = refs[2 * na:]
        pos = _mesh_pos()
        c, km = pos[2], _chip(pos)
        sib = _peer(pos, SIBLING)
        local, sends = [], []
        for a in range(na):
            for l in range(DEPTH):
                local.append(pltpu.make_async_copy(ins[a].at[l], outs[a].at[l, km], lsem.at[a, l]))
            for j, m in enumerate(CHIP_MASKS):
                sends.append(_rcopy(ins[a].at[c], outs[a].at[c, km], ssem.at[a, j], rsem.at[a, j], _peer(pos, m)))
        for cp in local + sends:
            cp.start()
        passed = []
        for a in range(na):
            for j, m in enumerate(CHIP_MASKS):
                src = _peer(pos, m)
                blk = outs[a].at[c, _chip(src)]
                _rcopy(ins[a].at[c], blk, ssem.at[a, j], rsem.at[a, j], src).wait_recv()
                fw = _rcopy(blk, blk, fsend.at[a, j], frecv.at[a, j], sib)
                fw.start()
                passed.append(fw)
        for a in range(na):
            for j, m in enumerate(CHIP_MASKS):
                blk = outs[a].at[1 - c, _chip(_peer(pos, m))]
                _rcopy(blk, blk, fsend.at[a, j], frecv.at[a, j], sib).wait_recv()
        for cp in sends + passed:
            cp.wait_send()
        for cp in local:
            cp.wait()

    dma = pltpu.SemaphoreType.DMA
    return pl.pallas_call(
        body, in_specs=[ANY] * na, out_specs=[ANY] * na,
        out_shape=[S((DEPTH, N_CHIPS) + s.shape[1:], s.dtype) for s in shards],
        scratch_shapes=[dma((na, 3)), dma((na, 3)), dma((na, 3)), dma((na, 3)), dma((na, DEPTH))],
        name="gather_weights")(*shards)


def _swap_half(arrs):
    na = len(arrs)

    def body(*refs):
        ins, outs = refs[:na], refs[na:2 * na]
        ssem, rsem = refs[2 * na:]
        pos = _mesh_pos()
        c = pos[2]
        sib = _peer(pos, SIBLING)
        cps = [_rcopy(ins[a].at[1 - c], outs[a], ssem.at[a], rsem.at[a], sib) for a in range(na)]
        for cp in cps:
            cp.start()
        for cp in cps:
            cp.wait()

    dma = pltpu.SemaphoreType.DMA
    return pl.pallas_call(
        body, in_specs=[ANY] * na, out_specs=[ANY] * na,
        out_shape=[S(s.shape[1:], s.dtype) for s in arrs],
        scratch_shapes=[dma((na,)), dma((na,))], name="swap_half")(*arrs)


def _scatter_chips(arrs):
    na = len(arrs)

    def body(*refs):
        ins, outs = refs[:na], refs[na:2 * na]
        ssem, rsem, lsem = refs[2 * na:]
        pos = _mesh_pos()
        km = _chip(pos)
        local, sends = [], []
        for a in range(na):
            local.append(pltpu.make_async_copy(ins[a].at[km], outs[a].at[km], lsem.at[a]))
            for j, m in enumerate(CHIP_MASKS):
                dst = _peer(pos, m)
                sends.append(_rcopy(ins[a].at[_chip(dst)], outs[a].at[km], ssem.at[a, j], rsem.at[a, j], dst))
        for cp in local + sends:
            cp.start()
        for a in range(na):
            for j, m in enumerate(CHIP_MASKS):
                src = _peer(pos, m)
                _rcopy(ins[a].at[km], outs[a].at[_chip(src)], ssem.at[a, j], rsem.at[a, j], src).wait_recv()
        for cp in sends:
            cp.wait_send()
        for cp in local:
            cp.wait()

    dma = pltpu.SemaphoreType.DMA
    return pl.pallas_call(
        body, in_specs=[ANY] * na, out_specs=[ANY] * na, out_shape=[S(s.shape, s.dtype) for s in arrs],
        scratch_shapes=[dma((na, 3)), dma((na, 3)), dma((na,))], name="scatter_chips")(*arrs)


def _share_sibling(arrs):
    na = len(arrs)

    def body(*refs):
        ins, outs = refs[:na], refs[na:2 * na]
        ssem, rsem, lsem = refs[2 * na:]
        pos = _mesh_pos()
        c = pos[2]
        sib = _peer(pos, SIBLING)
        local = [pltpu.make_async_copy(ins[a], outs[a].at[c], lsem.at[a]) for a in range(na)]
        sends = [_rcopy(ins[a], outs[a].at[c], ssem.at[a], rsem.at[a], sib) for a in range(na)]
        for cp in local + sends:
            cp.start()
        for a in range(na):
            _rcopy(ins[a], outs[a].at[1 - c], ssem.at[a], rsem.at[a], sib).wait_recv()
        for cp in sends:
            cp.wait_send()
        for cp in local:
            cp.wait()

    dma = pltpu.SemaphoreType.DMA
    return pl.pallas_call(
        body, in_specs=[ANY] * na, out_specs=[ANY] * na, out_shape=[S((DEPTH,) + s.shape, s.dtype) for s in arrs],
        scratch_shapes=[dma((na,)), dma((na,)), dma((na,))], name="share_sibling")(*arrs)


def _small_allreduce(vec):
    r = vec.shape[0]

    def body(v_ref, sum_ref, all_ref, ssem, rsem):
        pos = _mesh_pos()
        me = _lin(pos)
        all_ref[me] = v_ref[...]
        cps = [_rcopy(v_ref, all_ref.at[me], ssem.at[m - 1], rsem.at[m - 1], _peer(pos, m)) for m in range(1, N_DEV)]
        for cp in cps:
            cp.start()
        for m in range(1, N_DEV):
            src = _peer(pos, m)
            _rcopy(v_ref, all_ref.at[_lin(src)], ssem.at[m - 1], rsem.at[m - 1], src).wait_recv()
        acc = all_ref[0]
        for d in range(1, N_DEV):
            acc = acc + all_ref[d]
        sum_ref[...] = acc
        for cp in cps:
            cp.wait_send()

    dma = pltpu.SemaphoreType.DMA
    return pl.pallas_call(
        body, in_specs=[VMEM], out_specs=[VMEM, VMEM],
        out_shape=[S((r, 128), F32), S((N_DEV, r, 128), F32)],
        scratch_shapes=[dma((N_DEV - 1,)), dma((N_DEV - 1,))],
        compiler_params=pltpu.CompilerParams(vmem_limit_bytes=VMEM_LIMIT_V7X), name="small_allreduce")(vec)


ROW_BYTES_TARGET = 1 << 20


def _row_tile(rows, cols, itemsize=4):
    t = rows
    while t % 2 == 0 and t * cols * itemsize > ROW_BYTES_TARGET and (t // 2) % 16 == 0:
        t //= 2
    return t


def _pair_add(g, other, layer):
    _, r, cdim = g.shape
    tr = _row_tile(r, cdim)

    def body(l_ref, g_ref, o_ref, out_ref):
        del l_ref
        out_ref[...] = (g_ref[...].astype(F32) + o_ref[...].astype(F32)).astype(out_ref.dtype)

    return pl.pallas_call(
        body,
        grid_spec=pltpu.PrefetchScalarGridSpec(
            num_scalar_prefetch=1, grid=(r // tr,),
            in_specs=[pl.BlockSpec((None, tr, cdim), lambda i, l: (l[0], i, 0)), pl.BlockSpec((tr, cdim), lambda i, l: (i, 0))],
            out_specs=pl.BlockSpec((tr, cdim), lambda i, l: (i, 0))),
        out_shape=S((r, cdim), BF), compiler_params=_cparams("parallel"), name="pair_add")(layer, g, other)


def _sum4(q):
    _, r, cdim = q.shape
    tr = _row_tile(r, cdim)

    def body(q_ref, out_ref):
        acc = q_ref[0].astype(F32) + q_ref[1].astype(F32)
        acc = acc + q_ref[2].astype(F32)
        out_ref[...] = acc + q_ref[3].astype(F32)

    return pl.pallas_call(
        body, grid=(r // tr,), in_specs=[pl.BlockSpec((N_CHIPS, tr, cdim), lambda i: (0, i, 0))],
        out_specs=pl.BlockSpec((tr, cdim), lambda i: (i, 0)), out_shape=S((r, cdim), F32),
        compiler_params=_cparams("parallel"), name="sum4")(q)


def _ada_w_grad(silu_c, dmod_cols, chip):
    def body(k_ref, sc_ref, dm_ref, o_ref):
        del k_ref
        o_ref[...] = lax.dot_general(sc_ref[...], dm_ref[...], TN, precision=HIGHEST, preferred_element_type=F32)

    return pl.pallas_call(
        body,
        grid_spec=pltpu.PrefetchScalarGridSpec(
            num_scalar_prefetch=1, grid=(4,),
            in_specs=[pl.BlockSpec((N_DEV, D), lambda j, k: (0, 0)),
                      pl.BlockSpec((None, N_DEV, ADA_COLS), lambda j, k: (4 * j + k[0], 0, 0))],
            out_specs=pl.BlockSpec((None, D, ADA_COLS), lambda j, k: (j, 0, 0))),
        out_shape=S((4, D, ADA_COLS), F32), compiler_params=_cparams("parallel"), name="ada_w_grad")(chip, silu_c, dmod_cols)


def _adamw(w, g, m, v, *, name):
    r, cdim = w.shape
    tr = _row_tile(r, cdim)
    c1 = 1.0 / (1.0 - ADAM_B1 ** ADAM_STEP)
    c2 = 1.0 / (1.0 - ADAM_B2 ** ADAM_STEP)

    def body(w_ref, g_ref, m_ref, v_ref, d_ref, mo_ref, vo_ref):
        gv = g_ref[...]
        mn = ADAM_B1 * m_ref[...] + (1.0 - ADAM_B1) * gv
        vn = ADAM_B2 * v_ref[...] + (1.0 - ADAM_B2) * (gv * gv)
        mo_ref[...] = mn
        vo_ref[...] = vn
        d_ref[...] = -ADAM_LR * ((mn * c1) / (jnp.sqrt(vn * c2) + ADAM_EPS) + ADAM_WD * w_ref[...])

    blk = pl.BlockSpec((tr, cdim), lambda i: (i, 0))
    return pl.pallas_call(
        body, grid=(r // tr,), in_specs=[blk] * 4, out_specs=[blk] * 3, out_shape=[S((r, cdim), F32)] * 3,
        compiler_params=_cparams("parallel"), name=name)(w, g, m, v)


WEIGHTS = ['ada_mix_w', 'ada_mix_b', 'norm_mix_g', 'w_in', 'b_gate', 'ssd_conv_w', 'ssd_conv_b', 'ssd_dt_bias',
           'ssd_a_log', 'ssd_d', 'ssd_norm_g', 'w_ssd_out', 'conf_conv_w', 'conf_conv_b', 'conf_ln_g', 'conf_ln_b',
           'w_conf_out', 'sc_conv_w', 'w_sc_out', 'w_o', 'ada_ffn_w', 'ada_ffn_b', 'norm_ffn_g', 'w_up', 'ffn_conv_w',
           'ffn_conv_b', 'w_down', 'final_norm_g']
SMALL = ['ada_mix_b', 'norm_mix_g', 'b_gate', 'ssd_conv_b', 'ssd_dt_bias', 'ssd_a_log', 'ssd_d', 'ssd_norm_g', 'conf_conv_b',
         'conf_ln_g', 'conf_ln_b', 'ada_ffn_b', 'norm_ffn_g', 'ffn_conv_b']
CONVS = ['ssd_conv_w', 'conf_conv_w', 'sc_conv_w', 'ffn_conv_w']
BIG = ['ada_mix_w', 'ada_ffn_w', 'w_in', 'w_up', 'w_conf_out', 'w_sc_out', 'w_ssd_out', 'w_o', 'w_down']


def _pack_rows(pieces):
    flat = [p.reshape(-1) for p in pieces]
    offs, o = [], 0
    for f in flat:
        offs.append(o)
        o += f.shape[0]
    total = -(-o // 1024) * 1024
    vec = jnp.concatenate(flat + [jnp.zeros((total - o,), F32)])
    return vec.reshape(total // 128, 128), offs


def _by_chip(a, axis):
    shp = a.shape
    a = a.reshape(shp[:axis] + (N_CHIPS, shp[axis] // N_CHIPS) + shp[axis + 1:])
    return jnp.moveaxis(a, axis, 0)


def _from_chips(a, axis):
    a = jnp.moveaxis(a, 0, axis)
    shp = a.shape
    return a.reshape(shp[:axis] + (shp[axis] * shp[axis + 1],) + shp[axis + 2:])


def kernel(x, c, ada_mix_w, ada_mix_b, norm_mix_g, w_in, b_gate, ssd_conv_w, ssd_conv_b, ssd_dt_bias, ssd_a_log, ssd_d, ssd_norm_g, w_ssd_out, conf_conv_w, conf_conv_b, conf_ln_g, conf_ln_b, w_conf_out, sc_conv_w, w_sc_out, w_o, ada_ffn_w, ada_ffn_b, norm_ffn_g, w_up, ffn_conv_w, ffn_conv_b, w_down, final_norm_g, loss_target, m_ada_mix_w, m_ada_mix_b, m_norm_mix_g, m_w_in, m_b_gate, m_ssd_conv_w, m_ssd_conv_b, m_ssd_dt_bias, m_ssd_a_log, m_ssd_d, m_ssd_norm_g, m_w_ssd_out, m_conf_conv_w, m_conf_conv_b, m_conf_ln_g, m_conf_ln_b, m_w_conf_out, m_sc_conv_w, m_w_sc_out, m_w_o, m_ada_ffn_w, m_ada_ffn_b, m_norm_ffn_g, m_w_up, m_ffn_conv_w, m_ffn_conv_b, m_w_down, m_final_norm_g, v_ada_mix_w, v_ada_mix_b, v_norm_mix_g, v_w_in, v_b_gate, v_ssd_conv_w, v_ssd_conv_b, v_ssd_dt_bias, v_ssd_a_log, v_ssd_d, v_ssd_norm_g, v_w_ssd_out, v_conf_conv_w, v_conf_conv_b, v_conf_ln_g, v_conf_ln_b, v_w_conf_out, v_sc_conv_w, v_w_sc_out, v_w_o, v_ada_ffn_w, v_ada_ffn_b, v_norm_ffn_g, v_w_up, v_ffn_conv_w, v_ffn_conv_b, v_w_down, v_final_norm_g):
    args = locals()
    w = {n: args[n] for n in WEIGHTS}
    mom = {n: args["m_" + n] for n in WEIGHTS}
    var = {n: args["v_" + n] for n in WEIGHTS}
    pos = _mesh_pos()
    chip = _chip(pos)
    core = pos[2]

    conv_pack = jnp.zeros((DEPTH, CONV_ROWS, CONV_COLS), F32)
    for n, (r0, taps, width) in CONV_PACK.items():
        conv_pack = conv_pack.at[:, r0:r0 + taps, 0:width].set(w[n])
    c_blk = jnp.pad(c, ((0, 7), (0, 0)))
    mods_raw, silu_c, conv_all = _ada_exchange(c_blk, ada_mix_w, ada_ffn_w, conv_pack)
    ada_b = jnp.concatenate([ada_mix_b, ada_ffn_b], axis=0)
    mod_all = mods_raw.reshape(N_CHIPS, 4, ADA_COLS).transpose(1, 0, 2).reshape(4, 3 * D) + ada_b
    mod_all = mod_all.reshape(4, 3, D)
    mods = [(mod_all[i], mod_all[2 + i]) for i in range(DEPTH)]
    conv_full = {n: _from_chips(conv_all[:, :, r0:r0 + taps, 0:width], 2) for n, (r0, taps, width) in CONV_PACK.items()}

    cast = lambda a: a.astype(BF)
    shards = [cast(w_in), cast(w_up), jnp.concatenate([cast(w_conf_out), cast(w_sc_out)], axis=1),
              jnp.concatenate([cast(w_ssd_out), cast(w_o), cast(w_down)], axis=1)]
    g_in, g_up, g_cs, g_row = _gather_weights(shards)
    layers = []
    for l in range(DEPTH):
        p = {n: w[n][l] for n in SMALL if not n.startswith("ada_")}
        p.update({n: conv_full[n][l] for n in CONVS})
        p["w_in_pad"] = _pad_w_in(_from_chips(g_in[l], 1))
        p["w_up"] = _from_chips(g_up[l], 1)
        p["w_conf_out"] = _from_chips(g_cs[l, :, 0:CONF_W], 1)
        p["w_sc_out"] = _from_chips(g_cs[l, :, CONF_W:], 1)
        p["w_ssd_out"] = _from_chips(g_row[l, :, 0:INNER // N_CHIPS], 0)
        p["w_o"] = _from_chips(g_row[l, :, INNER // N_CHIPS:(INNER + D) // N_CHIPS], 0)
        p["w_down"] = _from_chips(g_row[l, :, (INNER + D) // N_CHIPS:], 0)
        layers.append(p)

    seq = x.shape[1]
    loss, dx, grads, dmods, dfinal = _local_step(x.reshape(seq, D), loss_target.reshape(seq, D), layers, mods, final_norm_g)

    def stack(fn):
        return jnp.stack([fn(grads[l]).astype(BF) for l in range(DEPTH)])

    big = [stack(lambda g: _by_chip(_unpad_w_in(g["w_in_pad"]), 1)),
           stack(lambda g: _by_chip(g["w_up"], 1)),
           stack(lambda g: _by_chip(jnp.concatenate([g["w_conf_out"], g["w_sc_out"]], axis=0), 1)),
           stack(lambda g: jnp.concatenate([_by_chip(g["w_ssd_out"], 0), _by_chip(g["w_o"], 0), _by_chip(g["w_down"], 0)],
                                           axis=1))]
    theirs = _swap_half(big)
    layer_idx = jnp.reshape(core, (1,)).astype(jnp.int32)
    pair = []
    for g2, t in zip(big, theirs):
        r, cdim = t.shape[1], t.shape[2]
        pair.append(_pair_add(g2.reshape(DEPTH, N_CHIPS * r, cdim), t.reshape(N_CHIPS * r, cdim), layer_idx)
                    .reshape(N_CHIPS, r, cdim))
    parts = _scatter_chips(pair)
    reduced = _share_sibling([_sum4(q) for q in parts])
    gw = {"w_in": reduced[0], "w_up": reduced[1],
          "w_conf_out": reduced[2][:, 0:CONF_W], "w_sc_out": reduced[2][:, CONF_W:],
          "w_ssd_out": reduced[3][:, 0:INNER // N_CHIPS], "w_o": reduced[3][:, INNER // N_CHIPS:(INNER + D) // N_CHIPS],
          "w_down": reduced[3][:, (INNER + D) // N_CHIPS:]}

    dmod = jnp.stack([dmods[0][0], dmods[1][0], dmods[0][1], dmods[1][1]])
    small_local = {n: jnp.stack([grads[l][n] for l in range(DEPTH)]) for n in SMALL if not n.startswith("ada_")}
    pieces = [loss[0]] + [small_local[n] for n in SMALL if not n.startswith("ada_")]
    pieces += [jnp.stack([grads[l][n] for l in range(DEPTH)]) for n in CONVS] + [dfinal, dmod]
    vec, offs = _pack_rows(pieces)
    vsum, vall = _small_allreduce(vec)
    flat = vsum.reshape(-1)

    def piece(k, like):
        return flat[offs[k]:offs[k] + like.size].reshape(like.shape)

    loss_out = flat[0]
    k = 1
    for n in SMALL:
        if not n.startswith("ada_"):
            gw[n] = piece(k, small_local[n])
            k += 1
    for n in CONVS:
        full = piece(k, conv_full[n])
        width = CONV_PACK[n][2]
        gw[n] = lax.dynamic_slice_in_dim(full, chip * width, width, axis=2)
        k += 1
    gw["final_norm_g"] = piece(k, dfinal)
    k += 1
    dmod_sum = piece(k, dmod)
    gw["ada_mix_b"], gw["ada_ffn_b"] = dmod_sum[0:2], dmod_sum[2:4]
    dmod_all = vall.reshape(N_DEV, -1)[:, offs[k]:offs[k] + dmod.size]
    dmod_cols = dmod_all.reshape(N_DEV, 4 * N_CHIPS, ADA_COLS).transpose(1, 0, 2)
    ada_g = _ada_w_grad(silu_c, dmod_cols, jnp.reshape(chip, (1,)).astype(jnp.int32))
    gw["ada_mix_w"], gw["ada_ffn_w"] = ada_g[0:2], ada_g[2:4]

    delta, new_m, new_v = {}, {}, {}
    for n in BIG:
        shp = w[n].shape
        two_d = lambda a: a.reshape(shp[0] * shp[1], shp[2])
        d_, m_, v_ = _adamw(two_d(w[n]), two_d(gw[n]), two_d(mom[n]), two_d(var[n]), name=f"adamw_{n}")
        delta[n], new_m[n], new_v[n] = d_.reshape(shp), m_.reshape(shp), v_.reshape(shp)
    rest = [n for n in WEIGHTS if n not in BIG]
    packs = [_pack_rows([src[n] for n in rest])[0] for src in (w, gw, mom, var)]
    offs_r = _pack_rows([w[n] for n in rest])[1]
    outs = _adamw(*packs, name="adamw_small")
    for dst, o in zip((delta, new_m, new_v), outs):
        of = o.reshape(-1)
        for n, off in zip(rest, offs_r):
            dst[n] = of[off:off + w[n].size].reshape(w[n].shape)

    return (loss_out, dx[None], *[gw[n] for n in WEIGHTS], *[delta[n] for n in WEIGHTS],
            *[new_m[n] for n in WEIGHTS], *[new_v[n] for n in WEIGHTS])
```

```python
import functools

import jax
import jax.numpy as jnp
from jax import lax
from jax.experimental import pallas as pl
from jax.experimental.pallas import tpu as pltpu

F32 = jnp.float32
BF = jnp.bfloat16
S = jax.ShapeDtypeStruct

D = 1024
HEADS = 16
HEAD_DIM = 64
INNER = HEADS * HEAD_DIM
GROUPS = 2
NSTATE = 64
Q = 128
SSD_K = 4
XBC = INNER + 2 * GROUPS * NSTATE
CONF_W = 512
CONF_K = 31
SC_W = 512
SC_K = 3
DFF = 2816
FFN_K = 3
EPS = 1e-6
DEPTH = 2
R_Z, R_XBC, R_DT, R_CONF, R_SC, R_GATES, N_IN = 0, 1024, 2304, 2320, 3344, 4880, 7952
P_GATES, P_SC, P_XBC, P_Z, P_CONF, PW = 0, 3072, 4608, 6144, 7168, 8192
XBC_PAD = 1536
DT_PAD = 128
P_DT = P_XBC + XBC
N_CHIPS = 4
N_DEV = 8

ADAM_LR, ADAM_B1, ADAM_B2, ADAM_EPS, ADAM_WD, ADAM_STEP = 0.001, 0.9, 0.999, 1e-08, 0.01, 10

VMEM_LIMIT_V7X = 56 * 1024 * 1024
HIGHEST = lax.Precision.HIGHEST


def _cparams(*sem):
    return pltpu.CompilerParams(dimension_semantics=sem, vmem_limit_bytes=VMEM_LIMIT_V7X)


def _full(shape):
    n = len(shape)
    return pl.BlockSpec(shape, lambda *_: (0,) * n)


def _rows(tm, w, cb=0):
    return pl.BlockSpec((tm, w), lambda i: (i, cb))


def _prev_rows(tm, halo, w, cb=0):
    r = tm // halo
    return pl.BlockSpec((halo, w), lambda i: (jnp.maximum(i * r - 1, 0), cb))


def _next_rows(tm, halo, w, nrows, cb=0):
    r = tm // halo
    last = nrows // halo - 1
    return pl.BlockSpec((halo, w), lambda i: (jnp.minimum((i + 1) * r, last), cb))


def _sigmoid(v):
    return 1.0 / (1.0 + jnp.exp(-v))


def _softplus(v):
    return jnp.maximum(v, 0.0) + jnp.log(1.0 + jnp.exp(-jnp.abs(v)))


def _colsum(v):
    return jnp.sum(v, axis=0, keepdims=True)


def _tile(n, want):
    t = min(n, want)
    assert n % t == 0, (n, want)
    return t


NN = (((1,), (0,)), ((), ()))
NT = (((1,), (1,)), ((), ()))
TN = (((0,), (0,)), ((), ()))


def _dot(a, b, dims=NN):
    return lax.dot_general(a, b, dims, preferred_element_type=F32)


def _mm(a, b, *, dims, grid, a_spec, b_spec, o_spec, out_shape, acc_shape, name):
    nk = grid[2]

    def body(a_ref, b_ref, o_ref, acc_ref):
        k = pl.program_id(2)
        part = _dot(a_ref[...], b_ref[...], dims)
        if nk == 1:
            o_ref[...] = part.astype(o_ref.dtype)
        else:
            @pl.when(k == 0)
            def _():
                acc_ref[...] = part

            @pl.when(k > 0)
            def _():
                acc_ref[...] += part

            @pl.when(k == nk - 1)
            def _():
                o_ref[...] = acc_ref[...].astype(o_ref.dtype)

    return pl.pallas_call(
        body, grid=grid, in_specs=[a_spec, b_spec], out_specs=o_spec, out_shape=out_shape,
        scratch_shapes=[pltpu.VMEM(acc_shape if nk > 1 else (8, 128), F32)],
        compiler_params=_cparams("parallel", "parallel", "arbitrary"), name=name)(a, b)


def _mm_nn(a, b, *, out_dtype, tm, tn, name):
    m, k = a.shape
    n = b.shape[1]
    tm, tn = _tile(m, tm), _tile(n, tn)
    return _mm(a, b, dims=NN, grid=(m // tm, n // tn, 1),
               a_spec=pl.BlockSpec((tm, k), lambda i, j, kk: (i, 0)),
               b_spec=pl.BlockSpec((k, tn), lambda i, j, kk: (0, j)),
               o_spec=pl.BlockSpec((tm, tn), lambda i, j, kk: (i, j)),
               out_shape=S((m, n), out_dtype), acc_shape=(tm, tn), name=name)


def _mm_nt(a, b, *, out_dtype, tm, tk, name):
    m, kc = a.shape
    n = b.shape[0]
    tm, tk = _tile(m, tm), _tile(kc, tk)
    return _mm(a, b, dims=NT, grid=(m // tm, 1, kc // tk),
               a_spec=pl.BlockSpec((tm, tk), lambda i, j, kk: (i, kk)),
               b_spec=pl.BlockSpec((n, tk), lambda i, j, kk: (0, kk)),
               o_spec=pl.BlockSpec((tm, n), lambda i, j, kk: (i, 0)),
               out_shape=S((m, n), out_dtype), acc_shape=(tm, n), name=name)


def _mm_tn(a, b, *, tn, tk, name, out_dtype=BF, by_chip=False):
    kc, m = a.shape
    n = b.shape[1]
    tn, tk = _tile(n, tn), _tile(kc, tk)
    if by_chip:
        assert n == N_CHIPS * tn
        o_spec, out_shape = pl.BlockSpec((None, m, tn), lambda i, j, kk: (j, 0, 0)), S((N_CHIPS, m, tn), out_dtype)
    else:
        o_spec, out_shape = pl.BlockSpec((m, tn), lambda i, j, kk: (0, j)), S((m, n), out_dtype)
    return _mm(a, b, dims=TN, grid=(1, n // tn, kc // tk),
               a_spec=pl.BlockSpec((tk, m), lambda i, j, kk: (kk, 0)),
               b_spec=pl.BlockSpec((tk, tn), lambda i, j, kk: (kk, j)),
               o_spec=o_spec, out_shape=out_shape, acc_shape=(m, tn), name=name)


def _mm_nn_chips(a, b4, *, out_dtype, tm, name):
    m, k = a.shape
    n4 = b4.shape[2]
    tm = _tile(m, tm)
    return _mm(a, b4, dims=NN, grid=(m // tm, N_CHIPS, 1),
               a_spec=pl.BlockSpec((tm, k), lambda i, j, kk: (i, 0)),
               b_spec=pl.BlockSpec((None, k, n4), lambda i, j, kk: (j, 0, 0)),
               o_spec=pl.BlockSpec((tm, n4), lambda i, j, kk: (i, j)),
               out_shape=S((m, N_CHIPS * n4), out_dtype), acc_shape=(tm, n4), name=name)


def _mm_nt_chips(a, b4, *, out_dtype, tm, name):
    m = a.shape[0]
    n, n4 = b4.shape[1], b4.shape[2]
    tm = _tile(m, tm)
    return _mm(a, b4, dims=NT, grid=(m // tm, 1, N_CHIPS),
               a_spec=pl.BlockSpec((tm, n4), lambda i, j, kk: (i, kk)),
               b_spec=pl.BlockSpec((None, n, n4), lambda i, j, kk: (kk, 0, 0)),
               o_spec=pl.BlockSpec((tm, n), lambda i, j, kk: (i, 0)),
               out_shape=S((m, n), out_dtype), acc_shape=(tm, n), name=name)


def _mm_resid(a, b, x, mod, *, tm, name):
    m, k = a.shape
    n = b.shape[1]
    tm = _tile(m, tm)

    def body(a_ref, b_ref, x_ref, mod_ref, o_ref, xn_ref):
        o = _dot(a_ref[...], b_ref[...])
        o_ref[...] = o.astype(o_ref.dtype)
        xn_ref[...] = x_ref[...] + mod_ref[2:3, :] * o

    return pl.pallas_call(
        body, grid=(m // tm,),
        in_specs=[_rows(tm, k), _full((k, n)), _rows(tm, n), _full((3, n))],
        out_specs=[_rows(tm, n), _rows(tm, n)],
        out_shape=[S((m, n), BF), S((m, n), F32)],
        compiler_params=_cparams("parallel"), name=name)(a, b, x, mod)


def _modnorm_fwd(x, gain, mod, *, name):
    n = x.shape[0]
    tm = _tile(n, 512)

    def body(x_ref, g_ref, mod_ref, h_ref):
        xv = x_ref[...]
        r = lax.rsqrt(jnp.mean(xv * xv, axis=-1, keepdims=True) + EPS)
        y = xv * r * g_ref[...]
        h_ref[...] = (y * (1.0 + mod_ref[1:2, :]) + mod_ref[0:1, :]).astype(h_ref.dtype)

    return pl.pallas_call(
        body, grid=(n // tm,), in_specs=[_rows(tm, D), _full((1, D)), _full((3, D))],
        out_specs=_rows(tm, D), out_shape=S((n, D), BF), compiler_params=_cparams("parallel"), name=name)(x, gain, mod)


def _modnorm_bwd(dh, x, dres, gain, mod, *, name):
    n = x.shape[0]
    tm = _tile(n, 512)

    def body(dh_ref, x_ref, dres_ref, g_ref, mod_ref, dx_ref, dg_ref, dsh_ref, dsc_ref):
        i = pl.program_id(0)
        xv = x_ref[...]
        r = lax.rsqrt(jnp.mean(xv * xv, axis=-1, keepdims=True) + EPS)
        xh = xv * r
        dhv = dh_ref[...]
        g = g_ref[...]
        dy = dhv * (1.0 + mod_ref[1:2, :])
        dxh = dy * g
        dx = r * (dxh - xh * jnp.mean(dxh * xh, axis=-1, keepdims=True))
        dx_ref[...] = dres_ref[...] + dx

        @pl.when(i == 0)
        def _():
            dg_ref[...] = jnp.zeros_like(dg_ref)
            dsh_ref[...] = jnp.zeros_like(dsh_ref)
            dsc_ref[...] = jnp.zeros_like(dsc_ref)

        dg_ref[...] += _colsum(dy * xh)
        dsh_ref[...] += _colsum(dhv)
        dsc_ref[...] += _colsum(dhv * xh * g)

    vec = S((1, D), F32)
    return pl.pallas_call(
        body, grid=(n // tm,),
        in_specs=[_rows(tm, D), _rows(tm, D), _rows(tm, D), _full((1, D)), _full((3, D))],
        out_specs=[_rows(tm, D), _full((1, D)), _full((1, D)), _full((1, D))],
        out_shape=[S((n, D), F32), vec, vec, vec],
        compiler_params=_cparams("arbitrary"), name=name)(dh, x, dres, gain, mod)


def _final_loss(x, gain, target, *, name):
    n = x.shape[0]
    tm = _tile(n, 512)

    def body(x_ref, g_ref, t_ref, dx_ref, loss_ref, dg_ref):
        i = pl.program_id(0)
        xv = x_ref[...]
        g = g_ref[...]
        r = lax.rsqrt(jnp.mean(xv * xv, axis=-1, keepdims=True) + EPS)
        xh = xv * r
        err = xh * g - t_ref[...]
        dy = err * (1.0 / D)
        dxh = dy * g
        dx_ref[...] = r * (dxh - xh * jnp.mean(dxh * xh, axis=-1, keepdims=True))

        @pl.when(i == 0)
        def _():
            loss_ref[...] = jnp.zeros_like(loss_ref)
            dg_ref[...] = jnp.zeros_like(dg_ref)

        part = _colsum(jnp.sum(err * err, axis=-1, keepdims=True)) * (0.5 / D)
        loss_ref[...] += jnp.broadcast_to(part, loss_ref.shape)
        dg_ref[...] += _colsum(dy * xh)

    return pl.pallas_call(
        body, grid=(n // tm,),
        in_specs=[_rows(tm, D), _full((1, D)), _rows(tm, D)],
        out_specs=[_rows(tm, D), _full((1, 128)), _full((1, D))],
        out_shape=[S((n, D), F32), S((1, 128), F32), S((1, D), F32)],
        compiler_params=_cparams("arbitrary"), name=name)(x, gain, target)


def _gate_bwd(dx, o, mod, *, name):
    n = dx.shape[0]
    tm = _tile(n, 512)

    def body(dx_ref, o_ref, mod_ref, do_ref, dgt_ref):
        i = pl.program_id(0)
        dxv = dx_ref[...]
        do_ref[...] = (dxv * mod_ref[2:3, :]).astype(do_ref.dtype)

        @pl.when(i == 0)
        def _():
            dgt_ref[...] = jnp.zeros_like(dgt_ref)

        dgt_ref[...] += _colsum(dxv * o_ref[...].astype(F32))

    return pl.pallas_call(
        body, grid=(n // tm,), in_specs=[_rows(tm, D), _rows(tm, D), _full((3, D))],
        out_specs=[_rows(tm, D), _full((1, D))], out_shape=[S((n, D), BF), S((1, D), F32)],
        compiler_params=_cparams("arbitrary"), name=name)(dx, o, mod)


def _conv(buf, w_ref, taps, start, rows, ch):
    acc = None
    for k in range(taps):
        term = buf[pl.ds(start - (taps - 1) + k, rows), 0:ch] * w_ref[k:k + 1, :]
        acc = term if acc is None else acc + term
    return acc


def _conv_t(buf, w_ref, taps, start, rows, ch):
    acc = None
    for k in range(taps):
        term = buf[pl.ds(start + (taps - 1) - k, rows), 0:ch] * w_ref[k:k + 1, :]
        acc = term if acc is None else acc + term
    return acc


def _conv_dw(dw_ref, dy, xbuf, taps, xstart, rows, ch):
    for k in range(taps):
        dw_ref[k:k + 1, :] += _colsum(dy * xbuf[pl.ds(xstart - (taps - 1) + k, rows), 0:ch])


HALO = 16
CONF_HALO = 32
CHUNK = 16
STRIP = 256


def _blocks8(v):
    return [v[8 * i:8 * (i + 1)] for i in range(v.shape[0] // 8)]


def _delay_rows(blocks, s):
    sub = lax.broadcasted_iota(jnp.int32, blocks[0].shape, 0)
    rolled = [pltpu.roll(b, s, 0) for b in blocks]
    return [jnp.where(sub < s, rolled[i - 1], rolled[i]) for i in range(1, len(blocks))]


def _advance_rows(blocks, s):
    sub = lax.broadcasted_iota(jnp.int32, blocks[0].shape, 0)
    rolled = [pltpu.roll(b, 8 - s, 0) for b in blocks]
    return [jnp.where(sub < 8 - s, rolled[i], rolled[i + 1]) for i in range(len(blocks) - 1)]


def _conv3_chunk(tail, xv, wk):
    blocks = [tail] + _blocks8(xv)
    x1 = jnp.concatenate(_delay_rows(blocks, 1), axis=0)
    x2 = jnp.concatenate(_delay_rows(blocks, 2), axis=0)
    return wk[0] * x2 + wk[1] * x1 + wk[2] * xv


def _ssd_pre_fwd(proj, w, b, *, name):
    n = proj.shape[0]
    tm = _tile(n, 512)
    cb = P_XBC // XBC_PAD

    def body(prev_ref, cur_ref, w_ref, b_ref, o_ref, buf):
        i = pl.program_id(0)
        buf[0:HALO, :] = jnp.where(i == 0, 0.0, prev_ref[:, 0:XBC].astype(F32))
        buf[HALO:HALO + tm, :] = cur_ref[:, 0:XBC].astype(F32)
        c = _conv(buf, w_ref, SSD_K, HALO, tm, XBC) + b_ref[...]
        o_ref[...] = (c * _sigmoid(c)).astype(o_ref.dtype)

    return pl.pallas_call(
        body, grid=(n // tm,),
        in_specs=[_prev_rows(tm, HALO, XBC_PAD, cb), _rows(tm, XBC_PAD, cb), _full((SSD_K, XBC)), _full((1, XBC))],
        out_specs=_rows(tm, XBC), out_shape=S((n, XBC), BF),
        scratch_shapes=[pltpu.VMEM((HALO + tm, XBC), F32)],
        compiler_params=_cparams("parallel"), name=name)(proj, proj, w, b)


def _ssd_pre_bwd(proj, dact, ddt, dproj, w, b, *, name):
    n = proj.shape[0]
    tm = _tile(n, 512)
    nt = n // tm
    cb = P_XBC // XBC_PAD

    def body(xp_ref, xc_ref, xn_ref, dc_ref, dn_ref, ddt_ref, w_ref, b_ref, dproj_in, o_ref, dw_ref, db_ref, xbuf, dbuf):
        del dproj_in
        i = pl.program_id(0)
        xbuf[0:HALO, :] = jnp.where(i == 0, 0.0, xp_ref[:, 0:XBC].astype(F32))
        xbuf[HALO:HALO + tm, :] = xc_ref[:, 0:XBC].astype(F32)
        xbuf[HALO + tm:HALO + tm + HALO, :] = xn_ref[:, 0:XBC].astype(F32)
        c = _conv(xbuf, w_ref, SSD_K, HALO, tm + HALO, XBC) + b_ref[...]
        s = _sigmoid(c)
        dsilu = s * (1.0 + c * (1.0 - s))
        dbuf[0:tm, :] = dc_ref[...].astype(F32) * dsilu[0:tm]
        dbuf[tm:tm + HALO, :] = jnp.where(i == nt - 1, 0.0, dn_ref[...].astype(F32) * dsilu[tm:tm + HALO])
        dx = _conv_t(dbuf, w_ref, SSD_K, 0, tm, XBC)
        o_ref[:, 0:XBC] = dx.astype(o_ref.dtype)
        o_ref[:, XBC:XBC + DT_PAD] = ddt_ref[...]
        o_ref[:, XBC + DT_PAD:XBC_PAD] = jnp.zeros((tm, XBC_PAD - XBC - DT_PAD), o_ref.dtype)

        @pl.when(i == 0)
        def _():
            dw_ref[...] = jnp.zeros_like(dw_ref)
            db_ref[...] = jnp.zeros_like(db_ref)

        dcur = dbuf[0:tm, :]
        db_ref[...] += _colsum(dcur)
        _conv_dw(dw_ref, dcur, xbuf, SSD_K, HALO, tm, XBC)

    return pl.pallas_call(
        body, grid=(nt,),
        in_specs=[_prev_rows(tm, HALO, XBC_PAD, cb), _rows(tm, XBC_PAD, cb), _next_rows(tm, HALO, XBC_PAD, n, cb),
                  _rows(tm, XBC), _next_rows(tm, HALO, XBC, n), _rows(tm, DT_PAD),
                  _full((SSD_K, XBC)), _full((1, XBC)), pl.BlockSpec(memory_space=pl.ANY)],
        out_specs=[_rows(tm, XBC_PAD, cb), _full((SSD_K, XBC)), _full((1, XBC))],
        out_shape=[S(dproj.shape, dproj.dtype), S((SSD_K, XBC), F32), S((1, XBC), F32)],
        scratch_shapes=[pltpu.VMEM((HALO + tm + HALO, XBC), F32), pltpu.VMEM((tm + HALO, XBC), F32)],
        input_output_aliases={8: 0},
        compiler_params=_cparams("arbitrary"), name=name)(proj, proj, proj, dact, dact, ddt, w, b, dproj)


def _sc_fwd(proj, w, *, name):
    n = proj.shape[0]
    tm = _tile(n, 512)
    cb = P_SC // (3 * SC_W)

    def body(prev_ref, cur_ref, w_ref, o_ref, buf):
        i = pl.program_id(0)
        pv = prev_ref[...].astype(F32)
        cv = cur_ref[...].astype(F32)
        buf[0:HALO, :] = jnp.where(i == 0, 0.0, pv[:, SC_W:2 * SC_W] * pv[:, 2 * SC_W:])
        buf[HALO:HALO + tm, :] = cv[:, SC_W:2 * SC_W] * cv[:, 2 * SC_W:]
        q = _conv(buf, w_ref, SC_K, HALO, tm, SC_W)
        o_ref[...] = (cv[:, 0:SC_W] * q).astype(o_ref.dtype)

    return pl.pallas_call(
        body, grid=(n // tm,),
        in_specs=[_prev_rows(tm, HALO, 3 * SC_W, cb), _rows(tm, 3 * SC_W, cb), _full((SC_K, SC_W))],
        out_specs=_rows(tm, SC_W), out_shape=S((n, SC_W), BF),
        scratch_shapes=[pltpu.VMEM((HALO + tm, SC_W), F32)],
        compiler_params=_cparams("parallel"), name=name)(proj, proj, w)


def _sc_bwd(proj, da, dproj, w, *, name):
    n = proj.shape[0]
    tm = _tile(n, 512)
    nt = n // tm
    cb = P_SC // (3 * SC_W)

    def body(xp_ref, xc_ref, xn_ref, dc_ref, dn_ref, w_ref, dproj_in, o_ref, dw_ref, pbuf, dbuf):
        del dproj_in
        i = pl.program_id(0)
        pv = xp_ref[...].astype(F32)
        cv = xc_ref[...].astype(F32)
        nv = xn_ref[...].astype(F32)
        gb, gc, xv = cv[:, 0:SC_W], cv[:, SC_W:2 * SC_W], cv[:, 2 * SC_W:]
        pbuf[0:HALO, :] = jnp.where(i == 0, 0.0, pv[:, SC_W:2 * SC_W] * pv[:, 2 * SC_W:])
        pbuf[HALO:HALO + tm, :] = gc * xv
        q = _conv(pbuf, w_ref, SC_K, HALO, tm, SC_W)
        dav = dc_ref[...].astype(F32)
        dbuf[0:tm, :] = dav * gb
        dbuf[tm:tm + HALO, :] = jnp.where(i == nt - 1, 0.0, dn_ref[...].astype(F32) * nv[:, 0:SC_W])
        dp = _conv_t(dbuf, w_ref, SC_K, 0, tm, SC_W)
        o_ref[:, 0:SC_W] = (dav * q).astype(o_ref.dtype)
        o_ref[:, SC_W:2 * SC_W] = (dp * xv).astype(o_ref.dtype)
        o_ref[:, 2 * SC_W:] = (dp * gc).astype(o_ref.dtype)

        @pl.when(i == 0)
        def _():
            dw_ref[...] = jnp.zeros_like(dw_ref)

        _conv_dw(dw_ref, dbuf[0:tm, :], pbuf, SC_K, HALO, tm, SC_W)

    return pl.pallas_call(
        body, grid=(nt,),
        in_specs=[_prev_rows(tm, HALO, 3 * SC_W, cb), _rows(tm, 3 * SC_W, cb), _next_rows(tm, HALO, 3 * SC_W, n, cb),
                  _rows(tm, SC_W), _next_rows(tm, HALO, SC_W, n), _full((SC_K, SC_W)),
                  pl.BlockSpec(memory_space=pl.ANY)],
        out_specs=[_rows(tm, 3 * SC_W, cb), _full((SC_K, SC_W))],
        out_shape=[S(dproj.shape, dproj.dtype), S((SC_K, SC_W), F32)],
        scratch_shapes=[pltpu.VMEM((HALO + tm, SC_W), F32), pltpu.VMEM((tm + HALO, SC_W), F32)],
        input_output_aliases={6: 0},
        compiler_params=_cparams("arbitrary"), name=name)(proj, proj, proj, da, da, w, dproj)


def _conf_fwd(proj, w, b, ln_g, ln_b, *, name):
    n = proj.shape[0]
    tm = _tile(n, 512)
    cb = P_CONF // (2 * CONF_W)
    h = CONF_HALO

    def body(prev_ref, cur_ref, w_ref, b_ref, g_ref, be_ref, a_ref, uc_ref, buf):
        i = pl.program_id(0)
        pv = prev_ref[...].astype(F32)
        cv = cur_ref[...].astype(F32)
        buf[0:h, :] = jnp.where(i == 0, 0.0, pv[:, 0:CONF_W] * _sigmoid(pv[:, CONF_W:]))
        buf[h:h + tm, :] = cv[:, 0:CONF_W] * _sigmoid(cv[:, CONF_W:])
        uc = _conv(buf, w_ref, CONF_K, h, tm, CONF_W) + b_ref[...]
        uc_ref[...] = uc.astype(uc_ref.dtype)
        mu = jnp.mean(uc, axis=-1, keepdims=True)
        xc = uc - mu
        v = xc * lax.rsqrt(jnp.mean(xc * xc, axis=-1, keepdims=True) + EPS) * g_ref[...] + be_ref[...]
        a_ref[...] = (v * _sigmoid(v)).astype(a_ref.dtype)

    vec = _full((1, CONF_W))
    return pl.pallas_call(
        body, grid=(n // tm,),
        in_specs=[_prev_rows(tm, h, 2 * CONF_W, cb), _rows(tm, 2 * CONF_W, cb), _full((CONF_K, CONF_W)), vec, vec, vec],
        out_specs=[_rows(tm, CONF_W), _rows(tm, CONF_W)],
        out_shape=[S((n, CONF_W), BF), S((n, CONF_W), BF)],
        scratch_shapes=[pltpu.VMEM((h + tm, CONF_W), F32)],
        compiler_params=_cparams("parallel"), name=name)(proj, proj, w, b, ln_g, ln_b)


def _conf_bwd(proj, uc, da, dproj, w, ln_g, ln_b, *, name):
    n = proj.shape[0]
    tm = _tile(n, 512)
    nt = n // tm
    cb = P_CONF // (2 * CONF_W)
    h = CONF_HALO

    def body(xp_ref, xc_ref, ucc_ref, ucn_ref, dac_ref, dan_ref, w_ref, g_ref, be_ref, dproj_in,
             o_ref, dw_ref, db_ref, dg_ref, dbe_ref, ubuf, dbuf):
        del dproj_in
        i = pl.program_id(0)
        pv = xp_ref[...].astype(F32)
        cv = xc_ref[...].astype(F32)
        val, gt = cv[:, 0:CONF_W], cv[:, CONF_W:]
        sg = _sigmoid(gt)
        ubuf[0:h, :] = jnp.where(i == 0, 0.0, pv[:, 0:CONF_W] * _sigmoid(pv[:, CONF_W:]))
        ubuf[h:h + tm, :] = val * sg

        def ln_silu_bwd(ucv, dav):
            mu = jnp.mean(ucv, axis=-1, keepdims=True)
            xc = ucv - mu
            r = lax.rsqrt(jnp.mean(xc * xc, axis=-1, keepdims=True) + EPS)
            xh = xc * r
            v = xh * g_ref[...] + be_ref[...]
            s = _sigmoid(v)
            dv = dav * (s * (1.0 + v * (1.0 - s)))
            dxh = dv * g_ref[...]
            duc = r * (dxh - jnp.mean(dxh, axis=-1, keepdims=True) - xh * jnp.mean(dxh * xh, axis=-1, keepdims=True))
            return duc, dv, xh

        duc, dv, xh = ln_silu_bwd(ucc_ref[...].astype(F32), dac_ref[...].astype(F32))
        dbuf[0:tm, :] = duc
        ducn, _, _ = ln_silu_bwd(ucn_ref[...].astype(F32), dan_ref[...].astype(F32))
        dbuf[tm:tm + h, :] = jnp.where(i == nt - 1, 0.0, ducn)
        du = _conv_t(dbuf, w_ref, CONF_K, 0, tm, CONF_W)
        o_ref[:, 0:CONF_W] = (du * sg).astype(o_ref.dtype)
        o_ref[:, CONF_W:] = (du * val * sg * (1.0 - sg)).astype(o_ref.dtype)

        @pl.when(i == 0)
        def _():
            dw_ref[...] = jnp.zeros_like(dw_ref)
            db_ref[...] = jnp.zeros_like(db_ref)
            dg_ref[...] = jnp.zeros_like(dg_ref)
            dbe_ref[...] = jnp.zeros_like(dbe_ref)

        dg_ref[...] += _colsum(dv * xh)
        dbe_ref[...] += _colsum(dv)
        db_ref[...] += _colsum(duc)
        _conv_dw(dw_ref, duc, ubuf, CONF_K, h, tm, CONF_W)

    vec = _full((1, CONF_W))
    vshape = S((1, CONF_W), F32)
    return pl.pallas_call(
        body, grid=(nt,),
        in_specs=[_prev_rows(tm, h, 2 * CONF_W, cb), _rows(tm, 2 * CONF_W, cb),
                  _rows(tm, CONF_W), _next_rows(tm, h, CONF_W, n), _rows(tm, CONF_W), _next_rows(tm, h, CONF_W, n),
                  _full((CONF_K, CONF_W)), vec, vec, pl.BlockSpec(memory_space=pl.ANY)],
        out_specs=[_rows(tm, 2 * CONF_W, cb), _full((CONF_K, CONF_W)), vec, vec, vec],
        out_shape=[S(dproj.shape, dproj.dtype), S((CONF_K, CONF_W), F32), vshape, vshape, vshape],
        scratch_shapes=[pltpu.VMEM((h + tm, CONF_W), F32), pltpu.VMEM((tm + h, CONF_W), F32)],
        input_output_aliases={9: 0},
        compiler_params=_cparams("arbitrary"), name=name)(proj, proj, uc, uc, da, da, w, ln_g, ln_b, dproj)


def _ffn_act_fwd(up, w, b, *, name):
    n = up.shape[0]
    tm = _tile(n, 512)
    c2 = 2 * DFF

    def body(prev_ref, cur_ref, w_ref, b_ref, o_ref, u_ref):
        first = pl.program_id(0) == 0
        for s in range(DFF // STRIP):
            cols = (pl.ds(s * STRIP, STRIP), pl.ds(DFF + s * STRIP, STRIP))
            wk = [[w_ref[k:k + 1, c] for k in range(FFN_K)] for c in cols]
            bk = [b_ref[:, c] for c in cols]
            tails = tuple(jnp.where(first, 0.0, prev_ref[:, c].astype(F32)[HALO - 8:HALO]) for c in cols)

            def step(j, tails):
                r0 = pl.multiple_of(j * CHUNK, CHUNK)
                us, new_tails = [], []
                for h in range(2):
                    xv = cur_ref[pl.ds(r0, CHUNK), cols[h]].astype(F32)
                    us.append(_conv3_chunk(tails[h], xv, wk[h]) + bk[h])
                    u_ref[pl.ds(r0, CHUNK), cols[h]] = us[h].astype(u_ref.dtype)
                    new_tails.append(xv[CHUNK - 8:CHUNK])
                o_ref[pl.ds(r0, CHUNK), cols[0]] = (us[0] * _sigmoid(us[0]) * us[1]).astype(o_ref.dtype)
                return tuple(new_tails)

            lax.fori_loop(0, tm // CHUNK, step, tails, unroll=2)

    return pl.pallas_call(
        body, grid=(n // tm,),
        in_specs=[_prev_rows(tm, HALO, c2), _rows(tm, c2), _full((FFN_K, c2)), _full((1, c2))],
        out_specs=[_rows(tm, DFF), _rows(tm, c2)], out_shape=[S((n, DFF), BF), S((n, c2), BF)],
        compiler_params=_cparams("parallel"), name=name)(up, up, w, b)


def _ffn_act_bwd(up, u, dact, w, *, name):
    n = up.shape[0]
    tm = _tile(n, 512)
    nt = n // tm
    c2 = 2 * DFF

    def body(x_ref, uc_ref, un_ref, dc_ref, dn_ref, w_ref, o_ref, dw_ref, db_ref, dbuf, acc):
        i = pl.program_id(0)
        last = i == nt - 1

        @pl.when(i == 0)
        def _():
            acc[...] = jnp.zeros_like(acc)

        def swiglu_bwd(gate, val, dav):
            sg = _sigmoid(gate)
            return dav * val * (sg * (1.0 + gate * (1.0 - sg))), dav * gate * sg

        for s in range(DFF // STRIP):
            cols = (pl.ds(s * STRIP, STRIP), pl.ds(DFF + s * STRIP, STRIP))
            wk = [[w_ref[k:k + 1, c] for k in range(FFN_K)] for c in cols]

            def step1(j, carry):
                r0 = pl.multiple_of(j * CHUNK, CHUNK)
                rows = pl.ds(r0, CHUNK)
                dus = swiglu_bwd(uc_ref[rows, cols[0]].astype(F32), uc_ref[rows, cols[1]].astype(F32),
                                 dc_ref[rows, cols[0]].astype(F32))
                for h in range(2):
                    dbuf[rows, cols[h]] = dus[h]
                return carry

            lax.fori_loop(0, tm // CHUNK, step1, 0, unroll=2)
            dus = swiglu_bwd(un_ref[:, cols[0]].astype(F32), un_ref[:, cols[1]].astype(F32),
                             jnp.where(last, 0.0, dn_ref[:, cols[0]].astype(F32)))
            for h in range(2):
                dbuf[tm:tm + HALO, cols[h]] = dus[h]

            def step2(j, carry):
                r0 = pl.multiple_of(j * CHUNK, CHUNK)
                rows = pl.ds(r0, CHUNK)
                for h in range(2):
                    win = dbuf[pl.ds(r0, CHUNK + 8), cols[h]]
                    blocks = _blocks8(win)
                    d0 = win[0:CHUNK]
                    d1 = jnp.concatenate(_advance_rows(blocks, 1), axis=0)
                    d2 = jnp.concatenate(_advance_rows(blocks, 2), axis=0)
                    o_ref[rows, cols[h]] = (wk[h][2] * d0 + wk[h][1] * d1 + wk[h][0] * d2).astype(o_ref.dtype)
                    xv = x_ref[rows, cols[h]].astype(F32)
                    acc[2, :, cols[h]] += d0 * xv
                    acc[1, :, cols[h]] += d1 * xv
                    acc[0, :, cols[h]] += d2 * xv
                    acc[FFN_K, :, cols[h]] += d0
                return carry

            lax.fori_loop(0, tm // CHUNK, step2, 0)

        @pl.when(last)
        def _():
            for k in range(FFN_K):
                dw_ref[k:k + 1, :] = _colsum(acc[k])
            db_ref[...] = _colsum(acc[FFN_K])

    return pl.pallas_call(
        body, grid=(nt,),
        in_specs=[_rows(tm, c2), _rows(tm, c2), _next_rows(tm, HALO, c2, n),
                  _rows(tm, DFF), _next_rows(tm, HALO, DFF, n), _full((FFN_K, c2))],
        out_specs=[_rows(tm, c2), _full((FFN_K, c2)), _full((1, c2))],
        out_shape=[S((n, c2), BF), S((FFN_K, c2), F32), S((1, c2), F32)],
        scratch_shapes=[pltpu.VMEM((tm + HALO, c2), F32), pltpu.VMEM((FFN_K + 1, CHUNK, c2), F32)],
        compiler_params=_cparams("arbitrary"), name=name)(up, u, u, dact, dact, w)


def _head_consts():
    lane = jnp.arange(INNER) // HEAD_DIM
    rep = (jnp.arange(128)[:, None] == lane[None, :]).astype(BF)
    return rep, rep.T


def _split_dot(v, m):
    hi = v.astype(BF)
    lo = (v - hi.astype(F32)).astype(BF)
    return _dot(hi, m) + _dot(lo, m)


def _chunk_decay_terms(dt_raw, dtb, alog, rep):
    row = lax.broadcasted_iota(jnp.int32, (Q, Q), 0)
    col = lax.broadcasted_iota(jnp.int32, (Q, Q), 1)
    lower = row >= col
    upper = col >= row
    dt = _softplus(dt_raw + dtb)
    a = -jnp.exp(alog)
    adt = dt * a
    acum = lax.dot_general(lower.astype(F32), adt, NN, precision=HIGHEST, preferred_element_type=F32)
    acum_t = lax.dot_general(adt, upper.astype(F32), TN, precision=HIGHEST, preferred_element_type=F32)
    alast = acum[Q - 1:Q, :]
    e = jnp.exp(acum)
    f = jnp.exp(alast - acum)
    ex = _split_dot(jnp.concatenate([dt, e, f, jnp.broadcast_to(jnp.exp(alast), (8, 128))], axis=0), rep)
    return dict(lower=lower, upper=upper, dt=dt, a=a, acum=acum, acum_t=acum_t, alast=alast,
                dt_x=ex[0:Q], e_x=ex[Q:2 * Q], f_x=ex[2 * Q:3 * Q], cd_x=ex[3 * Q:3 * Q + 1])


def _block_diag2(v, lo):
    return jnp.concatenate([jnp.where(lo, v, 0.0), jnp.where(lo, 0.0, v)], axis=0).astype(BF)


def _ssd_fwd(xbc_act, proj, dt_bias, a_log, d_x, norm_g, *, name):
    n = xbc_act.shape[0]
    nc = n // Q
    rep, _ = _head_consts()

    def body(xs_ref, bc_ref, dt_ref, z_ref, dtb_ref, alog_ref, dx_ref, ng_ref, rep_ref, y_ref, yn_ref, hp_ref,
             h_scr, y_scr):
        i = pl.program_id(0)

        @pl.when(i == 0)
        def _():
            h_scr[...] = jnp.zeros_like(h_scr)

        hp_ref[...] = h_scr[...]
        t = _chunk_decay_terms(dt_ref[...].astype(F32), dtb_ref[...], alog_ref[...], rep_ref[...])
        xs = xs_ref[...].astype(F32)
        xt = xs * t["dt_x"]
        lo = lax.broadcasted_iota(jnp.int32, (Q, 128), 1) < HEAD_DIM
        gw = INNER // GROUPS
        for g in range(GROUPS):
            bm = bc_ref[:, g * NSTATE:(g + 1) * NSTATE]
            cm = bc_ref[:, GROUPS * NSTATE + g * NSTATE:GROUPS * NSTATE + (g + 1) * NSTATE]
            cb = _dot(cm, bm, NT)
            hg = h_scr[:, g * gw:(g + 1) * gw]
            yoff = _dot(cm, hg.astype(BF))
            for jj in range(gw // 128):
                p = g * (gw // 128) + jj
                sl = slice(p * 128, (p + 1) * 128)
                ws = []
                for hd in (2 * p, 2 * p + 1):
                    seg = t["acum"][:, hd:hd + 1] - t["acum_t"][hd:hd + 1, :]
                    ws.append((cb * jnp.exp(jnp.where(t["lower"], seg, -jnp.inf))).astype(BF))
                ydiag = _dot(jnp.concatenate(ws, axis=1), _block_diag2(xt[:, sl], lo))
                y_scr[:, sl] = ydiag + yoff[:, jj * 128:(jj + 1) * 128] * t["e_x"][:, sl] + dx_ref[:, sl] * xs[:, sl]
            xf = (xt[:, g * gw:(g + 1) * gw] * t["f_x"][:, g * gw:(g + 1) * gw]).astype(BF)
            h_scr[:, g * gw:(g + 1) * gw] = hg * t["cd_x"][:, g * gw:(g + 1) * gw] + _dot(bm, xf, TN)
        y = y_scr[...]
        y_ref[...] = y.astype(y_ref.dtype)
        z = z_ref[...].astype(F32)
        v = y * z * _sigmoid(z)
        for g in range(GROUPS):
            vg = v[:, g * gw:(g + 1) * gw]
            r = lax.rsqrt(jnp.mean(vg * vg, axis=-1, keepdims=True) + EPS)
            yn_ref[:, g * gw:(g + 1) * gw] = (vg * r * ng_ref[:, g * gw:(g + 1) * gw]).astype(yn_ref.dtype)

    vec = _full((1, INNER))
    hv = _full((1, 128))
    return pl.pallas_call(
        body, grid=(nc,),
        in_specs=[_rows(Q, INNER, 0), _rows(Q, 2 * GROUPS * NSTATE, INNER // (2 * GROUPS * NSTATE)),
                  _rows(Q, DT_PAD, P_DT // DT_PAD), _rows(Q, INNER, P_Z // INNER),
                  hv, hv, vec, vec, _full((128, INNER))],
        out_specs=[_rows(Q, INNER), _rows(Q, INNER), pl.BlockSpec((None, NSTATE, INNER), lambda i: (i, 0, 0))],
        out_shape=[S((n, INNER), BF), S((n, INNER), BF), S((nc, NSTATE, INNER), F32)],
        scratch_shapes=[pltpu.VMEM((NSTATE, INNER), F32), pltpu.VMEM((Q, INNER), F32)],
        compiler_params=_cparams("arbitrary"), name=name)(xbc_act, xbc_act, proj, proj, dt_bias, a_log, d_x, norm_g, rep)


def _ssd_bwd(xbc_act, proj, y, dyn, hprev, dproj, dt_bias, a_log, d_x, norm_g, *, name):
    n = xbc_act.shape[0]
    nc = n // Q
    rep, sel = _head_consts()
    gw = INNER // GROUPS

    def rev(w, cb=0):
        return pl.BlockSpec((Q, w), lambda i: (nc - 1 - i, cb))

    def body(xs_ref, bc_ref, dt_ref, z_ref, y_ref, dyn_ref, hp_ref, dtb_ref, alog_ref, dx_ref, ng_ref, rep_ref,
             sel_ref, dproj_in, dz_ref, ddt_ref, dxbc_ref, dng_ref, ddtb_ref, dalog_ref, dd_ref,
             dh_scr, dxt_scr, st_scr, off_scr, rs_scr, cs_scr, dng_acc, ddtb_acc, da_acc, dd_acc):
        del dproj_in
        i = pl.program_id(0)

        @pl.when(i == 0)
        def _():
            for r in (dh_scr, dng_acc, ddtb_acc, da_acc, dd_acc):
                r[...] = jnp.zeros_like(r)

        y = y_ref[...].astype(F32)
        z = z_ref[...].astype(F32)
        sz = _sigmoid(z)
        silu = z * sz
        v = y * silu
        dyn = dyn_ref[...].astype(F32)
        dvs = []
        for g in range(GROUPS):
            gs = slice(g * gw, (g + 1) * gw)
            vg = v[:, gs]
            r = lax.rsqrt(jnp.mean(vg * vg, axis=-1, keepdims=True) + EPS)
            vn = vg * r
            dvn = dyn[:, gs] * ng_ref[:, gs]
            dng_acc[:, gs] += _colsum(dyn[:, gs] * vn)
            dvs.append(r * (dvn - vn * jnp.mean(dvn * vn, axis=-1, keepdims=True)))
        dv = jnp.concatenate(dvs, axis=1)
        dy = dv * silu
        dz_ref[...] = (dv * y * (sz * (1.0 + z * (1.0 - sz)))).astype(dz_ref.dtype)

        dt_raw = dt_ref[...].astype(F32)
        t = _chunk_decay_terms(dt_raw, dtb_ref[...], alog_ref[...], rep_ref[...])
        xs = xs_ref[...].astype(F32)
        dsk = dx_ref[...]
        dd_acc[...] += _colsum(dy * xs)
        xt = xs * t["dt_x"]
        dye = dy * t["e_x"]
        xtf = xt * t["f_x"]
        hp = hp_ref[...]
        dh = dh_scr[...]
        lo = lax.broadcasted_iota(jnp.int32, (Q, 128), 1) < HEAD_DIM
        rs_scr[...] = jnp.zeros_like(rs_scr)
        cs_scr[...] = jnp.zeros_like(cs_scr)
        for g in range(GROUPS):
            gs = slice(g * gw, (g + 1) * gw)
            bm = bc_ref[:, g * NSTATE:(g + 1) * NSTATE]
            cm = bc_ref[:, GROUPS * NSTATE + g * NSTATE:GROUPS * NSTATE + (g + 1) * NSTATE]
            cbt = _dot(bm, cm, NT)
            dhg = dh[:, gs].astype(BF)
            hpg = hp[:, gs].astype(BF)
            dxt_state = _dot(bm, dhg) * t["f_x"][:, gs]
            st_scr[:, gs] = dxt_state
            dye_g = dye[:, gs]
            off_scr[:, gs] = dye_g * _dot(cm, hpg)
            dye_b = dye_g.astype(BF)
            db = _dot(xtf[:, gs].astype(BF), dhg, NT)
            dc = _dot(dye_b, hpg, NT)
            dh_scr[:, gs] = t["cd_x"][:, gs] * dh[:, gs] + _dot(cm, dye_b, TN)
            dcbt = jnp.zeros((Q, Q), F32)
            for jj in range(gw // 128):
                p = g * (gw // 128) + jj
                sl = slice(p * 128, (p + 1) * 128)
                lts, wfs = [], []
                for hd in (2 * p, 2 * p + 1):
                    seg_t = t["acum_t"][hd:hd + 1, :] - t["acum"][:, hd:hd + 1]
                    lt = jnp.exp(jnp.where(t["upper"], seg_t, -jnp.inf))
                    lts.append(lt)
                    wfs.append(cbt * lt)
                dyp = dy[:, sl]
                dxt_diag = _dot(jnp.concatenate([w.astype(BF) for w in wfs], axis=1), _block_diag2(dyp, lo))
                dwt2 = _dot(_block_diag2(xt[:, sl], lo), dyp.astype(BF), NT)
                for k, hd in enumerate((2 * p, 2 * p + 1)):
                    dwt = dwt2[k * Q:(k + 1) * Q]
                    dcbt = dcbt + dwt * lts[k]
                    mt = dwt * wfs[k]
                    rs_scr[hd:hd + 1, :] = _colsum(mt)
                    cs_scr[:, hd:hd + 1] = jnp.sum(mt, axis=1, keepdims=True)
                dxt_scr[:, sl] = dxt_diag + dxt_state[:, jj * 128:(jj + 1) * 128]
            dcbt_b = dcbt.astype(BF)
            db = db + _dot(dcbt_b, cm)
            dc = dc + _dot(dcbt_b, bm, TN)
            dxbc_ref[:, INNER + g * NSTATE:INNER + (g + 1) * NSTATE] = db.astype(dxbc_ref.dtype)
            dxbc_ref[:, INNER + (GROUPS + g) * NSTATE:INNER + (GROUPS + g + 1) * NSTATE] = dc.astype(dxbc_ref.dtype)
        dxt = dxt_scr[...]
        dst = st_scr[...]
        sel_m = sel_ref[...]
        sums = _split_dot(jnp.concatenate([off_scr[...], xs * dst, xs * dxt], axis=0), sel_m)
        r1_off, r3_state, r3 = sums[0:Q], sums[Q:2 * Q], sums[2 * Q:3 * Q]
        t1 = _colsum(xt * dst)
        t2 = _colsum(dh * hp)
        tails = _split_dot(jnp.concatenate([jnp.broadcast_to(t1, (8, INNER)), jnp.broadcast_to(t2, (8, INNER))], axis=0),
                           sel_m)
        extra = tails[0:1] + jnp.exp(t["alast"]) * tails[8:9]
        last_row = lax.broadcasted_iota(jnp.int32, (Q, 128), 0) == Q - 1
        da_cum = (rs_scr[...].T - cs_scr[...]) + r1_off - t["dt"] * r3_state + jnp.where(last_row, extra, 0.0)
        dadt = lax.dot_general(t["upper"].astype(F32), da_cum, NN, precision=HIGHEST, preferred_element_type=F32)
        ddt = r3 + t["a"] * dadt
        da_acc[...] += _colsum(dadt * t["dt"])
        real = lax.broadcasted_iota(jnp.int32, (Q, 128), 1) < HEADS
        ddraw = jnp.where(real, ddt * _sigmoid(dt_raw + dtb_ref[...]), 0.0)
        ddt_ref[...] = ddraw.astype(ddt_ref.dtype)
        ddtb_acc[...] += _colsum(ddraw)
        dxbc_ref[:, 0:INNER] = (dy * dsk + dxt * t["dt_x"]).astype(dxbc_ref.dtype)

        @pl.when(i == nc - 1)
        def _():
            dng_ref[...] = dng_acc[...]
            ddtb_ref[...] = ddtb_acc[...]
            dalog_ref[...] = da_acc[...] * t["a"]
            dd_ref[...] = _split_dot(jnp.broadcast_to(dd_acc[...], (8, INNER)), sel_m)[0:1]

    vec = _full((1, INNER))
    hv = _full((1, 128))
    return pl.pallas_call(
        body, grid=(nc,),
        in_specs=[rev(INNER, 0), rev(2 * GROUPS * NSTATE, INNER // (2 * GROUPS * NSTATE)),
                  rev(DT_PAD, P_DT // DT_PAD), rev(INNER, P_Z // INNER), rev(INNER), rev(INNER),
                  pl.BlockSpec((None, NSTATE, INNER), lambda i: (nc - 1 - i, 0, 0)),
                  hv, hv, vec, vec, _full((128, INNER)), _full((INNER, 128)), pl.BlockSpec(memory_space=pl.ANY)],
        out_specs=[rev(INNER, P_Z // INNER), rev(DT_PAD), rev(XBC), vec, hv, hv, hv],
        out_shape=[S(dproj.shape, dproj.dtype), S((n, DT_PAD), BF), S((n, XBC), BF),
                   S((1, INNER), F32), S((1, 128), F32), S((1, 128), F32), S((1, 128), F32)],
        scratch_shapes=[pltpu.VMEM((NSTATE, INNER), F32), pltpu.VMEM((Q, INNER), F32), pltpu.VMEM((Q, INNER), F32),
                        pltpu.VMEM((Q, INNER), F32), pltpu.VMEM((128, Q), F32), pltpu.VMEM((Q, 128), F32),
                        pltpu.VMEM((1, INNER), F32), pltpu.VMEM((1, 128), F32), pltpu.VMEM((1, 128), F32),
                        pltpu.VMEM((1, INNER), F32)],
        input_output_aliases={13: 0},
        compiler_params=_cparams("arbitrary"), name=name)(
            xbc_act, xbc_act, proj, proj, y, dyn, hprev, dt_bias, a_log, d_x, norm_g, rep, sel, dproj)


def _mixer_out_fwd(yn, a_conf, a_sc, proj, b_gate, w_ssd, w_conf, w_sc, w_o, x, mod, *, name):
    n = x.shape[0]
    tm = _tile(n, 256)

    def body(yn_ref, ac_ref, as_ref, gt_ref, bg_ref, wa_ref, wb_ref, wc_ref, wo_ref, x_ref, mod_ref,
             ya_ref, yb_ref, yc_ref, mg_ref, mix_ref, xn_ref):
        ya = _dot(yn_ref[...], wa_ref[...])
        yb = _dot(ac_ref[...], wb_ref[...])
        yc = _dot(as_ref[...], wc_ref[...])
        ya_ref[...] = ya.astype(ya_ref.dtype)
        yb_ref[...] = yb.astype(yb_ref.dtype)
        yc_ref[...] = yc.astype(yc_ref.dtype)
        g = _sigmoid(gt_ref[...].astype(F32) + bg_ref[...])
        merged = (g[:, 0:D] * ya + g[:, D:2 * D] * yb + g[:, 2 * D:] * yc).astype(mg_ref.dtype)
        mg_ref[...] = merged
        mix = _dot(merged, wo_ref[...])
        mix_ref[...] = mix.astype(mix_ref.dtype)
        xn_ref[...] = x_ref[...] + mod_ref[2:3, :] * mix

    act = S((n, D), BF)
    return pl.pallas_call(
        body, grid=(n // tm,),
        in_specs=[_rows(tm, INNER), _rows(tm, CONF_W), _rows(tm, SC_W), _rows(tm, 3 * D, P_GATES // (3 * D)),
                  _full((1, 3 * D)), _full((INNER, D)), _full((CONF_W, D)), _full((SC_W, D)), _full((D, D)),
                  _rows(tm, D), _full((3, D))],
        out_specs=[_rows(tm, D)] * 6,
        out_shape=[act, act, act, act, act, S((n, D), F32)],
        compiler_params=_cparams("parallel"), name=name)(yn, a_conf, a_sc, proj, b_gate, w_ssd, w_conf, w_sc, w_o, x, mod)


def _mixer_out_bwd(dx, mix, ya, yb, yc, proj, b_gate, w_ssd, w_conf, w_sc, w_o, mod, *, name):
    n = dx.shape[0]
    tm = _tile(n, 256)

    def body(dx_ref, mix_ref, ya_ref, yb_ref, yc_ref, gt_ref, bg_ref, wa_ref, wb_ref, wc_ref, wo_ref, mod_ref,
             do_ref, dya_ref, dyb_ref, dyc_ref, dgt_ref, dyn_ref, dac_ref, das_ref, dgm_ref, dbg_ref):
        i = pl.program_id(0)
        dxv = dx_ref[...]
        do = (dxv * mod_ref[2:3, :]).astype(BF)
        do_ref[...] = do
        dm = _dot(do, wo_ref[...], NT)
        g = _sigmoid(gt_ref[...].astype(F32) + bg_ref[...])
        @pl.when(i == 0)
        def _():
            dgm_ref[...] = jnp.zeros_like(dgm_ref)
            dbg_ref[...] = jnp.zeros_like(dbg_ref)

        dys = []
        for j, (y_ref, o_ref) in enumerate(((ya_ref, dya_ref), (yb_ref, dyb_ref), (yc_ref, dyc_ref))):
            gj = g[:, j * D:(j + 1) * D]
            dyj = (dm * gj).astype(BF)
            o_ref[...] = dyj
            dys.append(dyj)
            dgpre = dm * y_ref[...].astype(F32) * gj * (1.0 - gj)
            dgt_ref[:, j * D:(j + 1) * D] = dgpre.astype(dgt_ref.dtype)
            dbg_ref[:, j * D:(j + 1) * D] += _colsum(dgpre)
        dyn_ref[...] = _dot(dys[0], wa_ref[...], NT).astype(dyn_ref.dtype)
        dac_ref[...] = _dot(dys[1], wb_ref[...], NT).astype(dac_ref.dtype)
        das_ref[...] = _dot(dys[2], wc_ref[...], NT).astype(das_ref.dtype)
        dgm_ref[...] += _colsum(dxv * mix_ref[...].astype(F32))

    act = S((n, D), BF)
    return pl.pallas_call(
        body, grid=(n // tm,),
        in_specs=[_rows(tm, D)] * 5 + [_rows(tm, 3 * D, P_GATES // (3 * D)), _full((1, 3 * D)), _full((INNER, D)),
                                       _full((CONF_W, D)), _full((SC_W, D)), _full((D, D)), _full((3, D))],
        out_specs=[_rows(tm, D)] * 4 + [_rows(tm, 3 * D, P_GATES // (3 * D)), _rows(tm, INNER), _rows(tm, CONF_W),
                                        _rows(tm, SC_W), _full((1, D)), _full((1, 3 * D))],
        out_shape=[act, act, act, act, S((n, PW), BF), S((n, INNER), BF), S((n, CONF_W), BF), S((n, SC_W), BF),
                   S((1, D), F32), S((1, 3 * D), F32)],
        compiler_params=_cparams("arbitrary"), name=name)(dx, mix, ya, yb, yc, proj, b_gate, w_ssd, w_conf, w_sc, w_o, mod)


def _pad_w_in(w):
    zeros = jnp.zeros((w.shape[0], XBC_PAD - XBC - (R_CONF - R_DT)), w.dtype)
    return jnp.concatenate([w[:, R_GATES:], w[:, R_SC:R_GATES], w[:, R_XBC:R_DT], w[:, R_DT:R_CONF], zeros,
                            w[:, R_Z:R_XBC], w[:, R_CONF:R_SC]], axis=1)


def _unpad_w_in(wp):
    return jnp.concatenate([wp[:, P_Z:P_Z + INNER], wp[:, P_XBC:P_XBC + XBC], wp[:, P_DT:P_DT + HEADS],
                            wp[:, P_CONF:P_CONF + 2 * CONF_W], wp[:, P_SC:P_SC + 3 * SC_W], wp[:, P_GATES:P_GATES + 3 * D]],
                           axis=1)


W_IN_SHARD = N_IN // N_CHIPS
W_IN_SEGMENTS = ((R_Z, R_XBC, P_Z), (R_XBC, R_DT, P_XBC), (R_DT, R_CONF, P_DT), (R_CONF, R_SC, P_CONF),
                 (R_SC, R_GATES, P_SC), (R_GATES, N_IN, P_GATES))


def _pad_w_in_chips(w4):
    parts = []
    for lo, hi, dst in sorted(W_IN_SEGMENTS, key=lambda sgm: sgm[2]):
        for k in range(N_CHIPS):
            a, b = max(lo, k * W_IN_SHARD), min(hi, (k + 1) * W_IN_SHARD)
            if a < b:
                parts.append((dst + a - lo, w4[k][:, a - k * W_IN_SHARD:b - k * W_IN_SHARD]))
    out, pos = [], 0
    for start, piece in parts:
        if start > pos:
            out.append(jnp.zeros((w4.shape[1], start - pos), w4.dtype))
        out.append(piece)
        pos = start + piece.shape[1]
    if pos < PW:
        out.append(jnp.zeros((w4.shape[1], PW - pos), w4.dtype))
    return jnp.concatenate(out, axis=1)


def _unpad_w_in_chips(wp):
    blocks = []
    for k in range(N_CHIPS):
        pieces = []
        for lo, hi, dst in W_IN_SEGMENTS:
            a, b = max(lo, k * W_IN_SHARD), min(hi, (k + 1) * W_IN_SHARD)
            if a < b:
                pieces.append(wp[:, dst + a - lo:dst + b - lo])
        blocks.append(jnp.concatenate(pieces, axis=1))
    return jnp.stack(blocks)


def _row(v):
    return v.reshape(1, -1)


def _head_row(v):
    return jnp.pad(v, (0, 128 - HEADS)).reshape(1, 128)


def _layer_fwd(x, p, mod_mix, mod_ffn, tag):
    sv = {"x0": x}
    h = _modnorm_fwd(x, _row(p["norm_mix_g"]), mod_mix, name=f"modnorm_mix_fwd{tag}")
    proj = _mm_nn(h, p["w_in_pad"], out_dtype=BF, tm=1024, tn=2048, name=f"proj_fwd{tag}")
    xbc_act = _ssd_pre_fwd(proj, p["ssd_conv_w"], _row(p["ssd_conv_b"]), name=f"ssd_pre_fwd{tag}")
    d_x = _row(jnp.repeat(p["ssd_d"], HEAD_DIM))
    y, yn, hprev = _ssd_fwd(xbc_act, proj, _head_row(p["ssd_dt_bias"]), _head_row(p["ssd_a_log"]), d_x,
                            _row(p["ssd_norm_g"]), name=f"ssd_fwd{tag}")
    a_conf, uc = _conf_fwd(proj, p["conf_conv_w"], _row(p["conf_conv_b"]), _row(p["conf_ln_g"]), _row(p["conf_ln_b"]),
                           name=f"conf_fwd{tag}")
    a_sc = _sc_fwd(proj, p["sc_conv_w"], name=f"sc_fwd{tag}")
    ya, yb, yc, merged, mix, x1 = _mixer_out_fwd(yn, a_conf, a_sc, proj, _row(p["b_gate"]), p["w_ssd_out"],
                                                 p["w_conf_out"], p["w_sc_out"], p["w_o"], x, mod_mix,
                                                 name=f"mixer_out_fwd{tag}")
    h2 = _modnorm_fwd(x1, _row(p["norm_ffn_g"]), mod_ffn, name=f"modnorm_ffn_fwd{tag}")
    up = _mm_nn_chips(h2, p["w_up4"], out_dtype=BF, tm=1024, name=f"up_fwd{tag}")
    act, u_ffn = _ffn_act_fwd(up, p["ffn_conv_w"], _row(p["ffn_conv_b"]), name=f"ffn_act_fwd{tag}")
    o, x2 = _mm_resid(act, p["w_down"], x1, mod_ffn, tm=512, name=f"down_fwd{tag}")
    sv.update(h=h, proj=proj, xbc_act=xbc_act, d_x=d_x, y=y, yn=yn, hprev=hprev, a_conf=a_conf, uc=uc, a_sc=a_sc,
              ya=ya, yb=yb, yc=yc, merged=merged, mix=mix, x1=x1, h2=h2, up=up, u_ffn=u_ffn, act=act, o=o)
    return x2, sv


def _layer_bwd(dx, p, sv, mod_mix, mod_ffn, tag):
    g = {}
    do2, dgate_ffn = _gate_bwd(dx, sv["o"], mod_ffn, name=f"gate_ffn_bwd{tag}")
    dact = _mm_nt(do2, p["w_down"], out_dtype=BF, tm=1024, tk=D, name=f"down_dx{tag}")
    g["w_down"] = _mm_tn(sv["act"], do2, tn=D, tk=512, name=f"down_dw{tag}")
    dup, g["ffn_conv_w"], dffn_b = _ffn_act_bwd(sv["up"], sv["u_ffn"], dact, p["ffn_conv_w"], name=f"ffn_act_bwd{tag}")
    g["ffn_conv_b"] = dffn_b[0]
    dh2 = _mm_nt_chips(dup, p["w_up4"], out_dtype=F32, tm=1024, name=f"up_dx{tag}")
    g["w_up4"] = _mm_tn(sv["h2"], dup, tn=2 * DFF // N_CHIPS, tk=512, by_chip=True, name=f"up_dw{tag}")
    dx1, dgn, dsh, dsc = _modnorm_bwd(dh2, sv["x1"], dx, _row(p["norm_ffn_g"]), mod_ffn, name=f"modnorm_ffn_bwd{tag}")
    g["norm_ffn_g"] = dgn[0]
    dmod_ffn = jnp.concatenate([dsh[0], dsc[0], dgate_ffn[0]])

    (do1, dya, dyb, dyc, dproj, dyn, dac, dasc, dgate_mix, dbg) = _mixer_out_bwd(
        dx1, sv["mix"], sv["ya"], sv["yb"], sv["yc"], sv["proj"], _row(p["b_gate"]), p["w_ssd_out"], p["w_conf_out"],
        p["w_sc_out"], p["w_o"], mod_mix, name=f"mixer_out_bwd{tag}")
    g["b_gate"] = dbg[0]
    g["w_o"] = _mm_tn(sv["merged"], do1, tn=D, tk=512, name=f"wo_dw{tag}")
    g["w_ssd_out"] = _mm_tn(sv["yn"], dya, tn=D, tk=512, name=f"wssd_dw{tag}")
    g["w_conf_out"] = _mm_tn(sv["a_conf"], dyb, tn=D, tk=512, name=f"wconf_dw{tag}")
    g["w_sc_out"] = _mm_tn(sv["a_sc"], dyc, tn=D, tk=512, name=f"wsc_dw{tag}")
    dproj, g["conf_conv_w"], dcb, dlg, dlb = _conf_bwd(sv["proj"], sv["uc"], dac, dproj, p["conf_conv_w"],
                                                      _row(p["conf_ln_g"]), _row(p["conf_ln_b"]), name=f"conf_bwd{tag}")
    g["conf_conv_b"], g["conf_ln_g"], g["conf_ln_b"] = dcb[0], dlg[0], dlb[0]
    dproj, g["sc_conv_w"] = _sc_bwd(sv["proj"], dasc, dproj, p["sc_conv_w"], name=f"sc_bwd{tag}")
    dproj, ddt, dxbc_act, dng, ddtb, dalog, ddd = _ssd_bwd(
        sv["xbc_act"], sv["proj"], sv["y"], dyn, sv["hprev"], dproj, _head_row(p["ssd_dt_bias"]),
        _head_row(p["ssd_a_log"]), sv["d_x"], _row(p["ssd_norm_g"]), name=f"ssd_bwd{tag}")
    g["ssd_norm_g"], g["ssd_dt_bias"], g["ssd_a_log"], g["ssd_d"] = dng[0], ddtb[0, :HEADS], dalog[0, :HEADS], ddd[0, :HEADS]
    dproj, g["ssd_conv_w"], dsb = _ssd_pre_bwd(sv["proj"], dxbc_act, ddt, dproj, p["ssd_conv_w"], _row(p["ssd_conv_b"]),
                                               name=f"ssd_pre_bwd{tag}")
    g["ssd_conv_b"] = dsb[0]
    dh = _mm_nt(dproj, p["w_in_pad"], out_dtype=F32, tm=1024, tk=2048, name=f"proj_dx{tag}")
    g["w_in_pad"] = _mm_tn(sv["h"], dproj, tn=2048, tk=512, name=f"proj_dw{tag}")
    dx0, dgn, dsh, dsc = _modnorm_bwd(dh, sv["x0"], dx1, _row(p["norm_mix_g"]), mod_mix, name=f"modnorm_mix_bwd{tag}")
    g["norm_mix_g"] = dgn[0]
    dmod_mix = jnp.concatenate([dsh[0], dsc[0], dgate_mix[0]])
    return dx0, g, dmod_mix, dmod_ffn


def _local_step(x, target, layers, mods, final_norm_g):
    saved = []
    for i, p in enumerate(layers):
        x, sv = _layer_fwd(x, p, mods[i][0], mods[i][1], f"_l{i}")
        saved.append(sv)
    dx, loss, dfg = _final_loss(x, _row(final_norm_g), target, name="final_loss")
    grads, dmods = [None] * len(layers), [None] * len(layers)
    for i in reversed(range(len(layers))):
        dx, grads[i], dmm, dmf = _layer_bwd(dx, layers[i], saved[i], mods[i][0], mods[i][1], f"_l{i}")
        dmods[i] = (dmm, dmf)
    return loss, dx, grads, dmods, dfg[0]


MESH = pl.DeviceIdType.MESH
ANY = pl.BlockSpec(memory_space=pl.ANY)
VMEM = pl.BlockSpec(memory_space=pltpu.VMEM)


def _mesh_pos():
    return lax.axis_index("x"), lax.axis_index("y"), lax.axis_index("c")


def _peer(pos, mask):
    return tuple(1 - v if (mask >> (2 - k)) & 1 else v for k, v in enumerate(pos))


def _lin(pos):
    return 4 * pos[0] + 2 * pos[1] + pos[2]


def _chip(pos):
    return 2 * pos[0] + pos[1]


def _rcopy(src, dst, send_sem, recv_sem, dev):
    return pltpu.make_async_remote_copy(src_ref=src, dst_ref=dst, send_sem=send_sem, recv_sem=recv_sem,
                                        device_id=dev, device_id_type=MESH)


CHIP_MASKS = (2, 4, 6)
SIBLING = 1
ADA_COLS = 3 * D // N_CHIPS
CONV_ROWS, CONV_COLS = 48, 2 * DFF // N_CHIPS
CONV_PACK = {"ffn_conv_w": (0, FFN_K, 2 * DFF // N_CHIPS), "ssd_conv_w": (3, SSD_K, XBC // N_CHIPS),
             "conf_conv_w": (8, CONF_K, CONF_W // N_CHIPS), "sc_conv_w": (40, SC_K, SC_W // N_CHIPS)}


def _ada_exchange(c_blk, ada_mix_w, ada_ffn_w, conv_pack):
    def body(c_ref, wm_ref, wf_ref, cw_ref, mods_ref, sc_ref, cwall_ref,
             call_scr, modp_scr, recv_scr, s1, r1, s3, r3, s4, r4):
        pos = _mesh_pos()
        me, km = _lin(pos), _chip(pos)
        call_scr[me] = c_ref[...]
        cwall_ref[km] = cw_ref[...]
        sends = []
        for m in range(1, N_DEV):
            sends.append(_rcopy(c_ref, call_scr.at[me], s1.at[m - 1], r1.at[m - 1], _peer(pos, m)))
        for j, m in enumerate(CHIP_MASKS):
            sends.append(_rcopy(cw_ref, cwall_ref.at[km], s4.at[j], r4.at[j], _peer(pos, m)))
        for cp in sends:
            cp.start()
        for m in range(1, N_DEV):
            src = _peer(pos, m)
            _rcopy(c_ref, call_scr.at[_lin(src)], s1.at[m - 1], r1.at[m - 1], src).wait_recv()
        cm = jnp.concatenate([call_scr[d, 0:1, :] for d in range(N_DEV)], axis=0)
        sc = cm * _sigmoid(cm)
        sc_ref[...] = sc
        for j, w in enumerate((wm_ref.at[0], wm_ref.at[1], wf_ref.at[0], wf_ref.at[1])):
            modp_scr[:, j * ADA_COLS:(j + 1) * ADA_COLS] = lax.dot_general(
                sc, w[...], NN, precision=HIGHEST, preferred_element_type=F32)
        recv_scr[km] = modp_scr[...]
        sends3 = [_rcopy(modp_scr, recv_scr.at[km], s3.at[j], r3.at[j], _peer(pos, m)) for j, m in enumerate(CHIP_MASKS)]
        for cp in sends3:
            cp.start()
        for j, m in enumerate(CHIP_MASKS):
            src = _peer(pos, m)
            _rcopy(modp_scr, recv_scr.at[_chip(src)], s3.at[j], r3.at[j], src).wait_recv()
            _rcopy(cw_ref, cwall_ref.at[_chip(src)], s4.at[j], r4.at[j], src).wait_recv()
        for k in range(N_CHIPS):
            mods_ref[k:k + 1, :] = recv_scr[k, pl.ds(me, 1), :]
        for cp in sends + sends3:
            cp.wait_send()

    dma = pltpu.SemaphoreType.DMA
    return pl.pallas_call(
        body, in_specs=[VMEM] * 4, out_specs=[VMEM] * 3,
        out_shape=[S((N_CHIPS, 4 * ADA_COLS), F32), S((N_DEV, D), F32), S((N_CHIPS,) + conv_pack.shape, F32)],
        scratch_shapes=[pltpu.VMEM((N_DEV, 8, D), F32), pltpu.VMEM((N_DEV, 4 * ADA_COLS), F32),
                        pltpu.VMEM((N_CHIPS, N_DEV, 4 * ADA_COLS), F32),
                        dma((N_DEV - 1,)), dma((N_DEV - 1,)), dma((3,)), dma((3,)), dma((3,)), dma((3,))],
        compiler_params=pltpu.CompilerParams(vmem_limit_bytes=VMEM_LIMIT_V7X), name="ada_exchange")(
            c_blk, ada_mix_w, ada_ffn_w, conv_pack)


def _gather_weights(shards):
    na = len(shards)

    def body(*refs):
        ins, outs = refs[:na], refs[na:2 * na]
        ssem, rsem, fsend, frecv = refs[2 * na:]
        pos = _mesh_pos()
        c, km = pos[2], _chip(pos)
        sib = _peer(pos, SIBLING)
        sends = []
        for a in range(na):
            for j, m in enumerate(CHIP_MASKS):
                sends.append(_rcopy(ins[a].at[c], outs[a].at[c, km], ssem.at[a, j], rsem.at[a, j], _peer(pos, m)))
        for cp in sends:
            cp.start()
        passed = []
        for a in range(na):
            for j, m in enumerate(CHIP_MASKS):
                src = _peer(pos, m)
                blk = outs[a].at[c, _chip(src)]
                _rcopy(ins[a].at[c], blk, ssem.at[a, j], rsem.at[a, j], src).wait_recv()
                fw = _rcopy(blk, blk, fsend.at[a, j], frecv.at[a, j], sib)
                fw.start()
                passed.append(fw)
        for a in range(na):
            for j, m in enumerate(CHIP_MASKS):
                blk = outs[a].at[1 - c, _chip(_peer(pos, m))]
                _rcopy(blk, blk, fsend.at[a, j], frecv.at[a, j], sib).wait_recv()
        for cp in sends + passed:
            cp.wait_send()

    dma = pltpu.SemaphoreType.DMA
    return pl.pallas_call(
        body, in_specs=[ANY] * na, out_specs=[ANY] * na,
        out_shape=[S((DEPTH, N_CHIPS) + s.shape[1:], s.dtype) for s in shards],
        scratch_shapes=[dma((na, 3)), dma((na, 3)), dma((na, 3)), dma((na, 3))],
        name="gather_weights")(*shards)


def _swap_half(arrs):
    na = len(arrs)

    def body(*refs):
        ins, outs = refs[:na], refs[na:2 * na]
        ssem, rsem = refs[2 * na:]
        pos = _mesh_pos()
        c = pos[2]
        sib = _peer(pos, SIBLING)
        cps = [_rcopy(ins[a].at[1 - c], outs[a], ssem.at[a], rsem.at[a], sib) for a in range(na)]
        for cp in cps:
            cp.start()
        for cp in cps:
            cp.wait()

    dma = pltpu.SemaphoreType.DMA
    return pl.pallas_call(
        body, in_specs=[ANY] * na, out_specs=[ANY] * na,
        out_shape=[S(s.shape[1:], s.dtype) for s in arrs],
        scratch_shapes=[dma((na,)), dma((na,))], name="swap_half")(*arrs)


def _scatter_chips(arrs):
    na = len(arrs)

    def body(*refs):
        ins, outs = refs[:na], refs[na:2 * na]
        ssem, rsem = refs[2 * na:]
        pos = _mesh_pos()
        km = _chip(pos)
        sends = []
        for a in range(na):
            for j, m in enumerate(CHIP_MASKS):
                dst = _peer(pos, m)
                sends.append(_rcopy(ins[a].at[_chip(dst)], outs[a].at[km], ssem.at[a, j], rsem.at[a, j], dst))
        for cp in sends:
            cp.start()
        for a in range(na):
            for j, m in enumerate(CHIP_MASKS):
                src = _peer(pos, m)
                _rcopy(ins[a].at[km], outs[a].at[_chip(src)], ssem.at[a, j], rsem.at[a, j], src).wait_recv()
        for cp in sends:
            cp.wait_send()

    dma = pltpu.SemaphoreType.DMA
    return pl.pallas_call(
        body, in_specs=[ANY] * na, out_specs=[ANY] * na, out_shape=[S(s.shape, s.dtype) for s in arrs],
        scratch_shapes=[dma((na, 3)), dma((na, 3))], name="scatter_chips")(*arrs)


def _share_sibling(arrs):
    na = len(arrs)

    def body(*refs):
        bufs = refs[na:2 * na]
        ssem, rsem = refs[2 * na:]
        pos = _mesh_pos()
        c = pos[2]
        sib = _peer(pos, SIBLING)
        sends = [_rcopy(bufs[a].at[c], bufs[a].at[c], ssem.at[a], rsem.at[a], sib) for a in range(na)]
        for cp in sends:
            cp.start()
        for a in range(na):
            _rcopy(bufs[a].at[c], bufs[a].at[1 - c], ssem.at[a], rsem.at[a], sib).wait_recv()
        for cp in sends:
            cp.wait_send()

    dma = pltpu.SemaphoreType.DMA
    return pl.pallas_call(
        body, in_specs=[ANY] * na, out_specs=[ANY] * na, out_shape=[S(s.shape, s.dtype) for s in arrs],
        input_output_aliases={a: a for a in range(na)},
        scratch_shapes=[dma((na,)), dma((na,))], name="share_sibling")(*arrs)


def _small_allreduce(vec):
    r = vec.shape[0]

    def body(v_ref, sum_ref, all_ref, ssem, rsem):
        pos = _mesh_pos()
        me = _lin(pos)
        all_ref[me] = v_ref[...]
        cps = [_rcopy(v_ref, all_ref.at[me], ssem.at[m - 1], rsem.at[m - 1], _peer(pos, m)) for m in range(1, N_DEV)]
        for cp in cps:
            cp.start()
        for m in range(1, N_DEV):
            src = _peer(pos, m)
            _rcopy(v_ref, all_ref.at[_lin(src)], ssem.at[m - 1], rsem.at[m - 1], src).wait_recv()
        acc = all_ref[0]
        for d in range(1, N_DEV):
            acc = acc + all_ref[d]
        sum_ref[...] = acc
        for cp in cps:
            cp.wait_send()

    dma = pltpu.SemaphoreType.DMA
    return pl.pallas_call(
        body, in_specs=[VMEM], out_specs=[VMEM, VMEM],
        out_shape=[S((r, 128), F32), S((N_DEV, r, 128), F32)],
        scratch_shapes=[dma((N_DEV - 1,)), dma((N_DEV - 1,))],
        compiler_params=pltpu.CompilerParams(vmem_limit_bytes=VMEM_LIMIT_V7X), name="small_allreduce")(vec)


ROW_BYTES_TARGET = 1 << 20


def _row_tile(rows, cols, itemsize=4):
    t = rows
    while t % 2 == 0 and t * cols * itemsize > ROW_BYTES_TARGET and (t // 2) % 16 == 0:
        t //= 2
    return t


def _pair_add(g, other, layer):
    _, r, cdim = g.shape
    tr = _row_tile(r, cdim)

    def body(l_ref, g_ref, o_ref, out_ref):
        del l_ref
        out_ref[...] = (g_ref[...].astype(F32) + o_ref[...].astype(F32)).astype(out_ref.dtype)

    return pl.pallas_call(
        body,
        grid_spec=pltpu.PrefetchScalarGridSpec(
            num_scalar_prefetch=1, grid=(r // tr,),
            in_specs=[pl.BlockSpec((None, tr, cdim), lambda i, l: (l[0], i, 0)), pl.BlockSpec((tr, cdim), lambda i, l: (i, 0))],
            out_specs=pl.BlockSpec((tr, cdim), lambda i, l: (i, 0))),
        out_shape=S((r, cdim), BF), compiler_params=_cparams("parallel"), name="pair_add")(layer, g, other)


def _sum4(q, own, chip_core):
    _, r, cdim = q.shape
    tr = _row_tile(r, cdim)

    def body(kc_ref, q_ref, own_ref, out_ref):
        mine = own_ref[...].astype(F32)
        terms = [jnp.where(kc_ref[0] == j, mine, q_ref[j].astype(F32)) for j in range(N_CHIPS)]
        out_ref[...] = ((terms[0] + terms[1]) + terms[2]) + terms[3]

    return pl.pallas_call(
        body,
        grid_spec=pltpu.PrefetchScalarGridSpec(
            num_scalar_prefetch=1, grid=(r // tr,),
            in_specs=[pl.BlockSpec((N_CHIPS, tr, cdim), lambda i, kc: (0, i, 0)),
                      pl.BlockSpec((None, tr, cdim), lambda i, kc: (kc[0], i, 0))],
            out_specs=pl.BlockSpec((None, tr, cdim), lambda i, kc: (kc[1], i, 0))),
        out_shape=S((DEPTH, r, cdim), F32), compiler_params=_cparams("parallel"), name="sum4")(chip_core, q, own)


def _ada_w_grad(silu_c, dmod_cols, chip):
    def body(k_ref, sc_ref, dm_ref, o_ref):
        del k_ref
        o_ref[...] = lax.dot_general(sc_ref[...], dm_ref[...], TN, precision=HIGHEST, preferred_element_type=F32)

    return pl.pallas_call(
        body,
        grid_spec=pltpu.PrefetchScalarGridSpec(
            num_scalar_prefetch=1, grid=(4,),
            in_specs=[pl.BlockSpec((N_DEV, D), lambda j, k: (0, 0)),
                      pl.BlockSpec((None, N_DEV, ADA_COLS), lambda j, k: (4 * j + k[0], 0, 0))],
            out_specs=pl.BlockSpec((None, D, ADA_COLS), lambda j, k: (j, 0, 0))),
        out_shape=S((4, D, ADA_COLS), F32), compiler_params=_cparams("parallel"), name="ada_w_grad")(chip, silu_c, dmod_cols)


def _adamw(w, g, m, v, *, name):
    r, cdim = w.shape
    tr = _row_tile(r, cdim)
    c1 = 1.0 / (1.0 - ADAM_B1 ** ADAM_STEP)
    c2 = 1.0 / (1.0 - ADAM_B2 ** ADAM_STEP)

    def body(w_ref, g_ref, m_ref, v_ref, d_ref, mo_ref, vo_ref):
        gv = g_ref[...]
        mn = ADAM_B1 * m_ref[...] + (1.0 - ADAM_B1) * gv
        vn = ADAM_B2 * v_ref[...] + (1.0 - ADAM_B2) * (gv * gv)
        mo_ref[...] = mn
        vo_ref[...] = vn
        d_ref[...] = -ADAM_LR * ((mn * c1) / (jnp.sqrt(vn * c2) + ADAM_EPS) + ADAM_WD * w_ref[...])

    blk = pl.BlockSpec((tr, cdim), lambda i: (i, 0))
    return pl.pallas_call(
        body, grid=(r // tr,), in_specs=[blk] * 4, out_specs=[blk] * 3, out_shape=[S((r, cdim), F32)] * 3,
        compiler_params=_cparams("parallel"), name=name)(w, g, m, v)


WEIGHTS = ['ada_mix_w', 'ada_mix_b', 'norm_mix_g', 'w_in', 'b_gate', 'ssd_conv_w', 'ssd_conv_b', 'ssd_dt_bias',
           'ssd_a_log', 'ssd_d', 'ssd_norm_g', 'w_ssd_out', 'conf_conv_w', 'conf_conv_b', 'conf_ln_g', 'conf_ln_b',
           'w_conf_out', 'sc_conv_w', 'w_sc_out', 'w_o', 'ada_ffn_w', 'ada_ffn_b', 'norm_ffn_g', 'w_up', 'ffn_conv_w',
           'ffn_conv_b', 'w_down', 'final_norm_g']
SMALL = ['ada_mix_b', 'norm_mix_g', 'b_gate', 'ssd_conv_b', 'ssd_dt_bias', 'ssd_a_log', 'ssd_d', 'ssd_norm_g', 'conf_conv_b',
         'conf_ln_g', 'conf_ln_b', 'ada_ffn_b', 'norm_ffn_g', 'ffn_conv_b']
CONVS = ['ssd_conv_w', 'conf_conv_w', 'sc_conv_w', 'ffn_conv_w']
BIG = ['ada_mix_w', 'ada_ffn_w', 'w_in', 'w_up', 'w_conf_out', 'w_sc_out', 'w_ssd_out', 'w_o', 'w_down']


def _pack_rows(pieces):
    flat = [p.reshape(-1) for p in pieces]
    offs, o = [], 0
    for f in flat:
        offs.append(o)
        o += f.shape[0]
    total = -(-o // 1024) * 1024
    vec = jnp.concatenate(flat + [jnp.zeros((total - o,), F32)])
    return vec.reshape(total // 128, 128), offs


def _by_chip(a, axis):
    shp = a.shape
    a = a.reshape(shp[:axis] + (N_CHIPS, shp[axis] // N_CHIPS) + shp[axis + 1:])
    return jnp.moveaxis(a, axis, 0)


def _from_chips(a, axis):
    a = jnp.moveaxis(a, 0, axis)
    shp = a.shape
    return a.reshape(shp[:axis] + (shp[axis] * shp[axis + 1],) + shp[axis + 2:])


def kernel(x, c, ada_mix_w, ada_mix_b, norm_mix_g, w_in, b_gate, ssd_conv_w, ssd_conv_b, ssd_dt_bias, ssd_a_log, ssd_d, ssd_norm_g, w_ssd_out, conf_conv_w, conf_conv_b, conf_ln_g, conf_ln_b, w_conf_out, sc_conv_w, w_sc_out, w_o, ada_ffn_w, ada_ffn_b, norm_ffn_g, w_up, ffn_conv_w, ffn_conv_b, w_down, final_norm_g, loss_target, m_ada_mix_w, m_ada_mix_b, m_norm_mix_g, m_w_in, m_b_gate, m_ssd_conv_w, m_ssd_conv_b, m_ssd_dt_bias, m_ssd_a_log, m_ssd_d, m_ssd_norm_g, m_w_ssd_out, m_conf_conv_w, m_conf_conv_b, m_conf_ln_g, m_conf_ln_b, m_w_conf_out, m_sc_conv_w, m_w_sc_out, m_w_o, m_ada_ffn_w, m_ada_ffn_b, m_norm_ffn_g, m_w_up, m_ffn_conv_w, m_ffn_conv_b, m_w_down, m_final_norm_g, v_ada_mix_w, v_ada_mix_b, v_norm_mix_g, v_w_in, v_b_gate, v_ssd_conv_w, v_ssd_conv_b, v_ssd_dt_bias, v_ssd_a_log, v_ssd_d, v_ssd_norm_g, v_w_ssd_out, v_conf_conv_w, v_conf_conv_b, v_conf_ln_g, v_conf_ln_b, v_w_conf_out, v_sc_conv_w, v_w_sc_out, v_w_o, v_ada_ffn_w, v_ada_ffn_b, v_norm_ffn_g, v_w_up, v_ffn_conv_w, v_ffn_conv_b, v_w_down, v_final_norm_g):
    args = locals()
    w = {n: args[n] for n in WEIGHTS}
    mom = {n: args["m_" + n] for n in WEIGHTS}
    var = {n: args["v_" + n] for n in WEIGHTS}
    pos = _mesh_pos()
    chip = _chip(pos)
    core = pos[2]

    conv_pack = jnp.zeros((DEPTH, CONV_ROWS, CONV_COLS), F32)
    for n, (r0, taps, width) in CONV_PACK.items():
        conv_pack = conv_pack.at[:, r0:r0 + taps, 0:width].set(w[n])
    c_blk = jnp.pad(c, ((0, 7), (0, 0)))
    mods_raw, silu_c, conv_all = _ada_exchange(c_blk, ada_mix_w, ada_ffn_w, conv_pack)
    ada_b = jnp.concatenate([ada_mix_b, ada_ffn_b], axis=0)
    mod_all = mods_raw.reshape(N_CHIPS, 4, ADA_COLS).transpose(1, 0, 2).reshape(4, 3 * D) + ada_b
    mod_all = mod_all.reshape(4, 3, D)
    mods = [(mod_all[i], mod_all[2 + i]) for i in range(DEPTH)]
    conv_full = {n: _from_chips(conv_all[:, :, r0:r0 + taps, 0:width], 2) for n, (r0, taps, width) in CONV_PACK.items()}

    cast = lambda a: a.astype(BF)
    shards = [cast(w_in), cast(w_up), jnp.concatenate([cast(w_conf_out), cast(w_sc_out)], axis=1),
              jnp.concatenate([cast(w_ssd_out), cast(w_o), cast(w_down)], axis=1)]
    gathered = _gather_weights(shards)
    g_in, g_up, g_cs, g_row = [lax.dynamic_update_slice(g, s[:, None], (0, chip, 0, 0)) for g, s in zip(gathered, shards)]
    layers = []
    for l in range(DEPTH):
        p = {n: w[n][l] for n in SMALL if not n.startswith("ada_")}
        p.update({n: conv_full[n][l] for n in CONVS})
        p["w_in_pad"] = _pad_w_in_chips(g_in[l])
        p["w_up4"] = g_up[l]
        p["w_conf_out"] = _from_chips(g_cs[l, :, 0:CONF_W], 1)
        p["w_sc_out"] = _from_chips(g_cs[l, :, CONF_W:], 1)
        p["w_ssd_out"] = _from_chips(g_row[l, :, 0:INNER // N_CHIPS], 0)
        p["w_o"] = _from_chips(g_row[l, :, INNER // N_CHIPS:(INNER + D) // N_CHIPS], 0)
        p["w_down"] = _from_chips(g_row[l, :, (INNER + D) // N_CHIPS:], 0)
        layers.append(p)

    seq = x.shape[1]
    loss, dx, grads, dmods, dfinal = _local_step(x.reshape(seq, D), loss_target.reshape(seq, D), layers, mods, final_norm_g)

    def stack(fn):
        return jnp.stack([fn(grads[l]).astype(BF) for l in range(DEPTH)])

    big = [stack(lambda g: _unpad_w_in_chips(g["w_in_pad"])),
           stack(lambda g: g["w_up4"]),
           stack(lambda g: _by_chip(jnp.concatenate([g["w_conf_out"], g["w_sc_out"]], axis=0), 1)),
           stack(lambda g: jnp.concatenate([_by_chip(g["w_ssd_out"], 0), _by_chip(g["w_o"], 0), _by_chip(g["w_down"], 0)],
                                           axis=1))]
    theirs = _swap_half(big)
    layer_idx = jnp.reshape(core, (1,)).astype(jnp.int32)
    pair = []
    for g2, t in zip(big, theirs):
        r, cdim = t.shape[1], t.shape[2]
        pair.append(_pair_add(g2.reshape(DEPTH, N_CHIPS * r, cdim), t.reshape(N_CHIPS * r, cdim), layer_idx)
                    .reshape(N_CHIPS, r, cdim))
    parts = _scatter_chips(pair)
    chip_core = jnp.stack([chip, core]).astype(jnp.int32)
    reduced = _share_sibling([_sum4(q, own, chip_core) for q, own in zip(parts, pair)])
    gw = {"w_in": reduced[0], "w_up": reduced[1],
          "w_conf_out": reduced[2][:, 0:CONF_W], "w_sc_out": reduced[2][:, CONF_W:],
          "w_ssd_out": reduced[3][:, 0:INNER // N_CHIPS], "w_o": reduced[3][:, INNER // N_CHIPS:(INNER + D) // N_CHIPS],
          "w_down": reduced[3][:, (INNER + D) // N_CHIPS:]}

    dmod = jnp.stack([dmods[0][0], dmods[1][0], dmods[0][1], dmods[1][1]])
    small_local = {n: jnp.stack([grads[l][n] for l in range(DEPTH)]) for n in SMALL if not n.startswith("ada_")}
    pieces = [loss[0]] + [small_local[n] for n in SMALL if not n.startswith("ada_")]
    pieces += [jnp.stack([grads[l][n] for l in range(DEPTH)]) for n in CONVS] + [dfinal, dmod]
    vec, offs = _pack_rows(pieces)
    vsum, vall = _small_allreduce(vec)
    flat = vsum.reshape(-1)

    def piece(k, like):
        return flat[offs[k]:offs[k] + like.size].reshape(like.shape)

    loss_out = flat[0]
    k = 1
    for n in SMALL:
        if not n.startswith("ada_"):
            gw[n] = piece(k, small_local[n])
            k += 1
    for n in CONVS:
        full = piece(k, conv_full[n])
        width = CONV_PACK[n][2]
        gw[n] = lax.dynamic_slice_in_dim(full, chip * width, width, axis=2)
        k += 1
    gw["final_norm_g"] = piece(k, dfinal)
    k += 1
    dmod_sum = piece(k, dmod)
    gw["ada_mix_b"], gw["ada_ffn_b"] = dmod_sum[0:2], dmod_sum[2:4]
    dmod_all = vall.reshape(N_DEV, -1)[:, offs[k]:offs[k] + dmod.size]
    dmod_cols = dmod_all.reshape(N_DEV, 4 * N_CHIPS, ADA_COLS).transpose(1, 0, 2)
    ada_g = _ada_w_grad(silu_c, dmod_cols, jnp.reshape(chip, (1,)).astype(jnp.int32))
    gw["ada_mix_w"], gw["ada_ffn_w"] = ada_g[0:2], ada_g[2:4]

    delta, new_m, new_v = {}, {}, {}
    for n in BIG:
        shp = w[n].shape
        two_d = lambda a: a.reshape(shp[0] * shp[1], shp[2])
        d_, m_, v_ = _adamw(two_d(w[n]), two_d(gw[n]), two_d(mom[n]), two_d(var[n]), name=f"adamw_{n}")
        delta[n], new_m[n], new_v[n] = d_.reshape(shp), m_.reshape(shp), v_.reshape(shp)
    rest = [n for n in WEIGHTS if n not in BIG]
    packs = [_pack_rows([src[n] for n in rest])[0] for src in (w, gw, mom, var)]
    offs_r = _pack_rows([w[n] for n in rest])[1]
    outs = _adamw(*packs, name="adamw_small")
    for dst, o in zip((delta, new_m, new_v), outs):
        of = o.reshape(-1)
        for n, off in zip(rest, offs_r):
            dst[n] = of[off:off + w[n].size].reshape(w[n].shape)

    return (loss_out, dx[None], *[gw[n] for n in WEIGHTS], *[delta[n] for n in WEIGHTS],
            *[new_m[n] for n in WEIGHTS], *[new_v[n] for n in WEIGHTS])
```

```python
import functools

import jax
import jax.numpy as jnp
from jax import lax
from jax.experimental import pallas as pl
from jax.experimental.pallas import tpu as pltpu

F32 = jnp.float32
BF = jnp.bfloat16
S = jax.ShapeDtypeStruct

D = 1024
HEADS = 16
HEAD_DIM = 64
INNER = HEADS * HEAD_DIM
GROUPS = 2
NSTATE = 64
Q = 128
SSD_K = 4
XBC = INNER + 2 * GROUPS * NSTATE
CONF_W = 512
CONF_K = 31
SC_W = 512
SC_K = 3
DFF = 2816
FFN_K = 3
EPS = 1e-6
DEPTH = 2
R_Z, R_XBC, R_DT, R_CONF, R_SC, R_GATES, N_IN = 0, 1024, 2304, 2320, 3344, 4880, 7952
P_GATES, P_SC, P_XBC, P_Z, P_CONF, PW = 0, 3072, 4608, 6144, 7168, 8192
XBC_PAD = 1536
DT_PAD = 128
P_DT = P_XBC + XBC
N_CHIPS = 4
N_DEV = 8

ADAM_LR, ADAM_B1, ADAM_B2, ADAM_EPS, ADAM_WD, ADAM_STEP = 0.001, 0.9, 0.999, 1e-08, 0.01, 10

VMEM_LIMIT_V7X = 56 * 1024 * 1024
HIGHEST = lax.Precision.HIGHEST


def _cparams(*sem):
    return pltpu.CompilerParams(dimension_semantics=sem, vmem_limit_bytes=VMEM_LIMIT_V7X)


def _full(shape):
    n = len(shape)
    return pl.BlockSpec(shape, lambda *_: (0,) * n)


def _rows(tm, w, cb=0):
    return pl.BlockSpec((tm, w), lambda i: (i, cb))


def _prev_rows(tm, halo, w, cb=0):
    r = tm // halo
    return pl.BlockSpec((halo, w), lambda i: (jnp.maximum(i * r - 1, 0), cb))


def _next_rows(tm, halo, w, nrows, cb=0):
    r = tm // halo
    last = nrows // halo - 1
    return pl.BlockSpec((halo, w), lambda i: (jnp.minimum((i + 1) * r, last), cb))


def _sigmoid(v):
    return 1.0 / (1.0 + jnp.exp(-v))


def _softplus(v):
    return jnp.maximum(v, 0.0) + jnp.log(1.0 + jnp.exp(-jnp.abs(v)))


def _colsum(v):
    return jnp.sum(v, axis=0, keepdims=True)


def _tile(n, want):
    t = min(n, want)
    assert n % t == 0, (n, want)
    return t


NN = (((1,), (0,)), ((), ()))
NT = (((1,), (1,)), ((), ()))
TN = (((0,), (0,)), ((), ()))


def _dot(a, b, dims=NN):
    return lax.dot_general(a, b, dims, preferred_element_type=F32)


def _mm(a, b, *, dims, grid, a_spec, b_spec, o_spec, out_shape, acc_shape, name):
    nk = grid[2]

    def body(a_ref, b_ref, o_ref, acc_ref):
        k = pl.program_id(2)
        part = _dot(a_ref[...], b_ref[...], dims)
        if nk == 1:
            o_ref[...] = part.astype(o_ref.dtype)
        else:
            @pl.when(k == 0)
            def _():
                acc_ref[...] = part

            @pl.when(k > 0)
            def _():
                acc_ref[...] += part

            @pl.when(k == nk - 1)
            def _():
                o_ref[...] = acc_ref[...].astype(o_ref.dtype)

    return pl.pallas_call(
        body, grid=grid, in_specs=[a_spec, b_spec], out_specs=o_spec, out_shape=out_shape,
        scratch_shapes=[pltpu.VMEM(acc_shape if nk > 1 else (8, 128), F32)],
        compiler_params=_cparams("parallel", "parallel", "arbitrary"), name=name)(a, b)


def _mm_nn(a, b, *, out_dtype, tm, tn, name):
    m, k = a.shape
    n = b.shape[1]
    tm, tn = _tile(m, tm), _tile(n, tn)
    return _mm(a, b, dims=NN, grid=(m // tm, n // tn, 1),
               a_spec=pl.BlockSpec((tm, k), lambda i, j, kk: (i, 0)),
               b_spec=pl.BlockSpec((k, tn), lambda i, j, kk: (0, j)),
               o_spec=pl.BlockSpec((tm, tn), lambda i, j, kk: (i, j)),
               out_shape=S((m, n), out_dtype), acc_shape=(tm, tn), name=name)


def _mm_nt(a, b, *, out_dtype, tm, tk, name):
    m, kc = a.shape
    n = b.shape[0]
    tm, tk = _tile(m, tm), _tile(kc, tk)
    return _mm(a, b, dims=NT, grid=(m // tm, 1, kc // tk),
               a_spec=pl.BlockSpec((tm, tk), lambda i, j, kk: (i, kk)),
               b_spec=pl.BlockSpec((n, tk), lambda i, j, kk: (0, kk)),
               o_spec=pl.BlockSpec((tm, n), lambda i, j, kk: (i, 0)),
               out_shape=S((m, n), out_dtype), acc_shape=(tm, n), name=name)


def _mm_tn(a, b, *, tn, tk, name, out_dtype=BF, by_chip=False):
    kc, m = a.shape
    n = b.shape[1]
    tn, tk = _tile(n, tn), _tile(kc, tk)
    if by_chip:
        assert n == N_CHIPS * tn
        o_spec, out_shape = pl.BlockSpec((None, m, tn), lambda i, j, kk: (j, 0, 0)), S((N_CHIPS, m, tn), out_dtype)
    else:
        o_spec, out_shape = pl.BlockSpec((m, tn), lambda i, j, kk: (0, j)), S((m, n), out_dtype)
    return _mm(a, b, dims=TN, grid=(1, n // tn, kc // tk),
               a_spec=pl.BlockSpec((tk, m), lambda i, j, kk: (kk, 0)),
               b_spec=pl.BlockSpec((tk, tn), lambda i, j, kk: (kk, j)),
               o_spec=o_spec, out_shape=out_shape, acc_shape=(m, tn), name=name)


def _mm_nn_chips(a, b4, *, out_dtype, tm, name):
    m, k = a.shape
    n4 = b4.shape[2]
    tm = _tile(m, tm)
    return _mm(a, b4, dims=NN, grid=(m // tm, N_CHIPS, 1),
               a_spec=pl.BlockSpec((tm, k), lambda i, j, kk: (i, 0)),
               b_spec=pl.BlockSpec((None, k, n4), lambda i, j, kk: (j, 0, 0)),
               o_spec=pl.BlockSpec((tm, n4), lambda i, j, kk: (i, j)),
               out_shape=S((m, N_CHIPS * n4), out_dtype), acc_shape=(tm, n4), name=name)


def _mm_nt_chips(a, b4, *, out_dtype, tm, name):
    m = a.shape[0]
    n, n4 = b4.shape[1], b4.shape[2]
    tm = _tile(m, tm)
    return _mm(a, b4, dims=NT, grid=(m // tm, 1, N_CHIPS),
               a_spec=pl.BlockSpec((tm, n4), lambda i, j, kk: (i, kk)),
               b_spec=pl.BlockSpec((None, n, n4), lambda i, j, kk: (kk, 0, 0)),
               o_spec=pl.BlockSpec((tm, n), lambda i, j, kk: (i, 0)),
               out_shape=S((m, n), out_dtype), acc_shape=(tm, n), name=name)


def _mm_resid(a, b, x, mod, *, tm, name):
    m, k = a.shape
    n = b.shape[1]
    tm = _tile(m, tm)

    def body(a_ref, b_ref, x_ref, mod_ref, o_ref, xn_ref):
        o = _dot(a_ref[...], b_ref[...])
        o_ref[...] = o.astype(o_ref.dtype)
        xn_ref[...] = x_ref[...] + mod_ref[2:3, :] * o

    return pl.pallas_call(
        body, grid=(m // tm,),
        in_specs=[_rows(tm, k), _full((k, n)), _rows(tm, n), _full((3, n))],
        out_specs=[_rows(tm, n), _rows(tm, n)],
        out_shape=[S((m, n), BF), S((m, n), F32)],
        compiler_params=_cparams("parallel"), name=name)(a, b, x, mod)


def _modnorm_fwd(x, gain, mod, *, name):
    n = x.shape[0]
    tm = _tile(n, 512)

    def body(x_ref, g_ref, mod_ref, h_ref):
        xv = x_ref[...]
        r = lax.rsqrt(jnp.mean(xv * xv, axis=-1, keepdims=True) + EPS)
        y = xv * r * g_ref[...]
        h_ref[...] = (y * (1.0 + mod_ref[1:2, :]) + mod_ref[0:1, :]).astype(h_ref.dtype)

    return pl.pallas_call(
        body, grid=(n // tm,), in_specs=[_rows(tm, D), _full((1, D)), _full((3, D))],
        out_specs=_rows(tm, D), out_shape=S((n, D), BF), compiler_params=_cparams("parallel"), name=name)(x, gain, mod)


def _modnorm_bwd(dh, x, dres, gain, mod, *, name):
    n = x.shape[0]
    tm = _tile(n, 512)

    def body(dh_ref, x_ref, dres_ref, g_ref, mod_ref, dx_ref, dg_ref, dsh_ref, dsc_ref):
        i = pl.program_id(0)
        xv = x_ref[...]
        r = lax.rsqrt(jnp.mean(xv * xv, axis=-1, keepdims=True) + EPS)
        xh = xv * r
        dhv = dh_ref[...]
        g = g_ref[...]
        dy = dhv * (1.0 + mod_ref[1:2, :])
        dxh = dy * g
        dx = r * (dxh - xh * jnp.mean(dxh * xh, axis=-1, keepdims=True))
        dx_ref[...] = dres_ref[...] + dx

        @pl.when(i == 0)
        def _():
            dg_ref[...] = jnp.zeros_like(dg_ref)
            dsh_ref[...] = jnp.zeros_like(dsh_ref)
            dsc_ref[...] = jnp.zeros_like(dsc_ref)

        dg_ref[...] += _colsum(dy * xh)
        dsh_ref[...] += _colsum(dhv)
        dsc_ref[...] += _colsum(dhv * xh * g)

    vec = S((1, D), F32)
    return pl.pallas_call(
        body, grid=(n // tm,),
        in_specs=[_rows(tm, D), _rows(tm, D), _rows(tm, D), _full((1, D)), _full((3, D))],
        out_specs=[_rows(tm, D), _full((1, D)), _full((1, D)), _full((1, D))],
        out_shape=[S((n, D), F32), vec, vec, vec],
        compiler_params=_cparams("arbitrary"), name=name)(dh, x, dres, gain, mod)


def _final_loss(x, gain, target, *, name):
    n = x.shape[0]
    tm = _tile(n, 512)

    def body(x_ref, g_ref, t_ref, dx_ref, loss_ref, dg_ref):
        i = pl.program_id(0)
        xv = x_ref[...]
        g = g_ref[...]
        r = lax.rsqrt(jnp.mean(xv * xv, axis=-1, keepdims=True) + EPS)
        xh = xv * r
        err = xh * g - t_ref[...]
        dy = err * (1.0 / D)
        dxh = dy * g
        dx_ref[...] = r * (dxh - xh * jnp.mean(dxh * xh, axis=-1, keepdims=True))

        @pl.when(i == 0)
        def _():
            loss_ref[...] = jnp.zeros_like(loss_ref)
            dg_ref[...] = jnp.zeros_like(dg_ref)

        part = _colsum(jnp.sum(err * err, axis=-1, keepdims=True)) * (0.5 / D)
        loss_ref[...] += jnp.broadcast_to(part, loss_ref.shape)
        dg_ref[...] += _colsum(dy * xh)

    return pl.pallas_call(
        body, grid=(n // tm,),
        in_specs=[_rows(tm, D), _full((1, D)), _rows(tm, D)],
        out_specs=[_rows(tm, D), _full((1, 128)), _full((1, D))],
        out_shape=[S((n, D), F32), S((1, 128), F32), S((1, D), F32)],
        compiler_params=_cparams("arbitrary"), name=name)(x, gain, target)


def _gate_bwd(dx, o, mod, *, name):
    n = dx.shape[0]
    tm = _tile(n, 512)

    def body(dx_ref, o_ref, mod_ref, do_ref, dgt_ref):
        i = pl.program_id(0)
        dxv = dx_ref[...]
        do_ref[...] = (dxv * mod_ref[2:3, :]).astype(do_ref.dtype)

        @pl.when(i == 0)
        def _():
            dgt_ref[...] = jnp.zeros_like(dgt_ref)

        dgt_ref[...] += _colsum(dxv * o_ref[...].astype(F32))

    return pl.pallas_call(
        body, grid=(n // tm,), in_specs=[_rows(tm, D), _rows(tm, D), _full((3, D))],
        out_specs=[_rows(tm, D), _full((1, D))], out_shape=[S((n, D), BF), S((1, D), F32)],
        compiler_params=_cparams("arbitrary"), name=name)(dx, o, mod)


def _conv(buf, w_ref, taps, start, rows, ch):
    acc = None
    for k in range(taps):
        term = buf[pl.ds(start - (taps - 1) + k, rows), 0:ch] * w_ref[k:k + 1, :]
        acc = term if acc is None else acc + term
    return acc


def _conv_t(buf, w_ref, taps, start, rows, ch):
    acc = None
    for k in range(taps):
        term = buf[pl.ds(start + (taps - 1) - k, rows), 0:ch] * w_ref[k:k + 1, :]
        acc = term if acc is None else acc + term
    return acc


def _conv_dw(dw_ref, dy, xbuf, taps, xstart, rows, ch):
    for k in range(taps):
        dw_ref[k:k + 1, :] += _colsum(dy * xbuf[pl.ds(xstart - (taps - 1) + k, rows), 0:ch])


HALO = 16
CONF_HALO = 32
CHUNK = 16
STRIP = 256


def _blocks8(v):
    return [v[8 * i:8 * (i + 1)] for i in range(v.shape[0] // 8)]


def _delay_rows(blocks, s):
    sub = lax.broadcasted_iota(jnp.int32, blocks[0].shape, 0)
    rolled = [pltpu.roll(b, s, 0) for b in blocks]
    return [jnp.where(sub < s, rolled[i - 1], rolled[i]) for i in range(1, len(blocks))]


def _advance_rows(blocks, s):
    sub = lax.broadcasted_iota(jnp.int32, blocks[0].shape, 0)
    rolled = [pltpu.roll(b, 8 - s, 0) for b in blocks]
    return [jnp.where(sub < 8 - s, rolled[i], rolled[i + 1]) for i in range(len(blocks) - 1)]


def _conv3_chunk(tail, xv, wk):
    blocks = [tail] + _blocks8(xv)
    x1 = jnp.concatenate(_delay_rows(blocks, 1), axis=0)
    x2 = jnp.concatenate(_delay_rows(blocks, 2), axis=0)
    return wk[0] * x2 + wk[1] * x1 + wk[2] * xv


def _conv_chunk(tail, xv, wk):
    taps = len(wk)
    blocks = [tail] + _blocks8(xv)
    acc = wk[taps - 1] * xv
    for d in range(1, taps):
        acc = acc + wk[taps - 1 - d] * jnp.concatenate(_delay_rows(blocks, d), axis=0)
    return acc


def _ssd_pre_fwd(proj, w, b, *, name):
    n = proj.shape[0]
    tm = _tile(n, 512)
    cb = P_XBC // XBC_PAD

    def body(prev_ref, cur_ref, w_ref, b_ref, o_ref, c_ref, buf):
        i = pl.program_id(0)
        buf[0:HALO, :] = jnp.where(i == 0, 0.0, prev_ref[:, 0:XBC].astype(F32))
        buf[HALO:HALO + tm, :] = cur_ref[:, 0:XBC].astype(F32)
        c = _conv(buf, w_ref, SSD_K, HALO, tm, XBC) + b_ref[...]
        c_ref[...] = c.astype(c_ref.dtype)
        o_ref[...] = (c * _sigmoid(c)).astype(o_ref.dtype)

    return pl.pallas_call(
        body, grid=(n // tm,),
        in_specs=[_prev_rows(tm, HALO, XBC_PAD, cb), _rows(tm, XBC_PAD, cb), _full((SSD_K, XBC)), _full((1, XBC))],
        out_specs=[_rows(tm, XBC), _rows(tm, XBC)], out_shape=[S((n, XBC), BF), S((n, XBC), BF)],
        scratch_shapes=[pltpu.VMEM((HALO + tm, XBC), F32)],
        compiler_params=_cparams("parallel"), name=name)(proj, proj, w, b)


def _ssd_pre_bwd(proj, cpre, dact, ddt, dproj, w, *, name):
    n = proj.shape[0]
    tm = _tile(n, 512)
    nt = n // tm
    cb = P_XBC // XBC_PAD

    def body(x_ref, cc_ref, cn_ref, dc_ref, dn_ref, ddt_ref, w_ref, dproj_in, o_ref, dw_ref, db_ref, dbuf, acc):
        del dproj_in
        i = pl.program_id(0)
        last = i == nt - 1

        @pl.when(i == 0)
        def _():
            acc[...] = jnp.zeros_like(acc)

        def silu_bwd(cv, dav):
            sg = _sigmoid(cv)
            return dav * (sg * (1.0 + cv * (1.0 - sg)))

        for s in range(XBC // STRIP):
            c = pl.ds(s * STRIP, STRIP)
            wk = [w_ref[k:k + 1, c] for k in range(SSD_K)]

            def step1(j, carry):
                rows = pl.ds(pl.multiple_of(j * CHUNK, CHUNK), CHUNK)
                dbuf[rows, c] = silu_bwd(cc_ref[rows, c].astype(F32), dc_ref[rows, c].astype(F32))
                return carry

            lax.fori_loop(0, tm // CHUNK, step1, 0, unroll=2)
            dbuf[tm:tm + HALO, c] = silu_bwd(cn_ref[:, c].astype(F32), jnp.where(last, 0.0, dn_ref[:, c].astype(F32)))

            def step2(j, carry):
                r0 = pl.multiple_of(j * CHUNK, CHUNK)
                rows = pl.ds(r0, CHUNK)
                win = dbuf[pl.ds(r0, CHUNK + 8), c]
                blocks = _blocks8(win)
                xv = x_ref[rows, c].astype(F32)
                d0 = win[0:CHUNK]
                dx = wk[SSD_K - 1] * d0
                acc[SSD_K - 1, :, c] += d0 * xv
                acc[SSD_K, :, c] += d0
                for adv in range(1, SSD_K):
                    dk = jnp.concatenate(_advance_rows(blocks, adv), axis=0)
                    dx = dx + wk[SSD_K - 1 - adv] * dk
                    acc[SSD_K - 1 - adv, :, c] += dk * xv
                o_ref[rows, c] = dx.astype(o_ref.dtype)
                return carry

            lax.fori_loop(0, tm // CHUNK, step2, 0)

        o_ref[:, XBC:XBC + DT_PAD] = ddt_ref[...]
        o_ref[:, XBC + DT_PAD:XBC_PAD] = jnp.zeros((tm, XBC_PAD - XBC - DT_PAD), o_ref.dtype)

        @pl.when(last)
        def _():
            for k in range(SSD_K):
                dw_ref[k:k + 1, :] = _colsum(acc[k])
            db_ref[...] = _colsum(acc[SSD_K])

    return pl.pallas_call(
        body, grid=(nt,),
        in_specs=[_rows(tm, XBC_PAD, cb), _rows(tm, XBC), _next_rows(tm, HALO, XBC, n),
                  _rows(tm, XBC), _next_rows(tm, HALO, XBC, n), _rows(tm, DT_PAD),
                  _full((SSD_K, XBC)), pl.BlockSpec(memory_space=pl.ANY)],
        out_specs=[_rows(tm, XBC_PAD, cb), _full((SSD_K, XBC)), _full((1, XBC))],
        out_shape=[S(dproj.shape, dproj.dtype), S((SSD_K, XBC), F32), S((1, XBC), F32)],
        scratch_shapes=[pltpu.VMEM((tm + HALO, XBC), F32), pltpu.VMEM((SSD_K + 1, CHUNK, XBC), F32)],
        input_output_aliases={7: 0},
        compiler_params=_cparams("arbitrary"), name=name)(proj, cpre, cpre, dact, dact, ddt, w, dproj)


def _sc_fwd(proj, w, *, name):
    n = proj.shape[0]
    tm = _tile(n, 512)
    cb = P_SC // (3 * SC_W)

    def body(prev_ref, cur_ref, w_ref, o_ref, buf):
        i = pl.program_id(0)
        pv = prev_ref[...].astype(F32)
        cv = cur_ref[...].astype(F32)
        buf[0:HALO, :] = jnp.where(i == 0, 0.0, pv[:, SC_W:2 * SC_W] * pv[:, 2 * SC_W:])
        buf[HALO:HALO + tm, :] = cv[:, SC_W:2 * SC_W] * cv[:, 2 * SC_W:]
        q = _conv(buf, w_ref, SC_K, HALO, tm, SC_W)
        o_ref[...] = (cv[:, 0:SC_W] * q).astype(o_ref.dtype)

    return pl.pallas_call(
        body, grid=(n // tm,),
        in_specs=[_prev_rows(tm, HALO, 3 * SC_W, cb), _rows(tm, 3 * SC_W, cb), _full((SC_K, SC_W))],
        out_specs=_rows(tm, SC_W), out_shape=S((n, SC_W), BF),
        scratch_shapes=[pltpu.VMEM((HALO + tm, SC_W), F32)],
        compiler_params=_cparams("parallel"), name=name)(proj, proj, w)


def _sc_bwd(proj, da, dproj, w, *, name):
    n = proj.shape[0]
    tm = _tile(n, 512)
    nt = n // tm
    cb = P_SC // (3 * SC_W)

    def body(xp_ref, xc_ref, xn_ref, dc_ref, dn_ref, w_ref, dproj_in, o_ref, dw_ref, pbuf, dbuf):
        del dproj_in
        i = pl.program_id(0)
        pv = xp_ref[...].astype(F32)
        cv = xc_ref[...].astype(F32)
        nv = xn_ref[...].astype(F32)
        gb, gc, xv = cv[:, 0:SC_W], cv[:, SC_W:2 * SC_W], cv[:, 2 * SC_W:]
        pbuf[0:HALO, :] = jnp.where(i == 0, 0.0, pv[:, SC_W:2 * SC_W] * pv[:, 2 * SC_W:])
        pbuf[HALO:HALO + tm, :] = gc * xv
        q = _conv(pbuf, w_ref, SC_K, HALO, tm, SC_W)
        dav = dc_ref[...].astype(F32)
        dbuf[0:tm, :] = dav * gb
        dbuf[tm:tm + HALO, :] = jnp.where(i == nt - 1, 0.0, dn_ref[...].astype(F32) * nv[:, 0:SC_W])
        dp = _conv_t(dbuf, w_ref, SC_K, 0, tm, SC_W)
        o_ref[:, 0:SC_W] = (dav * q).astype(o_ref.dtype)
        o_ref[:, SC_W:2 * SC_W] = (dp * xv).astype(o_ref.dtype)
        o_ref[:, 2 * SC_W:] = (dp * gc).astype(o_ref.dtype)

        @pl.when(i == 0)
        def _():
            dw_ref[...] = jnp.zeros_like(dw_ref)

        _conv_dw(dw_ref, dbuf[0:tm, :], pbuf, SC_K, HALO, tm, SC_W)

    return pl.pallas_call(
        body, grid=(nt,),
        in_specs=[_prev_rows(tm, HALO, 3 * SC_W, cb), _rows(tm, 3 * SC_W, cb), _next_rows(tm, HALO, 3 * SC_W, n, cb),
                  _rows(tm, SC_W), _next_rows(tm, HALO, SC_W, n), _full((SC_K, SC_W)),
                  pl.BlockSpec(memory_space=pl.ANY)],
        out_specs=[_rows(tm, 3 * SC_W, cb), _full((SC_K, SC_W))],
        out_shape=[S(dproj.shape, dproj.dtype), S((SC_K, SC_W), F32)],
        scratch_shapes=[pltpu.VMEM((HALO + tm, SC_W), F32), pltpu.VMEM((tm + HALO, SC_W), F32)],
        input_output_aliases={6: 0},
        compiler_params=_cparams("arbitrary"), name=name)(proj, proj, proj, da, da, w, dproj)


def _conf_fwd(proj, w, b, ln_g, ln_b, *, name):
    n = proj.shape[0]
    tm = _tile(n, 512)
    cb = P_CONF // (2 * CONF_W)
    h = CONF_HALO

    def body(prev_ref, cur_ref, w_ref, b_ref, g_ref, be_ref, a_ref, uc_ref, buf):
        i = pl.program_id(0)
        pv = prev_ref[...].astype(F32)
        cv = cur_ref[...].astype(F32)
        buf[0:h, :] = jnp.where(i == 0, 0.0, pv[:, 0:CONF_W] * _sigmoid(pv[:, CONF_W:]))
        buf[h:h + tm, :] = cv[:, 0:CONF_W] * _sigmoid(cv[:, CONF_W:])
        uc = _conv(buf, w_ref, CONF_K, h, tm, CONF_W) + b_ref[...]
        uc_ref[...] = uc.astype(uc_ref.dtype)
        mu = jnp.mean(uc, axis=-1, keepdims=True)
        xc = uc - mu
        v = xc * lax.rsqrt(jnp.mean(xc * xc, axis=-1, keepdims=True) + EPS) * g_ref[...] + be_ref[...]
        a_ref[...] = (v * _sigmoid(v)).astype(a_ref.dtype)

    vec = _full((1, CONF_W))
    return pl.pallas_call(
        body, grid=(n // tm,),
        in_specs=[_prev_rows(tm, h, 2 * CONF_W, cb), _rows(tm, 2 * CONF_W, cb), _full((CONF_K, CONF_W)), vec, vec, vec],
        out_specs=[_rows(tm, CONF_W), _rows(tm, CONF_W)],
        out_shape=[S((n, CONF_W), BF), S((n, CONF_W), BF)],
        scratch_shapes=[pltpu.VMEM((h + tm, CONF_W), F32)],
        compiler_params=_cparams("parallel"), name=name)(proj, proj, w, b, ln_g, ln_b)


def _conf_bwd(proj, uc, da, dproj, w, ln_g, ln_b, *, name):
    n = proj.shape[0]
    tm = _tile(n, 512)
    nt = n // tm
    cb = P_CONF // (2 * CONF_W)
    h = CONF_HALO

    def body(xp_ref, xc_ref, ucc_ref, ucn_ref, dac_ref, dan_ref, w_ref, g_ref, be_ref, dproj_in,
             o_ref, dw_ref, db_ref, dg_ref, dbe_ref, ubuf, dbuf):
        del dproj_in
        i = pl.program_id(0)
        pv = xp_ref[...].astype(F32)
        cv = xc_ref[...].astype(F32)
        val, gt = cv[:, 0:CONF_W], cv[:, CONF_W:]
        sg = _sigmoid(gt)
        ubuf[0:h, :] = jnp.where(i == 0, 0.0, pv[:, 0:CONF_W] * _sigmoid(pv[:, CONF_W:]))
        ubuf[h:h + tm, :] = val * sg

        def ln_silu_bwd(ucv, dav):
            mu = jnp.mean(ucv, axis=-1, keepdims=True)
            xc = ucv - mu
            r = lax.rsqrt(jnp.mean(xc * xc, axis=-1, keepdims=True) + EPS)
            xh = xc * r
            v = xh * g_ref[...] + be_ref[...]
            s = _sigmoid(v)
            dv = dav * (s * (1.0 + v * (1.0 - s)))
            dxh = dv * g_ref[...]
            duc = r * (dxh - jnp.mean(dxh, axis=-1, keepdims=True) - xh * jnp.mean(dxh * xh, axis=-1, keepdims=True))
            return duc, dv, xh

        duc, dv, xh = ln_silu_bwd(ucc_ref[...].astype(F32), dac_ref[...].astype(F32))
        dbuf[0:tm, :] = duc
        ducn, _, _ = ln_silu_bwd(ucn_ref[...].astype(F32), dan_ref[...].astype(F32))
        dbuf[tm:tm + h, :] = jnp.where(i == nt - 1, 0.0, ducn)
        du = _conv_t(dbuf, w_ref, CONF_K, 0, tm, CONF_W)
        o_ref[:, 0:CONF_W] = (du * sg).astype(o_ref.dtype)
        o_ref[:, CONF_W:] = (du * val * sg * (1.0 - sg)).astype(o_ref.dtype)

        @pl.when(i == 0)
        def _():
            dw_ref[...] = jnp.zeros_like(dw_ref)
            db_ref[...] = jnp.zeros_like(db_ref)
            dg_ref[...] = jnp.zeros_like(dg_ref)
            dbe_ref[...] = jnp.zeros_like(dbe_ref)

        dg_ref[...] += _colsum(dv * xh)
        dbe_ref[...] += _colsum(dv)
        db_ref[...] += _colsum(duc)
        _conv_dw(dw_ref, duc, ubuf, CONF_K, h, tm, CONF_W)

    vec = _full((1, CONF_W))
    vshape = S((1, CONF_W), F32)
    return pl.pallas_call(
        body, grid=(nt,),
        in_specs=[_prev_rows(tm, h, 2 * CONF_W, cb), _rows(tm, 2 * CONF_W, cb),
                  _rows(tm, CONF_W), _next_rows(tm, h, CONF_W, n), _rows(tm, CONF_W), _next_rows(tm, h, CONF_W, n),
                  _full((CONF_K, CONF_W)), vec, vec, pl.BlockSpec(memory_space=pl.ANY)],
        out_specs=[_rows(tm, 2 * CONF_W, cb), _full((CONF_K, CONF_W)), vec, vec, vec],
        out_shape=[S(dproj.shape, dproj.dtype), S((CONF_K, CONF_W), F32), vshape, vshape, vshape],
        scratch_shapes=[pltpu.VMEM((h + tm, CONF_W), F32), pltpu.VMEM((tm + h, CONF_W), F32)],
        input_output_aliases={9: 0},
        compiler_params=_cparams("arbitrary"), name=name)(proj, proj, uc, uc, da, da, w, ln_g, ln_b, dproj)


def _ffn_act_fwd(up, w, b, *, name):
    n = up.shape[0]
    tm = _tile(n, 512)
    c2 = 2 * DFF

    def body(prev_ref, cur_ref, w_ref, b_ref, o_ref, u_ref):
        first = pl.program_id(0) == 0
        for s in range(DFF // STRIP):
            cols = (pl.ds(s * STRIP, STRIP), pl.ds(DFF + s * STRIP, STRIP))
            wk = [[w_ref[k:k + 1, c] for k in range(FFN_K)] for c in cols]
            bk = [b_ref[:, c] for c in cols]
            tails = tuple(jnp.where(first, 0.0, prev_ref[:, c].astype(F32)[HALO - 8:HALO]) for c in cols)

            def step(j, tails):
                r0 = pl.multiple_of(j * CHUNK, CHUNK)
                us, new_tails = [], []
                for h in range(2):
                    xv = cur_ref[pl.ds(r0, CHUNK), cols[h]].astype(F32)
                    us.append(_conv3_chunk(tails[h], xv, wk[h]) + bk[h])
                    u_ref[pl.ds(r0, CHUNK), cols[h]] = us[h].astype(u_ref.dtype)
                    new_tails.append(xv[CHUNK - 8:CHUNK])
                o_ref[pl.ds(r0, CHUNK), cols[0]] = (us[0] * _sigmoid(us[0]) * us[1]).astype(o_ref.dtype)
                return tuple(new_tails)

            lax.fori_loop(0, tm // CHUNK, step, tails, unroll=2)

    return pl.pallas_call(
        body, grid=(n // tm,),
        in_specs=[_prev_rows(tm, HALO, c2), _rows(tm, c2), _full((FFN_K, c2)), _full((1, c2))],
        out_specs=[_rows(tm, DFF), _rows(tm, c2)], out_shape=[S((n, DFF), BF), S((n, c2), BF)],
        compiler_params=_cparams("parallel"), name=name)(up, up, w, b)


def _ffn_act_bwd(up, u, dact, w, *, name):
    n = up.shape[0]
    tm = _tile(n, 512)
    nt = n // tm
    c2 = 2 * DFF

    def body(x_ref, uc_ref, un_ref, dc_ref, dn_ref, w_ref, o_ref, dw_ref, db_ref, dbuf, acc):
        i = pl.program_id(0)
        last = i == nt - 1

        @pl.when(i == 0)
        def _():
            acc[...] = jnp.zeros_like(acc)

        def swiglu_bwd(gate, val, dav):
            sg = _sigmoid(gate)
            return dav * val * (sg * (1.0 + gate * (1.0 - sg))), dav * gate * sg

        for s in range(DFF // STRIP):
            cols = (pl.ds(s * STRIP, STRIP), pl.ds(DFF + s * STRIP, STRIP))
            wk = [[w_ref[k:k + 1, c] for k in range(FFN_K)] for c in cols]

            def step1(j, carry):
                r0 = pl.multiple_of(j * CHUNK, CHUNK)
                rows = pl.ds(r0, CHUNK)
                dus = swiglu_bwd(uc_ref[rows, cols[0]].astype(F32), uc_ref[rows, cols[1]].astype(F32),
                                 dc_ref[rows, cols[0]].astype(F32))
                for h in range(2):
                    dbuf[rows, cols[h]] = dus[h]
                return carry

            lax.fori_loop(0, tm // CHUNK, step1, 0, unroll=2)
            dus = swiglu_bwd(un_ref[:, cols[0]].astype(F32), un_ref[:, cols[1]].astype(F32),
                             jnp.where(last, 0.0, dn_ref[:, cols[0]].astype(F32)))
            for h in range(2):
                dbuf[tm:tm + HALO, cols[h]] = dus[h]

            def step2(j, carry):
                r0 = pl.multiple_of(j * CHUNK, CHUNK)
                rows = pl.ds(r0, CHUNK)
                for h in range(2):
                    win = dbuf[pl.ds(r0, CHUNK + 8), cols[h]]
                    blocks = _blocks8(win)
                    d0 = win[0:CHUNK]
                    d1 = jnp.concatenate(_advance_rows(blocks, 1), axis=0)
                    d2 = jnp.concatenate(_advance_rows(blocks, 2), axis=0)
                    o_ref[rows, cols[h]] = (wk[h][2] * d0 + wk[h][1] * d1 + wk[h][0] * d2).astype(o_ref.dtype)
                    xv = x_ref[rows, cols[h]].astype(F32)
                    acc[2, :, cols[h]] += d0 * xv
                    acc[1, :, cols[h]] += d1 * xv
                    acc[0, :, cols[h]] += d2 * xv
                    acc[FFN_K, :, cols[h]] += d0
                return carry

            lax.fori_loop(0, tm // CHUNK, step2, 0)

        @pl.when(last)
        def _():
            for k in range(FFN_K):
                dw_ref[k:k + 1, :] = _colsum(acc[k])
            db_ref[...] = _colsum(acc[FFN_K])

    return pl.pallas_call(
        body, grid=(nt,),
        in_specs=[_rows(tm, c2), _rows(tm, c2), _next_rows(tm, HALO, c2, n),
                  _rows(tm, DFF), _next_rows(tm, HALO, DFF, n), _full((FFN_K, c2))],
        out_specs=[_rows(tm, c2), _full((FFN_K, c2)), _full((1, c2))],
        out_shape=[S((n, c2), BF), S((FFN_K, c2), F32), S((1, c2), F32)],
        scratch_shapes=[pltpu.VMEM((tm + HALO, c2), F32), pltpu.VMEM((FFN_K + 1, CHUNK, c2), F32)],
        compiler_params=_cparams("arbitrary"), name=name)(up, u, u, dact, dact, w)


def _head_consts():
    lane = jnp.arange(INNER) // HEAD_DIM
    rep = (jnp.arange(128)[:, None] == lane[None, :]).astype(BF)
    return rep, rep.T


def _split_dot(v, m):
    hi = v.astype(BF)
    lo = (v - hi.astype(F32)).astype(BF)
    return _dot(hi, m) + _dot(lo, m)


def _chunk_decay_terms(dt_raw, dtb, alog, rep):
    row = lax.broadcasted_iota(jnp.int32, (Q, Q), 0)
    col = lax.broadcasted_iota(jnp.int32, (Q, Q), 1)
    lower = row >= col
    upper = col >= row
    dt = _softplus(dt_raw + dtb)
    a = -jnp.exp(alog)
    adt = dt * a
    acum = lax.dot_general(lower.astype(F32), adt, NN, precision=HIGHEST, preferred_element_type=F32)
    acum_t = lax.dot_general(adt, upper.astype(F32), TN, precision=HIGHEST, preferred_element_type=F32)
    alast = acum[Q - 1:Q, :]
    e = jnp.exp(acum)
    f = jnp.exp(alast - acum)
    ex = _split_dot(jnp.concatenate([dt, e, f, jnp.broadcast_to(jnp.exp(alast), (8, 128))], axis=0), rep)
    return dict(lower=lower, upper=upper, dt=dt, a=a, acum=acum, acum_t=acum_t, alast=alast,
                dt_x=ex[0:Q], e_x=ex[Q:2 * Q], f_x=ex[2 * Q:3 * Q], cd_x=ex[3 * Q:3 * Q + 1])


def _block_diag2(v, lo):
    return jnp.concatenate([jnp.where(lo, v, 0.0), jnp.where(lo, 0.0, v)], axis=0).astype(BF)


def _ssd_fwd(xbc_act, proj, dt_bias, a_log, d_x, norm_g, *, name):
    n = xbc_act.shape[0]
    nc = n // Q
    rep, _ = _head_consts()

    def body(xs_ref, bc_ref, dt_ref, z_ref, dtb_ref, alog_ref, dx_ref, ng_ref, rep_ref, y_ref, yn_ref, hp_ref,
             h_scr, y_scr):
        i = pl.program_id(0)

        @pl.when(i == 0)
        def _():
            h_scr[...] = jnp.zeros_like(h_scr)

        hp_ref[...] = h_scr[...]
        t = _chunk_decay_terms(dt_ref[...].astype(F32), dtb_ref[...], alog_ref[...], rep_ref[...])
        xs = xs_ref[...].astype(F32)
        xt = xs * t["dt_x"]
        lo = lax.broadcasted_iota(jnp.int32, (Q, 128), 1) < HEAD_DIM
        gw = INNER // GROUPS
        for g in range(GROUPS):
            bm = bc_ref[:, g * NSTATE:(g + 1) * NSTATE]
            cm = bc_ref[:, GROUPS * NSTATE + g * NSTATE:GROUPS * NSTATE + (g + 1) * NSTATE]
            cb = _dot(cm, bm, NT)
            hg = h_scr[:, g * gw:(g + 1) * gw]
            yoff = _dot(cm, hg.astype(BF))
            for jj in range(gw // 128):
                p = g * (gw // 128) + jj
                sl = slice(p * 128, (p + 1) * 128)
                ws = []
                for hd in (2 * p, 2 * p + 1):
                    seg = t["acum"][:, hd:hd + 1] - t["acum_t"][hd:hd + 1, :]
                    ws.append((cb * jnp.exp(jnp.where(t["lower"], seg, -jnp.inf))).astype(BF))
                ydiag = _dot(jnp.concatenate(ws, axis=1), _block_diag2(xt[:, sl], lo))
                y_scr[:, sl] = ydiag + yoff[:, jj * 128:(jj + 1) * 128] * t["e_x"][:, sl] + dx_ref[:, sl] * xs[:, sl]
            xf = (xt[:, g * gw:(g + 1) * gw] * t["f_x"][:, g * gw:(g + 1) * gw]).astype(BF)
            h_scr[:, g * gw:(g + 1) * gw] = hg * t["cd_x"][:, g * gw:(g + 1) * gw] + _dot(bm, xf, TN)
        y = y_scr[...]
        y_ref[...] = y.astype(y_ref.dtype)
        z = z_ref[...].astype(F32)
        v = y * z * _sigmoid(z)
        for g in range(GROUPS):
            vg = v[:, g * gw:(g + 1) * gw]
            r = lax.rsqrt(jnp.mean(vg * vg, axis=-1, keepdims=True) + EPS)
            yn_ref[:, g * gw:(g + 1) * gw] = (vg * r * ng_ref[:, g * gw:(g + 1) * gw]).astype(yn_ref.dtype)

    vec = _full((1, INNER))
    hv = _full((1, 128))
    return pl.pallas_call(
        body, grid=(nc,),
        in_specs=[_rows(Q, INNER, 0), _rows(Q, 2 * GROUPS * NSTATE, INNER // (2 * GROUPS * NSTATE)),
                  _rows(Q, DT_PAD, P_DT // DT_PAD), _rows(Q, INNER, P_Z // INNER),
                  hv, hv, vec, vec, _full((128, INNER))],
        out_specs=[_rows(Q, INNER), _rows(Q, INNER), pl.BlockSpec((None, NSTATE, INNER), lambda i: (i, 0, 0))],
        out_shape=[S((n, INNER), BF), S((n, INNER), BF), S((nc, NSTATE, INNER), F32)],
        scratch_shapes=[pltpu.VMEM((NSTATE, INNER), F32), pltpu.VMEM((Q, INNER), F32)],
        compiler_params=_cparams("arbitrary"), name=name)(xbc_act, xbc_act, proj, proj, dt_bias, a_log, d_x, norm_g, rep)


def _ssd_bwd(xbc_act, proj, y, dyn, hprev, dproj, dt_bias, a_log, d_x, norm_g, *, name):
    n = xbc_act.shape[0]
    nc = n // Q
    rep, sel = _head_consts()
    gw = INNER // GROUPS

    def rev(w, cb=0):
        return pl.BlockSpec((Q, w), lambda i: (nc - 1 - i, cb))

    def body(xs_ref, bc_ref, dt_ref, z_ref, y_ref, dyn_ref, hp_ref, dtb_ref, alog_ref, dx_ref, ng_ref, rep_ref,
             sel_ref, dproj_in, dz_ref, ddt_ref, dxbc_ref, dng_ref, ddtb_ref, dalog_ref, dd_ref,
             dh_scr, dxt_scr, st_scr, off_scr, rs_scr, cs_scr, dng_acc, ddtb_acc, da_acc, dd_acc):
        del dproj_in
        i = pl.program_id(0)

        @pl.when(i == 0)
        def _():
            for r in (dh_scr, dng_acc, ddtb_acc, da_acc, dd_acc):
                r[...] = jnp.zeros_like(r)

        y = y_ref[...].astype(F32)
        z = z_ref[...].astype(F32)
        sz = _sigmoid(z)
        silu = z * sz
        v = y * silu
        dyn = dyn_ref[...].astype(F32)
        dvs = []
        for g in range(GROUPS):
            gs = slice(g * gw, (g + 1) * gw)
            vg = v[:, gs]
            r = lax.rsqrt(jnp.mean(vg * vg, axis=-1, keepdims=True) + EPS)
            vn = vg * r
            dvn = dyn[:, gs] * ng_ref[:, gs]
            dng_acc[:, gs] += _colsum(dyn[:, gs] * vn)
            dvs.append(r * (dvn - vn * jnp.mean(dvn * vn, axis=-1, keepdims=True)))
        dv = jnp.concatenate(dvs, axis=1)
        dy = dv * silu
        dz_ref[...] = (dv * y * (sz * (1.0 + z * (1.0 - sz)))).astype(dz_ref.dtype)

        dt_raw = dt_ref[...].astype(F32)
        t = _chunk_decay_terms(dt_raw, dtb_ref[...], alog_ref[...], rep_ref[...])
        xs = xs_ref[...].astype(F32)
        dsk = dx_ref[...]
        dd_acc[...] += _colsum(dy * xs)
        xt = xs * t["dt_x"]
        dye = dy * t["e_x"]
        xtf = xt * t["f_x"]
        hp = hp_ref[...]
        dh = dh_scr[...]
        lo = lax.broadcasted_iota(jnp.int32, (Q, 128), 1) < HEAD_DIM
        rs_scr[...] = jnp.zeros_like(rs_scr)
        cs_scr[...] = jnp.zeros_like(cs_scr)
        for g in range(GROUPS):
            gs = slice(g * gw, (g + 1) * gw)
            bm = bc_ref[:, g * NSTATE:(g + 1) * NSTATE]
            cm = bc_ref[:, GROUPS * NSTATE + g * NSTATE:GROUPS * NSTATE + (g + 1) * NSTATE]
            cbt = _dot(bm, cm, NT)
            dhg = dh[:, gs].astype(BF)
            hpg = hp[:, gs].astype(BF)
            dxt_state = _dot(bm, dhg) * t["f_x"][:, gs]
            st_scr[:, gs] = dxt_state
            dye_g = dye[:, gs]
            off_scr[:, gs] = dye_g * _dot(cm, hpg)
            dye_b = dye_g.astype(BF)
            db = _dot(xtf[:, gs].astype(BF), dhg, NT)
            dc = _dot(dye_b, hpg, NT)
            dh_scr[:, gs] = t["cd_x"][:, gs] * dh[:, gs] + _dot(cm, dye_b, TN)
            dcbt = jnp.zeros((Q, Q), F32)
            for jj in range(gw // 128):
                p = g * (gw // 128) + jj
                sl = slice(p * 128, (p + 1) * 128)
                lts, wfs = [], []
                for hd in (2 * p, 2 * p + 1):
                    seg_t = t["acum_t"][hd:hd + 1, :] - t["acum"][:, hd:hd + 1]
                    lt = jnp.exp(jnp.where(t["upper"], seg_t, -jnp.inf))
                    lts.append(lt)
                    wfs.append(cbt * lt)
                dyp = dy[:, sl]
                dxt_diag = _dot(jnp.concatenate([w.astype(BF) for w in wfs], axis=1), _block_diag2(dyp, lo))
                dwt2 = _dot(_block_diag2(xt[:, sl], lo), dyp.astype(BF), NT)
                for k, hd in enumerate((2 * p, 2 * p + 1)):
                    dwt = dwt2[k * Q:(k + 1) * Q]
                    dcbt = dcbt + dwt * lts[k]
                    mt = dwt * wfs[k]
                    rs_scr[hd:hd + 1, :] = _colsum(mt)
                    cs_scr[:, hd:hd + 1] = jnp.sum(mt, axis=1, keepdims=True)
                dxt_scr[:, sl] = dxt_diag + dxt_state[:, jj * 128:(jj + 1) * 128]
            dcbt_b = dcbt.astype(BF)
            db = db + _dot(dcbt_b, cm)
            dc = dc + _dot(dcbt_b, bm, TN)
            dxbc_ref[:, INNER + g * NSTATE:INNER + (g + 1) * NSTATE] = db.astype(dxbc_ref.dtype)
            dxbc_ref[:, INNER + (GROUPS + g) * NSTATE:INNER + (GROUPS + g + 1) * NSTATE] = dc.astype(dxbc_ref.dtype)
        dxt = dxt_scr[...]
        dst = st_scr[...]
        sel_m = sel_ref[...]
        sums = _split_dot(jnp.concatenate([off_scr[...], xs * dst, xs * dxt], axis=0), sel_m)
        r1_off, r3_state, r3 = sums[0:Q], sums[Q:2 * Q], sums[2 * Q:3 * Q]
        t1 = _colsum(xt * dst)
        t2 = _colsum(dh * hp)
        tails = _split_dot(jnp.concatenate([jnp.broadcast_to(t1, (8, INNER)), jnp.broadcast_to(t2, (8, INNER))], axis=0),
                           sel_m)
        extra = tails[0:1] + jnp.exp(t["alast"]) * tails[8:9]
        last_row = lax.broadcasted_iota(jnp.int32, (Q, 128), 0) == Q - 1
        da_cum = (rs_scr[...].T - cs_scr[...]) + r1_off - t["dt"] * r3_state + jnp.where(last_row, extra, 0.0)
        dadt = lax.dot_general(t["upper"].astype(F32), da_cum, NN, precision=HIGHEST, preferred_element_type=F32)
        ddt = r3 + t["a"] * dadt
        da_acc[...] += _colsum(dadt * t["dt"])
        real = lax.broadcasted_iota(jnp.int32, (Q, 128), 1) < HEADS
        ddraw = jnp.where(real, ddt * _sigmoid(dt_raw + dtb_ref[...]), 0.0)
        ddt_ref[...] = ddraw.astype(ddt_ref.dtype)
        ddtb_acc[...] += _colsum(ddraw)
        dxbc_ref[:, 0:INNER] = (dy * dsk + dxt * t["dt_x"]).astype(dxbc_ref.dtype)

        @pl.when(i == nc - 1)
        def _():
            dng_ref[...] = dng_acc[...]
            ddtb_ref[...] = ddtb_acc[...]
            dalog_ref[...] = da_acc[...] * t["a"]
            dd_ref[...] = _split_dot(jnp.broadcast_to(dd_acc[...], (8, INNER)), sel_m)[0:1]

    vec = _full((1, INNER))
    hv = _full((1, 128))
    return pl.pallas_call(
        body, grid=(nc,),
        in_specs=[rev(INNER, 0), rev(2 * GROUPS * NSTATE, INNER // (2 * GROUPS * NSTATE)),
                  rev(DT_PAD, P_DT // DT_PAD), rev(INNER, P_Z // INNER), rev(INNER), rev(INNER),
                  pl.BlockSpec((None, NSTATE, INNER), lambda i: (nc - 1 - i, 0, 0)),
                  hv, hv, vec, vec, _full((128, INNER)), _full((INNER, 128)), pl.BlockSpec(memory_space=pl.ANY)],
        out_specs=[rev(INNER, P_Z // INNER), rev(DT_PAD), rev(XBC), vec, hv, hv, hv],
        out_shape=[S(dproj.shape, dproj.dtype), S((n, DT_PAD), BF), S((n, XBC), BF),
                   S((1, INNER), F32), S((1, 128), F32), S((1, 128), F32), S((1, 128), F32)],
        scratch_shapes=[pltpu.VMEM((NSTATE, INNER), F32), pltpu.VMEM((Q, INNER), F32), pltpu.VMEM((Q, INNER), F32),
                        pltpu.VMEM((Q, INNER), F32), pltpu.VMEM((128, Q), F32), pltpu.VMEM((Q, 128), F32),
                        pltpu.VMEM((1, INNER), F32), pltpu.VMEM((1, 128), F32), pltpu.VMEM((1, 128), F32),
                        pltpu.VMEM((1, INNER), F32)],
        input_output_aliases={13: 0},
        compiler_params=_cparams("arbitrary"), name=name)(
            xbc_act, xbc_act, proj, proj, y, dyn, hprev, dt_bias, a_log, d_x, norm_g, rep, sel, dproj)


def _mixer_out_fwd(yn, a_conf, a_sc, proj, b_gate, w_ssd, w_conf, w_sc, w_o, x, mod, *, name):
    n = x.shape[0]
    tm = _tile(n, 256)

    def body(yn_ref, ac_ref, as_ref, gt_ref, bg_ref, wa_ref, wb_ref, wc_ref, wo_ref, x_ref, mod_ref,
             ya_ref, yb_ref, yc_ref, mg_ref, mix_ref, xn_ref):
        ya = _dot(yn_ref[...], wa_ref[...])
        yb = _dot(ac_ref[...], wb_ref[...])
        yc = _dot(as_ref[...], wc_ref[...])
        ya_ref[...] = ya.astype(ya_ref.dtype)
        yb_ref[...] = yb.astype(yb_ref.dtype)
        yc_ref[...] = yc.astype(yc_ref.dtype)
        g = _sigmoid(gt_ref[...].astype(F32) + bg_ref[...])
        merged = (g[:, 0:D] * ya + g[:, D:2 * D] * yb + g[:, 2 * D:] * yc).astype(mg_ref.dtype)
        mg_ref[...] = merged
        mix = _dot(merged, wo_ref[...])
        mix_ref[...] = mix.astype(mix_ref.dtype)
        xn_ref[...] = x_ref[...] + mod_ref[2:3, :] * mix

    act = S((n, D), BF)
    return pl.pallas_call(
        body, grid=(n // tm,),
        in_specs=[_rows(tm, INNER), _rows(tm, CONF_W), _rows(tm, SC_W), _rows(tm, 3 * D, P_GATES // (3 * D)),
                  _full((1, 3 * D)), _full((INNER, D)), _full((CONF_W, D)), _full((SC_W, D)), _full((D, D)),
                  _rows(tm, D), _full((3, D))],
        out_specs=[_rows(tm, D)] * 6,
        out_shape=[act, act, act, act, act, S((n, D), F32)],
        compiler_params=_cparams("parallel"), name=name)(yn, a_conf, a_sc, proj, b_gate, w_ssd, w_conf, w_sc, w_o, x, mod)


def _mixer_out_bwd(dx, mix, ya, yb, yc, proj, b_gate, w_ssd, w_conf, w_sc, w_o, mod, *, name):
    n = dx.shape[0]
    tm = _tile(n, 256)

    def body(dx_ref, mix_ref, ya_ref, yb_ref, yc_ref, gt_ref, bg_ref, wa_ref, wb_ref, wc_ref, wo_ref, mod_ref,
             do_ref, dya_ref, dyb_ref, dyc_ref, dgt_ref, dyn_ref, dac_ref, das_ref, dgm_ref, dbg_ref):
        i = pl.program_id(0)
        dxv = dx_ref[...]
        do = (dxv * mod_ref[2:3, :]).astype(BF)
        do_ref[...] = do
        dm = _dot(do, wo_ref[...], NT)
        g = _sigmoid(gt_ref[...].astype(F32) + bg_ref[...])
        @pl.when(i == 0)
        def _():
            dgm_ref[...] = jnp.zeros_like(dgm_ref)
            dbg_ref[...] = jnp.zeros_like(dbg_ref)

        dys = []
        for j, (y_ref, o_ref) in enumerate(((ya_ref, dya_ref), (yb_ref, dyb_ref), (yc_ref, dyc_ref))):
            gj = g[:, j * D:(j + 1) * D]
            dyj = (dm * gj).astype(BF)
            o_ref[...] = dyj
            dys.append(dyj)
            dgpre = dm * y_ref[...].astype(F32) * gj * (1.0 - gj)
            dgt_ref[:, j * D:(j + 1) * D] = dgpre.astype(dgt_ref.dtype)
            dbg_ref[:, j * D:(j + 1) * D] += _colsum(dgpre)
        dyn_ref[...] = _dot(dys[0], wa_ref[...], NT).astype(dyn_ref.dtype)
        dac_ref[...] = _dot(dys[1], wb_ref[...], NT).astype(dac_ref.dtype)
        das_ref[...] = _dot(dys[2], wc_ref[...], NT).astype(das_ref.dtype)
        dgm_ref[...] += _colsum(dxv * mix_ref[...].astype(F32))

    act = S((n, D), BF)
    return pl.pallas_call(
        body, grid=(n // tm,),
        in_specs=[_rows(tm, D)] * 5 + [_rows(tm, 3 * D, P_GATES // (3 * D)), _full((1, 3 * D)), _full((INNER, D)),
                                       _full((CONF_W, D)), _full((SC_W, D)), _full((D, D)), _full((3, D))],
        out_specs=[_rows(tm, D)] * 4 + [_rows(tm, 3 * D, P_GATES // (3 * D)), _rows(tm, INNER), _rows(tm, CONF_W),
                                        _rows(tm, SC_W), _full((1, D)), _full((1, 3 * D))],
        out_shape=[act, act, act, act, S((n, PW), BF), S((n, INNER), BF), S((n, CONF_W), BF), S((n, SC_W), BF),
                   S((1, D), F32), S((1, 3 * D), F32)],
        compiler_params=_cparams("arbitrary"), name=name)(dx, mix, ya, yb, yc, proj, b_gate, w_ssd, w_conf, w_sc, w_o, mod)


def _pad_w_in(w):
    zeros = jnp.zeros((w.shape[0], XBC_PAD - XBC - (R_CONF - R_DT)), w.dtype)
    return jnp.concatenate([w[:, R_GATES:], w[:, R_SC:R_GATES], w[:, R_XBC:R_DT], w[:, R_DT:R_CONF], zeros,
                            w[:, R_Z:R_XBC], w[:, R_CONF:R_SC]], axis=1)


def _unpad_w_in(wp):
    return jnp.concatenate([wp[:, P_Z:P_Z + INNER], wp[:, P_XBC:P_XBC + XBC], wp[:, P_DT:P_DT + HEADS],
                            wp[:, P_CONF:P_CONF + 2 * CONF_W], wp[:, P_SC:P_SC + 3 * SC_W], wp[:, P_GATES:P_GATES + 3 * D]],
                           axis=1)


W_IN_SHARD = N_IN // N_CHIPS
W_IN_SEGMENTS = ((R_Z, R_XBC, P_Z), (R_XBC, R_DT, P_XBC), (R_DT, R_CONF, P_DT), (R_CONF, R_SC, P_CONF),
                 (R_SC, R_GATES, P_SC), (R_GATES, N_IN, P_GATES))


def _pad_w_in_chips(w4):
    parts = []
    for lo, hi, dst in sorted(W_IN_SEGMENTS, key=lambda sgm: sgm[2]):
        for k in range(N_CHIPS):
            a, b = max(lo, k * W_IN_SHARD), min(hi, (k + 1) * W_IN_SHARD)
            if a < b:
                parts.append((dst + a - lo, w4[k][:, a - k * W_IN_SHARD:b - k * W_IN_SHARD]))
    out, pos = [], 0
    for start, piece in parts:
        if start > pos:
            out.append(jnp.zeros((w4.shape[1], start - pos), w4.dtype))
        out.append(piece)
        pos = start + piece.shape[1]
    if pos < PW:
        out.append(jnp.zeros((w4.shape[1], PW - pos), w4.dtype))
    return jnp.concatenate(out, axis=1)


def _unpad_w_in_chips(wp):
    blocks = []
    for k in range(N_CHIPS):
        pieces = []
        for lo, hi, dst in W_IN_SEGMENTS:
            a, b = max(lo, k * W_IN_SHARD), min(hi, (k + 1) * W_IN_SHARD)
            if a < b:
                pieces.append(wp[:, dst + a - lo:dst + b - lo])
        blocks.append(jnp.concatenate(pieces, axis=1))
    return jnp.stack(blocks)


def _row(v):
    return v.reshape(1, -1)


def _head_row(v):
    return jnp.pad(v, (0, 128 - HEADS)).reshape(1, 128)


def _layer_fwd(x, p, mod_mix, mod_ffn, tag):
    sv = {"x0": x}
    h = _modnorm_fwd(x, _row(p["norm_mix_g"]), mod_mix, name=f"modnorm_mix_fwd{tag}")
    proj = _mm_nn(h, p["w_in_pad"], out_dtype=BF, tm=1024, tn=2048, name=f"proj_fwd{tag}")
    xbc_act, cpre = _ssd_pre_fwd(proj, p["ssd_conv_w"], _row(p["ssd_conv_b"]), name=f"ssd_pre_fwd{tag}")
    d_x = _row(jnp.repeat(p["ssd_d"], HEAD_DIM))
    y, yn, hprev = _ssd_fwd(xbc_act, proj, _head_row(p["ssd_dt_bias"]), _head_row(p["ssd_a_log"]), d_x,
                            _row(p["ssd_norm_g"]), name=f"ssd_fwd{tag}")
    a_conf, uc = _conf_fwd(proj, p["conf_conv_w"], _row(p["conf_conv_b"]), _row(p["conf_ln_g"]), _row(p["conf_ln_b"]),
                           name=f"conf_fwd{tag}")
    a_sc = _sc_fwd(proj, p["sc_conv_w"], name=f"sc_fwd{tag}")
    ya, yb, yc, merged, mix, x1 = _mixer_out_fwd(yn, a_conf, a_sc, proj, _row(p["b_gate"]), p["w_ssd_out"],
                                                 p["w_conf_out"], p["w_sc_out"], p["w_o"], x, mod_mix,
                                                 name=f"mixer_out_fwd{tag}")
    h2 = _modnorm_fwd(x1, _row(p["norm_ffn_g"]), mod_ffn, name=f"modnorm_ffn_fwd{tag}")
    up = _mm_nn_chips(h2, p["w_up4"], out_dtype=BF, tm=1024, name=f"up_fwd{tag}")
    act, u_ffn = _ffn_act_fwd(up, p["ffn_conv_w"], _row(p["ffn_conv_b"]), name=f"ffn_act_fwd{tag}")
    o, x2 = _mm_resid(act, p["w_down"], x1, mod_ffn, tm=512, name=f"down_fwd{tag}")
    sv.update(h=h, proj=proj, xbc_act=xbc_act, cpre=cpre, d_x=d_x, y=y, yn=yn, hprev=hprev, a_conf=a_conf, uc=uc, a_sc=a_sc,
              ya=ya, yb=yb, yc=yc, merged=merged, mix=mix, x1=x1, h2=h2, up=up, u_ffn=u_ffn, act=act, o=o)
    return x2, sv


def _layer_bwd(dx, p, sv, mod_mix, mod_ffn, tag):
    g = {}
    do2, dgate_ffn = _gate_bwd(dx, sv["o"], mod_ffn, name=f"gate_ffn_bwd{tag}")
    dact = _mm_nt(do2, p["w_down"], out_dtype=BF, tm=1024, tk=D, name=f"down_dx{tag}")
    g["w_down"] = _mm_tn(sv["act"], do2, tn=D, tk=1024, name=f"down_dw{tag}")
    dup, g["ffn_conv_w"], dffn_b = _ffn_act_bwd(sv["up"], sv["u_ffn"], dact, p["ffn_conv_w"], name=f"ffn_act_bwd{tag}")
    g["ffn_conv_b"] = dffn_b[0]
    dh2 = _mm_nt_chips(dup, p["w_up4"], out_dtype=F32, tm=1024, name=f"up_dx{tag}")
    g["w_up4"] = _mm_tn(sv["h2"], dup, tn=2 * DFF // N_CHIPS, tk=2048, by_chip=True, name=f"up_dw{tag}")
    dx1, dgn, dsh, dsc = _modnorm_bwd(dh2, sv["x1"], dx, _row(p["norm_ffn_g"]), mod_ffn, name=f"modnorm_ffn_bwd{tag}")
    g["norm_ffn_g"] = dgn[0]
    dmod_ffn = jnp.concatenate([dsh[0], dsc[0], dgate_ffn[0]])

    (do1, dya, dyb, dyc, dproj, dyn, dac, dasc, dgate_mix, dbg) = _mixer_out_bwd(
        dx1, sv["mix"], sv["ya"], sv["yb"], sv["yc"], sv["proj"], _row(p["b_gate"]), p["w_ssd_out"], p["w_conf_out"],
        p["w_sc_out"], p["w_o"], mod_mix, name=f"mixer_out_bwd{tag}")
    g["b_gate"] = dbg[0]
    g["w_o"] = _mm_tn(sv["merged"], do1, tn=D, tk=2048, name=f"wo_dw{tag}")
    g["w_ssd_out"] = _mm_tn(sv["yn"], dya, tn=D, tk=2048, name=f"wssd_dw{tag}")
    g["w_conf_out"] = _mm_tn(sv["a_conf"], dyb, tn=D, tk=2048, name=f"wconf_dw{tag}")
    g["w_sc_out"] = _mm_tn(sv["a_sc"], dyc, tn=D, tk=2048, name=f"wsc_dw{tag}")
    dproj, g["conf_conv_w"], dcb, dlg, dlb = _conf_bwd(sv["proj"], sv["uc"], dac, dproj, p["conf_conv_w"],
                                                      _row(p["conf_ln_g"]), _row(p["conf_ln_b"]), name=f"conf_bwd{tag}")
    g["conf_conv_b"], g["conf_ln_g"], g["conf_ln_b"] = dcb[0], dlg[0], dlb[0]
    dproj, g["sc_conv_w"] = _sc_bwd(sv["proj"], dasc, dproj, p["sc_conv_w"], name=f"sc_bwd{tag}")
    dproj, ddt, dxbc_act, dng, ddtb, dalog, ddd = _ssd_bwd(
        sv["xbc_act"], sv["proj"], sv["y"], dyn, sv["hprev"], dproj, _head_row(p["ssd_dt_bias"]),
        _head_row(p["ssd_a_log"]), sv["d_x"], _row(p["ssd_norm_g"]), name=f"ssd_bwd{tag}")
    g["ssd_norm_g"], g["ssd_dt_bias"], g["ssd_a_log"], g["ssd_d"] = dng[0], ddtb[0, :HEADS], dalog[0, :HEADS], ddd[0, :HEADS]
    dproj, g["ssd_conv_w"], dsb = _ssd_pre_bwd(sv["proj"], sv["cpre"], dxbc_act, ddt, dproj, p["ssd_conv_w"],
                                               name=f"ssd_pre_bwd{tag}")
    g["ssd_conv_b"] = dsb[0]
    dh = _mm_nt(dproj, p["w_in_pad"], out_dtype=F32, tm=1024, tk=4096, name=f"proj_dx{tag}")
    g["w_in_pad"] = _mm_tn(sv["h"], dproj, tn=2048, tk=2048, name=f"proj_dw{tag}")
    dx0, dgn, dsh, dsc = _modnorm_bwd(dh, sv["x0"], dx1, _row(p["norm_mix_g"]), mod_mix, name=f"modnorm_mix_bwd{tag}")
    g["norm_mix_g"] = dgn[0]
    dmod_mix = jnp.concatenate([dsh[0], dsc[0], dgate_mix[0]])
    return dx0, g, dmod_mix, dmod_ffn


def _local_step(x, target, layers, mods, final_norm_g):
    saved = []
    for i, p in enumerate(layers):
        x, sv = _layer_fwd(x, p, mods[i][0], mods[i][1], f"_l{i}")
        saved.append(sv)
    dx, loss, dfg = _final_loss(x, _row(final_norm_g), target, name="final_loss")
    grads, dmods = [None] * len(layers), [None] * len(layers)
    for i in reversed(range(len(layers))):
        dx, grads[i], dmm, dmf = _layer_bwd(dx, layers[i], saved[i], mods[i][0], mods[i][1], f"_l{i}")
        dmods[i] = (dmm, dmf)
    return loss, dx, grads, dmods, dfg[0]


MESH = pl.DeviceIdType.MESH
ANY = pl.BlockSpec(memory_space=pl.ANY)
VMEM = pl.BlockSpec(memory_space=pltpu.VMEM)


def _mesh_pos():
    return lax.axis_index("x"), lax.axis_index("y"), lax.axis_index("c")


def _peer(pos, mask):
    return tuple(1 - v if (mask >> (2 - k)) & 1 else v for k, v in enumerate(pos))


def _lin(pos):
    return 4 * pos[0] + 2 * pos[1] + pos[2]


def _chip(pos):
    return 2 * pos[0] + pos[1]


def _rcopy(src, dst, send_sem, recv_sem, dev):
    return pltpu.make_async_remote_copy(src_ref=src, dst_ref=dst, send_sem=send_sem, recv_sem=recv_sem,
                                        device_id=dev, device_id_type=MESH)


CHIP_MASKS = (2, 4, 6)
SIBLING = 1
ADA_COLS = 3 * D // N_CHIPS
CONV_ROWS, CONV_COLS = 48, 2 * DFF // N_CHIPS
CONV_PACK = {"ffn_conv_w": (0, FFN_K, 2 * DFF // N_CHIPS), "ssd_conv_w": (3, SSD_K, XBC // N_CHIPS),
             "conf_conv_w": (8, CONF_K, CONF_W // N_CHIPS), "sc_conv_w": (40, SC_K, SC_W // N_CHIPS)}


def _ada_exchange(c_blk, ada_mix_w, ada_ffn_w, conv_pack):
    def body(c_ref, wm_ref, wf_ref, cw_ref, mods_ref, sc_ref, cwall_ref,
             call_scr, modp_scr, recv_scr, s1, r1, s3, r3, s4, r4):
        pos = _mesh_pos()
        me, km = _lin(pos), _chip(pos)
        call_scr[me] = c_ref[...]
        cwall_ref[km] = cw_ref[...]
        sends = []
        for m in range(1, N_DEV):
            sends.append(_rcopy(c_ref, call_scr.at[me], s1.at[m - 1], r1.at[m - 1], _peer(pos, m)))
        for j, m in enumerate(CHIP_MASKS):
            sends.append(_rcopy(cw_ref, cwall_ref.at[km], s4.at[j], r4.at[j], _peer(pos, m)))
        for cp in sends:
            cp.start()
        for m in range(1, N_DEV):
            src = _peer(pos, m)
            _rcopy(c_ref, call_scr.at[_lin(src)], s1.at[m - 1], r1.at[m - 1], src).wait_recv()
        cm = jnp.concatenate([call_scr[d, 0:1, :] for d in range(N_DEV)], axis=0)
        sc = cm * _sigmoid(cm)
        sc_ref[...] = sc
        for j, w in enumerate((wm_ref.at[0], wm_ref.at[1], wf_ref.at[0], wf_ref.at[1])):
            modp_scr[:, j * ADA_COLS:(j + 1) * ADA_COLS] = lax.dot_general(
                sc, w[...], NN, precision=HIGHEST, preferred_element_type=F32)
        recv_scr[km] = modp_scr[...]
        sends3 = [_rcopy(modp_scr, recv_scr.at[km], s3.at[j], r3.at[j], _peer(pos, m)) for j, m in enumerate(CHIP_MASKS)]
        for cp in sends3:
            cp.start()
        for j, m in enumerate(CHIP_MASKS):
            src = _peer(pos, m)
            _rcopy(modp_scr, recv_scr.at[_chip(src)], s3.at[j], r3.at[j], src).wait_recv()
            _rcopy(cw_ref, cwall_ref.at[_chip(src)], s4.at[j], r4.at[j], src).wait_recv()
        for k in range(N_CHIPS):
            mods_ref[k:k + 1, :] = recv_scr[k, pl.ds(me, 1), :]
        for cp in sends + sends3:
            cp.wait_send()

    dma = pltpu.SemaphoreType.DMA
    return pl.pallas_call(
        body, in_specs=[VMEM] * 4, out_specs=[VMEM] * 3,
        out_shape=[S((N_CHIPS, 4 * ADA_COLS), F32), S((N_DEV, D), F32), S((N_CHIPS,) + conv_pack.shape, F32)],
        scratch_shapes=[pltpu.VMEM((N_DEV, 8, D), F32), pltpu.VMEM((N_DEV, 4 * ADA_COLS), F32),
                        pltpu.VMEM((N_CHIPS, N_DEV, 4 * ADA_COLS), F32),
                        dma((N_DEV - 1,)), dma((N_DEV - 1,)), dma((3,)), dma((3,)), dma((3,)), dma((3,))],
        compiler_params=pltpu.CompilerParams(vmem_limit_bytes=VMEM_LIMIT_V7X), name="ada_exchange")(
            c_blk, ada_mix_w, ada_ffn_w, conv_pack)


def _gather_weights(shards):
    na = len(shards)

    def body(*refs):
        ins, outs = refs[:na], refs[na:2 * na]
        ssem, rsem, fsend, frecv = refs[2 * na:]
        pos = _mesh_pos()
        c, km = pos[2], _chip(pos)
        sib = _peer(pos, SIBLING)
        sends = []
        for a in range(na):
            for j, m in enumerate(CHIP_MASKS):
                sends.append(_rcopy(ins[a].at[c], outs[a].at[c, km], ssem.at[a, j], rsem.at[a, j], _peer(pos, m)))
        for cp in sends:
            cp.start()
        passed = []
        for a in range(na):
            for j, m in enumerate(CHIP_MASKS):
                src = _peer(pos, m)
                blk = outs[a].at[c, _chip(src)]
                _rcopy(ins[a].at[c], blk, ssem.at[a, j], rsem.at[a, j], src).wait_recv()
                fw = _rcopy(blk, blk, fsend.at[a, j], frecv.at[a, j], sib)
                fw.start()
                passed.append(fw)
        for a in range(na):
            for j, m in enumerate(CHIP_MASKS):
                blk = outs[a].at[1 - c, _chip(_peer(pos, m))]
                _rcopy(blk, blk, fsend.at[a, j], frecv.at[a, j], sib).wait_recv()
        for cp in sends + passed:
            cp.wait_send()

    dma = pltpu.SemaphoreType.DMA
    return pl.pallas_call(
        body, in_specs=[ANY] * na, out_specs=[ANY] * na,
        out_shape=[S((DEPTH, N_CHIPS) + s.shape[1:], s.dtype) for s in shards],
        scratch_shapes=[dma((na, 3)), dma((na, 3)), dma((na, 3)), dma((na, 3))],
        name="gather_weights")(*shards)


def _swap_half(arrs):
    na = len(arrs)

    def body(*refs):
        ins, outs = refs[:na], refs[na:2 * na]
        ssem, rsem = refs[2 * na:]
        pos = _mesh_pos()
        c = pos[2]
        sib = _peer(pos, SIBLING)
        cps = [_rcopy(ins[a].at[1 - c], outs[a], ssem.at[a], rsem.at[a], sib) for a in range(na)]
        for cp in cps:
            cp.start()
        for cp in cps:
            cp.wait()

    dma = pltpu.SemaphoreType.DMA
    return pl.pallas_call(
        body, in_specs=[ANY] * na, out_specs=[ANY] * na,
        out_shape=[S(s.shape[1:], s.dtype) for s in arrs],
        scratch_shapes=[dma((na,)), dma((na,))], name="swap_half")(*arrs)


def _scatter_chips(arrs):
    na = len(arrs)

    def body(*refs):
        ins, outs = refs[:na], refs[na:2 * na]
        ssem, rsem = refs[2 * na:]
        pos = _mesh_pos()
        km = _chip(pos)
        sends = []
        for a in range(na):
            for j, m in enumerate(CHIP_MASKS):
                dst = _peer(pos, m)
                sends.append(_rcopy(ins[a].at[_chip(dst)], outs[a].at[km], ssem.at[a, j], rsem.at[a, j], dst))
        for cp in sends:
            cp.start()
        for a in range(na):
            for j, m in enumerate(CHIP_MASKS):
                src = _peer(pos, m)
                _rcopy(ins[a].at[km], outs[a].at[_chip(src)], ssem.at[a, j], rsem.at[a, j], src).wait_recv()
        for cp in sends:
            cp.wait_send()

    dma = pltpu.SemaphoreType.DMA
    return pl.pallas_call(
        body, in_specs=[ANY] * na, out_specs=[ANY] * na, out_shape=[S(s.shape, s.dtype) for s in arrs],
        scratch_shapes=[dma((na, 3)), dma((na, 3))], name="scatter_chips")(*arrs)


def _share_sibling(arrs):
    na = len(arrs)

    def body(*refs):
        bufs = refs[na:2 * na]
        ssem, rsem = refs[2 * na:]
        pos = _mesh_pos()
        c = pos[2]
        sib = _peer(pos, SIBLING)
        sends = [_rcopy(bufs[a].at[c], bufs[a].at[c], ssem.at[a], rsem.at[a], sib) for a in range(na)]
        for cp in sends:
            cp.start()
        for a in range(na):
            _rcopy(bufs[a].at[c], bufs[a].at[1 - c], ssem.at[a], rsem.at[a], sib).wait_recv()
        for cp in sends:
            cp.wait_send()

    dma = pltpu.SemaphoreType.DMA
    return pl.pallas_call(
        body, in_specs=[ANY] * na, out_specs=[ANY] * na, out_shape=[S(s.shape, s.dtype) for s in arrs],
        input_output_aliases={a: a for a in range(na)},
        scratch_shapes=[dma((na,)), dma((na,))], name="share_sibling")(*arrs)


def _small_allreduce(vec):
    r = vec.shape[0]

    def body(v_ref, sum_ref, all_ref, ssem, rsem):
        pos = _mesh_pos()
        me = _lin(pos)
        all_ref[me] = v_ref[...]
        cps = [_rcopy(v_ref, all_ref.at[me], ssem.at[m - 1], rsem.at[m - 1], _peer(pos, m)) for m in range(1, N_DEV)]
        for cp in cps:
            cp.start()
        for m in range(1, N_DEV):
            src = _peer(pos, m)
            _rcopy(v_ref, all_ref.at[_lin(src)], ssem.at[m - 1], rsem.at[m - 1], src).wait_recv()
        acc = all_ref[0]
        for d in range(1, N_DEV):
            acc = acc + all_ref[d]
        sum_ref[...] = acc
        for cp in cps:
            cp.wait_send()

    dma = pltpu.SemaphoreType.DMA
    return pl.pallas_call(
        body, in_specs=[VMEM], out_specs=[VMEM, VMEM],
        out_shape=[S((r, 128), F32), S((N_DEV, r, 128), F32)],
        scratch_shapes=[dma((N_DEV - 1,)), dma((N_DEV - 1,))],
        compiler_params=pltpu.CompilerParams(vmem_limit_bytes=VMEM_LIMIT_V7X), name="small_allreduce")(vec)


ROW_BYTES_TARGET = 1 << 20


def _row_tile(rows, cols, itemsize=4):
    t = rows
    while t % 2 == 0 and t * cols * itemsize > ROW_BYTES_TARGET and (t // 2) % 16 == 0:
        t //= 2
    return t


def _pair_add(g, other, layer):
    _, r, cdim = g.shape
    tr = _row_tile(r, cdim)

    def body(l_ref, g_ref, o_ref, out_ref):
        del l_ref
        out_ref[...] = (g_ref[...].astype(F32) + o_ref[...].astype(F32)).astype(out_ref.dtype)

    return pl.pallas_call(
        body,
        grid_spec=pltpu.PrefetchScalarGridSpec(
            num_scalar_prefetch=1, grid=(r // tr,),
            in_specs=[pl.BlockSpec((None, tr, cdim), lambda i, l: (l[0], i, 0)), pl.BlockSpec((tr, cdim), lambda i, l: (i, 0))],
            out_specs=pl.BlockSpec((tr, cdim), lambda i, l: (i, 0))),
        out_shape=S((r, cdim), BF), compiler_params=_cparams("parallel"), name="pair_add")(layer, g, other)


def _sum4(q, own, chip_core):
    _, r, cdim = q.shape
    tr = _row_tile(r, cdim)

    def body(kc_ref, q_ref, own_ref, out_ref):
        mine = own_ref[...].astype(F32)
        terms = [jnp.where(kc_ref[0] == j, mine, q_ref[j].astype(F32)) for j in range(N_CHIPS)]
        out_ref[...] = ((terms[0] + terms[1]) + terms[2]) + terms[3]

    return pl.pallas_call(
        body,
        grid_spec=pltpu.PrefetchScalarGridSpec(
            num_scalar_prefetch=1, grid=(r // tr,),
            in_specs=[pl.BlockSpec((N_CHIPS, tr, cdim), lambda i, kc: (0, i, 0)),
                      pl.BlockSpec((None, tr, cdim), lambda i, kc: (kc[0], i, 0))],
            out_specs=pl.BlockSpec((None, tr, cdim), lambda i, kc: (kc[1], i, 0))),
        out_shape=S((DEPTH, r, cdim), F32), compiler_params=_cparams("parallel"), name="sum4")(chip_core, q, own)


def _ada_w_grad(silu_c, dmod_cols, chip):
    def body(k_ref, sc_ref, dm_ref, o_ref):
        del k_ref
        o_ref[...] = lax.dot_general(sc_ref[...], dm_ref[...], TN, precision=HIGHEST, preferred_element_type=F32)

    return pl.pallas_call(
        body,
        grid_spec=pltpu.PrefetchScalarGridSpec(
            num_scalar_prefetch=1, grid=(4,),
            in_specs=[pl.BlockSpec((N_DEV, D), lambda j, k: (0, 0)),
                      pl.BlockSpec((None, N_DEV, ADA_COLS), lambda j, k: (4 * j + k[0], 0, 0))],
            out_specs=pl.BlockSpec((None, D, ADA_COLS), lambda j, k: (j, 0, 0))),
        out_shape=S((4, D, ADA_COLS), F32), compiler_params=_cparams("parallel"), name="ada_w_grad")(chip, silu_c, dmod_cols)


def _adamw(w, g, m, v, *, name):
    r, cdim = w.shape
    tr = _row_tile(r, cdim)
    c1 = 1.0 / (1.0 - ADAM_B1 ** ADAM_STEP)
    c2 = 1.0 / (1.0 - ADAM_B2 ** ADAM_STEP)

    def body(w_ref, g_ref, m_ref, v_ref, d_ref, mo_ref, vo_ref):
        gv = g_ref[...]
        mn = ADAM_B1 * m_ref[...] + (1.0 - ADAM_B1) * gv
        vn = ADAM_B2 * v_ref[...] + (1.0 - ADAM_B2) * (gv * gv)
        mo_ref[...] = mn
        vo_ref[...] = vn
        d_ref[...] = -ADAM_LR * ((mn * c1) / (jnp.sqrt(vn * c2) + ADAM_EPS) + ADAM_WD * w_ref[...])

    blk = pl.BlockSpec((tr, cdim), lambda i: (i, 0))
    return pl.pallas_call(
        body, grid=(r // tr,), in_specs=[blk] * 4, out_specs=[blk] * 3, out_shape=[S((r, cdim), F32)] * 3,
        compiler_params=_cparams("parallel"), name=name)(w, g, m, v)


def _adamw_many(ws, gs, ms, vs):
    n = len(ws)
    c1 = 1.0 / (1.0 - ADAM_B1 ** ADAM_STEP)
    c2 = 1.0 / (1.0 - ADAM_B2 ** ADAM_STEP)

    def body(*refs):
        for i in range(n):
            w_ref, g_ref, m_ref, v_ref, d_ref, mo_ref, vo_ref = (refs[k * n + i] for k in range(7))
            gv = g_ref[...]
            mn = ADAM_B1 * m_ref[...] + (1.0 - ADAM_B1) * gv
            vn = ADAM_B2 * v_ref[...] + (1.0 - ADAM_B2) * (gv * gv)
            mo_ref[...] = mn
            vo_ref[...] = vn
            d_ref[...] = -ADAM_LR * ((mn * c1) / (jnp.sqrt(vn * c2) + ADAM_EPS) + ADAM_WD * w_ref[...])

    shapes = [S(a.shape, F32) for a in ws]
    outs = pl.pallas_call(
        body, in_specs=[VMEM] * (4 * n), out_specs=[VMEM] * (3 * n), out_shape=shapes * 3,
        compiler_params=pltpu.CompilerParams(vmem_limit_bytes=VMEM_LIMIT_V7X), name="adamw_small")(*ws, *gs, *ms, *vs)
    return outs[0:n], outs[n:2 * n], outs[2 * n:3 * n]


WEIGHTS = ['ada_mix_w', 'ada_mix_b', 'norm_mix_g', 'w_in', 'b_gate', 'ssd_conv_w', 'ssd_conv_b', 'ssd_dt_bias',
           'ssd_a_log', 'ssd_d', 'ssd_norm_g', 'w_ssd_out', 'conf_conv_w', 'conf_conv_b', 'conf_ln_g', 'conf_ln_b',
           'w_conf_out', 'sc_conv_w', 'w_sc_out', 'w_o', 'ada_ffn_w', 'ada_ffn_b', 'norm_ffn_g', 'w_up', 'ffn_conv_w',
           'ffn_conv_b', 'w_down', 'final_norm_g']
SMALL = ['ada_mix_b', 'norm_mix_g', 'b_gate', 'ssd_conv_b', 'ssd_dt_bias', 'ssd_a_log', 'ssd_d', 'ssd_norm_g', 'conf_conv_b',
         'conf_ln_g', 'conf_ln_b', 'ada_ffn_b', 'norm_ffn_g', 'ffn_conv_b']
CONVS = ['ssd_conv_w', 'conf_conv_w', 'sc_conv_w', 'ffn_conv_w']
BIG = ['ada_mix_w', 'ada_ffn_w', 'w_in', 'w_up', 'w_conf_out', 'w_sc_out', 'w_ssd_out', 'w_o', 'w_down']


def _pack_rows(pieces):
    flat = [p.reshape(-1) for p in pieces]
    offs, o = [], 0
    for f in flat:
        offs.append(o)
        o += f.shape[0]
    total = -(-o // 1024) * 1024
    vec = jnp.concatenate(flat + [jnp.zeros((total - o,), F32)])
    return vec.reshape(total // 128, 128), offs


def _by_chip(a, axis):
    shp = a.shape
    a = a.reshape(shp[:axis] + (N_CHIPS, shp[axis] // N_CHIPS) + shp[axis + 1:])
    return jnp.moveaxis(a, axis, 0)


def _from_chips(a, axis):
    a = jnp.moveaxis(a, 0, axis)
    shp = a.shape
    return a.reshape(shp[:axis] + (shp[axis] * shp[axis + 1],) + shp[axis + 2:])


def kernel(x, c, ada_mix_w, ada_mix_b, norm_mix_g, w_in, b_gate, ssd_conv_w, ssd_conv_b, ssd_dt_bias, ssd_a_log, ssd_d, ssd_norm_g, w_ssd_out, conf_conv_w, conf_conv_b, conf_ln_g, conf_ln_b, w_conf_out, sc_conv_w, w_sc_out, w_o, ada_ffn_w, ada_ffn_b, norm_ffn_g, w_up, ffn_conv_w, ffn_conv_b, w_down, final_norm_g, loss_target, m_ada_mix_w, m_ada_mix_b, m_norm_mix_g, m_w_in, m_b_gate, m_ssd_conv_w, m_ssd_conv_b, m_ssd_dt_bias, m_ssd_a_log, m_ssd_d, m_ssd_norm_g, m_w_ssd_out, m_conf_conv_w, m_conf_conv_b, m_conf_ln_g, m_conf_ln_b, m_w_conf_out, m_sc_conv_w, m_w_sc_out, m_w_o, m_ada_ffn_w, m_ada_ffn_b, m_norm_ffn_g, m_w_up, m_ffn_conv_w, m_ffn_conv_b, m_w_down, m_final_norm_g, v_ada_mix_w, v_ada_mix_b, v_norm_mix_g, v_w_in, v_b_gate, v_ssd_conv_w, v_ssd_conv_b, v_ssd_dt_bias, v_ssd_a_log, v_ssd_d, v_ssd_norm_g, v_w_ssd_out, v_conf_conv_w, v_conf_conv_b, v_conf_ln_g, v_conf_ln_b, v_w_conf_out, v_sc_conv_w, v_w_sc_out, v_w_o, v_ada_ffn_w, v_ada_ffn_b, v_norm_ffn_g, v_w_up, v_ffn_conv_w, v_ffn_conv_b, v_w_down, v_final_norm_g):
    args = locals()
    w = {n: args[n] for n in WEIGHTS}
    mom = {n: args["m_" + n] for n in WEIGHTS}
    var = {n: args["v_" + n] for n in WEIGHTS}
    pos = _mesh_pos()
    chip = _chip(pos)
    core = pos[2]

    conv_pack = jnp.zeros((DEPTH, CONV_ROWS, CONV_COLS), F32)
    for n, (r0, taps, width) in CONV_PACK.items():
        conv_pack = conv_pack.at[:, r0:r0 + taps, 0:width].set(w[n])
    c_blk = jnp.pad(c, ((0, 7), (0, 0)))
    mods_raw, silu_c, conv_all = _ada_exchange(c_blk, ada_mix_w, ada_ffn_w, conv_pack)
    ada_b = jnp.concatenate([ada_mix_b, ada_ffn_b], axis=0)
    mod_all = mods_raw.reshape(N_CHIPS, 4, ADA_COLS).transpose(1, 0, 2).reshape(4, 3 * D) + ada_b
    mod_all = mod_all.reshape(4, 3, D)
    mods = [(mod_all[i], mod_all[2 + i]) for i in range(DEPTH)]
    conv_full = {n: _from_chips(conv_all[:, :, r0:r0 + taps, 0:width], 2) for n, (r0, taps, width) in CONV_PACK.items()}

    cast = lambda a: a.astype(BF)
    shards = [cast(w_in), cast(w_up), jnp.concatenate([cast(w_conf_out), cast(w_sc_out)], axis=1),
              jnp.concatenate([cast(w_ssd_out), cast(w_o), cast(w_down)], axis=1)]
    gathered = _gather_weights(shards)
    g_in, g_up, g_cs, g_row = [lax.dynamic_update_slice(g, s[:, None], (0, chip, 0, 0)) for g, s in zip(gathered, shards)]
    layers = []
    for l in range(DEPTH):
        p = {n: w[n][l] for n in SMALL if not n.startswith("ada_")}
        p.update({n: conv_full[n][l] for n in CONVS})
        p["w_in_pad"] = _pad_w_in_chips(g_in[l])
        p["w_up4"] = g_up[l]
        p["w_conf_out"] = _from_chips(g_cs[l, :, 0:CONF_W], 1)
        p["w_sc_out"] = _from_chips(g_cs[l, :, CONF_W:], 1)
        p["w_ssd_out"] = _from_chips(g_row[l, :, 0:INNER // N_CHIPS], 0)
        p["w_o"] = _from_chips(g_row[l, :, INNER // N_CHIPS:(INNER + D) // N_CHIPS], 0)
        p["w_down"] = _from_chips(g_row[l, :, (INNER + D) // N_CHIPS:], 0)
        layers.append(p)

    seq = x.shape[1]
    loss, dx, grads, dmods, dfinal = _local_step(x.reshape(seq, D), loss_target.reshape(seq, D), layers, mods, final_norm_g)

    def stack(fn):
        return jnp.stack([fn(grads[l]).astype(BF) for l in range(DEPTH)])

    big = [stack(lambda g: _unpad_w_in_chips(g["w_in_pad"])),
           stack(lambda g: g["w_up4"]),
           stack(lambda g: _by_chip(jnp.concatenate([g["w_conf_out"], g["w_sc_out"]], axis=0), 1)),
           stack(lambda g: jnp.concatenate([_by_chip(g["w_ssd_out"], 0), _by_chip(g["w_o"], 0), _by_chip(g["w_down"], 0)],
                                           axis=1))]
    theirs = _swap_half(big)
    layer_idx = jnp.reshape(core, (1,)).astype(jnp.int32)
    pair = []
    for g2, t in zip(big, theirs):
        r, cdim = t.shape[1], t.shape[2]
        pair.append(_pair_add(g2.reshape(DEPTH, N_CHIPS * r, cdim), t.reshape(N_CHIPS * r, cdim), layer_idx)
                    .reshape(N_CHIPS, r, cdim))
    parts = _scatter_chips(pair)
    chip_core = jnp.stack([chip, core]).astype(jnp.int32)
    reduced = _share_sibling([_sum4(q, own, chip_core) for q, own in zip(parts, pair)])
    gw = {"w_in": reduced[0], "w_up": reduced[1],
          "w_conf_out": reduced[2][:, 0:CONF_W], "w_sc_out": reduced[2][:, CONF_W:],
          "w_ssd_out": reduced[3][:, 0:INNER // N_CHIPS], "w_o": reduced[3][:, INNER // N_CHIPS:(INNER + D) // N_CHIPS],
          "w_down": reduced[3][:, (INNER + D) // N_CHIPS:]}

    dmod = jnp.stack([dmods[0][0], dmods[1][0], dmods[0][1], dmods[1][1]])
    small_local = {n: jnp.stack([grads[l][n] for l in range(DEPTH)]) for n in SMALL if not n.startswith("ada_")}
    pieces = [loss[0]] + [small_local[n] for n in SMALL if not n.startswith("ada_")]
    pieces += [jnp.stack([grads[l][n] for l in range(DEPTH)]) for n in CONVS] + [dfinal, dmod]
    vec, offs = _pack_rows(pieces)
    vsum, vall = _small_allreduce(vec)
    flat = vsum.reshape(-1)

    def piece(k, like):
        return flat[offs[k]:offs[k] + like.size].reshape(like.shape)

    loss_out = flat[0]
    k = 1
    for n in SMALL:
        if not n.startswith("ada_"):
            gw[n] = piece(k, small_local[n])
            k += 1
    for n in CONVS:
        full = piece(k, conv_full[n])
        width = CONV_PACK[n][2]
        gw[n] = lax.dynamic_slice_in_dim(full, chip * width, width, axis=2)
        k += 1
    gw["final_norm_g"] = piece(k, dfinal)
    k += 1
    dmod_sum = piece(k, dmod)
    gw["ada_mix_b"], gw["ada_ffn_b"] = dmod_sum[0:2], dmod_sum[2:4]
    dmod_all = vall.reshape(N_DEV, -1)[:, offs[k]:offs[k] + dmod.size]
    dmod_cols = dmod_all.reshape(N_DEV, 4 * N_CHIPS, ADA_COLS).transpose(1, 0, 2)
    ada_g = _ada_w_grad(silu_c, dmod_cols, jnp.reshape(chip, (1,)).astype(jnp.int32))
    gw["ada_mix_w"], gw["ada_ffn_w"] = ada_g[0:2], ada_g[2:4]

    delta, new_m, new_v = {}, {}, {}
    for n in BIG:
        shp = w[n].shape
        two_d = lambda a: a.reshape(shp[0] * shp[1], shp[2])
        d_, m_, v_ = _adamw(two_d(w[n]), two_d(gw[n]), two_d(mom[n]), two_d(var[n]), name=f"adamw_{n}")
        delta[n], new_m[n], new_v[n] = d_.reshape(shp), m_.reshape(shp), v_.reshape(shp)
    rest = [n for n in WEIGHTS if n not in BIG]
    two_d = lambda a: a.reshape(1, -1) if a.ndim == 1 else a
    outs = _adamw_many(*[[two_d(src[n]) for n in rest] for src in (w, gw, mom, var)])
    for dst, group in zip((delta, new_m, new_v), outs):
        for n, o in zip(rest, group):
            dst[n] = o.reshape(w[n].shape)

    return (loss_out, dx[None], *[gw[n] for n in WEIGHTS], *[delta[n] for n in WEIGHTS],
            *[new_m[n] for n in WEIGHTS], *[new_v[n] for n in WEIGHTS])
```

```python
import functools

import jax
import jax.numpy as jnp
from jax import lax
from jax.experimental import pallas as pl
from jax.experimental.pallas import tpu as pltpu

F32 = jnp.float32
BF = jnp.bfloat16
S = jax.ShapeDtypeStruct

D = 1024
HEADS = 16
HEAD_DIM = 64
INNER = HEADS * HEAD_DIM
GROUPS = 2
NSTATE = 64
Q = 128
SSD_K = 4
XBC = INNER + 2 * GROUPS * NSTATE
CONF_W = 512
CONF_K = 31
SC_W = 512
SC_K = 3
DFF = 2816
FFN_K = 3
EPS = 1e-6
DEPTH = 2
R_Z, R_XBC, R_DT, R_CONF, R_SC, R_GATES, N_IN = 0, 1024, 2304, 2320, 3344, 4880, 7952
P_GATES, P_SC, P_XBC, P_Z, P_CONF, PW = 0, 3072, 4608, 6144, 7168, 8192
XBC_PAD = 1536
DT_PAD = 128
P_DT = P_XBC + XBC
N_CHIPS = 4
N_DEV = 8

ADAM_LR, ADAM_B1, ADAM_B2, ADAM_EPS, ADAM_WD, ADAM_STEP = 0.001, 0.9, 0.999, 1e-08, 0.01, 10

VMEM_LIMIT_V7X = 56 * 1024 * 1024
HIGHEST = lax.Precision.HIGHEST


def _cparams(*sem):
    return pltpu.CompilerParams(dimension_semantics=sem, vmem_limit_bytes=VMEM_LIMIT_V7X)


def _full(shape):
    n = len(shape)
    return pl.BlockSpec(shape, lambda *_: (0,) * n)


def _rows(tm, w, cb=0):
    return pl.BlockSpec((tm, w), lambda i: (i, cb))


def _prev_rows(tm, halo, w, cb=0):
    r = tm // halo
    return pl.BlockSpec((halo, w), lambda i: (jnp.maximum(i * r - 1, 0), cb))


def _next_rows(tm, halo, w, nrows, cb=0):
    r = tm // halo
    last = nrows // halo - 1
    return pl.BlockSpec((halo, w), lambda i: (jnp.minimum((i + 1) * r, last), cb))


def _sigmoid(v):
    return 1.0 / (1.0 + jnp.exp(-v))


def _softplus(v):
    return jnp.maximum(v, 0.0) + jnp.log(1.0 + jnp.exp(-jnp.abs(v)))


def _colsum(v):
    return jnp.sum(v, axis=0, keepdims=True)


def _tile(n, want):
    t = min(n, want)
    assert n % t == 0, (n, want)
    return t


NN = (((1,), (0,)), ((), ()))
NT = (((1,), (1,)), ((), ()))
TN = (((0,), (0,)), ((), ()))


def _dot(a, b, dims=NN):
    return lax.dot_general(a, b, dims, preferred_element_type=F32)


def _mm(a, b, *, dims, grid, a_spec, b_spec, o_spec, out_shape, acc_shape, name):
    nk = grid[2]

    def body(a_ref, b_ref, o_ref, acc_ref):
        k = pl.program_id(2)
        part = _dot(a_ref[...], b_ref[...], dims)
        if nk == 1:
            o_ref[...] = part.astype(o_ref.dtype)
        else:
            @pl.when(k == 0)
            def _():
                acc_ref[...] = part

            @pl.when(k > 0)
            def _():
                acc_ref[...] += part

            @pl.when(k == nk - 1)
            def _():
                o_ref[...] = acc_ref[...].astype(o_ref.dtype)

    return pl.pallas_call(
        body, grid=grid, in_specs=[a_spec, b_spec], out_specs=o_spec, out_shape=out_shape,
        scratch_shapes=[pltpu.VMEM(acc_shape if nk > 1 else (8, 128), F32)],
        compiler_params=_cparams("parallel", "parallel", "arbitrary"), name=name)(a, b)


def _mm_nn(a, b, *, out_dtype, tm, tn, name):
    m, k = a.shape
    n = b.shape[1]
    tm, tn = _tile(m, tm), _tile(n, tn)
    return _mm(a, b, dims=NN, grid=(m // tm, n // tn, 1),
               a_spec=pl.BlockSpec((tm, k), lambda i, j, kk: (i, 0)),
               b_spec=pl.BlockSpec((k, tn), lambda i, j, kk: (0, j)),
               o_spec=pl.BlockSpec((tm, tn), lambda i, j, kk: (i, j)),
               out_shape=S((m, n), out_dtype), acc_shape=(tm, tn), name=name)


def _mm_nt(a, b, *, out_dtype, tm, tk, name):
    m, kc = a.shape
    n = b.shape[0]
    tm, tk = _tile(m, tm), _tile(kc, tk)
    return _mm(a, b, dims=NT, grid=(m // tm, 1, kc // tk),
               a_spec=pl.BlockSpec((tm, tk), lambda i, j, kk: (i, kk)),
               b_spec=pl.BlockSpec((n, tk), lambda i, j, kk: (0, kk)),
               o_spec=pl.BlockSpec((tm, n), lambda i, j, kk: (i, 0)),
               out_shape=S((m, n), out_dtype), acc_shape=(tm, n), name=name)


def _mm_tn(a, b, *, tn, tk, name, out_dtype=BF, by_chip=False):
    kc, m = a.shape
    n = b.shape[1]
    tn, tk = _tile(n, tn), _tile(kc, tk)
    if by_chip:
        assert n == N_CHIPS * tn
        o_spec, out_shape = pl.BlockSpec((None, m, tn), lambda i, j, kk: (j, 0, 0)), S((N_CHIPS, m, tn), out_dtype)
    else:
        o_spec, out_shape = pl.BlockSpec((m, tn), lambda i, j, kk: (0, j)), S((m, n), out_dtype)
    return _mm(a, b, dims=TN, grid=(1, n // tn, kc // tk),
               a_spec=pl.BlockSpec((tk, m), lambda i, j, kk: (kk, 0)),
               b_spec=pl.BlockSpec((tk, tn), lambda i, j, kk: (kk, j)),
               o_spec=o_spec, out_shape=out_shape, acc_shape=(m, tn), name=name)


def _mm_nn_chips(a, b4, *, out_dtype, tm, name):
    m, k = a.shape
    n4 = b4.shape[2]
    tm = _tile(m, tm)
    return _mm(a, b4, dims=NN, grid=(m // tm, N_CHIPS, 1),
               a_spec=pl.BlockSpec((tm, k), lambda i, j, kk: (i, 0)),
               b_spec=pl.BlockSpec((None, k, n4), lambda i, j, kk: (j, 0, 0)),
               o_spec=pl.BlockSpec((tm, n4), lambda i, j, kk: (i, j)),
               out_shape=S((m, N_CHIPS * n4), out_dtype), acc_shape=(tm, n4), name=name)


def _mm_nt_chips(a, b4, *, out_dtype, tm, name):
    m = a.shape[0]
    n, n4 = b4.shape[1], b4.shape[2]
    tm = _tile(m, tm)
    return _mm(a, b4, dims=NT, grid=(m // tm, 1, N_CHIPS),
               a_spec=pl.BlockSpec((tm, n4), lambda i, j, kk: (i, kk)),
               b_spec=pl.BlockSpec((None, n, n4), lambda i, j, kk: (kk, 0, 0)),
               o_spec=pl.BlockSpec((tm, n), lambda i, j, kk: (i, 0)),
               out_shape=S((m, n), out_dtype), acc_shape=(tm, n), name=name)


def _mm_resid(a, b, x, mod, *, tm, name):
    m, k = a.shape
    n = b.shape[1]
    tm = _tile(m, tm)

    def body(a_ref, b_ref, x_ref, mod_ref, o_ref, xn_ref):
        o = _dot(a_ref[...], b_ref[...])
        o_ref[...] = o.astype(o_ref.dtype)
        xn_ref[...] = x_ref[...] + mod_ref[2:3, :] * o

    return pl.pallas_call(
        body, grid=(m // tm,),
        in_specs=[_rows(tm, k), _full((k, n)), _rows(tm, n), _full((3, n))],
        out_specs=[_rows(tm, n), _rows(tm, n)],
        out_shape=[S((m, n), BF), S((m, n), F32)],
        compiler_params=_cparams("parallel"), name=name)(a, b, x, mod)


def _modnorm_fwd(x, gain, mod, *, name):
    n = x.shape[0]
    tm = _tile(n, 512)

    def body(x_ref, g_ref, mod_ref, h_ref):
        xv = x_ref[...]
        r = lax.rsqrt(jnp.mean(xv * xv, axis=-1, keepdims=True) + EPS)
        y = xv * r * g_ref[...]
        h_ref[...] = (y * (1.0 + mod_ref[1:2, :]) + mod_ref[0:1, :]).astype(h_ref.dtype)

    return pl.pallas_call(
        body, grid=(n // tm,), in_specs=[_rows(tm, D), _full((1, D)), _full((3, D))],
        out_specs=_rows(tm, D), out_shape=S((n, D), BF), compiler_params=_cparams("parallel"), name=name)(x, gain, mod)


def _modnorm_bwd(dh, x, dres, gain, mod, *, name):
    n = x.shape[0]
    tm = _tile(n, 512)

    def body(dh_ref, x_ref, dres_ref, g_ref, mod_ref, dx_ref, dg_ref, dsh_ref, dsc_ref):
        i = pl.program_id(0)
        xv = x_ref[...]
        r = lax.rsqrt(jnp.mean(xv * xv, axis=-1, keepdims=True) + EPS)
        xh = xv * r
        dhv = dh_ref[...]
        g = g_ref[...]
        dy = dhv * (1.0 + mod_ref[1:2, :])
        dxh = dy * g
        dx = r * (dxh - xh * jnp.mean(dxh * xh, axis=-1, keepdims=True))
        dx_ref[...] = dres_ref[...] + dx

        @pl.when(i == 0)
        def _():
            dg_ref[...] = jnp.zeros_like(dg_ref)
            dsh_ref[...] = jnp.zeros_like(dsh_ref)
            dsc_ref[...] = jnp.zeros_like(dsc_ref)

        dg_ref[...] += _colsum(dy * xh)
        dsh_ref[...] += _colsum(dhv)
        dsc_ref[...] += _colsum(dhv * xh * g)

    vec = S((1, D), F32)
    return pl.pallas_call(
        body, grid=(n // tm,),
        in_specs=[_rows(tm, D), _rows(tm, D), _rows(tm, D), _full((1, D)), _full((3, D))],
        out_specs=[_rows(tm, D), _full((1, D)), _full((1, D)), _full((1, D))],
        out_shape=[S((n, D), F32), vec, vec, vec],
        compiler_params=_cparams("arbitrary"), name=name)(dh, x, dres, gain, mod)


def _final_loss(x, gain, target, *, name):
    n = x.shape[0]
    tm = _tile(n, 512)

    def body(x_ref, g_ref, t_ref, dx_ref, loss_ref, dg_ref):
        i = pl.program_id(0)
        xv = x_ref[...]
        g = g_ref[...]
        r = lax.rsqrt(jnp.mean(xv * xv, axis=-1, keepdims=True) + EPS)
        xh = xv * r
        err = xh * g - t_ref[...]
        dy = err * (1.0 / D)
        dxh = dy * g
        dx_ref[...] = r * (dxh - xh * jnp.mean(dxh * xh, axis=-1, keepdims=True))

        @pl.when(i == 0)
        def _():
            loss_ref[...] = jnp.zeros_like(loss_ref)
            dg_ref[...] = jnp.zeros_like(dg_ref)

        part = _colsum(jnp.sum(err * err, axis=-1, keepdims=True)) * (0.5 / D)
        loss_ref[...] += jnp.broadcast_to(part, loss_ref.shape)
        dg_ref[...] += _colsum(dy * xh)

    return pl.pallas_call(
        body, grid=(n // tm,),
        in_specs=[_rows(tm, D), _full((1, D)), _rows(tm, D)],
        out_specs=[_rows(tm, D), _full((1, 128)), _full((1, D))],
        out_shape=[S((n, D), F32), S((1, 128), F32), S((1, D), F32)],
        compiler_params=_cparams("arbitrary"), name=name)(x, gain, target)


def _gate_bwd(dx, o, mod, *, name):
    n = dx.shape[0]
    tm = _tile(n, 512)

    def body(dx_ref, o_ref, mod_ref, do_ref, dgt_ref):
        i = pl.program_id(0)
        dxv = dx_ref[...]
        do_ref[...] = (dxv * mod_ref[2:3, :]).astype(do_ref.dtype)

        @pl.when(i == 0)
        def _():
            dgt_ref[...] = jnp.zeros_like(dgt_ref)

        dgt_ref[...] += _colsum(dxv * o_ref[...].astype(F32))

    return pl.pallas_call(
        body, grid=(n // tm,), in_specs=[_rows(tm, D), _rows(tm, D), _full((3, D))],
        out_specs=[_rows(tm, D), _full((1, D))], out_shape=[S((n, D), BF), S((1, D), F32)],
        compiler_params=_cparams("arbitrary"), name=name)(dx, o, mod)


def _conv(buf, w_ref, taps, start, rows, ch):
    acc = None
    for k in range(taps):
        term = buf[pl.ds(start - (taps - 1) + k, rows), 0:ch] * w_ref[k:k + 1, :]
        acc = term if acc is None else acc + term
    return acc


def _conv_t(buf, w_ref, taps, start, rows, ch):
    acc = None
    for k in range(taps):
        term = buf[pl.ds(start + (taps - 1) - k, rows), 0:ch] * w_ref[k:k + 1, :]
        acc = term if acc is None else acc + term
    return acc


def _conv_dw(dw_ref, dy, xbuf, taps, xstart, rows, ch):
    for k in range(taps):
        dw_ref[k:k + 1, :] += _colsum(dy * xbuf[pl.ds(xstart - (taps - 1) + k, rows), 0:ch])


HALO = 16
CONF_HALO = 32
CHUNK = 16
STRIP = 256


def _blocks8(v):
    return [v[8 * i:8 * (i + 1)] for i in range(v.shape[0] // 8)]


def _delay_rows(blocks, s):
    sub = lax.broadcasted_iota(jnp.int32, blocks[0].shape, 0)
    rolled = [pltpu.roll(b, s, 0) for b in blocks]
    return [jnp.where(sub < s, rolled[i - 1], rolled[i]) for i in range(1, len(blocks))]


def _advance_rows(blocks, s):
    sub = lax.broadcasted_iota(jnp.int32, blocks[0].shape, 0)
    rolled = [pltpu.roll(b, 8 - s, 0) for b in blocks]
    return [jnp.where(sub < 8 - s, rolled[i], rolled[i + 1]) for i in range(len(blocks) - 1)]


def _conv3_chunk(tail, xv, wk):
    blocks = [tail] + _blocks8(xv)
    x1 = jnp.concatenate(_delay_rows(blocks, 1), axis=0)
    x2 = jnp.concatenate(_delay_rows(blocks, 2), axis=0)
    return wk[0] * x2 + wk[1] * x1 + wk[2] * xv


def _conv_chunk(tail, xv, wk):
    taps = len(wk)
    blocks = [tail] + _blocks8(xv)
    acc = wk[taps - 1] * xv
    for d in range(1, taps):
        acc = acc + wk[taps - 1 - d] * jnp.concatenate(_delay_rows(blocks, d), axis=0)
    return acc


def _ssd_pre_fwd(proj, w, b, *, name):
    n = proj.shape[0]
    tm = _tile(n, 512)
    cb = P_XBC // XBC_PAD

    def body(prev_ref, cur_ref, w_ref, b_ref, o_ref, c_ref, buf):
        i = pl.program_id(0)
        buf[0:HALO, :] = jnp.where(i == 0, 0.0, prev_ref[:, 0:XBC].astype(F32))
        buf[HALO:HALO + tm, :] = cur_ref[:, 0:XBC].astype(F32)
        c = _conv(buf, w_ref, SSD_K, HALO, tm, XBC) + b_ref[...]
        c_ref[...] = c.astype(c_ref.dtype)
        o_ref[...] = (c * _sigmoid(c)).astype(o_ref.dtype)

    return pl.pallas_call(
        body, grid=(n // tm,),
        in_specs=[_prev_rows(tm, HALO, XBC_PAD, cb), _rows(tm, XBC_PAD, cb), _full((SSD_K, XBC)), _full((1, XBC))],
        out_specs=[_rows(tm, XBC), _rows(tm, XBC)], out_shape=[S((n, XBC), BF), S((n, XBC), BF)],
        scratch_shapes=[pltpu.VMEM((HALO + tm, XBC), F32)],
        compiler_params=_cparams("parallel"), name=name)(proj, proj, w, b)


def _ssd_pre_bwd(proj, cpre, dact, ddt, dproj, w, *, name):
    n = proj.shape[0]
    tm = _tile(n, 512)
    nt = n // tm
    cb = P_XBC // XBC_PAD

    def body(x_ref, cc_ref, cn_ref, dc_ref, dn_ref, ddt_ref, w_ref, dproj_in, o_ref, dw_ref, db_ref, dbuf, acc):
        del dproj_in
        i = pl.program_id(0)
        last = i == nt - 1

        @pl.when(i == 0)
        def _():
            acc[...] = jnp.zeros_like(acc)

        def silu_bwd(cv, dav):
            sg = _sigmoid(cv)
            return dav * (sg * (1.0 + cv * (1.0 - sg)))

        for s in range(XBC // STRIP):
            c = pl.ds(s * STRIP, STRIP)
            wk = [w_ref[k:k + 1, c] for k in range(SSD_K)]

            def step1(j, carry):
                rows = pl.ds(pl.multiple_of(j * CHUNK, CHUNK), CHUNK)
                dbuf[rows, c] = silu_bwd(cc_ref[rows, c].astype(F32), dc_ref[rows, c].astype(F32))
                return carry

            lax.fori_loop(0, tm // CHUNK, step1, 0, unroll=2)
            dbuf[tm:tm + HALO, c] = silu_bwd(cn_ref[:, c].astype(F32), jnp.where(last, 0.0, dn_ref[:, c].astype(F32)))

            def step2(j, carry):
                r0 = pl.multiple_of(j * CHUNK, CHUNK)
                rows = pl.ds(r0, CHUNK)
                win = dbuf[pl.ds(r0, CHUNK + 8), c]
                blocks = _blocks8(win)
                xv = x_ref[rows, c].astype(F32)
                d0 = win[0:CHUNK]
                dx = wk[SSD_K - 1] * d0
                acc[SSD_K - 1, :, c] += d0 * xv
                acc[SSD_K, :, c] += d0
                for adv in range(1, SSD_K):
                    dk = jnp.concatenate(_advance_rows(blocks, adv), axis=0)
                    dx = dx + wk[SSD_K - 1 - adv] * dk
                    acc[SSD_K - 1 - adv, :, c] += dk * xv
                o_ref[rows, c] = dx.astype(o_ref.dtype)
                return carry

            lax.fori_loop(0, tm // CHUNK, step2, 0)

        o_ref[:, XBC:XBC + DT_PAD] = ddt_ref[...]
        o_ref[:, XBC + DT_PAD:XBC_PAD] = jnp.zeros((tm, XBC_PAD - XBC - DT_PAD), o_ref.dtype)

        @pl.when(last)
        def _():
            for k in range(SSD_K):
                dw_ref[k:k + 1, :] = _colsum(acc[k])
            db_ref[...] = _colsum(acc[SSD_K])

    return pl.pallas_call(
        body, grid=(nt,),
        in_specs=[_rows(tm, XBC_PAD, cb), _rows(tm, XBC), _next_rows(tm, HALO, XBC, n),
                  _rows(tm, XBC), _next_rows(tm, HALO, XBC, n), _rows(tm, DT_PAD),
                  _full((SSD_K, XBC)), pl.BlockSpec(memory_space=pl.ANY)],
        out_specs=[_rows(tm, XBC_PAD, cb), _full((SSD_K, XBC)), _full((1, XBC))],
        out_shape=[S(dproj.shape, dproj.dtype), S((SSD_K, XBC), F32), S((1, XBC), F32)],
        scratch_shapes=[pltpu.VMEM((tm + HALO, XBC), F32), pltpu.VMEM((SSD_K + 1, CHUNK, XBC), F32)],
        input_output_aliases={7: 0},
        compiler_params=_cparams("arbitrary"), name=name)(proj, cpre, cpre, dact, dact, ddt, w, dproj)


def _sc_fwd(proj, w, *, name):
    n = proj.shape[0]
    tm = _tile(n, 512)
    cb = P_SC // (3 * SC_W)

    def body(prev_ref, cur_ref, w_ref, o_ref, buf):
        i = pl.program_id(0)
        pv = prev_ref[...].astype(F32)
        cv = cur_ref[...].astype(F32)
        buf[0:HALO, :] = jnp.where(i == 0, 0.0, pv[:, SC_W:2 * SC_W] * pv[:, 2 * SC_W:])
        buf[HALO:HALO + tm, :] = cv[:, SC_W:2 * SC_W] * cv[:, 2 * SC_W:]
        q = _conv(buf, w_ref, SC_K, HALO, tm, SC_W)
        o_ref[...] = (cv[:, 0:SC_W] * q).astype(o_ref.dtype)

    return pl.pallas_call(
        body, grid=(n // tm,),
        in_specs=[_prev_rows(tm, HALO, 3 * SC_W, cb), _rows(tm, 3 * SC_W, cb), _full((SC_K, SC_W))],
        out_specs=_rows(tm, SC_W), out_shape=S((n, SC_W), BF),
        scratch_shapes=[pltpu.VMEM((HALO + tm, SC_W), F32)],
        compiler_params=_cparams("parallel"), name=name)(proj, proj, w)


def _sc_bwd(proj, da, dproj, w, *, name):
    n = proj.shape[0]
    tm = _tile(n, 512)
    nt = n // tm
    cb = P_SC // (3 * SC_W)

    def body(xp_ref, xc_ref, xn_ref, dc_ref, dn_ref, w_ref, dproj_in, o_ref, dw_ref, pbuf, dbuf):
        del dproj_in
        i = pl.program_id(0)
        pv = xp_ref[...].astype(F32)
        cv = xc_ref[...].astype(F32)
        nv = xn_ref[...].astype(F32)
        gb, gc, xv = cv[:, 0:SC_W], cv[:, SC_W:2 * SC_W], cv[:, 2 * SC_W:]
        pbuf[0:HALO, :] = jnp.where(i == 0, 0.0, pv[:, SC_W:2 * SC_W] * pv[:, 2 * SC_W:])
        pbuf[HALO:HALO + tm, :] = gc * xv
        q = _conv(pbuf, w_ref, SC_K, HALO, tm, SC_W)
        dav = dc_ref[...].astype(F32)
        dbuf[0:tm, :] = dav * gb
        dbuf[tm:tm + HALO, :] = jnp.where(i == nt - 1, 0.0, dn_ref[...].astype(F32) * nv[:, 0:SC_W])
        dp = _conv_t(dbuf, w_ref, SC_K, 0, tm, SC_W)
        o_ref[:, 0:SC_W] = (dav * q).astype(o_ref.dtype)
        o_ref[:, SC_W:2 * SC_W] = (dp * xv).astype(o_ref.dtype)
        o_ref[:, 2 * SC_W:] = (dp * gc).astype(o_ref.dtype)

        @pl.when(i == 0)
        def _():
            dw_ref[...] = jnp.zeros_like(dw_ref)

        _conv_dw(dw_ref, dbuf[0:tm, :], pbuf, SC_K, HALO, tm, SC_W)

    return pl.pallas_call(
        body, grid=(nt,),
        in_specs=[_prev_rows(tm, HALO, 3 * SC_W, cb), _rows(tm, 3 * SC_W, cb), _next_rows(tm, HALO, 3 * SC_W, n, cb),
                  _rows(tm, SC_W), _next_rows(tm, HALO, SC_W, n), _full((SC_K, SC_W)),
                  pl.BlockSpec(memory_space=pl.ANY)],
        out_specs=[_rows(tm, 3 * SC_W, cb), _full((SC_K, SC_W))],
        out_shape=[S(dproj.shape, dproj.dtype), S((SC_K, SC_W), F32)],
        scratch_shapes=[pltpu.VMEM((HALO + tm, SC_W), F32), pltpu.VMEM((tm + HALO, SC_W), F32)],
        input_output_aliases={6: 0},
        compiler_params=_cparams("arbitrary"), name=name)(proj, proj, proj, da, da, w, dproj)


def _conf_fwd(proj, w, b, ln_g, ln_b, *, name):
    n = proj.shape[0]
    tm = _tile(n, 512)
    cb = P_CONF // (2 * CONF_W)
    h = CONF_HALO

    def body(prev_ref, cur_ref, w_ref, b_ref, g_ref, be_ref, a_ref, uc_ref, buf):
        i = pl.program_id(0)
        pv = prev_ref[...].astype(F32)
        cv = cur_ref[...].astype(F32)
        buf[0:h, :] = jnp.where(i == 0, 0.0, pv[:, 0:CONF_W] * _sigmoid(pv[:, CONF_W:]))
        buf[h:h + tm, :] = cv[:, 0:CONF_W] * _sigmoid(cv[:, CONF_W:])
        uc = _conv(buf, w_ref, CONF_K, h, tm, CONF_W) + b_ref[...]
        uc_ref[...] = uc.astype(uc_ref.dtype)
        mu = jnp.mean(uc, axis=-1, keepdims=True)
        xc = uc - mu
        v = xc * lax.rsqrt(jnp.mean(xc * xc, axis=-1, keepdims=True) + EPS) * g_ref[...] + be_ref[...]
        a_ref[...] = (v * _sigmoid(v)).astype(a_ref.dtype)

    vec = _full((1, CONF_W))
    return pl.pallas_call(
        body, grid=(n // tm,),
        in_specs=[_prev_rows(tm, h, 2 * CONF_W, cb), _rows(tm, 2 * CONF_W, cb), _full((CONF_K, CONF_W)), vec, vec, vec],
        out_specs=[_rows(tm, CONF_W), _rows(tm, CONF_W)],
        out_shape=[S((n, CONF_W), BF), S((n, CONF_W), BF)],
        scratch_shapes=[pltpu.VMEM((h + tm, CONF_W), F32)],
        compiler_params=_cparams("parallel"), name=name)(proj, proj, w, b, ln_g, ln_b)


def _conf_bwd(proj, uc, da, dproj, w, ln_g, ln_b, *, name):
    n = proj.shape[0]
    tm = _tile(n, 512)
    nt = n // tm
    cb = P_CONF // (2 * CONF_W)
    h = CONF_HALO

    def body(xp_ref, xc_ref, ucc_ref, ucn_ref, dac_ref, dan_ref, w_ref, g_ref, be_ref, dproj_in,
             o_ref, dw_ref, db_ref, dg_ref, dbe_ref, ubuf, dbuf):
        del dproj_in
        i = pl.program_id(0)
        pv = xp_ref[...].astype(F32)
        cv = xc_ref[...].astype(F32)
        val, gt = cv[:, 0:CONF_W], cv[:, CONF_W:]
        sg = _sigmoid(gt)
        ubuf[0:h, :] = jnp.where(i == 0, 0.0, pv[:, 0:CONF_W] * _sigmoid(pv[:, CONF_W:]))
        ubuf[h:h + tm, :] = val * sg

        def ln_silu_bwd(ucv, dav):
            mu = jnp.mean(ucv, axis=-1, keepdims=True)
            xc = ucv - mu
            r = lax.rsqrt(jnp.mean(xc * xc, axis=-1, keepdims=True) + EPS)
            xh = xc * r
            v = xh * g_ref[...] + be_ref[...]
            s = _sigmoid(v)
            dv = dav * (s * (1.0 + v * (1.0 - s)))
            dxh = dv * g_ref[...]
            duc = r * (dxh - jnp.mean(dxh, axis=-1, keepdims=True) - xh * jnp.mean(dxh * xh, axis=-1, keepdims=True))
            return duc, dv, xh

        duc, dv, xh = ln_silu_bwd(ucc_ref[...].astype(F32), dac_ref[...].astype(F32))
        dbuf[0:tm, :] = duc
        ducn, _, _ = ln_silu_bwd(ucn_ref[...].astype(F32), dan_ref[...].astype(F32))
        dbuf[tm:tm + h, :] = jnp.where(i == nt - 1, 0.0, ducn)
        du = _conv_t(dbuf, w_ref, CONF_K, 0, tm, CONF_W)
        o_ref[:, 0:CONF_W] = (du * sg).astype(o_ref.dtype)
        o_ref[:, CONF_W:] = (du * val * sg * (1.0 - sg)).astype(o_ref.dtype)

        @pl.when(i == 0)
        def _():
            dw_ref[...] = jnp.zeros_like(dw_ref)
            db_ref[...] = jnp.zeros_like(db_ref)
            dg_ref[...] = jnp.zeros_like(dg_ref)
            dbe_ref[...] = jnp.zeros_like(dbe_ref)

        dg_ref[...] += _colsum(dv * xh)
        dbe_ref[...] += _colsum(dv)
        db_ref[...] += _colsum(duc)
        _conv_dw(dw_ref, duc, ubuf, CONF_K, h, tm, CONF_W)

    vec = _full((1, CONF_W))
    vshape = S((1, CONF_W), F32)
    return pl.pallas_call(
        body, grid=(nt,),
        in_specs=[_prev_rows(tm, h, 2 * CONF_W, cb), _rows(tm, 2 * CONF_W, cb),
                  _rows(tm, CONF_W), _next_rows(tm, h, CONF_W, n), _rows(tm, CONF_W), _next_rows(tm, h, CONF_W, n),
                  _full((CONF_K, CONF_W)), vec, vec, pl.BlockSpec(memory_space=pl.ANY)],
        out_specs=[_rows(tm, 2 * CONF_W, cb), _full((CONF_K, CONF_W)), vec, vec, vec],
        out_shape=[S(dproj.shape, dproj.dtype), S((CONF_K, CONF_W), F32), vshape, vshape, vshape],
        scratch_shapes=[pltpu.VMEM((h + tm, CONF_W), F32), pltpu.VMEM((tm + h, CONF_W), F32)],
        input_output_aliases={9: 0},
        compiler_params=_cparams("arbitrary"), name=name)(proj, proj, uc, uc, da, da, w, ln_g, ln_b, dproj)


def _ffn_act_fwd(up, w, b, *, name, comm=None):
    n = up.shape[0]
    tm = _tile(n, 512)
    c2 = 2 * DFF

    def body(prev_ref, cur_ref, w_ref, b_ref, o_ref, u_ref):
        first = pl.program_id(0) == 0
        for s in range(DFF // STRIP):
            cols = (pl.ds(s * STRIP, STRIP), pl.ds(DFF + s * STRIP, STRIP))
            wk = [[w_ref[k:k + 1, c] for k in range(FFN_K)] for c in cols]
            bk = [b_ref[:, c] for c in cols]
            tails = tuple(jnp.where(first, 0.0, prev_ref[:, c].astype(F32)[HALO - 8:HALO]) for c in cols)

            def step(j, tails):
                r0 = pl.multiple_of(j * CHUNK, CHUNK)
                us, new_tails = [], []
                for h in range(2):
                    xv = cur_ref[pl.ds(r0, CHUNK), cols[h]].astype(F32)
                    us.append(_conv3_chunk(tails[h], xv, wk[h]) + bk[h])
                    u_ref[pl.ds(r0, CHUNK), cols[h]] = us[h].astype(u_ref.dtype)
                    new_tails.append(xv[CHUNK - 8:CHUNK])
                o_ref[pl.ds(r0, CHUNK), cols[0]] = (us[0] * _sigmoid(us[0]) * us[1]).astype(o_ref.dtype)
                return tuple(new_tails)

            lax.fori_loop(0, tm // CHUNK, step, tails, unroll=2)

    return _call_with_comm(
        body, n // tm,
        in_specs=[_prev_rows(tm, HALO, c2), _rows(tm, c2), _full((FFN_K, c2)), _full((1, c2))],
        out_specs=[_rows(tm, DFF), _rows(tm, c2)], out_shape=[S((n, DFF), BF), S((n, c2), BF)],
        scratch_shapes=[], args=(up, up, w, b), comm=comm, name=name)


def _ffn_act_bwd(up, u, dact, w, *, name, comm=None):
    n = up.shape[0]
    tm = _tile(n, 512)
    nt = n // tm
    c2 = 2 * DFF

    def body(x_ref, uc_ref, un_ref, dc_ref, dn_ref, w_ref, o_ref, dw_ref, db_ref, dbuf, acc):
        i = pl.program_id(0)
        last = i == nt - 1

        @pl.when(i == 0)
        def _():
            acc[...] = jnp.zeros_like(acc)

        def swiglu_bwd(gate, val, dav):
            sg = _sigmoid(gate)
            return dav * val * (sg * (1.0 + gate * (1.0 - sg))), dav * gate * sg

        for s in range(DFF // STRIP):
            cols = (pl.ds(s * STRIP, STRIP), pl.ds(DFF + s * STRIP, STRIP))
            wk = [[w_ref[k:k + 1, c] for k in range(FFN_K)] for c in cols]

            def step1(j, carry):
                r0 = pl.multiple_of(j * CHUNK, CHUNK)
                rows = pl.ds(r0, CHUNK)
                dus = swiglu_bwd(uc_ref[rows, cols[0]].astype(F32), uc_ref[rows, cols[1]].astype(F32),
                                 dc_ref[rows, cols[0]].astype(F32))
                for h in range(2):
                    dbuf[rows, cols[h]] = dus[h]
                return carry

            lax.fori_loop(0, tm // CHUNK, step1, 0, unroll=2)
            dus = swiglu_bwd(un_ref[:, cols[0]].astype(F32), un_ref[:, cols[1]].astype(F32),
                             jnp.where(last, 0.0, dn_ref[:, cols[0]].astype(F32)))
            for h in range(2):
                dbuf[tm:tm + HALO, cols[h]] = dus[h]

            def step2(j, carry):
                r0 = pl.multiple_of(j * CHUNK, CHUNK)
                rows = pl.ds(r0, CHUNK)
                for h in range(2):
                    win = dbuf[pl.ds(r0, CHUNK + 8), cols[h]]
                    blocks = _blocks8(win)
                    d0 = win[0:CHUNK]
                    d1 = jnp.concatenate(_advance_rows(blocks, 1), axis=0)
                    d2 = jnp.concatenate(_advance_rows(blocks, 2), axis=0)
                    o_ref[rows, cols[h]] = (wk[h][2] * d0 + wk[h][1] * d1 + wk[h][0] * d2).astype(o_ref.dtype)
                    xv = x_ref[rows, cols[h]].astype(F32)
                    acc[2, :, cols[h]] += d0 * xv
                    acc[1, :, cols[h]] += d1 * xv
                    acc[0, :, cols[h]] += d2 * xv
                    acc[FFN_K, :, cols[h]] += d0
                return carry

            lax.fori_loop(0, tm // CHUNK, step2, 0)

        @pl.when(last)
        def _():
            for k in range(FFN_K):
                dw_ref[k:k + 1, :] = _colsum(acc[k])
            db_ref[...] = _colsum(acc[FFN_K])

    return _call_with_comm(
        body, nt,
        in_specs=[_rows(tm, c2), _rows(tm, c2), _next_rows(tm, HALO, c2, n),
                  _rows(tm, DFF), _next_rows(tm, HALO, DFF, n), _full((FFN_K, c2))],
        out_specs=[_rows(tm, c2), _full((FFN_K, c2)), _full((1, c2))],
        out_shape=[S((n, c2), BF), S((FFN_K, c2), F32), S((1, c2), F32)],
        scratch_shapes=[pltpu.VMEM((tm + HALO, c2), F32), pltpu.VMEM((FFN_K + 1, CHUNK, c2), F32)],
        args=(up, u, u, dact, dact, w), comm=comm, name=name)


def _head_consts():
    lane = jnp.arange(INNER) // HEAD_DIM
    rep = (jnp.arange(128)[:, None] == lane[None, :]).astype(BF)
    return rep, rep.T


def _split_dot(v, m):
    hi = v.astype(BF)
    lo = (v - hi.astype(F32)).astype(BF)
    return _dot(hi, m) + _dot(lo, m)


def _chunk_decay_terms(dt_raw, dtb, alog, rep):
    row = lax.broadcasted_iota(jnp.int32, (Q, Q), 0)
    col = lax.broadcasted_iota(jnp.int32, (Q, Q), 1)
    lower = row >= col
    upper = col >= row
    dt = _softplus(dt_raw + dtb)
    a = -jnp.exp(alog)
    adt = dt * a
    acum = lax.dot_general(lower.astype(F32), adt, NN, precision=HIGHEST, preferred_element_type=F32)
    acum_t = lax.dot_general(adt, upper.astype(F32), TN, precision=HIGHEST, preferred_element_type=F32)
    alast = acum[Q - 1:Q, :]
    e = jnp.exp(acum)
    f = jnp.exp(alast - acum)
    ex = _split_dot(jnp.concatenate([dt, e, f, jnp.broadcast_to(jnp.exp(alast), (8, 128))], axis=0), rep)
    return dict(lower=lower, upper=upper, dt=dt, a=a, acum=acum, acum_t=acum_t, alast=alast,
                dt_x=ex[0:Q], e_x=ex[Q:2 * Q], f_x=ex[2 * Q:3 * Q], cd_x=ex[3 * Q:3 * Q + 1])


def _block_diag2(v, lo):
    return jnp.concatenate([jnp.where(lo, v, 0.0), jnp.where(lo, 0.0, v)], axis=0).astype(BF)


def _ssd_fwd(xbc_act, proj, dt_bias, a_log, d_x, norm_g, *, name):
    n = xbc_act.shape[0]
    nc = n // Q
    rep, _ = _head_consts()

    def body(xs_ref, bc_ref, dt_ref, z_ref, dtb_ref, alog_ref, dx_ref, ng_ref, rep_ref, y_ref, yn_ref, hp_ref,
             h_scr, y_scr):
        i = pl.program_id(0)

        @pl.when(i == 0)
        def _():
            h_scr[...] = jnp.zeros_like(h_scr)

        hp_ref[...] = h_scr[...]
        t = _chunk_decay_terms(dt_ref[...].astype(F32), dtb_ref[...], alog_ref[...], rep_ref[...])
        xs = xs_ref[...].astype(F32)
        xt = xs * t["dt_x"]
        lo = lax.broadcasted_iota(jnp.int32, (Q, 128), 1) < HEAD_DIM
        gw = INNER // GROUPS
        for g in range(GROUPS):
            bm = bc_ref[:, g * NSTATE:(g + 1) * NSTATE]
            cm = bc_ref[:, GROUPS * NSTATE + g * NSTATE:GROUPS * NSTATE + (g + 1) * NSTATE]
            cb = _dot(cm, bm, NT)
            hg = h_scr[:, g * gw:(g + 1) * gw]
            yoff = _dot(cm, hg.astype(BF))
            for jj in range(gw // 128):
                p = g * (gw // 128) + jj
                sl = slice(p * 128, (p + 1) * 128)
                ws = []
                for hd in (2 * p, 2 * p + 1):
                    seg = t["acum"][:, hd:hd + 1] - t["acum_t"][hd:hd + 1, :]
                    ws.append((cb * jnp.exp(jnp.where(t["lower"], seg, -jnp.inf))).astype(BF))
                ydiag = _dot(jnp.concatenate(ws, axis=1), _block_diag2(xt[:, sl], lo))
                y_scr[:, sl] = ydiag + yoff[:, jj * 128:(jj + 1) * 128] * t["e_x"][:, sl] + dx_ref[:, sl] * xs[:, sl]
            xf = (xt[:, g * gw:(g + 1) * gw] * t["f_x"][:, g * gw:(g + 1) * gw]).astype(BF)
            h_scr[:, g * gw:(g + 1) * gw] = hg * t["cd_x"][:, g * gw:(g + 1) * gw] + _dot(bm, xf, TN)
        y = y_scr[...]
        y_ref[...] = y.astype(y_ref.dtype)
        z = z_ref[...].astype(F32)
        v = y * z * _sigmoid(z)
        for g in range(GROUPS):
            vg = v[:, g * gw:(g + 1) * gw]
            r = lax.rsqrt(jnp.mean(vg * vg, axis=-1, keepdims=True) + EPS)
            yn_ref[:, g * gw:(g + 1) * gw] = (vg * r * ng_ref[:, g * gw:(g + 1) * gw]).astype(yn_ref.dtype)

    vec = _full((1, INNER))
    hv = _full((1, 128))
    return pl.pallas_call(
        body, grid=(nc,),
        in_specs=[_rows(Q, INNER, 0), _rows(Q, 2 * GROUPS * NSTATE, INNER // (2 * GROUPS * NSTATE)),
                  _rows(Q, DT_PAD, P_DT // DT_PAD), _rows(Q, INNER, P_Z // INNER),
                  hv, hv, vec, vec, _full((128, INNER))],
        out_specs=[_rows(Q, INNER), _rows(Q, INNER), pl.BlockSpec((None, NSTATE, INNER), lambda i: (i, 0, 0))],
        out_shape=[S((n, INNER), BF), S((n, INNER), BF), S((nc, NSTATE, INNER), F32)],
        scratch_shapes=[pltpu.VMEM((NSTATE, INNER), F32), pltpu.VMEM((Q, INNER), F32)],
        compiler_params=_cparams("arbitrary"), name=name)(xbc_act, xbc_act, proj, proj, dt_bias, a_log, d_x, norm_g, rep)


def _ssd_bwd(xbc_act, proj, y, dyn, hprev, dproj, dt_bias, a_log, d_x, norm_g, *, name):
    n = xbc_act.shape[0]
    nc = n // Q
    rep, sel = _head_consts()
    gw = INNER // GROUPS

    def rev(w, cb=0):
        return pl.BlockSpec((Q, w), lambda i: (nc - 1 - i, cb))

    def body(xs_ref, bc_ref, dt_ref, z_ref, y_ref, dyn_ref, hp_ref, dtb_ref, alog_ref, dx_ref, ng_ref, rep_ref,
             sel_ref, dproj_in, dz_ref, ddt_ref, dxbc_ref, dng_ref, ddtb_ref, dalog_ref, dd_ref,
             dh_scr, dxt_scr, st_scr, off_scr, rs_scr, cs_scr, dng_acc, ddtb_acc, da_acc, dd_acc):
        del dproj_in
        i = pl.program_id(0)

        @pl.when(i == 0)
        def _():
            for r in (dh_scr, dng_acc, ddtb_acc, da_acc, dd_acc):
                r[...] = jnp.zeros_like(r)

        y = y_ref[...].astype(F32)
        z = z_ref[...].astype(F32)
        sz = _sigmoid(z)
        silu = z * sz
        v = y * silu
        dyn = dyn_ref[...].astype(F32)
        dvs = []
        for g in range(GROUPS):
            gs = slice(g * gw, (g + 1) * gw)
            vg = v[:, gs]
            r = lax.rsqrt(jnp.mean(vg * vg, axis=-1, keepdims=True) + EPS)
            vn = vg * r
            dvn = dyn[:, gs] * ng_ref[:, gs]
            dng_acc[:, gs] += _colsum(dyn[:, gs] * vn)
            dvs.append(r * (dvn - vn * jnp.mean(dvn * vn, axis=-1, keepdims=True)))
        dv = jnp.concatenate(dvs, axis=1)
        dy = dv * silu
        dz_ref[...] = (dv * y * (sz * (1.0 + z * (1.0 - sz)))).astype(dz_ref.dtype)

        dt_raw = dt_ref[...].astype(F32)
        t = _chunk_decay_terms(dt_raw, dtb_ref[...], alog_ref[...], rep_ref[...])
        xs = xs_ref[...].astype(F32)
        dsk = dx_ref[...]
        dd_acc[...] += _colsum(dy * xs)
        xt = xs * t["dt_x"]
        dye = dy * t["e_x"]
        xtf = xt * t["f_x"]
        hp = hp_ref[...]
        dh = dh_scr[...]
        lo = lax.broadcasted_iota(jnp.int32, (Q, 128), 1) < HEAD_DIM
        rs_scr[...] = jnp.zeros_like(rs_scr)
        cs_scr[...] = jnp.zeros_like(cs_scr)
        for g in range(GROUPS):
            gs = slice(g * gw, (g + 1) * gw)
            bm = bc_ref[:, g * NSTATE:(g + 1) * NSTATE]
            cm = bc_ref[:, GROUPS * NSTATE + g * NSTATE:GROUPS * NSTATE + (g + 1) * NSTATE]
            cbt = _dot(bm, cm, NT)
            dhg = dh[:, gs].astype(BF)
            hpg = hp[:, gs].astype(BF)
            dxt_state = _dot(bm, dhg) * t["f_x"][:, gs]
            st_scr[:, gs] = dxt_state
            dye_g = dye[:, gs]
            off_scr[:, gs] = dye_g * _dot(cm, hpg)
            dye_b = dye_g.astype(BF)
            db = _dot(xtf[:, gs].astype(BF), dhg, NT)
            dc = _dot(dye_b, hpg, NT)
            dh_scr[:, gs] = t["cd_x"][:, gs] * dh[:, gs] + _dot(cm, dye_b, TN)
            dcbt = jnp.zeros((Q, Q), F32)
            for jj in range(gw // 128):
                p = g * (gw // 128) + jj
                sl = slice(p * 128, (p + 1) * 128)
                lts, wfs = [], []
                for hd in (2 * p, 2 * p + 1):
                    seg_t = t["acum_t"][hd:hd + 1, :] - t["acum"][:, hd:hd + 1]
                    lt = jnp.exp(jnp.where(t["upper"], seg_t, -jnp.inf))
                    lts.append(lt)
                    wfs.append(cbt * lt)
                dyp = dy[:, sl]
                dxt_diag = _dot(jnp.concatenate([w.astype(BF) for w in wfs], axis=1), _block_diag2(dyp, lo))
                dwt2 = _dot(_block_diag2(xt[:, sl], lo), dyp.astype(BF), NT)
                for k, hd in enumerate((2 * p, 2 * p + 1)):
                    dwt = dwt2[k * Q:(k + 1) * Q]
                    dcbt = dcbt + dwt * lts[k]
                    mt = dwt * wfs[k]
                    rs_scr[hd:hd + 1, :] = _colsum(mt)
                    cs_scr[:, hd:hd + 1] = jnp.sum(mt, axis=1, keepdims=True)
                dxt_scr[:, sl] = dxt_diag + dxt_state[:, jj * 128:(jj + 1) * 128]
            dcbt_b = dcbt.astype(BF)
            db = db + _dot(dcbt_b, cm)
            dc = dc + _dot(dcbt_b, bm, TN)
            dxbc_ref[:, INNER + g * NSTATE:INNER + (g + 1) * NSTATE] = db.astype(dxbc_ref.dtype)
            dxbc_ref[:, INNER + (GROUPS + g) * NSTATE:INNER + (GROUPS + g + 1) * NSTATE] = dc.astype(dxbc_ref.dtype)
        dxt = dxt_scr[...]
        dst = st_scr[...]
        sel_m = sel_ref[...]
        sums = _split_dot(jnp.concatenate([off_scr[...], xs * dst, xs * dxt], axis=0), sel_m)
        r1_off, r3_state, r3 = sums[0:Q], sums[Q:2 * Q], sums[2 * Q:3 * Q]
        t1 = _colsum(xt * dst)
        t2 = _colsum(dh * hp)
        tails = _split_dot(jnp.concatenate([jnp.broadcast_to(t1, (8, INNER)), jnp.broadcast_to(t2, (8, INNER))], axis=0),
                           sel_m)
        extra = tails[0:1] + jnp.exp(t["alast"]) * tails[8:9]
        last_row = lax.broadcasted_iota(jnp.int32, (Q, 128), 0) == Q - 1
        da_cum = (rs_scr[...].T - cs_scr[...]) + r1_off - t["dt"] * r3_state + jnp.where(last_row, extra, 0.0)
        dadt = lax.dot_general(t["upper"].astype(F32), da_cum, NN, precision=HIGHEST, preferred_element_type=F32)
        ddt = r3 + t["a"] * dadt
        da_acc[...] += _colsum(dadt * t["dt"])
        real = lax.broadcasted_iota(jnp.int32, (Q, 128), 1) < HEADS
        ddraw = jnp.where(real, ddt * _sigmoid(dt_raw + dtb_ref[...]), 0.0)
        ddt_ref[...] = ddraw.astype(ddt_ref.dtype)
        ddtb_acc[...] += _colsum(ddraw)
        dxbc_ref[:, 0:INNER] = (dy * dsk + dxt * t["dt_x"]).astype(dxbc_ref.dtype)

        @pl.when(i == nc - 1)
        def _():
            dng_ref[...] = dng_acc[...]
            ddtb_ref[...] = ddtb_acc[...]
            dalog_ref[...] = da_acc[...] * t["a"]
            dd_ref[...] = _split_dot(jnp.broadcast_to(dd_acc[...], (8, INNER)), sel_m)[0:1]

    vec = _full((1, INNER))
    hv = _full((1, 128))
    return pl.pallas_call(
        body, grid=(nc,),
        in_specs=[rev(INNER, 0), rev(2 * GROUPS * NSTATE, INNER // (2 * GROUPS * NSTATE)),
                  rev(DT_PAD, P_DT // DT_PAD), rev(INNER, P_Z // INNER), rev(INNER), rev(INNER),
                  pl.BlockSpec((None, NSTATE, INNER), lambda i: (nc - 1 - i, 0, 0)),
                  hv, hv, vec, vec, _full((128, INNER)), _full((INNER, 128)), pl.BlockSpec(memory_space=pl.ANY)],
        out_specs=[rev(INNER, P_Z // INNER), rev(DT_PAD), rev(XBC), vec, hv, hv, hv],
        out_shape=[S(dproj.shape, dproj.dtype), S((n, DT_PAD), BF), S((n, XBC), BF),
                   S((1, INNER), F32), S((1, 128), F32), S((1, 128), F32), S((1, 128), F32)],
        scratch_shapes=[pltpu.VMEM((NSTATE, INNER), F32), pltpu.VMEM((Q, INNER), F32), pltpu.VMEM((Q, INNER), F32),
                        pltpu.VMEM((Q, INNER), F32), pltpu.VMEM((128, Q), F32), pltpu.VMEM((Q, 128), F32),
                        pltpu.VMEM((1, INNER), F32), pltpu.VMEM((1, 128), F32), pltpu.VMEM((1, 128), F32),
                        pltpu.VMEM((1, INNER), F32)],
        input_output_aliases={13: 0},
        compiler_params=_cparams("arbitrary"), name=name)(
            xbc_act, xbc_act, proj, proj, y, dyn, hprev, dt_bias, a_log, d_x, norm_g, rep, sel, dproj)


def _mixer_out_fwd(yn, a_conf, a_sc, proj, b_gate, w_ssd, w_conf, w_sc, w_o, x, mod, *, name):
    n = x.shape[0]
    tm = _tile(n, 256)

    def body(yn_ref, ac_ref, as_ref, gt_ref, bg_ref, wa_ref, wb_ref, wc_ref, wo_ref, x_ref, mod_ref,
             ya_ref, yb_ref, yc_ref, mg_ref, mix_ref, xn_ref):
        ya = _dot(yn_ref[...], wa_ref[...])
        yb = _dot(ac_ref[...], wb_ref[...])
        yc = _dot(as_ref[...], wc_ref[...])
        ya_ref[...] = ya.astype(ya_ref.dtype)
        yb_ref[...] = yb.astype(yb_ref.dtype)
        yc_ref[...] = yc.astype(yc_ref.dtype)
        g = _sigmoid(gt_ref[...].astype(F32) + bg_ref[...])
        merged = (g[:, 0:D] * ya + g[:, D:2 * D] * yb + g[:, 2 * D:] * yc).astype(mg_ref.dtype)
        mg_ref[...] = merged
        mix = _dot(merged, wo_ref[...])
        mix_ref[...] = mix.astype(mix_ref.dtype)
        xn_ref[...] = x_ref[...] + mod_ref[2:3, :] * mix

    act = S((n, D), BF)
    return pl.pallas_call(
        body, grid=(n // tm,),
        in_specs=[_rows(tm, INNER), _rows(tm, CONF_W), _rows(tm, SC_W), _rows(tm, 3 * D, P_GATES // (3 * D)),
                  _full((1, 3 * D)), _full((INNER, D)), _full((CONF_W, D)), _full((SC_W, D)), _full((D, D)),
                  _rows(tm, D), _full((3, D))],
        out_specs=[_rows(tm, D)] * 6,
        out_shape=[act, act, act, act, act, S((n, D), F32)],
        compiler_params=_cparams("parallel"), name=name)(yn, a_conf, a_sc, proj, b_gate, w_ssd, w_conf, w_sc, w_o, x, mod)


def _mixer_out_bwd(dx, mix, ya, yb, yc, proj, b_gate, w_ssd, w_conf, w_sc, w_o, mod, *, name):
    n = dx.shape[0]
    tm = _tile(n, 256)

    def body(dx_ref, mix_ref, ya_ref, yb_ref, yc_ref, gt_ref, bg_ref, wa_ref, wb_ref, wc_ref, wo_ref, mod_ref,
             do_ref, dya_ref, dyb_ref, dyc_ref, dgt_ref, dyn_ref, dac_ref, das_ref, dgm_ref, dbg_ref):
        i = pl.program_id(0)
        dxv = dx_ref[...]
        do = (dxv * mod_ref[2:3, :]).astype(BF)
        do_ref[...] = do
        dm = _dot(do, wo_ref[...], NT)
        g = _sigmoid(gt_ref[...].astype(F32) + bg_ref[...])
        @pl.when(i == 0)
        def _():
            dgm_ref[...] = jnp.zeros_like(dgm_ref)
            dbg_ref[...] = jnp.zeros_like(dbg_ref)

        dys = []
        for j, (y_ref, o_ref) in enumerate(((ya_ref, dya_ref), (yb_ref, dyb_ref), (yc_ref, dyc_ref))):
            gj = g[:, j * D:(j + 1) * D]
            dyj = (dm * gj).astype(BF)
            o_ref[...] = dyj
            dys.append(dyj)
            dgpre = dm * y_ref[...].astype(F32) * gj * (1.0 - gj)
            dgt_ref[:, j * D:(j + 1) * D] = dgpre.astype(dgt_ref.dtype)
            dbg_ref[:, j * D:(j + 1) * D] += _colsum(dgpre)
        dyn_ref[...] = _dot(dys[0], wa_ref[...], NT).astype(dyn_ref.dtype)
        dac_ref[...] = _dot(dys[1], wb_ref[...], NT).astype(dac_ref.dtype)
        das_ref[...] = _dot(dys[2], wc_ref[...], NT).astype(das_ref.dtype)
        dgm_ref[...] += _colsum(dxv * mix_ref[...].astype(F32))

    act = S((n, D), BF)
    return pl.pallas_call(
        body, grid=(n // tm,),
        in_specs=[_rows(tm, D)] * 5 + [_rows(tm, 3 * D, P_GATES // (3 * D)), _full((1, 3 * D)), _full((INNER, D)),
                                       _full((CONF_W, D)), _full((SC_W, D)), _full((D, D)), _full((3, D))],
        out_specs=[_rows(tm, D)] * 4 + [_rows(tm, 3 * D, P_GATES // (3 * D)), _rows(tm, INNER), _rows(tm, CONF_W),
                                        _rows(tm, SC_W), _full((1, D)), _full((1, 3 * D))],
        out_shape=[act, act, act, act, S((n, PW), BF), S((n, INNER), BF), S((n, CONF_W), BF), S((n, SC_W), BF),
                   S((1, D), F32), S((1, 3 * D), F32)],
        compiler_params=_cparams("arbitrary"), name=name)(dx, mix, ya, yb, yc, proj, b_gate, w_ssd, w_conf, w_sc, w_o, mod)


def _pad_w_in(w):
    zeros = jnp.zeros((w.shape[0], XBC_PAD - XBC - (R_CONF - R_DT)), w.dtype)
    return jnp.concatenate([w[:, R_GATES:], w[:, R_SC:R_GATES], w[:, R_XBC:R_DT], w[:, R_DT:R_CONF], zeros,
                            w[:, R_Z:R_XBC], w[:, R_CONF:R_SC]], axis=1)


def _unpad_w_in(wp):
    return jnp.concatenate([wp[:, P_Z:P_Z + INNER], wp[:, P_XBC:P_XBC + XBC], wp[:, P_DT:P_DT + HEADS],
                            wp[:, P_CONF:P_CONF + 2 * CONF_W], wp[:, P_SC:P_SC + 3 * SC_W], wp[:, P_GATES:P_GATES + 3 * D]],
                           axis=1)


W_IN_SHARD = N_IN // N_CHIPS
W_IN_SEGMENTS = ((R_Z, R_XBC, P_Z), (R_XBC, R_DT, P_XBC), (R_DT, R_CONF, P_DT), (R_CONF, R_SC, P_CONF),
                 (R_SC, R_GATES, P_SC), (R_GATES, N_IN, P_GATES))


def _pad_w_in_chips(w4):
    parts = []
    for lo, hi, dst in sorted(W_IN_SEGMENTS, key=lambda sgm: sgm[2]):
        for k in range(N_CHIPS):
            a, b = max(lo, k * W_IN_SHARD), min(hi, (k + 1) * W_IN_SHARD)
            if a < b:
                parts.append((dst + a - lo, w4[k][:, a - k * W_IN_SHARD:b - k * W_IN_SHARD]))
    out, pos = [], 0
    for start, piece in parts:
        if start > pos:
            out.append(jnp.zeros((w4.shape[1], start - pos), w4.dtype))
        out.append(piece)
        pos = start + piece.shape[1]
    if pos < PW:
        out.append(jnp.zeros((w4.shape[1], PW - pos), w4.dtype))
    return jnp.concatenate(out, axis=1)


def _unpad_w_in_chips(wp):
    blocks = []
    for k in range(N_CHIPS):
        pieces = []
        for lo, hi, dst in W_IN_SEGMENTS:
            a, b = max(lo, k * W_IN_SHARD), min(hi, (k + 1) * W_IN_SHARD)
            if a < b:
                pieces.append(wp[:, dst + a - lo:dst + b - lo])
        blocks.append(jnp.concatenate(pieces, axis=1))
    return jnp.stack(blocks)


def _row(v):
    return v.reshape(1, -1)


def _head_row(v):
    return jnp.pad(v, (0, 128 - HEADS)).reshape(1, 128)


def _layer_fwd(x, p, mod_mix, mod_ffn, tag, comm=None):
    sv = {"x0": x}
    h = _modnorm_fwd(x, _row(p["norm_mix_g"]), mod_mix, name=f"modnorm_mix_fwd{tag}")
    proj = _mm_nn(h, p["w_in_pad"], out_dtype=BF, tm=1024, tn=2048, name=f"proj_fwd{tag}")
    xbc_act, cpre = _ssd_pre_fwd(proj, p["ssd_conv_w"], _row(p["ssd_conv_b"]), name=f"ssd_pre_fwd{tag}")
    d_x = _row(jnp.repeat(p["ssd_d"], HEAD_DIM))
    y, yn, hprev = _ssd_fwd(xbc_act, proj, _head_row(p["ssd_dt_bias"]), _head_row(p["ssd_a_log"]), d_x,
                            _row(p["ssd_norm_g"]), name=f"ssd_fwd{tag}")
    a_conf, uc = _conf_fwd(proj, p["conf_conv_w"], _row(p["conf_conv_b"]), _row(p["conf_ln_g"]), _row(p["conf_ln_b"]),
                           name=f"conf_fwd{tag}")
    a_sc = _sc_fwd(proj, p["sc_conv_w"], name=f"sc_fwd{tag}")
    ya, yb, yc, merged, mix, x1 = _mixer_out_fwd(yn, a_conf, a_sc, proj, _row(p["b_gate"]), p["w_ssd_out"],
                                                 p["w_conf_out"], p["w_sc_out"], p["w_o"], x, mod_mix,
                                                 name=f"mixer_out_fwd{tag}")
    h2 = _modnorm_fwd(x1, _row(p["norm_ffn_g"]), mod_ffn, name=f"modnorm_ffn_fwd{tag}")
    up = _mm_nn_chips(h2, p["w_up4"], out_dtype=BF, tm=1024, name=f"up_fwd{tag}")
    (act, u_ffn), comm_out = _ffn_act_fwd(up, p["ffn_conv_w"], _row(p["ffn_conv_b"]), name=f"ffn_act_fwd{tag}", comm=comm)
    o, x2 = _mm_resid(act, p["w_down"], x1, mod_ffn, tm=512, name=f"down_fwd{tag}")
    sv.update(h=h, proj=proj, xbc_act=xbc_act, cpre=cpre, d_x=d_x, y=y, yn=yn, hprev=hprev, a_conf=a_conf, uc=uc, a_sc=a_sc,
              ya=ya, yb=yb, yc=yc, merged=merged, mix=mix, x1=x1, h2=h2, up=up, u_ffn=u_ffn, act=act, o=o)
    return x2, sv, comm_out


def _layer_bwd(dx, p, sv, mod_mix, mod_ffn, tag, comm=None):
    g = {}
    do2, dgate_ffn = _gate_bwd(dx, sv["o"], mod_ffn, name=f"gate_ffn_bwd{tag}")
    dact = _mm_nt(do2, p["w_down"], out_dtype=BF, tm=1024, tk=D, name=f"down_dx{tag}")
    g["w_down"] = _mm_tn(sv["act"], do2, tn=D, tk=1024, name=f"down_dw{tag}")
    (dup, g["ffn_conv_w"], dffn_b), comm_out = _ffn_act_bwd(sv["up"], sv["u_ffn"], dact, p["ffn_conv_w"],
                                                            name=f"ffn_act_bwd{tag}", comm=comm)
    g["ffn_conv_b"] = dffn_b[0]
    dh2 = _mm_nt_chips(dup, p["w_up4"], out_dtype=F32, tm=1024, name=f"up_dx{tag}")
    g["w_up4"] = _mm_tn(sv["h2"], dup, tn=2 * DFF // N_CHIPS, tk=2048, by_chip=True, name=f"up_dw{tag}")
    dx1, dgn, dsh, dsc = _modnorm_bwd(dh2, sv["x1"], dx, _row(p["norm_ffn_g"]), mod_ffn, name=f"modnorm_ffn_bwd{tag}")
    g["norm_ffn_g"] = dgn[0]
    dmod_ffn = jnp.concatenate([dsh[0], dsc[0], dgate_ffn[0]])

    (do1, dya, dyb, dyc, dproj, dyn, dac, dasc, dgate_mix, dbg) = _mixer_out_bwd(
        dx1, sv["mix"], sv["ya"], sv["yb"], sv["yc"], sv["proj"], _row(p["b_gate"]), p["w_ssd_out"], p["w_conf_out"],
        p["w_sc_out"], p["w_o"], mod_mix, name=f"mixer_out_bwd{tag}")
    g["b_gate"] = dbg[0]
    g["w_o"] = _mm_tn(sv["merged"], do1, tn=D, tk=2048, name=f"wo_dw{tag}")
    g["w_ssd_out"] = _mm_tn(sv["yn"], dya, tn=D, tk=2048, name=f"wssd_dw{tag}")
    g["w_conf_out"] = _mm_tn(sv["a_conf"], dyb, tn=D, tk=2048, name=f"wconf_dw{tag}")
    g["w_sc_out"] = _mm_tn(sv["a_sc"], dyc, tn=D, tk=2048, name=f"wsc_dw{tag}")
    dproj, g["conf_conv_w"], dcb, dlg, dlb = _conf_bwd(sv["proj"], sv["uc"], dac, dproj, p["conf_conv_w"],
                                                      _row(p["conf_ln_g"]), _row(p["conf_ln_b"]), name=f"conf_bwd{tag}")
    g["conf_conv_b"], g["conf_ln_g"], g["conf_ln_b"] = dcb[0], dlg[0], dlb[0]
    dproj, g["sc_conv_w"] = _sc_bwd(sv["proj"], dasc, dproj, p["sc_conv_w"], name=f"sc_bwd{tag}")
    dproj, ddt, dxbc_act, dng, ddtb, dalog, ddd = _ssd_bwd(
        sv["xbc_act"], sv["proj"], sv["y"], dyn, sv["hprev"], dproj, _head_row(p["ssd_dt_bias"]),
        _head_row(p["ssd_a_log"]), sv["d_x"], _row(p["ssd_norm_g"]), name=f"ssd_bwd{tag}")
    g["ssd_norm_g"], g["ssd_dt_bias"], g["ssd_a_log"], g["ssd_d"] = dng[0], ddtb[0, :HEADS], dalog[0, :HEADS], ddd[0, :HEADS]
    dproj, g["ssd_conv_w"], dsb = _ssd_pre_bwd(sv["proj"], sv["cpre"], dxbc_act, ddt, dproj, p["ssd_conv_w"],
                                               name=f"ssd_pre_bwd{tag}")
    g["ssd_conv_b"] = dsb[0]
    dh = _mm_nt(dproj, p["w_in_pad"], out_dtype=F32, tm=1024, tk=4096, name=f"proj_dx{tag}")
    g["w_in_pad"] = _mm_tn(sv["h"], dproj, tn=2048, tk=2048, name=f"proj_dw{tag}")
    dx0, dgn, dsh, dsc = _modnorm_bwd(dh, sv["x0"], dx1, _row(p["norm_mix_g"]), mod_mix, name=f"modnorm_mix_bwd{tag}")
    g["norm_mix_g"] = dgn[0]
    dmod_mix = jnp.concatenate([dsh[0], dsc[0], dgate_mix[0]])
    return dx0, g, dmod_mix, dmod_ffn, comm_out


def _local_step(x, target, layers, mods, final_norm_g):
    saved = []
    for i, p in enumerate(layers):
        x, sv, _ = _layer_fwd(x, p, mods[i][0], mods[i][1], f"_l{i}")
        saved.append(sv)
    dx, loss, dfg = _final_loss(x, _row(final_norm_g), target, name="final_loss")
    grads, dmods = [None] * len(layers), [None] * len(layers)
    for i in reversed(range(len(layers))):
        dx, grads[i], dmm, dmf, _ = _layer_bwd(dx, layers[i], saved[i], mods[i][0], mods[i][1], f"_l{i}")
        dmods[i] = (dmm, dmf)
    return loss, dx, grads, dmods, dfg[0]


MESH = pl.DeviceIdType.MESH
ANY = pl.BlockSpec(memory_space=pl.ANY)
VMEM = pl.BlockSpec(memory_space=pltpu.VMEM)


def _mesh_pos():
    return lax.axis_index("x"), lax.axis_index("y"), lax.axis_index("c")


def _peer(pos, mask):
    return tuple(1 - v if (mask >> (2 - k)) & 1 else v for k, v in enumerate(pos))


def _lin(pos):
    return 4 * pos[0] + 2 * pos[1] + pos[2]


def _chip(pos):
    return 2 * pos[0] + pos[1]


def _rcopy(src, dst, send_sem, recv_sem, dev):
    return pltpu.make_async_remote_copy(src_ref=src, dst_ref=dst, send_sem=send_sem, recv_sem=recv_sem,
                                        device_id=dev, device_id_type=MESH)


CHIP_MASKS = (2, 4, 6)
SIBLING = 1
ADA_COLS = 3 * D // N_CHIPS
CONV_ROWS, CONV_COLS = 48, 2 * DFF // N_CHIPS
CONV_PACK = {"ffn_conv_w": (0, FFN_K, 2 * DFF // N_CHIPS), "ssd_conv_w": (3, SSD_K, XBC // N_CHIPS),
             "conf_conv_w": (8, CONF_K, CONF_W // N_CHIPS), "sc_conv_w": (40, SC_K, SC_W // N_CHIPS)}


def _ada_exchange(c_blk, ada_mix_w, ada_ffn_w, conv_pack):
    def body(c_ref, wm_ref, wf_ref, cw_ref, mods_ref, sc_ref, cwall_ref,
             call_scr, modp_scr, recv_scr, s1, r1, s3, r3, s4, r4):
        pos = _mesh_pos()
        me, km = _lin(pos), _chip(pos)
        call_scr[me] = c_ref[...]
        cwall_ref[km] = cw_ref[...]
        sends = []
        for m in range(1, N_DEV):
            sends.append(_rcopy(c_ref, call_scr.at[me], s1.at[m - 1], r1.at[m - 1], _peer(pos, m)))
        for j, m in enumerate(CHIP_MASKS):
            sends.append(_rcopy(cw_ref, cwall_ref.at[km], s4.at[j], r4.at[j], _peer(pos, m)))
        for cp in sends:
            cp.start()
        for m in range(1, N_DEV):
            src = _peer(pos, m)
            _rcopy(c_ref, call_scr.at[_lin(src)], s1.at[m - 1], r1.at[m - 1], src).wait_recv()
        cm = jnp.concatenate([call_scr[d, 0:1, :] for d in range(N_DEV)], axis=0)
        sc = cm * _sigmoid(cm)
        sc_ref[...] = sc
        for j, w in enumerate((wm_ref.at[0], wm_ref.at[1], wf_ref.at[0], wf_ref.at[1])):
            modp_scr[:, j * ADA_COLS:(j + 1) * ADA_COLS] = lax.dot_general(
                sc, w[...], NN, precision=HIGHEST, preferred_element_type=F32)
        recv_scr[km] = modp_scr[...]
        sends3 = [_rcopy(modp_scr, recv_scr.at[km], s3.at[j], r3.at[j], _peer(pos, m)) for j, m in enumerate(CHIP_MASKS)]
        for cp in sends3:
            cp.start()
        for j, m in enumerate(CHIP_MASKS):
            src = _peer(pos, m)
            _rcopy(modp_scr, recv_scr.at[_chip(src)], s3.at[j], r3.at[j], src).wait_recv()
            _rcopy(cw_ref, cwall_ref.at[_chip(src)], s4.at[j], r4.at[j], src).wait_recv()
        for k in range(N_CHIPS):
            mods_ref[k:k + 1, :] = recv_scr[k, pl.ds(me, 1), :]
        for cp in sends + sends3:
            cp.wait_send()

    dma = pltpu.SemaphoreType.DMA
    return pl.pallas_call(
        body, in_specs=[VMEM] * 4, out_specs=[VMEM] * 3,
        out_shape=[S((N_CHIPS, 4 * ADA_COLS), F32), S((N_DEV, D), F32), S((N_CHIPS,) + conv_pack.shape, F32)],
        scratch_shapes=[pltpu.VMEM((N_DEV, 8, D), F32), pltpu.VMEM((N_DEV, 4 * ADA_COLS), F32),
                        pltpu.VMEM((N_CHIPS, N_DEV, 4 * ADA_COLS), F32),
                        dma((N_DEV - 1,)), dma((N_DEV - 1,)), dma((3,)), dma((3,)), dma((3,)), dma((3,))],
        compiler_params=pltpu.CompilerParams(vmem_limit_bytes=VMEM_LIMIT_V7X), name="ada_exchange")(
            c_blk, ada_mix_w, ada_ffn_w, conv_pack)


class _Comm:
    def __init__(self, ins, out_shapes, scratch, start, finish):
        self.ins, self.out_shapes, self.scratch, self.start, self.finish = list(ins), list(out_shapes), list(scratch), start, finish


def _call_with_comm(body, nsteps, *, in_specs, out_specs, out_shape, scratch_shapes, args, comm, name):
    if comm is None:
        res = pl.pallas_call(body, grid=(nsteps,), in_specs=in_specs, out_specs=out_specs, out_shape=out_shape,
                             scratch_shapes=scratch_shapes, compiler_params=_cparams("arbitrary"), name=name)(*args)
        return tuple(res), ()
    ni, no, ns = len(in_specs), len(out_specs), len(scratch_shapes)
    ci, co = len(comm.ins), len(comm.out_shapes)

    def hosted(*refs):
        ins, cins = refs[:ni], refs[ni:ni + ci]
        outs, couts = refs[ni + ci:ni + ci + no], refs[ni + ci + no:ni + ci + no + co]
        scr, csem = refs[ni + ci + no + co:ni + ci + no + co + ns], refs[ni + ci + no + co + ns:]
        i = pl.program_id(0)

        @pl.when(i == 0)
        def _():
            comm.start(cins, couts, csem)

        body(*ins, *outs, *scr)

        @pl.when(i == nsteps - 1)
        def _():
            comm.finish(cins, couts, csem)

    res = pl.pallas_call(
        hosted, grid=(nsteps,), in_specs=list(in_specs) + [ANY] * ci, out_specs=list(out_specs) + [ANY] * co,
        out_shape=list(out_shape) + comm.out_shapes, scratch_shapes=list(scratch_shapes) + comm.scratch,
        compiler_params=_cparams("arbitrary"), name=name)(*args, *comm.ins)
    return tuple(res[:no]), tuple(res[no:])


def _run_comm(comm, name):
    def body(*refs):
        ci, co = len(comm.ins), len(comm.out_shapes)
        comm.start(refs[:ci], refs[ci:ci + co], refs[ci + co:])
        comm.finish(refs[:ci], refs[ci:ci + co], refs[ci + co:])

    return pl.pallas_call(body, in_specs=[ANY] * len(comm.ins), out_specs=[ANY] * len(comm.out_shapes),
                          out_shape=comm.out_shapes, scratch_shapes=comm.scratch, name=name)(*comm.ins)


def _gather_comm(shards, layer):
    na = len(shards)
    dma = pltpu.SemaphoreType.DMA

    def ici(ins, outs, sems, pos, a, j, m, sender):
        ssem, rsem = sems[0], sems[1]
        peer = _peer(pos, m)
        block = outs[a].at[_chip(pos) if sender else _chip(peer)]
        return _rcopy(ins[a].at[layer], block, ssem.at[a, j], rsem.at[a, j], peer)

    def start(ins, outs, sems):
        pos = _mesh_pos()

        @pl.when(pos[2] == layer)
        def _():
            for a in range(na):
                for j, m in enumerate(CHIP_MASKS):
                    ici(ins, outs, sems, pos, a, j, m, True).start()

    def finish(ins, outs, sems):
        fsend, frecv = sems[2], sems[3]
        pos = _mesh_pos()
        sib = _peer(pos, SIBLING)

        def forward(a, j, m):
            blk = outs[a].at[_chip(_peer(pos, m))]
            return _rcopy(blk, blk, fsend.at[a, j], frecv.at[a, j], sib)

        @pl.when(pos[2] == layer)
        def _():
            for a in range(na):
                for j, m in enumerate(CHIP_MASKS):
                    ici(ins, outs, sems, pos, a, j, m, False).wait_recv()
                    forward(a, j, m).start()
            for a in range(na):
                for j, m in enumerate(CHIP_MASKS):
                    ici(ins, outs, sems, pos, a, j, m, True).wait_send()
                    forward(a, j, m).wait_send()

        @pl.when(pos[2] != layer)
        def _():
            for a in range(na):
                for j, m in enumerate(CHIP_MASKS):
                    forward(a, j, m).wait_recv()

    return _Comm(shards, [S((N_CHIPS,) + s.shape[1:], s.dtype) for s in shards],
                 [dma((na, 3)), dma((na, 3)), dma((na, 3)), dma((na, 3))], start, finish)


def _scatter_comm(arrs, layer):
    na = len(arrs)
    dma = pltpu.SemaphoreType.DMA

    def copy(ins, outs, sems, pos, a, j, m, sender):
        peer = _peer(pos, m)
        src = ins[a].at[_chip(peer) if sender else _chip(pos)]
        dst = outs[a].at[_chip(pos) if sender else _chip(peer)]
        return _rcopy(src, dst, sems[0].at[a, j], sems[1].at[a, j], peer)

    def start(ins, outs, sems):
        pos = _mesh_pos()

        @pl.when(pos[2] == layer)
        def _():
            for a in range(na):
                for j, m in enumerate(CHIP_MASKS):
                    copy(ins, outs, sems, pos, a, j, m, True).start()

    def finish(ins, outs, sems):
        pos = _mesh_pos()

        @pl.when(pos[2] == layer)
        def _():
            for a in range(na):
                for j, m in enumerate(CHIP_MASKS):
                    copy(ins, outs, sems, pos, a, j, m, False).wait_recv()
            for a in range(na):
                for j, m in enumerate(CHIP_MASKS):
                    copy(ins, outs, sems, pos, a, j, m, True).wait_send()

    return _Comm(arrs, [S(s.shape, s.dtype) for s in arrs], [dma((na, 3)), dma((na, 3))], start, finish)


def _swap_layer(arrs, layer):
    na = len(arrs)

    def body(*refs):
        ins, outs = refs[:na], refs[na:2 * na]
        ssem, rsem = refs[2 * na:]
        pos = _mesh_pos()
        sib = _peer(pos, SIBLING)
        cps = [_rcopy(ins[a], outs[a], ssem.at[a], rsem.at[a], sib) for a in range(na)]

        @pl.when(pos[2] != layer)
        def _():
            for cp in cps:
                cp.start()
            for cp in cps:
                cp.wait_send()

        @pl.when(pos[2] == layer)
        def _():
            for cp in cps:
                cp.wait_recv()

    dma = pltpu.SemaphoreType.DMA
    return pl.pallas_call(
        body, in_specs=[ANY] * na, out_specs=[ANY] * na, out_shape=[S(s.shape, s.dtype) for s in arrs],
        scratch_shapes=[dma((na,)), dma((na,))], name=f"swap_layer{layer}")(*arrs)


def _share_sibling(arrs):
    na = len(arrs)

    def body(*refs):
        bufs = refs[na:2 * na]
        ssem, rsem = refs[2 * na:]
        pos = _mesh_pos()
        c = pos[2]
        sib = _peer(pos, SIBLING)
        sends = [_rcopy(bufs[a].at[c], bufs[a].at[c], ssem.at[a], rsem.at[a], sib) for a in range(na)]
        for cp in sends:
            cp.start()
        for a in range(na):
            _rcopy(bufs[a].at[c], bufs[a].at[1 - c], ssem.at[a], rsem.at[a], sib).wait_recv()
        for cp in sends:
            cp.wait_send()

    dma = pltpu.SemaphoreType.DMA
    return pl.pallas_call(
        body, in_specs=[ANY] * na, out_specs=[ANY] * na, out_shape=[S(s.shape, s.dtype) for s in arrs],
        input_output_aliases={a: a for a in range(na)},
        scratch_shapes=[dma((na,)), dma((na,))], name="share_sibling")(*arrs)


def _small_allreduce(vec):
    r = vec.shape[0]

    def body(v_ref, sum_ref, all_ref, ssem, rsem):
        pos = _mesh_pos()
        me = _lin(pos)
        all_ref[me] = v_ref[...]
        cps = [_rcopy(v_ref, all_ref.at[me], ssem.at[m - 1], rsem.at[m - 1], _peer(pos, m)) for m in range(1, N_DEV)]
        for cp in cps:
            cp.start()
        for m in range(1, N_DEV):
            src = _peer(pos, m)
            _rcopy(v_ref, all_ref.at[_lin(src)], ssem.at[m - 1], rsem.at[m - 1], src).wait_recv()
        acc = all_ref[0]
        for d in range(1, N_DEV):
            acc = acc + all_ref[d]
        sum_ref[...] = acc
        for cp in cps:
            cp.wait_send()

    dma = pltpu.SemaphoreType.DMA
    return pl.pallas_call(
        body, in_specs=[VMEM], out_specs=[VMEM, VMEM],
        out_shape=[S((r, 128), F32), S((N_DEV, r, 128), F32)],
        scratch_shapes=[dma((N_DEV - 1,)), dma((N_DEV - 1,))],
        compiler_params=pltpu.CompilerParams(vmem_limit_bytes=VMEM_LIMIT_V7X), name="small_allreduce")(vec)


ROW_BYTES_TARGET = 1 << 20


def _row_tile(rows, cols, itemsize=4):
    t = rows
    while t % 2 == 0 and t * cols * itemsize > ROW_BYTES_TARGET and (t // 2) % 16 == 0:
        t //= 2
    return t


def _pair_add(g, other):
    r, cdim = g.shape
    tr = _row_tile(r, cdim)

    def body(g_ref, o_ref, out_ref):
        out_ref[...] = (g_ref[...].astype(F32) + o_ref[...].astype(F32)).astype(out_ref.dtype)

    blk = pl.BlockSpec((tr, cdim), lambda i: (i, 0))
    return pl.pallas_call(body, grid=(r // tr,), in_specs=[blk, blk], out_specs=blk, out_shape=S((r, cdim), BF),
                          compiler_params=_cparams("parallel"), name="pair_add")(g, other)


def _sum4(parts, pairs, chip_core):
    _, r, cdim = parts[0].shape
    tr = _row_tile(r, cdim)

    def body(kc_ref, q0_ref, q1_ref, own0_ref, own1_ref, out_ref):
        first = kc_ref[1] == 0
        mine = jnp.where(first, own0_ref[...], own1_ref[...]).astype(F32)
        terms = [jnp.where(kc_ref[0] == j, mine, jnp.where(first, q0_ref[j], q1_ref[j]).astype(F32))
                 for j in range(N_CHIPS)]
        out_ref[...] = ((terms[0] + terms[1]) + terms[2]) + terms[3]

    allc = pl.BlockSpec((N_CHIPS, tr, cdim), lambda i, kc: (0, i, 0))
    own = pl.BlockSpec((None, tr, cdim), lambda i, kc: (kc[0], i, 0))
    return pl.pallas_call(
        body,
        grid_spec=pltpu.PrefetchScalarGridSpec(
            num_scalar_prefetch=1, grid=(r // tr,), in_specs=[allc, allc, own, own],
            out_specs=pl.BlockSpec((None, tr, cdim), lambda i, kc: (kc[1], i, 0))),
        out_shape=S((DEPTH, r, cdim), F32), compiler_params=_cparams("parallel"), name="sum4")(
            chip_core, parts[0], parts[1], pairs[0], pairs[1])


def _ada_w_grad(silu_c, dmod_cols, chip):
    def body(k_ref, sc_ref, dm_ref, o_ref):
        del k_ref
        o_ref[...] = lax.dot_general(sc_ref[...], dm_ref[...], TN, precision=HIGHEST, preferred_element_type=F32)

    return pl.pallas_call(
        body,
        grid_spec=pltpu.PrefetchScalarGridSpec(
            num_scalar_prefetch=1, grid=(4,),
            in_specs=[pl.BlockSpec((N_DEV, D), lambda j, k: (0, 0)),
                      pl.BlockSpec((None, N_DEV, ADA_COLS), lambda j, k: (4 * j + k[0], 0, 0))],
            out_specs=pl.BlockSpec((None, D, ADA_COLS), lambda j, k: (j, 0, 0))),
        out_shape=S((4, D, ADA_COLS), F32), compiler_params=_cparams("parallel"), name="ada_w_grad")(chip, silu_c, dmod_cols)


def _adamw(w, g, m, v, *, name):
    r, cdim = w.shape
    tr = _row_tile(r, cdim)
    c1 = 1.0 / (1.0 - ADAM_B1 ** ADAM_STEP)
    c2 = 1.0 / (1.0 - ADAM_B2 ** ADAM_STEP)

    def body(w_ref, g_ref, m_ref, v_ref, d_ref, mo_ref, vo_ref):
        gv = g_ref[...]
        mn = ADAM_B1 * m_ref[...] + (1.0 - ADAM_B1) * gv
        vn = ADAM_B2 * v_ref[...] + (1.0 - ADAM_B2) * (gv * gv)
        mo_ref[...] = mn
        vo_ref[...] = vn
        d_ref[...] = -ADAM_LR * ((mn * c1) / (jnp.sqrt(vn * c2) + ADAM_EPS) + ADAM_WD * w_ref[...])

    blk = pl.BlockSpec((tr, cdim), lambda i: (i, 0))
    return pl.pallas_call(
        body, grid=(r // tr,), in_specs=[blk] * 4, out_specs=[blk] * 3, out_shape=[S((r, cdim), F32)] * 3,
        compiler_params=_cparams("parallel"), name=name)(w, g, m, v)


def _adamw_many(ws, gs, ms, vs):
    n = len(ws)
    c1 = 1.0 / (1.0 - ADAM_B1 ** ADAM_STEP)
    c2 = 1.0 / (1.0 - ADAM_B2 ** ADAM_STEP)

    def body(*refs):
        for i in range(n):
            w_ref, g_ref, m_ref, v_ref, d_ref, mo_ref, vo_ref = (refs[k * n + i] for k in range(7))
            gv = g_ref[...]
            mn = ADAM_B1 * m_ref[...] + (1.0 - ADAM_B1) * gv
            vn = ADAM_B2 * v_ref[...] + (1.0 - ADAM_B2) * (gv * gv)
            mo_ref[...] = mn
            vo_ref[...] = vn
            d_ref[...] = -ADAM_LR * ((mn * c1) / (jnp.sqrt(vn * c2) + ADAM_EPS) + ADAM_WD * w_ref[...])

    shapes = [S(a.shape, F32) for a in ws]
    outs = pl.pallas_call(
        body, in_specs=[VMEM] * (4 * n), out_specs=[VMEM] * (3 * n), out_shape=shapes * 3,
        compiler_params=pltpu.CompilerParams(vmem_limit_bytes=VMEM_LIMIT_V7X), name="adamw_small")(*ws, *gs, *ms, *vs)
    return outs[0:n], outs[n:2 * n], outs[2 * n:3 * n]


WEIGHTS = ['ada_mix_w', 'ada_mix_b', 'norm_mix_g', 'w_in', 'b_gate', 'ssd_conv_w', 'ssd_conv_b', 'ssd_dt_bias',
           'ssd_a_log', 'ssd_d', 'ssd_norm_g', 'w_ssd_out', 'conf_conv_w', 'conf_conv_b', 'conf_ln_g', 'conf_ln_b',
           'w_conf_out', 'sc_conv_w', 'w_sc_out', 'w_o', 'ada_ffn_w', 'ada_ffn_b', 'norm_ffn_g', 'w_up', 'ffn_conv_w',
           'ffn_conv_b', 'w_down', 'final_norm_g']
SMALL = ['ada_mix_b', 'norm_mix_g', 'b_gate', 'ssd_conv_b', 'ssd_dt_bias', 'ssd_a_log', 'ssd_d', 'ssd_norm_g', 'conf_conv_b',
         'conf_ln_g', 'conf_ln_b', 'ada_ffn_b', 'norm_ffn_g', 'ffn_conv_b']
CONVS = ['ssd_conv_w', 'conf_conv_w', 'sc_conv_w', 'ffn_conv_w']
BIG = ['ada_mix_w', 'ada_ffn_w', 'w_in', 'w_up', 'w_conf_out', 'w_sc_out', 'w_ssd_out', 'w_o', 'w_down']


def _pack_rows(pieces):
    flat = [p.reshape(-1) for p in pieces]
    offs, o = [], 0
    for f in flat:
        offs.append(o)
        o += f.shape[0]
    total = -(-o // 1024) * 1024
    vec = jnp.concatenate(flat + [jnp.zeros((total - o,), F32)])
    return vec.reshape(total // 128, 128), offs


def _by_chip(a, axis):
    shp = a.shape
    a = a.reshape(shp[:axis] + (N_CHIPS, shp[axis] // N_CHIPS) + shp[axis + 1:])
    return jnp.moveaxis(a, axis, 0)


def _from_chips(a, axis):
    a = jnp.moveaxis(a, 0, axis)
    shp = a.shape
    return a.reshape(shp[:axis] + (shp[axis] * shp[axis + 1],) + shp[axis + 2:])


def kernel(x, c, ada_mix_w, ada_mix_b, norm_mix_g, w_in, b_gate, ssd_conv_w, ssd_conv_b, ssd_dt_bias, ssd_a_log, ssd_d, ssd_norm_g, w_ssd_out, conf_conv_w, conf_conv_b, conf_ln_g, conf_ln_b, w_conf_out, sc_conv_w, w_sc_out, w_o, ada_ffn_w, ada_ffn_b, norm_ffn_g, w_up, ffn_conv_w, ffn_conv_b, w_down, final_norm_g, loss_target, m_ada_mix_w, m_ada_mix_b, m_norm_mix_g, m_w_in, m_b_gate, m_ssd_conv_w, m_ssd_conv_b, m_ssd_dt_bias, m_ssd_a_log, m_ssd_d, m_ssd_norm_g, m_w_ssd_out, m_conf_conv_w, m_conf_conv_b, m_conf_ln_g, m_conf_ln_b, m_w_conf_out, m_sc_conv_w, m_w_sc_out, m_w_o, m_ada_ffn_w, m_ada_ffn_b, m_norm_ffn_g, m_w_up, m_ffn_conv_w, m_ffn_conv_b, m_w_down, m_final_norm_g, v_ada_mix_w, v_ada_mix_b, v_norm_mix_g, v_w_in, v_b_gate, v_ssd_conv_w, v_ssd_conv_b, v_ssd_dt_bias, v_ssd_a_log, v_ssd_d, v_ssd_norm_g, v_w_ssd_out, v_conf_conv_w, v_conf_conv_b, v_conf_ln_g, v_conf_ln_b, v_w_conf_out, v_sc_conv_w, v_w_sc_out, v_w_o, v_ada_ffn_w, v_ada_ffn_b, v_norm_ffn_g, v_w_up, v_ffn_conv_w, v_ffn_conv_b, v_w_down, v_final_norm_g):
    args = locals()
    w = {n: args[n] for n in WEIGHTS}
    mom = {n: args["m_" + n] for n in WEIGHTS}
    var = {n: args["v_" + n] for n in WEIGHTS}
    pos = _mesh_pos()
    chip = _chip(pos)
    core = pos[2]

    conv_pack = jnp.zeros((DEPTH, CONV_ROWS, CONV_COLS), F32)
    for n, (r0, taps, width) in CONV_PACK.items():
        conv_pack = conv_pack.at[:, r0:r0 + taps, 0:width].set(w[n])
    c_blk = jnp.pad(c, ((0, 7), (0, 0)))
    mods_raw, silu_c, conv_all = _ada_exchange(c_blk, ada_mix_w, ada_ffn_w, conv_pack)
    ada_b = jnp.concatenate([ada_mix_b, ada_ffn_b], axis=0)
    mod_all = mods_raw.reshape(N_CHIPS, 4, ADA_COLS).transpose(1, 0, 2).reshape(4, 3 * D) + ada_b
    mod_all = mod_all.reshape(4, 3, D)
    mods = [(mod_all[i], mod_all[2 + i]) for i in range(DEPTH)]
    conv_full = {n: _from_chips(conv_all[:, :, r0:r0 + taps, 0:width], 2) for n, (r0, taps, width) in CONV_PACK.items()}

    cast = lambda a: a.astype(BF)
    shards = [cast(w_in), cast(w_up), jnp.concatenate([cast(w_conf_out), cast(w_sc_out)], axis=1),
              jnp.concatenate([cast(w_ssd_out), cast(w_o), cast(w_down)], axis=1)]
    def layer_params(l, gathered):
        g_in, g_up, g_cs, g_row = [lax.dynamic_update_slice(g, sh[l][None], (chip, 0, 0)) for g, sh in zip(gathered, shards)]
        p = {n: w[n][l] for n in SMALL if not n.startswith("ada_")}
        p.update({n: conv_full[n][l] for n in CONVS})
        p["w_in_pad"] = _pad_w_in_chips(g_in)
        p["w_up4"] = g_up
        p["w_conf_out"] = _from_chips(g_cs[:, 0:CONF_W], 1)
        p["w_sc_out"] = _from_chips(g_cs[:, CONF_W:], 1)
        p["w_ssd_out"] = _from_chips(g_row[:, 0:INNER // N_CHIPS], 0)
        p["w_o"] = _from_chips(g_row[:, INNER // N_CHIPS:(INNER + D) // N_CHIPS], 0)
        p["w_down"] = _from_chips(g_row[:, (INNER + D) // N_CHIPS:], 0)
        return p

    def big_grads(g):
        return [_unpad_w_in_chips(g["w_in_pad"]).astype(BF), g["w_up4"].astype(BF),
                _by_chip(jnp.concatenate([g["w_conf_out"], g["w_sc_out"]], axis=0), 1).astype(BF),
                jnp.concatenate([_by_chip(g["w_ssd_out"], 0), _by_chip(g["w_o"], 0), _by_chip(g["w_down"], 0)],
                                axis=1).astype(BF)]

    def pair_sums(big, l):
        theirs = _swap_layer(big, l)
        out = []
        for g4, t4 in zip(big, theirs):
            k, r, cdim = g4.shape
            out.append(_pair_add(g4.reshape(k * r, cdim), t4.reshape(k * r, cdim)).reshape(k, r, cdim))
        return out

    seq = x.shape[1]
    xs = x.reshape(seq, D)
    p0 = layer_params(0, _run_comm(_gather_comm(shards, 0), "gather_weights_l0"))
    xs, sv0, gathered1 = _layer_fwd(xs, p0, mods[0][0], mods[0][1], "_l0", comm=_gather_comm(shards, 1))
    p1 = layer_params(1, gathered1)
    xs, sv1, _ = _layer_fwd(xs, p1, mods[1][0], mods[1][1], "_l1")
    dx, loss, dfg = _final_loss(xs, _row(final_norm_g), loss_target.reshape(seq, D), name="final_loss")
    dfinal = dfg[0]

    grads, dmods = [None, None], [None, None]
    dx, grads[1], dmm, dmf, _ = _layer_bwd(dx, p1, sv1, mods[1][0], mods[1][1], "_l1")
    dmods[1] = (dmm, dmf)
    pair1 = pair_sums(big_grads(grads[1]), 1)
    dx, grads[0], dmm, dmf, parts1 = _layer_bwd(dx, p0, sv0, mods[0][0], mods[0][1], "_l0", comm=_scatter_comm(pair1, 1))
    dmods[0] = (dmm, dmf)
    pair0 = pair_sums(big_grads(grads[0]), 0)
    parts0 = _run_comm(_scatter_comm(pair0, 0), "scatter_chips_l0")
    chip_core = jnp.stack([chip, core]).astype(jnp.int32)
    reduced = _share_sibling([_sum4((q0, q1), (o0, o1), chip_core)
                              for q0, q1, o0, o1 in zip(parts0, parts1, pair0, pair1)])
    gw = {"w_in": reduced[0], "w_up": reduced[1],
          "w_conf_out": reduced[2][:, 0:CONF_W], "w_sc_out": reduced[2][:, CONF_W:],
          "w_ssd_out": reduced[3][:, 0:INNER // N_CHIPS], "w_o": reduced[3][:, INNER // N_CHIPS:(INNER + D) // N_CHIPS],
          "w_down": reduced[3][:, (INNER + D) // N_CHIPS:]}

    dmod = jnp.stack([dmods[0][0], dmods[1][0], dmods[0][1], dmods[1][1]])
    small_local = {n: jnp.stack([grads[l][n] for l in range(DEPTH)]) for n in SMALL if not n.startswith("ada_")}
    pieces = [loss[0]] + [small_local[n] for n in SMALL if not n.startswith("ada_")]
    pieces += [jnp.stack([grads[l][n] for l in range(DEPTH)]) for n in CONVS] + [dfinal, dmod]
    vec, offs = _pack_rows(pieces)
    vsum, vall = _small_allreduce(vec)
    flat = vsum.reshape(-1)

    def piece(k, like):
        return flat[offs[k]:offs[k] + like.size].reshape(like.shape)

    loss_out = flat[0]
    k = 1
    for n in SMALL:
        if not n.startswith("ada_"):
            gw[n] = piece(k, small_local[n])
            k += 1
    for n in CONVS:
        full = piece(k, conv_full[n])
        width = CONV_PACK[n][2]
        gw[n] = lax.dynamic_slice_in_dim(full, chip * width, width, axis=2)
        k += 1
    gw["final_norm_g"] = piece(k, dfinal)
    k += 1
    dmod_sum = piece(k, dmod)
    gw["ada_mix_b"], gw["ada_ffn_b"] = dmod_sum[0:2], dmod_sum[2:4]
    dmod_all = vall.reshape(N_DEV, -1)[:, offs[k]:offs[k] + dmod.size]
    dmod_cols = dmod_all.reshape(N_DEV, 4 * N_CHIPS, ADA_COLS).transpose(1, 0, 2)
    ada_g = _ada_w_grad(silu_c, dmod_cols, jnp.reshape(chip, (1,)).astype(jnp.int32))
    gw["ada_mix_w"], gw["ada_ffn_w"] = ada_g[0:2], ada_g[2:4]

    delta, new_m, new_v = {}, {}, {}
    for n in BIG:
        shp = w[n].shape
        two_d = lambda a: a.reshape(shp[0] * shp[1], shp[2])
        d_, m_, v_ = _adamw(two_d(w[n]), two_d(gw[n]), two_d(mom[n]), two_d(var[n]), name=f"adamw_{n}")
        delta[n], new_m[n], new_v[n] = d_.reshape(shp), m_.reshape(shp), v_.reshape(shp)
    rest = [n for n in WEIGHTS if n not in BIG]
    two_d = lambda a: a.reshape(1, -1) if a.ndim == 1 else a
    outs = _adamw_many(*[[two_d(src[n]) for n in rest] for src in (w, gw, mom, var)])
    for dst, group in zip((delta, new_m, new_v), outs):
        for n, o in zip(rest, group):
            dst[n] = o.reshape(w[n].shape)

    return (loss_out, dx[None], *[gw[n] for n in WEIGHTS], *[delta[n] for n in WEIGHTS],
            *[new_m[n] for n in WEIGHTS], *[new_v[n] for n in WEIGHTS])
```

```python
import functools

import jax
import jax.numpy as jnp
from jax import lax
from jax.experimental import pallas as pl
from jax.experimental.pallas import tpu as pltpu

F32 = jnp.float32
BF = jnp.bfloat16
S = jax.ShapeDtypeStruct

D = 1024
HEADS = 16
HEAD_DIM = 64
INNER = HEADS * HEAD_DIM
GROUPS = 2
NSTATE = 64
Q = 128
SSD_K = 4
XBC = INNER + 2 * GROUPS * NSTATE
CONF_W = 512
CONF_K = 31
SC_W = 512
SC_K = 3
DFF = 2816
FFN_K = 3
EPS = 1e-6
DEPTH = 2
R_Z, R_XBC, R_DT, R_CONF, R_SC, R_GATES, N_IN = 0, 1024, 2304, 2320, 3344, 4880, 7952
P_GATES, P_SC, P_XBC, P_Z, P_CONF, PW = 0, 3072, 4608, 6144, 7168, 8192
XBC_PAD = 1536
DT_PAD = 128
P_DT = P_XBC + XBC
N_CHIPS = 4
N_DEV = 8

ADAM_LR, ADAM_B1, ADAM_B2, ADAM_EPS, ADAM_WD, ADAM_STEP = 0.001, 0.9, 0.999, 1e-08, 0.01, 10

VMEM_LIMIT_V7X = 56 * 1024 * 1024
HIGHEST = lax.Precision.HIGHEST


def _cparams(*sem):
    return pltpu.CompilerParams(dimension_semantics=sem, vmem_limit_bytes=VMEM_LIMIT_V7X)


def _full(shape):
    n = len(shape)
    return pl.BlockSpec(shape, lambda *_: (0,) * n)


def _rows(tm, w, cb=0):
    return pl.BlockSpec((tm, w), lambda i: (i, cb))


def _prev_rows(tm, halo, w, cb=0):
    r = tm // halo
    return pl.BlockSpec((halo, w), lambda i: (jnp.maximum(i * r - 1, 0), cb))


def _next_rows(tm, halo, w, nrows, cb=0):
    r = tm // halo
    last = nrows // halo - 1
    return pl.BlockSpec((halo, w), lambda i: (jnp.minimum((i + 1) * r, last), cb))


def _sigmoid(v):
    return 1.0 / (1.0 + jnp.exp(-v))


def _softplus(v):
    return jnp.maximum(v, 0.0) + jnp.log(1.0 + jnp.exp(-jnp.abs(v)))


def _colsum(v):
    return jnp.sum(v, axis=0, keepdims=True)


def _tile(n, want):
    t = min(n, want)
    assert n % t == 0, (n, want)
    return t


NN = (((1,), (0,)), ((), ()))
NT = (((1,), (1,)), ((), ()))
TN = (((0,), (0,)), ((), ()))


def _dot(a, b, dims=NN):
    return lax.dot_general(a, b, dims, preferred_element_type=F32)


def _mm(a, b, *, dims, grid, a_spec, b_spec, o_spec, out_shape, acc_shape, name):
    nk = grid[2]

    def body(a_ref, b_ref, o_ref, acc_ref):
        k = pl.program_id(2)
        part = _dot(a_ref[...], b_ref[...], dims)
        if nk == 1:
            o_ref[...] = part.astype(o_ref.dtype)
        else:
            @pl.when(k == 0)
            def _():
                acc_ref[...] = part

            @pl.when(k > 0)
            def _():
                acc_ref[...] += part

            @pl.when(k == nk - 1)
            def _():
                o_ref[...] = acc_ref[...].astype(o_ref.dtype)

    return pl.pallas_call(
        body, grid=grid, in_specs=[a_spec, b_spec], out_specs=o_spec, out_shape=out_shape,
        scratch_shapes=[pltpu.VMEM(acc_shape if nk > 1 else (8, 128), F32)],
        compiler_params=_cparams("parallel", "parallel", "arbitrary"), name=name)(a, b)


def _mm_nn(a, b, *, out_dtype, tm, tn, name):
    m, k = a.shape
    n = b.shape[1]
    tm, tn = _tile(m, tm), _tile(n, tn)
    return _mm(a, b, dims=NN, grid=(m // tm, n // tn, 1),
               a_spec=pl.BlockSpec((tm, k), lambda i, j, kk: (i, 0)),
               b_spec=pl.BlockSpec((k, tn), lambda i, j, kk: (0, j)),
               o_spec=pl.BlockSpec((tm, tn), lambda i, j, kk: (i, j)),
               out_shape=S((m, n), out_dtype), acc_shape=(tm, tn), name=name)


def _mm_nt(a, b, *, out_dtype, tm, tk, name):
    m, kc = a.shape
    n = b.shape[0]
    tm, tk = _tile(m, tm), _tile(kc, tk)
    return _mm(a, b, dims=NT, grid=(m // tm, 1, kc // tk),
               a_spec=pl.BlockSpec((tm, tk), lambda i, j, kk: (i, kk)),
               b_spec=pl.BlockSpec((n, tk), lambda i, j, kk: (0, kk)),
               o_spec=pl.BlockSpec((tm, n), lambda i, j, kk: (i, 0)),
               out_shape=S((m, n), out_dtype), acc_shape=(tm, n), name=name)


def _mm_tn(a, b, *, tn, tk, name, out_dtype=BF, by_chip=False):
    kc, m = a.shape
    n = b.shape[1]
    tn, tk = _tile(n, tn), _tile(kc, tk)
    if by_chip:
        assert n == N_CHIPS * tn
        o_spec, out_shape = pl.BlockSpec((None, m, tn), lambda i, j, kk: (j, 0, 0)), S((N_CHIPS, m, tn), out_dtype)
    else:
        o_spec, out_shape = pl.BlockSpec((m, tn), lambda i, j, kk: (0, j)), S((m, n), out_dtype)
    return _mm(a, b, dims=TN, grid=(1, n // tn, kc // tk),
               a_spec=pl.BlockSpec((tk, m), lambda i, j, kk: (kk, 0)),
               b_spec=pl.BlockSpec((tk, tn), lambda i, j, kk: (kk, j)),
               o_spec=o_spec, out_shape=out_shape, acc_shape=(m, tn), name=name)


def _mm_nn_chips(a, b4, *, out_dtype, tm, name):
    m, k = a.shape
    n4 = b4.shape[2]
    tm = _tile(m, tm)
    return _mm(a, b4, dims=NN, grid=(m // tm, N_CHIPS, 1),
               a_spec=pl.BlockSpec((tm, k), lambda i, j, kk: (i, 0)),
               b_spec=pl.BlockSpec((None, k, n4), lambda i, j, kk: (j, 0, 0)),
               o_spec=pl.BlockSpec((tm, n4), lambda i, j, kk: (i, j)),
               out_shape=S((m, N_CHIPS * n4), out_dtype), acc_shape=(tm, n4), name=name)


def _mm_nt_chips(a, b4, *, out_dtype, tm, name):
    m = a.shape[0]
    n, n4 = b4.shape[1], b4.shape[2]
    tm = _tile(m, tm)
    return _mm(a, b4, dims=NT, grid=(m // tm, 1, N_CHIPS),
               a_spec=pl.BlockSpec((tm, n4), lambda i, j, kk: (i, kk)),
               b_spec=pl.BlockSpec((None, n, n4), lambda i, j, kk: (kk, 0, 0)),
               o_spec=pl.BlockSpec((tm, n), lambda i, j, kk: (i, 0)),
               out_shape=S((m, n), out_dtype), acc_shape=(tm, n), name=name)


def _mm_resid(a, b, x, mod, *, tm, name):
    m, k = a.shape
    n = b.shape[1]
    tm = _tile(m, tm)

    def body(a_ref, b_ref, x_ref, mod_ref, o_ref, xn_ref):
        o = _dot(a_ref[...], b_ref[...])
        o_ref[...] = o.astype(o_ref.dtype)
        xn_ref[...] = x_ref[...] + mod_ref[2:3, :] * o

    return pl.pallas_call(
        body, grid=(m // tm,),
        in_specs=[_rows(tm, k), _full((k, n)), _rows(tm, n), _full((3, n))],
        out_specs=[_rows(tm, n), _rows(tm, n)],
        out_shape=[S((m, n), BF), S((m, n), F32)],
        compiler_params=_cparams("parallel"), name=name)(a, b, x, mod)


def _modnorm_fwd(x, gain, mod, *, name):
    n = x.shape[0]
    tm = _tile(n, 512)

    def body(x_ref, g_ref, mod_ref, h_ref):
        xv = x_ref[...]
        r = lax.rsqrt(jnp.mean(xv * xv, axis=-1, keepdims=True) + EPS)
        y = xv * r * g_ref[...]
        h_ref[...] = (y * (1.0 + mod_ref[1:2, :]) + mod_ref[0:1, :]).astype(h_ref.dtype)

    return pl.pallas_call(
        body, grid=(n // tm,), in_specs=[_rows(tm, D), _full((1, D)), _full((3, D))],
        out_specs=_rows(tm, D), out_shape=S((n, D), BF), compiler_params=_cparams("parallel"), name=name)(x, gain, mod)


def _modnorm_bwd(dh, x, dres, gain, mod, *, name):
    n = x.shape[0]
    tm = _tile(n, 512)

    def body(dh_ref, x_ref, dres_ref, g_ref, mod_ref, dx_ref, dg_ref, dsh_ref, dsc_ref):
        i = pl.program_id(0)
        xv = x_ref[...]
        r = lax.rsqrt(jnp.mean(xv * xv, axis=-1, keepdims=True) + EPS)
        xh = xv * r
        dhv = dh_ref[...]
        g = g_ref[...]
        dy = dhv * (1.0 + mod_ref[1:2, :])
        dxh = dy * g
        dx = r * (dxh - xh * jnp.mean(dxh * xh, axis=-1, keepdims=True))
        dx_ref[...] = dres_ref[...] + dx

        @pl.when(i == 0)
        def _():
            dg_ref[...] = jnp.zeros_like(dg_ref)
            dsh_ref[...] = jnp.zeros_like(dsh_ref)
            dsc_ref[...] = jnp.zeros_like(dsc_ref)

        dg_ref[...] += _colsum(dy * xh)
        dsh_ref[...] += _colsum(dhv)
        dsc_ref[...] += _colsum(dhv * xh * g)

    vec = S((1, D), F32)
    return pl.pallas_call(
        body, grid=(n // tm,),
        in_specs=[_rows(tm, D), _rows(tm, D), _rows(tm, D), _full((1, D)), _full((3, D))],
        out_specs=[_rows(tm, D), _full((1, D)), _full((1, D)), _full((1, D))],
        out_shape=[S((n, D), F32), vec, vec, vec],
        compiler_params=_cparams("arbitrary"), name=name)(dh, x, dres, gain, mod)


def _final_loss(x, gain, target, *, name):
    n = x.shape[0]
    tm = _tile(n, 512)

    def body(x_ref, g_ref, t_ref, dx_ref, loss_ref, dg_ref):
        i = pl.program_id(0)
        xv = x_ref[...]
        g = g_ref[...]
        r = lax.rsqrt(jnp.mean(xv * xv, axis=-1, keepdims=True) + EPS)
        xh = xv * r
        err = xh * g - t_ref[...]
        dy = err * (1.0 / D)
        dxh = dy * g
        dx_ref[...] = r * (dxh - xh * jnp.mean(dxh * xh, axis=-1, keepdims=True))

        @pl.when(i == 0)
        def _():
            loss_ref[...] = jnp.zeros_like(loss_ref)
            dg_ref[...] = jnp.zeros_like(dg_ref)

        part = _colsum(jnp.sum(err * err, axis=-1, keepdims=True)) * (0.5 / D)
        loss_ref[...] += jnp.broadcast_to(part, loss_ref.shape)
        dg_ref[...] += _colsum(dy * xh)

    return pl.pallas_call(
        body, grid=(n // tm,),
        in_specs=[_rows(tm, D), _full((1, D)), _rows(tm, D)],
        out_specs=[_rows(tm, D), _full((1, 128)), _full((1, D))],
        out_shape=[S((n, D), F32), S((1, 128), F32), S((1, D), F32)],
        compiler_params=_cparams("arbitrary"), name=name)(x, gain, target)


def _gate_bwd(dx, o, mod, *, name):
    n = dx.shape[0]
    tm = _tile(n, 512)

    def body(dx_ref, o_ref, mod_ref, do_ref, dgt_ref):
        i = pl.program_id(0)
        dxv = dx_ref[...]
        do_ref[...] = (dxv * mod_ref[2:3, :]).astype(do_ref.dtype)

        @pl.when(i == 0)
        def _():
            dgt_ref[...] = jnp.zeros_like(dgt_ref)

        dgt_ref[...] += _colsum(dxv * o_ref[...].astype(F32))

    return pl.pallas_call(
        body, grid=(n // tm,), in_specs=[_rows(tm, D), _rows(tm, D), _full((3, D))],
        out_specs=[_rows(tm, D), _full((1, D))], out_shape=[S((n, D), BF), S((1, D), F32)],
        compiler_params=_cparams("arbitrary"), name=name)(dx, o, mod)


def _conv(buf, w_ref, taps, start, rows, ch):
    acc = None
    for k in range(taps):
        term = buf[pl.ds(start - (taps - 1) + k, rows), 0:ch] * w_ref[k:k + 1, :]
        acc = term if acc is None else acc + term
    return acc


def _conv_t(buf, w_ref, taps, start, rows, ch):
    acc = None
    for k in range(taps):
        term = buf[pl.ds(start + (taps - 1) - k, rows), 0:ch] * w_ref[k:k + 1, :]
        acc = term if acc is None else acc + term
    return acc


def _conv_dw(dw_ref, dy, xbuf, taps, xstart, rows, ch):
    for k in range(taps):
        dw_ref[k:k + 1, :] += _colsum(dy * xbuf[pl.ds(xstart - (taps - 1) + k, rows), 0:ch])


HALO = 16
CONF_HALO = 32
CHUNK = 16
STRIP = 256


def _blocks8(v):
    return [v[8 * i:8 * (i + 1)] for i in range(v.shape[0] // 8)]


def _delay_rows(blocks, s):
    sub = lax.broadcasted_iota(jnp.int32, blocks[0].shape, 0)
    rolled = [pltpu.roll(b, s, 0) for b in blocks]
    return [jnp.where(sub < s, rolled[i - 1], rolled[i]) for i in range(1, len(blocks))]


def _advance_rows(blocks, s):
    sub = lax.broadcasted_iota(jnp.int32, blocks[0].shape, 0)
    rolled = [pltpu.roll(b, 8 - s, 0) for b in blocks]
    return [jnp.where(sub < 8 - s, rolled[i], rolled[i + 1]) for i in range(len(blocks) - 1)]


def _conv3_chunk(tail, xv, wk):
    blocks = [tail] + _blocks8(xv)
    x1 = jnp.concatenate(_delay_rows(blocks, 1), axis=0)
    x2 = jnp.concatenate(_delay_rows(blocks, 2), axis=0)
    return wk[0] * x2 + wk[1] * x1 + wk[2] * xv


def _conv_chunk(tail, xv, wk):
    taps = len(wk)
    blocks = [tail] + _blocks8(xv)
    acc = wk[taps - 1] * xv
    for d in range(1, taps):
        acc = acc + wk[taps - 1 - d] * jnp.concatenate(_delay_rows(blocks, d), axis=0)
    return acc


def _ssd_pre_fwd(proj, w, b, *, name):
    n = proj.shape[0]
    tm = _tile(n, 512)
    cb = P_XBC // XBC_PAD

    def body(prev_ref, cur_ref, w_ref, b_ref, o_ref, c_ref, buf):
        i = pl.program_id(0)
        buf[0:HALO, :] = jnp.where(i == 0, 0.0, prev_ref[:, 0:XBC].astype(F32))
        buf[HALO:HALO + tm, :] = cur_ref[:, 0:XBC].astype(F32)
        c = _conv(buf, w_ref, SSD_K, HALO, tm, XBC) + b_ref[...]
        c_ref[...] = c.astype(c_ref.dtype)
        o_ref[...] = (c * _sigmoid(c)).astype(o_ref.dtype)

    return pl.pallas_call(
        body, grid=(n // tm,),
        in_specs=[_prev_rows(tm, HALO, XBC_PAD, cb), _rows(tm, XBC_PAD, cb), _full((SSD_K, XBC)), _full((1, XBC))],
        out_specs=[_rows(tm, XBC), _rows(tm, XBC)], out_shape=[S((n, XBC), BF), S((n, XBC), BF)],
        scratch_shapes=[pltpu.VMEM((HALO + tm, XBC), F32)],
        compiler_params=_cparams("parallel"), name=name)(proj, proj, w, b)


def _ssd_pre_bwd(proj, cpre, dact, ddt, dproj, w, *, name):
    n = proj.shape[0]
    tm = _tile(n, 512)
    nt = n // tm
    cb = P_XBC // XBC_PAD

    def body(x_ref, cc_ref, cn_ref, dc_ref, dn_ref, ddt_ref, w_ref, dproj_in, o_ref, dw_ref, db_ref, dbuf, acc):
        del dproj_in
        i = pl.program_id(0)
        last = i == nt - 1

        @pl.when(i == 0)
        def _():
            acc[...] = jnp.zeros_like(acc)

        def silu_bwd(cv, dav):
            sg = _sigmoid(cv)
            return dav * (sg * (1.0 + cv * (1.0 - sg)))

        for s in range(XBC // STRIP):
            c = pl.ds(s * STRIP, STRIP)
            wk = [w_ref[k:k + 1, c] for k in range(SSD_K)]

            def step1(j, carry):
                rows = pl.ds(pl.multiple_of(j * CHUNK, CHUNK), CHUNK)
                dbuf[rows, c] = silu_bwd(cc_ref[rows, c].astype(F32), dc_ref[rows, c].astype(F32))
                return carry

            lax.fori_loop(0, tm // CHUNK, step1, 0, unroll=2)
            dbuf[tm:tm + HALO, c] = silu_bwd(cn_ref[:, c].astype(F32), jnp.where(last, 0.0, dn_ref[:, c].astype(F32)))

            def step2(j, carry):
                r0 = pl.multiple_of(j * CHUNK, CHUNK)
                rows = pl.ds(r0, CHUNK)
                win = dbuf[pl.ds(r0, CHUNK + 8), c]
                blocks = _blocks8(win)
                xv = x_ref[rows, c].astype(F32)
                d0 = win[0:CHUNK]
                dx = wk[SSD_K - 1] * d0
                acc[SSD_K - 1, :, c] += d0 * xv
                acc[SSD_K, :, c] += d0
                for adv in range(1, SSD_K):
                    dk = jnp.concatenate(_advance_rows(blocks, adv), axis=0)
                    dx = dx + wk[SSD_K - 1 - adv] * dk
                    acc[SSD_K - 1 - adv, :, c] += dk * xv
                o_ref[rows, c] = dx.astype(o_ref.dtype)
                return carry

            lax.fori_loop(0, tm // CHUNK, step2, 0)

        o_ref[:, XBC:XBC + DT_PAD] = ddt_ref[...]
        o_ref[:, XBC + DT_PAD:XBC_PAD] = jnp.zeros((tm, XBC_PAD - XBC - DT_PAD), o_ref.dtype)

        @pl.when(last)
        def _():
            for k in range(SSD_K):
                dw_ref[k:k + 1, :] = _colsum(acc[k])
            db_ref[...] = _colsum(acc[SSD_K])

    return pl.pallas_call(
        body, grid=(nt,),
        in_specs=[_rows(tm, XBC_PAD, cb), _rows(tm, XBC), _next_rows(tm, HALO, XBC, n),
                  _rows(tm, XBC), _next_rows(tm, HALO, XBC, n), _rows(tm, DT_PAD),
                  _full((SSD_K, XBC)), pl.BlockSpec(memory_space=pl.ANY)],
        out_specs=[_rows(tm, XBC_PAD, cb), _full((SSD_K, XBC)), _full((1, XBC))],
        out_shape=[S(dproj.shape, dproj.dtype), S((SSD_K, XBC), F32), S((1, XBC), F32)],
        scratch_shapes=[pltpu.VMEM((tm + HALO, XBC), F32), pltpu.VMEM((SSD_K + 1, CHUNK, XBC), F32)],
        input_output_aliases={7: 0},
        compiler_params=_cparams("arbitrary"), name=name)(proj, cpre, cpre, dact, dact, ddt, w, dproj)


def _sc_fwd(proj, w, *, name):
    n = proj.shape[0]
    tm = _tile(n, 512)
    cb = P_SC // (3 * SC_W)

    def body(prev_ref, cur_ref, w_ref, o_ref, buf):
        i = pl.program_id(0)
        pv = prev_ref[...].astype(F32)
        cv = cur_ref[...].astype(F32)
        buf[0:HALO, :] = jnp.where(i == 0, 0.0, pv[:, SC_W:2 * SC_W] * pv[:, 2 * SC_W:])
        buf[HALO:HALO + tm, :] = cv[:, SC_W:2 * SC_W] * cv[:, 2 * SC_W:]
        q = _conv(buf, w_ref, SC_K, HALO, tm, SC_W)
        o_ref[...] = (cv[:, 0:SC_W] * q).astype(o_ref.dtype)

    return pl.pallas_call(
        body, grid=(n // tm,),
        in_specs=[_prev_rows(tm, HALO, 3 * SC_W, cb), _rows(tm, 3 * SC_W, cb), _full((SC_K, SC_W))],
        out_specs=_rows(tm, SC_W), out_shape=S((n, SC_W), BF),
        scratch_shapes=[pltpu.VMEM((HALO + tm, SC_W), F32)],
        compiler_params=_cparams("parallel"), name=name)(proj, proj, w)


def _sc_bwd(proj, da, dproj, w, *, name):
    n = proj.shape[0]
    tm = _tile(n, 512)
    nt = n // tm
    cb = P_SC // (3 * SC_W)

    def body(xp_ref, xc_ref, xn_ref, dc_ref, dn_ref, w_ref, dproj_in, o_ref, dw_ref, pbuf, dbuf):
        del dproj_in
        i = pl.program_id(0)
        pv = xp_ref[...].astype(F32)
        cv = xc_ref[...].astype(F32)
        nv = xn_ref[...].astype(F32)
        gb, gc, xv = cv[:, 0:SC_W], cv[:, SC_W:2 * SC_W], cv[:, 2 * SC_W:]
        pbuf[0:HALO, :] = jnp.where(i == 0, 0.0, pv[:, SC_W:2 * SC_W] * pv[:, 2 * SC_W:])
        pbuf[HALO:HALO + tm, :] = gc * xv
        q = _conv(pbuf, w_ref, SC_K, HALO, tm, SC_W)
        dav = dc_ref[...].astype(F32)
        dbuf[0:tm, :] = dav * gb
        dbuf[tm:tm + HALO, :] = jnp.where(i == nt - 1, 0.0, dn_ref[...].astype(F32) * nv[:, 0:SC_W])
        dp = _conv_t(dbuf, w_ref, SC_K, 0, tm, SC_W)
        o_ref[:, 0:SC_W] = (dav * q).astype(o_ref.dtype)
        o_ref[:, SC_W:2 * SC_W] = (dp * xv).astype(o_ref.dtype)
        o_ref[:, 2 * SC_W:] = (dp * gc).astype(o_ref.dtype)

        @pl.when(i == 0)
        def _():
            dw_ref[...] = jnp.zeros_like(dw_ref)

        _conv_dw(dw_ref, dbuf[0:tm, :], pbuf, SC_K, HALO, tm, SC_W)

    return pl.pallas_call(
        body, grid=(nt,),
        in_specs=[_prev_rows(tm, HALO, 3 * SC_W, cb), _rows(tm, 3 * SC_W, cb), _next_rows(tm, HALO, 3 * SC_W, n, cb),
                  _rows(tm, SC_W), _next_rows(tm, HALO, SC_W, n), _full((SC_K, SC_W)),
                  pl.BlockSpec(memory_space=pl.ANY)],
        out_specs=[_rows(tm, 3 * SC_W, cb), _full((SC_K, SC_W))],
        out_shape=[S(dproj.shape, dproj.dtype), S((SC_K, SC_W), F32)],
        scratch_shapes=[pltpu.VMEM((HALO + tm, SC_W), F32), pltpu.VMEM((tm + HALO, SC_W), F32)],
        input_output_aliases={6: 0},
        compiler_params=_cparams("arbitrary"), name=name)(proj, proj, proj, da, da, w, dproj)


def _conf_fwd(proj, w, b, ln_g, ln_b, *, name):
    n = proj.shape[0]
    tm = _tile(n, 512)
    cb = P_CONF // (2 * CONF_W)
    h = CONF_HALO

    def body(prev_ref, cur_ref, w_ref, b_ref, g_ref, be_ref, a_ref, uc_ref, buf):
        i = pl.program_id(0)
        pv = prev_ref[...].astype(F32)
        cv = cur_ref[...].astype(F32)
        buf[0:h, :] = jnp.where(i == 0, 0.0, pv[:, 0:CONF_W] * _sigmoid(pv[:, CONF_W:]))
        buf[h:h + tm, :] = cv[:, 0:CONF_W] * _sigmoid(cv[:, CONF_W:])
        uc = _conv(buf, w_ref, CONF_K, h, tm, CONF_W) + b_ref[...]
        uc_ref[...] = uc.astype(uc_ref.dtype)
        mu = jnp.mean(uc, axis=-1, keepdims=True)
        xc = uc - mu
        v = xc * lax.rsqrt(jnp.mean(xc * xc, axis=-1, keepdims=True) + EPS) * g_ref[...] + be_ref[...]
        a_ref[...] = (v * _sigmoid(v)).astype(a_ref.dtype)

    vec = _full((1, CONF_W))
    return pl.pallas_call(
        body, grid=(n // tm,),
        in_specs=[_prev_rows(tm, h, 2 * CONF_W, cb), _rows(tm, 2 * CONF_W, cb), _full((CONF_K, CONF_W)), vec, vec, vec],
        out_specs=[_rows(tm, CONF_W), _rows(tm, CONF_W)],
        out_shape=[S((n, CONF_W), BF), S((n, CONF_W), BF)],
        scratch_shapes=[pltpu.VMEM((h + tm, CONF_W), F32)],
        compiler_params=_cparams("parallel"), name=name)(proj, proj, w, b, ln_g, ln_b)


def _conf_bwd(proj, uc, da, dproj, w, ln_g, ln_b, *, name, comm=None):
    n = proj.shape[0]
    tm = _tile(n, 512)
    nt = n // tm
    cb = P_CONF // (2 * CONF_W)
    h = CONF_HALO

    def body(xp_ref, xc_ref, ucc_ref, ucn_ref, dac_ref, dan_ref, w_ref, g_ref, be_ref, dproj_in,
             o_ref, dw_ref, db_ref, dg_ref, dbe_ref, ubuf, dbuf):
        del dproj_in
        i = pl.program_id(0)
        pv = xp_ref[...].astype(F32)
        cv = xc_ref[...].astype(F32)
        val, gt = cv[:, 0:CONF_W], cv[:, CONF_W:]
        sg = _sigmoid(gt)
        ubuf[0:h, :] = jnp.where(i == 0, 0.0, pv[:, 0:CONF_W] * _sigmoid(pv[:, CONF_W:]))
        ubuf[h:h + tm, :] = val * sg

        def ln_silu_bwd(ucv, dav):
            mu = jnp.mean(ucv, axis=-1, keepdims=True)
            xc = ucv - mu
            r = lax.rsqrt(jnp.mean(xc * xc, axis=-1, keepdims=True) + EPS)
            xh = xc * r
            v = xh * g_ref[...] + be_ref[...]
            s = _sigmoid(v)
            dv = dav * (s * (1.0 + v * (1.0 - s)))
            dxh = dv * g_ref[...]
            duc = r * (dxh - jnp.mean(dxh, axis=-1, keepdims=True) - xh * jnp.mean(dxh * xh, axis=-1, keepdims=True))
            return duc, dv, xh

        duc, dv, xh = ln_silu_bwd(ucc_ref[...].astype(F32), dac_ref[...].astype(F32))
        dbuf[0:tm, :] = duc
        ducn, _, _ = ln_silu_bwd(ucn_ref[...].astype(F32), dan_ref[...].astype(F32))
        dbuf[tm:tm + h, :] = jnp.where(i == nt - 1, 0.0, ducn)
        du = _conv_t(dbuf, w_ref, CONF_K, 0, tm, CONF_W)
        o_ref[:, 0:CONF_W] = (du * sg).astype(o_ref.dtype)
        o_ref[:, CONF_W:] = (du * val * sg * (1.0 - sg)).astype(o_ref.dtype)

        @pl.when(i == 0)
        def _():
            dw_ref[...] = jnp.zeros_like(dw_ref)
            db_ref[...] = jnp.zeros_like(db_ref)
            dg_ref[...] = jnp.zeros_like(dg_ref)
            dbe_ref[...] = jnp.zeros_like(dbe_ref)

        dg_ref[...] += _colsum(dv * xh)
        dbe_ref[...] += _colsum(dv)
        db_ref[...] += _colsum(duc)
        _conv_dw(dw_ref, duc, ubuf, CONF_K, h, tm, CONF_W)

    vec = _full((1, CONF_W))
    vshape = S((1, CONF_W), F32)
    return _call_with_comm(
        body, nt,
        in_specs=[_prev_rows(tm, h, 2 * CONF_W, cb), _rows(tm, 2 * CONF_W, cb),
                  _rows(tm, CONF_W), _next_rows(tm, h, CONF_W, n), _rows(tm, CONF_W), _next_rows(tm, h, CONF_W, n),
                  _full((CONF_K, CONF_W)), vec, vec, pl.BlockSpec(memory_space=pl.ANY)],
        out_specs=[_rows(tm, 2 * CONF_W, cb), _full((CONF_K, CONF_W)), vec, vec, vec],
        out_shape=[S(dproj.shape, dproj.dtype), S((CONF_K, CONF_W), F32), vshape, vshape, vshape],
        scratch_shapes=[pltpu.VMEM((h + tm, CONF_W), F32), pltpu.VMEM((tm + h, CONF_W), F32)],
        args=(proj, proj, uc, uc, da, da, w, ln_g, ln_b, dproj), comm=comm, name=name, aliases={9: 0})


def _ffn_act_fwd(up, w, b, *, name, comm=None):
    n = up.shape[0]
    tm = _tile(n, 512)
    c2 = 2 * DFF

    def body(prev_ref, cur_ref, w_ref, b_ref, o_ref, u_ref):
        first = pl.program_id(0) == 0
        for s in range(DFF // STRIP):
            cols = (pl.ds(s * STRIP, STRIP), pl.ds(DFF + s * STRIP, STRIP))
            wk = [[w_ref[k:k + 1, c] for k in range(FFN_K)] for c in cols]
            bk = [b_ref[:, c] for c in cols]
            tails = tuple(jnp.where(first, 0.0, prev_ref[:, c].astype(F32)[HALO - 8:HALO]) for c in cols)

            def step(j, tails):
                r0 = pl.multiple_of(j * CHUNK, CHUNK)
                us, new_tails = [], []
                for h in range(2):
                    xv = cur_ref[pl.ds(r0, CHUNK), cols[h]].astype(F32)
                    us.append(_conv3_chunk(tails[h], xv, wk[h]) + bk[h])
                    u_ref[pl.ds(r0, CHUNK), cols[h]] = us[h].astype(u_ref.dtype)
                    new_tails.append(xv[CHUNK - 8:CHUNK])
                o_ref[pl.ds(r0, CHUNK), cols[0]] = (us[0] * _sigmoid(us[0]) * us[1]).astype(o_ref.dtype)
                return tuple(new_tails)

            lax.fori_loop(0, tm // CHUNK, step, tails, unroll=2)

    return _call_with_comm(
        body, n // tm,
        in_specs=[_prev_rows(tm, HALO, c2), _rows(tm, c2), _full((FFN_K, c2)), _full((1, c2))],
        out_specs=[_rows(tm, DFF), _rows(tm, c2)], out_shape=[S((n, DFF), BF), S((n, c2), BF)],
        scratch_shapes=[], args=(up, up, w, b), comm=comm, name=name)


def _ffn_act_bwd(up, u, dact, w, *, name, comm=None):
    n = up.shape[0]
    tm = _tile(n, 512)
    nt = n // tm
    c2 = 2 * DFF

    def body(x_ref, uc_ref, un_ref, dc_ref, dn_ref, w_ref, o_ref, dw_ref, db_ref, dbuf, acc):
        i = pl.program_id(0)
        last = i == nt - 1

        @pl.when(i == 0)
        def _():
            acc[...] = jnp.zeros_like(acc)

        def swiglu_bwd(gate, val, dav):
            sg = _sigmoid(gate)
            return dav * val * (sg * (1.0 + gate * (1.0 - sg))), dav * gate * sg

        for s in range(DFF // STRIP):
            cols = (pl.ds(s * STRIP, STRIP), pl.ds(DFF + s * STRIP, STRIP))
            wk = [[w_ref[k:k + 1, c] for k in range(FFN_K)] for c in cols]

            def step1(j, carry):
                r0 = pl.multiple_of(j * CHUNK, CHUNK)
                rows = pl.ds(r0, CHUNK)
                dus = swiglu_bwd(uc_ref[rows, cols[0]].astype(F32), uc_ref[rows, cols[1]].astype(F32),
                                 dc_ref[rows, cols[0]].astype(F32))
                for h in range(2):
                    dbuf[rows, cols[h]] = dus[h]
                return carry

            lax.fori_loop(0, tm // CHUNK, step1, 0, unroll=2)
            dus = swiglu_bwd(un_ref[:, cols[0]].astype(F32), un_ref[:, cols[1]].astype(F32),
                             jnp.where(last, 0.0, dn_ref[:, cols[0]].astype(F32)))
            for h in range(2):
                dbuf[tm:tm + HALO, cols[h]] = dus[h]

            def step2(j, carry):
                r0 = pl.multiple_of(j * CHUNK, CHUNK)
                rows = pl.ds(r0, CHUNK)
                for h in range(2):
                    win = dbuf[pl.ds(r0, CHUNK + 8), cols[h]]
                    blocks = _blocks8(win)
                    d0 = win[0:CHUNK]
                    d1 = jnp.concatenate(_advance_rows(blocks, 1), axis=0)
                    d2 = jnp.concatenate(_advance_rows(blocks, 2), axis=0)
                    o_ref[rows, cols[h]] = (wk[h][2] * d0 + wk[h][1] * d1 + wk[h][0] * d2).astype(o_ref.dtype)
                    xv = x_ref[rows, cols[h]].astype(F32)
                    acc[2, :, cols[h]] += d0 * xv
                    acc[1, :, cols[h]] += d1 * xv
                    acc[0, :, cols[h]] += d2 * xv
                    acc[FFN_K, :, cols[h]] += d0
                return carry

            lax.fori_loop(0, tm // CHUNK, step2, 0)

        @pl.when(last)
        def _():
            for k in range(FFN_K):
                dw_ref[k:k + 1, :] = _colsum(acc[k])
            db_ref[...] = _colsum(acc[FFN_K])

    return _call_with_comm(
        body, nt,
        in_specs=[_rows(tm, c2), _rows(tm, c2), _next_rows(tm, HALO, c2, n),
                  _rows(tm, DFF), _next_rows(tm, HALO, DFF, n), _full((FFN_K, c2))],
        out_specs=[_rows(tm, c2), _full((FFN_K, c2)), _full((1, c2))],
        out_shape=[S((n, c2), BF), S((FFN_K, c2), F32), S((1, c2), F32)],
        scratch_shapes=[pltpu.VMEM((tm + HALO, c2), F32), pltpu.VMEM((FFN_K + 1, CHUNK, c2), F32)],
        args=(up, u, u, dact, dact, w), comm=comm, name=name)


def _head_consts():
    lane = jnp.arange(INNER) // HEAD_DIM
    rep = (jnp.arange(128)[:, None] == lane[None, :]).astype(BF)
    return rep, rep.T


def _split_dot(v, m):
    hi = v.astype(BF)
    lo = (v - hi.astype(F32)).astype(BF)
    return _dot(hi, m) + _dot(lo, m)


def _chunk_decay_terms(dt_raw, dtb, alog, rep):
    row = lax.broadcasted_iota(jnp.int32, (Q, Q), 0)
    col = lax.broadcasted_iota(jnp.int32, (Q, Q), 1)
    lower = row >= col
    upper = col >= row
    dt = _softplus(dt_raw + dtb)
    a = -jnp.exp(alog)
    adt = dt * a
    acum = lax.dot_general(lower.astype(F32), adt, NN, precision=HIGHEST, preferred_element_type=F32)
    acum_t = lax.dot_general(adt, upper.astype(F32), TN, precision=HIGHEST, preferred_element_type=F32)
    alast = acum[Q - 1:Q, :]
    e = jnp.exp(acum)
    f = jnp.exp(alast - acum)
    ex = _split_dot(jnp.concatenate([dt, e, f, jnp.broadcast_to(jnp.exp(alast), (8, 128))], axis=0), rep)
    return dict(lower=lower, upper=upper, dt=dt, a=a, acum=acum, acum_t=acum_t, alast=alast,
                dt_x=ex[0:Q], e_x=ex[Q:2 * Q], f_x=ex[2 * Q:3 * Q], cd_x=ex[3 * Q:3 * Q + 1])


def _block_diag2(v, lo):
    return jnp.concatenate([jnp.where(lo, v, 0.0), jnp.where(lo, 0.0, v)], axis=0).astype(BF)


def _ssd_fwd(xbc_act, proj, dt_bias, a_log, d_x, norm_g, *, name, comm=None):
    n = xbc_act.shape[0]
    nc = n // Q
    rep, _ = _head_consts()

    def body(xs_ref, bc_ref, dt_ref, z_ref, dtb_ref, alog_ref, dx_ref, ng_ref, rep_ref, y_ref, yn_ref, hp_ref,
             h_scr, y_scr):
        i = pl.program_id(0)

        @pl.when(i == 0)
        def _():
            h_scr[...] = jnp.zeros_like(h_scr)

        hp_ref[...] = h_scr[...]
        t = _chunk_decay_terms(dt_ref[...].astype(F32), dtb_ref[...], alog_ref[...], rep_ref[...])
        xs = xs_ref[...].astype(F32)
        xt = xs * t["dt_x"]
        lo = lax.broadcasted_iota(jnp.int32, (Q, 128), 1) < HEAD_DIM
        gw = INNER // GROUPS
        for g in range(GROUPS):
            bm = bc_ref[:, g * NSTATE:(g + 1) * NSTATE]
            cm = bc_ref[:, GROUPS * NSTATE + g * NSTATE:GROUPS * NSTATE + (g + 1) * NSTATE]
            cb = _dot(cm, bm, NT)
            hg = h_scr[:, g * gw:(g + 1) * gw]
            yoff = _dot(cm, hg.astype(BF))
            for jj in range(gw // 128):
                p = g * (gw // 128) + jj
                sl = slice(p * 128, (p + 1) * 128)
                ws = []
                for hd in (2 * p, 2 * p + 1):
                    seg = t["acum"][:, hd:hd + 1] - t["acum_t"][hd:hd + 1, :]
                    ws.append((cb * jnp.exp(jnp.where(t["lower"], seg, -jnp.inf))).astype(BF))
                ydiag = _dot(jnp.concatenate(ws, axis=1), _block_diag2(xt[:, sl], lo))
                y_scr[:, sl] = ydiag + yoff[:, jj * 128:(jj + 1) * 128] * t["e_x"][:, sl] + dx_ref[:, sl] * xs[:, sl]
            xf = (xt[:, g * gw:(g + 1) * gw] * t["f_x"][:, g * gw:(g + 1) * gw]).astype(BF)
            h_scr[:, g * gw:(g + 1) * gw] = hg * t["cd_x"][:, g * gw:(g + 1) * gw] + _dot(bm, xf, TN)
        y = y_scr[...]
        y_ref[...] = y.astype(y_ref.dtype)
        z = z_ref[...].astype(F32)
        v = y * z * _sigmoid(z)
        for g in range(GROUPS):
            vg = v[:, g * gw:(g + 1) * gw]
            r = lax.rsqrt(jnp.mean(vg * vg, axis=-1, keepdims=True) + EPS)
            yn_ref[:, g * gw:(g + 1) * gw] = (vg * r * ng_ref[:, g * gw:(g + 1) * gw]).astype(yn_ref.dtype)

    vec = _full((1, INNER))
    hv = _full((1, 128))
    return _call_with_comm(
        body, nc,
        in_specs=[_rows(Q, INNER, 0), _rows(Q, 2 * GROUPS * NSTATE, INNER // (2 * GROUPS * NSTATE)),
                  _rows(Q, DT_PAD, P_DT // DT_PAD), _rows(Q, INNER, P_Z // INNER),
                  hv, hv, vec, vec, _full((128, INNER))],
        out_specs=[_rows(Q, INNER), _rows(Q, INNER), pl.BlockSpec((None, NSTATE, INNER), lambda i: (i, 0, 0))],
        out_shape=[S((n, INNER), BF), S((n, INNER), BF), S((nc, NSTATE, INNER), F32)],
        scratch_shapes=[pltpu.VMEM((NSTATE, INNER), F32), pltpu.VMEM((Q, INNER), F32)],
        args=(xbc_act, xbc_act, proj, proj, dt_bias, a_log, d_x, norm_g, rep), comm=comm, name=name)


def _ssd_bwd(xbc_act, proj, y, dyn, hprev, dproj, dt_bias, a_log, d_x, norm_g, *, name):
    n = xbc_act.shape[0]
    nc = n // Q
    rep, sel = _head_consts()
    gw = INNER // GROUPS

    def rev(w, cb=0):
        return pl.BlockSpec((Q, w), lambda i: (nc - 1 - i, cb))

    def body(xs_ref, bc_ref, dt_ref, z_ref, y_ref, dyn_ref, hp_ref, dtb_ref, alog_ref, dx_ref, ng_ref, rep_ref,
             sel_ref, dproj_in, dz_ref, ddt_ref, dxbc_ref, dng_ref, ddtb_ref, dalog_ref, dd_ref,
             dh_scr, dxt_scr, st_scr, off_scr, rs_scr, cs_scr, dng_acc, ddtb_acc, da_acc, dd_acc):
        del dproj_in
        i = pl.program_id(0)

        @pl.when(i == 0)
        def _():
            for r in (dh_scr, dng_acc, ddtb_acc, da_acc, dd_acc):
                r[...] = jnp.zeros_like(r)

        y = y_ref[...].astype(F32)
        z = z_ref[...].astype(F32)
        sz = _sigmoid(z)
        silu = z * sz
        v = y * silu
        dyn = dyn_ref[...].astype(F32)
        dvs = []
        for g in range(GROUPS):
            gs = slice(g * gw, (g + 1) * gw)
            vg = v[:, gs]
            r = lax.rsqrt(jnp.mean(vg * vg, axis=-1, keepdims=True) + EPS)
            vn = vg * r
            dvn = dyn[:, gs] * ng_ref[:, gs]
            dng_acc[:, gs] += _colsum(dyn[:, gs] * vn)
            dvs.append(r * (dvn - vn * jnp.mean(dvn * vn, axis=-1, keepdims=True)))
        dv = jnp.concatenate(dvs, axis=1)
        dy = dv * silu
        dz_ref[...] = (dv * y * (sz * (1.0 + z * (1.0 - sz)))).astype(dz_ref.dtype)

        dt_raw = dt_ref[...].astype(F32)
        t = _chunk_decay_terms(dt_raw, dtb_ref[...], alog_ref[...], rep_ref[...])
        xs = xs_ref[...].astype(F32)
        dsk = dx_ref[...]
        dd_acc[...] += _colsum(dy * xs)
        xt = xs * t["dt_x"]
        dye = dy * t["e_x"]
        xtf = xt * t["f_x"]
        hp = hp_ref[...]
        dh = dh_scr[...]
        lo = lax.broadcasted_iota(jnp.int32, (Q, 128), 1) < HEAD_DIM
        rs_scr[...] = jnp.zeros_like(rs_scr)
        cs_scr[...] = jnp.zeros_like(cs_scr)
        for g in range(GROUPS):
            gs = slice(g * gw, (g + 1) * gw)
            bm = bc_ref[:, g * NSTATE:(g + 1) * NSTATE]
            cm = bc_ref[:, GROUPS * NSTATE + g * NSTATE:GROUPS * NSTATE + (g + 1) * NSTATE]
            cbt = _dot(bm, cm, NT)
            dhg = dh[:, gs].astype(BF)
            hpg = hp[:, gs].astype(BF)
            dxt_state = _dot(bm, dhg) * t["f_x"][:, gs]
            st_scr[:, gs] = dxt_state
            dye_g = dye[:, gs]
            off_scr[:, gs] = dye_g * _dot(cm, hpg)
            dye_b = dye_g.astype(BF)
            db = _dot(xtf[:, gs].astype(BF), dhg, NT)
            dc = _dot(dye_b, hpg, NT)
            dh_scr[:, gs] = t["cd_x"][:, gs] * dh[:, gs] + _dot(cm, dye_b, TN)
            dcbt = jnp.zeros((Q, Q), F32)
            for jj in range(gw // 128):
                p = g * (gw // 128) + jj
                sl = slice(p * 128, (p + 1) * 128)
                lts, wfs = [], []
                for hd in (2 * p, 2 * p + 1):
                    seg_t = t["acum_t"][hd:hd + 1, :] - t["acum"][:, hd:hd + 1]
                    lt = jnp.exp(jnp.where(t["upper"], seg_t, -jnp.inf))
                    lts.append(lt)
                    wfs.append(cbt * lt)
                dyp = dy[:, sl]
                dxt_diag = _dot(jnp.concatenate([w.astype(BF) for w in wfs], axis=1), _block_diag2(dyp, lo))
                dwt2 = _dot(_block_diag2(xt[:, sl], lo), dyp.astype(BF), NT)
                for k, hd in enumerate((2 * p, 2 * p + 1)):
                    dwt = dwt2[k * Q:(k + 1) * Q]
                    dcbt = dcbt + dwt * lts[k]
                    mt = dwt * wfs[k]
                    rs_scr[hd:hd + 1, :] = _colsum(mt)
                    cs_scr[:, hd:hd + 1] = jnp.sum(mt, axis=1, keepdims=True)
                dxt_scr[:, sl] = dxt_diag + dxt_state[:, jj * 128:(jj + 1) * 128]
            dcbt_b = dcbt.astype(BF)
            db = db + _dot(dcbt_b, cm)
            dc = dc + _dot(dcbt_b, bm, TN)
            dxbc_ref[:, INNER + g * NSTATE:INNER + (g + 1) * NSTATE] = db.astype(dxbc_ref.dtype)
            dxbc_ref[:, INNER + (GROUPS + g) * NSTATE:INNER + (GROUPS + g + 1) * NSTATE] = dc.astype(dxbc_ref.dtype)
        dxt = dxt_scr[...]
        dst = st_scr[...]
        sel_m = sel_ref[...]
        sums = _split_dot(jnp.concatenate([off_scr[...], xs * dst, xs * dxt], axis=0), sel_m)
        r1_off, r3_state, r3 = sums[0:Q], sums[Q:2 * Q], sums[2 * Q:3 * Q]
        t1 = _colsum(xt * dst)
        t2 = _colsum(dh * hp)
        tails = _split_dot(jnp.concatenate([jnp.broadcast_to(t1, (8, INNER)), jnp.broadcast_to(t2, (8, INNER))], axis=0),
                           sel_m)
        extra = tails[0:1] + jnp.exp(t["alast"]) * tails[8:9]
        last_row = lax.broadcasted_iota(jnp.int32, (Q, 128), 0) == Q - 1
        da_cum = (rs_scr[...].T - cs_scr[...]) + r1_off - t["dt"] * r3_state + jnp.where(last_row, extra, 0.0)
        dadt = lax.dot_general(t["upper"].astype(F32), da_cum, NN, precision=HIGHEST, preferred_element_type=F32)
        ddt = r3 + t["a"] * dadt
        da_acc[...] += _colsum(dadt * t["dt"])
        real = lax.broadcasted_iota(jnp.int32, (Q, 128), 1) < HEADS
        ddraw = jnp.where(real, ddt * _sigmoid(dt_raw + dtb_ref[...]), 0.0)
        ddt_ref[...] = ddraw.astype(ddt_ref.dtype)
        ddtb_acc[...] += _colsum(ddraw)
        dxbc_ref[:, 0:INNER] = (dy * dsk + dxt * t["dt_x"]).astype(dxbc_ref.dtype)

        @pl.when(i == nc - 1)
        def _():
            dng_ref[...] = dng_acc[...]
            ddtb_ref[...] = ddtb_acc[...]
            dalog_ref[...] = da_acc[...] * t["a"]
            dd_ref[...] = _split_dot(jnp.broadcast_to(dd_acc[...], (8, INNER)), sel_m)[0:1]

    vec = _full((1, INNER))
    hv = _full((1, 128))
    return pl.pallas_call(
        body, grid=(nc,),
        in_specs=[rev(INNER, 0), rev(2 * GROUPS * NSTATE, INNER // (2 * GROUPS * NSTATE)),
                  rev(DT_PAD, P_DT // DT_PAD), rev(INNER, P_Z // INNER), rev(INNER), rev(INNER),
                  pl.BlockSpec((None, NSTATE, INNER), lambda i: (nc - 1 - i, 0, 0)),
                  hv, hv, vec, vec, _full((128, INNER)), _full((INNER, 128)), pl.BlockSpec(memory_space=pl.ANY)],
        out_specs=[rev(INNER, P_Z // INNER), rev(DT_PAD), rev(XBC), vec, hv, hv, hv],
        out_shape=[S(dproj.shape, dproj.dtype), S((n, DT_PAD), BF), S((n, XBC), BF),
                   S((1, INNER), F32), S((1, 128), F32), S((1, 128), F32), S((1, 128), F32)],
        scratch_shapes=[pltpu.VMEM((NSTATE, INNER), F32), pltpu.VMEM((Q, INNER), F32), pltpu.VMEM((Q, INNER), F32),
                        pltpu.VMEM((Q, INNER), F32), pltpu.VMEM((128, Q), F32), pltpu.VMEM((Q, 128), F32),
                        pltpu.VMEM((1, INNER), F32), pltpu.VMEM((1, 128), F32), pltpu.VMEM((1, 128), F32),
                        pltpu.VMEM((1, INNER), F32)],
        input_output_aliases={13: 0},
        compiler_params=_cparams("arbitrary"), name=name)(
            xbc_act, xbc_act, proj, proj, y, dyn, hprev, dt_bias, a_log, d_x, norm_g, rep, sel, dproj)


def _mixer_out_fwd(yn, a_conf, a_sc, proj, b_gate, w_ssd, w_conf, w_sc, w_o, x, mod, *, name):
    n = x.shape[0]
    tm = _tile(n, 256)

    def body(yn_ref, ac_ref, as_ref, gt_ref, bg_ref, wa_ref, wb_ref, wc_ref, wo_ref, x_ref, mod_ref,
             ya_ref, yb_ref, yc_ref, mg_ref, mix_ref, xn_ref):
        ya = _dot(yn_ref[...], wa_ref[...])
        yb = _dot(ac_ref[...], wb_ref[...])
        yc = _dot(as_ref[...], wc_ref[...])
        ya_ref[...] = ya.astype(ya_ref.dtype)
        yb_ref[...] = yb.astype(yb_ref.dtype)
        yc_ref[...] = yc.astype(yc_ref.dtype)
        g = _sigmoid(gt_ref[...].astype(F32) + bg_ref[...])
        merged = (g[:, 0:D] * ya + g[:, D:2 * D] * yb + g[:, 2 * D:] * yc).astype(mg_ref.dtype)
        mg_ref[...] = merged
        mix = _dot(merged, wo_ref[...])
        mix_ref[...] = mix.astype(mix_ref.dtype)
        xn_ref[...] = x_ref[...] + mod_ref[2:3, :] * mix

    act = S((n, D), BF)
    return pl.pallas_call(
        body, grid=(n // tm,),
        in_specs=[_rows(tm, INNER), _rows(tm, CONF_W), _rows(tm, SC_W), _rows(tm, 3 * D, P_GATES // (3 * D)),
                  _full((1, 3 * D)), _full((INNER, D)), _full((CONF_W, D)), _full((SC_W, D)), _full((D, D)),
                  _rows(tm, D), _full((3, D))],
        out_specs=[_rows(tm, D)] * 6,
        out_shape=[act, act, act, act, act, S((n, D), F32)],
        compiler_params=_cparams("parallel"), name=name)(yn, a_conf, a_sc, proj, b_gate, w_ssd, w_conf, w_sc, w_o, x, mod)


def _mixer_out_bwd(dx, mix, ya, yb, yc, proj, b_gate, w_ssd, w_conf, w_sc, w_o, mod, *, name):
    n = dx.shape[0]
    tm = _tile(n, 256)

    def body(dx_ref, mix_ref, ya_ref, yb_ref, yc_ref, gt_ref, bg_ref, wa_ref, wb_ref, wc_ref, wo_ref, mod_ref,
             do_ref, dya_ref, dyb_ref, dyc_ref, dgt_ref, dyn_ref, dac_ref, das_ref, dgm_ref, dbg_ref):
        i = pl.program_id(0)
        dxv = dx_ref[...]
        do = (dxv * mod_ref[2:3, :]).astype(BF)
        do_ref[...] = do
        dm = _dot(do, wo_ref[...], NT)
        g = _sigmoid(gt_ref[...].astype(F32) + bg_ref[...])
        @pl.when(i == 0)
        def _():
            dgm_ref[...] = jnp.zeros_like(dgm_ref)
            dbg_ref[...] = jnp.zeros_like(dbg_ref)

        dys = []
        for j, (y_ref, o_ref) in enumerate(((ya_ref, dya_ref), (yb_ref, dyb_ref), (yc_ref, dyc_ref))):
            gj = g[:, j * D:(j + 1) * D]
            dyj = (dm * gj).astype(BF)
            o_ref[...] = dyj
            dys.append(dyj)
            dgpre = dm * y_ref[...].astype(F32) * gj * (1.0 - gj)
            dgt_ref[:, j * D:(j + 1) * D] = dgpre.astype(dgt_ref.dtype)
            dbg_ref[:, j * D:(j + 1) * D] += _colsum(dgpre)
        dyn_ref[...] = _dot(dys[0], wa_ref[...], NT).astype(dyn_ref.dtype)
        dac_ref[...] = _dot(dys[1], wb_ref[...], NT).astype(dac_ref.dtype)
        das_ref[...] = _dot(dys[2], wc_ref[...], NT).astype(das_ref.dtype)
        dgm_ref[...] += _colsum(dxv * mix_ref[...].astype(F32))

    act = S((n, D), BF)
    return pl.pallas_call(
        body, grid=(n // tm,),
        in_specs=[_rows(tm, D)] * 5 + [_rows(tm, 3 * D, P_GATES // (3 * D)), _full((1, 3 * D)), _full((INNER, D)),
                                       _full((CONF_W, D)), _full((SC_W, D)), _full((D, D)), _full((3, D))],
        out_specs=[_rows(tm, D)] * 4 + [_rows(tm, 3 * D, P_GATES // (3 * D)), _rows(tm, INNER), _rows(tm, CONF_W),
                                        _rows(tm, SC_W), _full((1, D)), _full((1, 3 * D))],
        out_shape=[act, act, act, act, S((n, PW), BF), S((n, INNER), BF), S((n, CONF_W), BF), S((n, SC_W), BF),
                   S((1, D), F32), S((1, 3 * D), F32)],
        compiler_params=_cparams("arbitrary"), name=name)(dx, mix, ya, yb, yc, proj, b_gate, w_ssd, w_conf, w_sc, w_o, mod)


def _pad_w_in(w):
    zeros = jnp.zeros((w.shape[0], XBC_PAD - XBC - (R_CONF - R_DT)), w.dtype)
    return jnp.concatenate([w[:, R_GATES:], w[:, R_SC:R_GATES], w[:, R_XBC:R_DT], w[:, R_DT:R_CONF], zeros,
                            w[:, R_Z:R_XBC], w[:, R_CONF:R_SC]], axis=1)


def _unpad_w_in(wp):
    return jnp.concatenate([wp[:, P_Z:P_Z + INNER], wp[:, P_XBC:P_XBC + XBC], wp[:, P_DT:P_DT + HEADS],
                            wp[:, P_CONF:P_CONF + 2 * CONF_W], wp[:, P_SC:P_SC + 3 * SC_W], wp[:, P_GATES:P_GATES + 3 * D]],
                           axis=1)


W_IN_SHARD = N_IN // N_CHIPS
W_IN_SEGMENTS = ((R_Z, R_XBC, P_Z), (R_XBC, R_DT, P_XBC), (R_DT, R_CONF, P_DT), (R_CONF, R_SC, P_CONF),
                 (R_SC, R_GATES, P_SC), (R_GATES, N_IN, P_GATES))


def _pad_w_in_chips(w4):
    parts = []
    for lo, hi, dst in sorted(W_IN_SEGMENTS, key=lambda sgm: sgm[2]):
        for k in range(N_CHIPS):
            a, b = max(lo, k * W_IN_SHARD), min(hi, (k + 1) * W_IN_SHARD)
            if a < b:
                parts.append((dst + a - lo, w4[k][:, a - k * W_IN_SHARD:b - k * W_IN_SHARD]))
    out, pos = [], 0
    for start, piece in parts:
        if start > pos:
            out.append(jnp.zeros((w4.shape[1], start - pos), w4.dtype))
        out.append(piece)
        pos = start + piece.shape[1]
    if pos < PW:
        out.append(jnp.zeros((w4.shape[1], PW - pos), w4.dtype))
    return jnp.concatenate(out, axis=1)


def _unpad_w_in_chips(wp):
    blocks = []
    for k in range(N_CHIPS):
        pieces = []
        for lo, hi, dst in W_IN_SEGMENTS:
            a, b = max(lo, k * W_IN_SHARD), min(hi, (k + 1) * W_IN_SHARD)
            if a < b:
                pieces.append(wp[:, dst + a - lo:dst + b - lo])
        blocks.append(jnp.concatenate(pieces, axis=1))
    return jnp.stack(blocks)


def _row(v):
    return v.reshape(1, -1)


def _head_row(v):
    return jnp.pad(v, (0, 128 - HEADS)).reshape(1, 128)


def _layer_fwd(x, p, mod_mix, mod_ffn, tag, comm=None, comm_ssd=None, late_params=None):
    sv = {"x0": x}
    h = _modnorm_fwd(x, _row(p["norm_mix_g"]), mod_mix, name=f"modnorm_mix_fwd{tag}")
    proj = _mm_nn(h, p["w_in_pad"], out_dtype=BF, tm=1024, tn=2048, name=f"proj_fwd{tag}")
    xbc_act, cpre = _ssd_pre_fwd(proj, p["ssd_conv_w"], _row(p["ssd_conv_b"]), name=f"ssd_pre_fwd{tag}")
    d_x = _row(jnp.repeat(p["ssd_d"], HEAD_DIM))
    (y, yn, hprev), ssd_comm_out = _ssd_fwd(xbc_act, proj, _head_row(p["ssd_dt_bias"]), _head_row(p["ssd_a_log"]), d_x,
                                            _row(p["ssd_norm_g"]), name=f"ssd_fwd{tag}", comm=comm_ssd)
    if late_params is not None:
        p.update(late_params(ssd_comm_out))
    a_conf, uc = _conf_fwd(proj, p["conf_conv_w"], _row(p["conf_conv_b"]), _row(p["conf_ln_g"]), _row(p["conf_ln_b"]),
                           name=f"conf_fwd{tag}")
    a_sc = _sc_fwd(proj, p["sc_conv_w"], name=f"sc_fwd{tag}")
    ya, yb, yc, merged, mix, x1 = _mixer_out_fwd(yn, a_conf, a_sc, proj, _row(p["b_gate"]), p["w_ssd_out"],
                                                 p["w_conf_out"], p["w_sc_out"], p["w_o"], x, mod_mix,
                                                 name=f"mixer_out_fwd{tag}")
    h2 = _modnorm_fwd(x1, _row(p["norm_ffn_g"]), mod_ffn, name=f"modnorm_ffn_fwd{tag}")
    up = _mm_nn_chips(h2, p["w_up4"], out_dtype=BF, tm=1024, name=f"up_fwd{tag}")
    (act, u_ffn), comm_out = _ffn_act_fwd(up, p["ffn_conv_w"], _row(p["ffn_conv_b"]), name=f"ffn_act_fwd{tag}", comm=comm)
    o, x2 = _mm_resid(act, p["w_down"], x1, mod_ffn, tm=512, name=f"down_fwd{tag}")
    sv.update(h=h, proj=proj, xbc_act=xbc_act, cpre=cpre, d_x=d_x, y=y, yn=yn, hprev=hprev, a_conf=a_conf, uc=uc, a_sc=a_sc,
              ya=ya, yb=yb, yc=yc, merged=merged, mix=mix, x1=x1, h2=h2, up=up, u_ffn=u_ffn, act=act, o=o)
    return x2, sv, comm_out


def _layer_bwd(dx, p, sv, mod_mix, mod_ffn, tag, comm=None, make_comm_conf=None):
    g = {}
    do2, dgate_ffn = _gate_bwd(dx, sv["o"], mod_ffn, name=f"gate_ffn_bwd{tag}")
    dact = _mm_nt(do2, p["w_down"], out_dtype=BF, tm=1024, tk=D, name=f"down_dx{tag}")
    g["w_down"] = _mm_tn(sv["act"], do2, tn=D, tk=1024, name=f"down_dw{tag}")
    (dup, g["ffn_conv_w"], dffn_b), comm_out = _ffn_act_bwd(sv["up"], sv["u_ffn"], dact, p["ffn_conv_w"],
                                                            name=f"ffn_act_bwd{tag}", comm=comm)
    g["ffn_conv_b"] = dffn_b[0]
    dh2 = _mm_nt_chips(dup, p["w_up4"], out_dtype=F32, tm=1024, name=f"up_dx{tag}")
    g["w_up4"] = _mm_tn(sv["h2"], dup, tn=2 * DFF // N_CHIPS, tk=2048, by_chip=True, name=f"up_dw{tag}")
    dx1, dgn, dsh, dsc = _modnorm_bwd(dh2, sv["x1"], dx, _row(p["norm_ffn_g"]), mod_ffn, name=f"modnorm_ffn_bwd{tag}")
    g["norm_ffn_g"] = dgn[0]
    dmod_ffn = jnp.concatenate([dsh[0], dsc[0], dgate_ffn[0]])

    (do1, dya, dyb, dyc, dproj, dyn, dac, dasc, dgate_mix, dbg) = _mixer_out_bwd(
        dx1, sv["mix"], sv["ya"], sv["yb"], sv["yc"], sv["proj"], _row(p["b_gate"]), p["w_ssd_out"], p["w_conf_out"],
        p["w_sc_out"], p["w_o"], mod_mix, name=f"mixer_out_bwd{tag}")
    g["b_gate"] = dbg[0]
    g["w_o"] = _mm_tn(sv["merged"], do1, tn=D, tk=2048, name=f"wo_dw{tag}")
    g["w_ssd_out"] = _mm_tn(sv["yn"], dya, tn=D, tk=2048, name=f"wssd_dw{tag}")
    g["w_conf_out"] = _mm_tn(sv["a_conf"], dyb, tn=D, tk=2048, name=f"wconf_dw{tag}")
    g["w_sc_out"] = _mm_tn(sv["a_sc"], dyc, tn=D, tk=2048, name=f"wsc_dw{tag}")
    comm_conf = make_comm_conf(g) if make_comm_conf is not None else None
    (dproj, g["conf_conv_w"], dcb, dlg, dlb), conf_comm_out = _conf_bwd(
        sv["proj"], sv["uc"], dac, dproj, p["conf_conv_w"], _row(p["conf_ln_g"]), _row(p["conf_ln_b"]),
        name=f"conf_bwd{tag}", comm=comm_conf)
    g["conf_conv_b"], g["conf_ln_g"], g["conf_ln_b"] = dcb[0], dlg[0], dlb[0]
    dproj, g["sc_conv_w"] = _sc_bwd(sv["proj"], dasc, dproj, p["sc_conv_w"], name=f"sc_bwd{tag}")
    dproj, ddt, dxbc_act, dng, ddtb, dalog, ddd = _ssd_bwd(
        sv["xbc_act"], sv["proj"], sv["y"], dyn, sv["hprev"], dproj, _head_row(p["ssd_dt_bias"]),
        _head_row(p["ssd_a_log"]), sv["d_x"], _row(p["ssd_norm_g"]), name=f"ssd_bwd{tag}")
    g["ssd_norm_g"], g["ssd_dt_bias"], g["ssd_a_log"], g["ssd_d"] = dng[0], ddtb[0, :HEADS], dalog[0, :HEADS], ddd[0, :HEADS]
    dproj, g["ssd_conv_w"], dsb = _ssd_pre_bwd(sv["proj"], sv["cpre"], dxbc_act, ddt, dproj, p["ssd_conv_w"],
                                               name=f"ssd_pre_bwd{tag}")
    g["ssd_conv_b"] = dsb[0]
    dh = _mm_nt(dproj, p["w_in_pad"], out_dtype=F32, tm=1024, tk=4096, name=f"proj_dx{tag}")
    g["w_in_pad"] = _mm_tn(sv["h"], dproj, tn=2048, tk=2048, name=f"proj_dw{tag}")
    dx0, dgn, dsh, dsc = _modnorm_bwd(dh, sv["x0"], dx1, _row(p["norm_mix_g"]), mod_mix, name=f"modnorm_mix_bwd{tag}")
    g["norm_mix_g"] = dgn[0]
    dmod_mix = jnp.concatenate([dsh[0], dsc[0], dgate_mix[0]])
    return dx0, g, dmod_mix, dmod_ffn, (comm_out, conf_comm_out)


def _local_step(x, target, layers, mods, final_norm_g):
    saved = []
    for i, p in enumerate(layers):
        x, sv, _ = _layer_fwd(x, p, mods[i][0], mods[i][1], f"_l{i}")
        saved.append(sv)
    dx, loss, dfg = _final_loss(x, _row(final_norm_g), target, name="final_loss")
    grads, dmods = [None] * len(layers), [None] * len(layers)
    for i in reversed(range(len(layers))):
        dx, grads[i], dmm, dmf, _ = _layer_bwd(dx, layers[i], saved[i], mods[i][0], mods[i][1], f"_l{i}")
        dmods[i] = (dmm, dmf)
    return loss, dx, grads, dmods, dfg[0]


MESH = pl.DeviceIdType.MESH
ANY = pl.BlockSpec(memory_space=pl.ANY)
VMEM = pl.BlockSpec(memory_space=pltpu.VMEM)


def _mesh_pos():
    return lax.axis_index("x"), lax.axis_index("y"), lax.axis_index("c")


def _peer(pos, mask):
    return tuple(1 - v if (mask >> (2 - k)) & 1 else v for k, v in enumerate(pos))


def _lin(pos):
    return 4 * pos[0] + 2 * pos[1] + pos[2]


def _chip(pos):
    return 2 * pos[0] + pos[1]


def _rcopy(src, dst, send_sem, recv_sem, dev):
    return pltpu.make_async_remote_copy(src_ref=src, dst_ref=dst, send_sem=send_sem, recv_sem=recv_sem,
                                        device_id=dev, device_id_type=MESH)


CHIP_MASKS = (2, 4, 6)
SIBLING = 1
ADA_COLS = 3 * D // N_CHIPS
CONV_ROWS, CONV_COLS = 48, 2 * DFF // N_CHIPS
CONV_PACK = {"ffn_conv_w": (0, FFN_K, 2 * DFF // N_CHIPS), "ssd_conv_w": (3, SSD_K, XBC // N_CHIPS),
             "conf_conv_w": (8, CONF_K, CONF_W // N_CHIPS), "sc_conv_w": (40, SC_K, SC_W // N_CHIPS)}


def _ada_exchange(c_blk, ada_mix_w, ada_ffn_w, conv_pack):
    def body(c_ref, wm_ref, wf_ref, cw_ref, mods_ref, sc_ref, cwall_ref,
             call_scr, modp_scr, recv_scr, s1, r1, s3, r3, s4, r4):
        pos = _mesh_pos()
        me, km = _lin(pos), _chip(pos)
        call_scr[me] = c_ref[...]
        cwall_ref[km] = cw_ref[...]
        sends = []
        for m in range(1, N_DEV):
            sends.append(_rcopy(c_ref, call_scr.at[me], s1.at[m - 1], r1.at[m - 1], _peer(pos, m)))
        for j, m in enumerate(CHIP_MASKS):
            sends.append(_rcopy(cw_ref, cwall_ref.at[km], s4.at[j], r4.at[j], _peer(pos, m)))
        for cp in sends:
            cp.start()
        for m in range(1, N_DEV):
            src = _peer(pos, m)
            _rcopy(c_ref, call_scr.at[_lin(src)], s1.at[m - 1], r1.at[m - 1], src).wait_recv()
        cm = jnp.concatenate([call_scr[d, 0:1, :] for d in range(N_DEV)], axis=0)
        sc = cm * _sigmoid(cm)
        sc_ref[...] = sc
        for j, w in enumerate((wm_ref.at[0], wm_ref.at[1], wf_ref.at[0], wf_ref.at[1])):
            modp_scr[:, j * ADA_COLS:(j + 1) * ADA_COLS] = lax.dot_general(
                sc, w[...], NN, precision=HIGHEST, preferred_element_type=F32)
        recv_scr[km] = modp_scr[...]
        sends3 = [_rcopy(modp_scr, recv_scr.at[km], s3.at[j], r3.at[j], _peer(pos, m)) for j, m in enumerate(CHIP_MASKS)]
        for cp in sends3:
            cp.start()
        for j, m in enumerate(CHIP_MASKS):
            src = _peer(pos, m)
            _rcopy(modp_scr, recv_scr.at[_chip(src)], s3.at[j], r3.at[j], src).wait_recv()
            _rcopy(cw_ref, cwall_ref.at[_chip(src)], s4.at[j], r4.at[j], src).wait_recv()
        for k in range(N_CHIPS):
            mods_ref[k:k + 1, :] = recv_scr[k, pl.ds(me, 1), :]
        for cp in sends + sends3:
            cp.wait_send()

    dma = pltpu.SemaphoreType.DMA
    return pl.pallas_call(
        body, in_specs=[VMEM] * 4, out_specs=[VMEM] * 3,
        out_shape=[S((N_CHIPS, 4 * ADA_COLS), F32), S((N_DEV, D), F32), S((N_CHIPS,) + conv_pack.shape, F32)],
        scratch_shapes=[pltpu.VMEM((N_DEV, 8, D), F32), pltpu.VMEM((N_DEV, 4 * ADA_COLS), F32),
                        pltpu.VMEM((N_CHIPS, N_DEV, 4 * ADA_COLS), F32),
                        dma((N_DEV - 1,)), dma((N_DEV - 1,)), dma((3,)), dma((3,)), dma((3,)), dma((3,))],
        compiler_params=pltpu.CompilerParams(vmem_limit_bytes=VMEM_LIMIT_V7X), name="ada_exchange")(
            c_blk, ada_mix_w, ada_ffn_w, conv_pack)


class _Comm:
    def __init__(self, ins, out_shapes, scratch, start, finish):
        self.ins, self.out_shapes, self.scratch, self.start, self.finish = list(ins), list(out_shapes), list(scratch), start, finish


def _call_with_comm(body, nsteps, *, in_specs, out_specs, out_shape, scratch_shapes, args, comm, name, aliases=None):
    if comm is None:
        res = pl.pallas_call(body, grid=(nsteps,), in_specs=in_specs, out_specs=out_specs, out_shape=out_shape,
                             scratch_shapes=scratch_shapes, input_output_aliases=aliases or {},
                             compiler_params=_cparams("arbitrary"), name=name)(*args)
        return tuple(res), ()
    ni, no, ns = len(in_specs), len(out_specs), len(scratch_shapes)
    ci, co = len(comm.ins), len(comm.out_shapes)

    def hosted(*refs):
        ins, cins = refs[:ni], refs[ni:ni + ci]
        outs, couts = refs[ni + ci:ni + ci + no], refs[ni + ci + no:ni + ci + no + co]
        scr, csem = refs[ni + ci + no + co:ni + ci + no + co + ns], refs[ni + ci + no + co + ns:]
        i = pl.program_id(0)

        @pl.when(i == 0)
        def _():
            comm.start(cins, couts, csem)

        body(*ins, *outs, *scr)

        @pl.when(i == nsteps - 1)
        def _():
            comm.finish(cins, couts, csem)

    res = pl.pallas_call(
        hosted, grid=(nsteps,), in_specs=list(in_specs) + [ANY] * ci, out_specs=list(out_specs) + [ANY] * co,
        out_shape=list(out_shape) + comm.out_shapes, scratch_shapes=list(scratch_shapes) + comm.scratch,
        input_output_aliases=aliases or {}, compiler_params=_cparams("arbitrary"), name=name)(*args, *comm.ins)
    return tuple(res[:no]), tuple(res[no:])


def _run_comm(comm, name):
    def body(*refs):
        ci, co = len(comm.ins), len(comm.out_shapes)
        comm.start(refs[:ci], refs[ci:ci + co], refs[ci + co:])
        comm.finish(refs[:ci], refs[ci:ci + co], refs[ci + co:])

    return pl.pallas_call(body, in_specs=[ANY] * len(comm.ins), out_specs=[ANY] * len(comm.out_shapes),
                          out_shape=comm.out_shapes, scratch_shapes=comm.scratch, name=name)(*comm.ins)


def _gather_comm(shards, layer):
    na = len(shards)
    dma = pltpu.SemaphoreType.DMA

    def ici(ins, outs, sems, pos, a, j, m, sender):
        ssem, rsem = sems[0], sems[1]
        peer = _peer(pos, m)
        block = outs[a].at[_chip(pos) if sender else _chip(peer)]
        return _rcopy(ins[a].at[layer], block, ssem.at[a, j], rsem.at[a, j], peer)

    def start(ins, outs, sems):
        pos = _mesh_pos()

        @pl.when(pos[2] == layer)
        def _():
            for a in range(na):
                for j, m in enumerate(CHIP_MASKS):
                    ici(ins, outs, sems, pos, a, j, m, True).start()

    def finish(ins, outs, sems):
        fsend, frecv = sems[2], sems[3]
        pos = _mesh_pos()
        sib = _peer(pos, SIBLING)

        def forward(a, j, m):
            blk = outs[a].at[_chip(_peer(pos, m))]
            return _rcopy(blk, blk, fsend.at[a, j], frecv.at[a, j], sib)

        @pl.when(pos[2] == layer)
        def _():
            for a in range(na):
                for j, m in enumerate(CHIP_MASKS):
                    ici(ins, outs, sems, pos, a, j, m, False).wait_recv()
                    forward(a, j, m).start()
            for a in range(na):
                for j, m in enumerate(CHIP_MASKS):
                    ici(ins, outs, sems, pos, a, j, m, True).wait_send()
                    forward(a, j, m).wait_send()

        @pl.when(pos[2] != layer)
        def _():
            for a in range(na):
                for j, m in enumerate(CHIP_MASKS):
                    forward(a, j, m).wait_recv()

    return _Comm(shards, [S((N_CHIPS,) + s.shape[1:], s.dtype) for s in shards],
                 [dma((na, 3)), dma((na, 3)), dma((na, 3)), dma((na, 3))], start, finish)


def _scatter_comm(arrs, layer):
    na = len(arrs)
    dma = pltpu.SemaphoreType.DMA

    def copy(ins, outs, sems, pos, a, j, m, sender):
        peer = _peer(pos, m)
        src = ins[a].at[_chip(peer) if sender else _chip(pos)]
        dst = outs[a].at[_chip(pos) if sender else _chip(peer)]
        return _rcopy(src, dst, sems[0].at[a, j], sems[1].at[a, j], peer)

    def start(ins, outs, sems):
        pos = _mesh_pos()

        @pl.when(pos[2] == layer)
        def _():
            for a in range(na):
                for j, m in enumerate(CHIP_MASKS):
                    copy(ins, outs, sems, pos, a, j, m, True).start()

    def finish(ins, outs, sems):
        pos = _mesh_pos()

        @pl.when(pos[2] == layer)
        def _():
            for a in range(na):
                for j, m in enumerate(CHIP_MASKS):
                    copy(ins, outs, sems, pos, a, j, m, False).wait_recv()
            for a in range(na):
                for j, m in enumerate(CHIP_MASKS):
                    copy(ins, outs, sems, pos, a, j, m, True).wait_send()

    return _Comm(arrs, [S(s.shape, s.dtype) for s in arrs], [dma((na, 3)), dma((na, 3))], start, finish)


def _swap_layer(arrs, layer, tag):
    na = len(arrs)

    def body(*refs):
        ins, outs = refs[:na], refs[na:2 * na]
        ssem, rsem = refs[2 * na:]
        pos = _mesh_pos()
        sib = _peer(pos, SIBLING)
        cps = [_rcopy(ins[a], outs[a], ssem.at[a], rsem.at[a], sib) for a in range(na)]

        @pl.when(pos[2] != layer)
        def _():
            for cp in cps:
                cp.start()
            for cp in cps:
                cp.wait_send()

        @pl.when(pos[2] == layer)
        def _():
            for cp in cps:
                cp.wait_recv()

    dma = pltpu.SemaphoreType.DMA
    return pl.pallas_call(
        body, in_specs=[ANY] * na, out_specs=[ANY] * na, out_shape=[S(s.shape, s.dtype) for s in arrs],
        scratch_shapes=[dma((na,)), dma((na,))], name=f"swap_layer{tag}")(*arrs)


def _share_sibling(arrs):
    na = len(arrs)

    def body(*refs):
        bufs = refs[na:2 * na]
        ssem, rsem = refs[2 * na:]
        pos = _mesh_pos()
        c = pos[2]
        sib = _peer(pos, SIBLING)
        sends = [_rcopy(bufs[a].at[c], bufs[a].at[c], ssem.at[a], rsem.at[a], sib) for a in range(na)]
        for cp in sends:
            cp.start()
        for a in range(na):
            _rcopy(bufs[a].at[c], bufs[a].at[1 - c], ssem.at[a], rsem.at[a], sib).wait_recv()
        for cp in sends:
            cp.wait_send()

    dma = pltpu.SemaphoreType.DMA
    return pl.pallas_call(
        body, in_specs=[ANY] * na, out_specs=[ANY] * na, out_shape=[S(s.shape, s.dtype) for s in arrs],
        input_output_aliases={a: a for a in range(na)},
        scratch_shapes=[dma((na,)), dma((na,))], name="share_sibling")(*arrs)


def _small_allreduce(vec):
    r = vec.shape[0]

    def body(v_ref, sum_ref, all_ref, ssem, rsem):
        pos = _mesh_pos()
        me = _lin(pos)
        all_ref[me] = v_ref[...]
        cps = [_rcopy(v_ref, all_ref.at[me], ssem.at[m - 1], rsem.at[m - 1], _peer(pos, m)) for m in range(1, N_DEV)]
        for cp in cps:
            cp.start()
        for m in range(1, N_DEV):
            src = _peer(pos, m)
            _rcopy(v_ref, all_ref.at[_lin(src)], ssem.at[m - 1], rsem.at[m - 1], src).wait_recv()
        acc = all_ref[0]
        for d in range(1, N_DEV):
            acc = acc + all_ref[d]
        sum_ref[...] = acc
        for cp in cps:
            cp.wait_send()

    dma = pltpu.SemaphoreType.DMA
    return pl.pallas_call(
        body, in_specs=[VMEM], out_specs=[VMEM, VMEM],
        out_shape=[S((r, 128), F32), S((N_DEV, r, 128), F32)],
        scratch_shapes=[dma((N_DEV - 1,)), dma((N_DEV - 1,))],
        compiler_params=pltpu.CompilerParams(vmem_limit_bytes=VMEM_LIMIT_V7X), name="small_allreduce")(vec)


ROW_BYTES_TARGET = 1 << 20


def _row_tile(rows, cols, itemsize=4):
    t = rows
    while t % 2 == 0 and t * cols * itemsize > ROW_BYTES_TARGET and (t // 2) % 16 == 0:
        t //= 2
    return t


def _pair_add(g, other):
    r, cdim = g.shape
    tr = _row_tile(r, cdim)

    def body(g_ref, o_ref, out_ref):
        out_ref[...] = (g_ref[...].astype(F32) + o_ref[...].astype(F32)).astype(out_ref.dtype)

    blk = pl.BlockSpec((tr, cdim), lambda i: (i, 0))
    return pl.pallas_call(body, grid=(r // tr,), in_specs=[blk, blk], out_specs=blk, out_shape=S((r, cdim), BF),
                          compiler_params=_cparams("parallel"), name="pair_add")(g, other)


def _sum4(parts, pairs, chip_core):
    _, r, cdim = parts[0].shape
    tr = _row_tile(r, cdim)

    def body(kc_ref, q0_ref, q1_ref, own0_ref, own1_ref, out_ref):
        first = kc_ref[1] == 0
        mine = jnp.where(first, own0_ref[...], own1_ref[...]).astype(F32)
        terms = [jnp.where(kc_ref[0] == j, mine, jnp.where(first, q0_ref[j], q1_ref[j]).astype(F32))
                 for j in range(N_CHIPS)]
        out_ref[...] = ((terms[0] + terms[1]) + terms[2]) + terms[3]

    allc = pl.BlockSpec((N_CHIPS, tr, cdim), lambda i, kc: (0, i, 0))
    own = pl.BlockSpec((None, tr, cdim), lambda i, kc: (kc[0], i, 0))
    return pl.pallas_call(
        body,
        grid_spec=pltpu.PrefetchScalarGridSpec(
            num_scalar_prefetch=1, grid=(r // tr,), in_specs=[allc, allc, own, own],
            out_specs=pl.BlockSpec((None, tr, cdim), lambda i, kc: (kc[1], i, 0))),
        out_shape=S((DEPTH, r, cdim), F32), compiler_params=_cparams("parallel"), name="sum4")(
            chip_core, parts[0], parts[1], pairs[0], pairs[1])


def _ada_w_grad(silu_c, dmod_cols, chip):
    def body(k_ref, sc_ref, dm_ref, o_ref):
        del k_ref
        o_ref[...] = lax.dot_general(sc_ref[...], dm_ref[...], TN, precision=HIGHEST, preferred_element_type=F32)

    return pl.pallas_call(
        body,
        grid_spec=pltpu.PrefetchScalarGridSpec(
            num_scalar_prefetch=1, grid=(4,),
            in_specs=[pl.BlockSpec((N_DEV, D), lambda j, k: (0, 0)),
                      pl.BlockSpec((None, N_DEV, ADA_COLS), lambda j, k: (4 * j + k[0], 0, 0))],
            out_specs=pl.BlockSpec((None, D, ADA_COLS), lambda j, k: (j, 0, 0))),
        out_shape=S((4, D, ADA_COLS), F32), compiler_params=_cparams("parallel"), name="ada_w_grad")(chip, silu_c, dmod_cols)


def _adamw(w, g, m, v, *, name):
    r, cdim = w.shape
    tr = _row_tile(r, cdim)
    c1 = 1.0 / (1.0 - ADAM_B1 ** ADAM_STEP)
    c2 = 1.0 / (1.0 - ADAM_B2 ** ADAM_STEP)

    def body(w_ref, g_ref, m_ref, v_ref, d_ref, mo_ref, vo_ref):
        gv = g_ref[...]
        mn = ADAM_B1 * m_ref[...] + (1.0 - ADAM_B1) * gv
        vn = ADAM_B2 * v_ref[...] + (1.0 - ADAM_B2) * (gv * gv)
        mo_ref[...] = mn
        vo_ref[...] = vn
        d_ref[...] = -ADAM_LR * ((mn * c1) / (jnp.sqrt(vn * c2) + ADAM_EPS) + ADAM_WD * w_ref[...])

    blk = pl.BlockSpec((tr, cdim), lambda i: (i, 0))
    return pl.pallas_call(
        body, grid=(r // tr,), in_specs=[blk] * 4, out_specs=[blk] * 3, out_shape=[S((r, cdim), F32)] * 3,
        compiler_params=_cparams("parallel"), name=name)(w, g, m, v)


def _adamw_many(ws, gs, ms, vs):
    n = len(ws)
    c1 = 1.0 / (1.0 - ADAM_B1 ** ADAM_STEP)
    c2 = 1.0 / (1.0 - ADAM_B2 ** ADAM_STEP)

    def body(*refs):
        for i in range(n):
            w_ref, g_ref, m_ref, v_ref, d_ref, mo_ref, vo_ref = (refs[k * n + i] for k in range(7))
            gv = g_ref[...]
            mn = ADAM_B1 * m_ref[...] + (1.0 - ADAM_B1) * gv
            vn = ADAM_B2 * v_ref[...] + (1.0 - ADAM_B2) * (gv * gv)
            mo_ref[...] = mn
            vo_ref[...] = vn
            d_ref[...] = -ADAM_LR * ((mn * c1) / (jnp.sqrt(vn * c2) + ADAM_EPS) + ADAM_WD * w_ref[...])

    shapes = [S(a.shape, F32) for a in ws]
    outs = pl.pallas_call(
        body, in_specs=[VMEM] * (4 * n), out_specs=[VMEM] * (3 * n), out_shape=shapes * 3,
        compiler_params=pltpu.CompilerParams(vmem_limit_bytes=VMEM_LIMIT_V7X), name="adamw_small")(*ws, *gs, *ms, *vs)
    return outs[0:n], outs[n:2 * n], outs[2 * n:3 * n]


WEIGHTS = ['ada_mix_w', 'ada_mix_b', 'norm_mix_g', 'w_in', 'b_gate', 'ssd_conv_w', 'ssd_conv_b', 'ssd_dt_bias',
           'ssd_a_log', 'ssd_d', 'ssd_norm_g', 'w_ssd_out', 'conf_conv_w', 'conf_conv_b', 'conf_ln_g', 'conf_ln_b',
           'w_conf_out', 'sc_conv_w', 'w_sc_out', 'w_o', 'ada_ffn_w', 'ada_ffn_b', 'norm_ffn_g', 'w_up', 'ffn_conv_w',
           'ffn_conv_b', 'w_down', 'final_norm_g']
SMALL = ['ada_mix_b', 'norm_mix_g', 'b_gate', 'ssd_conv_b', 'ssd_dt_bias', 'ssd_a_log', 'ssd_d', 'ssd_norm_g', 'conf_conv_b',
         'conf_ln_g', 'conf_ln_b', 'ada_ffn_b', 'norm_ffn_g', 'ffn_conv_b']
CONVS = ['ssd_conv_w', 'conf_conv_w', 'sc_conv_w', 'ffn_conv_w']
BIG = ['ada_mix_w', 'ada_ffn_w', 'w_in', 'w_up', 'w_conf_out', 'w_sc_out', 'w_ssd_out', 'w_o', 'w_down']


def _pack_rows(pieces):
    flat = [p.reshape(-1) for p in pieces]
    offs, o = [], 0
    for f in flat:
        offs.append(o)
        o += f.shape[0]
    total = -(-o // 1024) * 1024
    vec = jnp.concatenate(flat + [jnp.zeros((total - o,), F32)])
    return vec.reshape(total // 128, 128), offs


def _by_chip(a, axis):
    shp = a.shape
    a = a.reshape(shp[:axis] + (N_CHIPS, shp[axis] // N_CHIPS) + shp[axis + 1:])
    return jnp.moveaxis(a, axis, 0)


def _from_chips(a, axis):
    a = jnp.moveaxis(a, 0, axis)
    shp = a.shape
    return a.reshape(shp[:axis] + (shp[axis] * shp[axis + 1],) + shp[axis + 2:])


def kernel(x, c, ada_mix_w, ada_mix_b, norm_mix_g, w_in, b_gate, ssd_conv_w, ssd_conv_b, ssd_dt_bias, ssd_a_log, ssd_d, ssd_norm_g, w_ssd_out, conf_conv_w, conf_conv_b, conf_ln_g, conf_ln_b, w_conf_out, sc_conv_w, w_sc_out, w_o, ada_ffn_w, ada_ffn_b, norm_ffn_g, w_up, ffn_conv_w, ffn_conv_b, w_down, final_norm_g, loss_target, m_ada_mix_w, m_ada_mix_b, m_norm_mix_g, m_w_in, m_b_gate, m_ssd_conv_w, m_ssd_conv_b, m_ssd_dt_bias, m_ssd_a_log, m_ssd_d, m_ssd_norm_g, m_w_ssd_out, m_conf_conv_w, m_conf_conv_b, m_conf_ln_g, m_conf_ln_b, m_w_conf_out, m_sc_conv_w, m_w_sc_out, m_w_o, m_ada_ffn_w, m_ada_ffn_b, m_norm_ffn_g, m_w_up, m_ffn_conv_w, m_ffn_conv_b, m_w_down, m_final_norm_g, v_ada_mix_w, v_ada_mix_b, v_norm_mix_g, v_w_in, v_b_gate, v_ssd_conv_w, v_ssd_conv_b, v_ssd_dt_bias, v_ssd_a_log, v_ssd_d, v_ssd_norm_g, v_w_ssd_out, v_conf_conv_w, v_conf_conv_b, v_conf_ln_g, v_conf_ln_b, v_w_conf_out, v_sc_conv_w, v_w_sc_out, v_w_o, v_ada_ffn_w, v_ada_ffn_b, v_norm_ffn_g, v_w_up, v_ffn_conv_w, v_ffn_conv_b, v_w_down, v_final_norm_g):
    args = locals()
    w = {n: args[n] for n in WEIGHTS}
    mom = {n: args["m_" + n] for n in WEIGHTS}
    var = {n: args["v_" + n] for n in WEIGHTS}
    pos = _mesh_pos()
    chip = _chip(pos)
    core = pos[2]

    conv_pack = jnp.zeros((DEPTH, CONV_ROWS, CONV_COLS), F32)
    for n, (r0, taps, width) in CONV_PACK.items():
        conv_pack = conv_pack.at[:, r0:r0 + taps, 0:width].set(w[n])
    c_blk = jnp.pad(c, ((0, 7), (0, 0)))
    mods_raw, silu_c, conv_all = _ada_exchange(c_blk, ada_mix_w, ada_ffn_w, conv_pack)
    ada_b = jnp.concatenate([ada_mix_b, ada_ffn_b], axis=0)
    mod_all = mods_raw.reshape(N_CHIPS, 4, ADA_COLS).transpose(1, 0, 2).reshape(4, 3 * D) + ada_b
    mod_all = mod_all.reshape(4, 3, D)
    mods = [(mod_all[i], mod_all[2 + i]) for i in range(DEPTH)]
    conv_full = {n: _from_chips(conv_all[:, :, r0:r0 + taps, 0:width], 2) for n, (r0, taps, width) in CONV_PACK.items()}

    cast = lambda a: a.astype(BF)
    shards = [cast(w_in), jnp.concatenate([cast(w_conf_out), cast(w_sc_out)], axis=1),
              jnp.concatenate([cast(w_ssd_out), cast(w_o)], axis=1), cast(w_up), cast(w_down)]
    EARLY, LATE = (0, 1, 2), (3, 4)
    pick = lambda arrs, idx: [arrs[k] for k in idx]

    def own_block(l, gathered, idx):
        return [lax.dynamic_update_slice(g, shards[k][l][None], (chip, 0, 0)) for g, k in zip(gathered, idx)]

    def early_params(l, gathered):
        g_in, g_cs, g_row = own_block(l, gathered, EARLY)
        p = {n: w[n][l] for n in SMALL if not n.startswith("ada_")}
        p.update({n: conv_full[n][l] for n in CONVS})
        p["w_in_pad"] = _pad_w_in_chips(g_in)
        p["w_conf_out"] = _from_chips(g_cs[:, 0:CONF_W], 1)
        p["w_sc_out"] = _from_chips(g_cs[:, CONF_W:], 1)
        p["w_ssd_out"] = _from_chips(g_row[:, 0:INNER // N_CHIPS], 0)
        p["w_o"] = _from_chips(g_row[:, INNER // N_CHIPS:], 0)
        return p

    def late_params(l, gathered):
        g_up, g_down = own_block(l, gathered, LATE)
        return {"w_up4": g_up, "w_down": _from_chips(g_down, 0)}

    def big_grads(g, idx):
        makers = (lambda: _unpad_w_in_chips(g["w_in_pad"]),
                  lambda: _by_chip(jnp.concatenate([g["w_conf_out"], g["w_sc_out"]], axis=0), 1),
                  lambda: jnp.concatenate([_by_chip(g["w_ssd_out"], 0), _by_chip(g["w_o"], 0)], axis=1),
                  lambda: g["w_up4"], lambda: _by_chip(g["w_down"], 0))
        return [makers[k]().astype(BF) for k in idx]

    def pair_sums(big, l, tag):
        theirs = _swap_layer(big, l, tag)
        out = []
        for g4, t4 in zip(big, theirs):
            k, r, cdim = g4.shape
            out.append(_pair_add(g4.reshape(k * r, cdim), t4.reshape(k * r, cdim)).reshape(k, r, cdim))
        return out

    seq = x.shape[1]
    xs = x.reshape(seq, D)
    p0 = early_params(0, _run_comm(_gather_comm(pick(shards, EARLY), 0), "gather_weights_l0"))
    xs, sv0, gathered1 = _layer_fwd(xs, p0, mods[0][0], mods[0][1], "_l0", comm=_gather_comm(shards, 1),
                                    comm_ssd=_gather_comm(pick(shards, LATE), 0),
                                    late_params=lambda got: late_params(0, got))
    p1 = early_params(1, pick(gathered1, EARLY))
    p1.update(late_params(1, pick(gathered1, LATE)))
    xs, sv1, _ = _layer_fwd(xs, p1, mods[1][0], mods[1][1], "_l1")
    dx, loss, dfg = _final_loss(xs, _row(final_norm_g), loss_target.reshape(seq, D), name="final_loss")
    dfinal = dfg[0]

    every = EARLY + LATE
    grads, dmods = [None, None], [None, None]
    dx, grads[1], dmm, dmf, _ = _layer_bwd(dx, p1, sv1, mods[1][0], mods[1][1], "_l1")
    dmods[1] = (dmm, dmf)
    pair1 = pair_sums(big_grads(grads[1], every), 1, "_l1")
    late0 = {}

    def scatter_late0(g):
        late0["pair"] = pair_sums(big_grads(g, LATE), 0, "_l0_ffn")
        return _scatter_comm(late0["pair"], 0)

    dx, grads[0], dmm, dmf, (parts1, parts0_late) = _layer_bwd(
        dx, p0, sv0, mods[0][0], mods[0][1], "_l0", comm=_scatter_comm(pair1, 1), make_comm_conf=scatter_late0)
    dmods[0] = (dmm, dmf)
    pair0_early = pair_sums(big_grads(grads[0], EARLY), 0, "_l0")
    parts0_early = _run_comm(_scatter_comm(pair0_early, 0), "scatter_chips_l0")
    pair0 = list(pair0_early) + list(late0["pair"])
    parts0 = list(parts0_early) + list(parts0_late)
    chip_core = jnp.stack([chip, core]).astype(jnp.int32)
    red = _share_sibling([_sum4((q0, q1), (o0, o1), chip_core)
                          for q0, q1, o0, o1 in zip(parts0, parts1, pair0, pair1)])
    reduced = {"w_in": red[0], "cs": red[1], "row": red[2], "w_up": red[3], "w_down": red[4]}
    gw = {"w_in": reduced["w_in"], "w_up": reduced["w_up"], "w_down": reduced["w_down"],
          "w_conf_out": reduced["cs"][:, 0:CONF_W], "w_sc_out": reduced["cs"][:, CONF_W:],
          "w_ssd_out": reduced["row"][:, 0:INNER // N_CHIPS], "w_o": reduced["row"][:, INNER // N_CHIPS:]}

    dmod = jnp.stack([dmods[0][0], dmods[1][0], dmods[0][1], dmods[1][1]])
    small_local = {n: jnp.stack([grads[l][n] for l in range(DEPTH)]) for n in SMALL if not n.startswith("ada_")}
    pieces = [loss[0]] + [small_local[n] for n in SMALL if not n.startswith("ada_")]
    pieces += [jnp.stack([grads[l][n] for l in range(DEPTH)]) for n in CONVS] + [dfinal, dmod]
    vec, offs = _pack_rows(pieces)
    vsum, vall = _small_allreduce(vec)
    flat = vsum.reshape(-1)

    def piece(k, like):
        return flat[offs[k]:offs[k] + like.size].reshape(like.shape)

    loss_out = flat[0]
    k = 1
    for n in SMALL:
        if not n.startswith("ada_"):
            gw[n] = piece(k, small_local[n])
            k += 1
    for n in CONVS:
        full = piece(k, conv_full[n])
        width = CONV_PACK[n][2]
        gw[n] = lax.dynamic_slice_in_dim(full, chip * width, width, axis=2)
        k += 1
    gw["final_norm_g"] = piece(k, dfinal)
    k += 1
    dmod_sum = piece(k, dmod)
    gw["ada_mix_b"], gw["ada_ffn_b"] = dmod_sum[0:2], dmod_sum[2:4]
    dmod_all = vall.reshape(N_DEV, -1)[:, offs[k]:offs[k] + dmod.size]
    dmod_cols = dmod_all.reshape(N_DEV, 4 * N_CHIPS, ADA_COLS).transpose(1, 0, 2)
    ada_g = _ada_w_grad(silu_c, dmod_cols, jnp.reshape(chip, (1,)).astype(jnp.int32))
    gw["ada_mix_w"], gw["ada_ffn_w"] = ada_g[0:2], ada_g[2:4]

    delta, new_m, new_v = {}, {}, {}
    for n in BIG:
        shp = w[n].shape
        two_d = lambda a: a.reshape(shp[0] * shp[1], shp[2])
        d_, m_, v_ = _adamw(two_d(w[n]), two_d(gw[n]), two_d(mom[n]), two_d(var[n]), name=f"adamw_{n}")
        delta[n], new_m[n], new_v[n] = d_.reshape(shp), m_.reshape(shp), v_.reshape(shp)
    rest = [n for n in WEIGHTS if n not in BIG]
    two_d = lambda a: a.reshape(1, -1) if a.ndim == 1 else a
    outs = _adamw_many(*[[two_d(src[n]) for n in rest] for src in (w, gw, mom, var)])
    for dst, group in zip((delta, new_m, new_v), outs):
        for n, o in zip(rest, group):
            dst[n] = o.reshape(w[n].shape)

    return (loss_out, dx[None], *[gw[n] for n in WEIGHTS], *[delta[n] for n in WEIGHTS],
            *[new_m[n] for n in WEIGHTS], *[new_v[n] for n in WEIGHTS])
```

```python
import functools

import jax
import jax.numpy as jnp
from jax import lax
from jax.experimental import pallas as pl
from jax.experimental.pallas import tpu as pltpu

F32 = jnp.float32
BF = jnp.bfloat16
S = jax.ShapeDtypeStruct

D = 1024
HEADS = 16
HEAD_DIM = 64
INNER = HEADS * HEAD_DIM
GROUPS = 2
NSTATE = 64
Q = 128
SSD_K = 4
XBC = INNER + 2 * GROUPS * NSTATE
CONF_W = 512
CONF_K = 31
SC_W = 512
SC_K = 3
DFF = 2816
FFN_K = 3
EPS = 1e-6
DEPTH = 2
R_Z, R_XBC, R_DT, R_CONF, R_SC, R_GATES, N_IN = 0, 1024, 2304, 2320, 3344, 4880, 7952
P_GATES, P_SC, P_XBC, P_Z, P_CONF, PW = 0, 3072, 4608, 6144, 7168, 8192
XBC_PAD = 1536
DT_PAD = 128
P_DT = P_XBC + XBC
N_CHIPS = 4
N_DEV = 8

ADAM_LR, ADAM_B1, ADAM_B2, ADAM_EPS, ADAM_WD, ADAM_STEP = 0.001, 0.9, 0.999, 1e-08, 0.01, 10

VMEM_LIMIT_V7X = 56 * 1024 * 1024
HIGHEST = lax.Precision.HIGHEST


def _cparams(*sem):
    return pltpu.CompilerParams(dimension_semantics=sem, vmem_limit_bytes=VMEM_LIMIT_V7X)


def _full(shape):
    n = len(shape)
    return pl.BlockSpec(shape, lambda *_: (0,) * n)


def _rows(tm, w, cb=0):
    return pl.BlockSpec((tm, w), lambda i: (i, cb))


def _prev_rows(tm, halo, w, cb=0):
    r = tm // halo
    return pl.BlockSpec((halo, w), lambda i: (jnp.maximum(i * r - 1, 0), cb))


def _next_rows(tm, halo, w, nrows, cb=0):
    r = tm // halo
    last = nrows // halo - 1
    return pl.BlockSpec((halo, w), lambda i: (jnp.minimum((i + 1) * r, last), cb))


def _sigmoid(v):
    return 1.0 / (1.0 + jnp.exp(-v))


def _softplus(v):
    return jnp.maximum(v, 0.0) + jnp.log(1.0 + jnp.exp(-jnp.abs(v)))


def _colsum(v):
    return jnp.sum(v, axis=0, keepdims=True)


def _tile(n, want):
    t = min(n, want)
    assert n % t == 0, (n, want)
    return t


NN = (((1,), (0,)), ((), ()))
NT = (((1,), (1,)), ((), ()))
TN = (((0,), (0,)), ((), ()))


def _dot(a, b, dims=NN):
    return lax.dot_general(a, b, dims, preferred_element_type=F32)


def _mm(a, b, *, dims, grid, a_spec, b_spec, o_spec, out_shape, acc_shape, name, comm=None):
    nk = grid[2]

    def body(a_ref, b_ref, o_ref, acc_ref):
        k = pl.program_id(2)
        part = _dot(a_ref[...], b_ref[...], dims)
        if nk == 1:
            o_ref[...] = part.astype(o_ref.dtype)
        else:
            @pl.when(k == 0)
            def _():
                acc_ref[...] = part

            @pl.when(k > 0)
            def _():
                acc_ref[...] += part

            @pl.when(k == nk - 1)
            def _():
                o_ref[...] = acc_ref[...].astype(o_ref.dtype)

    scratch = [pltpu.VMEM(acc_shape if nk > 1 else (8, 128), F32)]
    if comm is not None:
        (out,), comm_out = _call_with_comm(body, tuple(grid), in_specs=[a_spec, b_spec], out_specs=[o_spec],
                                           out_shape=[out_shape], scratch_shapes=scratch, args=(a, b), comm=comm, name=name)
        return out, comm_out
    return pl.pallas_call(
        body, grid=grid, in_specs=[a_spec, b_spec], out_specs=o_spec, out_shape=out_shape, scratch_shapes=scratch,
        compiler_params=_cparams("parallel", "parallel", "arbitrary"), name=name)(a, b)


def _mm_nn(a, b, *, out_dtype, tm, tn, name):
    m, k = a.shape
    n = b.shape[1]
    tm, tn = _tile(m, tm), _tile(n, tn)
    return _mm(a, b, dims=NN, grid=(m // tm, n // tn, 1),
               a_spec=pl.BlockSpec((tm, k), lambda i, j, kk: (i, 0)),
               b_spec=pl.BlockSpec((k, tn), lambda i, j, kk: (0, j)),
               o_spec=pl.BlockSpec((tm, tn), lambda i, j, kk: (i, j)),
               out_shape=S((m, n), out_dtype), acc_shape=(tm, tn), name=name)


def _mm_nt(a, b, *, out_dtype, tm, tk, name):
    m, kc = a.shape
    n = b.shape[0]
    tm, tk = _tile(m, tm), _tile(kc, tk)
    return _mm(a, b, dims=NT, grid=(m // tm, 1, kc // tk),
               a_spec=pl.BlockSpec((tm, tk), lambda i, j, kk: (i, kk)),
               b_spec=pl.BlockSpec((n, tk), lambda i, j, kk: (0, kk)),
               o_spec=pl.BlockSpec((tm, n), lambda i, j, kk: (i, 0)),
               out_shape=S((m, n), out_dtype), acc_shape=(tm, n), name=name)


def _mm_tn(a, b, *, tn, tk, name, out_dtype=BF, by_chip=False):
    kc, m = a.shape
    n = b.shape[1]
    tn, tk = _tile(n, tn), _tile(kc, tk)
    if by_chip:
        assert n == N_CHIPS * tn
        o_spec, out_shape = pl.BlockSpec((None, m, tn), lambda i, j, kk: (j, 0, 0)), S((N_CHIPS, m, tn), out_dtype)
    else:
        o_spec, out_shape = pl.BlockSpec((m, tn), lambda i, j, kk: (0, j)), S((m, n), out_dtype)
    return _mm(a, b, dims=TN, grid=(1, n // tn, kc // tk),
               a_spec=pl.BlockSpec((tk, m), lambda i, j, kk: (kk, 0)),
               b_spec=pl.BlockSpec((tk, tn), lambda i, j, kk: (kk, j)),
               o_spec=o_spec, out_shape=out_shape, acc_shape=(m, tn), name=name)


def _mm_nn_chips(a, b4, *, out_dtype, tm, name, comm=None):
    m, k = a.shape
    n4 = b4.shape[2]
    tm = _tile(m, tm)
    return _mm(a, b4, dims=NN, grid=(m // tm, N_CHIPS, 1),
               a_spec=pl.BlockSpec((tm, k), lambda i, j, kk: (i, 0)),
               b_spec=pl.BlockSpec((None, k, n4), lambda i, j, kk: (j, 0, 0)),
               o_spec=pl.BlockSpec((tm, n4), lambda i, j, kk: (i, j)),
               out_shape=S((m, N_CHIPS * n4), out_dtype), acc_shape=(tm, n4), name=name, comm=comm)


def _mm_nt_chips(a, b4, *, out_dtype, tm, name):
    m = a.shape[0]
    n, n4 = b4.shape[1], b4.shape[2]
    tm = _tile(m, tm)
    return _mm(a, b4, dims=NT, grid=(m // tm, 1, N_CHIPS),
               a_spec=pl.BlockSpec((tm, n4), lambda i, j, kk: (i, kk)),
               b_spec=pl.BlockSpec((None, n, n4), lambda i, j, kk: (kk, 0, 0)),
               o_spec=pl.BlockSpec((tm, n), lambda i, j, kk: (i, 0)),
               out_shape=S((m, n), out_dtype), acc_shape=(tm, n), name=name)


def _mm_resid(a, b, x, mod, *, tm, name):
    m, k = a.shape
    n = b.shape[1]
    tm = _tile(m, tm)

    def body(a_ref, b_ref, x_ref, mod_ref, o_ref, xn_ref):
        o = _dot(a_ref[...], b_ref[...])
        o_ref[...] = o.astype(o_ref.dtype)
        xn_ref[...] = x_ref[...] + mod_ref[2:3, :] * o

    return pl.pallas_call(
        body, grid=(m // tm,),
        in_specs=[_rows(tm, k), _full((k, n)), _rows(tm, n), _full((3, n))],
        out_specs=[_rows(tm, n), _rows(tm, n)],
        out_shape=[S((m, n), BF), S((m, n), F32)],
        compiler_params=_cparams("parallel"), name=name)(a, b, x, mod)


def _modnorm_fwd(x, gain, mod, *, name):
    n = x.shape[0]
    tm = _tile(n, 512)

    def body(x_ref, g_ref, mod_ref, h_ref):
        xv = x_ref[...]
        r = lax.rsqrt(jnp.mean(xv * xv, axis=-1, keepdims=True) + EPS)
        y = xv * r * g_ref[...]
        h_ref[...] = (y * (1.0 + mod_ref[1:2, :]) + mod_ref[0:1, :]).astype(h_ref.dtype)

    return pl.pallas_call(
        body, grid=(n // tm,), in_specs=[_rows(tm, D), _full((1, D)), _full((3, D))],
        out_specs=_rows(tm, D), out_shape=S((n, D), BF), compiler_params=_cparams("parallel"), name=name)(x, gain, mod)


def _modnorm_bwd(dh, x, dres, gain, mod, *, name):
    n = x.shape[0]
    tm = _tile(n, 512)

    def body(dh_ref, x_ref, dres_ref, g_ref, mod_ref, dx_ref, dg_ref, dsh_ref, dsc_ref):
        i = pl.program_id(0)
        xv = x_ref[...]
        r = lax.rsqrt(jnp.mean(xv * xv, axis=-1, keepdims=True) + EPS)
        xh = xv * r
        dhv = dh_ref[...]
        g = g_ref[...]
        dy = dhv * (1.0 + mod_ref[1:2, :])
        dxh = dy * g
        dx = r * (dxh - xh * jnp.mean(dxh * xh, axis=-1, keepdims=True))
        dx_ref[...] = dres_ref[...] + dx

        @pl.when(i == 0)
        def _():
            dg_ref[...] = jnp.zeros_like(dg_ref)
            dsh_ref[...] = jnp.zeros_like(dsh_ref)
            dsc_ref[...] = jnp.zeros_like(dsc_ref)

        dg_ref[...] += _colsum(dy * xh)
        dsh_ref[...] += _colsum(dhv)
        dsc_ref[...] += _colsum(dhv * xh * g)

    vec = S((1, D), F32)
    return pl.pallas_call(
        body, grid=(n // tm,),
        in_specs=[_rows(tm, D), _rows(tm, D), _rows(tm, D), _full((1, D)), _full((3, D))],
        out_specs=[_rows(tm, D), _full((1, D)), _full((1, D)), _full((1, D))],
        out_shape=[S((n, D), F32), vec, vec, vec],
        compiler_params=_cparams("arbitrary"), name=name)(dh, x, dres, gain, mod)


def _final_loss(x, gain, target, *, name):
    n = x.shape[0]
    tm = _tile(n, 512)

    def body(x_ref, g_ref, t_ref, dx_ref, loss_ref, dg_ref):
        i = pl.program_id(0)
        xv = x_ref[...]
        g = g_ref[...]
        r = lax.rsqrt(jnp.mean(xv * xv, axis=-1, keepdims=True) + EPS)
        xh = xv * r
        err = xh * g - t_ref[...]
        dy = err * (1.0 / D)
        dxh = dy * g
        dx_ref[...] = r * (dxh - xh * jnp.mean(dxh * xh, axis=-1, keepdims=True))

        @pl.when(i == 0)
        def _():
            loss_ref[...] = jnp.zeros_like(loss_ref)
            dg_ref[...] = jnp.zeros_like(dg_ref)

        part = _colsum(jnp.sum(err * err, axis=-1, keepdims=True)) * (0.5 / D)
        loss_ref[...] += jnp.broadcast_to(part, loss_ref.shape)
        dg_ref[...] += _colsum(dy * xh)

    return pl.pallas_call(
        body, grid=(n // tm,),
        in_specs=[_rows(tm, D), _full((1, D)), _rows(tm, D)],
        out_specs=[_rows(tm, D), _full((1, 128)), _full((1, D))],
        out_shape=[S((n, D), F32), S((1, 128), F32), S((1, D), F32)],
        compiler_params=_cparams("arbitrary"), name=name)(x, gain, target)


def _gate_bwd(dx, o, mod, *, name):
    n = dx.shape[0]
    tm = _tile(n, 512)

    def body(dx_ref, o_ref, mod_ref, do_ref, dgt_ref):
        i = pl.program_id(0)
        dxv = dx_ref[...]
        do_ref[...] = (dxv * mod_ref[2:3, :]).astype(do_ref.dtype)

        @pl.when(i == 0)
        def _():
            dgt_ref[...] = jnp.zeros_like(dgt_ref)

        dgt_ref[...] += _colsum(dxv * o_ref[...].astype(F32))

    return pl.pallas_call(
        body, grid=(n // tm,), in_specs=[_rows(tm, D), _rows(tm, D), _full((3, D))],
        out_specs=[_rows(tm, D), _full((1, D))], out_shape=[S((n, D), BF), S((1, D), F32)],
        compiler_params=_cparams("arbitrary"), name=name)(dx, o, mod)


def _conv(buf, w_ref, taps, start, rows, ch):
    acc = None
    for k in range(taps):
        term = buf[pl.ds(start - (taps - 1) + k, rows), 0:ch] * w_ref[k:k + 1, :]
        acc = term if acc is None else acc + term
    return acc


def _conv_t(buf, w_ref, taps, start, rows, ch):
    acc = None
    for k in range(taps):
        term = buf[pl.ds(start + (taps - 1) - k, rows), 0:ch] * w_ref[k:k + 1, :]
        acc = term if acc is None else acc + term
    return acc


def _conv_dw(dw_ref, dy, xbuf, taps, xstart, rows, ch):
    for k in range(taps):
        dw_ref[k:k + 1, :] += _colsum(dy * xbuf[pl.ds(xstart - (taps - 1) + k, rows), 0:ch])


HALO = 16
CONF_HALO = 32
CHUNK = 32
STRIP = 256


def _blocks8(v):
    return [v[8 * i:8 * (i + 1)] for i in range(v.shape[0] // 8)]


def _delay_rows(blocks, s):
    sub = lax.broadcasted_iota(jnp.int32, blocks[0].shape, 0)
    rolled = [pltpu.roll(b, s, 0) for b in blocks]
    return [jnp.where(sub < s, rolled[i - 1], rolled[i]) for i in range(1, len(blocks))]


def _advance_rows(blocks, s):
    sub = lax.broadcasted_iota(jnp.int32, blocks[0].shape, 0)
    rolled = [pltpu.roll(b, 8 - s, 0) for b in blocks]
    return [jnp.where(sub < 8 - s, rolled[i], rolled[i + 1]) for i in range(len(blocks) - 1)]


def _conv3_chunk(tail, xv, wk):
    blocks = [tail] + _blocks8(xv)
    x1 = jnp.concatenate(_delay_rows(blocks, 1), axis=0)
    x2 = jnp.concatenate(_delay_rows(blocks, 2), axis=0)
    return wk[0] * x2 + wk[1] * x1 + wk[2] * xv


def _conv_chunk(tail, xv, wk):
    taps = len(wk)
    blocks = [tail] + _blocks8(xv)
    acc = wk[taps - 1] * xv
    for d in range(1, taps):
        acc = acc + wk[taps - 1 - d] * jnp.concatenate(_delay_rows(blocks, d), axis=0)
    return acc


def _ssd_pre_fwd(proj, w, b, *, name):
    n = proj.shape[0]
    tm = _tile(n, 512)
    cb = P_XBC // XBC_PAD

    def body(prev_ref, cur_ref, w_ref, b_ref, o_ref, c_ref, buf):
        i = pl.program_id(0)
        buf[0:HALO, :] = jnp.where(i == 0, 0.0, prev_ref[:, 0:XBC].astype(F32))
        buf[HALO:HALO + tm, :] = cur_ref[:, 0:XBC].astype(F32)
        c = _conv(buf, w_ref, SSD_K, HALO, tm, XBC) + b_ref[...]
        c_ref[...] = c.astype(c_ref.dtype)
        o_ref[...] = (c * _sigmoid(c)).astype(o_ref.dtype)

    return pl.pallas_call(
        body, grid=(n // tm,),
        in_specs=[_prev_rows(tm, HALO, XBC_PAD, cb), _rows(tm, XBC_PAD, cb), _full((SSD_K, XBC)), _full((1, XBC))],
        out_specs=[_rows(tm, XBC), _rows(tm, XBC)], out_shape=[S((n, XBC), BF), S((n, XBC), BF)],
        scratch_shapes=[pltpu.VMEM((HALO + tm, XBC), F32)],
        compiler_params=_cparams("parallel"), name=name)(proj, proj, w, b)


def _ssd_pre_bwd(proj, cpre, dact, ddt, dproj, w, *, name):
    n = proj.shape[0]
    tm = _tile(n, 512)
    nt = n // tm
    cb = P_XBC // XBC_PAD

    def body(x_ref, cc_ref, cn_ref, dc_ref, dn_ref, ddt_ref, w_ref, dproj_in, o_ref, dw_ref, db_ref, dbuf, acc):
        del dproj_in
        i = pl.program_id(0)
        last = i == nt - 1

        @pl.when(i == 0)
        def _():
            acc[...] = jnp.zeros_like(acc)

        def silu_bwd(cv, dav):
            sg = _sigmoid(cv)
            return dav * (sg * (1.0 + cv * (1.0 - sg)))

        for s in range(XBC // STRIP):
            c = pl.ds(s * STRIP, STRIP)
            wk = [w_ref[k:k + 1, c] for k in range(SSD_K)]

            def step1(j, carry):
                rows = pl.ds(pl.multiple_of(j * CHUNK, CHUNK), CHUNK)
                dbuf[rows, c] = silu_bwd(cc_ref[rows, c].astype(F32), dc_ref[rows, c].astype(F32))
                return carry

            lax.fori_loop(0, tm // CHUNK, step1, 0, unroll=2)
            dbuf[tm:tm + HALO, c] = silu_bwd(cn_ref[:, c].astype(F32), jnp.where(last, 0.0, dn_ref[:, c].astype(F32)))

            def step2(j, carry):
                r0 = pl.multiple_of(j * CHUNK, CHUNK)
                rows = pl.ds(r0, CHUNK)
                win = dbuf[pl.ds(r0, CHUNK + 8), c]
                blocks = _blocks8(win)
                xv = x_ref[rows, c].astype(F32)
                d0 = win[0:CHUNK]
                dx = wk[SSD_K - 1] * d0
                acc[SSD_K - 1, :, c] += d0 * xv
                acc[SSD_K, :, c] += d0
                for adv in range(1, SSD_K):
                    dk = jnp.concatenate(_advance_rows(blocks, adv), axis=0)
                    dx = dx + wk[SSD_K - 1 - adv] * dk
                    acc[SSD_K - 1 - adv, :, c] += dk * xv
                o_ref[rows, c] = dx.astype(o_ref.dtype)
                return carry

            lax.fori_loop(0, tm // CHUNK, step2, 0)

        o_ref[:, XBC:XBC + DT_PAD] = ddt_ref[...]
        o_ref[:, XBC + DT_PAD:XBC_PAD] = jnp.zeros((tm, XBC_PAD - XBC - DT_PAD), o_ref.dtype)

        @pl.when(last)
        def _():
            for k in range(SSD_K):
                dw_ref[k:k + 1, :] = _colsum(acc[k])
            db_ref[...] = _colsum(acc[SSD_K])

    return pl.pallas_call(
        body, grid=(nt,),
        in_specs=[_rows(tm, XBC_PAD, cb), _rows(tm, XBC), _next_rows(tm, HALO, XBC, n),
                  _rows(tm, XBC), _next_rows(tm, HALO, XBC, n), _rows(tm, DT_PAD),
                  _full((SSD_K, XBC)), pl.BlockSpec(memory_space=pl.ANY)],
        out_specs=[_rows(tm, XBC_PAD, cb), _full((SSD_K, XBC)), _full((1, XBC))],
        out_shape=[S(dproj.shape, dproj.dtype), S((SSD_K, XBC), F32), S((1, XBC), F32)],
        scratch_shapes=[pltpu.VMEM((tm + HALO, XBC), F32), pltpu.VMEM((SSD_K + 1, CHUNK, XBC), F32)],
        input_output_aliases={7: 0},
        compiler_params=_cparams("arbitrary"), name=name)(proj, cpre, cpre, dact, dact, ddt, w, dproj)


def _sc_fwd(proj, w, *, name):
    n = proj.shape[0]
    tm = _tile(n, 512)
    cb = P_SC // (3 * SC_W)

    def body(prev_ref, cur_ref, w_ref, o_ref, buf):
        i = pl.program_id(0)
        pv = prev_ref[...].astype(F32)
        cv = cur_ref[...].astype(F32)
        buf[0:HALO, :] = jnp.where(i == 0, 0.0, pv[:, SC_W:2 * SC_W] * pv[:, 2 * SC_W:])
        buf[HALO:HALO + tm, :] = cv[:, SC_W:2 * SC_W] * cv[:, 2 * SC_W:]
        q = _conv(buf, w_ref, SC_K, HALO, tm, SC_W)
        o_ref[...] = (cv[:, 0:SC_W] * q).astype(o_ref.dtype)

    return pl.pallas_call(
        body, grid=(n // tm,),
        in_specs=[_prev_rows(tm, HALO, 3 * SC_W, cb), _rows(tm, 3 * SC_W, cb), _full((SC_K, SC_W))],
        out_specs=_rows(tm, SC_W), out_shape=S((n, SC_W), BF),
        scratch_shapes=[pltpu.VMEM((HALO + tm, SC_W), F32)],
        compiler_params=_cparams("parallel"), name=name)(proj, proj, w)


def _sc_bwd(proj, da, dproj, w, *, name):
    n = proj.shape[0]
    tm = _tile(n, 512)
    nt = n // tm
    cb = P_SC // (3 * SC_W)

    def body(xp_ref, xc_ref, xn_ref, dc_ref, dn_ref, w_ref, dproj_in, o_ref, dw_ref, pbuf, dbuf):
        del dproj_in
        i = pl.program_id(0)
        pv = xp_ref[...].astype(F32)
        cv = xc_ref[...].astype(F32)
        nv = xn_ref[...].astype(F32)
        gb, gc, xv = cv[:, 0:SC_W], cv[:, SC_W:2 * SC_W], cv[:, 2 * SC_W:]
        pbuf[0:HALO, :] = jnp.where(i == 0, 0.0, pv[:, SC_W:2 * SC_W] * pv[:, 2 * SC_W:])
        pbuf[HALO:HALO + tm, :] = gc * xv
        q = _conv(pbuf, w_ref, SC_K, HALO, tm, SC_W)
        dav = dc_ref[...].astype(F32)
        dbuf[0:tm, :] = dav * gb
        dbuf[tm:tm + HALO, :] = jnp.where(i == nt - 1, 0.0, dn_ref[...].astype(F32) * nv[:, 0:SC_W])
        dp = _conv_t(dbuf, w_ref, SC_K, 0, tm, SC_W)
        o_ref[:, 0:SC_W] = (dav * q).astype(o_ref.dtype)
        o_ref[:, SC_W:2 * SC_W] = (dp * xv).astype(o_ref.dtype)
        o_ref[:, 2 * SC_W:] = (dp * gc).astype(o_ref.dtype)

        @pl.when(i == 0)
        def _():
            dw_ref[...] = jnp.zeros_like(dw_ref)

        _conv_dw(dw_ref, dbuf[0:tm, :], pbuf, SC_K, HALO, tm, SC_W)

    return pl.pallas_call(
        body, grid=(nt,),
        in_specs=[_prev_rows(tm, HALO, 3 * SC_W, cb), _rows(tm, 3 * SC_W, cb), _next_rows(tm, HALO, 3 * SC_W, n, cb),
                  _rows(tm, SC_W), _next_rows(tm, HALO, SC_W, n), _full((SC_K, SC_W)),
                  pl.BlockSpec(memory_space=pl.ANY)],
        out_specs=[_rows(tm, 3 * SC_W, cb), _full((SC_K, SC_W))],
        out_shape=[S(dproj.shape, dproj.dtype), S((SC_K, SC_W), F32)],
        scratch_shapes=[pltpu.VMEM((HALO + tm, SC_W), F32), pltpu.VMEM((tm + HALO, SC_W), F32)],
        input_output_aliases={6: 0},
        compiler_params=_cparams("arbitrary"), name=name)(proj, proj, proj, da, da, w, dproj)


def _conf_fwd(proj, w, b, ln_g, ln_b, *, name):
    n = proj.shape[0]
    tm = _tile(n, 512)
    cb = P_CONF // (2 * CONF_W)
    h = CONF_HALO

    def body(prev_ref, cur_ref, w_ref, b_ref, g_ref, be_ref, a_ref, uc_ref, buf):
        i = pl.program_id(0)
        pv = prev_ref[...].astype(F32)
        cv = cur_ref[...].astype(F32)
        buf[0:h, :] = jnp.where(i == 0, 0.0, pv[:, 0:CONF_W] * _sigmoid(pv[:, CONF_W:]))
        buf[h:h + tm, :] = cv[:, 0:CONF_W] * _sigmoid(cv[:, CONF_W:])
        uc = _conv(buf, w_ref, CONF_K, h, tm, CONF_W) + b_ref[...]
        uc_ref[...] = uc.astype(uc_ref.dtype)
        mu = jnp.mean(uc, axis=-1, keepdims=True)
        xc = uc - mu
        v = xc * lax.rsqrt(jnp.mean(xc * xc, axis=-1, keepdims=True) + EPS) * g_ref[...] + be_ref[...]
        a_ref[...] = (v * _sigmoid(v)).astype(a_ref.dtype)

    vec = _full((1, CONF_W))
    return pl.pallas_call(
        body, grid=(n // tm,),
        in_specs=[_prev_rows(tm, h, 2 * CONF_W, cb), _rows(tm, 2 * CONF_W, cb), _full((CONF_K, CONF_W)), vec, vec, vec],
        out_specs=[_rows(tm, CONF_W), _rows(tm, CONF_W)],
        out_shape=[S((n, CONF_W), BF), S((n, CONF_W), BF)],
        scratch_shapes=[pltpu.VMEM((h + tm, CONF_W), F32)],
        compiler_params=_cparams("parallel"), name=name)(proj, proj, w, b, ln_g, ln_b)


def _conf_bwd(proj, uc, da, dproj, w, ln_g, ln_b, *, name, comm=None):
    n = proj.shape[0]
    tm = _tile(n, 512)
    nt = n // tm
    cb = P_CONF // (2 * CONF_W)
    h = CONF_HALO

    def body(xp_ref, xc_ref, ucc_ref, ucn_ref, dac_ref, dan_ref, w_ref, g_ref, be_ref, dproj_in,
             o_ref, dw_ref, db_ref, dg_ref, dbe_ref, ubuf, dbuf):
        del dproj_in
        i = pl.program_id(0)
        pv = xp_ref[...].astype(F32)
        cv = xc_ref[...].astype(F32)
        val, gt = cv[:, 0:CONF_W], cv[:, CONF_W:]
        sg = _sigmoid(gt)
        ubuf[0:h, :] = jnp.where(i == 0, 0.0, pv[:, 0:CONF_W] * _sigmoid(pv[:, CONF_W:]))
        ubuf[h:h + tm, :] = val * sg

        def ln_silu_bwd(ucv, dav):
            mu = jnp.mean(ucv, axis=-1, keepdims=True)
            xc = ucv - mu
            r = lax.rsqrt(jnp.mean(xc * xc, axis=-1, keepdims=True) + EPS)
            xh = xc * r
            v = xh * g_ref[...] + be_ref[...]
            s = _sigmoid(v)
            dv = dav * (s * (1.0 + v * (1.0 - s)))
            dxh = dv * g_ref[...]
            duc = r * (dxh - jnp.mean(dxh, axis=-1, keepdims=True) - xh * jnp.mean(dxh * xh, axis=-1, keepdims=True))
            return duc, dv, xh

        duc, dv, xh = ln_silu_bwd(ucc_ref[...].astype(F32), dac_ref[...].astype(F32))
        dbuf[0:tm, :] = duc
        ducn, _, _ = ln_silu_bwd(ucn_ref[...].astype(F32), dan_ref[...].astype(F32))
        dbuf[tm:tm + h, :] = jnp.where(i == nt - 1, 0.0, ducn)
        du = _conv_t(dbuf, w_ref, CONF_K, 0, tm, CONF_W)
        o_ref[:, 0:CONF_W] = (du * sg).astype(o_ref.dtype)
        o_ref[:, CONF_W:] = (du * val * sg * (1.0 - sg)).astype(o_ref.dtype)

        @pl.when(i == 0)
        def _():
            dw_ref[...] = jnp.zeros_like(dw_ref)
            db_ref[...] = jnp.zeros_like(db_ref)
            dg_ref[...] = jnp.zeros_like(dg_ref)
            dbe_ref[...] = jnp.zeros_like(dbe_ref)

        dg_ref[...] += _colsum(dv * xh)
        dbe_ref[...] += _colsum(dv)
        db_ref[...] += _colsum(duc)
        _conv_dw(dw_ref, duc, ubuf, CONF_K, h, tm, CONF_W)

    vec = _full((1, CONF_W))
    vshape = S((1, CONF_W), F32)
    return _call_with_comm(
        body, nt,
        in_specs=[_prev_rows(tm, h, 2 * CONF_W, cb), _rows(tm, 2 * CONF_W, cb),
                  _rows(tm, CONF_W), _next_rows(tm, h, CONF_W, n), _rows(tm, CONF_W), _next_rows(tm, h, CONF_W, n),
                  _full((CONF_K, CONF_W)), vec, vec, pl.BlockSpec(memory_space=pl.ANY)],
        out_specs=[_rows(tm, 2 * CONF_W, cb), _full((CONF_K, CONF_W)), vec, vec, vec],
        out_shape=[S(dproj.shape, dproj.dtype), S((CONF_K, CONF_W), F32), vshape, vshape, vshape],
        scratch_shapes=[pltpu.VMEM((h + tm, CONF_W), F32), pltpu.VMEM((tm + h, CONF_W), F32)],
        args=(proj, proj, uc, uc, da, da, w, ln_g, ln_b, dproj), comm=comm, name=name, aliases={9: 0})


def _ffn_act_fwd(up, w, b, *, name, comm=None):
    n = up.shape[0]
    tm = _tile(n, 512)
    c2 = 2 * DFF

    def body(prev_ref, cur_ref, w_ref, b_ref, o_ref, u_ref):
        first = pl.program_id(0) == 0
        for s in range(DFF // STRIP):
            cols = (pl.ds(s * STRIP, STRIP), pl.ds(DFF + s * STRIP, STRIP))
            wk = [[w_ref[k:k + 1, c] for k in range(FFN_K)] for c in cols]
            bk = [b_ref[:, c] for c in cols]
            tails = tuple(jnp.where(first, 0.0, prev_ref[:, c].astype(F32)[HALO - 8:HALO]) for c in cols)

            def step(j, tails):
                r0 = pl.multiple_of(j * CHUNK, CHUNK)
                us, new_tails = [], []
                for h in range(2):
                    xv = cur_ref[pl.ds(r0, CHUNK), cols[h]].astype(F32)
                    us.append(_conv3_chunk(tails[h], xv, wk[h]) + bk[h])
                    u_ref[pl.ds(r0, CHUNK), cols[h]] = us[h].astype(u_ref.dtype)
                    new_tails.append(xv[CHUNK - 8:CHUNK])
                o_ref[pl.ds(r0, CHUNK), cols[0]] = (us[0] * _sigmoid(us[0]) * us[1]).astype(o_ref.dtype)
                return tuple(new_tails)

            lax.fori_loop(0, tm // CHUNK, step, tails, unroll=2)

    return _call_with_comm(
        body, n // tm,
        in_specs=[_prev_rows(tm, HALO, c2), _rows(tm, c2), _full((FFN_K, c2)), _full((1, c2))],
        out_specs=[_rows(tm, DFF), _rows(tm, c2)], out_shape=[S((n, DFF), BF), S((n, c2), BF)],
        scratch_shapes=[], args=(up, up, w, b), comm=comm, name=name)


def _ffn_act_bwd(up, u, dact, w, *, name, comm=None):
    n = up.shape[0]
    tm = _tile(n, 512)
    nt = n // tm
    c2 = 2 * DFF

    def body(x_ref, uc_ref, un_ref, dc_ref, dn_ref, w_ref, o_ref, dw_ref, db_ref, dbuf, acc):
        i = pl.program_id(0)
        last = i == nt - 1

        @pl.when(i == 0)
        def _():
            acc[...] = jnp.zeros_like(acc)

        def swiglu_bwd(gate, val, dav):
            sg = _sigmoid(gate)
            return dav * val * (sg * (1.0 + gate * (1.0 - sg))), dav * gate * sg

        for s in range(DFF // STRIP):
            cols = (pl.ds(s * STRIP, STRIP), pl.ds(DFF + s * STRIP, STRIP))
            wk = [[w_ref[k:k + 1, c] for k in range(FFN_K)] for c in cols]

            def step1(j, carry):
                r0 = pl.multiple_of(j * CHUNK, CHUNK)
                rows = pl.ds(r0, CHUNK)
                dus = swiglu_bwd(uc_ref[rows, cols[0]].astype(F32), uc_ref[rows, cols[1]].astype(F32),
                                 dc_ref[rows, cols[0]].astype(F32))
                for h in range(2):
                    dbuf[rows, cols[h]] = dus[h]
                return carry

            lax.fori_loop(0, tm // CHUNK, step1, 0, unroll=2)
            dus = swiglu_bwd(un_ref[:, cols[0]].astype(F32), un_ref[:, cols[1]].astype(F32),
                             jnp.where(last, 0.0, dn_ref[:, cols[0]].astype(F32)))
            for h in range(2):
                dbuf[tm:tm + HALO, cols[h]] = dus[h]

            def step2(j, carry):
                r0 = pl.multiple_of(j * CHUNK, CHUNK)
                rows = pl.ds(r0, CHUNK)
                for h in range(2):
                    win = dbuf[pl.ds(r0, CHUNK + 8), cols[h]]
                    blocks = _blocks8(win)
                    d0 = win[0:CHUNK]
                    d1 = jnp.concatenate(_advance_rows(blocks, 1), axis=0)
                    d2 = jnp.concatenate(_advance_rows(blocks, 2), axis=0)
                    o_ref[rows, cols[h]] = (wk[h][2] * d0 + wk[h][1] * d1 + wk[h][0] * d2).astype(o_ref.dtype)
                    xv = x_ref[rows, cols[h]].astype(F32)
                    acc[2, :, cols[h]] += d0 * xv
                    acc[1, :, cols[h]] += d1 * xv
                    acc[0, :, cols[h]] += d2 * xv
                    acc[FFN_K, :, cols[h]] += d0
                return carry

            lax.fori_loop(0, tm // CHUNK, step2, 0)

        @pl.when(last)
        def _():
            for k in range(FFN_K):
                dw_ref[k:k + 1, :] = _colsum(acc[k])
            db_ref[...] = _colsum(acc[FFN_K])

    return _call_with_comm(
        body, nt,
        in_specs=[_rows(tm, c2), _rows(tm, c2), _next_rows(tm, HALO, c2, n),
                  _rows(tm, DFF), _next_rows(tm, HALO, DFF, n), _full((FFN_K, c2))],
        out_specs=[_rows(tm, c2), _full((FFN_K, c2)), _full((1, c2))],
        out_shape=[S((n, c2), BF), S((FFN_K, c2), F32), S((1, c2), F32)],
        scratch_shapes=[pltpu.VMEM((tm + HALO, c2), F32), pltpu.VMEM((FFN_K + 1, CHUNK, c2), F32)],
        args=(up, u, u, dact, dact, w), comm=comm, name=name)


def _head_consts():
    lane = jnp.arange(INNER) // HEAD_DIM
    rep = (jnp.arange(128)[:, None] == lane[None, :]).astype(BF)
    return rep, rep.T


def _split_dot(v, m):
    hi = v.astype(BF)
    lo = (v - hi.astype(F32)).astype(BF)
    return _dot(hi, m) + _dot(lo, m)


def _chunk_decay_terms(dt_raw, dtb, alog, rep):
    row = lax.broadcasted_iota(jnp.int32, (Q, Q), 0)
    col = lax.broadcasted_iota(jnp.int32, (Q, Q), 1)
    lower = row >= col
    upper = col >= row
    dt = _softplus(dt_raw + dtb)
    a = -jnp.exp(alog)
    adt = dt * a
    acum = lax.dot_general(lower.astype(F32), adt, NN, precision=HIGHEST, preferred_element_type=F32)
    acum_t = lax.dot_general(adt, upper.astype(F32), TN, precision=HIGHEST, preferred_element_type=F32)
    alast = acum[Q - 1:Q, :]
    e = jnp.exp(acum)
    f = jnp.exp(alast - acum)
    ex = _split_dot(jnp.concatenate([dt, e, f, jnp.broadcast_to(jnp.exp(alast), (8, 128))], axis=0), rep)
    return dict(lower=lower, upper=upper, dt=dt, a=a, acum=acum, acum_t=acum_t, alast=alast,
                dt_x=ex[0:Q], e_x=ex[Q:2 * Q], f_x=ex[2 * Q:3 * Q], cd_x=ex[3 * Q:3 * Q + 1])


def _block_diag2(v, lo):
    return jnp.concatenate([jnp.where(lo, v, 0.0), jnp.where(lo, 0.0, v)], axis=0).astype(BF)


def _ssd_fwd(xbc_act, proj, dt_bias, a_log, d_x, norm_g, *, name, comm=None):
    n = xbc_act.shape[0]
    nc = n // Q
    rep, _ = _head_consts()

    def body(xs_ref, bc_ref, dt_ref, z_ref, dtb_ref, alog_ref, dx_ref, ng_ref, rep_ref, y_ref, yn_ref, hp_ref,
             h_scr, y_scr):
        i = pl.program_id(0)

        @pl.when(i == 0)
        def _():
            h_scr[...] = jnp.zeros_like(h_scr)

        hp_ref[...] = h_scr[...]
        t = _chunk_decay_terms(dt_ref[...].astype(F32), dtb_ref[...], alog_ref[...], rep_ref[...])
        xs = xs_ref[...].astype(F32)
        xt = xs * t["dt_x"]
        lo = lax.broadcasted_iota(jnp.int32, (Q, 128), 1) < HEAD_DIM
        gw = INNER // GROUPS
        for g in range(GROUPS):
            bm = bc_ref[:, g * NSTATE:(g + 1) * NSTATE]
            cm = bc_ref[:, GROUPS * NSTATE + g * NSTATE:GROUPS * NSTATE + (g + 1) * NSTATE]
            cb = _dot(cm, bm, NT)
            hg = h_scr[:, g * gw:(g + 1) * gw]
            yoff = _dot(cm, hg.astype(BF))
            for jj in range(gw // 128):
                p = g * (gw // 128) + jj
                sl = slice(p * 128, (p + 1) * 128)
                ws = []
                for hd in (2 * p, 2 * p + 1):
                    seg = t["acum"][:, hd:hd + 1] - t["acum_t"][hd:hd + 1, :]
                    ws.append((cb * jnp.exp(jnp.where(t["lower"], seg, -jnp.inf))).astype(BF))
                ydiag = _dot(jnp.concatenate(ws, axis=1), _block_diag2(xt[:, sl], lo))
                y_scr[:, sl] = ydiag + yoff[:, jj * 128:(jj + 1) * 128] * t["e_x"][:, sl] + dx_ref[:, sl] * xs[:, sl]
            xf = (xt[:, g * gw:(g + 1) * gw] * t["f_x"][:, g * gw:(g + 1) * gw]).astype(BF)
            h_scr[:, g * gw:(g + 1) * gw] = hg * t["cd_x"][:, g * gw:(g + 1) * gw] + _dot(bm, xf, TN)
        y = y_scr[...]
        y_ref[...] = y.astype(y_ref.dtype)
        z = z_ref[...].astype(F32)
        v = y * z * _sigmoid(z)
        for g in range(GROUPS):
            vg = v[:, g * gw:(g + 1) * gw]
            r = lax.rsqrt(jnp.mean(vg * vg, axis=-1, keepdims=True) + EPS)
            yn_ref[:, g * gw:(g + 1) * gw] = (vg * r * ng_ref[:, g * gw:(g + 1) * gw]).astype(yn_ref.dtype)

    vec = _full((1, INNER))
    hv = _full((1, 128))
    return _call_with_comm(
        body, nc,
        in_specs=[_rows(Q, INNER, 0), _rows(Q, 2 * GROUPS * NSTATE, INNER // (2 * GROUPS * NSTATE)),
                  _rows(Q, DT_PAD, P_DT // DT_PAD), _rows(Q, INNER, P_Z // INNER),
                  hv, hv, vec, vec, _full((128, INNER))],
        out_specs=[_rows(Q, INNER), _rows(Q, INNER), pl.BlockSpec((None, NSTATE, INNER), lambda i: (i, 0, 0))],
        out_shape=[S((n, INNER), BF), S((n, INNER), BF), S((nc, NSTATE, INNER), F32)],
        scratch_shapes=[pltpu.VMEM((NSTATE, INNER), F32), pltpu.VMEM((Q, INNER), F32)],
        args=(xbc_act, xbc_act, proj, proj, dt_bias, a_log, d_x, norm_g, rep), comm=comm, name=name)


def _ssd_bwd(xbc_act, proj, y, dyn, hprev, dproj, dt_bias, a_log, d_x, norm_g, *, name):
    n = xbc_act.shape[0]
    nc = n // Q
    rep, sel = _head_consts()
    gw = INNER // GROUPS

    def rev(w, cb=0):
        return pl.BlockSpec((Q, w), lambda i: (nc - 1 - i, cb))

    def body(xs_ref, bc_ref, dt_ref, z_ref, y_ref, dyn_ref, hp_ref, dtb_ref, alog_ref, dx_ref, ng_ref, rep_ref,
             sel_ref, dproj_in, dz_ref, ddt_ref, dxbc_ref, dng_ref, ddtb_ref, dalog_ref, dd_ref,
             dh_scr, dxt_scr, st_scr, off_scr, rs_scr, cs_scr, dng_acc, ddtb_acc, da_acc, dd_acc):
        del dproj_in
        i = pl.program_id(0)

        @pl.when(i == 0)
        def _():
            for r in (dh_scr, dng_acc, ddtb_acc, da_acc, dd_acc):
                r[...] = jnp.zeros_like(r)

        y = y_ref[...].astype(F32)
        z = z_ref[...].astype(F32)
        sz = _sigmoid(z)
        silu = z * sz
        v = y * silu
        dyn = dyn_ref[...].astype(F32)
        dvs = []
        for g in range(GROUPS):
            gs = slice(g * gw, (g + 1) * gw)
            vg = v[:, gs]
            r = lax.rsqrt(jnp.mean(vg * vg, axis=-1, keepdims=True) + EPS)
            vn = vg * r
            dvn = dyn[:, gs] * ng_ref[:, gs]
            dng_acc[:, gs] += _colsum(dyn[:, gs] * vn)
            dvs.append(r * (dvn - vn * jnp.mean(dvn * vn, axis=-1, keepdims=True)))
        dv = jnp.concatenate(dvs, axis=1)
        dy = dv * silu
        dz_ref[...] = (dv * y * (sz * (1.0 + z * (1.0 - sz)))).astype(dz_ref.dtype)

        dt_raw = dt_ref[...].astype(F32)
        t = _chunk_decay_terms(dt_raw, dtb_ref[...], alog_ref[...], rep_ref[...])
        xs = xs_ref[...].astype(F32)
        dsk = dx_ref[...]
        dd_acc[...] += _colsum(dy * xs)
        xt = xs * t["dt_x"]
        dye = dy * t["e_x"]
        xtf = xt * t["f_x"]
        hp = hp_ref[...]
        dh = dh_scr[...]
        lo = lax.broadcasted_iota(jnp.int32, (Q, 128), 1) < HEAD_DIM
        rs_scr[...] = jnp.zeros_like(rs_scr)
        cs_scr[...] = jnp.zeros_like(cs_scr)
        for g in range(GROUPS):
            gs = slice(g * gw, (g + 1) * gw)
            bm = bc_ref[:, g * NSTATE:(g + 1) * NSTATE]
            cm = bc_ref[:, GROUPS * NSTATE + g * NSTATE:GROUPS * NSTATE + (g + 1) * NSTATE]
            cbt = _dot(bm, cm, NT)
            dhg = dh[:, gs].astype(BF)
            hpg = hp[:, gs].astype(BF)
            dxt_state = _dot(bm, dhg) * t["f_x"][:, gs]
            st_scr[:, gs] = dxt_state
            dye_g = dye[:, gs]
            off_scr[:, gs] = dye_g * _dot(cm, hpg)
            dye_b = dye_g.astype(BF)
            db = _dot(xtf[:, gs].astype(BF), dhg, NT)
            dc = _dot(dye_b, hpg, NT)
            dh_scr[:, gs] = t["cd_x"][:, gs] * dh[:, gs] + _dot(cm, dye_b, TN)
            dcbt = jnp.zeros((Q, Q), F32)
            for jj in range(gw // 128):
                p = g * (gw // 128) + jj
                sl = slice(p * 128, (p + 1) * 128)
                lts, wfs = [], []
                for hd in (2 * p, 2 * p + 1):
                    seg_t = t["acum_t"][hd:hd + 1, :] - t["acum"][:, hd:hd + 1]
                    lt = jnp.exp(jnp.where(t["upper"], seg_t, -jnp.inf))
                    lts.append(lt)
                    wfs.append(cbt * lt)
                dyp = dy[:, sl]
                dxt_diag = _dot(jnp.concatenate([w.astype(BF) for w in wfs], axis=1), _block_diag2(dyp, lo))
                dwt2 = _dot(_block_diag2(xt[:, sl], lo), dyp.astype(BF), NT)
                for k, hd in enumerate((2 * p, 2 * p + 1)):
                    dwt = dwt2[k * Q:(k + 1) * Q]
                    dcbt = dcbt + dwt * lts[k]
                    mt = dwt * wfs[k]
                    rs_scr[hd:hd + 1, :] = _colsum(mt)
                    cs_scr[:, hd:hd + 1] = jnp.sum(mt, axis=1, keepdims=True)
                dxt_scr[:, sl] = dxt_diag + dxt_state[:, jj * 128:(jj + 1) * 128]
            dcbt_b = dcbt.astype(BF)
            db = db + _dot(dcbt_b, cm)
            dc = dc + _dot(dcbt_b, bm, TN)
            dxbc_ref[:, INNER + g * NSTATE:INNER + (g + 1) * NSTATE] = db.astype(dxbc_ref.dtype)
            dxbc_ref[:, INNER + (GROUPS + g) * NSTATE:INNER + (GROUPS + g + 1) * NSTATE] = dc.astype(dxbc_ref.dtype)
        dxt = dxt_scr[...]
        dst = st_scr[...]
        sel_m = sel_ref[...]
        sums = _split_dot(jnp.concatenate([off_scr[...], xs * dst, xs * dxt], axis=0), sel_m)
        r1_off, r3_state, r3 = sums[0:Q], sums[Q:2 * Q], sums[2 * Q:3 * Q]
        t1 = _colsum(xt * dst)
        t2 = _colsum(dh * hp)
        tails = _split_dot(jnp.concatenate([jnp.broadcast_to(t1, (8, INNER)), jnp.broadcast_to(t2, (8, INNER))], axis=0),
                           sel_m)
        extra = tails[0:1] + jnp.exp(t["alast"]) * tails[8:9]
        last_row = lax.broadcasted_iota(jnp.int32, (Q, 128), 0) == Q - 1
        da_cum = (rs_scr[...].T - cs_scr[...]) + r1_off - t["dt"] * r3_state + jnp.where(last_row, extra, 0.0)
        dadt = lax.dot_general(t["upper"].astype(F32), da_cum, NN, precision=HIGHEST, preferred_element_type=F32)
        ddt = r3 + t["a"] * dadt
        da_acc[...] += _colsum(dadt * t["dt"])
        real = lax.broadcasted_iota(jnp.int32, (Q, 128), 1) < HEADS
        ddraw = jnp.where(real, ddt * _sigmoid(dt_raw + dtb_ref[...]), 0.0)
        ddt_ref[...] = ddraw.astype(ddt_ref.dtype)
        ddtb_acc[...] += _colsum(ddraw)
        dxbc_ref[:, 0:INNER] = (dy * dsk + dxt * t["dt_x"]).astype(dxbc_ref.dtype)

        @pl.when(i == nc - 1)
        def _():
            dng_ref[...] = dng_acc[...]
            ddtb_ref[...] = ddtb_acc[...]
            dalog_ref[...] = da_acc[...] * t["a"]
            dd_ref[...] = _split_dot(jnp.broadcast_to(dd_acc[...], (8, INNER)), sel_m)[0:1]

    vec = _full((1, INNER))
    hv = _full((1, 128))
    return pl.pallas_call(
        body, grid=(nc,),
        in_specs=[rev(INNER, 0), rev(2 * GROUPS * NSTATE, INNER // (2 * GROUPS * NSTATE)),
                  rev(DT_PAD, P_DT // DT_PAD), rev(INNER, P_Z // INNER), rev(INNER), rev(INNER),
                  pl.BlockSpec((None, NSTATE, INNER), lambda i: (nc - 1 - i, 0, 0)),
                  hv, hv, vec, vec, _full((128, INNER)), _full((INNER, 128)), pl.BlockSpec(memory_space=pl.ANY)],
        out_specs=[rev(INNER, P_Z // INNER), rev(DT_PAD), rev(XBC), vec, hv, hv, hv],
        out_shape=[S(dproj.shape, dproj.dtype), S((n, DT_PAD), BF), S((n, XBC), BF),
                   S((1, INNER), F32), S((1, 128), F32), S((1, 128), F32), S((1, 128), F32)],
        scratch_shapes=[pltpu.VMEM((NSTATE, INNER), F32), pltpu.VMEM((Q, INNER), F32), pltpu.VMEM((Q, INNER), F32),
                        pltpu.VMEM((Q, INNER), F32), pltpu.VMEM((128, Q), F32), pltpu.VMEM((Q, 128), F32),
                        pltpu.VMEM((1, INNER), F32), pltpu.VMEM((1, 128), F32), pltpu.VMEM((1, 128), F32),
                        pltpu.VMEM((1, INNER), F32)],
        input_output_aliases={13: 0},
        compiler_params=_cparams("arbitrary"), name=name)(
            xbc_act, xbc_act, proj, proj, y, dyn, hprev, dt_bias, a_log, d_x, norm_g, rep, sel, dproj)


def _mixer_out_fwd(yn, a_conf, a_sc, proj, b_gate, w_ssd, w_conf, w_sc, w_o, x, mod, *, name):
    n = x.shape[0]
    tm = _tile(n, 256)

    def body(yn_ref, ac_ref, as_ref, gt_ref, bg_ref, wa_ref, wb_ref, wc_ref, wo_ref, x_ref, mod_ref,
             ya_ref, yb_ref, yc_ref, mg_ref, mix_ref, xn_ref):
        ya = _dot(yn_ref[...], wa_ref[...])
        yb = _dot(ac_ref[...], wb_ref[...])
        yc = _dot(as_ref[...], wc_ref[...])
        ya_ref[...] = ya.astype(ya_ref.dtype)
        yb_ref[...] = yb.astype(yb_ref.dtype)
        yc_ref[...] = yc.astype(yc_ref.dtype)
        g = _sigmoid(gt_ref[...].astype(F32) + bg_ref[...])
        merged = (g[:, 0:D] * ya + g[:, D:2 * D] * yb + g[:, 2 * D:] * yc).astype(mg_ref.dtype)
        mg_ref[...] = merged
        mix = _dot(merged, wo_ref[...])
        mix_ref[...] = mix.astype(mix_ref.dtype)
        xn_ref[...] = x_ref[...] + mod_ref[2:3, :] * mix

    act = S((n, D), BF)
    return pl.pallas_call(
        body, grid=(n // tm,),
        in_specs=[_rows(tm, INNER), _rows(tm, CONF_W), _rows(tm, SC_W), _rows(tm, 3 * D, P_GATES // (3 * D)),
                  _full((1, 3 * D)), _full((INNER, D)), _full((CONF_W, D)), _full((SC_W, D)), _full((D, D)),
                  _rows(tm, D), _full((3, D))],
        out_specs=[_rows(tm, D)] * 6,
        out_shape=[act, act, act, act, act, S((n, D), F32)],
        compiler_params=_cparams("parallel"), name=name)(yn, a_conf, a_sc, proj, b_gate, w_ssd, w_conf, w_sc, w_o, x, mod)


def _mixer_out_bwd(dx, mix, ya, yb, yc, proj, b_gate, w_ssd, w_conf, w_sc, w_o, mod, *, name):
    n = dx.shape[0]
    tm = _tile(n, 256)

    def body(dx_ref, mix_ref, ya_ref, yb_ref, yc_ref, gt_ref, bg_ref, wa_ref, wb_ref, wc_ref, wo_ref, mod_ref,
             do_ref, dya_ref, dyb_ref, dyc_ref, dgt_ref, dyn_ref, dac_ref, das_ref, dgm_ref, dbg_ref):
        i = pl.program_id(0)
        dxv = dx_ref[...]
        do = (dxv * mod_ref[2:3, :]).astype(BF)
        do_ref[...] = do
        dm = _dot(do, wo_ref[...], NT)
        g = _sigmoid(gt_ref[...].astype(F32) + bg_ref[...])
        @pl.when(i == 0)
        def _():
            dgm_ref[...] = jnp.zeros_like(dgm_ref)
            dbg_ref[...] = jnp.zeros_like(dbg_ref)

        dys = []
        for j, (y_ref, o_ref) in enumerate(((ya_ref, dya_ref), (yb_ref, dyb_ref), (yc_ref, dyc_ref))):
            gj = g[:, j * D:(j + 1) * D]
            dyj = (dm * gj).astype(BF)
            o_ref[...] = dyj
            dys.append(dyj)
            dgpre = dm * y_ref[...].astype(F32) * gj * (1.0 - gj)
            dgt_ref[:, j * D:(j + 1) * D] = dgpre.astype(dgt_ref.dtype)
            dbg_ref[:, j * D:(j + 1) * D] += _colsum(dgpre)
        dyn_ref[...] = _dot(dys[0], wa_ref[...], NT).astype(dyn_ref.dtype)
        dac_ref[...] = _dot(dys[1], wb_ref[...], NT).astype(dac_ref.dtype)
        das_ref[...] = _dot(dys[2], wc_ref[...], NT).astype(das_ref.dtype)
        dgm_ref[...] += _colsum(dxv * mix_ref[...].astype(F32))

    act = S((n, D), BF)
    return pl.pallas_call(
        body, grid=(n // tm,),
        in_specs=[_rows(tm, D)] * 5 + [_rows(tm, 3 * D, P_GATES // (3 * D)), _full((1, 3 * D)), _full((INNER, D)),
                                       _full((CONF_W, D)), _full((SC_W, D)), _full((D, D)), _full((3, D))],
        out_specs=[_rows(tm, D)] * 4 + [_rows(tm, 3 * D, P_GATES // (3 * D)), _rows(tm, INNER), _rows(tm, CONF_W),
                                        _rows(tm, SC_W), _full((1, D)), _full((1, 3 * D))],
        out_shape=[act, act, act, act, S((n, PW), BF), S((n, INNER), BF), S((n, CONF_W), BF), S((n, SC_W), BF),
                   S((1, D), F32), S((1, 3 * D), F32)],
        compiler_params=_cparams("arbitrary"), name=name)(dx, mix, ya, yb, yc, proj, b_gate, w_ssd, w_conf, w_sc, w_o, mod)


def _pad_w_in(w):
    zeros = jnp.zeros((w.shape[0], XBC_PAD - XBC - (R_CONF - R_DT)), w.dtype)
    return jnp.concatenate([w[:, R_GATES:], w[:, R_SC:R_GATES], w[:, R_XBC:R_DT], w[:, R_DT:R_CONF], zeros,
                            w[:, R_Z:R_XBC], w[:, R_CONF:R_SC]], axis=1)


def _unpad_w_in(wp):
    return jnp.concatenate([wp[:, P_Z:P_Z + INNER], wp[:, P_XBC:P_XBC + XBC], wp[:, P_DT:P_DT + HEADS],
                            wp[:, P_CONF:P_CONF + 2 * CONF_W], wp[:, P_SC:P_SC + 3 * SC_W], wp[:, P_GATES:P_GATES + 3 * D]],
                           axis=1)


W_IN_SHARD = N_IN // N_CHIPS
W_IN_SEGMENTS = ((R_Z, R_XBC, P_Z), (R_XBC, R_DT, P_XBC), (R_DT, R_CONF, P_DT), (R_CONF, R_SC, P_CONF),
                 (R_SC, R_GATES, P_SC), (R_GATES, N_IN, P_GATES))


def _pad_w_in_chips(w4):
    parts = []
    for lo, hi, dst in sorted(W_IN_SEGMENTS, key=lambda sgm: sgm[2]):
        for k in range(N_CHIPS):
            a, b = max(lo, k * W_IN_SHARD), min(hi, (k + 1) * W_IN_SHARD)
            if a < b:
                parts.append((dst + a - lo, w4[k][:, a - k * W_IN_SHARD:b - k * W_IN_SHARD]))
    out, pos = [], 0
    for start, piece in parts:
        if start > pos:
            out.append(jnp.zeros((w4.shape[1], start - pos), w4.dtype))
        out.append(piece)
        pos = start + piece.shape[1]
    if pos < PW:
        out.append(jnp.zeros((w4.shape[1], PW - pos), w4.dtype))
    return jnp.concatenate(out, axis=1)


def _unpad_w_in_chips(wp):
    blocks = []
    for k in range(N_CHIPS):
        pieces = []
        for lo, hi, dst in W_IN_SEGMENTS:
            a, b = max(lo, k * W_IN_SHARD), min(hi, (k + 1) * W_IN_SHARD)
            if a < b:
                pieces.append(wp[:, dst + a - lo:dst + b - lo])
        blocks.append(jnp.concatenate(pieces, axis=1))
    return jnp.stack(blocks)


def _row(v):
    return v.reshape(1, -1)


def _head_row(v):
    return jnp.pad(v, (0, 128 - HEADS)).reshape(1, 128)


def _layer_fwd(x, p, mod_mix, mod_ffn, tag, comm=None, comm_ssd=None, late_params=None, comm_up=None):
    sv = {"x0": x}
    h = _modnorm_fwd(x, _row(p["norm_mix_g"]), mod_mix, name=f"modnorm_mix_fwd{tag}")
    proj = _mm_nn(h, p["w_in_pad"], out_dtype=BF, tm=1024, tn=2048, name=f"proj_fwd{tag}")
    xbc_act, cpre = _ssd_pre_fwd(proj, p["ssd_conv_w"], _row(p["ssd_conv_b"]), name=f"ssd_pre_fwd{tag}")
    d_x = _row(jnp.repeat(p["ssd_d"], HEAD_DIM))
    (y, yn, hprev), ssd_comm_out = _ssd_fwd(xbc_act, proj, _head_row(p["ssd_dt_bias"]), _head_row(p["ssd_a_log"]), d_x,
                                            _row(p["ssd_norm_g"]), name=f"ssd_fwd{tag}", comm=comm_ssd)
    if late_params is not None:
        p.update(late_params(ssd_comm_out))
    a_conf, uc = _conf_fwd(proj, p["conf_conv_w"], _row(p["conf_conv_b"]), _row(p["conf_ln_g"]), _row(p["conf_ln_b"]),
                           name=f"conf_fwd{tag}")
    a_sc = _sc_fwd(proj, p["sc_conv_w"], name=f"sc_fwd{tag}")
    ya, yb, yc, merged, mix, x1 = _mixer_out_fwd(yn, a_conf, a_sc, proj, _row(p["b_gate"]), p["w_ssd_out"],
                                                 p["w_conf_out"], p["w_sc_out"], p["w_o"], x, mod_mix,
                                                 name=f"mixer_out_fwd{tag}")
    h2 = _modnorm_fwd(x1, _row(p["norm_ffn_g"]), mod_ffn, name=f"modnorm_ffn_fwd{tag}")
    up = _mm_nn_chips(h2, p["w_up4"], out_dtype=BF, tm=1024, name=f"up_fwd{tag}", comm=comm_up)
    up_comm_out = ()
    if comm_up is not None:
        up, up_comm_out = up
    (act, u_ffn), comm_out = _ffn_act_fwd(up, p["ffn_conv_w"], _row(p["ffn_conv_b"]), name=f"ffn_act_fwd{tag}", comm=comm)
    o, x2 = _mm_resid(act, p["w_down"], x1, mod_ffn, tm=512, name=f"down_fwd{tag}")
    sv.update(h=h, proj=proj, xbc_act=xbc_act, cpre=cpre, d_x=d_x, y=y, yn=yn, hprev=hprev, a_conf=a_conf, uc=uc, a_sc=a_sc,
              ya=ya, yb=yb, yc=yc, merged=merged, mix=mix, x1=x1, h2=h2, up=up, u_ffn=u_ffn, act=act, o=o)
    return x2, sv, tuple(up_comm_out) + tuple(comm_out)


def _layer_bwd(dx, p, sv, mod_mix, mod_ffn, tag, comm=None, make_comm_conf=None):
    g = {}
    do2, dgate_ffn = _gate_bwd(dx, sv["o"], mod_ffn, name=f"gate_ffn_bwd{tag}")
    dact = _mm_nt(do2, p["w_down"], out_dtype=BF, tm=1024, tk=D, name=f"down_dx{tag}")
    g["w_down"] = _mm_tn(sv["act"], do2, tn=D, tk=1024, name=f"down_dw{tag}")
    (dup, g["ffn_conv_w"], dffn_b), comm_out = _ffn_act_bwd(sv["up"], sv["u_ffn"], dact, p["ffn_conv_w"],
                                                            name=f"ffn_act_bwd{tag}", comm=comm)
    g["ffn_conv_b"] = dffn_b[0]
    dh2 = _mm_nt_chips(dup, p["w_up4"], out_dtype=F32, tm=1024, name=f"up_dx{tag}")
    g["w_up4"] = _mm_tn(sv["h2"], dup, tn=2 * DFF // N_CHIPS, tk=2048, by_chip=True, name=f"up_dw{tag}")
    dx1, dgn, dsh, dsc = _modnorm_bwd(dh2, sv["x1"], dx, _row(p["norm_ffn_g"]), mod_ffn, name=f"modnorm_ffn_bwd{tag}")
    g["norm_ffn_g"] = dgn[0]
    dmod_ffn = jnp.concatenate([dsh[0], dsc[0], dgate_ffn[0]])

    (do1, dya, dyb, dyc, dproj, dyn, dac, dasc, dgate_mix, dbg) = _mixer_out_bwd(
        dx1, sv["mix"], sv["ya"], sv["yb"], sv["yc"], sv["proj"], _row(p["b_gate"]), p["w_ssd_out"], p["w_conf_out"],
        p["w_sc_out"], p["w_o"], mod_mix, name=f"mixer_out_bwd{tag}")
    g["b_gate"] = dbg[0]
    g["w_o"] = _mm_tn(sv["merged"], do1, tn=D, tk=2048, name=f"wo_dw{tag}")
    g["w_ssd_out"] = _mm_tn(sv["yn"], dya, tn=D, tk=2048, name=f"wssd_dw{tag}")
    g["w_conf_out"] = _mm_tn(sv["a_conf"], dyb, tn=D, tk=2048, name=f"wconf_dw{tag}")
    g["w_sc_out"] = _mm_tn(sv["a_sc"], dyc, tn=D, tk=2048, name=f"wsc_dw{tag}")
    comm_conf = make_comm_conf(g) if make_comm_conf is not None else None
    (dproj, g["conf_conv_w"], dcb, dlg, dlb), conf_comm_out = _conf_bwd(
        sv["proj"], sv["uc"], dac, dproj, p["conf_conv_w"], _row(p["conf_ln_g"]), _row(p["conf_ln_b"]),
        name=f"conf_bwd{tag}", comm=comm_conf)
    g["conf_conv_b"], g["conf_ln_g"], g["conf_ln_b"] = dcb[0], dlg[0], dlb[0]
    dproj, g["sc_conv_w"] = _sc_bwd(sv["proj"], dasc, dproj, p["sc_conv_w"], name=f"sc_bwd{tag}")
    dproj, ddt, dxbc_act, dng, ddtb, dalog, ddd = _ssd_bwd(
        sv["xbc_act"], sv["proj"], sv["y"], dyn, sv["hprev"], dproj, _head_row(p["ssd_dt_bias"]),
        _head_row(p["ssd_a_log"]), sv["d_x"], _row(p["ssd_norm_g"]), name=f"ssd_bwd{tag}")
    g["ssd_norm_g"], g["ssd_dt_bias"], g["ssd_a_log"], g["ssd_d"] = dng[0], ddtb[0, :HEADS], dalog[0, :HEADS], ddd[0, :HEADS]
    dproj, g["ssd_conv_w"], dsb = _ssd_pre_bwd(sv["proj"], sv["cpre"], dxbc_act, ddt, dproj, p["ssd_conv_w"],
                                               name=f"ssd_pre_bwd{tag}")
    g["ssd_conv_b"] = dsb[0]
    dh = _mm_nt(dproj, p["w_in_pad"], out_dtype=F32, tm=1024, tk=4096, name=f"proj_dx{tag}")
    g["w_in_pad"] = _mm_tn(sv["h"], dproj, tn=2048, tk=2048, name=f"proj_dw{tag}")
    dx0, dgn, dsh, dsc = _modnorm_bwd(dh, sv["x0"], dx1, _row(p["norm_mix_g"]), mod_mix, name=f"modnorm_mix_bwd{tag}")
    g["norm_mix_g"] = dgn[0]
    dmod_mix = jnp.concatenate([dsh[0], dsc[0], dgate_mix[0]])
    return dx0, g, dmod_mix, dmod_ffn, (comm_out, conf_comm_out)


def _local_step(x, target, layers, mods, final_norm_g):
    saved = []
    for i, p in enumerate(layers):
        x, sv, _ = _layer_fwd(x, p, mods[i][0], mods[i][1], f"_l{i}")
        saved.append(sv)
    dx, loss, dfg = _final_loss(x, _row(final_norm_g), target, name="final_loss")
    grads, dmods = [None] * len(layers), [None] * len(layers)
    for i in reversed(range(len(layers))):
        dx, grads[i], dmm, dmf, _ = _layer_bwd(dx, layers[i], saved[i], mods[i][0], mods[i][1], f"_l{i}")
        dmods[i] = (dmm, dmf)
    return loss, dx, grads, dmods, dfg[0]


MESH = pl.DeviceIdType.MESH
ANY = pl.BlockSpec(memory_space=pl.ANY)
VMEM = pl.BlockSpec(memory_space=pltpu.VMEM)


def _mesh_pos():
    return lax.axis_index("x"), lax.axis_index("y"), lax.axis_index("c")


def _peer(pos, mask):
    return tuple(1 - v if (mask >> (2 - k)) & 1 else v for k, v in enumerate(pos))


def _lin(pos):
    return 4 * pos[0] + 2 * pos[1] + pos[2]


def _chip(pos):
    return 2 * pos[0] + pos[1]


def _rcopy(src, dst, send_sem, recv_sem, dev):
    return pltpu.make_async_remote_copy(src_ref=src, dst_ref=dst, send_sem=send_sem, recv_sem=recv_sem,
                                        device_id=dev, device_id_type=MESH)


CHIP_MASKS = (2, 4, 6)
SIBLING = 1
ADA_COLS = 3 * D // N_CHIPS
CONV_ROWS, CONV_COLS = 48, 2 * DFF // N_CHIPS
CONV_PACK = {"ffn_conv_w": (0, FFN_K, 2 * DFF // N_CHIPS), "ssd_conv_w": (3, SSD_K, XBC // N_CHIPS),
             "conf_conv_w": (8, CONF_K, CONF_W // N_CHIPS), "sc_conv_w": (40, SC_K, SC_W // N_CHIPS)}


def _ada_exchange(c_blk, ada_mix_w, ada_ffn_w, conv_pack):
    def body(c_ref, wm_ref, wf_ref, cw_ref, mods_ref, sc_ref, cwall_ref,
             call_scr, modp_scr, recv_scr, s1, r1, s3, r3, s4, r4):
        pos = _mesh_pos()
        me, km = _lin(pos), _chip(pos)
        call_scr[me] = c_ref[...]
        cwall_ref[km] = cw_ref[...]
        sends = []
        for m in range(1, N_DEV):
            sends.append(_rcopy(c_ref, call_scr.at[me], s1.at[m - 1], r1.at[m - 1], _peer(pos, m)))
        for j, m in enumerate(CHIP_MASKS):
            sends.append(_rcopy(cw_ref, cwall_ref.at[km], s4.at[j], r4.at[j], _peer(pos, m)))
        for cp in sends:
            cp.start()
        for m in range(1, N_DEV):
            src = _peer(pos, m)
            _rcopy(c_ref, call_scr.at[_lin(src)], s1.at[m - 1], r1.at[m - 1], src).wait_recv()
        cm = jnp.concatenate([call_scr[d, 0:1, :] for d in range(N_DEV)], axis=0)
        sc = cm * _sigmoid(cm)
        sc_ref[...] = sc
        for j, w in enumerate((wm_ref.at[0], wm_ref.at[1], wf_ref.at[0], wf_ref.at[1])):
            modp_scr[:, j * ADA_COLS:(j + 1) * ADA_COLS] = lax.dot_general(
                sc, w[...], NN, precision=HIGHEST, preferred_element_type=F32)
        recv_scr[km] = modp_scr[...]
        sends3 = [_rcopy(modp_scr, recv_scr.at[km], s3.at[j], r3.at[j], _peer(pos, m)) for j, m in enumerate(CHIP_MASKS)]
        for cp in sends3:
            cp.start()
        for j, m in enumerate(CHIP_MASKS):
            src = _peer(pos, m)
            _rcopy(modp_scr, recv_scr.at[_chip(src)], s3.at[j], r3.at[j], src).wait_recv()
            _rcopy(cw_ref, cwall_ref.at[_chip(src)], s4.at[j], r4.at[j], src).wait_recv()
        for k in range(N_CHIPS):
            mods_ref[k:k + 1, :] = recv_scr[k, pl.ds(me, 1), :]
        for cp in sends + sends3:
            cp.wait_send()

    dma = pltpu.SemaphoreType.DMA
    return pl.pallas_call(
        body, in_specs=[VMEM] * 4, out_specs=[VMEM] * 3,
        out_shape=[S((N_CHIPS, 4 * ADA_COLS), F32), S((N_DEV, D), F32), S((N_CHIPS,) + conv_pack.shape, F32)],
        scratch_shapes=[pltpu.VMEM((N_DEV, 8, D), F32), pltpu.VMEM((N_DEV, 4 * ADA_COLS), F32),
                        pltpu.VMEM((N_CHIPS, N_DEV, 4 * ADA_COLS), F32),
                        dma((N_DEV - 1,)), dma((N_DEV - 1,)), dma((3,)), dma((3,)), dma((3,)), dma((3,))],
        compiler_params=pltpu.CompilerParams(vmem_limit_bytes=VMEM_LIMIT_V7X), name="ada_exchange")(
            c_blk, ada_mix_w, ada_ffn_w, conv_pack)


class _Comm:
    def __init__(self, ins, out_shapes, scratch, start, finish):
        self.ins, self.out_shapes, self.scratch, self.start, self.finish = list(ins), list(out_shapes), list(scratch), start, finish


def _call_with_comm(body, nsteps, *, in_specs, out_specs, out_shape, scratch_shapes, args, comm, name, aliases=None):
    grid = nsteps if isinstance(nsteps, tuple) else (nsteps,)
    sem = ("arbitrary",) * len(grid)
    if comm is None:
        res = pl.pallas_call(body, grid=grid, in_specs=in_specs, out_specs=out_specs, out_shape=out_shape,
                             scratch_shapes=scratch_shapes, input_output_aliases=aliases or {},
                             compiler_params=_cparams(*sem), name=name)(*args)
        return tuple(res), ()
    ni, no, ns = len(in_specs), len(out_specs), len(scratch_shapes)
    ci, co = len(comm.ins), len(comm.out_shapes)

    def hosted(*refs):
        ins, cins = refs[:ni], refs[ni:ni + ci]
        outs, couts = refs[ni + ci:ni + ci + no], refs[ni + ci + no:ni + ci + no + co]
        scr, csem = refs[ni + ci + no + co:ni + ci + no + co + ns], refs[ni + ci + no + co + ns:]
        ids = [pl.program_id(d) for d in range(len(grid))]
        is_first, is_last = ids[0] == 0, ids[0] == grid[0] - 1
        for d in range(1, len(grid)):
            is_first = jnp.logical_and(is_first, ids[d] == 0)
            is_last = jnp.logical_and(is_last, ids[d] == grid[d] - 1)

        @pl.when(is_first)
        def _():
            comm.start(cins, couts, csem)

        body(*ins, *outs, *scr)

        @pl.when(is_last)
        def _():
            comm.finish(cins, couts, csem)

    res = pl.pallas_call(
        hosted, grid=grid, in_specs=list(in_specs) + [ANY] * ci, out_specs=list(out_specs) + [ANY] * co,
        out_shape=list(out_shape) + comm.out_shapes, scratch_shapes=list(scratch_shapes) + comm.scratch,
        input_output_aliases=aliases or {}, compiler_params=_cparams(*sem), name=name)(*args, *comm.ins)
    return tuple(res[:no]), tuple(res[no:])


def _run_comm(comm, name):
    def body(*refs):
        ci, co = len(comm.ins), len(comm.out_shapes)
        comm.start(refs[:ci], refs[ci:ci + co], refs[ci + co:])
        comm.finish(refs[:ci], refs[ci:ci + co], refs[ci + co:])

    return pl.pallas_call(body, in_specs=[ANY] * len(comm.ins), out_specs=[ANY] * len(comm.out_shapes),
                          out_shape=comm.out_shapes, scratch_shapes=comm.scratch, name=name)(*comm.ins)


def _gather_comm(shards, layer):
    na = len(shards)
    dma = pltpu.SemaphoreType.DMA

    def ici(ins, outs, sems, pos, a, j, m, sender):
        ssem, rsem = sems[0], sems[1]
        peer = _peer(pos, m)
        block = outs[a].at[_chip(pos) if sender else _chip(peer)]
        return _rcopy(ins[a].at[layer], block, ssem.at[a, j], rsem.at[a, j], peer)

    def start(ins, outs, sems):
        pos = _mesh_pos()

        @pl.when(pos[2] == layer)
        def _():
            for a in range(na):
                for j, m in enumerate(CHIP_MASKS):
                    ici(ins, outs, sems, pos, a, j, m, True).start()

    def finish(ins, outs, sems):
        fsend, frecv = sems[2], sems[3]
        pos = _mesh_pos()
        sib = _peer(pos, SIBLING)

        def forward(a, j, m):
            blk = outs[a].at[_chip(_peer(pos, m))]
            return _rcopy(blk, blk, fsend.at[a, j], frecv.at[a, j], sib)

        @pl.when(pos[2] == layer)
        def _():
            for a in range(na):
                for j, m in enumerate(CHIP_MASKS):
                    ici(ins, outs, sems, pos, a, j, m, False).wait_recv()
                    forward(a, j, m).start()
            for a in range(na):
                for j, m in enumerate(CHIP_MASKS):
                    ici(ins, outs, sems, pos, a, j, m, True).wait_send()
                    forward(a, j, m).wait_send()

        @pl.when(pos[2] != layer)
        def _():
            for a in range(na):
                for j, m in enumerate(CHIP_MASKS):
                    forward(a, j, m).wait_recv()

    return _Comm(shards, [S((N_CHIPS,) + s.shape[1:], s.dtype) for s in shards],
                 [dma((na, 3)), dma((na, 3)), dma((na, 3)), dma((na, 3))], start, finish)


def _scatter_comm(arrs, layer):
    na = len(arrs)
    dma = pltpu.SemaphoreType.DMA

    def copy(ins, outs, sems, pos, a, j, m, sender):
        peer = _peer(pos, m)
        src = ins[a].at[_chip(peer) if sender else _chip(pos)]
        dst = outs[a].at[_chip(pos) if sender else _chip(peer)]
        return _rcopy(src, dst, sems[0].at[a, j], sems[1].at[a, j], peer)

    def start(ins, outs, sems):
        pos = _mesh_pos()

        @pl.when(pos[2] == layer)
        def _():
            for a in range(na):
                for j, m in enumerate(CHIP_MASKS):
                    copy(ins, outs, sems, pos, a, j, m, True).start()

    def finish(ins, outs, sems):
        pos = _mesh_pos()

        @pl.when(pos[2] == layer)
        def _():
            for a in range(na):
                for j, m in enumerate(CHIP_MASKS):
                    copy(ins, outs, sems, pos, a, j, m, False).wait_recv()
            for a in range(na):
                for j, m in enumerate(CHIP_MASKS):
                    copy(ins, outs, sems, pos, a, j, m, True).wait_send()

    return _Comm(arrs, [S(s.shape, s.dtype) for s in arrs], [dma((na, 3)), dma((na, 3))], start, finish)


def _swap_layer(arrs, layer, tag):
    na = len(arrs)

    def body(*refs):
        ins, outs = refs[:na], refs[na:2 * na]
        ssem, rsem = refs[2 * na:]
        pos = _mesh_pos()
        sib = _peer(pos, SIBLING)
        cps = [_rcopy(ins[a], outs[a], ssem.at[a], rsem.at[a], sib) for a in range(na)]

        @pl.when(pos[2] != layer)
        def _():
            for cp in cps:
                cp.start()
            for cp in cps:
                cp.wait_send()

        @pl.when(pos[2] == layer)
        def _():
            for cp in cps:
                cp.wait_recv()

    dma = pltpu.SemaphoreType.DMA
    return pl.pallas_call(
        body, in_specs=[ANY] * na, out_specs=[ANY] * na, out_shape=[S(s.shape, s.dtype) for s in arrs],
        scratch_shapes=[dma((na,)), dma((na,))], name=f"swap_layer{tag}")(*arrs)


def _share_sibling(arrs):
    na = len(arrs)

    def body(*refs):
        bufs = refs[na:2 * na]
        ssem, rsem = refs[2 * na:]
        pos = _mesh_pos()
        c = pos[2]
        sib = _peer(pos, SIBLING)
        sends = [_rcopy(bufs[a].at[c], bufs[a].at[c], ssem.at[a], rsem.at[a], sib) for a in range(na)]
        for cp in sends:
            cp.start()
        for a in range(na):
            _rcopy(bufs[a].at[c], bufs[a].at[1 - c], ssem.at[a], rsem.at[a], sib).wait_recv()
        for cp in sends:
            cp.wait_send()

    dma = pltpu.SemaphoreType.DMA
    return pl.pallas_call(
        body, in_specs=[ANY] * na, out_specs=[ANY] * na, out_shape=[S(s.shape, s.dtype) for s in arrs],
        input_output_aliases={a: a for a in range(na)},
        scratch_shapes=[dma((na,)), dma((na,))], name="share_sibling")(*arrs)


def _small_allreduce(vec):
    r = vec.shape[0]

    def body(v_ref, sum_ref, all_ref, ssem, rsem):
        pos = _mesh_pos()
        me = _lin(pos)
        all_ref[me] = v_ref[...]
        cps = [_rcopy(v_ref, all_ref.at[me], ssem.at[m - 1], rsem.at[m - 1], _peer(pos, m)) for m in range(1, N_DEV)]
        for cp in cps:
            cp.start()
        for m in range(1, N_DEV):
            src = _peer(pos, m)
            _rcopy(v_ref, all_ref.at[_lin(src)], ssem.at[m - 1], rsem.at[m - 1], src).wait_recv()
        acc = all_ref[0]
        for d in range(1, N_DEV):
            acc = acc + all_ref[d]
        sum_ref[...] = acc
        for cp in cps:
            cp.wait_send()

    dma = pltpu.SemaphoreType.DMA
    return pl.pallas_call(
        body, in_specs=[VMEM], out_specs=[VMEM, VMEM],
        out_shape=[S((r, 128), F32), S((N_DEV, r, 128), F32)],
        scratch_shapes=[dma((N_DEV - 1,)), dma((N_DEV - 1,))],
        compiler_params=pltpu.CompilerParams(vmem_limit_bytes=VMEM_LIMIT_V7X), name="small_allreduce")(vec)


ROW_BYTES_TARGET = 1 << 20


def _row_tile(rows, cols, itemsize=4):
    t = rows
    while t % 2 == 0 and t * cols * itemsize > ROW_BYTES_TARGET and (t // 2) % 16 == 0:
        t //= 2
    return t


def _pair_add(g, other):
    r, cdim = g.shape
    tr = _row_tile(r, cdim, 1)

    def body(g_ref, o_ref, out_ref):
        out_ref[...] = (g_ref[...].astype(F32) + o_ref[...].astype(F32)).astype(out_ref.dtype)

    blk = pl.BlockSpec((tr, cdim), lambda i: (i, 0))
    return pl.pallas_call(body, grid=(r // tr,), in_specs=[blk, blk], out_specs=blk, out_shape=S((r, cdim), BF),
                          compiler_params=_cparams("parallel"), name="pair_add")(g, other)


def _sum4(parts, pairs, chip_core):
    _, r, cdim = parts[0].shape
    tr = _row_tile(r, cdim)

    def body(kc_ref, q0_ref, q1_ref, own0_ref, own1_ref, out_ref):
        first = kc_ref[1] == 0
        mine = jnp.where(first, own0_ref[...], own1_ref[...]).astype(F32)
        terms = [jnp.where(kc_ref[0] == j, mine, jnp.where(first, q0_ref[j], q1_ref[j]).astype(F32))
                 for j in range(N_CHIPS)]
        out_ref[...] = ((terms[0] + terms[1]) + terms[2]) + terms[3]

    allc = pl.BlockSpec((N_CHIPS, tr, cdim), lambda i, kc: (0, i, 0))
    own = pl.BlockSpec((None, tr, cdim), lambda i, kc: (kc[0], i, 0))
    return pl.pallas_call(
        body,
        grid_spec=pltpu.PrefetchScalarGridSpec(
            num_scalar_prefetch=1, grid=(r // tr,), in_specs=[allc, allc, own, own],
            out_specs=pl.BlockSpec((None, tr, cdim), lambda i, kc: (kc[1], i, 0))),
        out_shape=S((DEPTH, r, cdim), F32), compiler_params=_cparams("parallel"), name="sum4")(
            chip_core, parts[0], parts[1], pairs[0], pairs[1])


def _ada_w_grad(silu_c, dmod_cols, chip):
    def body(k_ref, sc_ref, dm_ref, o_ref):
        del k_ref
        o_ref[...] = lax.dot_general(sc_ref[...], dm_ref[...], TN, precision=HIGHEST, preferred_element_type=F32)

    return pl.pallas_call(
        body,
        grid_spec=pltpu.PrefetchScalarGridSpec(
            num_scalar_prefetch=1, grid=(4,),
            in_specs=[pl.BlockSpec((N_DEV, D), lambda j, k: (0, 0)),
                      pl.BlockSpec((None, N_DEV, ADA_COLS), lambda j, k: (4 * j + k[0], 0, 0))],
            out_specs=pl.BlockSpec((None, D, ADA_COLS), lambda j, k: (j, 0, 0))),
        out_shape=S((4, D, ADA_COLS), F32), compiler_params=_cparams("parallel"), name="ada_w_grad")(chip, silu_c, dmod_cols)


def _adamw(w, g, m, v, *, name):
    r, cdim = w.shape
    tr = _row_tile(r, cdim)
    c1 = 1.0 / (1.0 - ADAM_B1 ** ADAM_STEP)
    c2 = 1.0 / (1.0 - ADAM_B2 ** ADAM_STEP)

    def body(w_ref, g_ref, m_ref, v_ref, d_ref, mo_ref, vo_ref):
        gv = g_ref[...]
        mn = ADAM_B1 * m_ref[...] + (1.0 - ADAM_B1) * gv
        vn = ADAM_B2 * v_ref[...] + (1.0 - ADAM_B2) * (gv * gv)
        mo_ref[...] = mn
        vo_ref[...] = vn
        d_ref[...] = -ADAM_LR * ((mn * c1) / (jnp.sqrt(vn * c2) + ADAM_EPS) + ADAM_WD * w_ref[...])

    blk = pl.BlockSpec((tr, cdim), lambda i: (i, 0))
    return pl.pallas_call(
        body, grid=(r // tr,), in_specs=[blk] * 4, out_specs=[blk] * 3, out_shape=[S((r, cdim), F32)] * 3,
        compiler_params=_cparams("parallel"), name=name)(w, g, m, v)


def _adamw_many(ws, gs, ms, vs):
    n = len(ws)
    c1 = 1.0 / (1.0 - ADAM_B1 ** ADAM_STEP)
    c2 = 1.0 / (1.0 - ADAM_B2 ** ADAM_STEP)

    def body(*refs):
        for i in range(n):
            w_ref, g_ref, m_ref, v_ref, d_ref, mo_ref, vo_ref = (refs[k * n + i] for k in range(7))
            gv = g_ref[...]
            mn = ADAM_B1 * m_ref[...] + (1.0 - ADAM_B1) * gv
            vn = ADAM_B2 * v_ref[...] + (1.0 - ADAM_B2) * (gv * gv)
            mo_ref[...] = mn
            vo_ref[...] = vn
            d_ref[...] = -ADAM_LR * ((mn * c1) / (jnp.sqrt(vn * c2) + ADAM_EPS) + ADAM_WD * w_ref[...])

    shapes = [S(a.shape, F32) for a in ws]
    outs = pl.pallas_call(
        body, in_specs=[VMEM] * (4 * n), out_specs=[VMEM] * (3 * n), out_shape=shapes * 3,
        compiler_params=pltpu.CompilerParams(vmem_limit_bytes=VMEM_LIMIT_V7X), name="adamw_small")(*ws, *gs, *ms, *vs)
    return outs[0:n], outs[n:2 * n], outs[2 * n:3 * n]


WEIGHTS = ['ada_mix_w', 'ada_mix_b', 'norm_mix_g', 'w_in', 'b_gate', 'ssd_conv_w', 'ssd_conv_b', 'ssd_dt_bias',
           'ssd_a_log', 'ssd_d', 'ssd_norm_g', 'w_ssd_out', 'conf_conv_w', 'conf_conv_b', 'conf_ln_g', 'conf_ln_b',
           'w_conf_out', 'sc_conv_w', 'w_sc_out', 'w_o', 'ada_ffn_w', 'ada_ffn_b', 'norm_ffn_g', 'w_up', 'ffn_conv_w',
           'ffn_conv_b', 'w_down', 'final_norm_g']
SMALL = ['ada_mix_b', 'norm_mix_g', 'b_gate', 'ssd_conv_b', 'ssd_dt_bias', 'ssd_a_log', 'ssd_d', 'ssd_norm_g', 'conf_conv_b',
         'conf_ln_g', 'conf_ln_b', 'ada_ffn_b', 'norm_ffn_g', 'ffn_conv_b']
CONVS = ['ssd_conv_w', 'conf_conv_w', 'sc_conv_w', 'ffn_conv_w']
BIG = ['ada_mix_w', 'ada_ffn_w', 'w_in', 'w_up', 'w_conf_out', 'w_sc_out', 'w_ssd_out', 'w_o', 'w_down']


def _pack_rows(pieces):
    flat = [p.reshape(-1) for p in pieces]
    offs, o = [], 0
    for f in flat:
        offs.append(o)
        o += f.shape[0]
    total = -(-o // 1024) * 1024
    vec = jnp.concatenate(flat + [jnp.zeros((total - o,), F32)])
    return vec.reshape(total // 128, 128), offs


def _by_chip(a, axis):
    shp = a.shape
    a = a.reshape(shp[:axis] + (N_CHIPS, shp[axis] // N_CHIPS) + shp[axis + 1:])
    return jnp.moveaxis(a, axis, 0)


def _from_chips(a, axis):
    a = jnp.moveaxis(a, 0, axis)
    shp = a.shape
    return a.reshape(shp[:axis] + (shp[axis] * shp[axis + 1],) + shp[axis + 2:])


def kernel(x, c, ada_mix_w, ada_mix_b, norm_mix_g, w_in, b_gate, ssd_conv_w, ssd_conv_b, ssd_dt_bias, ssd_a_log, ssd_d, ssd_norm_g, w_ssd_out, conf_conv_w, conf_conv_b, conf_ln_g, conf_ln_b, w_conf_out, sc_conv_w, w_sc_out, w_o, ada_ffn_w, ada_ffn_b, norm_ffn_g, w_up, ffn_conv_w, ffn_conv_b, w_down, final_norm_g, loss_target, m_ada_mix_w, m_ada_mix_b, m_norm_mix_g, m_w_in, m_b_gate, m_ssd_conv_w, m_ssd_conv_b, m_ssd_dt_bias, m_ssd_a_log, m_ssd_d, m_ssd_norm_g, m_w_ssd_out, m_conf_conv_w, m_conf_conv_b, m_conf_ln_g, m_conf_ln_b, m_w_conf_out, m_sc_conv_w, m_w_sc_out, m_w_o, m_ada_ffn_w, m_ada_ffn_b, m_norm_ffn_g, m_w_up, m_ffn_conv_w, m_ffn_conv_b, m_w_down, m_final_norm_g, v_ada_mix_w, v_ada_mix_b, v_norm_mix_g, v_w_in, v_b_gate, v_ssd_conv_w, v_ssd_conv_b, v_ssd_dt_bias, v_ssd_a_log, v_ssd_d, v_ssd_norm_g, v_w_ssd_out, v_conf_conv_w, v_conf_conv_b, v_conf_ln_g, v_conf_ln_b, v_w_conf_out, v_sc_conv_w, v_w_sc_out, v_w_o, v_ada_ffn_w, v_ada_ffn_b, v_norm_ffn_g, v_w_up, v_ffn_conv_w, v_ffn_conv_b, v_w_down, v_final_norm_g):
    args = locals()
    w = {n: args[n] for n in WEIGHTS}
    mom = {n: args["m_" + n] for n in WEIGHTS}
    var = {n: args["v_" + n] for n in WEIGHTS}
    pos = _mesh_pos()
    chip = _chip(pos)
    core = pos[2]

    conv_pack = jnp.zeros((DEPTH, CONV_ROWS, CONV_COLS), F32)
    for n, (r0, taps, width) in CONV_PACK.items():
        conv_pack = conv_pack.at[:, r0:r0 + taps, 0:width].set(w[n])
    c_blk = jnp.pad(c, ((0, 7), (0, 0)))
    mods_raw, silu_c, conv_all = _ada_exchange(c_blk, ada_mix_w, ada_ffn_w, conv_pack)
    ada_b = jnp.concatenate([ada_mix_b, ada_ffn_b], axis=0)
    mod_all = mods_raw.reshape(N_CHIPS, 4, ADA_COLS).transpose(1, 0, 2).reshape(4, 3 * D) + ada_b
    mod_all = mod_all.reshape(4, 3, D)
    mods = [(mod_all[i], mod_all[2 + i]) for i in range(DEPTH)]
    conv_full = {n: _from_chips(conv_all[:, :, r0:r0 + taps, 0:width], 2) for n, (r0, taps, width) in CONV_PACK.items()}

    cast = lambda a: a.astype(BF)
    shards = [cast(w_in), jnp.concatenate([cast(w_conf_out), cast(w_sc_out)], axis=1),
              jnp.concatenate([cast(w_ssd_out), cast(w_o)], axis=1), cast(w_up), cast(w_down)]
    EARLY, LATE = (0, 1, 2), (3, 4)
    pick = lambda arrs, idx: [arrs[k] for k in idx]

    def own_block(l, gathered, idx):
        return [lax.dynamic_update_slice(g, shards[k][l][None], (chip, 0, 0)) for g, k in zip(gathered, idx)]

    def early_params(l, gathered):
        g_in, g_cs, g_row = own_block(l, gathered, EARLY)
        p = {n: w[n][l] for n in SMALL if not n.startswith("ada_")}
        p.update({n: conv_full[n][l] for n in CONVS})
        p["w_in_pad"] = _pad_w_in_chips(g_in)
        p["w_conf_out"] = _from_chips(g_cs[:, 0:CONF_W], 1)
        p["w_sc_out"] = _from_chips(g_cs[:, CONF_W:], 1)
        p["w_ssd_out"] = _from_chips(g_row[:, 0:INNER // N_CHIPS], 0)
        p["w_o"] = _from_chips(g_row[:, INNER // N_CHIPS:], 0)
        return p

    def late_params(l, gathered):
        g_up, g_down = own_block(l, gathered, LATE)
        return {"w_up4": g_up, "w_down": _from_chips(g_down, 0)}

    def big_grads(g, idx):
        makers = (lambda: _unpad_w_in_chips(g["w_in_pad"]),
                  lambda: _by_chip(jnp.concatenate([g["w_conf_out"], g["w_sc_out"]], axis=0), 1),
                  lambda: jnp.concatenate([_by_chip(g["w_ssd_out"], 0), _by_chip(g["w_o"], 0)], axis=1),
                  lambda: g["w_up4"], lambda: _by_chip(g["w_down"], 0))
        return [makers[k]().astype(BF) for k in idx]

    def pair_sums(big, l, tag):
        theirs = _swap_layer(big, l, tag)
        out = []
        for g4, t4 in zip(big, theirs):
            k, r, cdim = g4.shape
            out.append(_pair_add(g4.reshape(k * r, cdim), t4.reshape(k * r, cdim)).reshape(k, r, cdim))
        return out

    seq = x.shape[1]
    xs = x.reshape(seq, D)
    p0 = early_params(0, _run_comm(_gather_comm(pick(shards, EARLY), 0), "gather_weights_l0"))
    xs, sv0, gathered1 = _layer_fwd(xs, p0, mods[0][0], mods[0][1], "_l0",
                                    comm_ssd=_gather_comm(pick(shards, LATE), 0),
                                    late_params=lambda got: late_params(0, got),
                                    comm_up=_gather_comm(pick(shards, EARLY), 1),
                                    comm=_gather_comm(pick(shards, LATE), 1))
    p1 = early_params(1, gathered1[:len(EARLY)])
    p1.update(late_params(1, gathered1[len(EARLY):]))
    xs, sv1, _ = _layer_fwd(xs, p1, mods[1][0], mods[1][1], "_l1")
    dx, loss, dfg = _final_loss(xs, _row(final_norm_g), loss_target.reshape(seq, D), name="final_loss")
    dfinal = dfg[0]

    every = EARLY + LATE
    grads, dmods = [None, None], [None, None]
    dx, grads[1], dmm, dmf, _ = _layer_bwd(dx, p1, sv1, mods[1][0], mods[1][1], "_l1")
    dmods[1] = (dmm, dmf)
    pair1 = pair_sums(big_grads(grads[1], every), 1, "_l1")
    late0 = {}

    def scatter_late0(g):
        late0["pair"] = pair_sums(big_grads(g, LATE), 0, "_l0_ffn")
        return _scatter_comm(late0["pair"], 0)

    dx, grads[0], dmm, dmf, (parts1, parts0_late) = _layer_bwd(
        dx, p0, sv0, mods[0][0], mods[0][1], "_l0", comm=_scatter_comm(pair1, 1), make_comm_conf=scatter_late0)
    dmods[0] = (dmm, dmf)
    pair0_early = pair_sums(big_grads(grads[0], EARLY), 0, "_l0")
    parts0_early = _run_comm(_scatter_comm(pair0_early, 0), "scatter_chips_l0")
    pair0 = list(pair0_early) + list(late0["pair"])
    parts0 = list(parts0_early) + list(parts0_late)
    chip_core = jnp.stack([chip, core]).astype(jnp.int32)
    red = _share_sibling([_sum4((q0, q1), (o0, o1), chip_core)
                          for q0, q1, o0, o1 in zip(parts0, parts1, pair0, pair1)])
    reduced = {"w_in": red[0], "cs": red[1], "row": red[2], "w_up": red[3], "w_down": red[4]}
    gw = {"w_in": reduced["w_in"], "w_up": reduced["w_up"], "w_down": reduced["w_down"],
          "w_conf_out": reduced["cs"][:, 0:CONF_W], "w_sc_out": reduced["cs"][:, CONF_W:],
          "w_ssd_out": reduced["row"][:, 0:INNER // N_CHIPS], "w_o": reduced["row"][:, INNER // N_CHIPS:]}

    dmod = jnp.stack([dmods[0][0], dmods[1][0], dmods[0][1], dmods[1][1]])
    small_local = {n: jnp.stack([grads[l][n] for l in range(DEPTH)]) for n in SMALL if not n.startswith("ada_")}
    pieces = [loss[0]] + [small_local[n] for n in SMALL if not n.startswith("ada_")]
    pieces += [jnp.stack([grads[l][n] for l in range(DEPTH)]) for n in CONVS] + [dfinal, dmod]
    vec, offs = _pack_rows(pieces)
    vsum, vall = _small_allreduce(vec)
    flat = vsum.reshape(-1)

    def piece(k, like):
        return flat[offs[k]:offs[k] + like.size].reshape(like.shape)

    loss_out = flat[0]
    k = 1
    for n in SMALL:
        if not n.startswith("ada_"):
            gw[n] = piece(k, small_local[n])
            k += 1
    for n in CONVS:
        full = piece(k, conv_full[n])
        width = CONV_PACK[n][2]
        gw[n] = lax.dynamic_slice_in_dim(full, chip * width, width, axis=2)
        k += 1
    gw["final_norm_g"] = piece(k, dfinal)
    k += 1
    dmod_sum = piece(k, dmod)
    gw["ada_mix_b"], gw["ada_ffn_b"] = dmod_sum[0:2], dmod_sum[2:4]
    dmod_all = vall.reshape(N_DEV, -1)[:, offs[k]:offs[k] + dmod.size]
    dmod_cols = dmod_all.reshape(N_DEV, 4 * N_CHIPS, ADA_COLS).transpose(1, 0, 2)
    ada_g = _ada_w_grad(silu_c, dmod_cols, jnp.reshape(chip, (1,)).astype(jnp.int32))
    gw["ada_mix_w"], gw["ada_ffn_w"] = ada_g[0:2], ada_g[2:4]

    delta, new_m, new_v = {}, {}, {}
    for n in BIG:
        shp = w[n].shape
        two_d = lambda a: a.reshape(shp[0] * shp[1], shp[2])
        d_, m_, v_ = _adamw(two_d(w[n]), two_d(gw[n]), two_d(mom[n]), two_d(var[n]), name=f"adamw_{n}")
        delta[n], new_m[n], new_v[n] = d_.reshape(shp), m_.reshape(shp), v_.reshape(shp)
    rest = [n for n in WEIGHTS if n not in BIG]
    two_d = lambda a: a.reshape(1, -1) if a.ndim == 1 else a
    outs = _adamw_many(*[[two_d(src[n]) for n in rest] for src in (w, gw, mom, var)])
    for dst, group in zip((delta, new_m, new_v), outs):
        for n, o in zip(rest, group):
            dst[n] = o.reshape(w[n].shape)

    return (loss_out, dx[None], *[gw[n] for n in WEIGHTS], *[delta[n] for n in WEIGHTS],
            *[new_m[n] for n in WEIGHTS], *[new_v[n] for n in WEIGHTS])
```

```python
import functools

import jax
import jax.numpy as jnp
from jax import lax
from jax.experimental import pallas as pl
from jax.experimental.pallas import tpu as pltpu

F32 = jnp.float32
BF = jnp.bfloat16
S = jax.ShapeDtypeStruct

D = 1024
HEADS = 16
HEAD_DIM = 64
INNER = HEADS * HEAD_DIM
GROUPS = 2
NSTATE = 64
Q = 128
SSD_K = 4
XBC = INNER + 2 * GROUPS * NSTATE
CONF_W = 512
CONF_K = 31
SC_W = 512
SC_K = 3
DFF = 2816
FFN_K = 3
EPS = 1e-6
DEPTH = 2
R_Z, R_XBC, R_DT, R_CONF, R_SC, R_GATES, N_IN = 0, 1024, 2304, 2320, 3344, 4880, 7952
P_GATES, P_SC, P_XBC, P_Z, P_CONF, PW = 0, 3072, 4608, 6144, 7168, 8192
XBC_PAD = 1536
DT_PAD = 128
P_DT = P_XBC + XBC
N_CHIPS = 4
N_DEV = 8

ADAM_LR, ADAM_B1, ADAM_B2, ADAM_EPS, ADAM_WD, ADAM_STEP = 0.001, 0.9, 0.999, 1e-08, 0.01, 10

VMEM_LIMIT_V7X = 56 * 1024 * 1024
HIGHEST = lax.Precision.HIGHEST


def _cparams(*sem):
    return pltpu.CompilerParams(dimension_semantics=sem, vmem_limit_bytes=VMEM_LIMIT_V7X)


def _full(shape):
    n = len(shape)
    return pl.BlockSpec(shape, lambda *_: (0,) * n)


def _rows(tm, w, cb=0):
    return pl.BlockSpec((tm, w), lambda i: (i, cb))


def _prev_rows(tm, halo, w, cb=0):
    r = tm // halo
    return pl.BlockSpec((halo, w), lambda i: (jnp.maximum(i * r - 1, 0), cb))


def _next_rows(tm, halo, w, nrows, cb=0):
    r = tm // halo
    last = nrows // halo - 1
    return pl.BlockSpec((halo, w), lambda i: (jnp.minimum((i + 1) * r, last), cb))


def _sigmoid(v):
    return 1.0 / (1.0 + jnp.exp(-v))


def _softplus(v):
    return jnp.maximum(v, 0.0) + jnp.log(1.0 + jnp.exp(-jnp.abs(v)))


def _colsum(v):
    return jnp.sum(v, axis=0, keepdims=True)


def _tile(n, want):
    t = min(n, want)
    assert n % t == 0, (n, want)
    return t


NN = (((1,), (0,)), ((), ()))
NT = (((1,), (1,)), ((), ()))
TN = (((0,), (0,)), ((), ()))


def _dot(a, b, dims=NN):
    return lax.dot_general(a, b, dims, preferred_element_type=F32)


def _mm(a, b, *, dims, grid, a_spec, b_spec, o_spec, out_shape, acc_shape, name, comm=None):
    nk = grid[2]

    def body(a_ref, b_ref, o_ref, acc_ref):
        k = pl.program_id(2)
        part = _dot(a_ref[...], b_ref[...], dims)
        if nk == 1:
            o_ref[...] = part.astype(o_ref.dtype)
        else:
            @pl.when(k == 0)
            def _():
                acc_ref[...] = part

            @pl.when(k > 0)
            def _():
                acc_ref[...] += part

            @pl.when(k == nk - 1)
            def _():
                o_ref[...] = acc_ref[...].astype(o_ref.dtype)

    scratch = [pltpu.VMEM(acc_shape if nk > 1 else (8, 128), F32)]
    if comm is not None:
        (out,), comm_out = _call_with_comm(body, tuple(grid), in_specs=[a_spec, b_spec], out_specs=[o_spec],
                                           out_shape=[out_shape], scratch_shapes=scratch, args=(a, b), comm=comm, name=name)
        return out, comm_out
    return pl.pallas_call(
        body, grid=grid, in_specs=[a_spec, b_spec], out_specs=o_spec, out_shape=out_shape, scratch_shapes=scratch,
        compiler_params=_cparams("parallel", "parallel", "arbitrary"), name=name)(a, b)


def _mm_nn(a, b, *, out_dtype, tm, tn, name):
    m, k = a.shape
    n = b.shape[1]
    tm, tn = _tile(m, tm), _tile(n, tn)
    return _mm(a, b, dims=NN, grid=(m // tm, n // tn, 1),
               a_spec=pl.BlockSpec((tm, k), lambda i, j, kk: (i, 0)),
               b_spec=pl.BlockSpec((k, tn), lambda i, j, kk: (0, j)),
               o_spec=pl.BlockSpec((tm, tn), lambda i, j, kk: (i, j)),
               out_shape=S((m, n), out_dtype), acc_shape=(tm, tn), name=name)


def _mm_nt(a, b, *, out_dtype, tm, tk, name):
    m, kc = a.shape
    n = b.shape[0]
    tm, tk = _tile(m, tm), _tile(kc, tk)
    return _mm(a, b, dims=NT, grid=(m // tm, 1, kc // tk),
               a_spec=pl.BlockSpec((tm, tk), lambda i, j, kk: (i, kk)),
               b_spec=pl.BlockSpec((n, tk), lambda i, j, kk: (0, kk)),
               o_spec=pl.BlockSpec((tm, n), lambda i, j, kk: (i, 0)),
               out_shape=S((m, n), out_dtype), acc_shape=(tm, n), name=name)


def _mm_tn(a, b, *, tn, tk, name, out_dtype=BF, by_chip=False):
    kc, m = a.shape
    n = b.shape[1]
    tn, tk = _tile(n, tn), _tile(kc, tk)
    if by_chip:
        assert n == N_CHIPS * tn
        o_spec, out_shape = pl.BlockSpec((None, m, tn), lambda i, j, kk: (j, 0, 0)), S((N_CHIPS, m, tn), out_dtype)
    else:
        o_spec, out_shape = pl.BlockSpec((m, tn), lambda i, j, kk: (0, j)), S((m, n), out_dtype)
    return _mm(a, b, dims=TN, grid=(1, n // tn, kc // tk),
               a_spec=pl.BlockSpec((tk, m), lambda i, j, kk: (kk, 0)),
               b_spec=pl.BlockSpec((tk, tn), lambda i, j, kk: (kk, j)),
               o_spec=o_spec, out_shape=out_shape, acc_shape=(m, tn), name=name)


def _mm_nn_chips(a, b4, *, out_dtype, tm, name, comm=None):
    m, k = a.shape
    n4 = b4.shape[2]
    tm = _tile(m, tm)
    return _mm(a, b4, dims=NN, grid=(m // tm, N_CHIPS, 1),
               a_spec=pl.BlockSpec((tm, k), lambda i, j, kk: (i, 0)),
               b_spec=pl.BlockSpec((None, k, n4), lambda i, j, kk: (j, 0, 0)),
               o_spec=pl.BlockSpec((tm, n4), lambda i, j, kk: (i, j)),
               out_shape=S((m, N_CHIPS * n4), out_dtype), acc_shape=(tm, n4), name=name, comm=comm)


def _mm_nt_chips(a, b4, *, out_dtype, tm, name):
    m = a.shape[0]
    n, n4 = b4.shape[1], b4.shape[2]
    tm = _tile(m, tm)
    return _mm(a, b4, dims=NT, grid=(m // tm, 1, N_CHIPS),
               a_spec=pl.BlockSpec((tm, n4), lambda i, j, kk: (i, kk)),
               b_spec=pl.BlockSpec((None, n, n4), lambda i, j, kk: (kk, 0, 0)),
               o_spec=pl.BlockSpec((tm, n), lambda i, j, kk: (i, 0)),
               out_shape=S((m, n), out_dtype), acc_shape=(tm, n), name=name)


def _mm_resid(a, b, x, mod, *, tm, name):
    m, k = a.shape
    n = b.shape[1]
    tm = _tile(m, tm)

    def body(a_ref, b_ref, x_ref, mod_ref, o_ref, xn_ref):
        o = _dot(a_ref[...], b_ref[...])
        o_ref[...] = o.astype(o_ref.dtype)
        xn_ref[...] = x_ref[...] + mod_ref[2:3, :] * o

    return pl.pallas_call(
        body, grid=(m // tm,),
        in_specs=[_rows(tm, k), _full((k, n)), _rows(tm, n), _full((3, n))],
        out_specs=[_rows(tm, n), _rows(tm, n)],
        out_shape=[S((m, n), BF), S((m, n), F32)],
        compiler_params=_cparams("parallel"), name=name)(a, b, x, mod)


def _modnorm_fwd(x, gain, mod, *, name):
    n = x.shape[0]
    tm = _tile(n, 512)

    def body(x_ref, g_ref, mod_ref, h_ref):
        xv = x_ref[...]
        r = lax.rsqrt(jnp.mean(xv * xv, axis=-1, keepdims=True) + EPS)
        y = xv * r * g_ref[...]
        h_ref[...] = (y * (1.0 + mod_ref[1:2, :]) + mod_ref[0:1, :]).astype(h_ref.dtype)

    return pl.pallas_call(
        body, grid=(n // tm,), in_specs=[_rows(tm, D), _full((1, D)), _full((3, D))],
        out_specs=_rows(tm, D), out_shape=S((n, D), BF), compiler_params=_cparams("parallel"), name=name)(x, gain, mod)


def _modnorm_bwd(dh, x, dres, gain, mod, *, name):
    n = x.shape[0]
    tm = _tile(n, 512)

    def body(dh_ref, x_ref, dres_ref, g_ref, mod_ref, dx_ref, dg_ref, dsh_ref, dsc_ref):
        i = pl.program_id(0)
        xv = x_ref[...]
        r = lax.rsqrt(jnp.mean(xv * xv, axis=-1, keepdims=True) + EPS)
        xh = xv * r
        dhv = dh_ref[...]
        g = g_ref[...]
        dy = dhv * (1.0 + mod_ref[1:2, :])
        dxh = dy * g
        dx = r * (dxh - xh * jnp.mean(dxh * xh, axis=-1, keepdims=True))
        dx_ref[...] = dres_ref[...] + dx

        @pl.when(i == 0)
        def _():
            dg_ref[...] = jnp.zeros_like(dg_ref)
            dsh_ref[...] = jnp.zeros_like(dsh_ref)
            dsc_ref[...] = jnp.zeros_like(dsc_ref)

        dg_ref[...] += _colsum(dy * xh)
        dsh_ref[...] += _colsum(dhv)
        dsc_ref[...] += _colsum(dhv * xh * g)

    vec = S((1, D), F32)
    return pl.pallas_call(
        body, grid=(n // tm,),
        in_specs=[_rows(tm, D), _rows(tm, D), _rows(tm, D), _full((1, D)), _full((3, D))],
        out_specs=[_rows(tm, D), _full((1, D)), _full((1, D)), _full((1, D))],
        out_shape=[S((n, D), F32), vec, vec, vec],
        compiler_params=_cparams("arbitrary"), name=name)(dh, x, dres, gain, mod)


def _final_loss(x, gain, target, *, name):
    n = x.shape[0]
    tm = _tile(n, 512)

    def body(x_ref, g_ref, t_ref, dx_ref, loss_ref, dg_ref):
        i = pl.program_id(0)
        xv = x_ref[...]
        g = g_ref[...]
        r = lax.rsqrt(jnp.mean(xv * xv, axis=-1, keepdims=True) + EPS)
        xh = xv * r
        err = xh * g - t_ref[...]
        dy = err * (1.0 / D)
        dxh = dy * g
        dx_ref[...] = r * (dxh - xh * jnp.mean(dxh * xh, axis=-1, keepdims=True))

        @pl.when(i == 0)
        def _():
            loss_ref[...] = jnp.zeros_like(loss_ref)
            dg_ref[...] = jnp.zeros_like(dg_ref)

        part = _colsum(jnp.sum(err * err, axis=-1, keepdims=True)) * (0.5 / D)
        loss_ref[...] += jnp.broadcast_to(part, loss_ref.shape)
        dg_ref[...] += _colsum(dy * xh)

    return pl.pallas_call(
        body, grid=(n // tm,),
        in_specs=[_rows(tm, D), _full((1, D)), _rows(tm, D)],
        out_specs=[_rows(tm, D), _full((1, 128)), _full((1, D))],
        out_shape=[S((n, D), F32), S((1, 128), F32), S((1, D), F32)],
        compiler_params=_cparams("arbitrary"), name=name)(x, gain, target)


def _gate_bwd(dx, o, mod, *, name):
    n = dx.shape[0]
    tm = _tile(n, 512)

    def body(dx_ref, o_ref, mod_ref, do_ref, dgt_ref):
        i = pl.program_id(0)
        dxv = dx_ref[...]
        do_ref[...] = (dxv * mod_ref[2:3, :]).astype(do_ref.dtype)

        @pl.when(i == 0)
        def _():
            dgt_ref[...] = jnp.zeros_like(dgt_ref)

        dgt_ref[...] += _colsum(dxv * o_ref[...].astype(F32))

    return pl.pallas_call(
        body, grid=(n // tm,), in_specs=[_rows(tm, D), _rows(tm, D), _full((3, D))],
        out_specs=[_rows(tm, D), _full((1, D))], out_shape=[S((n, D), BF), S((1, D), F32)],
        compiler_params=_cparams("arbitrary"), name=name)(dx, o, mod)


def _conv(buf, w_ref, taps, start, rows, ch):
    acc = None
    for k in range(taps):
        term = buf[pl.ds(start - (taps - 1) + k, rows), 0:ch] * w_ref[k:k + 1, :]
        acc = term if acc is None else acc + term
    return acc


def _conv_t(buf, w_ref, taps, start, rows, ch):
    acc = None
    for k in range(taps):
        term = buf[pl.ds(start + (taps - 1) - k, rows), 0:ch] * w_ref[k:k + 1, :]
        acc = term if acc is None else acc + term
    return acc


def _conv_dw(dw_ref, dy, xbuf, taps, xstart, rows, ch):
    for k in range(taps):
        dw_ref[k:k + 1, :] += _colsum(dy * xbuf[pl.ds(xstart - (taps - 1) + k, rows), 0:ch])


HALO = 16
CONF_HALO = 32
CHUNK = 32
STRIP = 256


def _blocks8(v):
    return [v[8 * i:8 * (i + 1)] for i in range(v.shape[0] // 8)]


def _delay_rows(blocks, s):
    sub = lax.broadcasted_iota(jnp.int32, blocks[0].shape, 0)
    rolled = [pltpu.roll(b, s, 0) for b in blocks]
    return [jnp.where(sub < s, rolled[i - 1], rolled[i]) for i in range(1, len(blocks))]


def _advance_rows(blocks, s):
    sub = lax.broadcasted_iota(jnp.int32, blocks[0].shape, 0)
    rolled = [pltpu.roll(b, 8 - s, 0) for b in blocks]
    return [jnp.where(sub < 8 - s, rolled[i], rolled[i + 1]) for i in range(len(blocks) - 1)]


def _conv3_chunk(tail, xv, wk):
    blocks = [tail] + _blocks8(xv)
    x1 = jnp.concatenate(_delay_rows(blocks, 1), axis=0)
    x2 = jnp.concatenate(_delay_rows(blocks, 2), axis=0)
    return wk[0] * x2 + wk[1] * x1 + wk[2] * xv


def _conv_chunk(tail, xv, wk):
    taps = len(wk)
    blocks = [tail] + _blocks8(xv)
    acc = wk[taps - 1] * xv
    for d in range(1, taps):
        acc = acc + wk[taps - 1 - d] * jnp.concatenate(_delay_rows(blocks, d), axis=0)
    return acc


def _ssd_pre_fwd(proj, w, b, *, name):
    n = proj.shape[0]
    tm = _tile(n, 512)
    cb = P_XBC // XBC_PAD

    def body(prev_ref, cur_ref, w_ref, b_ref, o_ref, c_ref, buf):
        i = pl.program_id(0)
        buf[0:HALO, :] = jnp.where(i == 0, 0.0, prev_ref[:, 0:XBC].astype(F32))
        buf[HALO:HALO + tm, :] = cur_ref[:, 0:XBC].astype(F32)
        c = _conv(buf, w_ref, SSD_K, HALO, tm, XBC) + b_ref[...]
        c_ref[...] = c.astype(c_ref.dtype)
        o_ref[...] = (c * _sigmoid(c)).astype(o_ref.dtype)

    return pl.pallas_call(
        body, grid=(n // tm,),
        in_specs=[_prev_rows(tm, HALO, XBC_PAD, cb), _rows(tm, XBC_PAD, cb), _full((SSD_K, XBC)), _full((1, XBC))],
        out_specs=[_rows(tm, XBC), _rows(tm, XBC)], out_shape=[S((n, XBC), BF), S((n, XBC), BF)],
        scratch_shapes=[pltpu.VMEM((HALO + tm, XBC), F32)],
        compiler_params=_cparams("parallel"), name=name)(proj, proj, w, b)


def _ssd_pre_bwd(proj, cpre, dact, ddt, dproj, w, *, name):
    n = proj.shape[0]
    tm = _tile(n, 512)
    nt = n // tm
    cb = P_XBC // XBC_PAD

    def body(x_ref, cc_ref, cn_ref, dc_ref, dn_ref, ddt_ref, w_ref, dproj_in, o_ref, dw_ref, db_ref, dbuf, acc):
        del dproj_in
        i = pl.program_id(0)
        last = i == nt - 1

        @pl.when(i == 0)
        def _():
            acc[...] = jnp.zeros_like(acc)

        def silu_bwd(cv, dav):
            sg = _sigmoid(cv)
            return dav * (sg * (1.0 + cv * (1.0 - sg)))

        for s in range(XBC // STRIP):
            c = pl.ds(s * STRIP, STRIP)
            wk = [w_ref[k:k + 1, c] for k in range(SSD_K)]

            def step1(j, carry):
                rows = pl.ds(pl.multiple_of(j * CHUNK, CHUNK), CHUNK)
                dbuf[rows, c] = silu_bwd(cc_ref[rows, c].astype(F32), dc_ref[rows, c].astype(F32))
                return carry

            lax.fori_loop(0, tm // CHUNK, step1, 0, unroll=2)
            dbuf[tm:tm + HALO, c] = silu_bwd(cn_ref[:, c].astype(F32), jnp.where(last, 0.0, dn_ref[:, c].astype(F32)))

            def step2(j, carry):
                r0 = pl.multiple_of(j * CHUNK, CHUNK)
                rows = pl.ds(r0, CHUNK)
                win = dbuf[pl.ds(r0, CHUNK + 8), c]
                blocks = _blocks8(win)
                xv = x_ref[rows, c].astype(F32)
                d0 = win[0:CHUNK]
                dx = wk[SSD_K - 1] * d0
                acc[SSD_K - 1, :, c] += d0 * xv
                acc[SSD_K, :, c] += d0
                for adv in range(1, SSD_K):
                    dk = jnp.concatenate(_advance_rows(blocks, adv), axis=0)
                    dx = dx + wk[SSD_K - 1 - adv] * dk
                    acc[SSD_K - 1 - adv, :, c] += dk * xv
                o_ref[rows, c] = dx.astype(o_ref.dtype)
                return carry

            lax.fori_loop(0, tm // CHUNK, step2, 0)

        o_ref[:, XBC:XBC + DT_PAD] = ddt_ref[...]
        o_ref[:, XBC + DT_PAD:XBC_PAD] = jnp.zeros((tm, XBC_PAD - XBC - DT_PAD), o_ref.dtype)

        @pl.when(last)
        def _():
            for k in range(SSD_K):
                dw_ref[k:k + 1, :] = _colsum(acc[k])
            db_ref[...] = _colsum(acc[SSD_K])

    return pl.pallas_call(
        body, grid=(nt,),
        in_specs=[_rows(tm, XBC_PAD, cb), _rows(tm, XBC), _next_rows(tm, HALO, XBC, n),
                  _rows(tm, XBC), _next_rows(tm, HALO, XBC, n), _rows(tm, DT_PAD),
                  _full((SSD_K, XBC)), pl.BlockSpec(memory_space=pl.ANY)],
        out_specs=[_rows(tm, XBC_PAD, cb), _full((SSD_K, XBC)), _full((1, XBC))],
        out_shape=[S(dproj.shape, dproj.dtype), S((SSD_K, XBC), F32), S((1, XBC), F32)],
        scratch_shapes=[pltpu.VMEM((tm + HALO, XBC), F32), pltpu.VMEM((SSD_K + 1, CHUNK, XBC), F32)],
        input_output_aliases={7: 0},
        compiler_params=_cparams("arbitrary"), name=name)(proj, cpre, cpre, dact, dact, ddt, w, dproj)


def _sc_fwd(proj, w, *, name):
    n = proj.shape[0]
    tm = _tile(n, 512)
    cb = P_SC // (3 * SC_W)

    def body(prev_ref, cur_ref, w_ref, o_ref, buf):
        i = pl.program_id(0)
        pv = prev_ref[...].astype(F32)
        cv = cur_ref[...].astype(F32)
        buf[0:HALO, :] = jnp.where(i == 0, 0.0, pv[:, SC_W:2 * SC_W] * pv[:, 2 * SC_W:])
        buf[HALO:HALO + tm, :] = cv[:, SC_W:2 * SC_W] * cv[:, 2 * SC_W:]
        q = _conv(buf, w_ref, SC_K, HALO, tm, SC_W)
        o_ref[...] = (cv[:, 0:SC_W] * q).astype(o_ref.dtype)

    return pl.pallas_call(
        body, grid=(n // tm,),
        in_specs=[_prev_rows(tm, HALO, 3 * SC_W, cb), _rows(tm, 3 * SC_W, cb), _full((SC_K, SC_W))],
        out_specs=_rows(tm, SC_W), out_shape=S((n, SC_W), BF),
        scratch_shapes=[pltpu.VMEM((HALO + tm, SC_W), F32)],
        compiler_params=_cparams("parallel"), name=name)(proj, proj, w)


def _sc_bwd(proj, da, dproj, w, *, name):
    n = proj.shape[0]
    tm = _tile(n, 512)
    nt = n // tm
    cb = P_SC // (3 * SC_W)

    def body(xp_ref, xc_ref, xn_ref, dc_ref, dn_ref, w_ref, dproj_in, o_ref, dw_ref, pbuf, dbuf):
        del dproj_in
        i = pl.program_id(0)
        pv = xp_ref[...].astype(F32)
        cv = xc_ref[...].astype(F32)
        nv = xn_ref[...].astype(F32)
        gb, gc, xv = cv[:, 0:SC_W], cv[:, SC_W:2 * SC_W], cv[:, 2 * SC_W:]
        pbuf[0:HALO, :] = jnp.where(i == 0, 0.0, pv[:, SC_W:2 * SC_W] * pv[:, 2 * SC_W:])
        pbuf[HALO:HALO + tm, :] = gc * xv
        q = _conv(pbuf, w_ref, SC_K, HALO, tm, SC_W)
        dav = dc_ref[...].astype(F32)
        dbuf[0:tm, :] = dav * gb
        dbuf[tm:tm + HALO, :] = jnp.where(i == nt - 1, 0.0, dn_ref[...].astype(F32) * nv[:, 0:SC_W])
        dp = _conv_t(dbuf, w_ref, SC_K, 0, tm, SC_W)
        o_ref[:, 0:SC_W] = (dav * q).astype(o_ref.dtype)
        o_ref[:, SC_W:2 * SC_W] = (dp * xv).astype(o_ref.dtype)
        o_ref[:, 2 * SC_W:] = (dp * gc).astype(o_ref.dtype)

        @pl.when(i == 0)
        def _():
            dw_ref[...] = jnp.zeros_like(dw_ref)

        _conv_dw(dw_ref, dbuf[0:tm, :], pbuf, SC_K, HALO, tm, SC_W)

    return pl.pallas_call(
        body, grid=(nt,),
        in_specs=[_prev_rows(tm, HALO, 3 * SC_W, cb), _rows(tm, 3 * SC_W, cb), _next_rows(tm, HALO, 3 * SC_W, n, cb),
                  _rows(tm, SC_W), _next_rows(tm, HALO, SC_W, n), _full((SC_K, SC_W)),
                  pl.BlockSpec(memory_space=pl.ANY)],
        out_specs=[_rows(tm, 3 * SC_W, cb), _full((SC_K, SC_W))],
        out_shape=[S(dproj.shape, dproj.dtype), S((SC_K, SC_W), F32)],
        scratch_shapes=[pltpu.VMEM((HALO + tm, SC_W), F32), pltpu.VMEM((tm + HALO, SC_W), F32)],
        input_output_aliases={6: 0},
        compiler_params=_cparams("arbitrary"), name=name)(proj, proj, proj, da, da, w, dproj)


def _conf_fwd(proj, w, b, ln_g, ln_b, *, name):
    n = proj.shape[0]
    tm = _tile(n, 512)
    cb = P_CONF // (2 * CONF_W)
    h = CONF_HALO

    def body(prev_ref, cur_ref, w_ref, b_ref, g_ref, be_ref, a_ref, uc_ref, buf):
        i = pl.program_id(0)
        pv = prev_ref[...].astype(F32)
        cv = cur_ref[...].astype(F32)
        buf[0:h, :] = jnp.where(i == 0, 0.0, pv[:, 0:CONF_W] * _sigmoid(pv[:, CONF_W:]))
        buf[h:h + tm, :] = cv[:, 0:CONF_W] * _sigmoid(cv[:, CONF_W:])
        uc = _conv(buf, w_ref, CONF_K, h, tm, CONF_W) + b_ref[...]
        uc_ref[...] = uc.astype(uc_ref.dtype)
        mu = jnp.mean(uc, axis=-1, keepdims=True)
        xc = uc - mu
        v = xc * lax.rsqrt(jnp.mean(xc * xc, axis=-1, keepdims=True) + EPS) * g_ref[...] + be_ref[...]
        a_ref[...] = (v * _sigmoid(v)).astype(a_ref.dtype)

    vec = _full((1, CONF_W))
    return pl.pallas_call(
        body, grid=(n // tm,),
        in_specs=[_prev_rows(tm, h, 2 * CONF_W, cb), _rows(tm, 2 * CONF_W, cb), _full((CONF_K, CONF_W)), vec, vec, vec],
        out_specs=[_rows(tm, CONF_W), _rows(tm, CONF_W)],
        out_shape=[S((n, CONF_W), BF), S((n, CONF_W), BF)],
        scratch_shapes=[pltpu.VMEM((h + tm, CONF_W), F32)],
        compiler_params=_cparams("parallel"), name=name)(proj, proj, w, b, ln_g, ln_b)


def _conf_bwd(proj, uc, da, dproj, w, ln_g, ln_b, *, name, comm=None):
    n = proj.shape[0]
    tm = _tile(n, 512)
    nt = n // tm
    cb = P_CONF // (2 * CONF_W)
    h = CONF_HALO

    def body(xp_ref, xc_ref, ucc_ref, ucn_ref, dac_ref, dan_ref, w_ref, g_ref, be_ref, dproj_in,
             o_ref, dw_ref, db_ref, dg_ref, dbe_ref, ubuf, dbuf):
        del dproj_in
        i = pl.program_id(0)
        pv = xp_ref[...].astype(F32)
        cv = xc_ref[...].astype(F32)
        val, gt = cv[:, 0:CONF_W], cv[:, CONF_W:]
        sg = _sigmoid(gt)
        ubuf[0:h, :] = jnp.where(i == 0, 0.0, pv[:, 0:CONF_W] * _sigmoid(pv[:, CONF_W:]))
        ubuf[h:h + tm, :] = val * sg

        def ln_silu_bwd(ucv, dav):
            mu = jnp.mean(ucv, axis=-1, keepdims=True)
            xc = ucv - mu
            r = lax.rsqrt(jnp.mean(xc * xc, axis=-1, keepdims=True) + EPS)
            xh = xc * r
            v = xh * g_ref[...] + be_ref[...]
            s = _sigmoid(v)
            dv = dav * (s * (1.0 + v * (1.0 - s)))
            dxh = dv * g_ref[...]
            duc = r * (dxh - jnp.mean(dxh, axis=-1, keepdims=True) - xh * jnp.mean(dxh * xh, axis=-1, keepdims=True))
            return duc, dv, xh

        duc, dv, xh = ln_silu_bwd(ucc_ref[...].astype(F32), dac_ref[...].astype(F32))
        dbuf[0:tm, :] = duc
        ducn, _, _ = ln_silu_bwd(ucn_ref[...].astype(F32), dan_ref[...].astype(F32))
        dbuf[tm:tm + h, :] = jnp.where(i == nt - 1, 0.0, ducn)
        du = _conv_t(dbuf, w_ref, CONF_K, 0, tm, CONF_W)
        o_ref[:, 0:CONF_W] = (du * sg).astype(o_ref.dtype)
        o_ref[:, CONF_W:] = (du * val * sg * (1.0 - sg)).astype(o_ref.dtype)

        @pl.when(i == 0)
        def _():
            dw_ref[...] = jnp.zeros_like(dw_ref)
            db_ref[...] = jnp.zeros_like(db_ref)
            dg_ref[...] = jnp.zeros_like(dg_ref)
            dbe_ref[...] = jnp.zeros_like(dbe_ref)

        dg_ref[...] += _colsum(dv * xh)
        dbe_ref[...] += _colsum(dv)
        db_ref[...] += _colsum(duc)
        _conv_dw(dw_ref, duc, ubuf, CONF_K, h, tm, CONF_W)

    vec = _full((1, CONF_W))
    vshape = S((1, CONF_W), F32)
    return _call_with_comm(
        body, nt,
        in_specs=[_prev_rows(tm, h, 2 * CONF_W, cb), _rows(tm, 2 * CONF_W, cb),
                  _rows(tm, CONF_W), _next_rows(tm, h, CONF_W, n), _rows(tm, CONF_W), _next_rows(tm, h, CONF_W, n),
                  _full((CONF_K, CONF_W)), vec, vec, pl.BlockSpec(memory_space=pl.ANY)],
        out_specs=[_rows(tm, 2 * CONF_W, cb), _full((CONF_K, CONF_W)), vec, vec, vec],
        out_shape=[S(dproj.shape, dproj.dtype), S((CONF_K, CONF_W), F32), vshape, vshape, vshape],
        scratch_shapes=[pltpu.VMEM((h + tm, CONF_W), F32), pltpu.VMEM((tm + h, CONF_W), F32)],
        args=(proj, proj, uc, uc, da, da, w, ln_g, ln_b, dproj), comm=comm, name=name, aliases={9: 0})


def _ffn_act_fwd(up, w, b, *, name, comm=None):
    n = up.shape[0]
    tm = _tile(n, 512)
    c2 = 2 * DFF

    def body(prev_ref, cur_ref, w_ref, b_ref, o_ref, u_ref):
        first = pl.program_id(0) == 0
        for s in range(DFF // STRIP):
            cols = (pl.ds(s * STRIP, STRIP), pl.ds(DFF + s * STRIP, STRIP))
            wk = [[w_ref[k:k + 1, c] for k in range(FFN_K)] for c in cols]
            bk = [b_ref[:, c] for c in cols]
            tails = tuple(jnp.where(first, 0.0, prev_ref[:, c].astype(F32)[HALO - 8:HALO]) for c in cols)

            def step(j, tails):
                r0 = pl.multiple_of(j * CHUNK, CHUNK)
                us, new_tails = [], []
                for h in range(2):
                    xv = cur_ref[pl.ds(r0, CHUNK), cols[h]].astype(F32)
                    us.append(_conv3_chunk(tails[h], xv, wk[h]) + bk[h])
                    u_ref[pl.ds(r0, CHUNK), cols[h]] = us[h].astype(u_ref.dtype)
                    new_tails.append(xv[CHUNK - 8:CHUNK])
                o_ref[pl.ds(r0, CHUNK), cols[0]] = (us[0] * _sigmoid(us[0]) * us[1]).astype(o_ref.dtype)
                return tuple(new_tails)

            lax.fori_loop(0, tm // CHUNK, step, tails, unroll=2)

    return _call_with_comm(
        body, n // tm,
        in_specs=[_prev_rows(tm, HALO, c2), _rows(tm, c2), _full((FFN_K, c2)), _full((1, c2))],
        out_specs=[_rows(tm, DFF), _rows(tm, c2)], out_shape=[S((n, DFF), BF), S((n, c2), BF)],
        scratch_shapes=[], args=(up, up, w, b), comm=comm, name=name)


def _ffn_act_bwd(up, u, dact, w, *, name, comm=None):
    n = up.shape[0]
    tm = _tile(n, 512)
    nt = n // tm
    c2 = 2 * DFF

    def body(x_ref, uc_ref, un_ref, dc_ref, dn_ref, w_ref, o_ref, dw_ref, db_ref, dbuf, acc):
        i = pl.program_id(0)
        last = i == nt - 1

        @pl.when(i == 0)
        def _():
            acc[...] = jnp.zeros_like(acc)

        def swiglu_bwd(gate, val, dav):
            sg = _sigmoid(gate)
            return dav * val * (sg * (1.0 + gate * (1.0 - sg))), dav * gate * sg

        for s in range(DFF // STRIP):
            cols = (pl.ds(s * STRIP, STRIP), pl.ds(DFF + s * STRIP, STRIP))
            wk = [[w_ref[k:k + 1, c] for k in range(FFN_K)] for c in cols]

            def step1(j, carry):
                r0 = pl.multiple_of(j * CHUNK, CHUNK)
                rows = pl.ds(r0, CHUNK)
                dus = swiglu_bwd(uc_ref[rows, cols[0]].astype(F32), uc_ref[rows, cols[1]].astype(F32),
                                 dc_ref[rows, cols[0]].astype(F32))
                for h in range(2):
                    dbuf[rows, cols[h]] = dus[h]
                return carry

            lax.fori_loop(0, tm // CHUNK, step1, 0, unroll=2)
            dus = swiglu_bwd(un_ref[:, cols[0]].astype(F32), un_ref[:, cols[1]].astype(F32),
                             jnp.where(last, 0.0, dn_ref[:, cols[0]].astype(F32)))
            for h in range(2):
                dbuf[tm:tm + HALO, cols[h]] = dus[h]

            def step2(j, carry):
                r0 = pl.multiple_of(j * CHUNK, CHUNK)
                rows = pl.ds(r0, CHUNK)
                for h in range(2):
                    win = dbuf[pl.ds(r0, CHUNK + 8), cols[h]]
                    blocks = _blocks8(win)
                    d0 = win[0:CHUNK]
                    d1 = jnp.concatenate(_advance_rows(blocks, 1), axis=0)
                    d2 = jnp.concatenate(_advance_rows(blocks, 2), axis=0)
                    o_ref[rows, cols[h]] = (wk[h][2] * d0 + wk[h][1] * d1 + wk[h][0] * d2).astype(o_ref.dtype)
                    xv = x_ref[rows, cols[h]].astype(F32)
                    acc[2, :, cols[h]] += d0 * xv
                    acc[1, :, cols[h]] += d1 * xv
                    acc[0, :, cols[h]] += d2 * xv
                    acc[FFN_K, :, cols[h]] += d0
                return carry

            lax.fori_loop(0, tm // CHUNK, step2, 0)

        @pl.when(last)
        def _():
            for k in range(FFN_K):
                dw_ref[k:k + 1, :] = _colsum(acc[k])
            db_ref[...] = _colsum(acc[FFN_K])

    return _call_with_comm(
        body, nt,
        in_specs=[_rows(tm, c2), _rows(tm, c2), _next_rows(tm, HALO, c2, n),
                  _rows(tm, DFF), _next_rows(tm, HALO, DFF, n), _full((FFN_K, c2))],
        out_specs=[_rows(tm, c2), _full((FFN_K, c2)), _full((1, c2))],
        out_shape=[S((n, c2), BF), S((FFN_K, c2), F32), S((1, c2), F32)],
        scratch_shapes=[pltpu.VMEM((tm + HALO, c2), F32), pltpu.VMEM((FFN_K + 1, CHUNK, c2), F32)],
        args=(up, u, u, dact, dact, w), comm=comm, name=name)


def _head_consts():
    lane = jnp.arange(INNER) // HEAD_DIM
    rep = (jnp.arange(128)[:, None] == lane[None, :]).astype(BF)
    return rep, rep.T


def _split_dot(v, m):
    hi = v.astype(BF)
    lo = (v - hi.astype(F32)).astype(BF)
    return _dot(hi, m) + _dot(lo, m)


def _chunk_decay_terms(dt_raw, dtb, alog, rep):
    row = lax.broadcasted_iota(jnp.int32, (Q, Q), 0)
    col = lax.broadcasted_iota(jnp.int32, (Q, Q), 1)
    lower = row >= col
    upper = col >= row
    dt = _softplus(dt_raw + dtb)
    a = -jnp.exp(alog)
    adt = dt * a
    acum = lax.dot_general(lower.astype(F32), adt, NN, precision=HIGHEST, preferred_element_type=F32)
    acum_t = lax.dot_general(adt, upper.astype(F32), TN, precision=HIGHEST, preferred_element_type=F32)
    alast = acum[Q - 1:Q, :]
    e = jnp.exp(acum)
    f = jnp.exp(alast - acum)
    ex = _split_dot(jnp.concatenate([dt, e, f, jnp.broadcast_to(jnp.exp(alast), (8, 128))], axis=0), rep)
    return dict(lower=lower, upper=upper, dt=dt, a=a, acum=acum, acum_t=acum_t, alast=alast,
                dt_x=ex[0:Q], e_x=ex[Q:2 * Q], f_x=ex[2 * Q:3 * Q], cd_x=ex[3 * Q:3 * Q + 1])


def _block_diag2(v, lo):
    return jnp.concatenate([jnp.where(lo, v, 0.0), jnp.where(lo, 0.0, v)], axis=0).astype(BF)


def _ssd_fwd(xbc_act, proj, dt_bias, a_log, d_x, norm_g, *, name, comm=None):
    n = xbc_act.shape[0]
    nc = n // Q
    rep, _ = _head_consts()

    def body(xs_ref, bc_ref, dt_ref, z_ref, dtb_ref, alog_ref, dx_ref, ng_ref, rep_ref, y_ref, yn_ref, hp_ref,
             h_scr, y_scr):
        i = pl.program_id(0)

        @pl.when(i == 0)
        def _():
            h_scr[...] = jnp.zeros_like(h_scr)

        hp_ref[...] = h_scr[...]
        t = _chunk_decay_terms(dt_ref[...].astype(F32), dtb_ref[...], alog_ref[...], rep_ref[...])
        xs = xs_ref[...].astype(F32)
        xt = xs * t["dt_x"]
        lo = lax.broadcasted_iota(jnp.int32, (Q, 128), 1) < HEAD_DIM
        gw = INNER // GROUPS
        for g in range(GROUPS):
            bm = bc_ref[:, g * NSTATE:(g + 1) * NSTATE]
            cm = bc_ref[:, GROUPS * NSTATE + g * NSTATE:GROUPS * NSTATE + (g + 1) * NSTATE]
            cb = _dot(cm, bm, NT)
            hg = h_scr[:, g * gw:(g + 1) * gw]
            yoff = _dot(cm, hg.astype(BF))
            for jj in range(gw // 128):
                p = g * (gw // 128) + jj
                sl = slice(p * 128, (p + 1) * 128)
                ws = []
                for hd in (2 * p, 2 * p + 1):
                    seg = t["acum"][:, hd:hd + 1] - t["acum_t"][hd:hd + 1, :]
                    ws.append((cb * jnp.exp(jnp.where(t["lower"], seg, -jnp.inf))).astype(BF))
                ydiag = _dot(jnp.concatenate(ws, axis=1), _block_diag2(xt[:, sl], lo))
                y_scr[:, sl] = ydiag + yoff[:, jj * 128:(jj + 1) * 128] * t["e_x"][:, sl] + dx_ref[:, sl] * xs[:, sl]
            xf = (xt[:, g * gw:(g + 1) * gw] * t["f_x"][:, g * gw:(g + 1) * gw]).astype(BF)
            h_scr[:, g * gw:(g + 1) * gw] = hg * t["cd_x"][:, g * gw:(g + 1) * gw] + _dot(bm, xf, TN)
        y = y_scr[...]
        y_ref[...] = y.astype(y_ref.dtype)
        z = z_ref[...].astype(F32)
        v = y * z * _sigmoid(z)
        for g in range(GROUPS):
            vg = v[:, g * gw:(g + 1) * gw]
            r = lax.rsqrt(jnp.mean(vg * vg, axis=-1, keepdims=True) + EPS)
            yn_ref[:, g * gw:(g + 1) * gw] = (vg * r * ng_ref[:, g * gw:(g + 1) * gw]).astype(yn_ref.dtype)

    vec = _full((1, INNER))
    hv = _full((1, 128))
    return _call_with_comm(
        body, nc,
        in_specs=[_rows(Q, INNER, 0), _rows(Q, 2 * GROUPS * NSTATE, INNER // (2 * GROUPS * NSTATE)),
                  _rows(Q, DT_PAD, P_DT // DT_PAD), _rows(Q, INNER, P_Z // INNER),
                  hv, hv, vec, vec, _full((128, INNER))],
        out_specs=[_rows(Q, INNER), _rows(Q, INNER), pl.BlockSpec((None, NSTATE, INNER), lambda i: (i, 0, 0))],
        out_shape=[S((n, INNER), BF), S((n, INNER), BF), S((nc, NSTATE, INNER), F32)],
        scratch_shapes=[pltpu.VMEM((NSTATE, INNER), F32), pltpu.VMEM((Q, INNER), F32)],
        args=(xbc_act, xbc_act, proj, proj, dt_bias, a_log, d_x, norm_g, rep), comm=comm, name=name)


def _ssd_bwd(xbc_act, proj, y, dyn, hprev, dproj, dt_bias, a_log, d_x, norm_g, *, name):
    n = xbc_act.shape[0]
    nc = n // Q
    rep, sel = _head_consts()
    gw = INNER // GROUPS

    def rev(w, cb=0):
        return pl.BlockSpec((Q, w), lambda i: (nc - 1 - i, cb))

    def body(xs_ref, bc_ref, dt_ref, z_ref, y_ref, dyn_ref, hp_ref, dtb_ref, alog_ref, dx_ref, ng_ref, rep_ref,
             sel_ref, dproj_in, dz_ref, ddt_ref, dxbc_ref, dng_ref, ddtb_ref, dalog_ref, dd_ref,
             dh_scr, dxt_scr, st_scr, off_scr, rs_scr, cs_scr, dng_acc, ddtb_acc, da_acc, dd_acc):
        del dproj_in
        i = pl.program_id(0)

        @pl.when(i == 0)
        def _():
            for r in (dh_scr, dng_acc, ddtb_acc, da_acc, dd_acc):
                r[...] = jnp.zeros_like(r)

        y = y_ref[...].astype(F32)
        z = z_ref[...].astype(F32)
        sz = _sigmoid(z)
        silu = z * sz
        v = y * silu
        dyn = dyn_ref[...].astype(F32)
        dvs = []
        for g in range(GROUPS):
            gs = slice(g * gw, (g + 1) * gw)
            vg = v[:, gs]
            r = lax.rsqrt(jnp.mean(vg * vg, axis=-1, keepdims=True) + EPS)
            vn = vg * r
            dvn = dyn[:, gs] * ng_ref[:, gs]
            dng_acc[:, gs] += _colsum(dyn[:, gs] * vn)
            dvs.append(r * (dvn - vn * jnp.mean(dvn * vn, axis=-1, keepdims=True)))
        dv = jnp.concatenate(dvs, axis=1)
        dy = dv * silu
        dz_ref[...] = (dv * y * (sz * (1.0 + z * (1.0 - sz)))).astype(dz_ref.dtype)

        dt_raw = dt_ref[...].astype(F32)
        t = _chunk_decay_terms(dt_raw, dtb_ref[...], alog_ref[...], rep_ref[...])
        xs = xs_ref[...].astype(F32)
        dsk = dx_ref[...]
        dd_acc[...] += _colsum(dy * xs)
        xt = xs * t["dt_x"]
        dye = dy * t["e_x"]
        xtf = xt * t["f_x"]
        hp = hp_ref[...]
        dh = dh_scr[...]
        lo = lax.broadcasted_iota(jnp.int32, (Q, 128), 1) < HEAD_DIM
        rs_scr[...] = jnp.zeros_like(rs_scr)
        cs_scr[...] = jnp.zeros_like(cs_scr)
        for g in range(GROUPS):
            gs = slice(g * gw, (g + 1) * gw)
            bm = bc_ref[:, g * NSTATE:(g + 1) * NSTATE]
            cm = bc_ref[:, GROUPS * NSTATE + g * NSTATE:GROUPS * NSTATE + (g + 1) * NSTATE]
            cbt = _dot(bm, cm, NT)
            dhg = dh[:, gs].astype(BF)
            hpg = hp[:, gs].astype(BF)
            dxt_state = _dot(bm, dhg) * t["f_x"][:, gs]
            st_scr[:, gs] = dxt_state
            dye_g = dye[:, gs]
            off_scr[:, gs] = dye_g * _dot(cm, hpg)
            dye_b = dye_g.astype(BF)
            db = _dot(xtf[:, gs].astype(BF), dhg, NT)
            dc = _dot(dye_b, hpg, NT)
            dh_scr[:, gs] = t["cd_x"][:, gs] * dh[:, gs] + _dot(cm, dye_b, TN)
            dcbt = jnp.zeros((Q, Q), F32)
            for jj in range(gw // 128):
                p = g * (gw // 128) + jj
                sl = slice(p * 128, (p + 1) * 128)
                lts, wfs = [], []
                for hd in (2 * p, 2 * p + 1):
                    seg_t = t["acum_t"][hd:hd + 1, :] - t["acum"][:, hd:hd + 1]
                    lt = jnp.exp(jnp.where(t["upper"], seg_t, -jnp.inf))
                    lts.append(lt)
                    wfs.append(cbt * lt)
                dyp = dy[:, sl]
                dxt_diag = _dot(jnp.concatenate([w.astype(BF) for w in wfs], axis=1), _block_diag2(dyp, lo))
                dwt2 = _dot(_block_diag2(xt[:, sl], lo), dyp.astype(BF), NT)
                for k, hd in enumerate((2 * p, 2 * p + 1)):
                    dwt = dwt2[k * Q:(k + 1) * Q]
                    dcbt = dcbt + dwt * lts[k]
                    mt = dwt * wfs[k]
                    rs_scr[hd:hd + 1, :] = _colsum(mt)
                    cs_scr[:, hd:hd + 1] = jnp.sum(mt, axis=1, keepdims=True)
                dxt_scr[:, sl] = dxt_diag + dxt_state[:, jj * 128:(jj + 1) * 128]
            dcbt_b = dcbt.astype(BF)
            db = db + _dot(dcbt_b, cm)
            dc = dc + _dot(dcbt_b, bm, TN)
            dxbc_ref[:, INNER + g * NSTATE:INNER + (g + 1) * NSTATE] = db.astype(dxbc_ref.dtype)
            dxbc_ref[:, INNER + (GROUPS + g) * NSTATE:INNER + (GROUPS + g + 1) * NSTATE] = dc.astype(dxbc_ref.dtype)
        dxt = dxt_scr[...]
        dst = st_scr[...]
        sel_m = sel_ref[...]
        sums = _split_dot(jnp.concatenate([off_scr[...], xs * dst, xs * dxt], axis=0), sel_m)
        r1_off, r3_state, r3 = sums[0:Q], sums[Q:2 * Q], sums[2 * Q:3 * Q]
        t1 = _colsum(xt * dst)
        t2 = _colsum(dh * hp)
        tails = _split_dot(jnp.concatenate([jnp.broadcast_to(t1, (8, INNER)), jnp.broadcast_to(t2, (8, INNER))], axis=0),
                           sel_m)
        extra = tails[0:1] + jnp.exp(t["alast"]) * tails[8:9]
        last_row = lax.broadcasted_iota(jnp.int32, (Q, 128), 0) == Q - 1
        da_cum = (rs_scr[...].T - cs_scr[...]) + r1_off - t["dt"] * r3_state + jnp.where(last_row, extra, 0.0)
        dadt = lax.dot_general(t["upper"].astype(F32), da_cum, NN, precision=HIGHEST, preferred_element_type=F32)
        ddt = r3 + t["a"] * dadt
        da_acc[...] += _colsum(dadt * t["dt"])
        real = lax.broadcasted_iota(jnp.int32, (Q, 128), 1) < HEADS
        ddraw = jnp.where(real, ddt * _sigmoid(dt_raw + dtb_ref[...]), 0.0)
        ddt_ref[...] = ddraw.astype(ddt_ref.dtype)
        ddtb_acc[...] += _colsum(ddraw)
        dxbc_ref[:, 0:INNER] = (dy * dsk + dxt * t["dt_x"]).astype(dxbc_ref.dtype)

        @pl.when(i == nc - 1)
        def _():
            dng_ref[...] = dng_acc[...]
            ddtb_ref[...] = ddtb_acc[...]
            dalog_ref[...] = da_acc[...] * t["a"]
            dd_ref[...] = _split_dot(jnp.broadcast_to(dd_acc[...], (8, INNER)), sel_m)[0:1]

    vec = _full((1, INNER))
    hv = _full((1, 128))
    return pl.pallas_call(
        body, grid=(nc,),
        in_specs=[rev(INNER, 0), rev(2 * GROUPS * NSTATE, INNER // (2 * GROUPS * NSTATE)),
                  rev(DT_PAD, P_DT // DT_PAD), rev(INNER, P_Z // INNER), rev(INNER), rev(INNER),
                  pl.BlockSpec((None, NSTATE, INNER), lambda i: (nc - 1 - i, 0, 0)),
                  hv, hv, vec, vec, _full((128, INNER)), _full((INNER, 128)), pl.BlockSpec(memory_space=pl.ANY)],
        out_specs=[rev(INNER, P_Z // INNER), rev(DT_PAD), rev(XBC), vec, hv, hv, hv],
        out_shape=[S(dproj.shape, dproj.dtype), S((n, DT_PAD), BF), S((n, XBC), BF),
                   S((1, INNER), F32), S((1, 128), F32), S((1, 128), F32), S((1, 128), F32)],
        scratch_shapes=[pltpu.VMEM((NSTATE, INNER), F32), pltpu.VMEM((Q, INNER), F32), pltpu.VMEM((Q, INNER), F32),
                        pltpu.VMEM((Q, INNER), F32), pltpu.VMEM((128, Q), F32), pltpu.VMEM((Q, 128), F32),
                        pltpu.VMEM((1, INNER), F32), pltpu.VMEM((1, 128), F32), pltpu.VMEM((1, 128), F32),
                        pltpu.VMEM((1, INNER), F32)],
        input_output_aliases={13: 0},
        compiler_params=_cparams("arbitrary"), name=name)(
            xbc_act, xbc_act, proj, proj, y, dyn, hprev, dt_bias, a_log, d_x, norm_g, rep, sel, dproj)


def _mixer_out_fwd(yn, a_conf, a_sc, proj, b_gate, w_ssd, w_conf, w_sc, w_o, x, mod, *, name):
    n = x.shape[0]
    tm = _tile(n, 256)

    def body(yn_ref, ac_ref, as_ref, gt_ref, bg_ref, wa_ref, wb_ref, wc_ref, wo_ref, x_ref, mod_ref,
             ya_ref, yb_ref, yc_ref, mg_ref, mix_ref, xn_ref):
        ya = _dot(yn_ref[...], wa_ref[...])
        yb = _dot(ac_ref[...], wb_ref[...])
        yc = _dot(as_ref[...], wc_ref[...])
        ya_ref[...] = ya.astype(ya_ref.dtype)
        yb_ref[...] = yb.astype(yb_ref.dtype)
        yc_ref[...] = yc.astype(yc_ref.dtype)
        g = _sigmoid(gt_ref[...].astype(F32) + bg_ref[...])
        merged = (g[:, 0:D] * ya + g[:, D:2 * D] * yb + g[:, 2 * D:] * yc).astype(mg_ref.dtype)
        mg_ref[...] = merged
        mix = _dot(merged, wo_ref[...])
        mix_ref[...] = mix.astype(mix_ref.dtype)
        xn_ref[...] = x_ref[...] + mod_ref[2:3, :] * mix

    act = S((n, D), BF)
    return pl.pallas_call(
        body, grid=(n // tm,),
        in_specs=[_rows(tm, INNER), _rows(tm, CONF_W), _rows(tm, SC_W), _rows(tm, 3 * D, P_GATES // (3 * D)),
                  _full((1, 3 * D)), _full((INNER, D)), _full((CONF_W, D)), _full((SC_W, D)), _full((D, D)),
                  _rows(tm, D), _full((3, D))],
        out_specs=[_rows(tm, D)] * 6,
        out_shape=[act, act, act, act, act, S((n, D), F32)],
        compiler_params=_cparams("parallel"), name=name)(yn, a_conf, a_sc, proj, b_gate, w_ssd, w_conf, w_sc, w_o, x, mod)


def _mixer_out_bwd(dx, mix, ya, yb, yc, proj, b_gate, w_ssd, w_conf, w_sc, w_o, mod, *, name):
    n = dx.shape[0]
    tm = _tile(n, 256)

    def body(dx_ref, mix_ref, ya_ref, yb_ref, yc_ref, gt_ref, bg_ref, wa_ref, wb_ref, wc_ref, wo_ref, mod_ref,
             do_ref, dya_ref, dyb_ref, dyc_ref, dgt_ref, dyn_ref, dac_ref, das_ref, dgm_ref, dbg_ref, dm_scr, acc):
        i = pl.program_id(0)
        last = i == pl.num_programs(0) - 1

        @pl.when(i == 0)
        def _():
            acc[...] = jnp.zeros_like(acc)

        do_ref[...] = (dx_ref[...] * mod_ref[2:3, :]).astype(BF)
        dm_scr[...] = _dot(do_ref[...], wo_ref[...], NT)
        branches = ((ya_ref, dya_ref), (yb_ref, dyb_ref), (yc_ref, dyc_ref))
        for s_ in range(D // STRIP):
            c = pl.ds(s_ * STRIP, STRIP)

            def step(j, carry):
                rows = pl.ds(pl.multiple_of(j * CHUNK, CHUNK), CHUNK)
                dmv = dm_scr[rows, c]
                acc[3, :, c] += dx_ref[rows, c] * mix_ref[rows, c].astype(F32)
                for jb, (y_ref, o_ref) in enumerate(branches):
                    cj = pl.ds(jb * D + s_ * STRIP, STRIP)
                    gj = _sigmoid(gt_ref[rows, cj].astype(F32) + bg_ref[:, cj])
                    o_ref[rows, c] = (dmv * gj).astype(o_ref.dtype)
                    dgpre = dmv * y_ref[rows, c].astype(F32) * gj * (1.0 - gj)
                    dgt_ref[rows, cj] = dgpre.astype(dgt_ref.dtype)
                    acc[jb, :, c] += dgpre
                return carry

            lax.fori_loop(0, tm // CHUNK, step, 0)
        dyn_ref[...] = _dot(dya_ref[...], wa_ref[...], NT).astype(dyn_ref.dtype)
        dac_ref[...] = _dot(dyb_ref[...], wb_ref[...], NT).astype(dac_ref.dtype)
        das_ref[...] = _dot(dyc_ref[...], wc_ref[...], NT).astype(das_ref.dtype)

        @pl.when(last)
        def _():
            for jb in range(3):
                dbg_ref[:, jb * D:(jb + 1) * D] = _colsum(acc[jb])
            dgm_ref[...] = _colsum(acc[3])

    act = S((n, D), BF)
    return pl.pallas_call(
        body, grid=(n // tm,),
        in_specs=[_rows(tm, D)] * 5 + [_rows(tm, 3 * D, P_GATES // (3 * D)), _full((1, 3 * D)), _full((INNER, D)),
                                       _full((CONF_W, D)), _full((SC_W, D)), _full((D, D)), _full((3, D))],
        out_specs=[_rows(tm, D)] * 4 + [_rows(tm, 3 * D, P_GATES // (3 * D)), _rows(tm, INNER), _rows(tm, CONF_W),
                                        _rows(tm, SC_W), _full((1, D)), _full((1, 3 * D))],
        out_shape=[act, act, act, act, S((n, PW), BF), S((n, INNER), BF), S((n, CONF_W), BF), S((n, SC_W), BF),
                   S((1, D), F32), S((1, 3 * D), F32)],
        scratch_shapes=[pltpu.VMEM((tm, D), F32), pltpu.VMEM((4, CHUNK, D), F32)],
        compiler_params=_cparams("arbitrary"), name=name)(dx, mix, ya, yb, yc, proj, b_gate, w_ssd, w_conf, w_sc, w_o, mod)


def _pad_w_in(w):
    zeros = jnp.zeros((w.shape[0], XBC_PAD - XBC - (R_CONF - R_DT)), w.dtype)
    return jnp.concatenate([w[:, R_GATES:], w[:, R_SC:R_GATES], w[:, R_XBC:R_DT], w[:, R_DT:R_CONF], zeros,
                            w[:, R_Z:R_XBC], w[:, R_CONF:R_SC]], axis=1)


def _unpad_w_in(wp):
    return jnp.concatenate([wp[:, P_Z:P_Z + INNER], wp[:, P_XBC:P_XBC + XBC], wp[:, P_DT:P_DT + HEADS],
                            wp[:, P_CONF:P_CONF + 2 * CONF_W], wp[:, P_SC:P_SC + 3 * SC_W], wp[:, P_GATES:P_GATES + 3 * D]],
                           axis=1)


W_IN_SHARD = N_IN // N_CHIPS
W_IN_SEGMENTS = ((R_Z, R_XBC, P_Z), (R_XBC, R_DT, P_XBC), (R_DT, R_CONF, P_DT), (R_CONF, R_SC, P_CONF),
                 (R_SC, R_GATES, P_SC), (R_GATES, N_IN, P_GATES))


def _pad_w_in_chips(w4):
    parts = []
    for lo, hi, dst in sorted(W_IN_SEGMENTS, key=lambda sgm: sgm[2]):
        for k in range(N_CHIPS):
            a, b = max(lo, k * W_IN_SHARD), min(hi, (k + 1) * W_IN_SHARD)
            if a < b:
                parts.append((dst + a - lo, w4[k][:, a - k * W_IN_SHARD:b - k * W_IN_SHARD]))
    out, pos = [], 0
    for start, piece in parts:
        if start > pos:
            out.append(jnp.zeros((w4.shape[1], start - pos), w4.dtype))
        out.append(piece)
        pos = start + piece.shape[1]
    if pos < PW:
        out.append(jnp.zeros((w4.shape[1], PW - pos), w4.dtype))
    return jnp.concatenate(out, axis=1)


def _unpad_w_in_chips(wp):
    blocks = []
    for k in range(N_CHIPS):
        pieces = []
        for lo, hi, dst in W_IN_SEGMENTS:
            a, b = max(lo, k * W_IN_SHARD), min(hi, (k + 1) * W_IN_SHARD)
            if a < b:
                pieces.append(wp[:, dst + a - lo:dst + b - lo])
        blocks.append(jnp.concatenate(pieces, axis=1))
    return jnp.stack(blocks)


def _row(v):
    return v.reshape(1, -1)


def _head_row(v):
    return jnp.pad(v, (0, 128 - HEADS)).reshape(1, 128)


def _layer_fwd(x, p, mod_mix, mod_ffn, tag, comm=None, comm_ssd=None, late_params=None, comm_up=None):
    sv = {"x0": x}
    h = _modnorm_fwd(x, _row(p["norm_mix_g"]), mod_mix, name=f"modnorm_mix_fwd{tag}")
    proj = _mm_nn(h, p["w_in_pad"], out_dtype=BF, tm=1024, tn=2048, name=f"proj_fwd{tag}")
    xbc_act, cpre = _ssd_pre_fwd(proj, p["ssd_conv_w"], _row(p["ssd_conv_b"]), name=f"ssd_pre_fwd{tag}")
    d_x = _row(jnp.repeat(p["ssd_d"], HEAD_DIM))
    (y, yn, hprev), ssd_comm_out = _ssd_fwd(xbc_act, proj, _head_row(p["ssd_dt_bias"]), _head_row(p["ssd_a_log"]), d_x,
                                            _row(p["ssd_norm_g"]), name=f"ssd_fwd{tag}", comm=comm_ssd)
    if late_params is not None:
        p.update(late_params(ssd_comm_out))
    a_conf, uc = _conf_fwd(proj, p["conf_conv_w"], _row(p["conf_conv_b"]), _row(p["conf_ln_g"]), _row(p["conf_ln_b"]),
                           name=f"conf_fwd{tag}")
    a_sc = _sc_fwd(proj, p["sc_conv_w"], name=f"sc_fwd{tag}")
    ya, yb, yc, merged, mix, x1 = _mixer_out_fwd(yn, a_conf, a_sc, proj, _row(p["b_gate"]), p["w_ssd_out"],
                                                 p["w_conf_out"], p["w_sc_out"], p["w_o"], x, mod_mix,
                                                 name=f"mixer_out_fwd{tag}")
    h2 = _modnorm_fwd(x1, _row(p["norm_ffn_g"]), mod_ffn, name=f"modnorm_ffn_fwd{tag}")
    up = _mm_nn_chips(h2, p["w_up4"], out_dtype=BF, tm=1024, name=f"up_fwd{tag}", comm=comm_up)
    up_comm_out = ()
    if comm_up is not None:
        up, up_comm_out = up
    (act, u_ffn), comm_out = _ffn_act_fwd(up, p["ffn_conv_w"], _row(p["ffn_conv_b"]), name=f"ffn_act_fwd{tag}", comm=comm)
    o, x2 = _mm_resid(act, p["w_down"], x1, mod_ffn, tm=512, name=f"down_fwd{tag}")
    sv.update(h=h, proj=proj, xbc_act=xbc_act, cpre=cpre, d_x=d_x, y=y, yn=yn, hprev=hprev, a_conf=a_conf, uc=uc, a_sc=a_sc,
              ya=ya, yb=yb, yc=yc, merged=merged, mix=mix, x1=x1, h2=h2, up=up, u_ffn=u_ffn, act=act, o=o)
    return x2, sv, tuple(up_comm_out) + tuple(comm_out)


def _layer_bwd(dx, p, sv, mod_mix, mod_ffn, tag, comm=None, make_comm_conf=None):
    g = {}
    do2, dgate_ffn = _gate_bwd(dx, sv["o"], mod_ffn, name=f"gate_ffn_bwd{tag}")
    dact = _mm_nt(do2, p["w_down"], out_dtype=BF, tm=1024, tk=D, name=f"down_dx{tag}")
    g["w_down"] = _mm_tn(sv["act"], do2, tn=D, tk=1024, name=f"down_dw{tag}")
    (dup, g["ffn_conv_w"], dffn_b), comm_out = _ffn_act_bwd(sv["up"], sv["u_ffn"], dact, p["ffn_conv_w"],
                                                            name=f"ffn_act_bwd{tag}", comm=comm)
    g["ffn_conv_b"] = dffn_b[0]
    dh2 = _mm_nt_chips(dup, p["w_up4"], out_dtype=F32, tm=1024, name=f"up_dx{tag}")
    g["w_up4"] = _mm_tn(sv["h2"], dup, tn=2 * DFF // N_CHIPS, tk=2048, by_chip=True, name=f"up_dw{tag}")
    dx1, dgn, dsh, dsc = _modnorm_bwd(dh2, sv["x1"], dx, _row(p["norm_ffn_g"]), mod_ffn, name=f"modnorm_ffn_bwd{tag}")
    g["norm_ffn_g"] = dgn[0]
    dmod_ffn = jnp.concatenate([dsh[0], dsc[0], dgate_ffn[0]])

    (do1, dya, dyb, dyc, dproj, dyn, dac, dasc, dgate_mix, dbg) = _mixer_out_bwd(
        dx1, sv["mix"], sv["ya"], sv["yb"], sv["yc"], sv["proj"], _row(p["b_gate"]), p["w_ssd_out"], p["w_conf_out"],
        p["w_sc_out"], p["w_o"], mod_mix, name=f"mixer_out_bwd{tag}")
    g["b_gate"] = dbg[0]
    g["w_o"] = _mm_tn(sv["merged"], do1, tn=D, tk=2048, name=f"wo_dw{tag}")
    g["w_ssd_out"] = _mm_tn(sv["yn"], dya, tn=D, tk=2048, name=f"wssd_dw{tag}")
    g["w_conf_out"] = _mm_tn(sv["a_conf"], dyb, tn=D, tk=2048, name=f"wconf_dw{tag}")
    g["w_sc_out"] = _mm_tn(sv["a_sc"], dyc, tn=D, tk=2048, name=f"wsc_dw{tag}")
    comm_conf = make_comm_conf(g) if make_comm_conf is not None else None
    (dproj, g["conf_conv_w"], dcb, dlg, dlb), conf_comm_out = _conf_bwd(
        sv["proj"], sv["uc"], dac, dproj, p["conf_conv_w"], _row(p["conf_ln_g"]), _row(p["conf_ln_b"]),
        name=f"conf_bwd{tag}", comm=comm_conf)
    g["conf_conv_b"], g["conf_ln_g"], g["conf_ln_b"] = dcb[0], dlg[0], dlb[0]
    dproj, g["sc_conv_w"] = _sc_bwd(sv["proj"], dasc, dproj, p["sc_conv_w"], name=f"sc_bwd{tag}")
    dproj, ddt, dxbc_act, dng, ddtb, dalog, ddd = _ssd_bwd(
        sv["xbc_act"], sv["proj"], sv["y"], dyn, sv["hprev"], dproj, _head_row(p["ssd_dt_bias"]),
        _head_row(p["ssd_a_log"]), sv["d_x"], _row(p["ssd_norm_g"]), name=f"ssd_bwd{tag}")
    g["ssd_norm_g"], g["ssd_dt_bias"], g["ssd_a_log"], g["ssd_d"] = dng[0], ddtb[0, :HEADS], dalog[0, :HEADS], ddd[0, :HEADS]
    dproj, g["ssd_conv_w"], dsb = _ssd_pre_bwd(sv["proj"], sv["cpre"], dxbc_act, ddt, dproj, p["ssd_conv_w"],
                                               name=f"ssd_pre_bwd{tag}")
    g["ssd_conv_b"] = dsb[0]
    dh = _mm_nt(dproj, p["w_in_pad"], out_dtype=F32, tm=1024, tk=4096, name=f"proj_dx{tag}")
    g["w_in_pad"] = _mm_tn(sv["h"], dproj, tn=2048, tk=2048, name=f"proj_dw{tag}")
    dx0, dgn, dsh, dsc = _modnorm_bwd(dh, sv["x0"], dx1, _row(p["norm_mix_g"]), mod_mix, name=f"modnorm_mix_bwd{tag}")
    g["norm_mix_g"] = dgn[0]
    dmod_mix = jnp.concatenate([dsh[0], dsc[0], dgate_mix[0]])
    return dx0, g, dmod_mix, dmod_ffn, (comm_out, conf_comm_out)


def _local_step(x, target, layers, mods, final_norm_g):
    saved = []
    for i, p in enumerate(layers):
        x, sv, _ = _layer_fwd(x, p, mods[i][0], mods[i][1], f"_l{i}")
        saved.append(sv)
    dx, loss, dfg = _final_loss(x, _row(final_norm_g), target, name="final_loss")
    grads, dmods = [None] * len(layers), [None] * len(layers)
    for i in reversed(range(len(layers))):
        dx, grads[i], dmm, dmf, _ = _layer_bwd(dx, layers[i], saved[i], mods[i][0], mods[i][1], f"_l{i}")
        dmods[i] = (dmm, dmf)
    return loss, dx, grads, dmods, dfg[0]


MESH = pl.DeviceIdType.MESH
ANY = pl.BlockSpec(memory_space=pl.ANY)
VMEM = pl.BlockSpec(memory_space=pltpu.VMEM)


def _mesh_pos():
    return lax.axis_index("x"), lax.axis_index("y"), lax.axis_index("c")


def _peer(pos, mask):
    return tuple(1 - v if (mask >> (2 - k)) & 1 else v for k, v in enumerate(pos))


def _lin(pos):
    return 4 * pos[0] + 2 * pos[1] + pos[2]


def _chip(pos):
    return 2 * pos[0] + pos[1]


def _rcopy(src, dst, send_sem, recv_sem, dev):
    return pltpu.make_async_remote_copy(src_ref=src, dst_ref=dst, send_sem=send_sem, recv_sem=recv_sem,
                                        device_id=dev, device_id_type=MESH)


CHIP_MASKS = (2, 4, 6)
SIBLING = 1
ADA_COLS = 3 * D // N_CHIPS
CONV_ROWS, CONV_COLS = 48, 2 * DFF // N_CHIPS
CONV_PACK = {"ffn_conv_w": (0, FFN_K, 2 * DFF // N_CHIPS), "ssd_conv_w": (3, SSD_K, XBC // N_CHIPS),
             "conf_conv_w": (8, CONF_K, CONF_W // N_CHIPS), "sc_conv_w": (40, SC_K, SC_W // N_CHIPS)}


def _ada_exchange(c_blk, ada_mix_w, ada_ffn_w, conv_pack):
    def body(c_ref, wm_ref, wf_ref, cw_ref, mods_ref, sc_ref, cwall_ref,
             call_scr, modp_scr, recv_scr, s1, r1, s3, r3, s4, r4):
        pos = _mesh_pos()
        me, km = _lin(pos), _chip(pos)
        call_scr[me] = c_ref[...]
        cwall_ref[km] = cw_ref[...]
        sends = []
        for m in range(1, N_DEV):
            sends.append(_rcopy(c_ref, call_scr.at[me], s1.at[m - 1], r1.at[m - 1], _peer(pos, m)))
        for j, m in enumerate(CHIP_MASKS):
            sends.append(_rcopy(cw_ref, cwall_ref.at[km], s4.at[j], r4.at[j], _peer(pos, m)))
        for cp in sends:
            cp.start()
        for m in range(1, N_DEV):
            src = _peer(pos, m)
            _rcopy(c_ref, call_scr.at[_lin(src)], s1.at[m - 1], r1.at[m - 1], src).wait_recv()
        cm = jnp.concatenate([call_scr[d, 0:1, :] for d in range(N_DEV)], axis=0)
        sc = cm * _sigmoid(cm)
        sc_ref[...] = sc
        for j, w in enumerate((wm_ref.at[0], wm_ref.at[1], wf_ref.at[0], wf_ref.at[1])):
            modp_scr[:, j * ADA_COLS:(j + 1) * ADA_COLS] = lax.dot_general(
                sc, w[...], NN, precision=HIGHEST, preferred_element_type=F32)
        recv_scr[km] = modp_scr[...]
        sends3 = [_rcopy(modp_scr, recv_scr.at[km], s3.at[j], r3.at[j], _peer(pos, m)) for j, m in enumerate(CHIP_MASKS)]
        for cp in sends3:
            cp.start()
        for j, m in enumerate(CHIP_MASKS):
            src = _peer(pos, m)
            _rcopy(modp_scr, recv_scr.at[_chip(src)], s3.at[j], r3.at[j], src).wait_recv()
            _rcopy(cw_ref, cwall_ref.at[_chip(src)], s4.at[j], r4.at[j], src).wait_recv()
        for k in range(N_CHIPS):
            mods_ref[k:k + 1, :] = recv_scr[k, pl.ds(me, 1), :]
        for cp in sends + sends3:
            cp.wait_send()

    dma = pltpu.SemaphoreType.DMA
    return pl.pallas_call(
        body, in_specs=[VMEM] * 4, out_specs=[VMEM] * 3,
        out_shape=[S((N_CHIPS, 4 * ADA_COLS), F32), S((N_DEV, D), F32), S((N_CHIPS,) + conv_pack.shape, F32)],
        scratch_shapes=[pltpu.VMEM((N_DEV, 8, D), F32), pltpu.VMEM((N_DEV, 4 * ADA_COLS), F32),
                        pltpu.VMEM((N_CHIPS, N_DEV, 4 * ADA_COLS), F32),
                        dma((N_DEV - 1,)), dma((N_DEV - 1,)), dma((3,)), dma((3,)), dma((3,)), dma((3,))],
        compiler_params=pltpu.CompilerParams(vmem_limit_bytes=VMEM_LIMIT_V7X), name="ada_exchange")(
            c_blk, ada_mix_w, ada_ffn_w, conv_pack)


class _Comm:
    def __init__(self, ins, out_shapes, scratch, start, finish):
        self.ins, self.out_shapes, self.scratch, self.start, self.finish = list(ins), list(out_shapes), list(scratch), start, finish


def _call_with_comm(body, nsteps, *, in_specs, out_specs, out_shape, scratch_shapes, args, comm, name, aliases=None):
    grid = nsteps if isinstance(nsteps, tuple) else (nsteps,)
    sem = ("arbitrary",) * len(grid)
    if comm is None:
        res = pl.pallas_call(body, grid=grid, in_specs=in_specs, out_specs=out_specs, out_shape=out_shape,
                             scratch_shapes=scratch_shapes, input_output_aliases=aliases or {},
                             compiler_params=_cparams(*sem), name=name)(*args)
        return tuple(res), ()
    ni, no, ns = len(in_specs), len(out_specs), len(scratch_shapes)
    ci, co = len(comm.ins), len(comm.out_shapes)

    def hosted(*refs):
        ins, cins = refs[:ni], refs[ni:ni + ci]
        outs, couts = refs[ni + ci:ni + ci + no], refs[ni + ci + no:ni + ci + no + co]
        scr, csem = refs[ni + ci + no + co:ni + ci + no + co + ns], refs[ni + ci + no + co + ns:]
        ids = [pl.program_id(d) for d in range(len(grid))]
        is_first, is_last = ids[0] == 0, ids[0] == grid[0] - 1
        for d in range(1, len(grid)):
            is_first = jnp.logical_and(is_first, ids[d] == 0)
            is_last = jnp.logical_and(is_last, ids[d] == grid[d] - 1)

        @pl.when(is_first)
        def _():
            comm.start(cins, couts, csem)

        body(*ins, *outs, *scr)

        @pl.when(is_last)
        def _():
            comm.finish(cins, couts, csem)

    res = pl.pallas_call(
        hosted, grid=grid, in_specs=list(in_specs) + [ANY] * ci, out_specs=list(out_specs) + [ANY] * co,
        out_shape=list(out_shape) + comm.out_shapes, scratch_shapes=list(scratch_shapes) + comm.scratch,
        input_output_aliases=aliases or {}, compiler_params=_cparams(*sem), name=name)(*args, *comm.ins)
    return tuple(res[:no]), tuple(res[no:])


def _run_comm(comm, name):
    def body(*refs):
        ci, co = len(comm.ins), len(comm.out_shapes)
        comm.start(refs[:ci], refs[ci:ci + co], refs[ci + co:])
        comm.finish(refs[:ci], refs[ci:ci + co], refs[ci + co:])

    return pl.pallas_call(body, in_specs=[ANY] * len(comm.ins), out_specs=[ANY] * len(comm.out_shapes),
                          out_shape=comm.out_shapes, scratch_shapes=comm.scratch, name=name)(*comm.ins)


def _gather_comm(shards, layer):
    na = len(shards)
    dma = pltpu.SemaphoreType.DMA

    def ici(ins, outs, sems, pos, a, j, m, sender):
        ssem, rsem = sems[0], sems[1]
        peer = _peer(pos, m)
        block = outs[a].at[_chip(pos) if sender else _chip(peer)]
        return _rcopy(ins[a].at[layer], block, ssem.at[a, j], rsem.at[a, j], peer)

    def start(ins, outs, sems):
        pos = _mesh_pos()

        @pl.when(pos[2] == layer)
        def _():
            for a in range(na):
                for j, m in enumerate(CHIP_MASKS):
                    ici(ins, outs, sems, pos, a, j, m, True).start()

    def finish(ins, outs, sems):
        fsend, frecv = sems[2], sems[3]
        pos = _mesh_pos()
        sib = _peer(pos, SIBLING)

        def forward(a, j, m):
            blk = outs[a].at[_chip(_peer(pos, m))]
            return _rcopy(blk, blk, fsend.at[a, j], frecv.at[a, j], sib)

        @pl.when(pos[2] == layer)
        def _():
            for a in range(na):
                for j, m in enumerate(CHIP_MASKS):
                    ici(ins, outs, sems, pos, a, j, m, False).wait_recv()
                    forward(a, j, m).start()
            for a in range(na):
                for j, m in enumerate(CHIP_MASKS):
                    ici(ins, outs, sems, pos, a, j, m, True).wait_send()
                    forward(a, j, m).wait_send()

        @pl.when(pos[2] != layer)
        def _():
            for a in range(na):
                for j, m in enumerate(CHIP_MASKS):
                    forward(a, j, m).wait_recv()

    return _Comm(shards, [S((N_CHIPS,) + s.shape[1:], s.dtype) for s in shards],
                 [dma((na, 3)), dma((na, 3)), dma((na, 3)), dma((na, 3))], start, finish)


def _scatter_comm(arrs, layer):
    na = len(arrs)
    dma = pltpu.SemaphoreType.DMA

    def copy(ins, outs, sems, pos, a, j, m, sender):
        peer = _peer(pos, m)
        src = ins[a].at[_chip(peer) if sender else _chip(pos)]
        dst = outs[a].at[_chip(pos) if sender else _chip(peer)]
        return _rcopy(src, dst, sems[0].at[a, j], sems[1].at[a, j], peer)

    def start(ins, outs, sems):
        pos = _mesh_pos()

        @pl.when(pos[2] == layer)
        def _():
            for a in range(na):
                for j, m in enumerate(CHIP_MASKS):
                    copy(ins, outs, sems, pos, a, j, m, True).start()

    def finish(ins, outs, sems):
        pos = _mesh_pos()

        @pl.when(pos[2] == layer)
        def _():
            for a in range(na):
                for j, m in enumerate(CHIP_MASKS):
                    copy(ins, outs, sems, pos, a, j, m, False).wait_recv()
            for a in range(na):
                for j, m in enumerate(CHIP_MASKS):
                    copy(ins, outs, sems, pos, a, j, m, True).wait_send()

    return _Comm(arrs, [S(s.shape, s.dtype) for s in arrs], [dma((na, 3)), dma((na, 3))], start, finish)


def _swap_layer(arrs, layer, tag):
    na = len(arrs)

    def body(*refs):
        ins, outs = refs[:na], refs[na:2 * na]
        ssem, rsem = refs[2 * na:]
        pos = _mesh_pos()
        sib = _peer(pos, SIBLING)
        cps = [_rcopy(ins[a], outs[a], ssem.at[a], rsem.at[a], sib) for a in range(na)]

        @pl.when(pos[2] != layer)
        def _():
            for cp in cps:
                cp.start()
            for cp in cps:
                cp.wait_send()

        @pl.when(pos[2] == layer)
        def _():
            for cp in cps:
                cp.wait_recv()

    dma = pltpu.SemaphoreType.DMA
    return pl.pallas_call(
        body, in_specs=[ANY] * na, out_specs=[ANY] * na, out_shape=[S(s.shape, s.dtype) for s in arrs],
        scratch_shapes=[dma((na,)), dma((na,))], name=f"swap_layer{tag}")(*arrs)


def _share_sibling(arrs):
    na = len(arrs)

    def body(*refs):
        bufs = refs[na:2 * na]
        ssem, rsem = refs[2 * na:]
        pos = _mesh_pos()
        c = pos[2]
        sib = _peer(pos, SIBLING)
        sends = [_rcopy(bufs[a].at[c], bufs[a].at[c], ssem.at[a], rsem.at[a], sib) for a in range(na)]
        for cp in sends:
            cp.start()
        for a in range(na):
            _rcopy(bufs[a].at[c], bufs[a].at[1 - c], ssem.at[a], rsem.at[a], sib).wait_recv()
        for cp in sends:
            cp.wait_send()

    dma = pltpu.SemaphoreType.DMA
    return pl.pallas_call(
        body, in_specs=[ANY] * na, out_specs=[ANY] * na, out_shape=[S(s.shape, s.dtype) for s in arrs],
        input_output_aliases={a: a for a in range(na)},
        scratch_shapes=[dma((na,)), dma((na,))], name="share_sibling")(*arrs)


def _small_allreduce(vec):
    r = vec.shape[0]

    def body(v_ref, sum_ref, all_ref, ssem, rsem):
        pos = _mesh_pos()
        me = _lin(pos)
        all_ref[me] = v_ref[...]
        cps = [_rcopy(v_ref, all_ref.at[me], ssem.at[m - 1], rsem.at[m - 1], _peer(pos, m)) for m in range(1, N_DEV)]
        for cp in cps:
            cp.start()
        for m in range(1, N_DEV):
            src = _peer(pos, m)
            _rcopy(v_ref, all_ref.at[_lin(src)], ssem.at[m - 1], rsem.at[m - 1], src).wait_recv()
        acc = all_ref[0]
        for d in range(1, N_DEV):
            acc = acc + all_ref[d]
        sum_ref[...] = acc
        for cp in cps:
            cp.wait_send()

    dma = pltpu.SemaphoreType.DMA
    return pl.pallas_call(
        body, in_specs=[VMEM], out_specs=[VMEM, VMEM],
        out_shape=[S((r, 128), F32), S((N_DEV, r, 128), F32)],
        scratch_shapes=[dma((N_DEV - 1,)), dma((N_DEV - 1,))],
        compiler_params=pltpu.CompilerParams(vmem_limit_bytes=VMEM_LIMIT_V7X), name="small_allreduce")(vec)


ROW_BYTES_TARGET = 1 << 20


def _row_tile(rows, cols, itemsize=4, align=16, target=ROW_BYTES_TARGET):
    fits = [t for t in range(align, rows + 1, align) if rows % t == 0]
    under = [t for t in fits if t * cols * itemsize <= target]
    return max(under) if under else (min(fits) if fits else rows)


def _pair_add(g, other):
    r, cdim = g.shape
    tr = _row_tile(r, cdim, 1)

    def body(g_ref, o_ref, out_ref):
        out_ref[...] = (g_ref[...].astype(F32) + o_ref[...].astype(F32)).astype(out_ref.dtype)

    blk = pl.BlockSpec((tr, cdim), lambda i: (i, 0))
    return pl.pallas_call(body, grid=(r // tr,), in_specs=[blk, blk], out_specs=blk, out_shape=S((r, cdim), BF),
                          compiler_params=_cparams("parallel"), name="pair_add")(g, other)


def _sum4(parts, pairs, chip_core):
    _, r, cdim = parts[0].shape
    tr = _row_tile(r, cdim)

    def body(kc_ref, q0_ref, q1_ref, own0_ref, own1_ref, out_ref):
        first = kc_ref[1] == 0
        mine = jnp.where(first, own0_ref[...], own1_ref[...]).astype(F32)
        terms = [jnp.where(kc_ref[0] == j, mine, jnp.where(first, q0_ref[j], q1_ref[j]).astype(F32))
                 for j in range(N_CHIPS)]
        out_ref[...] = ((terms[0] + terms[1]) + terms[2]) + terms[3]

    allc = pl.BlockSpec((N_CHIPS, tr, cdim), lambda i, kc: (0, i, 0))
    own = pl.BlockSpec((None, tr, cdim), lambda i, kc: (kc[0], i, 0))
    return pl.pallas_call(
        body,
        grid_spec=pltpu.PrefetchScalarGridSpec(
            num_scalar_prefetch=1, grid=(r // tr,), in_specs=[allc, allc, own, own],
            out_specs=pl.BlockSpec((None, tr, cdim), lambda i, kc: (kc[1], i, 0))),
        out_shape=S((DEPTH, r, cdim), F32), compiler_params=_cparams("parallel"), name="sum4")(
            chip_core, parts[0], parts[1], pairs[0], pairs[1])


def _ada_w_grad(silu_c, dmod_cols, chip):
    def body(k_ref, sc_ref, dm_ref, o_ref):
        del k_ref
        o_ref[...] = lax.dot_general(sc_ref[...], dm_ref[...], TN, precision=HIGHEST, preferred_element_type=F32)

    return pl.pallas_call(
        body,
        grid_spec=pltpu.PrefetchScalarGridSpec(
            num_scalar_prefetch=1, grid=(4,),
            in_specs=[pl.BlockSpec((N_DEV, D), lambda j, k: (0, 0)),
                      pl.BlockSpec((None, N_DEV, ADA_COLS), lambda j, k: (4 * j + k[0], 0, 0))],
            out_specs=pl.BlockSpec((None, D, ADA_COLS), lambda j, k: (j, 0, 0))),
        out_shape=S((4, D, ADA_COLS), F32), compiler_params=_cparams("parallel"), name="ada_w_grad")(chip, silu_c, dmod_cols)


def _adamw(w, g, m, v, *, name):
    r = w.shape[0]
    rest = w.shape[1:]
    row_elems = 1
    for d in rest:
        row_elems *= d
    tr = _row_tile(r, row_elems, align=8 if len(rest) == 1 else 1, target=5 * ROW_BYTES_TARGET // 2)
    c1 = 1.0 / (1.0 - ADAM_B1 ** ADAM_STEP)
    c2 = 1.0 / (1.0 - ADAM_B2 ** ADAM_STEP)

    def body(w_ref, g_ref, m_ref, v_ref, d_ref, mo_ref, vo_ref):
        gv = g_ref[...]
        mn = ADAM_B1 * m_ref[...] + (1.0 - ADAM_B1) * gv
        vn = ADAM_B2 * v_ref[...] + (1.0 - ADAM_B2) * (gv * gv)
        mo_ref[...] = mn
        vo_ref[...] = vn
        d_ref[...] = -ADAM_LR * ((mn * c1) / (jnp.sqrt(vn * c2) + ADAM_EPS) + ADAM_WD * w_ref[...])

    zeros = (0,) * len(rest)
    blk = pl.BlockSpec((tr,) + rest, lambda i: (i,) + zeros)
    return pl.pallas_call(
        body, grid=(r // tr,), in_specs=[blk] * 4, out_specs=[blk] * 3, out_shape=[S(w.shape, F32)] * 3,
        compiler_params=_cparams("parallel"), name=name)(w, g, m, v)


def _adamw_many(ws, gs, ms, vs):
    n = len(ws)
    c1 = 1.0 / (1.0 - ADAM_B1 ** ADAM_STEP)
    c2 = 1.0 / (1.0 - ADAM_B2 ** ADAM_STEP)

    def body(*refs):
        for i in range(n):
            w_ref, g_ref, m_ref, v_ref, d_ref, mo_ref, vo_ref = (refs[k * n + i] for k in range(7))
            gv = g_ref[...]
            mn = ADAM_B1 * m_ref[...] + (1.0 - ADAM_B1) * gv
            vn = ADAM_B2 * v_ref[...] + (1.0 - ADAM_B2) * (gv * gv)
            mo_ref[...] = mn
            vo_ref[...] = vn
            d_ref[...] = -ADAM_LR * ((mn * c1) / (jnp.sqrt(vn * c2) + ADAM_EPS) + ADAM_WD * w_ref[...])

    shapes = [S(a.shape, F32) for a in ws]
    outs = pl.pallas_call(
        body, in_specs=[VMEM] * (4 * n), out_specs=[VMEM] * (3 * n), out_shape=shapes * 3,
        compiler_params=pltpu.CompilerParams(vmem_limit_bytes=VMEM_LIMIT_V7X), name="adamw_small")(*ws, *gs, *ms, *vs)
    return outs[0:n], outs[n:2 * n], outs[2 * n:3 * n]


WEIGHTS = ['ada_mix_w', 'ada_mix_b', 'norm_mix_g', 'w_in', 'b_gate', 'ssd_conv_w', 'ssd_conv_b', 'ssd_dt_bias',
           'ssd_a_log', 'ssd_d', 'ssd_norm_g', 'w_ssd_out', 'conf_conv_w', 'conf_conv_b', 'conf_ln_g', 'conf_ln_b',
           'w_conf_out', 'sc_conv_w', 'w_sc_out', 'w_o', 'ada_ffn_w', 'ada_ffn_b', 'norm_ffn_g', 'w_up', 'ffn_conv_w',
           'ffn_conv_b', 'w_down', 'final_norm_g']
SMALL = ['ada_mix_b', 'norm_mix_g', 'b_gate', 'ssd_conv_b', 'ssd_dt_bias', 'ssd_a_log', 'ssd_d', 'ssd_norm_g', 'conf_conv_b',
         'conf_ln_g', 'conf_ln_b', 'ada_ffn_b', 'norm_ffn_g', 'ffn_conv_b']
CONVS = ['ssd_conv_w', 'conf_conv_w', 'sc_conv_w', 'ffn_conv_w']
BIG = ['ada_mix_w', 'ada_ffn_w', 'w_in', 'w_up', 'w_conf_out', 'w_sc_out', 'w_ssd_out', 'w_o', 'w_down']


def _pack_rows(pieces):
    flat = [p.reshape(-1) for p in pieces]
    offs, o = [], 0
    for f in flat:
        offs.append(o)
        o += f.shape[0]
    total = -(-o // 1024) * 1024
    vec = jnp.concatenate(flat + [jnp.zeros((total - o,), F32)])
    return vec.reshape(total // 128, 128), offs


def _by_chip(a, axis):
    shp = a.shape
    a = a.reshape(shp[:axis] + (N_CHIPS, shp[axis] // N_CHIPS) + shp[axis + 1:])
    return jnp.moveaxis(a, axis, 0)


def _from_chips(a, axis):
    a = jnp.moveaxis(a, 0, axis)
    shp = a.shape
    return a.reshape(shp[:axis] + (shp[axis] * shp[axis + 1],) + shp[axis + 2:])


def kernel(x, c, ada_mix_w, ada_mix_b, norm_mix_g, w_in, b_gate, ssd_conv_w, ssd_conv_b, ssd_dt_bias, ssd_a_log, ssd_d, ssd_norm_g, w_ssd_out, conf_conv_w, conf_conv_b, conf_ln_g, conf_ln_b, w_conf_out, sc_conv_w, w_sc_out, w_o, ada_ffn_w, ada_ffn_b, norm_ffn_g, w_up, ffn_conv_w, ffn_conv_b, w_down, final_norm_g, loss_target, m_ada_mix_w, m_ada_mix_b, m_norm_mix_g, m_w_in, m_b_gate, m_ssd_conv_w, m_ssd_conv_b, m_ssd_dt_bias, m_ssd_a_log, m_ssd_d, m_ssd_norm_g, m_w_ssd_out, m_conf_conv_w, m_conf_conv_b, m_conf_ln_g, m_conf_ln_b, m_w_conf_out, m_sc_conv_w, m_w_sc_out, m_w_o, m_ada_ffn_w, m_ada_ffn_b, m_norm_ffn_g, m_w_up, m_ffn_conv_w, m_ffn_conv_b, m_w_down, m_final_norm_g, v_ada_mix_w, v_ada_mix_b, v_norm_mix_g, v_w_in, v_b_gate, v_ssd_conv_w, v_ssd_conv_b, v_ssd_dt_bias, v_ssd_a_log, v_ssd_d, v_ssd_norm_g, v_w_ssd_out, v_conf_conv_w, v_conf_conv_b, v_conf_ln_g, v_conf_ln_b, v_w_conf_out, v_sc_conv_w, v_w_sc_out, v_w_o, v_ada_ffn_w, v_ada_ffn_b, v_norm_ffn_g, v_w_up, v_ffn_conv_w, v_ffn_conv_b, v_w_down, v_final_norm_g):
    args = locals()
    w = {n: args[n] for n in WEIGHTS}
    mom = {n: args["m_" + n] for n in WEIGHTS}
    var = {n: args["v_" + n] for n in WEIGHTS}
    pos = _mesh_pos()
    chip = _chip(pos)
    core = pos[2]

    conv_pack = jnp.zeros((DEPTH, CONV_ROWS, CONV_COLS), F32)
    for n, (r0, taps, width) in CONV_PACK.items():
        conv_pack = conv_pack.at[:, r0:r0 + taps, 0:width].set(w[n])
    c_blk = jnp.pad(c, ((0, 7), (0, 0)))
    mods_raw, silu_c, conv_all = _ada_exchange(c_blk, ada_mix_w, ada_ffn_w, conv_pack)
    ada_b = jnp.concatenate([ada_mix_b, ada_ffn_b], axis=0)
    mod_all = mods_raw.reshape(N_CHIPS, 4, ADA_COLS).transpose(1, 0, 2).reshape(4, 3 * D) + ada_b
    mod_all = mod_all.reshape(4, 3, D)
    mods = [(mod_all[i], mod_all[2 + i]) for i in range(DEPTH)]
    conv_full = {n: _from_chips(conv_all[:, :, r0:r0 + taps, 0:width], 2) for n, (r0, taps, width) in CONV_PACK.items()}

    cast = lambda a: a.astype(BF)
    shards = [cast(w_in), jnp.concatenate([cast(w_conf_out), cast(w_sc_out)], axis=1),
              jnp.concatenate([cast(w_ssd_out), cast(w_o)], axis=1), cast(w_up), cast(w_down)]
    EARLY, LATE = (0, 1, 2), (3, 4)
    pick = lambda arrs, idx: [arrs[k] for k in idx]

    def own_block(l, gathered, idx):
        return [lax.dynamic_update_slice(g, shards[k][l][None], (chip, 0, 0)) for g, k in zip(gathered, idx)]

    def early_params(l, gathered):
        g_in, g_cs, g_row = own_block(l, gathered, EARLY)
        p = {n: w[n][l] for n in SMALL if not n.startswith("ada_")}
        p.update({n: conv_full[n][l] for n in CONVS})
        p["w_in_pad"] = _pad_w_in_chips(g_in)
        p["w_conf_out"] = _from_chips(g_cs[:, 0:CONF_W], 1)
        p["w_sc_out"] = _from_chips(g_cs[:, CONF_W:], 1)
        p["w_ssd_out"] = _from_chips(g_row[:, 0:INNER // N_CHIPS], 0)
        p["w_o"] = _from_chips(g_row[:, INNER // N_CHIPS:], 0)
        return p

    def late_params(l, gathered):
        g_up, g_down = own_block(l, gathered, LATE)
        return {"w_up4": g_up, "w_down": _from_chips(g_down, 0)}

    def big_grads(g, idx):
        makers = (lambda: _unpad_w_in_chips(g["w_in_pad"]),
                  lambda: _by_chip(jnp.concatenate([g["w_conf_out"], g["w_sc_out"]], axis=0), 1),
                  lambda: jnp.concatenate([_by_chip(g["w_ssd_out"], 0), _by_chip(g["w_o"], 0)], axis=1),
                  lambda: g["w_up4"], lambda: _by_chip(g["w_down"], 0))
        return [makers[k]().astype(BF) for k in idx]

    def pair_sums(big, l, tag):
        theirs = _swap_layer(big, l, tag)
        out = []
        for g4, t4 in zip(big, theirs):
            k, r, cdim = g4.shape
            out.append(_pair_add(g4.reshape(k * r, cdim), t4.reshape(k * r, cdim)).reshape(k, r, cdim))
        return out

    seq = x.shape[1]
    xs = x.reshape(seq, D)
    p0 = early_params(0, _run_comm(_gather_comm(pick(shards, EARLY), 0), "gather_weights_l0"))
    xs, sv0, gathered1 = _layer_fwd(xs, p0, mods[0][0], mods[0][1], "_l0",
                                    comm_ssd=_gather_comm(pick(shards, LATE), 0),
                                    late_params=lambda got: late_params(0, got),
                                    comm_up=_gather_comm(pick(shards, EARLY), 1),
                                    comm=_gather_comm(pick(shards, LATE), 1))
    p1 = early_params(1, gathered1[:len(EARLY)])
    p1.update(late_params(1, gathered1[len(EARLY):]))
    xs, sv1, _ = _layer_fwd(xs, p1, mods[1][0], mods[1][1], "_l1")
    dx, loss, dfg = _final_loss(xs, _row(final_norm_g), loss_target.reshape(seq, D), name="final_loss")
    dfinal = dfg[0]

    every = EARLY + LATE
    grads, dmods = [None, None], [None, None]
    dx, grads[1], dmm, dmf, _ = _layer_bwd(dx, p1, sv1, mods[1][0], mods[1][1], "_l1")
    dmods[1] = (dmm, dmf)
    pair1 = pair_sums(big_grads(grads[1], every), 1, "_l1")
    late0 = {}

    def scatter_late0(g):
        late0["pair"] = pair_sums(big_grads(g, LATE), 0, "_l0_ffn")
        return _scatter_comm(late0["pair"], 0)

    dx, grads[0], dmm, dmf, (parts1, parts0_late) = _layer_bwd(
        dx, p0, sv0, mods[0][0], mods[0][1], "_l0", comm=_scatter_comm(pair1, 1), make_comm_conf=scatter_late0)
    dmods[0] = (dmm, dmf)
    pair0_early = pair_sums(big_grads(grads[0], EARLY), 0, "_l0")
    parts0_early = _run_comm(_scatter_comm(pair0_early, 0), "scatter_chips_l0")
    pair0 = list(pair0_early) + list(late0["pair"])
    parts0 = list(parts0_early) + list(parts0_late)
    chip_core = jnp.stack([chip, core]).astype(jnp.int32)
    red = _share_sibling([_sum4((q0, q1), (o0, o1), chip_core)
                          for q0, q1, o0, o1 in zip(parts0, parts1, pair0, pair1)])
    reduced = {"w_in": red[0], "cs": red[1], "row": red[2], "w_up": red[3], "w_down": red[4]}
    gw = {"w_in": reduced["w_in"], "w_up": reduced["w_up"], "w_down": reduced["w_down"],
          "w_conf_out": reduced["cs"][:, 0:CONF_W], "w_sc_out": reduced["cs"][:, CONF_W:],
          "w_ssd_out": reduced["row"][:, 0:INNER // N_CHIPS], "w_o": reduced["row"][:, INNER // N_CHIPS:]}

    dmod = jnp.stack([dmods[0][0], dmods[1][0], dmods[0][1], dmods[1][1]])
    small_local = {n: jnp.stack([grads[l][n] for l in range(DEPTH)]) for n in SMALL if not n.startswith("ada_")}
    pieces = [loss[0]] + [small_local[n] for n in SMALL if not n.startswith("ada_")]
    pieces += [jnp.stack([grads[l][n] for l in range(DEPTH)]) for n in CONVS] + [dfinal, dmod]
    vec, offs = _pack_rows(pieces)
    vsum, vall = _small_allreduce(vec)
    flat = vsum.reshape(-1)

    def piece(k, like):
        return flat[offs[k]:offs[k] + like.size].reshape(like.shape)

    loss_out = flat[0]
    k = 1
    for n in SMALL:
        if not n.startswith("ada_"):
            gw[n] = piece(k, small_local[n])
            k += 1
    for n in CONVS:
        full = piece(k, conv_full[n])
        width = CONV_PACK[n][2]
        gw[n] = lax.dynamic_slice_in_dim(full, chip * width, width, axis=2)
        k += 1
    gw["final_norm_g"] = piece(k, dfinal)
    k += 1
    dmod_sum = piece(k, dmod)
    gw["ada_mix_b"], gw["ada_ffn_b"] = dmod_sum[0:2], dmod_sum[2:4]
    dmod_all = vall.reshape(N_DEV, -1)[:, offs[k]:offs[k] + dmod.size]
    dmod_cols = dmod_all.reshape(N_DEV, 4 * N_CHIPS, ADA_COLS).transpose(1, 0, 2)
    ada_g = _ada_w_grad(silu_c, dmod_cols, jnp.reshape(chip, (1,)).astype(jnp.int32))
    gw["ada_mix_w"], gw["ada_ffn_w"] = ada_g[0:2], ada_g[2:4]

    delta, new_m, new_v = {}, {}, {}
    for n in BIG:
        shp = w[n].shape
        if shp[2] % 128:
            two_d = lambda a: jnp.transpose(a, (2, 0, 1))
            back = lambda a: jnp.transpose(a, (1, 2, 0))
        else:
            two_d = lambda a: a.reshape(shp[0] * shp[1], shp[2])
            back = lambda a: a.reshape(shp)
        d_, m_, v_ = _adamw(two_d(w[n]), two_d(gw[n]), two_d(mom[n]), two_d(var[n]), name=f"adamw_{n}")
        delta[n], new_m[n], new_v[n] = back(d_), back(m_), back(v_)
    rest = [n for n in WEIGHTS if n not in BIG]
    two_d = lambda a: a.reshape(1, -1) if a.ndim == 1 else a
    outs = _adamw_many(*[[two_d(src[n]) for n in rest] for src in (w, gw, mom, var)])
    for dst, group in zip((delta, new_m, new_v), outs):
        for n, o in zip(rest, group):
            dst[n] = o.reshape(w[n].shape)

    return (loss_out, dx[None], *[gw[n] for n in WEIGHTS], *[delta[n] for n in WEIGHTS],
            *[new_m[n] for n in WEIGHTS], *[new_v[n] for n in WEIGHTS])
```

```python
import functools

import jax
import jax.numpy as jnp
from jax import lax
from jax.experimental import pallas as pl
from jax.experimental.pallas import tpu as pltpu

F32 = jnp.float32
BF = jnp.bfloat16
S = jax.ShapeDtypeStruct

D = 1024
HEADS = 16
HEAD_DIM = 64
INNER = HEADS * HEAD_DIM
GROUPS = 2
NSTATE = 64
Q = 128
SSD_K = 4
XBC = INNER + 2 * GROUPS * NSTATE
CONF_W = 512
CONF_K = 31
SC_W = 512
SC_K = 3
DFF = 2816
FFN_K = 3
EPS = 1e-6
DEPTH = 2
R_Z, R_XBC, R_DT, R_CONF, R_SC, R_GATES, N_IN = 0, 1024, 2304, 2320, 3344, 4880, 7952
P_GATES, P_SC, P_XBC, P_Z, P_CONF, PW = 0, 3072, 4608, 6144, 7168, 8192
XBC_PAD = 1536
DT_PAD = 128
P_DT = P_XBC + XBC
N_CHIPS = 4
N_DEV = 8

ADAM_LR, ADAM_B1, ADAM_B2, ADAM_EPS, ADAM_WD, ADAM_STEP = 0.001, 0.9, 0.999, 1e-08, 0.01, 10

VMEM_LIMIT_V7X = 56 * 1024 * 1024
HIGHEST = lax.Precision.HIGHEST


def _cparams(*sem):
    return pltpu.CompilerParams(dimension_semantics=sem, vmem_limit_bytes=VMEM_LIMIT_V7X)


def _full(shape):
    n = len(shape)
    return pl.BlockSpec(shape, lambda *_: (0,) * n)


def _rows(tm, w, cb=0):
    return pl.BlockSpec((tm, w), lambda i: (i, cb))


def _prev_rows(tm, halo, w, cb=0):
    r = tm // halo
    return pl.BlockSpec((halo, w), lambda i: (jnp.maximum(i * r - 1, 0), cb))


def _next_rows(tm, halo, w, nrows, cb=0):
    r = tm // halo
    last = nrows // halo - 1
    return pl.BlockSpec((halo, w), lambda i: (jnp.minimum((i + 1) * r, last), cb))


def _sigmoid(v):
    return 1.0 / (1.0 + jnp.exp(-v))


def _softplus(v):
    return jnp.maximum(v, 0.0) + jnp.log(1.0 + jnp.exp(-jnp.abs(v)))


def _colsum(v):
    return jnp.sum(v, axis=0, keepdims=True)


def _tile(n, want):
    t = min(n, want)
    assert n % t == 0, (n, want)
    return t


NN = (((1,), (0,)), ((), ()))
NT = (((1,), (1,)), ((), ()))
TN = (((0,), (0,)), ((), ()))


def _dot(a, b, dims=NN):
    return lax.dot_general(a, b, dims, preferred_element_type=F32)


def _mm(a, b, *, dims, grid, a_spec, b_spec, o_spec, out_shape, acc_shape, name, comm=None):
    nk = grid[2]

    def body(a_ref, b_ref, o_ref, acc_ref):
        k = pl.program_id(2)
        part = _dot(a_ref[...], b_ref[...], dims)
        if nk == 1:
            o_ref[...] = part.astype(o_ref.dtype)
        else:
            @pl.when(k == 0)
            def _():
                acc_ref[...] = part

            @pl.when(k > 0)
            def _():
                acc_ref[...] += part

            @pl.when(k == nk - 1)
            def _():
                o_ref[...] = acc_ref[...].astype(o_ref.dtype)

    scratch = [pltpu.VMEM(acc_shape if nk > 1 else (8, 128), F32)]
    if comm is not None:
        (out,), comm_out = _call_with_comm(body, tuple(grid), in_specs=[a_spec, b_spec], out_specs=[o_spec],
                                           out_shape=[out_shape], scratch_shapes=scratch, args=(a, b), comm=comm, name=name)
        return out, comm_out
    return pl.pallas_call(
        body, grid=grid, in_specs=[a_spec, b_spec], out_specs=o_spec, out_shape=out_shape, scratch_shapes=scratch,
        compiler_params=_cparams("parallel", "parallel", "arbitrary"), name=name)(a, b)


def _mm_nn(a, b, *, out_dtype, tm, tn, name, comm=None):
    m, k = a.shape
    n = b.shape[1]
    tm, tn = _tile(m, tm), _tile(n, tn)
    return _mm(a, b, dims=NN, grid=(m // tm, n // tn, 1),
               a_spec=pl.BlockSpec((tm, k), lambda i, j, kk: (i, 0)),
               b_spec=pl.BlockSpec((k, tn), lambda i, j, kk: (0, j)),
               o_spec=pl.BlockSpec((tm, tn), lambda i, j, kk: (i, j)),
               out_shape=S((m, n), out_dtype), acc_shape=(tm, tn), name=name, comm=comm)


def _mm_nt(a, b, *, out_dtype, tm, tk, name):
    m, kc = a.shape
    n = b.shape[0]
    tm, tk = _tile(m, tm), _tile(kc, tk)
    return _mm(a, b, dims=NT, grid=(m // tm, 1, kc // tk),
               a_spec=pl.BlockSpec((tm, tk), lambda i, j, kk: (i, kk)),
               b_spec=pl.BlockSpec((n, tk), lambda i, j, kk: (0, kk)),
               o_spec=pl.BlockSpec((tm, n), lambda i, j, kk: (i, 0)),
               out_shape=S((m, n), out_dtype), acc_shape=(tm, n), name=name)


def _mm_tn(a, b, *, tn, tk, name, out_dtype=BF, by_chip=False):
    kc, m = a.shape
    n = b.shape[1]
    tn, tk = _tile(n, tn), _tile(kc, tk)
    if by_chip:
        assert n == N_CHIPS * tn
        o_spec, out_shape = pl.BlockSpec((None, m, tn), lambda i, j, kk: (j, 0, 0)), S((N_CHIPS, m, tn), out_dtype)
    else:
        o_spec, out_shape = pl.BlockSpec((m, tn), lambda i, j, kk: (0, j)), S((m, n), out_dtype)
    return _mm(a, b, dims=TN, grid=(1, n // tn, kc // tk),
               a_spec=pl.BlockSpec((tk, m), lambda i, j, kk: (kk, 0)),
               b_spec=pl.BlockSpec((tk, tn), lambda i, j, kk: (kk, j)),
               o_spec=o_spec, out_shape=out_shape, acc_shape=(m, tn), name=name)


def _mm_nn_chips(a, b4, *, out_dtype, tm, name, comm=None):
    m, k = a.shape
    n4 = b4.shape[2]
    tm = _tile(m, tm)
    return _mm(a, b4, dims=NN, grid=(m // tm, N_CHIPS, 1),
               a_spec=pl.BlockSpec((tm, k), lambda i, j, kk: (i, 0)),
               b_spec=pl.BlockSpec((None, k, n4), lambda i, j, kk: (j, 0, 0)),
               o_spec=pl.BlockSpec((tm, n4), lambda i, j, kk: (i, j)),
               out_shape=S((m, N_CHIPS * n4), out_dtype), acc_shape=(tm, n4), name=name, comm=comm)


def _mm_nt_chips(a, b4, *, out_dtype, tm, name):
    m = a.shape[0]
    n, n4 = b4.shape[1], b4.shape[2]
    tm = _tile(m, tm)
    return _mm(a, b4, dims=NT, grid=(m // tm, 1, N_CHIPS),
               a_spec=pl.BlockSpec((tm, n4), lambda i, j, kk: (i, kk)),
               b_spec=pl.BlockSpec((None, n, n4), lambda i, j, kk: (kk, 0, 0)),
               o_spec=pl.BlockSpec((tm, n), lambda i, j, kk: (i, 0)),
               out_shape=S((m, n), out_dtype), acc_shape=(tm, n), name=name)


def _mm_resid(a, b, x, mod, *, tm, name):
    m, k = a.shape
    n = b.shape[1]
    tm = _tile(m, tm)

    def body(a_ref, b_ref, x_ref, mod_ref, o_ref, xn_ref):
        o = _dot(a_ref[...], b_ref[...])
        o_ref[...] = o.astype(o_ref.dtype)
        xn_ref[...] = x_ref[...] + mod_ref[2:3, :] * o

    return pl.pallas_call(
        body, grid=(m // tm,),
        in_specs=[_rows(tm, k), _full((k, n)), _rows(tm, n), _full((3, n))],
        out_specs=[_rows(tm, n), _rows(tm, n)],
        out_shape=[S((m, n), BF), S((m, n), F32)],
        compiler_params=_cparams("parallel"), name=name)(a, b, x, mod)


def _modnorm_fwd(x, gain, mod, *, name):
    n = x.shape[0]
    tm = _tile(n, 512)

    def body(x_ref, g_ref, mod_ref, h_ref):
        xv = x_ref[...]
        r = lax.rsqrt(jnp.mean(xv * xv, axis=-1, keepdims=True) + EPS)
        y = xv * r * g_ref[...]
        h_ref[...] = (y * (1.0 + mod_ref[1:2, :]) + mod_ref[0:1, :]).astype(h_ref.dtype)

    return pl.pallas_call(
        body, grid=(n // tm,), in_specs=[_rows(tm, D), _full((1, D)), _full((3, D))],
        out_specs=_rows(tm, D), out_shape=S((n, D), BF), compiler_params=_cparams("parallel"), name=name)(x, gain, mod)


def _modnorm_bwd(dh, x, dres, gain, mod, *, name):
    n = x.shape[0]
    tm = _tile(n, 512)

    def body(dh_ref, x_ref, dres_ref, g_ref, mod_ref, dx_ref, dg_ref, dsh_ref, dsc_ref):
        i = pl.program_id(0)
        xv = x_ref[...]
        r = lax.rsqrt(jnp.mean(xv * xv, axis=-1, keepdims=True) + EPS)
        xh = xv * r
        dhv = dh_ref[...]
        g = g_ref[...]
        dy = dhv * (1.0 + mod_ref[1:2, :])
        dxh = dy * g
        dx = r * (dxh - xh * jnp.mean(dxh * xh, axis=-1, keepdims=True))
        dx_ref[...] = dres_ref[...] + dx

        @pl.when(i == 0)
        def _():
            dg_ref[...] = jnp.zeros_like(dg_ref)
            dsh_ref[...] = jnp.zeros_like(dsh_ref)
            dsc_ref[...] = jnp.zeros_like(dsc_ref)

        dg_ref[...] += _colsum(dy * xh)
        dsh_ref[...] += _colsum(dhv)
        dsc_ref[...] += _colsum(dhv * xh * g)

    vec = S((1, D), F32)
    return pl.pallas_call(
        body, grid=(n // tm,),
        in_specs=[_rows(tm, D), _rows(tm, D), _rows(tm, D), _full((1, D)), _full((3, D))],
        out_specs=[_rows(tm, D), _full((1, D)), _full((1, D)), _full((1, D))],
        out_shape=[S((n, D), F32), vec, vec, vec],
        compiler_params=_cparams("arbitrary"), name=name)(dh, x, dres, gain, mod)


def _final_loss(x, gain, target, *, name):
    n = x.shape[0]
    tm = _tile(n, 512)

    def body(x_ref, g_ref, t_ref, dx_ref, loss_ref, dg_ref):
        i = pl.program_id(0)
        xv = x_ref[...]
        g = g_ref[...]
        r = lax.rsqrt(jnp.mean(xv * xv, axis=-1, keepdims=True) + EPS)
        xh = xv * r
        err = xh * g - t_ref[...]
        dy = err * (1.0 / D)
        dxh = dy * g
        dx_ref[...] = r * (dxh - xh * jnp.mean(dxh * xh, axis=-1, keepdims=True))

        @pl.when(i == 0)
        def _():
            loss_ref[...] = jnp.zeros_like(loss_ref)
            dg_ref[...] = jnp.zeros_like(dg_ref)

        part = _colsum(jnp.sum(err * err, axis=-1, keepdims=True)) * (0.5 / D)
        loss_ref[...] += jnp.broadcast_to(part, loss_ref.shape)
        dg_ref[...] += _colsum(dy * xh)

    return pl.pallas_call(
        body, grid=(n // tm,),
        in_specs=[_rows(tm, D), _full((1, D)), _rows(tm, D)],
        out_specs=[_rows(tm, D), _full((1, 128)), _full((1, D))],
        out_shape=[S((n, D), F32), S((1, 128), F32), S((1, D), F32)],
        compiler_params=_cparams("arbitrary"), name=name)(x, gain, target)


def _gate_bwd(dx, o, mod, *, name):
    n = dx.shape[0]
    tm = _tile(n, 512)

    def body(dx_ref, o_ref, mod_ref, do_ref, dgt_ref):
        i = pl.program_id(0)
        dxv = dx_ref[...]
        do_ref[...] = (dxv * mod_ref[2:3, :]).astype(do_ref.dtype)

        @pl.when(i == 0)
        def _():
            dgt_ref[...] = jnp.zeros_like(dgt_ref)

        dgt_ref[...] += _colsum(dxv * o_ref[...].astype(F32))

    return pl.pallas_call(
        body, grid=(n // tm,), in_specs=[_rows(tm, D), _rows(tm, D), _full((3, D))],
        out_specs=[_rows(tm, D), _full((1, D))], out_shape=[S((n, D), BF), S((1, D), F32)],
        compiler_params=_cparams("arbitrary"), name=name)(dx, o, mod)


def _conv(buf, w_ref, taps, start, rows, ch):
    acc = None
    for k in range(taps):
        term = buf[pl.ds(start - (taps - 1) + k, rows), 0:ch] * w_ref[k:k + 1, :]
        acc = term if acc is None else acc + term
    return acc


def _conv_t(buf, w_ref, taps, start, rows, ch):
    acc = None
    for k in range(taps):
        term = buf[pl.ds(start + (taps - 1) - k, rows), 0:ch] * w_ref[k:k + 1, :]
        acc = term if acc is None else acc + term
    return acc


def _conv_dw(dw_ref, dy, xbuf, taps, xstart, rows, ch):
    for k in range(taps):
        dw_ref[k:k + 1, :] += _colsum(dy * xbuf[pl.ds(xstart - (taps - 1) + k, rows), 0:ch])


HALO = 16
CONF_HALO = 32
CHUNK = 32
STRIP = 256


def _blocks8(v):
    return [v[8 * i:8 * (i + 1)] for i in range(v.shape[0] // 8)]


def _delay_rows(blocks, s):
    sub = lax.broadcasted_iota(jnp.int32, blocks[0].shape, 0)
    rolled = [pltpu.roll(b, s, 0) for b in blocks]
    return [jnp.where(sub < s, rolled[i - 1], rolled[i]) for i in range(1, len(blocks))]


def _advance_rows(blocks, s):
    sub = lax.broadcasted_iota(jnp.int32, blocks[0].shape, 0)
    rolled = [pltpu.roll(b, 8 - s, 0) for b in blocks]
    return [jnp.where(sub < 8 - s, rolled[i], rolled[i + 1]) for i in range(len(blocks) - 1)]


def _conv3_chunk(tail, xv, wk):
    blocks = [tail] + _blocks8(xv)
    x1 = jnp.concatenate(_delay_rows(blocks, 1), axis=0)
    x2 = jnp.concatenate(_delay_rows(blocks, 2), axis=0)
    return wk[0] * x2 + wk[1] * x1 + wk[2] * xv


def _conv_chunk(tail, xv, wk):
    taps = len(wk)
    blocks = [tail] + _blocks8(xv)
    acc = wk[taps - 1] * xv
    for d in range(1, taps):
        acc = acc + wk[taps - 1 - d] * jnp.concatenate(_delay_rows(blocks, d), axis=0)
    return acc


def _ssd_pre_fwd(proj, w, b, *, name):
    n = proj.shape[0]
    tm = _tile(n, 512)
    cb = P_XBC // XBC_PAD

    def body(prev_ref, cur_ref, w_ref, b_ref, o_ref, c_ref, buf):
        i = pl.program_id(0)
        buf[0:HALO, :] = jnp.where(i == 0, 0.0, prev_ref[:, 0:XBC].astype(F32))
        buf[HALO:HALO + tm, :] = cur_ref[:, 0:XBC].astype(F32)
        c = _conv(buf, w_ref, SSD_K, HALO, tm, XBC) + b_ref[...]
        c_ref[...] = c.astype(c_ref.dtype)
        o_ref[...] = (c * _sigmoid(c)).astype(o_ref.dtype)

    return pl.pallas_call(
        body, grid=(n // tm,),
        in_specs=[_prev_rows(tm, HALO, XBC_PAD, cb), _rows(tm, XBC_PAD, cb), _full((SSD_K, XBC)), _full((1, XBC))],
        out_specs=[_rows(tm, XBC), _rows(tm, XBC)], out_shape=[S((n, XBC), BF), S((n, XBC), BF)],
        scratch_shapes=[pltpu.VMEM((HALO + tm, XBC), F32)],
        compiler_params=_cparams("parallel"), name=name)(proj, proj, w, b)


def _ssd_pre_bwd(proj, cpre, dact, ddt, dproj, w, *, name):
    n = proj.shape[0]
    tm = _tile(n, 512)
    nt = n // tm
    cb = P_XBC // XBC_PAD

    def body(x_ref, cc_ref, cn_ref, dc_ref, dn_ref, ddt_ref, w_ref, dproj_in, o_ref, dw_ref, db_ref, dbuf, acc):
        del dproj_in
        i = pl.program_id(0)
        last = i == nt - 1

        @pl.when(i == 0)
        def _():
            acc[...] = jnp.zeros_like(acc)

        def silu_bwd(cv, dav):
            sg = _sigmoid(cv)
            return dav * (sg * (1.0 + cv * (1.0 - sg)))

        for s in range(XBC // STRIP):
            c = pl.ds(s * STRIP, STRIP)
            wk = [w_ref[k:k + 1, c] for k in range(SSD_K)]

            def step1(j, carry):
                rows = pl.ds(pl.multiple_of(j * CHUNK, CHUNK), CHUNK)
                dbuf[rows, c] = silu_bwd(cc_ref[rows, c].astype(F32), dc_ref[rows, c].astype(F32))
                return carry

            lax.fori_loop(0, tm // CHUNK, step1, 0, unroll=2)
            dbuf[tm:tm + HALO, c] = silu_bwd(cn_ref[:, c].astype(F32), jnp.where(last, 0.0, dn_ref[:, c].astype(F32)))

            def step2(j, carry):
                r0 = pl.multiple_of(j * CHUNK, CHUNK)
                rows = pl.ds(r0, CHUNK)
                win = dbuf[pl.ds(r0, CHUNK + 8), c]
                blocks = _blocks8(win)
                xv = x_ref[rows, c].astype(F32)
                d0 = win[0:CHUNK]
                dx = wk[SSD_K - 1] * d0
                acc[SSD_K - 1, :, c] += d0 * xv
                acc[SSD_K, :, c] += d0
                for adv in range(1, SSD_K):
                    dk = jnp.concatenate(_advance_rows(blocks, adv), axis=0)
                    dx = dx + wk[SSD_K - 1 - adv] * dk
                    acc[SSD_K - 1 - adv, :, c] += dk * xv
                o_ref[rows, c] = dx.astype(o_ref.dtype)
                return carry

            lax.fori_loop(0, tm // CHUNK, step2, 0)

        o_ref[:, XBC:XBC + DT_PAD] = ddt_ref[...]
        o_ref[:, XBC + DT_PAD:XBC_PAD] = jnp.zeros((tm, XBC_PAD - XBC - DT_PAD), o_ref.dtype)

        @pl.when(last)
        def _():
            for k in range(SSD_K):
                dw_ref[k:k + 1, :] = _colsum(acc[k])
            db_ref[...] = _colsum(acc[SSD_K])

    return pl.pallas_call(
        body, grid=(nt,),
        in_specs=[_rows(tm, XBC_PAD, cb), _rows(tm, XBC), _next_rows(tm, HALO, XBC, n),
                  _rows(tm, XBC), _next_rows(tm, HALO, XBC, n), _rows(tm, DT_PAD),
                  _full((SSD_K, XBC)), pl.BlockSpec(memory_space=pl.ANY)],
        out_specs=[_rows(tm, XBC_PAD, cb), _full((SSD_K, XBC)), _full((1, XBC))],
        out_shape=[S(dproj.shape, dproj.dtype), S((SSD_K, XBC), F32), S((1, XBC), F32)],
        scratch_shapes=[pltpu.VMEM((tm + HALO, XBC), F32), pltpu.VMEM((SSD_K + 1, CHUNK, XBC), F32)],
        input_output_aliases={7: 0},
        compiler_params=_cparams("arbitrary"), name=name)(proj, cpre, cpre, dact, dact, ddt, w, dproj)


def _sc_fwd(proj, w, *, name):
    n = proj.shape[0]
    tm = _tile(n, 512)
    cb = P_SC // (3 * SC_W)

    def body(prev_ref, cur_ref, w_ref, o_ref, buf):
        i = pl.program_id(0)
        pv = prev_ref[...].astype(F32)
        cv = cur_ref[...].astype(F32)
        buf[0:HALO, :] = jnp.where(i == 0, 0.0, pv[:, SC_W:2 * SC_W] * pv[:, 2 * SC_W:])
        buf[HALO:HALO + tm, :] = cv[:, SC_W:2 * SC_W] * cv[:, 2 * SC_W:]
        q = _conv(buf, w_ref, SC_K, HALO, tm, SC_W)
        o_ref[...] = (cv[:, 0:SC_W] * q).astype(o_ref.dtype)

    return pl.pallas_call(
        body, grid=(n // tm,),
        in_specs=[_prev_rows(tm, HALO, 3 * SC_W, cb), _rows(tm, 3 * SC_W, cb), _full((SC_K, SC_W))],
        out_specs=_rows(tm, SC_W), out_shape=S((n, SC_W), BF),
        scratch_shapes=[pltpu.VMEM((HALO + tm, SC_W), F32)],
        compiler_params=_cparams("parallel"), name=name)(proj, proj, w)


def _sc_bwd(proj, da, dproj, w, *, name):
    n = proj.shape[0]
    tm = _tile(n, 512)
    nt = n // tm
    cb = P_SC // (3 * SC_W)

    def body(xp_ref, xc_ref, xn_ref, dc_ref, dn_ref, w_ref, dproj_in, o_ref, dw_ref, pbuf, dbuf):
        del dproj_in
        i = pl.program_id(0)
        pv = xp_ref[...].astype(F32)
        cv = xc_ref[...].astype(F32)
        nv = xn_ref[...].astype(F32)
        gb, gc, xv = cv[:, 0:SC_W], cv[:, SC_W:2 * SC_W], cv[:, 2 * SC_W:]
        pbuf[0:HALO, :] = jnp.where(i == 0, 0.0, pv[:, SC_W:2 * SC_W] * pv[:, 2 * SC_W:])
        pbuf[HALO:HALO + tm, :] = gc * xv
        q = _conv(pbuf, w_ref, SC_K, HALO, tm, SC_W)
        dav = dc_ref[...].astype(F32)
        dbuf[0:tm, :] = dav * gb
        dbuf[tm:tm + HALO, :] = jnp.where(i == nt - 1, 0.0, dn_ref[...].astype(F32) * nv[:, 0:SC_W])
        dp = _conv_t(dbuf, w_ref, SC_K, 0, tm, SC_W)
        o_ref[:, 0:SC_W] = (dav * q).astype(o_ref.dtype)
        o_ref[:, SC_W:2 * SC_W] = (dp * xv).astype(o_ref.dtype)
        o_ref[:, 2 * SC_W:] = (dp * gc).astype(o_ref.dtype)

        @pl.when(i == 0)
        def _():
            dw_ref[...] = jnp.zeros_like(dw_ref)

        _conv_dw(dw_ref, dbuf[0:tm, :], pbuf, SC_K, HALO, tm, SC_W)

    return pl.pallas_call(
        body, grid=(nt,),
        in_specs=[_prev_rows(tm, HALO, 3 * SC_W, cb), _rows(tm, 3 * SC_W, cb), _next_rows(tm, HALO, 3 * SC_W, n, cb),
                  _rows(tm, SC_W), _next_rows(tm, HALO, SC_W, n), _full((SC_K, SC_W)),
                  pl.BlockSpec(memory_space=pl.ANY)],
        out_specs=[_rows(tm, 3 * SC_W, cb), _full((SC_K, SC_W))],
        out_shape=[S(dproj.shape, dproj.dtype), S((SC_K, SC_W), F32)],
        scratch_shapes=[pltpu.VMEM((HALO + tm, SC_W), F32), pltpu.VMEM((tm + HALO, SC_W), F32)],
        input_output_aliases={6: 0},
        compiler_params=_cparams("arbitrary"), name=name)(proj, proj, proj, da, da, w, dproj)


CONF_ROWS = 32
CONF_PHASES = 8


def _fill_advanced(src, dst, nblk, ncol):
    for s in range(ncol // STRIP):
        c = pl.ds(s * STRIP, STRIP)
        sub = lax.broadcasted_iota(jnp.int32, (8, STRIP), 0)
        first = src[0:8, c]
        carry0 = tuple(pltpu.roll(first, 8 - b, 0) for b in range(1, CONF_PHASES))

        def step(i, prev):
            r0 = pl.multiple_of(i * 8, 8)
            blk = src[pl.ds(r0, 8), c]
            cur = []
            for b in range(1, CONF_PHASES):
                rolled = pltpu.roll(blk, 8 - b, 0)
                cur.append(rolled)
                dst[b - 1, pl.ds(r0 - 8, 8), c] = jnp.where(sub < 8 - b, prev[b - 1], rolled)
            return tuple(cur)

        lax.fori_loop(1, nblk, step, carry0)


def _conf_fwd(proj, w, b, ln_g, ln_b, *, name):
    n = proj.shape[0]
    tm = _tile(n, 512)
    cb = P_CONF // (2 * CONF_W)
    h = CONF_HALO

    def body(prev_ref, cur_ref, w_ref, b_ref, g_ref, be_ref, a_ref, uc_ref, buf):
        i = pl.program_id(0)
        pv = prev_ref[...].astype(F32)
        cv = cur_ref[...].astype(F32)
        buf[0:h, :] = jnp.where(i == 0, 0.0, pv[:, 0:CONF_W] * _sigmoid(pv[:, CONF_W:]))
        buf[h:h + tm, :] = cv[:, 0:CONF_W] * _sigmoid(cv[:, CONF_W:])
        uc = _conv(buf, w_ref, CONF_K, h, tm, CONF_W) + b_ref[...]
        uc_ref[...] = uc.astype(uc_ref.dtype)
        mu = jnp.mean(uc, axis=-1, keepdims=True)
        xc = uc - mu
        v = xc * lax.rsqrt(jnp.mean(xc * xc, axis=-1, keepdims=True) + EPS) * g_ref[...] + be_ref[...]
        a_ref[...] = (v * _sigmoid(v)).astype(a_ref.dtype)

    vec = _full((1, CONF_W))
    return pl.pallas_call(
        body, grid=(n // tm,),
        in_specs=[_prev_rows(tm, h, 2 * CONF_W, cb), _rows(tm, 2 * CONF_W, cb), _full((CONF_K, CONF_W)), vec, vec, vec],
        out_specs=[_rows(tm, CONF_W), _rows(tm, CONF_W)],
        out_shape=[S((n, CONF_W), BF), S((n, CONF_W), BF)],
        scratch_shapes=[pltpu.VMEM((h + tm, CONF_W), F32)],
        compiler_params=_cparams("parallel"), name=name)(proj, proj, w, b, ln_g, ln_b)


def _conf_bwd(proj, uc, da, dproj, w, ln_g, ln_b, *, name, comm=None):
    n = proj.shape[0]
    tm = _tile(n, 512)
    nt = n // tm
    cb = P_CONF // (2 * CONF_W)
    h = CONF_HALO

    def body(xp_ref, xc_ref, ucc_ref, ucn_ref, dac_ref, dan_ref, w_ref, g_ref, be_ref, dproj_in,
             o_ref, dw_ref, db_ref, dg_ref, dbe_ref, ubuf, dbuf, dsh, wb):
        del dproj_in
        i = pl.program_id(0)
        pv = xp_ref[...].astype(F32)
        cv = xc_ref[...].astype(F32)
        val, gt = cv[:, 0:CONF_W], cv[:, CONF_W:]
        sg = _sigmoid(gt)
        ubuf[0:h, :] = jnp.where(i == 0, 0.0, pv[:, 0:CONF_W] * _sigmoid(pv[:, CONF_W:]))
        ubuf[h:h + tm, :] = val * sg

        def ln_silu_bwd(ucv, dav):
            mu = jnp.mean(ucv, axis=-1, keepdims=True)
            xc = ucv - mu
            r = lax.rsqrt(jnp.mean(xc * xc, axis=-1, keepdims=True) + EPS)
            xh = xc * r
            v = xh * g_ref[...] + be_ref[...]
            s = _sigmoid(v)
            dv = dav * (s * (1.0 + v * (1.0 - s)))
            dxh = dv * g_ref[...]
            duc = r * (dxh - jnp.mean(dxh, axis=-1, keepdims=True) - xh * jnp.mean(dxh * xh, axis=-1, keepdims=True))
            return duc, dv, xh

        duc, dv, xh = ln_silu_bwd(ucc_ref[...].astype(F32), dac_ref[...].astype(F32))
        dbuf[0:tm, :] = duc
        ducn, _, _ = ln_silu_bwd(ucn_ref[...].astype(F32), dan_ref[...].astype(F32))
        dbuf[tm:tm + h, :] = jnp.where(i == nt - 1, 0.0, ducn)
        for k in range(CONF_K):
            wb[k] = jnp.broadcast_to(w_ref[k:k + 1, :], (8, CONF_W))
        _fill_advanced(dbuf, dsh, (tm + h) // 8, CONF_W)
        nb = CONF_ROWS // 8
        for s_ in range(CONF_W // STRIP):
            c = pl.ds(s_ * STRIP, STRIP)
            cg = pl.ds(CONF_W + s_ * STRIP, STRIP)

            def dconv(j, carry):
                r0 = pl.multiple_of(j * CONF_ROWS, CONF_ROWS)
                acc = [jnp.zeros((8, STRIP), F32) for _ in range(nb)]
                for t in range(CONF_K):
                    wv = wb[CONF_K - 1 - t, :, c]
                    for q in range(nb):
                        rows8 = pl.ds(r0 + 8 * q + 8 * (t // 8), 8)
                        dv = dbuf[rows8, c] if t % 8 == 0 else dsh[t % 8 - 1, rows8, c]
                        acc[q] = acc[q] + wv * dv
                du = jnp.concatenate(acc, axis=0)
                rows = pl.ds(r0, CONF_ROWS)
                vl = xc_ref[rows, c].astype(F32)
                sgv = _sigmoid(xc_ref[rows, cg].astype(F32))
                o_ref[rows, c] = (du * sgv).astype(o_ref.dtype)
                o_ref[rows, cg] = (du * vl * sgv * (1.0 - sgv)).astype(o_ref.dtype)
                return carry

            lax.fori_loop(0, tm // CONF_ROWS, dconv, 0)

        @pl.when(i == 0)
        def _():
            dw_ref[...] = jnp.zeros_like(dw_ref)
            db_ref[...] = jnp.zeros_like(db_ref)
            dg_ref[...] = jnp.zeros_like(dg_ref)
            dbe_ref[...] = jnp.zeros_like(dbe_ref)

        dg_ref[...] += _colsum(dv * xh)
        dbe_ref[...] += _colsum(dv)
        db_ref[...] += _colsum(duc)
        _conv_dw(dw_ref, duc, ubuf, CONF_K, h, tm, CONF_W)

    vec = _full((1, CONF_W))
    vshape = S((1, CONF_W), F32)
    return _call_with_comm(
        body, nt,
        in_specs=[_prev_rows(tm, h, 2 * CONF_W, cb), _rows(tm, 2 * CONF_W, cb),
                  _rows(tm, CONF_W), _next_rows(tm, h, CONF_W, n), _rows(tm, CONF_W), _next_rows(tm, h, CONF_W, n),
                  _full((CONF_K, CONF_W)), vec, vec, pl.BlockSpec(memory_space=pl.ANY)],
        out_specs=[_rows(tm, 2 * CONF_W, cb), _full((CONF_K, CONF_W)), vec, vec, vec],
        out_shape=[S(dproj.shape, dproj.dtype), S((CONF_K, CONF_W), F32), vshape, vshape, vshape],
        scratch_shapes=[pltpu.VMEM((h + tm, CONF_W), F32), pltpu.VMEM((tm + h, CONF_W), F32),
                        pltpu.VMEM((CONF_PHASES - 1, tm + h, CONF_W), F32), pltpu.VMEM((CONF_K, 8, CONF_W), F32)],
        args=(proj, proj, uc, uc, da, da, w, ln_g, ln_b, dproj), comm=comm, name=name, aliases={9: 0})


def _ffn_act_fwd(up, w, b, *, name, comm=None):
    n = up.shape[0]
    tm = _tile(n, 512)
    c2 = 2 * DFF

    def body(prev_ref, cur_ref, w_ref, b_ref, o_ref, u_ref):
        first = pl.program_id(0) == 0
        for s in range(DFF // STRIP):
            cols = (pl.ds(s * STRIP, STRIP), pl.ds(DFF + s * STRIP, STRIP))
            wk = [[w_ref[k:k + 1, c] for k in range(FFN_K)] for c in cols]
            bk = [b_ref[:, c] for c in cols]
            tails = tuple(jnp.where(first, 0.0, prev_ref[:, c].astype(F32)[HALO - 8:HALO]) for c in cols)

            def step(j, tails):
                r0 = pl.multiple_of(j * CHUNK, CHUNK)
                us, new_tails = [], []
                for h in range(2):
                    xv = cur_ref[pl.ds(r0, CHUNK), cols[h]].astype(F32)
                    us.append(_conv3_chunk(tails[h], xv, wk[h]) + bk[h])
                    u_ref[pl.ds(r0, CHUNK), cols[h]] = us[h].astype(u_ref.dtype)
                    new_tails.append(xv[CHUNK - 8:CHUNK])
                o_ref[pl.ds(r0, CHUNK), cols[0]] = (us[0] * _sigmoid(us[0]) * us[1]).astype(o_ref.dtype)
                return tuple(new_tails)

            lax.fori_loop(0, tm // CHUNK, step, tails, unroll=2)

    return _call_with_comm(
        body, n // tm,
        in_specs=[_prev_rows(tm, HALO, c2), _rows(tm, c2), _full((FFN_K, c2)), _full((1, c2))],
        out_specs=[_rows(tm, DFF), _rows(tm, c2)], out_shape=[S((n, DFF), BF), S((n, c2), BF)],
        scratch_shapes=[], args=(up, up, w, b), comm=comm, name=name)


def _ffn_act_bwd(up, u, dact, w, *, name, comm=None):
    n = up.shape[0]
    tm = _tile(n, 512)
    nt = n // tm
    c2 = 2 * DFF

    def body(x_ref, uc_ref, un_ref, dc_ref, dn_ref, w_ref, o_ref, dw_ref, db_ref, dbuf, acc):
        i = pl.program_id(0)
        last = i == nt - 1

        @pl.when(i == 0)
        def _():
            acc[...] = jnp.zeros_like(acc)

        def swiglu_bwd(gate, val, dav):
            sg = _sigmoid(gate)
            return dav * val * (sg * (1.0 + gate * (1.0 - sg))), dav * gate * sg

        for s in range(DFF // STRIP):
            cols = (pl.ds(s * STRIP, STRIP), pl.ds(DFF + s * STRIP, STRIP))
            wk = [[w_ref[k:k + 1, c] for k in range(FFN_K)] for c in cols]

            def step1(j, carry):
                r0 = pl.multiple_of(j * CHUNK, CHUNK)
                rows = pl.ds(r0, CHUNK)
                dus = swiglu_bwd(uc_ref[rows, cols[0]].astype(F32), uc_ref[rows, cols[1]].astype(F32),
                                 dc_ref[rows, cols[0]].astype(F32))
                for h in range(2):
                    dbuf[rows, cols[h]] = dus[h]
                return carry

            lax.fori_loop(0, tm // CHUNK, step1, 0, unroll=2)
            dus = swiglu_bwd(un_ref[:, cols[0]].astype(F32), un_ref[:, cols[1]].astype(F32),
                             jnp.where(last, 0.0, dn_ref[:, cols[0]].astype(F32)))
            for h in range(2):
                dbuf[tm:tm + HALO, cols[h]] = dus[h]

            def step2(j, carry):
                r0 = pl.multiple_of(j * CHUNK, CHUNK)
                rows = pl.ds(r0, CHUNK)
                for h in range(2):
                    win = dbuf[pl.ds(r0, CHUNK + 8), cols[h]]
                    blocks = _blocks8(win)
                    d0 = win[0:CHUNK]
                    d1 = jnp.concatenate(_advance_rows(blocks, 1), axis=0)
                    d2 = jnp.concatenate(_advance_rows(blocks, 2), axis=0)
                    o_ref[rows, cols[h]] = (wk[h][2] * d0 + wk[h][1] * d1 + wk[h][0] * d2).astype(o_ref.dtype)
                    xv = x_ref[rows, cols[h]].astype(F32)
                    acc[2, :, cols[h]] += d0 * xv
                    acc[1, :, cols[h]] += d1 * xv
                    acc[0, :, cols[h]] += d2 * xv
                    acc[FFN_K, :, cols[h]] += d0
                return carry

            lax.fori_loop(0, tm // CHUNK, step2, 0)

        @pl.when(last)
        def _():
            for k in range(FFN_K):
                dw_ref[k:k + 1, :] = _colsum(acc[k])
            db_ref[...] = _colsum(acc[FFN_K])

    return _call_with_comm(
        body, nt,
        in_specs=[_rows(tm, c2), _rows(tm, c2), _next_rows(tm, HALO, c2, n),
                  _rows(tm, DFF), _next_rows(tm, HALO, DFF, n), _full((FFN_K, c2))],
        out_specs=[_rows(tm, c2), _full((FFN_K, c2)), _full((1, c2))],
        out_shape=[S((n, c2), BF), S((FFN_K, c2), F32), S((1, c2), F32)],
        scratch_shapes=[pltpu.VMEM((tm + HALO, c2), F32), pltpu.VMEM((FFN_K + 1, CHUNK, c2), F32)],
        args=(up, u, u, dact, dact, w), comm=comm, name=name)


def _head_consts():
    lane = jnp.arange(INNER) // HEAD_DIM
    rep = (jnp.arange(128)[:, None] == lane[None, :]).astype(BF)
    return rep, rep.T


def _split_dot(v, m):
    hi = v.astype(BF)
    lo = (v - hi.astype(F32)).astype(BF)
    return _dot(hi, m) + _dot(lo, m)


def _chunk_decay_terms(dt_raw, dtb, alog, rep):
    row = lax.broadcasted_iota(jnp.int32, (Q, Q), 0)
    col = lax.broadcasted_iota(jnp.int32, (Q, Q), 1)
    lower = row >= col
    upper = col >= row
    dt = _softplus(dt_raw + dtb)
    a = -jnp.exp(alog)
    adt = dt * a
    acum = lax.dot_general(lower.astype(F32), adt, NN, precision=HIGHEST, preferred_element_type=F32)
    acum_t = lax.dot_general(adt, upper.astype(F32), TN, precision=HIGHEST, preferred_element_type=F32)
    alast = acum[Q - 1:Q, :]
    e = jnp.exp(acum)
    f = jnp.exp(alast - acum)
    ex = _split_dot(jnp.concatenate([dt, e, f, jnp.broadcast_to(jnp.exp(alast), (8, 128))], axis=0), rep)
    return dict(lower=lower, upper=upper, dt=dt, a=a, acum=acum, acum_t=acum_t, alast=alast,
                dt_x=ex[0:Q], e_x=ex[Q:2 * Q], f_x=ex[2 * Q:3 * Q], cd_x=ex[3 * Q:3 * Q + 1])


def _block_diag2(v, lo):
    return jnp.concatenate([jnp.where(lo, v, 0.0), jnp.where(lo, 0.0, v)], axis=0).astype(BF)


def _ssd_fwd(xbc_act, proj, dt_bias, a_log, d_x, norm_g, *, name, comm=None):
    n = xbc_act.shape[0]
    nc = n // Q
    rep, _ = _head_consts()

    def body(xs_ref, bc_ref, dt_ref, z_ref, dtb_ref, alog_ref, dx_ref, ng_ref, rep_ref, y_ref, yn_ref, hp_ref,
             h_scr, y_scr):
        i = pl.program_id(0)

        @pl.when(i == 0)
        def _():
            h_scr[...] = jnp.zeros_like(h_scr)

        hp_ref[...] = h_scr[...]
        t = _chunk_decay_terms(dt_ref[...].astype(F32), dtb_ref[...], alog_ref[...], rep_ref[...])
        xs = xs_ref[...].astype(F32)
        xt = xs * t["dt_x"]
        lo = lax.broadcasted_iota(jnp.int32, (Q, 128), 1) < HEAD_DIM
        gw = INNER // GROUPS
        for g in range(GROUPS):
            bm = bc_ref[:, g * NSTATE:(g + 1) * NSTATE]
            cm = bc_ref[:, GROUPS * NSTATE + g * NSTATE:GROUPS * NSTATE + (g + 1) * NSTATE]
            cb = _dot(cm, bm, NT)
            hg = h_scr[:, g * gw:(g + 1) * gw]
            yoff = _dot(cm, hg.astype(BF))
            for jj in range(gw // 128):
                p = g * (gw // 128) + jj
                sl = slice(p * 128, (p + 1) * 128)
                ws = []
                for hd in (2 * p, 2 * p + 1):
                    seg = t["acum"][:, hd:hd + 1] - t["acum_t"][hd:hd + 1, :]
                    ws.append((cb * jnp.exp(jnp.where(t["lower"], seg, -jnp.inf))).astype(BF))
                ydiag = _dot(jnp.concatenate(ws, axis=1), _block_diag2(xt[:, sl], lo))
                y_scr[:, sl] = ydiag + yoff[:, jj * 128:(jj + 1) * 128] * t["e_x"][:, sl] + dx_ref[:, sl] * xs[:, sl]
            xf = (xt[:, g * gw:(g + 1) * gw] * t["f_x"][:, g * gw:(g + 1) * gw]).astype(BF)
            h_scr[:, g * gw:(g + 1) * gw] = hg * t["cd_x"][:, g * gw:(g + 1) * gw] + _dot(bm, xf, TN)
        y = y_scr[...]
        y_ref[...] = y.astype(y_ref.dtype)
        z = z_ref[...].astype(F32)
        v = y * z * _sigmoid(z)
        for g in range(GROUPS):
            vg = v[:, g * gw:(g + 1) * gw]
            r = lax.rsqrt(jnp.mean(vg * vg, axis=-1, keepdims=True) + EPS)
            yn_ref[:, g * gw:(g + 1) * gw] = (vg * r * ng_ref[:, g * gw:(g + 1) * gw]).astype(yn_ref.dtype)

    vec = _full((1, INNER))
    hv = _full((1, 128))
    return _call_with_comm(
        body, nc,
        in_specs=[_rows(Q, INNER, 0), _rows(Q, 2 * GROUPS * NSTATE, INNER // (2 * GROUPS * NSTATE)),
                  _rows(Q, DT_PAD, P_DT // DT_PAD), _rows(Q, INNER, P_Z // INNER),
                  hv, hv, vec, vec, _full((128, INNER))],
        out_specs=[_rows(Q, INNER), _rows(Q, INNER), pl.BlockSpec((None, NSTATE, INNER), lambda i: (i, 0, 0))],
        out_shape=[S((n, INNER), BF), S((n, INNER), BF), S((nc, NSTATE, INNER), F32)],
        scratch_shapes=[pltpu.VMEM((NSTATE, INNER), F32), pltpu.VMEM((Q, INNER), F32)],
        args=(xbc_act, xbc_act, proj, proj, dt_bias, a_log, d_x, norm_g, rep), comm=comm, name=name)


def _ssd_bwd(xbc_act, proj, y, dyn, hprev, dproj, dt_bias, a_log, d_x, norm_g, *, name):
    n = xbc_act.shape[0]
    nc = n // Q
    rep, sel = _head_consts()
    gw = INNER // GROUPS

    def rev(w, cb=0):
        return pl.BlockSpec((Q, w), lambda i: (nc - 1 - i, cb))

    def body(xs_ref, bc_ref, dt_ref, z_ref, y_ref, dyn_ref, hp_ref, dtb_ref, alog_ref, dx_ref, ng_ref, rep_ref,
             sel_ref, dproj_in, dz_ref, ddt_ref, dxbc_ref, dng_ref, ddtb_ref, dalog_ref, dd_ref,
             dh_scr, dxt_scr, st_scr, off_scr, rs_scr, cs_scr, dng_acc, ddtb_acc, da_acc, dd_acc):
        del dproj_in
        i = pl.program_id(0)

        @pl.when(i == 0)
        def _():
            for r in (dh_scr, dng_acc, ddtb_acc, da_acc, dd_acc):
                r[...] = jnp.zeros_like(r)

        y = y_ref[...].astype(F32)
        z = z_ref[...].astype(F32)
        sz = _sigmoid(z)
        silu = z * sz
        v = y * silu
        dyn = dyn_ref[...].astype(F32)
        dvs = []
        for g in range(GROUPS):
            gs = slice(g * gw, (g + 1) * gw)
            vg = v[:, gs]
            r = lax.rsqrt(jnp.mean(vg * vg, axis=-1, keepdims=True) + EPS)
            vn = vg * r
            dvn = dyn[:, gs] * ng_ref[:, gs]
            dng_acc[:, gs] += _colsum(dyn[:, gs] * vn)
            dvs.append(r * (dvn - vn * jnp.mean(dvn * vn, axis=-1, keepdims=True)))
        dv = jnp.concatenate(dvs, axis=1)
        dy = dv * silu
        dz_ref[...] = (dv * y * (sz * (1.0 + z * (1.0 - sz)))).astype(dz_ref.dtype)

        dt_raw = dt_ref[...].astype(F32)
        t = _chunk_decay_terms(dt_raw, dtb_ref[...], alog_ref[...], rep_ref[...])
        xs = xs_ref[...].astype(F32)
        dsk = dx_ref[...]
        dd_acc[...] += _colsum(dy * xs)
        xt = xs * t["dt_x"]
        dye = dy * t["e_x"]
        xtf = xt * t["f_x"]
        hp = hp_ref[...]
        dh = dh_scr[...]
        lo = lax.broadcasted_iota(jnp.int32, (Q, 128), 1) < HEAD_DIM
        rs_scr[...] = jnp.zeros_like(rs_scr)
        cs_scr[...] = jnp.zeros_like(cs_scr)
        for g in range(GROUPS):
            gs = slice(g * gw, (g + 1) * gw)
            bm = bc_ref[:, g * NSTATE:(g + 1) * NSTATE]
            cm = bc_ref[:, GROUPS * NSTATE + g * NSTATE:GROUPS * NSTATE + (g + 1) * NSTATE]
            cbt = _dot(bm, cm, NT)
            dhg = dh[:, gs].astype(BF)
            hpg = hp[:, gs].astype(BF)
            dxt_state = _dot(bm, dhg) * t["f_x"][:, gs]
            st_scr[:, gs] = dxt_state
            dye_g = dye[:, gs]
            off_scr[:, gs] = dye_g * _dot(cm, hpg)
            dye_b = dye_g.astype(BF)
            db = _dot(xtf[:, gs].astype(BF), dhg, NT)
            dc = _dot(dye_b, hpg, NT)
            dh_scr[:, gs] = t["cd_x"][:, gs] * dh[:, gs] + _dot(cm, dye_b, TN)
            dcbt = jnp.zeros((Q, Q), F32)
            for jj in range(gw // 128):
                p = g * (gw // 128) + jj
                sl = slice(p * 128, (p + 1) * 128)
                lts, wfs = [], []
                for hd in (2 * p, 2 * p + 1):
                    seg_t = t["acum_t"][hd:hd + 1, :] - t["acum"][:, hd:hd + 1]
                    lt = jnp.exp(jnp.where(t["upper"], seg_t, -jnp.inf))
                    lts.append(lt)
                    wfs.append(cbt * lt)
                dyp = dy[:, sl]
                dxt_diag = _dot(jnp.concatenate([w.astype(BF) for w in wfs], axis=1), _block_diag2(dyp, lo))
                dwt2 = _dot(_block_diag2(xt[:, sl], lo), dyp.astype(BF), NT)
                for k, hd in enumerate((2 * p, 2 * p + 1)):
                    dwt = dwt2[k * Q:(k + 1) * Q]
                    dcbt = dcbt + dwt * lts[k]
                    mt = dwt * wfs[k]
                    rs_scr[hd:hd + 1, :] = _colsum(mt)
                    cs_scr[:, hd:hd + 1] = jnp.sum(mt, axis=1, keepdims=True)
                dxt_scr[:, sl] = dxt_diag + dxt_state[:, jj * 128:(jj + 1) * 128]
            dcbt_b = dcbt.astype(BF)
            db = db + _dot(dcbt_b, cm)
            dc = dc + _dot(dcbt_b, bm, TN)
            dxbc_ref[:, INNER + g * NSTATE:INNER + (g + 1) * NSTATE] = db.astype(dxbc_ref.dtype)
            dxbc_ref[:, INNER + (GROUPS + g) * NSTATE:INNER + (GROUPS + g + 1) * NSTATE] = dc.astype(dxbc_ref.dtype)
        dxt = dxt_scr[...]
        dst = st_scr[...]
        sel_m = sel_ref[...]
        sums = _split_dot(jnp.concatenate([off_scr[...], xs * dst, xs * dxt], axis=0), sel_m)
        r1_off, r3_state, r3 = sums[0:Q], sums[Q:2 * Q], sums[2 * Q:3 * Q]
        t1 = _colsum(xt * dst)
        t2 = _colsum(dh * hp)
        tails = _split_dot(jnp.concatenate([jnp.broadcast_to(t1, (8, INNER)), jnp.broadcast_to(t2, (8, INNER))], axis=0),
                           sel_m)
        extra = tails[0:1] + jnp.exp(t["alast"]) * tails[8:9]
        last_row = lax.broadcasted_iota(jnp.int32, (Q, 128), 0) == Q - 1
        da_cum = (rs_scr[...].T - cs_scr[...]) + r1_off - t["dt"] * r3_state + jnp.where(last_row, extra, 0.0)
        dadt = lax.dot_general(t["upper"].astype(F32), da_cum, NN, precision=HIGHEST, preferred_element_type=F32)
        ddt = r3 + t["a"] * dadt
        da_acc[...] += _colsum(dadt * t["dt"])
        real = lax.broadcasted_iota(jnp.int32, (Q, 128), 1) < HEADS
        ddraw = jnp.where(real, ddt * _sigmoid(dt_raw + dtb_ref[...]), 0.0)
        ddt_ref[...] = ddraw.astype(ddt_ref.dtype)
        ddtb_acc[...] += _colsum(ddraw)
        dxbc_ref[:, 0:INNER] = (dy * dsk + dxt * t["dt_x"]).astype(dxbc_ref.dtype)

        @pl.when(i == nc - 1)
        def _():
            dng_ref[...] = dng_acc[...]
            ddtb_ref[...] = ddtb_acc[...]
            dalog_ref[...] = da_acc[...] * t["a"]
            dd_ref[...] = _split_dot(jnp.broadcast_to(dd_acc[...], (8, INNER)), sel_m)[0:1]

    vec = _full((1, INNER))
    hv = _full((1, 128))
    return pl.pallas_call(
        body, grid=(nc,),
        in_specs=[rev(INNER, 0), rev(2 * GROUPS * NSTATE, INNER // (2 * GROUPS * NSTATE)),
                  rev(DT_PAD, P_DT // DT_PAD), rev(INNER, P_Z // INNER), rev(INNER), rev(INNER),
                  pl.BlockSpec((None, NSTATE, INNER), lambda i: (nc - 1 - i, 0, 0)),
                  hv, hv, vec, vec, _full((128, INNER)), _full((INNER, 128)), pl.BlockSpec(memory_space=pl.ANY)],
        out_specs=[rev(INNER, P_Z // INNER), rev(DT_PAD), rev(XBC), vec, hv, hv, hv],
        out_shape=[S(dproj.shape, dproj.dtype), S((n, DT_PAD), BF), S((n, XBC), BF),
                   S((1, INNER), F32), S((1, 128), F32), S((1, 128), F32), S((1, 128), F32)],
        scratch_shapes=[pltpu.VMEM((NSTATE, INNER), F32), pltpu.VMEM((Q, INNER), F32), pltpu.VMEM((Q, INNER), F32),
                        pltpu.VMEM((Q, INNER), F32), pltpu.VMEM((128, Q), F32), pltpu.VMEM((Q, 128), F32),
                        pltpu.VMEM((1, INNER), F32), pltpu.VMEM((1, 128), F32), pltpu.VMEM((1, 128), F32),
                        pltpu.VMEM((1, INNER), F32)],
        input_output_aliases={13: 0},
        compiler_params=_cparams("arbitrary"), name=name)(
            xbc_act, xbc_act, proj, proj, y, dyn, hprev, dt_bias, a_log, d_x, norm_g, rep, sel, dproj)


def _mixer_out_fwd(yn, a_conf, a_sc, proj, b_gate, w_ssd, w_conf, w_sc, w_o, x, mod, *, name):
    n = x.shape[0]
    tm = _tile(n, 256)

    def body(yn_ref, ac_ref, as_ref, gt_ref, bg_ref, wa_ref, wb_ref, wc_ref, wo_ref, x_ref, mod_ref,
             ya_ref, yb_ref, yc_ref, mg_ref, mix_ref, xn_ref):
        ya = _dot(yn_ref[...], wa_ref[...])
        yb = _dot(ac_ref[...], wb_ref[...])
        yc = _dot(as_ref[...], wc_ref[...])
        ya_ref[...] = ya.astype(ya_ref.dtype)
        yb_ref[...] = yb.astype(yb_ref.dtype)
        yc_ref[...] = yc.astype(yc_ref.dtype)
        g = _sigmoid(gt_ref[...].astype(F32) + bg_ref[...])
        merged = (g[:, 0:D] * ya + g[:, D:2 * D] * yb + g[:, 2 * D:] * yc).astype(mg_ref.dtype)
        mg_ref[...] = merged
        mix = _dot(merged, wo_ref[...])
        mix_ref[...] = mix.astype(mix_ref.dtype)
        xn_ref[...] = x_ref[...] + mod_ref[2:3, :] * mix

    act = S((n, D), BF)
    return pl.pallas_call(
        body, grid=(n // tm,),
        in_specs=[_rows(tm, INNER), _rows(tm, CONF_W), _rows(tm, SC_W), _rows(tm, 3 * D, P_GATES // (3 * D)),
                  _full((1, 3 * D)), _full((INNER, D)), _full((CONF_W, D)), _full((SC_W, D)), _full((D, D)),
                  _rows(tm, D), _full((3, D))],
        out_specs=[_rows(tm, D)] * 6,
        out_shape=[act, act, act, act, act, S((n, D), F32)],
        compiler_params=_cparams("parallel"), name=name)(yn, a_conf, a_sc, proj, b_gate, w_ssd, w_conf, w_sc, w_o, x, mod)


def _mixer_out_bwd(dx, mix, ya, yb, yc, proj, b_gate, w_ssd, w_conf, w_sc, w_o, mod, *, name):
    n = dx.shape[0]
    tm = _tile(n, 256)

    def body(dx_ref, mix_ref, ya_ref, yb_ref, yc_ref, gt_ref, bg_ref, wa_ref, wb_ref, wc_ref, wo_ref, mod_ref,
             do_ref, dya_ref, dyb_ref, dyc_ref, dgt_ref, dyn_ref, dac_ref, das_ref, dgm_ref, dbg_ref, dm_scr, acc):
        i = pl.program_id(0)
        last = i == pl.num_programs(0) - 1

        @pl.when(i == 0)
        def _():
            acc[...] = jnp.zeros_like(acc)

        do_ref[...] = (dx_ref[...] * mod_ref[2:3, :]).astype(BF)
        dm_scr[...] = _dot(do_ref[...], wo_ref[...], NT)
        branches = ((ya_ref, dya_ref), (yb_ref, dyb_ref), (yc_ref, dyc_ref))
        for s_ in range(D // STRIP):
            c = pl.ds(s_ * STRIP, STRIP)

            def step(j, carry):
                rows = pl.ds(pl.multiple_of(j * CHUNK, CHUNK), CHUNK)
                dmv = dm_scr[rows, c]
                acc[3, :, c] += dx_ref[rows, c] * mix_ref[rows, c].astype(F32)
                for jb, (y_ref, o_ref) in enumerate(branches):
                    cj = pl.ds(jb * D + s_ * STRIP, STRIP)
                    gj = _sigmoid(gt_ref[rows, cj].astype(F32) + bg_ref[:, cj])
                    o_ref[rows, c] = (dmv * gj).astype(o_ref.dtype)
                    dgpre = dmv * y_ref[rows, c].astype(F32) * gj * (1.0 - gj)
                    dgt_ref[rows, cj] = dgpre.astype(dgt_ref.dtype)
                    acc[jb, :, c] += dgpre
                return carry

            lax.fori_loop(0, tm // CHUNK, step, 0)
        dyn_ref[...] = _dot(dya_ref[...], wa_ref[...], NT).astype(dyn_ref.dtype)
        dac_ref[...] = _dot(dyb_ref[...], wb_ref[...], NT).astype(dac_ref.dtype)
        das_ref[...] = _dot(dyc_ref[...], wc_ref[...], NT).astype(das_ref.dtype)

        @pl.when(last)
        def _():
            for jb in range(3):
                dbg_ref[:, jb * D:(jb + 1) * D] = _colsum(acc[jb])
            dgm_ref[...] = _colsum(acc[3])

    act = S((n, D), BF)
    return pl.pallas_call(
        body, grid=(n // tm,),
        in_specs=[_rows(tm, D)] * 5 + [_rows(tm, 3 * D, P_GATES // (3 * D)), _full((1, 3 * D)), _full((INNER, D)),
                                       _full((CONF_W, D)), _full((SC_W, D)), _full((D, D)), _full((3, D))],
        out_specs=[_rows(tm, D)] * 4 + [_rows(tm, 3 * D, P_GATES // (3 * D)), _rows(tm, INNER), _rows(tm, CONF_W),
                                        _rows(tm, SC_W), _full((1, D)), _full((1, 3 * D))],
        out_shape=[act, act, act, act, S((n, PW), BF), S((n, INNER), BF), S((n, CONF_W), BF), S((n, SC_W), BF),
                   S((1, D), F32), S((1, 3 * D), F32)],
        scratch_shapes=[pltpu.VMEM((tm, D), F32), pltpu.VMEM((4, CHUNK, D), F32)],
        compiler_params=_cparams("arbitrary"), name=name)(dx, mix, ya, yb, yc, proj, b_gate, w_ssd, w_conf, w_sc, w_o, mod)


def _pad_w_in(w):
    zeros = jnp.zeros((w.shape[0], XBC_PAD - XBC - (R_CONF - R_DT)), w.dtype)
    return jnp.concatenate([w[:, R_GATES:], w[:, R_SC:R_GATES], w[:, R_XBC:R_DT], w[:, R_DT:R_CONF], zeros,
                            w[:, R_Z:R_XBC], w[:, R_CONF:R_SC]], axis=1)


def _unpad_w_in(wp):
    return jnp.concatenate([wp[:, P_Z:P_Z + INNER], wp[:, P_XBC:P_XBC + XBC], wp[:, P_DT:P_DT + HEADS],
                            wp[:, P_CONF:P_CONF + 2 * CONF_W], wp[:, P_SC:P_SC + 3 * SC_W], wp[:, P_GATES:P_GATES + 3 * D]],
                           axis=1)


W_IN_SHARD = N_IN // N_CHIPS
W_IN_SEGMENTS = ((R_Z, R_XBC, P_Z), (R_XBC, R_DT, P_XBC), (R_DT, R_CONF, P_DT), (R_CONF, R_SC, P_CONF),
                 (R_SC, R_GATES, P_SC), (R_GATES, N_IN, P_GATES))


def _pad_w_in_chips(w4):
    parts = []
    for lo, hi, dst in sorted(W_IN_SEGMENTS, key=lambda sgm: sgm[2]):
        for k in range(N_CHIPS):
            a, b = max(lo, k * W_IN_SHARD), min(hi, (k + 1) * W_IN_SHARD)
            if a < b:
                parts.append((dst + a - lo, w4[k][:, a - k * W_IN_SHARD:b - k * W_IN_SHARD]))
    out, pos = [], 0
    for start, piece in parts:
        if start > pos:
            out.append(jnp.zeros((w4.shape[1], start - pos), w4.dtype))
        out.append(piece)
        pos = start + piece.shape[1]
    if pos < PW:
        out.append(jnp.zeros((w4.shape[1], PW - pos), w4.dtype))
    return jnp.concatenate(out, axis=1)


def _unpad_w_in_chips(wp):
    blocks = []
    for k in range(N_CHIPS):
        pieces = []
        for lo, hi, dst in W_IN_SEGMENTS:
            a, b = max(lo, k * W_IN_SHARD), min(hi, (k + 1) * W_IN_SHARD)
            if a < b:
                pieces.append(wp[:, dst + a - lo:dst + b - lo])
        blocks.append(jnp.concatenate(pieces, axis=1))
    return jnp.stack(blocks)


def _row(v):
    return v.reshape(1, -1)


def _head_row(v):
    return jnp.pad(v, (0, 128 - HEADS)).reshape(1, 128)


def _layer_fwd(x, p, mod_mix, mod_ffn, tag, comm=None, comm_ssd=None, late_params=None, comm_up=None, comm_proj=None,
               mid_params=None):
    sv = {"x0": x}
    h = _modnorm_fwd(x, _row(p["norm_mix_g"]), mod_mix, name=f"modnorm_mix_fwd{tag}")
    proj = _mm_nn(h, p["w_in_pad"], out_dtype=BF, tm=1024, tn=2048, name=f"proj_fwd{tag}", comm=comm_proj)
    if comm_proj is not None:
        proj, proj_comm_out = proj
        p.update(mid_params(proj_comm_out))
    xbc_act, cpre = _ssd_pre_fwd(proj, p["ssd_conv_w"], _row(p["ssd_conv_b"]), name=f"ssd_pre_fwd{tag}")
    d_x = _row(jnp.repeat(p["ssd_d"], HEAD_DIM))
    (y, yn, hprev), ssd_comm_out = _ssd_fwd(xbc_act, proj, _head_row(p["ssd_dt_bias"]), _head_row(p["ssd_a_log"]), d_x,
                                            _row(p["ssd_norm_g"]), name=f"ssd_fwd{tag}", comm=comm_ssd)
    if late_params is not None:
        p.update(late_params(ssd_comm_out))
    a_conf, uc = _conf_fwd(proj, p["conf_conv_w"], _row(p["conf_conv_b"]), _row(p["conf_ln_g"]), _row(p["conf_ln_b"]),
                           name=f"conf_fwd{tag}")
    a_sc = _sc_fwd(proj, p["sc_conv_w"], name=f"sc_fwd{tag}")
    ya, yb, yc, merged, mix, x1 = _mixer_out_fwd(yn, a_conf, a_sc, proj, _row(p["b_gate"]), p["w_ssd_out"],
                                                 p["w_conf_out"], p["w_sc_out"], p["w_o"], x, mod_mix,
                                                 name=f"mixer_out_fwd{tag}")
    h2 = _modnorm_fwd(x1, _row(p["norm_ffn_g"]), mod_ffn, name=f"modnorm_ffn_fwd{tag}")
    up = _mm_nn_chips(h2, p["w_up4"], out_dtype=BF, tm=1024, name=f"up_fwd{tag}", comm=comm_up)
    up_comm_out = ()
    if comm_up is not None:
        up, up_comm_out = up
    (act, u_ffn), comm_out = _ffn_act_fwd(up, p["ffn_conv_w"], _row(p["ffn_conv_b"]), name=f"ffn_act_fwd{tag}", comm=comm)
    o, x2 = _mm_resid(act, p["w_down"], x1, mod_ffn, tm=512, name=f"down_fwd{tag}")
    sv.update(h=h, proj=proj, xbc_act=xbc_act, cpre=cpre, d_x=d_x, y=y, yn=yn, hprev=hprev, a_conf=a_conf, uc=uc, a_sc=a_sc,
              ya=ya, yb=yb, yc=yc, merged=merged, mix=mix, x1=x1, h2=h2, up=up, u_ffn=u_ffn, act=act, o=o)
    return x2, sv, tuple(up_comm_out) + tuple(comm_out)


def _layer_bwd(dx, p, sv, mod_mix, mod_ffn, tag, comm=None, make_comm_conf=None):
    g = {}
    do2, dgate_ffn = _gate_bwd(dx, sv["o"], mod_ffn, name=f"gate_ffn_bwd{tag}")
    dact = _mm_nt(do2, p["w_down"], out_dtype=BF, tm=1024, tk=D, name=f"down_dx{tag}")
    g["w_down"] = _mm_tn(sv["act"], do2, tn=D, tk=1024, name=f"down_dw{tag}")
    (dup, g["ffn_conv_w"], dffn_b), comm_out = _ffn_act_bwd(sv["up"], sv["u_ffn"], dact, p["ffn_conv_w"],
                                                            name=f"ffn_act_bwd{tag}", comm=comm)
    g["ffn_conv_b"] = dffn_b[0]
    dh2 = _mm_nt_chips(dup, p["w_up4"], out_dtype=F32, tm=1024, name=f"up_dx{tag}")
    g["w_up4"] = _mm_tn(sv["h2"], dup, tn=2 * DFF // N_CHIPS, tk=2048, by_chip=True, name=f"up_dw{tag}")
    dx1, dgn, dsh, dsc = _modnorm_bwd(dh2, sv["x1"], dx, _row(p["norm_ffn_g"]), mod_ffn, name=f"modnorm_ffn_bwd{tag}")
    g["norm_ffn_g"] = dgn[0]
    dmod_ffn = jnp.concatenate([dsh[0], dsc[0], dgate_ffn[0]])

    (do1, dya, dyb, dyc, dproj, dyn, dac, dasc, dgate_mix, dbg) = _mixer_out_bwd(
        dx1, sv["mix"], sv["ya"], sv["yb"], sv["yc"], sv["proj"], _row(p["b_gate"]), p["w_ssd_out"], p["w_conf_out"],
        p["w_sc_out"], p["w_o"], mod_mix, name=f"mixer_out_bwd{tag}")
    g["b_gate"] = dbg[0]
    g["w_o"] = _mm_tn(sv["merged"], do1, tn=D, tk=2048, name=f"wo_dw{tag}")
    g["w_ssd_out"] = _mm_tn(sv["yn"], dya, tn=D, tk=2048, name=f"wssd_dw{tag}")
    g["w_conf_out"] = _mm_tn(sv["a_conf"], dyb, tn=D, tk=2048, name=f"wconf_dw{tag}")
    g["w_sc_out"] = _mm_tn(sv["a_sc"], dyc, tn=D, tk=2048, name=f"wsc_dw{tag}")
    comm_conf = make_comm_conf(g) if make_comm_conf is not None else None
    (dproj, g["conf_conv_w"], dcb, dlg, dlb), conf_comm_out = _conf_bwd(
        sv["proj"], sv["uc"], dac, dproj, p["conf_conv_w"], _row(p["conf_ln_g"]), _row(p["conf_ln_b"]),
        name=f"conf_bwd{tag}", comm=comm_conf)
    g["conf_conv_b"], g["conf_ln_g"], g["conf_ln_b"] = dcb[0], dlg[0], dlb[0]
    dproj, g["sc_conv_w"] = _sc_bwd(sv["proj"], dasc, dproj, p["sc_conv_w"], name=f"sc_bwd{tag}")
    dproj, ddt, dxbc_act, dng, ddtb, dalog, ddd = _ssd_bwd(
        sv["xbc_act"], sv["proj"], sv["y"], dyn, sv["hprev"], dproj, _head_row(p["ssd_dt_bias"]),
        _head_row(p["ssd_a_log"]), sv["d_x"], _row(p["ssd_norm_g"]), name=f"ssd_bwd{tag}")
    g["ssd_norm_g"], g["ssd_dt_bias"], g["ssd_a_log"], g["ssd_d"] = dng[0], ddtb[0, :HEADS], dalog[0, :HEADS], ddd[0, :HEADS]
    dproj, g["ssd_conv_w"], dsb = _ssd_pre_bwd(sv["proj"], sv["cpre"], dxbc_act, ddt, dproj, p["ssd_conv_w"],
                                               name=f"ssd_pre_bwd{tag}")
    g["ssd_conv_b"] = dsb[0]
    dh = _mm_nt(dproj, p["w_in_pad"], out_dtype=F32, tm=1024, tk=4096, name=f"proj_dx{tag}")
    g["w_in_pad"] = _mm_tn(sv["h"], dproj, tn=2048, tk=2048, name=f"proj_dw{tag}")
    dx0, dgn, dsh, dsc = _modnorm_bwd(dh, sv["x0"], dx1, _row(p["norm_mix_g"]), mod_mix, name=f"modnorm_mix_bwd{tag}")
    g["norm_mix_g"] = dgn[0]
    dmod_mix = jnp.concatenate([dsh[0], dsc[0], dgate_mix[0]])
    return dx0, g, dmod_mix, dmod_ffn, (comm_out, conf_comm_out)


def _local_step(x, target, layers, mods, final_norm_g):
    saved = []
    for i, p in enumerate(layers):
        x, sv, _ = _layer_fwd(x, p, mods[i][0], mods[i][1], f"_l{i}")
        saved.append(sv)
    dx, loss, dfg = _final_loss(x, _row(final_norm_g), target, name="final_loss")
    grads, dmods = [None] * len(layers), [None] * len(layers)
    for i in reversed(range(len(layers))):
        dx, grads[i], dmm, dmf, _ = _layer_bwd(dx, layers[i], saved[i], mods[i][0], mods[i][1], f"_l{i}")
        dmods[i] = (dmm, dmf)
    return loss, dx, grads, dmods, dfg[0]


MESH = pl.DeviceIdType.MESH
ANY = pl.BlockSpec(memory_space=pl.ANY)
VMEM = pl.BlockSpec(memory_space=pltpu.VMEM)


def _mesh_pos():
    return lax.axis_index("x"), lax.axis_index("y"), lax.axis_index("c")


def _peer(pos, mask):
    return tuple(1 - v if (mask >> (2 - k)) & 1 else v for k, v in enumerate(pos))


def _lin(pos):
    return 4 * pos[0] + 2 * pos[1] + pos[2]


def _chip(pos):
    return 2 * pos[0] + pos[1]


def _rcopy(src, dst, send_sem, recv_sem, dev):
    return pltpu.make_async_remote_copy(src_ref=src, dst_ref=dst, send_sem=send_sem, recv_sem=recv_sem,
                                        device_id=dev, device_id_type=MESH)


CHIP_MASKS = (2, 4, 6)
SIBLING = 1
ADA_COLS = 3 * D // N_CHIPS
CONV_ROWS, CONV_COLS = 48, 2 * DFF // N_CHIPS
CONV_PACK = {"ffn_conv_w": (0, FFN_K, 2 * DFF // N_CHIPS), "ssd_conv_w": (3, SSD_K, XBC // N_CHIPS),
             "conf_conv_w": (8, CONF_K, CONF_W // N_CHIPS), "sc_conv_w": (40, SC_K, SC_W // N_CHIPS)}


def _ada_exchange(c_blk, ada_mix_w, ada_ffn_w, conv_pack):
    def body(c_ref, wm_ref, wf_ref, cw_ref, mods_ref, sc_ref, cwall_ref,
             call_scr, modp_scr, recv_scr, s1, r1, s3, r3, s4, r4):
        pos = _mesh_pos()
        me, km = _lin(pos), _chip(pos)
        call_scr[me] = c_ref[...]
        cwall_ref[km] = cw_ref[...]
        sends = []
        for m in range(1, N_DEV):
            sends.append(_rcopy(c_ref, call_scr.at[me], s1.at[m - 1], r1.at[m - 1], _peer(pos, m)))
        for j, m in enumerate(CHIP_MASKS):
            sends.append(_rcopy(cw_ref, cwall_ref.at[km], s4.at[j], r4.at[j], _peer(pos, m)))
        for cp in sends:
            cp.start()
        for m in range(1, N_DEV):
            src = _peer(pos, m)
            _rcopy(c_ref, call_scr.at[_lin(src)], s1.at[m - 1], r1.at[m - 1], src).wait_recv()
        cm = jnp.concatenate([call_scr[d, 0:1, :] for d in range(N_DEV)], axis=0)
        sc = cm * _sigmoid(cm)
        sc_ref[...] = sc
        for j, w in enumerate((wm_ref.at[0], wm_ref.at[1], wf_ref.at[0], wf_ref.at[1])):
            modp_scr[:, j * ADA_COLS:(j + 1) * ADA_COLS] = lax.dot_general(
                sc, w[...], NN, precision=HIGHEST, preferred_element_type=F32)
        recv_scr[km] = modp_scr[...]
        sends3 = [_rcopy(modp_scr, recv_scr.at[km], s3.at[j], r3.at[j], _peer(pos, m)) for j, m in enumerate(CHIP_MASKS)]
        for cp in sends3:
            cp.start()
        for j, m in enumerate(CHIP_MASKS):
            src = _peer(pos, m)
            _rcopy(modp_scr, recv_scr.at[_chip(src)], s3.at[j], r3.at[j], src).wait_recv()
            _rcopy(cw_ref, cwall_ref.at[_chip(src)], s4.at[j], r4.at[j], src).wait_recv()
        for k in range(N_CHIPS):
            mods_ref[k:k + 1, :] = recv_scr[k, pl.ds(me, 1), :]
        for cp in sends + sends3:
            cp.wait_send()

    dma = pltpu.SemaphoreType.DMA
    return pl.pallas_call(
        body, in_specs=[VMEM] * 4, out_specs=[VMEM] * 3,
        out_shape=[S((N_CHIPS, 4 * ADA_COLS), F32), S((N_DEV, D), F32), S((N_CHIPS,) + conv_pack.shape, F32)],
        scratch_shapes=[pltpu.VMEM((N_DEV, 8, D), F32), pltpu.VMEM((N_DEV, 4 * ADA_COLS), F32),
                        pltpu.VMEM((N_CHIPS, N_DEV, 4 * ADA_COLS), F32),
                        dma((N_DEV - 1,)), dma((N_DEV - 1,)), dma((3,)), dma((3,)), dma((3,)), dma((3,))],
        compiler_params=pltpu.CompilerParams(vmem_limit_bytes=VMEM_LIMIT_V7X), name="ada_exchange")(
            c_blk, ada_mix_w, ada_ffn_w, conv_pack)


class _Comm:
    def __init__(self, ins, out_shapes, scratch, start, finish):
        self.ins, self.out_shapes, self.scratch, self.start, self.finish = list(ins), list(out_shapes), list(scratch), start, finish


def _call_with_comm(body, nsteps, *, in_specs, out_specs, out_shape, scratch_shapes, args, comm, name, aliases=None):
    grid = nsteps if isinstance(nsteps, tuple) else (nsteps,)
    sem = ("arbitrary",) * len(grid)
    if comm is None:
        res = pl.pallas_call(body, grid=grid, in_specs=in_specs, out_specs=out_specs, out_shape=out_shape,
                             scratch_shapes=scratch_shapes, input_output_aliases=aliases or {},
                             compiler_params=_cparams(*sem), name=name)(*args)
        return tuple(res), ()
    ni, no, ns = len(in_specs), len(out_specs), len(scratch_shapes)
    ci, co = len(comm.ins), len(comm.out_shapes)

    def hosted(*refs):
        ins, cins = refs[:ni], refs[ni:ni + ci]
        outs, couts = refs[ni + ci:ni + ci + no], refs[ni + ci + no:ni + ci + no + co]
        scr, csem = refs[ni + ci + no + co:ni + ci + no + co + ns], refs[ni + ci + no + co + ns:]
        ids = [pl.program_id(d) for d in range(len(grid))]
        is_first, is_last = ids[0] == 0, ids[0] == grid[0] - 1
        for d in range(1, len(grid)):
            is_first = jnp.logical_and(is_first, ids[d] == 0)
            is_last = jnp.logical_and(is_last, ids[d] == grid[d] - 1)

        @pl.when(is_first)
        def _():
            comm.start(cins, couts, csem)

        body(*ins, *outs, *scr)

        @pl.when(is_last)
        def _():
            comm.finish(cins, couts, csem)

    res = pl.pallas_call(
        hosted, grid=grid, in_specs=list(in_specs) + [ANY] * ci, out_specs=list(out_specs) + [ANY] * co,
        out_shape=list(out_shape) + comm.out_shapes, scratch_shapes=list(scratch_shapes) + comm.scratch,
        input_output_aliases=aliases or {}, compiler_params=_cparams(*sem), name=name)(*args, *comm.ins)
    return tuple(res[:no]), tuple(res[no:])


def _run_comm(comm, name):
    def body(*refs):
        ci, co = len(comm.ins), len(comm.out_shapes)
        comm.start(refs[:ci], refs[ci:ci + co], refs[ci + co:])
        comm.finish(refs[:ci], refs[ci:ci + co], refs[ci + co:])

    return pl.pallas_call(body, in_specs=[ANY] * len(comm.ins), out_specs=[ANY] * len(comm.out_shapes),
                          out_shape=comm.out_shapes, scratch_shapes=comm.scratch, name=name)(*comm.ins)


def _gather_comm(shards, layer):
    na = len(shards)
    dma = pltpu.SemaphoreType.DMA

    def ici(ins, outs, sems, pos, a, j, m, sender):
        ssem, rsem = sems[0], sems[1]
        peer = _peer(pos, m)
        block = outs[a].at[_chip(pos) if sender else _chip(peer)]
        return _rcopy(ins[a].at[layer], block, ssem.at[a, j], rsem.at[a, j], peer)

    def start(ins, outs, sems):
        pos = _mesh_pos()

        @pl.when(pos[2] == layer)
        def _():
            for a in range(na):
                for j, m in enumerate(CHIP_MASKS):
                    ici(ins, outs, sems, pos, a, j, m, True).start()

    def finish(ins, outs, sems):
        fsend, frecv = sems[2], sems[3]
        pos = _mesh_pos()
        sib = _peer(pos, SIBLING)

        def forward(a, j, m):
            blk = outs[a].at[_chip(_peer(pos, m))]
            return _rcopy(blk, blk, fsend.at[a, j], frecv.at[a, j], sib)

        @pl.when(pos[2] == layer)
        def _():
            for a in range(na):
                for j, m in enumerate(CHIP_MASKS):
                    ici(ins, outs, sems, pos, a, j, m, False).wait_recv()
                    forward(a, j, m).start()
            for a in range(na):
                for j, m in enumerate(CHIP_MASKS):
                    ici(ins, outs, sems, pos, a, j, m, True).wait_send()
                    forward(a, j, m).wait_send()

        @pl.when(pos[2] != layer)
        def _():
            for a in range(na):
                for j, m in enumerate(CHIP_MASKS):
                    forward(a, j, m).wait_recv()

    return _Comm(shards, [S((N_CHIPS,) + s.shape[1:], s.dtype) for s in shards],
                 [dma((na, 3)), dma((na, 3)), dma((na, 3)), dma((na, 3))], start, finish)


def _scatter_comm(arrs, layer):
    na = len(arrs)
    dma = pltpu.SemaphoreType.DMA

    def copy(ins, outs, sems, pos, a, j, m, sender):
        peer = _peer(pos, m)
        src = ins[a].at[_chip(peer) if sender else _chip(pos)]
        dst = outs[a].at[_chip(pos) if sender else _chip(peer)]
        return _rcopy(src, dst, sems[0].at[a, j], sems[1].at[a, j], peer)

    def start(ins, outs, sems):
        pos = _mesh_pos()

        @pl.when(pos[2] == layer)
        def _():
            for a in range(na):
                for j, m in enumerate(CHIP_MASKS):
                    copy(ins, outs, sems, pos, a, j, m, True).start()

    def finish(ins, outs, sems):
        pos = _mesh_pos()

        @pl.when(pos[2] == layer)
        def _():
            for a in range(na):
                for j, m in enumerate(CHIP_MASKS):
                    copy(ins, outs, sems, pos, a, j, m, False).wait_recv()
            for a in range(na):
                for j, m in enumerate(CHIP_MASKS):
                    copy(ins, outs, sems, pos, a, j, m, True).wait_send()

    return _Comm(arrs, [S(s.shape, s.dtype) for s in arrs], [dma((na, 3)), dma((na, 3))], start, finish)


def _swap_layer(arrs, layer, tag):
    na = len(arrs)

    def body(*refs):
        ins, outs = refs[:na], refs[na:2 * na]
        ssem, rsem = refs[2 * na:]
        pos = _mesh_pos()
        sib = _peer(pos, SIBLING)
        cps = [_rcopy(ins[a], outs[a], ssem.at[a], rsem.at[a], sib) for a in range(na)]

        @pl.when(pos[2] != layer)
        def _():
            for cp in cps:
                cp.start()
            for cp in cps:
                cp.wait_send()

        @pl.when(pos[2] == layer)
        def _():
            for cp in cps:
                cp.wait_recv()

    dma = pltpu.SemaphoreType.DMA
    return pl.pallas_call(
        body, in_specs=[ANY] * na, out_specs=[ANY] * na, out_shape=[S(s.shape, s.dtype) for s in arrs],
        scratch_shapes=[dma((na,)), dma((na,))], name=f"swap_layer{tag}")(*arrs)


def _share_sibling(arrs):
    na = len(arrs)

    def body(*refs):
        bufs = refs[na:2 * na]
        ssem, rsem = refs[2 * na:]
        pos = _mesh_pos()
        c = pos[2]
        sib = _peer(pos, SIBLING)
        sends = [_rcopy(bufs[a].at[c], bufs[a].at[c], ssem.at[a], rsem.at[a], sib) for a in range(na)]
        for cp in sends:
            cp.start()
        for a in range(na):
            _rcopy(bufs[a].at[c], bufs[a].at[1 - c], ssem.at[a], rsem.at[a], sib).wait_recv()
        for cp in sends:
            cp.wait_send()

    dma = pltpu.SemaphoreType.DMA
    return pl.pallas_call(
        body, in_specs=[ANY] * na, out_specs=[ANY] * na, out_shape=[S(s.shape, s.dtype) for s in arrs],
        input_output_aliases={a: a for a in range(na)},
        scratch_shapes=[dma((na,)), dma((na,))], name="share_sibling")(*arrs)


def _small_allreduce(vec):
    r = vec.shape[0]

    def body(v_ref, sum_ref, all_ref, ssem, rsem):
        pos = _mesh_pos()
        me = _lin(pos)
        all_ref[me] = v_ref[...]
        cps = [_rcopy(v_ref, all_ref.at[me], ssem.at[m - 1], rsem.at[m - 1], _peer(pos, m)) for m in range(1, N_DEV)]
        for cp in cps:
            cp.start()
        for m in range(1, N_DEV):
            src = _peer(pos, m)
            _rcopy(v_ref, all_ref.at[_lin(src)], ssem.at[m - 1], rsem.at[m - 1], src).wait_recv()
        acc = all_ref[0]
        for d in range(1, N_DEV):
            acc = acc + all_ref[d]
        sum_ref[...] = acc
        for cp in cps:
            cp.wait_send()

    dma = pltpu.SemaphoreType.DMA
    return pl.pallas_call(
        body, in_specs=[VMEM], out_specs=[VMEM, VMEM],
        out_shape=[S((r, 128), F32), S((N_DEV, r, 128), F32)],
        scratch_shapes=[dma((N_DEV - 1,)), dma((N_DEV - 1,))],
        compiler_params=pltpu.CompilerParams(vmem_limit_bytes=VMEM_LIMIT_V7X), name="small_allreduce")(vec)


ROW_BYTES_TARGET = 1 << 20


def _row_tile(rows, cols, itemsize=4, align=16, target=ROW_BYTES_TARGET):
    fits = [t for t in range(align, rows + 1, align) if rows % t == 0]
    under = [t for t in fits if t * cols * itemsize <= target]
    return max(under) if under else (min(fits) if fits else rows)


def _pair_add(g, other):
    r, cdim = g.shape
    tr = _row_tile(r, cdim, 1)

    def body(g_ref, o_ref, out_ref):
        out_ref[...] = (g_ref[...].astype(F32) + o_ref[...].astype(F32)).astype(out_ref.dtype)

    blk = pl.BlockSpec((tr, cdim), lambda i: (i, 0))
    return pl.pallas_call(body, grid=(r // tr,), in_specs=[blk, blk], out_specs=blk, out_shape=S((r, cdim), BF),
                          compiler_params=_cparams("parallel"), name="pair_add")(g, other)


def _sum4(parts, pairs, chip_core):
    _, r, cdim = parts[0].shape
    tr = _row_tile(r, cdim)

    def body(kc_ref, q0_ref, q1_ref, own0_ref, own1_ref, out_ref):
        first = kc_ref[1] == 0
        mine = jnp.where(first, own0_ref[...], own1_ref[...]).astype(F32)
        terms = [jnp.where(kc_ref[0] == j, mine, jnp.where(first, q0_ref[j], q1_ref[j]).astype(F32))
                 for j in range(N_CHIPS)]
        out_ref[...] = ((terms[0] + terms[1]) + terms[2]) + terms[3]

    allc = pl.BlockSpec((N_CHIPS, tr, cdim), lambda i, kc: (0, i, 0))
    own = pl.BlockSpec((None, tr, cdim), lambda i, kc: (kc[0], i, 0))
    return pl.pallas_call(
        body,
        grid_spec=pltpu.PrefetchScalarGridSpec(
            num_scalar_prefetch=1, grid=(r // tr,), in_specs=[allc, allc, own, own],
            out_specs=pl.BlockSpec((None, tr, cdim), lambda i, kc: (kc[1], i, 0))),
        out_shape=S((DEPTH, r, cdim), F32), compiler_params=_cparams("parallel"), name="sum4")(
            chip_core, parts[0], parts[1], pairs[0], pairs[1])


def _ada_w_grad(silu_c, dmod_cols, chip):
    def body(k_ref, sc_ref, dm_ref, o_ref):
        del k_ref
        o_ref[...] = lax.dot_general(sc_ref[...], dm_ref[...], TN, precision=HIGHEST, preferred_element_type=F32)

    return pl.pallas_call(
        body,
        grid_spec=pltpu.PrefetchScalarGridSpec(
            num_scalar_prefetch=1, grid=(4,),
            in_specs=[pl.BlockSpec((N_DEV, D), lambda j, k: (0, 0)),
                      pl.BlockSpec((None, N_DEV, ADA_COLS), lambda j, k: (4 * j + k[0], 0, 0))],
            out_specs=pl.BlockSpec((None, D, ADA_COLS), lambda j, k: (j, 0, 0))),
        out_shape=S((4, D, ADA_COLS), F32), compiler_params=_cparams("parallel"), name="ada_w_grad")(chip, silu_c, dmod_cols)


def _adamw(w, g, m, v, *, name):
    r = w.shape[0]
    rest = w.shape[1:]
    row_elems = 1
    for d in rest:
        row_elems *= d
    tr = _row_tile(r, row_elems, align=8 if len(rest) == 1 else 1, target=5 * ROW_BYTES_TARGET // 2)
    c1 = 1.0 / (1.0 - ADAM_B1 ** ADAM_STEP)
    c2 = 1.0 / (1.0 - ADAM_B2 ** ADAM_STEP)

    def body(w_ref, g_ref, m_ref, v_ref, d_ref, mo_ref, vo_ref):
        gv = g_ref[...]
        mn = ADAM_B1 * m_ref[...] + (1.0 - ADAM_B1) * gv
        vn = ADAM_B2 * v_ref[...] + (1.0 - ADAM_B2) * (gv * gv)
        mo_ref[...] = mn
        vo_ref[...] = vn
        d_ref[...] = -ADAM_LR * ((mn * c1) / (jnp.sqrt(vn * c2) + ADAM_EPS) + ADAM_WD * w_ref[...])

    zeros = (0,) * len(rest)
    blk = pl.BlockSpec((tr,) + rest, lambda i: (i,) + zeros)
    return pl.pallas_call(
        body, grid=(r // tr,), in_specs=[blk] * 4, out_specs=[blk] * 3, out_shape=[S(w.shape, F32)] * 3,
        compiler_params=_cparams("parallel"), name=name)(w, g, m, v)


def _adamw_many(ws, gs, ms, vs):
    n = len(ws)
    c1 = 1.0 / (1.0 - ADAM_B1 ** ADAM_STEP)
    c2 = 1.0 / (1.0 - ADAM_B2 ** ADAM_STEP)

    def body(*refs):
        for i in range(n):
            w_ref, g_ref, m_ref, v_ref, d_ref, mo_ref, vo_ref = (refs[k * n + i] for k in range(7))
            gv = g_ref[...]
            mn = ADAM_B1 * m_ref[...] + (1.0 - ADAM_B1) * gv
            vn = ADAM_B2 * v_ref[...] + (1.0 - ADAM_B2) * (gv * gv)
            mo_ref[...] = mn
            vo_ref[...] = vn
            d_ref[...] = -ADAM_LR * ((mn * c1) / (jnp.sqrt(vn * c2) + ADAM_EPS) + ADAM_WD * w_ref[...])

    shapes = [S(a.shape, F32) for a in ws]
    outs = pl.pallas_call(
        body, in_specs=[VMEM] * (4 * n), out_specs=[VMEM] * (3 * n), out_shape=shapes * 3,
        compiler_params=pltpu.CompilerParams(vmem_limit_bytes=VMEM_LIMIT_V7X), name="adamw_small")(*ws, *gs, *ms, *vs)
    return outs[0:n], outs[n:2 * n], outs[2 * n:3 * n]


WEIGHTS = ['ada_mix_w', 'ada_mix_b', 'norm_mix_g', 'w_in', 'b_gate', 'ssd_conv_w', 'ssd_conv_b', 'ssd_dt_bias',
           'ssd_a_log', 'ssd_d', 'ssd_norm_g', 'w_ssd_out', 'conf_conv_w', 'conf_conv_b', 'conf_ln_g', 'conf_ln_b',
           'w_conf_out', 'sc_conv_w', 'w_sc_out', 'w_o', 'ada_ffn_w', 'ada_ffn_b', 'norm_ffn_g', 'w_up', 'ffn_conv_w',
           'ffn_conv_b', 'w_down', 'final_norm_g']
SMALL = ['ada_mix_b', 'norm_mix_g', 'b_gate', 'ssd_conv_b', 'ssd_dt_bias', 'ssd_a_log', 'ssd_d', 'ssd_norm_g', 'conf_conv_b',
         'conf_ln_g', 'conf_ln_b', 'ada_ffn_b', 'norm_ffn_g', 'ffn_conv_b']
CONVS = ['ssd_conv_w', 'conf_conv_w', 'sc_conv_w', 'ffn_conv_w']
BIG = ['ada_mix_w', 'ada_ffn_w', 'w_in', 'w_up', 'w_conf_out', 'w_sc_out', 'w_ssd_out', 'w_o', 'w_down']


def _pack_rows(pieces):
    flat = [p.reshape(-1) for p in pieces]
    offs, o = [], 0
    for f in flat:
        offs.append(o)
        o += f.shape[0]
    total = -(-o // 1024) * 1024
    vec = jnp.concatenate(flat + [jnp.zeros((total - o,), F32)])
    return vec.reshape(total // 128, 128), offs


def _by_chip(a, axis):
    shp = a.shape
    a = a.reshape(shp[:axis] + (N_CHIPS, shp[axis] // N_CHIPS) + shp[axis + 1:])
    return jnp.moveaxis(a, axis, 0)


def _from_chips(a, axis):
    a = jnp.moveaxis(a, 0, axis)
    shp = a.shape
    return a.reshape(shp[:axis] + (shp[axis] * shp[axis + 1],) + shp[axis + 2:])


def kernel(x, c, ada_mix_w, ada_mix_b, norm_mix_g, w_in, b_gate, ssd_conv_w, ssd_conv_b, ssd_dt_bias, ssd_a_log, ssd_d, ssd_norm_g, w_ssd_out, conf_conv_w, conf_conv_b, conf_ln_g, conf_ln_b, w_conf_out, sc_conv_w, w_sc_out, w_o, ada_ffn_w, ada_ffn_b, norm_ffn_g, w_up, ffn_conv_w, ffn_conv_b, w_down, final_norm_g, loss_target, m_ada_mix_w, m_ada_mix_b, m_norm_mix_g, m_w_in, m_b_gate, m_ssd_conv_w, m_ssd_conv_b, m_ssd_dt_bias, m_ssd_a_log, m_ssd_d, m_ssd_norm_g, m_w_ssd_out, m_conf_conv_w, m_conf_conv_b, m_conf_ln_g, m_conf_ln_b, m_w_conf_out, m_sc_conv_w, m_w_sc_out, m_w_o, m_ada_ffn_w, m_ada_ffn_b, m_norm_ffn_g, m_w_up, m_ffn_conv_w, m_ffn_conv_b, m_w_down, m_final_norm_g, v_ada_mix_w, v_ada_mix_b, v_norm_mix_g, v_w_in, v_b_gate, v_ssd_conv_w, v_ssd_conv_b, v_ssd_dt_bias, v_ssd_a_log, v_ssd_d, v_ssd_norm_g, v_w_ssd_out, v_conf_conv_w, v_conf_conv_b, v_conf_ln_g, v_conf_ln_b, v_w_conf_out, v_sc_conv_w, v_w_sc_out, v_w_o, v_ada_ffn_w, v_ada_ffn_b, v_norm_ffn_g, v_w_up, v_ffn_conv_w, v_ffn_conv_b, v_w_down, v_final_norm_g):
    args = locals()
    w = {n: args[n] for n in WEIGHTS}
    mom = {n: args["m_" + n] for n in WEIGHTS}
    var = {n: args["v_" + n] for n in WEIGHTS}
    pos = _mesh_pos()
    chip = _chip(pos)
    core = pos[2]

    conv_pack = jnp.zeros((DEPTH, CONV_ROWS, CONV_COLS), F32)
    for n, (r0, taps, width) in CONV_PACK.items():
        conv_pack = conv_pack.at[:, r0:r0 + taps, 0:width].set(w[n])
    c_blk = jnp.pad(c, ((0, 7), (0, 0)))
    mods_raw, silu_c, conv_all = _ada_exchange(c_blk, ada_mix_w, ada_ffn_w, conv_pack)
    ada_b = jnp.concatenate([ada_mix_b, ada_ffn_b], axis=0)
    mod_all = mods_raw.reshape(N_CHIPS, 4, ADA_COLS).transpose(1, 0, 2).reshape(4, 3 * D) + ada_b
    mod_all = mod_all.reshape(4, 3, D)
    mods = [(mod_all[i], mod_all[2 + i]) for i in range(DEPTH)]
    conv_full = {n: _from_chips(conv_all[:, :, r0:r0 + taps, 0:width], 2) for n, (r0, taps, width) in CONV_PACK.items()}

    cast = lambda a: a.astype(BF)
    shards = [cast(w_in), jnp.concatenate([cast(w_conf_out), cast(w_sc_out)], axis=1),
              jnp.concatenate([cast(w_ssd_out), cast(w_o)], axis=1), cast(w_up), cast(w_down)]
    FIRST, MID, LATE = (0,), (1, 2), (3, 4)
    EARLY = FIRST + MID
    pick = lambda arrs, idx: [arrs[k] for k in idx]

    def own_block(l, gathered, idx):
        return [lax.dynamic_update_slice(g, shards[k][l][None], (chip, 0, 0)) for g, k in zip(gathered, idx)]

    def first_params(l, gathered):
        (g_in,) = own_block(l, gathered, FIRST)
        p = {n: w[n][l] for n in SMALL if not n.startswith("ada_")}
        p.update({n: conv_full[n][l] for n in CONVS})
        p["w_in_pad"] = _pad_w_in_chips(g_in)
        return p

    def mid_params(l, gathered):
        g_cs, g_row = own_block(l, gathered, MID)
        return {"w_conf_out": _from_chips(g_cs[:, 0:CONF_W], 1), "w_sc_out": _from_chips(g_cs[:, CONF_W:], 1),
                "w_ssd_out": _from_chips(g_row[:, 0:INNER // N_CHIPS], 0), "w_o": _from_chips(g_row[:, INNER // N_CHIPS:], 0)}

    def late_params(l, gathered):
        g_up, g_down = own_block(l, gathered, LATE)
        return {"w_up4": g_up, "w_down": _from_chips(g_down, 0)}

    def big_grads(g, idx):
        makers = (lambda: _unpad_w_in_chips(g["w_in_pad"]),
                  lambda: _by_chip(jnp.concatenate([g["w_conf_out"], g["w_sc_out"]], axis=0), 1),
                  lambda: jnp.concatenate([_by_chip(g["w_ssd_out"], 0), _by_chip(g["w_o"], 0)], axis=1),
                  lambda: g["w_up4"], lambda: _by_chip(g["w_down"], 0))
        return [makers[k]().astype(BF) for k in idx]

    def pair_sums(big, l, tag):
        theirs = _swap_layer(big, l, tag)
        out = []
        for g4, t4 in zip(big, theirs):
            k, r, cdim = g4.shape
            out.append(_pair_add(g4.reshape(k * r, cdim), t4.reshape(k * r, cdim)).reshape(k, r, cdim))
        return out

    seq = x.shape[1]
    xs = x.reshape(seq, D)
    p0 = first_params(0, _run_comm(_gather_comm(pick(shards, FIRST), 0), "gather_weights_l0"))
    xs, sv0, gathered1 = _layer_fwd(xs, p0, mods[0][0], mods[0][1], "_l0",
                                    comm_proj=_gather_comm(pick(shards, MID), 0),
                                    mid_params=lambda got: mid_params(0, got),
                                    comm_ssd=_gather_comm(pick(shards, LATE), 0),
                                    late_params=lambda got: late_params(0, got),
                                    comm_up=_gather_comm(pick(shards, EARLY), 1),
                                    comm=_gather_comm(pick(shards, LATE), 1))
    p1 = first_params(1, gathered1[0:1])
    p1.update(mid_params(1, gathered1[1:3]))
    p1.update(late_params(1, gathered1[3:5]))
    xs, sv1, _ = _layer_fwd(xs, p1, mods[1][0], mods[1][1], "_l1")
    dx, loss, dfg = _final_loss(xs, _row(final_norm_g), loss_target.reshape(seq, D), name="final_loss")
    dfinal = dfg[0]

    every = EARLY + LATE
    grads, dmods = [None, None], [None, None]
    dx, grads[1], dmm, dmf, _ = _layer_bwd(dx, p1, sv1, mods[1][0], mods[1][1], "_l1")
    dmods[1] = (dmm, dmf)
    pair1 = pair_sums(big_grads(grads[1], every), 1, "_l1")
    late0 = {}

    def scatter_late0(g):
        late0["pair"] = pair_sums(big_grads(g, MID + LATE), 0, "_l0_ffn")
        return _scatter_comm(late0["pair"], 0)

    dx, grads[0], dmm, dmf, (parts1, parts0_late) = _layer_bwd(
        dx, p0, sv0, mods[0][0], mods[0][1], "_l0", comm=_scatter_comm(pair1, 1), make_comm_conf=scatter_late0)
    dmods[0] = (dmm, dmf)
    pair0_early = pair_sums(big_grads(grads[0], FIRST), 0, "_l0")
    parts0_early = _run_comm(_scatter_comm(pair0_early, 0), "scatter_chips_l0")
    pair0 = list(pair0_early) + list(late0["pair"])
    parts0 = list(parts0_early) + list(parts0_late)
    chip_core = jnp.stack([chip, core]).astype(jnp.int32)
    red = _share_sibling([_sum4((q0, q1), (o0, o1), chip_core)
                          for q0, q1, o0, o1 in zip(parts0, parts1, pair0, pair1)])
    reduced = {"w_in": red[0], "cs": red[1], "row": red[2], "w_up": red[3], "w_down": red[4]}
    gw = {"w_in": reduced["w_in"], "w_up": reduced["w_up"], "w_down": reduced["w_down"],
          "w_conf_out": reduced["cs"][:, 0:CONF_W], "w_sc_out": reduced["cs"][:, CONF_W:],
          "w_ssd_out": reduced["row"][:, 0:INNER // N_CHIPS], "w_o": reduced["row"][:, INNER // N_CHIPS:]}

    dmod = jnp.stack([dmods[0][0], dmods[1][0], dmods[0][1], dmods[1][1]])
    small_local = {n: jnp.stack([grads[l][n] for l in range(DEPTH)]) for n in SMALL if not n.startswith("ada_")}
    pieces = [loss[0]] + [small_local[n] for n in SMALL if not n.startswith("ada_")]
    pieces += [jnp.stack([grads[l][n] for l in range(DEPTH)]) for n in CONVS] + [dfinal, dmod]
    vec, offs = _pack_rows(pieces)
    vsum, vall = _small_allreduce(vec)
    flat = vsum.reshape(-1)

    def piece(k, like):
        return flat[offs[k]:offs[k] + like.size].reshape(like.shape)

    loss_out = flat[0]
    k = 1
    for n in SMALL:
        if not n.startswith("ada_"):
            gw[n] = piece(k, small_local[n])
            k += 1
    for n in CONVS:
        full = piece(k, conv_full[n])
        width = CONV_PACK[n][2]
        gw[n] = lax.dynamic_slice_in_dim(full, chip * width, width, axis=2)
        k += 1
    gw["final_norm_g"] = piece(k, dfinal)
    k += 1
    dmod_sum = piece(k, dmod)
    gw["ada_mix_b"], gw["ada_ffn_b"] = dmod_sum[0:2], dmod_sum[2:4]
    dmod_all = vall.reshape(N_DEV, -1)[:, offs[k]:offs[k] + dmod.size]
    dmod_cols = dmod_all.reshape(N_DEV, 4 * N_CHIPS, ADA_COLS).transpose(1, 0, 2)
    ada_g = _ada_w_grad(silu_c, dmod_cols, jnp.reshape(chip, (1,)).astype(jnp.int32))
    gw["ada_mix_w"], gw["ada_ffn_w"] = ada_g[0:2], ada_g[2:4]

    delta, new_m, new_v = {}, {}, {}
    for n in BIG:
        shp = w[n].shape
        if shp[2] % 128:
            two_d = lambda a: jnp.transpose(a, (2, 0, 1))
            back = lambda a: jnp.transpose(a, (1, 2, 0))
        else:
            two_d = lambda a: a.reshape(shp[0] * shp[1], shp[2])
            back = lambda a: a.reshape(shp)
        d_, m_, v_ = _adamw(two_d(w[n]), two_d(gw[n]), two_d(mom[n]), two_d(var[n]), name=f"adamw_{n}")
        delta[n], new_m[n], new_v[n] = back(d_), back(m_), back(v_)
    rest = [n for n in WEIGHTS if n not in BIG]
    two_d = lambda a: a.reshape(1, -1) if a.ndim == 1 else a
    outs = _adamw_many(*[[two_d(src[n]) for n in rest] for src in (w, gw, mom, var)])
    for dst, group in zip((delta, new_m, new_v), outs):
        for n, o in zip(rest, group):
            dst[n] = o.reshape(w[n].shape)

    return (loss_out, dx[None], *[gw[n] for n in WEIGHTS], *[delta[n] for n in WEIGHTS],
            *[new_m[n] for n in WEIGHTS], *[new_v[n] for n in WEIGHTS])
```

```python
import functools

import jax
import jax.numpy as jnp
from jax import lax
from jax.experimental import pallas as pl
from jax.experimental.pallas import tpu as pltpu

F32 = jnp.float32
BF = jnp.bfloat16
S = jax.ShapeDtypeStruct

D = 1024
HEADS = 16
HEAD_DIM = 64
INNER = HEADS * HEAD_DIM
GROUPS = 2
NSTATE = 64
Q = 128
SSD_K = 4
XBC = INNER + 2 * GROUPS * NSTATE
CONF_W = 512
CONF_K = 31
SC_W = 512
SC_K = 3
DFF = 2816
FFN_K = 3
EPS = 1e-6
DEPTH = 2
R_Z, R_XBC, R_DT, R_CONF, R_SC, R_GATES, N_IN = 0, 1024, 2304, 2320, 3344, 4880, 7952
P_GATES, P_SC, P_XBC, P_Z, P_CONF, PW = 0, 3072, 4608, 6144, 7168, 8192
XBC_PAD = 1536
DT_PAD = 128
P_DT = P_XBC + XBC
N_CHIPS = 4
N_DEV = 8

ADAM_LR, ADAM_B1, ADAM_B2, ADAM_EPS, ADAM_WD, ADAM_STEP = 0.001, 0.9, 0.999, 1e-08, 0.01, 10

VMEM_LIMIT_V7X = 56 * 1024 * 1024
HIGHEST = lax.Precision.HIGHEST


def _cparams(*sem):
    return pltpu.CompilerParams(dimension_semantics=sem, vmem_limit_bytes=VMEM_LIMIT_V7X)


def _full(shape):
    n = len(shape)
    return pl.BlockSpec(shape, lambda *_: (0,) * n)


def _rows(tm, w, cb=0):
    return pl.BlockSpec((tm, w), lambda i: (i, cb))


def _prev_rows(tm, halo, w, cb=0):
    r = tm // halo
    return pl.BlockSpec((halo, w), lambda i: (jnp.maximum(i * r - 1, 0), cb))


def _next_rows(tm, halo, w, nrows, cb=0):
    r = tm // halo
    last = nrows // halo - 1
    return pl.BlockSpec((halo, w), lambda i: (jnp.minimum((i + 1) * r, last), cb))


def _sigmoid(v):
    return 1.0 / (1.0 + jnp.exp(-v))


def _softplus(v):
    return jnp.maximum(v, 0.0) + jnp.log(1.0 + jnp.exp(-jnp.abs(v)))


def _colsum(v):
    return jnp.sum(v, axis=0, keepdims=True)


def _tile(n, want):
    t = min(n, want)
    assert n % t == 0, (n, want)
    return t


NN = (((1,), (0,)), ((), ()))
NT = (((1,), (1,)), ((), ()))
TN = (((0,), (0,)), ((), ()))


def _dot(a, b, dims=NN):
    return lax.dot_general(a, b, dims, preferred_element_type=F32)


def _mm(a, b, *, dims, grid, a_spec, b_spec, o_spec, out_shape, acc_shape, name, comm=None):
    nk = grid[2]

    def body(a_ref, b_ref, o_ref, acc_ref):
        k = pl.program_id(2)
        part = _dot(a_ref[...], b_ref[...], dims)
        if nk == 1:
            o_ref[...] = part.astype(o_ref.dtype)
        else:
            @pl.when(k == 0)
            def _():
                acc_ref[...] = part

            @pl.when(k > 0)
            def _():
                acc_ref[...] += part

            @pl.when(k == nk - 1)
            def _():
                o_ref[...] = acc_ref[...].astype(o_ref.dtype)

    scratch = [pltpu.VMEM(acc_shape if nk > 1 else (8, 128), F32)]
    if comm is not None:
        (out,), comm_out = _call_with_comm(body, tuple(grid), in_specs=[a_spec, b_spec], out_specs=[o_spec],
                                           out_shape=[out_shape], scratch_shapes=scratch, args=(a, b), comm=comm, name=name)
        return out, comm_out
    return pl.pallas_call(
        body, grid=grid, in_specs=[a_spec, b_spec], out_specs=o_spec, out_shape=out_shape, scratch_shapes=scratch,
        compiler_params=_cparams("parallel", "parallel", "arbitrary"), name=name)(a, b)


def _mm_nn(a, b, *, out_dtype, tm, tn, name, comm=None):
    m, k = a.shape
    n = b.shape[1]
    tm, tn = _tile(m, tm), _tile(n, tn)
    return _mm(a, b, dims=NN, grid=(m // tm, n // tn, 1),
               a_spec=pl.BlockSpec((tm, k), lambda i, j, kk: (i, 0)),
               b_spec=pl.BlockSpec((k, tn), lambda i, j, kk: (0, j)),
               o_spec=pl.BlockSpec((tm, tn), lambda i, j, kk: (i, j)),
               out_shape=S((m, n), out_dtype), acc_shape=(tm, tn), name=name, comm=comm)


def _mm_nt(a, b, *, out_dtype, tm, tk, name):
    m, kc = a.shape
    n = b.shape[0]
    tm, tk = _tile(m, tm), _tile(kc, tk)
    return _mm(a, b, dims=NT, grid=(m // tm, 1, kc // tk),
               a_spec=pl.BlockSpec((tm, tk), lambda i, j, kk: (i, kk)),
               b_spec=pl.BlockSpec((n, tk), lambda i, j, kk: (0, kk)),
               o_spec=pl.BlockSpec((tm, n), lambda i, j, kk: (i, 0)),
               out_shape=S((m, n), out_dtype), acc_shape=(tm, n), name=name)


def _mm_tn(a, b, *, tn, tk, name, out_dtype=BF, by_chip=False):
    kc, m = a.shape
    n = b.shape[1]
    tn, tk = _tile(n, tn), _tile(kc, tk)
    if by_chip:
        assert n == N_CHIPS * tn
        o_spec, out_shape = pl.BlockSpec((None, m, tn), lambda i, j, kk: (j, 0, 0)), S((N_CHIPS, m, tn), out_dtype)
    else:
        o_spec, out_shape = pl.BlockSpec((m, tn), lambda i, j, kk: (0, j)), S((m, n), out_dtype)
    return _mm(a, b, dims=TN, grid=(1, n // tn, kc // tk),
               a_spec=pl.BlockSpec((tk, m), lambda i, j, kk: (kk, 0)),
               b_spec=pl.BlockSpec((tk, tn), lambda i, j, kk: (kk, j)),
               o_spec=o_spec, out_shape=out_shape, acc_shape=(m, tn), name=name)


def _mm_nn_chips(a, b4, *, out_dtype, tm, name, comm=None):
    m, k = a.shape
    n4 = b4.shape[2]
    tm = _tile(m, tm)
    return _mm(a, b4, dims=NN, grid=(m // tm, N_CHIPS, 1),
               a_spec=pl.BlockSpec((tm, k), lambda i, j, kk: (i, 0)),
               b_spec=pl.BlockSpec((None, k, n4), lambda i, j, kk: (j, 0, 0)),
               o_spec=pl.BlockSpec((tm, n4), lambda i, j, kk: (i, j)),
               out_shape=S((m, N_CHIPS * n4), out_dtype), acc_shape=(tm, n4), name=name, comm=comm)


def _mm_nt_chips(a, b4, *, out_dtype, tm, name):
    m = a.shape[0]
    n, n4 = b4.shape[1], b4.shape[2]
    tm = _tile(m, tm)
    return _mm(a, b4, dims=NT, grid=(m // tm, 1, N_CHIPS),
               a_spec=pl.BlockSpec((tm, n4), lambda i, j, kk: (i, kk)),
               b_spec=pl.BlockSpec((None, n, n4), lambda i, j, kk: (kk, 0, 0)),
               o_spec=pl.BlockSpec((tm, n), lambda i, j, kk: (i, 0)),
               out_shape=S((m, n), out_dtype), acc_shape=(tm, n), name=name)


def _mm_resid(a, b, x, mod, *, tm, name):
    m, k = a.shape
    n = b.shape[1]
    tm = _tile(m, tm)

    def body(a_ref, b_ref, x_ref, mod_ref, o_ref, xn_ref):
        o = _dot(a_ref[...], b_ref[...])
        o_ref[...] = o.astype(o_ref.dtype)
        xn_ref[...] = x_ref[...] + mod_ref[2:3, :] * o

    return pl.pallas_call(
        body, grid=(m // tm,),
        in_specs=[_rows(tm, k), _full((k, n)), _rows(tm, n), _full((3, n))],
        out_specs=[_rows(tm, n), _rows(tm, n)],
        out_shape=[S((m, n), BF), S((m, n), F32)],
        compiler_params=_cparams("parallel"), name=name)(a, b, x, mod)


def _modnorm_fwd(x, gain, mod, *, name):
    n = x.shape[0]
    tm = _tile(n, 512)

    def body(x_ref, g_ref, mod_ref, h_ref):
        xv = x_ref[...]
        r = lax.rsqrt(jnp.mean(xv * xv, axis=-1, keepdims=True) + EPS)
        y = xv * r * g_ref[...]
        h_ref[...] = (y * (1.0 + mod_ref[1:2, :]) + mod_ref[0:1, :]).astype(h_ref.dtype)

    return pl.pallas_call(
        body, grid=(n // tm,), in_specs=[_rows(tm, D), _full((1, D)), _full((3, D))],
        out_specs=_rows(tm, D), out_shape=S((n, D), BF), compiler_params=_cparams("parallel"), name=name)(x, gain, mod)


def _modnorm_bwd(dh, x, dres, gain, mod, *, name):
    n = x.shape[0]
    tm = _tile(n, 512)

    def body(dh_ref, x_ref, dres_ref, g_ref, mod_ref, dx_ref, dg_ref, dsh_ref, dsc_ref):
        i = pl.program_id(0)
        xv = x_ref[...]
        r = lax.rsqrt(jnp.mean(xv * xv, axis=-1, keepdims=True) + EPS)
        xh = xv * r
        dhv = dh_ref[...]
        g = g_ref[...]
        dy = dhv * (1.0 + mod_ref[1:2, :])
        dxh = dy * g
        dx = r * (dxh - xh * jnp.mean(dxh * xh, axis=-1, keepdims=True))
        dx_ref[...] = dres_ref[...] + dx

        @pl.when(i == 0)
        def _():
            dg_ref[...] = jnp.zeros_like(dg_ref)
            dsh_ref[...] = jnp.zeros_like(dsh_ref)
            dsc_ref[...] = jnp.zeros_like(dsc_ref)

        dg_ref[...] += _colsum(dy * xh)
        dsh_ref[...] += _colsum(dhv)
        dsc_ref[...] += _colsum(dhv * xh * g)

    vec = S((1, D), F32)
    return pl.pallas_call(
        body, grid=(n // tm,),
        in_specs=[_rows(tm, D), _rows(tm, D), _rows(tm, D), _full((1, D)), _full((3, D))],
        out_specs=[_rows(tm, D), _full((1, D)), _full((1, D)), _full((1, D))],
        out_shape=[S((n, D), F32), vec, vec, vec],
        compiler_params=_cparams("arbitrary"), name=name)(dh, x, dres, gain, mod)


def _final_loss(x, gain, target, *, name):
    n = x.shape[0]
    tm = _tile(n, 512)

    def body(x_ref, g_ref, t_ref, dx_ref, loss_ref, dg_ref):
        i = pl.program_id(0)
        xv = x_ref[...]
        g = g_ref[...]
        r = lax.rsqrt(jnp.mean(xv * xv, axis=-1, keepdims=True) + EPS)
        xh = xv * r
        err = xh * g - t_ref[...]
        dy = err * (1.0 / D)
        dxh = dy * g
        dx_ref[...] = r * (dxh - xh * jnp.mean(dxh * xh, axis=-1, keepdims=True))

        @pl.when(i == 0)
        def _():
            loss_ref[...] = jnp.zeros_like(loss_ref)
            dg_ref[...] = jnp.zeros_like(dg_ref)

        part = _colsum(jnp.sum(err * err, axis=-1, keepdims=True)) * (0.5 / D)
        loss_ref[...] += jnp.broadcast_to(part, loss_ref.shape)
        dg_ref[...] += _colsum(dy * xh)

    return pl.pallas_call(
        body, grid=(n // tm,),
        in_specs=[_rows(tm, D), _full((1, D)), _rows(tm, D)],
        out_specs=[_rows(tm, D), _full((1, 128)), _full((1, D))],
        out_shape=[S((n, D), F32), S((1, 128), F32), S((1, D), F32)],
        compiler_params=_cparams("arbitrary"), name=name)(x, gain, target)


def _gate_bwd(dx, o, mod, *, name):
    n = dx.shape[0]
    tm = _tile(n, 512)

    def body(dx_ref, o_ref, mod_ref, do_ref, dgt_ref):
        i = pl.program_id(0)
        dxv = dx_ref[...]
        do_ref[...] = (dxv * mod_ref[2:3, :]).astype(do_ref.dtype)

        @pl.when(i == 0)
        def _():
            dgt_ref[...] = jnp.zeros_like(dgt_ref)

        dgt_ref[...] += _colsum(dxv * o_ref[...].astype(F32))

    return pl.pallas_call(
        body, grid=(n // tm,), in_specs=[_rows(tm, D), _rows(tm, D), _full((3, D))],
        out_specs=[_rows(tm, D), _full((1, D))], out_shape=[S((n, D), BF), S((1, D), F32)],
        compiler_params=_cparams("arbitrary"), name=name)(dx, o, mod)


def _conv(buf, w_ref, taps, start, rows, ch):
    acc = None
    for k in range(taps):
        term = buf[pl.ds(start - (taps - 1) + k, rows), 0:ch] * w_ref[k:k + 1, :]
        acc = term if acc is None else acc + term
    return acc


def _conv_t(buf, w_ref, taps, start, rows, ch):
    acc = None
    for k in range(taps):
        term = buf[pl.ds(start + (taps - 1) - k, rows), 0:ch] * w_ref[k:k + 1, :]
        acc = term if acc is None else acc + term
    return acc


def _conv_dw(dw_ref, dy, xbuf, taps, xstart, rows, ch):
    for k in range(taps):
        dw_ref[k:k + 1, :] += _colsum(dy * xbuf[pl.ds(xstart - (taps - 1) + k, rows), 0:ch])


HALO = 16
CONF_HALO = 32
CHUNK = 32
STRIP = 256


def _blocks8(v):
    return [v[8 * i:8 * (i + 1)] for i in range(v.shape[0] // 8)]


def _delay_rows(blocks, s):
    sub = lax.broadcasted_iota(jnp.int32, blocks[0].shape, 0)
    rolled = [pltpu.roll(b, s, 0) for b in blocks]
    return [jnp.where(sub < s, rolled[i - 1], rolled[i]) for i in range(1, len(blocks))]


def _advance_rows(blocks, s):
    sub = lax.broadcasted_iota(jnp.int32, blocks[0].shape, 0)
    rolled = [pltpu.roll(b, 8 - s, 0) for b in blocks]
    return [jnp.where(sub < 8 - s, rolled[i], rolled[i + 1]) for i in range(len(blocks) - 1)]


def _conv3_chunk(tail, xv, wk):
    blocks = [tail] + _blocks8(xv)
    x1 = jnp.concatenate(_delay_rows(blocks, 1), axis=0)
    x2 = jnp.concatenate(_delay_rows(blocks, 2), axis=0)
    return wk[0] * x2 + wk[1] * x1 + wk[2] * xv


def _conv_chunk(tail, xv, wk):
    taps = len(wk)
    blocks = [tail] + _blocks8(xv)
    acc = wk[taps - 1] * xv
    for d in range(1, taps):
        acc = acc + wk[taps - 1 - d] * jnp.concatenate(_delay_rows(blocks, d), axis=0)
    return acc


def _ssd_pre_fwd(proj, w, b, *, name):
    n = proj.shape[0]
    tm = _tile(n, 512)
    cb = P_XBC // XBC_PAD

    def body(prev_ref, cur_ref, w_ref, b_ref, o_ref, c_ref, buf):
        i = pl.program_id(0)
        buf[0:HALO, :] = jnp.where(i == 0, 0.0, prev_ref[:, 0:XBC].astype(F32))
        buf[HALO:HALO + tm, :] = cur_ref[:, 0:XBC].astype(F32)
        c = _conv(buf, w_ref, SSD_K, HALO, tm, XBC) + b_ref[...]
        c_ref[...] = c.astype(c_ref.dtype)
        o_ref[...] = (c * _sigmoid(c)).astype(o_ref.dtype)

    return pl.pallas_call(
        body, grid=(n // tm,),
        in_specs=[_prev_rows(tm, HALO, XBC_PAD, cb), _rows(tm, XBC_PAD, cb), _full((SSD_K, XBC)), _full((1, XBC))],
        out_specs=[_rows(tm, XBC), _rows(tm, XBC)], out_shape=[S((n, XBC), BF), S((n, XBC), BF)],
        scratch_shapes=[pltpu.VMEM((HALO + tm, XBC), F32)],
        compiler_params=_cparams("parallel"), name=name)(proj, proj, w, b)


def _ssd_pre_bwd(proj, cpre, dact, ddt, dproj, w, *, name):
    n = proj.shape[0]
    tm = _tile(n, 512)
    nt = n // tm
    cb = P_XBC // XBC_PAD

    def body(x_ref, cc_ref, cn_ref, dc_ref, dn_ref, ddt_ref, w_ref, dproj_in, o_ref, dw_ref, db_ref, dbuf, acc):
        del dproj_in
        i = pl.program_id(0)
        last = i == nt - 1

        @pl.when(i == 0)
        def _():
            acc[...] = jnp.zeros_like(acc)

        def silu_bwd(cv, dav):
            sg = _sigmoid(cv)
            return dav * (sg * (1.0 + cv * (1.0 - sg)))

        for s in range(XBC // STRIP):
            c = pl.ds(s * STRIP, STRIP)
            wk = [w_ref[k:k + 1, c] for k in range(SSD_K)]

            def step1(j, carry):
                rows = pl.ds(pl.multiple_of(j * CHUNK, CHUNK), CHUNK)
                dbuf[rows, c] = silu_bwd(cc_ref[rows, c].astype(F32), dc_ref[rows, c].astype(F32))
                return carry

            lax.fori_loop(0, tm // CHUNK, step1, 0, unroll=2)
            dbuf[tm:tm + HALO, c] = silu_bwd(cn_ref[:, c].astype(F32), jnp.where(last, 0.0, dn_ref[:, c].astype(F32)))

            def step2(j, carry):
                r0 = pl.multiple_of(j * CHUNK, CHUNK)
                rows = pl.ds(r0, CHUNK)
                win = dbuf[pl.ds(r0, CHUNK + 8), c]
                blocks = _blocks8(win)
                xv = x_ref[rows, c].astype(F32)
                d0 = win[0:CHUNK]
                dx = wk[SSD_K - 1] * d0
                acc[SSD_K - 1, :, c] += d0 * xv
                acc[SSD_K, :, c] += d0
                for adv in range(1, SSD_K):
                    dk = jnp.concatenate(_advance_rows(blocks, adv), axis=0)
                    dx = dx + wk[SSD_K - 1 - adv] * dk
                    acc[SSD_K - 1 - adv, :, c] += dk * xv
                o_ref[rows, c] = dx.astype(o_ref.dtype)
                return carry

            lax.fori_loop(0, tm // CHUNK, step2, 0)

        o_ref[:, XBC:XBC + DT_PAD] = ddt_ref[...]
        o_ref[:, XBC + DT_PAD:XBC_PAD] = jnp.zeros((tm, XBC_PAD - XBC - DT_PAD), o_ref.dtype)

        @pl.when(last)
        def _():
            for k in range(SSD_K):
                dw_ref[k:k + 1, :] = _colsum(acc[k])
            db_ref[...] = _colsum(acc[SSD_K])

    return pl.pallas_call(
        body, grid=(nt,),
        in_specs=[_rows(tm, XBC_PAD, cb), _rows(tm, XBC), _next_rows(tm, HALO, XBC, n),
                  _rows(tm, XBC), _next_rows(tm, HALO, XBC, n), _rows(tm, DT_PAD),
                  _full((SSD_K, XBC)), pl.BlockSpec(memory_space=pl.ANY)],
        out_specs=[_rows(tm, XBC_PAD, cb), _full((SSD_K, XBC)), _full((1, XBC))],
        out_shape=[S(dproj.shape, dproj.dtype), S((SSD_K, XBC), F32), S((1, XBC), F32)],
        scratch_shapes=[pltpu.VMEM((tm + HALO, XBC), F32), pltpu.VMEM((SSD_K + 1, CHUNK, XBC), F32)],
        input_output_aliases={7: 0},
        compiler_params=_cparams("arbitrary"), name=name)(proj, cpre, cpre, dact, dact, ddt, w, dproj)


def _sc_fwd(proj, w, *, name):
    n = proj.shape[0]
    tm = _tile(n, 512)
    cb = P_SC // (3 * SC_W)

    def body(prev_ref, cur_ref, w_ref, o_ref, buf):
        i = pl.program_id(0)
        pv = prev_ref[...].astype(F32)
        cv = cur_ref[...].astype(F32)
        buf[0:HALO, :] = jnp.where(i == 0, 0.0, pv[:, SC_W:2 * SC_W] * pv[:, 2 * SC_W:])
        buf[HALO:HALO + tm, :] = cv[:, SC_W:2 * SC_W] * cv[:, 2 * SC_W:]
        q = _conv(buf, w_ref, SC_K, HALO, tm, SC_W)
        o_ref[...] = (cv[:, 0:SC_W] * q).astype(o_ref.dtype)

    return pl.pallas_call(
        body, grid=(n // tm,),
        in_specs=[_prev_rows(tm, HALO, 3 * SC_W, cb), _rows(tm, 3 * SC_W, cb), _full((SC_K, SC_W))],
        out_specs=_rows(tm, SC_W), out_shape=S((n, SC_W), BF),
        scratch_shapes=[pltpu.VMEM((HALO + tm, SC_W), F32)],
        compiler_params=_cparams("parallel"), name=name)(proj, proj, w)


def _sc_bwd(proj, da, dproj, w, *, name):
    n = proj.shape[0]
    tm = _tile(n, 512)
    nt = n // tm
    cb = P_SC // (3 * SC_W)

    def body(xp_ref, xc_ref, xn_ref, dc_ref, dn_ref, w_ref, dproj_in, o_ref, dw_ref, pbuf, dbuf):
        del dproj_in
        i = pl.program_id(0)
        pv = xp_ref[...].astype(F32)
        cv = xc_ref[...].astype(F32)
        nv = xn_ref[...].astype(F32)
        gb, gc, xv = cv[:, 0:SC_W], cv[:, SC_W:2 * SC_W], cv[:, 2 * SC_W:]
        pbuf[0:HALO, :] = jnp.where(i == 0, 0.0, pv[:, SC_W:2 * SC_W] * pv[:, 2 * SC_W:])
        pbuf[HALO:HALO + tm, :] = gc * xv
        q = _conv(pbuf, w_ref, SC_K, HALO, tm, SC_W)
        dav = dc_ref[...].astype(F32)
        dbuf[0:tm, :] = dav * gb
        dbuf[tm:tm + HALO, :] = jnp.where(i == nt - 1, 0.0, dn_ref[...].astype(F32) * nv[:, 0:SC_W])
        dp = _conv_t(dbuf, w_ref, SC_K, 0, tm, SC_W)
        o_ref[:, 0:SC_W] = (dav * q).astype(o_ref.dtype)
        o_ref[:, SC_W:2 * SC_W] = (dp * xv).astype(o_ref.dtype)
        o_ref[:, 2 * SC_W:] = (dp * gc).astype(o_ref.dtype)

        @pl.when(i == 0)
        def _():
            dw_ref[...] = jnp.zeros_like(dw_ref)

        _conv_dw(dw_ref, dbuf[0:tm, :], pbuf, SC_K, HALO, tm, SC_W)

    return pl.pallas_call(
        body, grid=(nt,),
        in_specs=[_prev_rows(tm, HALO, 3 * SC_W, cb), _rows(tm, 3 * SC_W, cb), _next_rows(tm, HALO, 3 * SC_W, n, cb),
                  _rows(tm, SC_W), _next_rows(tm, HALO, SC_W, n), _full((SC_K, SC_W)),
                  pl.BlockSpec(memory_space=pl.ANY)],
        out_specs=[_rows(tm, 3 * SC_W, cb), _full((SC_K, SC_W))],
        out_shape=[S(dproj.shape, dproj.dtype), S((SC_K, SC_W), F32)],
        scratch_shapes=[pltpu.VMEM((HALO + tm, SC_W), F32), pltpu.VMEM((tm + HALO, SC_W), F32)],
        input_output_aliases={6: 0},
        compiler_params=_cparams("arbitrary"), name=name)(proj, proj, proj, da, da, w, dproj)


CONF_ROWS = 32
CONF_PHASES = 8


def _fill_advanced(src, dst, nblk, ncol):
    for s in range(ncol // STRIP):
        c = pl.ds(s * STRIP, STRIP)
        sub = lax.broadcasted_iota(jnp.int32, (8, STRIP), 0)
        first = src[0:8, c]
        carry0 = tuple(pltpu.roll(first, 8 - b, 0) for b in range(1, CONF_PHASES))

        def step(i, prev):
            r0 = pl.multiple_of(i * 8, 8)
            blk = src[pl.ds(r0, 8), c]
            cur = []
            for b in range(1, CONF_PHASES):
                rolled = pltpu.roll(blk, 8 - b, 0)
                cur.append(rolled)
                dst[b - 1, pl.ds(r0 - 8, 8), c] = jnp.where(sub < 8 - b, prev[b - 1], rolled)
            return tuple(cur)

        lax.fori_loop(1, nblk, step, carry0)


def _conf_fwd(proj, w, b, ln_g, ln_b, *, name):
    n = proj.shape[0]
    tm = _tile(n, 512)
    cb = P_CONF // (2 * CONF_W)
    h = CONF_HALO

    def body(prev_ref, cur_ref, w_ref, b_ref, g_ref, be_ref, a_ref, uc_ref, buf):
        i = pl.program_id(0)
        pv = prev_ref[...].astype(F32)
        cv = cur_ref[...].astype(F32)
        buf[0:h, :] = jnp.where(i == 0, 0.0, pv[:, 0:CONF_W] * _sigmoid(pv[:, CONF_W:]))
        buf[h:h + tm, :] = cv[:, 0:CONF_W] * _sigmoid(cv[:, CONF_W:])
        uc = _conv(buf, w_ref, CONF_K, h, tm, CONF_W) + b_ref[...]
        uc_ref[...] = uc.astype(uc_ref.dtype)
        mu = jnp.mean(uc, axis=-1, keepdims=True)
        xc = uc - mu
        v = xc * lax.rsqrt(jnp.mean(xc * xc, axis=-1, keepdims=True) + EPS) * g_ref[...] + be_ref[...]
        a_ref[...] = (v * _sigmoid(v)).astype(a_ref.dtype)

    vec = _full((1, CONF_W))
    return pl.pallas_call(
        body, grid=(n // tm,),
        in_specs=[_prev_rows(tm, h, 2 * CONF_W, cb), _rows(tm, 2 * CONF_W, cb), _full((CONF_K, CONF_W)), vec, vec, vec],
        out_specs=[_rows(tm, CONF_W), _rows(tm, CONF_W)],
        out_shape=[S((n, CONF_W), BF), S((n, CONF_W), BF)],
        scratch_shapes=[pltpu.VMEM((h + tm, CONF_W), F32)],
        compiler_params=_cparams("parallel"), name=name)(proj, proj, w, b, ln_g, ln_b)


def _conf_bwd(proj, uc, da, dproj, w, ln_g, ln_b, *, name, comm=None):
    n = proj.shape[0]
    tm = _tile(n, 512)
    nt = n // tm
    cb = P_CONF // (2 * CONF_W)
    h = CONF_HALO

    def body(xp_ref, xc_ref, ucc_ref, ucn_ref, dac_ref, dan_ref, w_ref, g_ref, be_ref, dproj_in,
             o_ref, dw_ref, db_ref, dg_ref, dbe_ref, ubuf, dbuf, dsh, wb):
        del dproj_in
        i = pl.program_id(0)
        pv = xp_ref[...].astype(F32)
        cv = xc_ref[...].astype(F32)
        val, gt = cv[:, 0:CONF_W], cv[:, CONF_W:]
        sg = _sigmoid(gt)
        ubuf[0:h, :] = jnp.where(i == 0, 0.0, pv[:, 0:CONF_W] * _sigmoid(pv[:, CONF_W:]))
        ubuf[h:h + tm, :] = val * sg

        def ln_silu_bwd(ucv, dav):
            mu = jnp.mean(ucv, axis=-1, keepdims=True)
            xc = ucv - mu
            r = lax.rsqrt(jnp.mean(xc * xc, axis=-1, keepdims=True) + EPS)
            xh = xc * r
            v = xh * g_ref[...] + be_ref[...]
            s = _sigmoid(v)
            dv = dav * (s * (1.0 + v * (1.0 - s)))
            dxh = dv * g_ref[...]
            duc = r * (dxh - jnp.mean(dxh, axis=-1, keepdims=True) - xh * jnp.mean(dxh * xh, axis=-1, keepdims=True))
            return duc, dv, xh

        duc, dv, xh = ln_silu_bwd(ucc_ref[...].astype(F32), dac_ref[...].astype(F32))
        dbuf[0:tm, :] = duc
        ducn, _, _ = ln_silu_bwd(ucn_ref[...].astype(F32), dan_ref[...].astype(F32))
        dbuf[tm:tm + h, :] = jnp.where(i == nt - 1, 0.0, ducn)
        for k in range(CONF_K):
            wb[k] = jnp.broadcast_to(w_ref[k:k + 1, :], (8, CONF_W))
        _fill_advanced(dbuf, dsh, (tm + h) // 8, CONF_W)
        nb = CONF_ROWS // 8
        for s_ in range(CONF_W // STRIP):
            c = pl.ds(s_ * STRIP, STRIP)
            cg = pl.ds(CONF_W + s_ * STRIP, STRIP)

            def dconv(j, carry):
                r0 = pl.multiple_of(j * CONF_ROWS, CONF_ROWS)
                acc = [jnp.zeros((8, STRIP), F32) for _ in range(nb)]
                for t in range(CONF_K):
                    wv = wb[CONF_K - 1 - t, :, c]
                    for q in range(nb):
                        rows8 = pl.ds(r0 + 8 * q + 8 * (t // 8), 8)
                        dv = dbuf[rows8, c] if t % 8 == 0 else dsh[t % 8 - 1, rows8, c]
                        acc[q] = acc[q] + wv * dv
                du = jnp.concatenate(acc, axis=0)
                rows = pl.ds(r0, CONF_ROWS)
                vl = xc_ref[rows, c].astype(F32)
                sgv = _sigmoid(xc_ref[rows, cg].astype(F32))
                o_ref[rows, c] = (du * sgv).astype(o_ref.dtype)
                o_ref[rows, cg] = (du * vl * sgv * (1.0 - sgv)).astype(o_ref.dtype)
                return carry

            lax.fori_loop(0, tm // CONF_ROWS, dconv, 0)

        @pl.when(i == 0)
        def _():
            dw_ref[...] = jnp.zeros_like(dw_ref)
            db_ref[...] = jnp.zeros_like(db_ref)
            dg_ref[...] = jnp.zeros_like(dg_ref)
            dbe_ref[...] = jnp.zeros_like(dbe_ref)

        dg_ref[...] += _colsum(dv * xh)
        dbe_ref[...] += _colsum(dv)
        db_ref[...] += _colsum(duc)
        _conv_dw(dw_ref, duc, ubuf, CONF_K, h, tm, CONF_W)

    vec = _full((1, CONF_W))
    vshape = S((1, CONF_W), F32)
    return _call_with_comm(
        body, nt,
        in_specs=[_prev_rows(tm, h, 2 * CONF_W, cb), _rows(tm, 2 * CONF_W, cb),
                  _rows(tm, CONF_W), _next_rows(tm, h, CONF_W, n), _rows(tm, CONF_W), _next_rows(tm, h, CONF_W, n),
                  _full((CONF_K, CONF_W)), vec, vec, pl.BlockSpec(memory_space=pl.ANY)],
        out_specs=[_rows(tm, 2 * CONF_W, cb), _full((CONF_K, CONF_W)), vec, vec, vec],
        out_shape=[S(dproj.shape, dproj.dtype), S((CONF_K, CONF_W), F32), vshape, vshape, vshape],
        scratch_shapes=[pltpu.VMEM((h + tm, CONF_W), F32), pltpu.VMEM((tm + h, CONF_W), F32),
                        pltpu.VMEM((CONF_PHASES - 1, tm + h, CONF_W), F32), pltpu.VMEM((CONF_K, 8, CONF_W), F32)],
        args=(proj, proj, uc, uc, da, da, w, ln_g, ln_b, dproj), comm=comm, name=name, aliases={9: 0})


def _ffn_act_fwd(up, w, b, *, name, comm=None):
    n = up.shape[0]
    tm = _tile(n, 512)
    c2 = 2 * DFF

    def body(prev_ref, cur_ref, w_ref, b_ref, o_ref, u_ref):
        first = pl.program_id(0) == 0
        for s in range(DFF // STRIP):
            cols = (pl.ds(s * STRIP, STRIP), pl.ds(DFF + s * STRIP, STRIP))
            wk = [[w_ref[k:k + 1, c] for k in range(FFN_K)] for c in cols]
            bk = [b_ref[:, c] for c in cols]
            tails = tuple(jnp.where(first, 0.0, prev_ref[:, c].astype(F32)[HALO - 8:HALO]) for c in cols)

            def step(j, tails):
                r0 = pl.multiple_of(j * CHUNK, CHUNK)
                us, new_tails = [], []
                for h in range(2):
                    xv = cur_ref[pl.ds(r0, CHUNK), cols[h]].astype(F32)
                    us.append(_conv3_chunk(tails[h], xv, wk[h]) + bk[h])
                    u_ref[pl.ds(r0, CHUNK), cols[h]] = us[h].astype(u_ref.dtype)
                    new_tails.append(xv[CHUNK - 8:CHUNK])
                o_ref[pl.ds(r0, CHUNK), cols[0]] = (us[0] * _sigmoid(us[0]) * us[1]).astype(o_ref.dtype)
                return tuple(new_tails)

            lax.fori_loop(0, tm // CHUNK, step, tails, unroll=2)

    return _call_with_comm(
        body, n // tm,
        in_specs=[_prev_rows(tm, HALO, c2), _rows(tm, c2), _full((FFN_K, c2)), _full((1, c2))],
        out_specs=[_rows(tm, DFF), _rows(tm, c2)], out_shape=[S((n, DFF), BF), S((n, c2), BF)],
        scratch_shapes=[], args=(up, up, w, b), comm=comm, name=name)


def _ffn_act_bwd(up, u, dact, w, *, name, comm=None):
    n = up.shape[0]
    tm = _tile(n, 512)
    nt = n // tm
    c2 = 2 * DFF

    def body(x_ref, uc_ref, un_ref, dc_ref, dn_ref, w_ref, o_ref, dw_ref, db_ref, dbuf, acc):
        i = pl.program_id(0)
        last = i == nt - 1

        @pl.when(i == 0)
        def _():
            acc[...] = jnp.zeros_like(acc)

        def swiglu_bwd(gate, val, dav):
            sg = _sigmoid(gate)
            return dav * val * (sg * (1.0 + gate * (1.0 - sg))), dav * gate * sg

        for s in range(DFF // STRIP):
            cols = (pl.ds(s * STRIP, STRIP), pl.ds(DFF + s * STRIP, STRIP))
            wk = [[w_ref[k:k + 1, c] for k in range(FFN_K)] for c in cols]

            def step1(j, carry):
                r0 = pl.multiple_of(j * CHUNK, CHUNK)
                rows = pl.ds(r0, CHUNK)
                dus = swiglu_bwd(uc_ref[rows, cols[0]].astype(F32), uc_ref[rows, cols[1]].astype(F32),
                                 dc_ref[rows, cols[0]].astype(F32))
                for h in range(2):
                    dbuf[rows, cols[h]] = dus[h]
                return carry

            lax.fori_loop(0, tm // CHUNK, step1, 0, unroll=2)
            dus = swiglu_bwd(un_ref[:, cols[0]].astype(F32), un_ref[:, cols[1]].astype(F32),
                             jnp.where(last, 0.0, dn_ref[:, cols[0]].astype(F32)))
            for h in range(2):
                dbuf[tm:tm + HALO, cols[h]] = dus[h]

            def step2(j, carry):
                r0 = pl.multiple_of(j * CHUNK, CHUNK)
                rows = pl.ds(r0, CHUNK)
                for h in range(2):
                    win = dbuf[pl.ds(r0, CHUNK + 8), cols[h]]
                    blocks = _blocks8(win)
                    d0 = win[0:CHUNK]
                    d1 = jnp.concatenate(_advance_rows(blocks, 1), axis=0)
                    d2 = jnp.concatenate(_advance_rows(blocks, 2), axis=0)
                    o_ref[rows, cols[h]] = (wk[h][2] * d0 + wk[h][1] * d1 + wk[h][0] * d2).astype(o_ref.dtype)
                    xv = x_ref[rows, cols[h]].astype(F32)
                    acc[2, :, cols[h]] += d0 * xv
                    acc[1, :, cols[h]] += d1 * xv
                    acc[0, :, cols[h]] += d2 * xv
                    acc[FFN_K, :, cols[h]] += d0
                return carry

            lax.fori_loop(0, tm // CHUNK, step2, 0)

        @pl.when(last)
        def _():
            for k in range(FFN_K):
                dw_ref[k:k + 1, :] = _colsum(acc[k])
            db_ref[...] = _colsum(acc[FFN_K])

    return _call_with_comm(
        body, nt,
        in_specs=[_rows(tm, c2), _rows(tm, c2), _next_rows(tm, HALO, c2, n),
                  _rows(tm, DFF), _next_rows(tm, HALO, DFF, n), _full((FFN_K, c2))],
        out_specs=[_rows(tm, c2), _full((FFN_K, c2)), _full((1, c2))],
        out_shape=[S((n, c2), BF), S((FFN_K, c2), F32), S((1, c2), F32)],
        scratch_shapes=[pltpu.VMEM((tm + HALO, c2), F32), pltpu.VMEM((FFN_K + 1, CHUNK, c2), F32)],
        args=(up, u, u, dact, dact, w), comm=comm, name=name)


def _head_consts():
    lane = jnp.arange(INNER) // HEAD_DIM
    rep = (jnp.arange(128)[:, None] == lane[None, :]).astype(BF)
    return rep, rep.T


def _split_dot(v, m):
    hi = v.astype(BF)
    lo = (v - hi.astype(F32)).astype(BF)
    return _dot(hi, m) + _dot(lo, m)


def _chunk_decay_terms(dt_raw, dtb, alog, rep):
    row = lax.broadcasted_iota(jnp.int32, (Q, Q), 0)
    col = lax.broadcasted_iota(jnp.int32, (Q, Q), 1)
    lower = row >= col
    upper = col >= row
    dt = _softplus(dt_raw + dtb)
    a = -jnp.exp(alog)
    adt = dt * a
    acum = lax.dot_general(lower.astype(F32), adt, NN, precision=HIGHEST, preferred_element_type=F32)
    acum_t = lax.dot_general(adt, upper.astype(F32), TN, precision=HIGHEST, preferred_element_type=F32)
    alast = acum[Q - 1:Q, :]
    e = jnp.exp(acum)
    f = jnp.exp(alast - acum)
    ex = _split_dot(jnp.concatenate([dt, e, f, jnp.broadcast_to(jnp.exp(alast), (8, 128))], axis=0), rep)
    return dict(lower=lower, upper=upper, dt=dt, a=a, acum=acum, acum_t=acum_t, alast=alast,
                dt_x=ex[0:Q], e_x=ex[Q:2 * Q], f_x=ex[2 * Q:3 * Q], cd_x=ex[3 * Q:3 * Q + 1])


def _block_diag2(v, lo):
    return jnp.concatenate([jnp.where(lo, v, 0.0), jnp.where(lo, 0.0, v)], axis=0).astype(BF)


def _ssd_fwd(xbc_act, proj, dt_bias, a_log, d_x, norm_g, *, name, comm=None):
    n = xbc_act.shape[0]
    nc = n // Q
    rep, _ = _head_consts()

    def body(xs_ref, bc_ref, dt_ref, z_ref, dtb_ref, alog_ref, dx_ref, ng_ref, rep_ref, y_ref, yn_ref, hp_ref,
             h_scr, y_scr):
        i = pl.program_id(0)

        @pl.when(i == 0)
        def _():
            h_scr[...] = jnp.zeros_like(h_scr)

        hp_ref[...] = h_scr[...]
        t = _chunk_decay_terms(dt_ref[...].astype(F32), dtb_ref[...], alog_ref[...], rep_ref[...])
        xs = xs_ref[...].astype(F32)
        xt = xs * t["dt_x"]
        lo = lax.broadcasted_iota(jnp.int32, (Q, 128), 1) < HEAD_DIM
        gw = INNER // GROUPS
        for g in range(GROUPS):
            bm = bc_ref[:, g * NSTATE:(g + 1) * NSTATE]
            cm = bc_ref[:, GROUPS * NSTATE + g * NSTATE:GROUPS * NSTATE + (g + 1) * NSTATE]
            cb = _dot(cm, bm, NT)
            hg = h_scr[:, g * gw:(g + 1) * gw]
            yoff = _dot(cm, hg.astype(BF))
            for jj in range(gw // 128):
                p = g * (gw // 128) + jj
                sl = slice(p * 128, (p + 1) * 128)
                ws = []
                for hd in (2 * p, 2 * p + 1):
                    seg = t["acum"][:, hd:hd + 1] - t["acum_t"][hd:hd + 1, :]
                    ws.append((cb * jnp.exp(jnp.where(t["lower"], seg, -jnp.inf))).astype(BF))
                ydiag = _dot(jnp.concatenate(ws, axis=1), _block_diag2(xt[:, sl], lo))
                y_scr[:, sl] = ydiag + yoff[:, jj * 128:(jj + 1) * 128] * t["e_x"][:, sl] + dx_ref[:, sl] * xs[:, sl]
            xf = (xt[:, g * gw:(g + 1) * gw] * t["f_x"][:, g * gw:(g + 1) * gw]).astype(BF)
            h_scr[:, g * gw:(g + 1) * gw] = hg * t["cd_x"][:, g * gw:(g + 1) * gw] + _dot(bm, xf, TN)
        y = y_scr[...]
        y_ref[...] = y.astype(y_ref.dtype)
        z = z_ref[...].astype(F32)
        v = y * z * _sigmoid(z)
        for g in range(GROUPS):
            vg = v[:, g * gw:(g + 1) * gw]
            r = lax.rsqrt(jnp.mean(vg * vg, axis=-1, keepdims=True) + EPS)
            yn_ref[:, g * gw:(g + 1) * gw] = (vg * r * ng_ref[:, g * gw:(g + 1) * gw]).astype(yn_ref.dtype)

    vec = _full((1, INNER))
    hv = _full((1, 128))
    return _call_with_comm(
        body, nc,
        in_specs=[_rows(Q, INNER, 0), _rows(Q, 2 * GROUPS * NSTATE, INNER // (2 * GROUPS * NSTATE)),
                  _rows(Q, DT_PAD, P_DT // DT_PAD), _rows(Q, INNER, P_Z // INNER),
                  hv, hv, vec, vec, _full((128, INNER))],
        out_specs=[_rows(Q, INNER), _rows(Q, INNER), pl.BlockSpec((None, NSTATE, INNER), lambda i: (i, 0, 0))],
        out_shape=[S((n, INNER), BF), S((n, INNER), BF), S((nc, NSTATE, INNER), F32)],
        scratch_shapes=[pltpu.VMEM((NSTATE, INNER), F32), pltpu.VMEM((Q, INNER), F32)],
        args=(xbc_act, xbc_act, proj, proj, dt_bias, a_log, d_x, norm_g, rep), comm=comm, name=name)


def _ssd_bwd(xbc_act, proj, y, dyn, hprev, dproj, dt_bias, a_log, d_x, norm_g, *, name):
    n = xbc_act.shape[0]
    nc = n // Q
    rep, sel = _head_consts()
    gw = INNER // GROUPS

    def rev(w, cb=0):
        return pl.BlockSpec((Q, w), lambda i: (nc - 1 - i, cb))

    def body(xs_ref, bc_ref, dt_ref, z_ref, y_ref, dyn_ref, hp_ref, dtb_ref, alog_ref, dx_ref, ng_ref, rep_ref,
             sel_ref, dproj_in, dz_ref, ddt_ref, dxbc_ref, dng_ref, ddtb_ref, dalog_ref, dd_ref,
             dh_scr, dxt_scr, st_scr, off_scr, rs_scr, cs_scr, dng_acc, ddtb_acc, da_acc, dd_acc):
        del dproj_in
        i = pl.program_id(0)

        @pl.when(i == 0)
        def _():
            for r in (dh_scr, dng_acc, ddtb_acc, da_acc, dd_acc):
                r[...] = jnp.zeros_like(r)

        y = y_ref[...].astype(F32)
        z = z_ref[...].astype(F32)
        sz = _sigmoid(z)
        silu = z * sz
        v = y * silu
        dyn = dyn_ref[...].astype(F32)
        dvs = []
        for g in range(GROUPS):
            gs = slice(g * gw, (g + 1) * gw)
            vg = v[:, gs]
            r = lax.rsqrt(jnp.mean(vg * vg, axis=-1, keepdims=True) + EPS)
            vn = vg * r
            dvn = dyn[:, gs] * ng_ref[:, gs]
            dng_acc[:, gs] += _colsum(dyn[:, gs] * vn)
            dvs.append(r * (dvn - vn * jnp.mean(dvn * vn, axis=-1, keepdims=True)))
        dv = jnp.concatenate(dvs, axis=1)
        dy = dv * silu
        dz_ref[...] = (dv * y * (sz * (1.0 + z * (1.0 - sz)))).astype(dz_ref.dtype)

        dt_raw = dt_ref[...].astype(F32)
        t = _chunk_decay_terms(dt_raw, dtb_ref[...], alog_ref[...], rep_ref[...])
        xs = xs_ref[...].astype(F32)
        dsk = dx_ref[...]
        dd_acc[...] += _colsum(dy * xs)
        xt = xs * t["dt_x"]
        dye = dy * t["e_x"]
        xtf = xt * t["f_x"]
        hp = hp_ref[...]
        dh = dh_scr[...]
        lo = lax.broadcasted_iota(jnp.int32, (Q, 128), 1) < HEAD_DIM
        rs_scr[...] = jnp.zeros_like(rs_scr)
        cs_scr[...] = jnp.zeros_like(cs_scr)
        for g in range(GROUPS):
            gs = slice(g * gw, (g + 1) * gw)
            bm = bc_ref[:, g * NSTATE:(g + 1) * NSTATE]
            cm = bc_ref[:, GROUPS * NSTATE + g * NSTATE:GROUPS * NSTATE + (g + 1) * NSTATE]
            cbt = _dot(bm, cm, NT)
            dhg = dh[:, gs].astype(BF)
            hpg = hp[:, gs].astype(BF)
            dxt_state = _dot(bm, dhg) * t["f_x"][:, gs]
            st_scr[:, gs] = dxt_state
            dye_g = dye[:, gs]
            off_scr[:, gs] = dye_g * _dot(cm, hpg)
            dye_b = dye_g.astype(BF)
            db = _dot(xtf[:, gs].astype(BF), dhg, NT)
            dc = _dot(dye_b, hpg, NT)
            dh_scr[:, gs] = t["cd_x"][:, gs] * dh[:, gs] + _dot(cm, dye_b, TN)
            dcbt = jnp.zeros((Q, Q), F32)
            for jj in range(gw // 128):
                p = g * (gw // 128) + jj
                sl = slice(p * 128, (p + 1) * 128)
                lts, wfs = [], []
                for hd in (2 * p, 2 * p + 1):
                    seg_t = t["acum_t"][hd:hd + 1, :] - t["acum"][:, hd:hd + 1]
                    lt = jnp.exp(jnp.where(t["upper"], seg_t, -jnp.inf))
                    lts.append(lt)
                    wfs.append(cbt * lt)
                dyp = dy[:, sl]
                dxt_diag = _dot(jnp.concatenate([w.astype(BF) for w in wfs], axis=1), _block_diag2(dyp, lo))
                dwt2 = _dot(_block_diag2(xt[:, sl], lo), dyp.astype(BF), NT)
                for k, hd in enumerate((2 * p, 2 * p + 1)):
                    dwt = dwt2[k * Q:(k + 1) * Q]
                    dcbt = dcbt + dwt * lts[k]
                    mt = dwt * wfs[k]
                    rs_scr[hd:hd + 1, :] = _colsum(mt)
                    cs_scr[:, hd:hd + 1] = jnp.sum(mt, axis=1, keepdims=True)
                dxt_scr[:, sl] = dxt_diag + dxt_state[:, jj * 128:(jj + 1) * 128]
            dcbt_b = dcbt.astype(BF)
            db = db + _dot(dcbt_b, cm)
            dc = dc + _dot(dcbt_b, bm, TN)
            dxbc_ref[:, INNER + g * NSTATE:INNER + (g + 1) * NSTATE] = db.astype(dxbc_ref.dtype)
            dxbc_ref[:, INNER + (GROUPS + g) * NSTATE:INNER + (GROUPS + g + 1) * NSTATE] = dc.astype(dxbc_ref.dtype)
        dxt = dxt_scr[...]
        dst = st_scr[...]
        sel_m = sel_ref[...]
        sums = _split_dot(jnp.concatenate([off_scr[...], xs * dst, xs * dxt], axis=0), sel_m)
        r1_off, r3_state, r3 = sums[0:Q], sums[Q:2 * Q], sums[2 * Q:3 * Q]
        t1 = _colsum(xt * dst)
        t2 = _colsum(dh * hp)
        tails = _split_dot(jnp.concatenate([jnp.broadcast_to(t1, (8, INNER)), jnp.broadcast_to(t2, (8, INNER))], axis=0),
                           sel_m)
        extra = tails[0:1] + jnp.exp(t["alast"]) * tails[8:9]
        last_row = lax.broadcasted_iota(jnp.int32, (Q, 128), 0) == Q - 1
        da_cum = (rs_scr[...].T - cs_scr[...]) + r1_off - t["dt"] * r3_state + jnp.where(last_row, extra, 0.0)
        dadt = lax.dot_general(t["upper"].astype(F32), da_cum, NN, precision=HIGHEST, preferred_element_type=F32)
        ddt = r3 + t["a"] * dadt
        da_acc[...] += _colsum(dadt * t["dt"])
        real = lax.broadcasted_iota(jnp.int32, (Q, 128), 1) < HEADS
        ddraw = jnp.where(real, ddt * _sigmoid(dt_raw + dtb_ref[...]), 0.0)
        ddt_ref[...] = ddraw.astype(ddt_ref.dtype)
        ddtb_acc[...] += _colsum(ddraw)
        dxbc_ref[:, 0:INNER] = (dy * dsk + dxt * t["dt_x"]).astype(dxbc_ref.dtype)

        @pl.when(i == nc - 1)
        def _():
            dng_ref[...] = dng_acc[...]
            ddtb_ref[...] = ddtb_acc[...]
            dalog_ref[...] = da_acc[...] * t["a"]
            dd_ref[...] = _split_dot(jnp.broadcast_to(dd_acc[...], (8, INNER)), sel_m)[0:1]

    vec = _full((1, INNER))
    hv = _full((1, 128))
    return pl.pallas_call(
        body, grid=(nc,),
        in_specs=[rev(INNER, 0), rev(2 * GROUPS * NSTATE, INNER // (2 * GROUPS * NSTATE)),
                  rev(DT_PAD, P_DT // DT_PAD), rev(INNER, P_Z // INNER), rev(INNER), rev(INNER),
                  pl.BlockSpec((None, NSTATE, INNER), lambda i: (nc - 1 - i, 0, 0)),
                  hv, hv, vec, vec, _full((128, INNER)), _full((INNER, 128)), pl.BlockSpec(memory_space=pl.ANY)],
        out_specs=[rev(INNER, P_Z // INNER), rev(DT_PAD), rev(XBC), vec, hv, hv, hv],
        out_shape=[S(dproj.shape, dproj.dtype), S((n, DT_PAD), BF), S((n, XBC), BF),
                   S((1, INNER), F32), S((1, 128), F32), S((1, 128), F32), S((1, 128), F32)],
        scratch_shapes=[pltpu.VMEM((NSTATE, INNER), F32), pltpu.VMEM((Q, INNER), F32), pltpu.VMEM((Q, INNER), F32),
                        pltpu.VMEM((Q, INNER), F32), pltpu.VMEM((128, Q), F32), pltpu.VMEM((Q, 128), F32),
                        pltpu.VMEM((1, INNER), F32), pltpu.VMEM((1, 128), F32), pltpu.VMEM((1, 128), F32),
                        pltpu.VMEM((1, INNER), F32)],
        input_output_aliases={13: 0},
        compiler_params=_cparams("arbitrary"), name=name)(
            xbc_act, xbc_act, proj, proj, y, dyn, hprev, dt_bias, a_log, d_x, norm_g, rep, sel, dproj)


def _mixer_out_fwd(yn, a_conf, a_sc, proj, b_gate, w_ssd, w_conf, w_sc, w_o, x, mod, *, name):
    n = x.shape[0]
    tm = _tile(n, 256)

    def body(yn_ref, ac_ref, as_ref, gt_ref, bg_ref, wa_ref, wb_ref, wc_ref, wo_ref, x_ref, mod_ref,
             ya_ref, yb_ref, yc_ref, mg_ref, mix_ref, xn_ref):
        ya = _dot(yn_ref[...], wa_ref[...])
        yb = _dot(ac_ref[...], wb_ref[...])
        yc = _dot(as_ref[...], wc_ref[...])
        ya_ref[...] = ya.astype(ya_ref.dtype)
        yb_ref[...] = yb.astype(yb_ref.dtype)
        yc_ref[...] = yc.astype(yc_ref.dtype)
        g = _sigmoid(gt_ref[...].astype(F32) + bg_ref[...])
        merged = (g[:, 0:D] * ya + g[:, D:2 * D] * yb + g[:, 2 * D:] * yc).astype(mg_ref.dtype)
        mg_ref[...] = merged
        mix = _dot(merged, wo_ref[...])
        mix_ref[...] = mix.astype(mix_ref.dtype)
        xn_ref[...] = x_ref[...] + mod_ref[2:3, :] * mix

    act = S((n, D), BF)
    return pl.pallas_call(
        body, grid=(n // tm,),
        in_specs=[_rows(tm, INNER), _rows(tm, CONF_W), _rows(tm, SC_W), _rows(tm, 3 * D, P_GATES // (3 * D)),
                  _full((1, 3 * D)), _full((INNER, D)), _full((CONF_W, D)), _full((SC_W, D)), _full((D, D)),
                  _rows(tm, D), _full((3, D))],
        out_specs=[_rows(tm, D)] * 6,
        out_shape=[act, act, act, act, act, S((n, D), F32)],
        compiler_params=_cparams("parallel"), name=name)(yn, a_conf, a_sc, proj, b_gate, w_ssd, w_conf, w_sc, w_o, x, mod)


def _mixer_out_bwd(dx, mix, ya, yb, yc, proj, b_gate, w_ssd, w_conf, w_sc, w_o, mod, *, name):
    n = dx.shape[0]
    tm = _tile(n, 256)

    def body(dx_ref, mix_ref, ya_ref, yb_ref, yc_ref, gt_ref, bg_ref, wa_ref, wb_ref, wc_ref, wo_ref, mod_ref,
             do_ref, dya_ref, dyb_ref, dyc_ref, dgt_ref, dyn_ref, dac_ref, das_ref, dgm_ref, dbg_ref, dm_scr, acc):
        i = pl.program_id(0)
        last = i == pl.num_programs(0) - 1

        @pl.when(i == 0)
        def _():
            acc[...] = jnp.zeros_like(acc)

        do_ref[...] = (dx_ref[...] * mod_ref[2:3, :]).astype(BF)
        dm_scr[...] = _dot(do_ref[...], wo_ref[...], NT)
        branches = ((ya_ref, dya_ref), (yb_ref, dyb_ref), (yc_ref, dyc_ref))
        for s_ in range(D // STRIP):
            c = pl.ds(s_ * STRIP, STRIP)

            def step(j, carry):
                rows = pl.ds(pl.multiple_of(j * CHUNK, CHUNK), CHUNK)
                dmv = dm_scr[rows, c]
                acc[3, :, c] += dx_ref[rows, c] * mix_ref[rows, c].astype(F32)
                for jb, (y_ref, o_ref) in enumerate(branches):
                    cj = pl.ds(jb * D + s_ * STRIP, STRIP)
                    gj = _sigmoid(gt_ref[rows, cj].astype(F32) + bg_ref[:, cj])
                    o_ref[rows, c] = (dmv * gj).astype(o_ref.dtype)
                    dgpre = dmv * y_ref[rows, c].astype(F32) * gj * (1.0 - gj)
                    dgt_ref[rows, cj] = dgpre.astype(dgt_ref.dtype)
                    acc[jb, :, c] += dgpre
                return carry

            lax.fori_loop(0, tm // CHUNK, step, 0)
        dyn_ref[...] = _dot(dya_ref[...], wa_ref[...], NT).astype(dyn_ref.dtype)
        dac_ref[...] = _dot(dyb_ref[...], wb_ref[...], NT).astype(dac_ref.dtype)
        das_ref[...] = _dot(dyc_ref[...], wc_ref[...], NT).astype(das_ref.dtype)

        @pl.when(last)
        def _():
            for jb in range(3):
                dbg_ref[:, jb * D:(jb + 1) * D] = _colsum(acc[jb])
            dgm_ref[...] = _colsum(acc[3])

    act = S((n, D), BF)
    return pl.pallas_call(
        body, grid=(n // tm,),
        in_specs=[_rows(tm, D)] * 5 + [_rows(tm, 3 * D, P_GATES // (3 * D)), _full((1, 3 * D)), _full((INNER, D)),
                                       _full((CONF_W, D)), _full((SC_W, D)), _full((D, D)), _full((3, D))],
        out_specs=[_rows(tm, D)] * 4 + [_rows(tm, 3 * D, P_GATES // (3 * D)), _rows(tm, INNER), _rows(tm, CONF_W),
                                        _rows(tm, SC_W), _full((1, D)), _full((1, 3 * D))],
        out_shape=[act, act, act, act, S((n, PW), BF), S((n, INNER), BF), S((n, CONF_W), BF), S((n, SC_W), BF),
                   S((1, D), F32), S((1, 3 * D), F32)],
        scratch_shapes=[pltpu.VMEM((tm, D), F32), pltpu.VMEM((4, CHUNK, D), F32)],
        compiler_params=_cparams("arbitrary"), name=name)(dx, mix, ya, yb, yc, proj, b_gate, w_ssd, w_conf, w_sc, w_o, mod)


def _pad_w_in(w):
    zeros = jnp.zeros((w.shape[0], XBC_PAD - XBC - (R_CONF - R_DT)), w.dtype)
    return jnp.concatenate([w[:, R_GATES:], w[:, R_SC:R_GATES], w[:, R_XBC:R_DT], w[:, R_DT:R_CONF], zeros,
                            w[:, R_Z:R_XBC], w[:, R_CONF:R_SC]], axis=1)


def _unpad_w_in(wp):
    return jnp.concatenate([wp[:, P_Z:P_Z + INNER], wp[:, P_XBC:P_XBC + XBC], wp[:, P_DT:P_DT + HEADS],
                            wp[:, P_CONF:P_CONF + 2 * CONF_W], wp[:, P_SC:P_SC + 3 * SC_W], wp[:, P_GATES:P_GATES + 3 * D]],
                           axis=1)


W_IN_SHARD = N_IN // N_CHIPS
W_IN_SEGMENTS = ((R_Z, R_XBC, P_Z), (R_XBC, R_DT, P_XBC), (R_DT, R_CONF, P_DT), (R_CONF, R_SC, P_CONF),
                 (R_SC, R_GATES, P_SC), (R_GATES, N_IN, P_GATES))


def _pad_w_in_chips(w4):
    parts = []
    for lo, hi, dst in sorted(W_IN_SEGMENTS, key=lambda sgm: sgm[2]):
        for k in range(N_CHIPS):
            a, b = max(lo, k * W_IN_SHARD), min(hi, (k + 1) * W_IN_SHARD)
            if a < b:
                parts.append((dst + a - lo, w4[k][:, a - k * W_IN_SHARD:b - k * W_IN_SHARD]))
    out, pos = [], 0
    for start, piece in parts:
        if start > pos:
            out.append(jnp.zeros((w4.shape[1], start - pos), w4.dtype))
        out.append(piece)
        pos = start + piece.shape[1]
    if pos < PW:
        out.append(jnp.zeros((w4.shape[1], PW - pos), w4.dtype))
    return jnp.concatenate(out, axis=1)


def _unpad_w_in_chips(wp):
    blocks = []
    for k in range(N_CHIPS):
        pieces = []
        for lo, hi, dst in W_IN_SEGMENTS:
            a, b = max(lo, k * W_IN_SHARD), min(hi, (k + 1) * W_IN_SHARD)
            if a < b:
                pieces.append(wp[:, dst + a - lo:dst + b - lo])
        blocks.append(jnp.concatenate(pieces, axis=1))
    return jnp.stack(blocks)


def _row(v):
    return v.reshape(1, -1)


def _head_row(v):
    return jnp.pad(v, (0, 128 - HEADS)).reshape(1, 128)


def _layer_fwd(x, p, mod_mix, mod_ffn, tag, comm=None, comm_ssd=None, late_params=None, comm_up=None, comm_proj=None,
               mid_params=None):
    sv = {"x0": x}
    h = _modnorm_fwd(x, _row(p["norm_mix_g"]), mod_mix, name=f"modnorm_mix_fwd{tag}")
    proj = _mm_nn(h, p["w_in_pad"], out_dtype=BF, tm=1024, tn=2048, name=f"proj_fwd{tag}", comm=comm_proj)
    if comm_proj is not None:
        proj, proj_comm_out = proj
        p.update(mid_params(proj_comm_out))
    xbc_act, cpre = _ssd_pre_fwd(proj, p["ssd_conv_w"], _row(p["ssd_conv_b"]), name=f"ssd_pre_fwd{tag}")
    d_x = _row(jnp.repeat(p["ssd_d"], HEAD_DIM))
    (y, yn, hprev), ssd_comm_out = _ssd_fwd(xbc_act, proj, _head_row(p["ssd_dt_bias"]), _head_row(p["ssd_a_log"]), d_x,
                                            _row(p["ssd_norm_g"]), name=f"ssd_fwd{tag}", comm=comm_ssd)
    if late_params is not None:
        p.update(late_params(ssd_comm_out))
    a_conf, uc = _conf_fwd(proj, p["conf_conv_w"], _row(p["conf_conv_b"]), _row(p["conf_ln_g"]), _row(p["conf_ln_b"]),
                           name=f"conf_fwd{tag}")
    a_sc = _sc_fwd(proj, p["sc_conv_w"], name=f"sc_fwd{tag}")
    ya, yb, yc, merged, mix, x1 = _mixer_out_fwd(yn, a_conf, a_sc, proj, _row(p["b_gate"]), p["w_ssd_out"],
                                                 p["w_conf_out"], p["w_sc_out"], p["w_o"], x, mod_mix,
                                                 name=f"mixer_out_fwd{tag}")
    h2 = _modnorm_fwd(x1, _row(p["norm_ffn_g"]), mod_ffn, name=f"modnorm_ffn_fwd{tag}")
    up = _mm_nn_chips(h2, p["w_up4"], out_dtype=BF, tm=1024, name=f"up_fwd{tag}", comm=comm_up)
    up_comm_out = ()
    if comm_up is not None:
        up, up_comm_out = up
    (act, u_ffn), comm_out = _ffn_act_fwd(up, p["ffn_conv_w"], _row(p["ffn_conv_b"]), name=f"ffn_act_fwd{tag}", comm=comm)
    o, x2 = _mm_resid(act, p["w_down"], x1, mod_ffn, tm=512, name=f"down_fwd{tag}")
    sv.update(h=h, proj=proj, xbc_act=xbc_act, cpre=cpre, d_x=d_x, y=y, yn=yn, hprev=hprev, a_conf=a_conf, uc=uc, a_sc=a_sc,
              ya=ya, yb=yb, yc=yc, merged=merged, mix=mix, x1=x1, h2=h2, up=up, u_ffn=u_ffn, act=act, o=o)
    return x2, sv, tuple(up_comm_out) + tuple(comm_out)


def _layer_bwd(dx, p, sv, mod_mix, mod_ffn, tag, comm=None, make_comm_conf=None):
    g = {}
    do2, dgate_ffn = _gate_bwd(dx, sv["o"], mod_ffn, name=f"gate_ffn_bwd{tag}")
    dact = _mm_nt(do2, p["w_down"], out_dtype=BF, tm=1024, tk=D, name=f"down_dx{tag}")
    g["w_down"] = _mm_tn(sv["act"], do2, tn=D, tk=1024, name=f"down_dw{tag}")
    (dup, g["ffn_conv_w"], dffn_b), comm_out = _ffn_act_bwd(sv["up"], sv["u_ffn"], dact, p["ffn_conv_w"],
                                                            name=f"ffn_act_bwd{tag}", comm=comm)
    g["ffn_conv_b"] = dffn_b[0]
    dh2 = _mm_nt_chips(dup, p["w_up4"], out_dtype=F32, tm=1024, name=f"up_dx{tag}")
    g["w_up4"] = _mm_tn(sv["h2"], dup, tn=2 * DFF // N_CHIPS, tk=2048, by_chip=True, name=f"up_dw{tag}")
    dx1, dgn, dsh, dsc = _modnorm_bwd(dh2, sv["x1"], dx, _row(p["norm_ffn_g"]), mod_ffn, name=f"modnorm_ffn_bwd{tag}")
    g["norm_ffn_g"] = dgn[0]
    dmod_ffn = jnp.concatenate([dsh[0], dsc[0], dgate_ffn[0]])

    (do1, dya, dyb, dyc, dproj, dyn, dac, dasc, dgate_mix, dbg) = _mixer_out_bwd(
        dx1, sv["mix"], sv["ya"], sv["yb"], sv["yc"], sv["proj"], _row(p["b_gate"]), p["w_ssd_out"], p["w_conf_out"],
        p["w_sc_out"], p["w_o"], mod_mix, name=f"mixer_out_bwd{tag}")
    g["b_gate"] = dbg[0]
    g["w_o"] = _mm_tn(sv["merged"], do1, tn=D, tk=2048, name=f"wo_dw{tag}")
    g["w_ssd_out"] = _mm_tn(sv["yn"], dya, tn=D, tk=2048, name=f"wssd_dw{tag}")
    g["w_conf_out"] = _mm_tn(sv["a_conf"], dyb, tn=D, tk=2048, name=f"wconf_dw{tag}")
    g["w_sc_out"] = _mm_tn(sv["a_sc"], dyc, tn=D, tk=2048, name=f"wsc_dw{tag}")
    comm_conf = make_comm_conf(g) if make_comm_conf is not None else None
    (dproj, g["conf_conv_w"], dcb, dlg, dlb), conf_comm_out = _conf_bwd(
        sv["proj"], sv["uc"], dac, dproj, p["conf_conv_w"], _row(p["conf_ln_g"]), _row(p["conf_ln_b"]),
        name=f"conf_bwd{tag}", comm=comm_conf)
    g["conf_conv_b"], g["conf_ln_g"], g["conf_ln_b"] = dcb[0], dlg[0], dlb[0]
    dproj, g["sc_conv_w"] = _sc_bwd(sv["proj"], dasc, dproj, p["sc_conv_w"], name=f"sc_bwd{tag}")
    dproj, ddt, dxbc_act, dng, ddtb, dalog, ddd = _ssd_bwd(
        sv["xbc_act"], sv["proj"], sv["y"], dyn, sv["hprev"], dproj, _head_row(p["ssd_dt_bias"]),
        _head_row(p["ssd_a_log"]), sv["d_x"], _row(p["ssd_norm_g"]), name=f"ssd_bwd{tag}")
    g["ssd_norm_g"], g["ssd_dt_bias"], g["ssd_a_log"], g["ssd_d"] = dng[0], ddtb[0, :HEADS], dalog[0, :HEADS], ddd[0, :HEADS]
    dproj, g["ssd_conv_w"], dsb = _ssd_pre_bwd(sv["proj"], sv["cpre"], dxbc_act, ddt, dproj, p["ssd_conv_w"],
                                               name=f"ssd_pre_bwd{tag}")
    g["ssd_conv_b"] = dsb[0]
    dh = _mm_nt(dproj, p["w_in_pad"], out_dtype=F32, tm=1024, tk=4096, name=f"proj_dx{tag}")
    g["w_in_pad"] = _mm_tn(sv["h"], dproj, tn=2048, tk=2048, name=f"proj_dw{tag}")
    dx0, dgn, dsh, dsc = _modnorm_bwd(dh, sv["x0"], dx1, _row(p["norm_mix_g"]), mod_mix, name=f"modnorm_mix_bwd{tag}")
    g["norm_mix_g"] = dgn[0]
    dmod_mix = jnp.concatenate([dsh[0], dsc[0], dgate_mix[0]])
    return dx0, g, dmod_mix, dmod_ffn, (comm_out, conf_comm_out)


def _local_step(x, target, layers, mods, final_norm_g):
    saved = []
    for i, p in enumerate(layers):
        x, sv, _ = _layer_fwd(x, p, mods[i][0], mods[i][1], f"_l{i}")
        saved.append(sv)
    dx, loss, dfg = _final_loss(x, _row(final_norm_g), target, name="final_loss")
    grads, dmods = [None] * len(layers), [None] * len(layers)
    for i in reversed(range(len(layers))):
        dx, grads[i], dmm, dmf, _ = _layer_bwd(dx, layers[i], saved[i], mods[i][0], mods[i][1], f"_l{i}")
        dmods[i] = (dmm, dmf)
    return loss, dx, grads, dmods, dfg[0]


MESH = pl.DeviceIdType.MESH
ANY = pl.BlockSpec(memory_space=pl.ANY)
VMEM = pl.BlockSpec(memory_space=pltpu.VMEM)


def _mesh_pos():
    return lax.axis_index("x"), lax.axis_index("y"), lax.axis_index("c")


def _peer(pos, mask):
    return tuple(1 - v if (mask >> (2 - k)) & 1 else v for k, v in enumerate(pos))


def _lin(pos):
    return 4 * pos[0] + 2 * pos[1] + pos[2]


def _chip(pos):
    return 2 * pos[0] + pos[1]


def _rcopy(src, dst, send_sem, recv_sem, dev):
    return pltpu.make_async_remote_copy(src_ref=src, dst_ref=dst, send_sem=send_sem, recv_sem=recv_sem,
                                        device_id=dev, device_id_type=MESH)


CHIP_MASKS = (2, 4, 6)
SIBLING = 1
ADA_COLS = 3 * D // N_CHIPS
CONV_ROWS, CONV_COLS = 48, 2 * DFF // N_CHIPS
CONV_PACK = {"ffn_conv_w": (0, FFN_K, 2 * DFF // N_CHIPS), "ssd_conv_w": (3, SSD_K, XBC // N_CHIPS),
             "conf_conv_w": (8, CONF_K, CONF_W // N_CHIPS), "sc_conv_w": (40, SC_K, SC_W // N_CHIPS)}


def _ada_exchange(c_blk, ada_mix_w, ada_ffn_w, conv_pack):
    def body(c_ref, wm_ref, wf_ref, cw_ref, mods_ref, sc_ref, cwall_ref,
             call_scr, modp_scr, recv_scr, s1, r1, s3, r3, s4, r4):
        pos = _mesh_pos()
        me, km = _lin(pos), _chip(pos)
        call_scr[me] = c_ref[...]
        cwall_ref[km] = cw_ref[...]
        sends = []
        for m in range(1, N_DEV):
            sends.append(_rcopy(c_ref, call_scr.at[me], s1.at[m - 1], r1.at[m - 1], _peer(pos, m)))
        for j, m in enumerate(CHIP_MASKS):
            sends.append(_rcopy(cw_ref, cwall_ref.at[km], s4.at[j], r4.at[j], _peer(pos, m)))
        for cp in sends:
            cp.start()
        for m in range(1, N_DEV):
            src = _peer(pos, m)
            _rcopy(c_ref, call_scr.at[_lin(src)], s1.at[m - 1], r1.at[m - 1], src).wait_recv()
        cm = jnp.concatenate([call_scr[d, 0:1, :] for d in range(N_DEV)], axis=0)
        sc = cm * _sigmoid(cm)
        sc_ref[...] = sc
        for j, w in enumerate((wm_ref.at[0], wm_ref.at[1], wf_ref.at[0], wf_ref.at[1])):
            modp_scr[:, j * ADA_COLS:(j + 1) * ADA_COLS] = lax.dot_general(
                sc, w[...], NN, precision=HIGHEST, preferred_element_type=F32)
        recv_scr[km] = modp_scr[...]
        sends3 = [_rcopy(modp_scr, recv_scr.at[km], s3.at[j], r3.at[j], _peer(pos, m)) for j, m in enumerate(CHIP_MASKS)]
        for cp in sends3:
            cp.start()
        for j, m in enumerate(CHIP_MASKS):
            src = _peer(pos, m)
            _rcopy(modp_scr, recv_scr.at[_chip(src)], s3.at[j], r3.at[j], src).wait_recv()
            _rcopy(cw_ref, cwall_ref.at[_chip(src)], s4.at[j], r4.at[j], src).wait_recv()
        for k in range(N_CHIPS):
            mods_ref[k:k + 1, :] = recv_scr[k, pl.ds(me, 1), :]
        for cp in sends + sends3:
            cp.wait_send()

    dma = pltpu.SemaphoreType.DMA
    return pl.pallas_call(
        body, in_specs=[VMEM] * 4, out_specs=[VMEM] * 3,
        out_shape=[S((N_CHIPS, 4 * ADA_COLS), F32), S((N_DEV, D), F32), S((N_CHIPS,) + conv_pack.shape, F32)],
        scratch_shapes=[pltpu.VMEM((N_DEV, 8, D), F32), pltpu.VMEM((N_DEV, 4 * ADA_COLS), F32),
                        pltpu.VMEM((N_CHIPS, N_DEV, 4 * ADA_COLS), F32),
                        dma((N_DEV - 1,)), dma((N_DEV - 1,)), dma((3,)), dma((3,)), dma((3,)), dma((3,))],
        compiler_params=pltpu.CompilerParams(vmem_limit_bytes=VMEM_LIMIT_V7X), name="ada_exchange")(
            c_blk, ada_mix_w, ada_ffn_w, conv_pack)


class _Comm:
    def __init__(self, ins, out_shapes, scratch, start, finish, mid=None):
        self.ins, self.out_shapes, self.scratch, self.start, self.finish = list(ins), list(out_shapes), list(scratch), start, finish
        self.mid = mid


def _call_with_comm(body, nsteps, *, in_specs, out_specs, out_shape, scratch_shapes, args, comm, name, aliases=None):
    grid = nsteps if isinstance(nsteps, tuple) else (nsteps,)
    sem = ("arbitrary",) * len(grid)
    if comm is None:
        res = pl.pallas_call(body, grid=grid, in_specs=in_specs, out_specs=out_specs, out_shape=out_shape,
                             scratch_shapes=scratch_shapes, input_output_aliases=aliases or {},
                             compiler_params=_cparams(*sem), name=name)(*args)
        return tuple(res), ()
    ni, no, ns = len(in_specs), len(out_specs), len(scratch_shapes)
    ci, co = len(comm.ins), len(comm.out_shapes)

    def hosted(*refs):
        ins, cins = refs[:ni], refs[ni:ni + ci]
        outs, couts = refs[ni + ci:ni + ci + no], refs[ni + ci + no:ni + ci + no + co]
        scr, csem = refs[ni + ci + no + co:ni + ci + no + co + ns], refs[ni + ci + no + co + ns:]
        step, total = pl.program_id(0), grid[0]
        for d in range(1, len(grid)):
            step, total = step * grid[d] + pl.program_id(d), total * grid[d]

        @pl.when(step == 0)
        def _():
            comm.start(cins, couts, csem)

        body(*ins, *outs, *scr)
        if comm.mid is not None:
            @pl.when(step == max(0, total - 1 - max(2, total // 6)))
            def _():
                comm.mid(cins, couts, csem)

        @pl.when(step == total - 1)
        def _():
            comm.finish(cins, couts, csem)

    res = pl.pallas_call(
        hosted, grid=grid, in_specs=list(in_specs) + [ANY] * ci, out_specs=list(out_specs) + [ANY] * co,
        out_shape=list(out_shape) + comm.out_shapes, scratch_shapes=list(scratch_shapes) + comm.scratch,
        input_output_aliases=aliases or {}, compiler_params=_cparams(*sem), name=name)(*args, *comm.ins)
    return tuple(res[:no]), tuple(res[no:])


def _run_comm(comm, name):
    def body(*refs):
        ci, co = len(comm.ins), len(comm.out_shapes)
        comm.start(refs[:ci], refs[ci:ci + co], refs[ci + co:])
        if comm.mid is not None:
            comm.mid(refs[:ci], refs[ci:ci + co], refs[ci + co:])
        comm.finish(refs[:ci], refs[ci:ci + co], refs[ci + co:])

    return pl.pallas_call(body, in_specs=[ANY] * len(comm.ins), out_specs=[ANY] * len(comm.out_shapes),
                          out_shape=comm.out_shapes, scratch_shapes=comm.scratch, name=name)(*comm.ins)


def _gather_comm(shards, layer):
    na = len(shards)
    dma = pltpu.SemaphoreType.DMA

    def ici(ins, outs, sems, pos, a, j, m, sender):
        ssem, rsem = sems[0], sems[1]
        peer = _peer(pos, m)
        block = outs[a].at[_chip(pos) if sender else _chip(peer)]
        return _rcopy(ins[a].at[layer], block, ssem.at[a, j], rsem.at[a, j], peer)

    def start(ins, outs, sems):
        pos = _mesh_pos()

        @pl.when(pos[2] == layer)
        def _():
            for a in range(na):
                for j, m in enumerate(CHIP_MASKS):
                    ici(ins, outs, sems, pos, a, j, m, True).start()

    def forward(outs, sems, pos, a, j, m):
        blk = outs[a].at[_chip(_peer(pos, m))]
        return _rcopy(blk, blk, sems[2].at[a, j], sems[3].at[a, j], _peer(pos, SIBLING))

    def mid(ins, outs, sems):
        pos = _mesh_pos()

        @pl.when(pos[2] == layer)
        def _():
            for a in range(na):
                for j, m in enumerate(CHIP_MASKS):
                    ici(ins, outs, sems, pos, a, j, m, False).wait_recv()
                    forward(outs, sems, pos, a, j, m).start()

    def finish(ins, outs, sems):
        pos = _mesh_pos()

        @pl.when(pos[2] == layer)
        def _():
            for a in range(na):
                for j, m in enumerate(CHIP_MASKS):
                    ici(ins, outs, sems, pos, a, j, m, True).wait_send()
                    forward(outs, sems, pos, a, j, m).wait_send()

        @pl.when(pos[2] != layer)
        def _():
            for a in range(na):
                for j, m in enumerate(CHIP_MASKS):
                    forward(outs, sems, pos, a, j, m).wait_recv()

    return _Comm(shards, [S((N_CHIPS,) + s.shape[1:], s.dtype) for s in shards],
                 [dma((na, 3)), dma((na, 3)), dma((na, 3)), dma((na, 3))], start, finish, mid)


def _scatter_comm(arrs, layer):
    na = len(arrs)
    dma = pltpu.SemaphoreType.DMA

    def copy(ins, outs, sems, pos, a, j, m, sender):
        peer = _peer(pos, m)
        src = ins[a].at[_chip(peer) if sender else _chip(pos)]
        dst = outs[a].at[_chip(pos) if sender else _chip(peer)]
        return _rcopy(src, dst, sems[0].at[a, j], sems[1].at[a, j], peer)

    def start(ins, outs, sems):
        pos = _mesh_pos()

        @pl.when(pos[2] == layer)
        def _():
            for a in range(na):
                for j, m in enumerate(CHIP_MASKS):
                    copy(ins, outs, sems, pos, a, j, m, True).start()

    def finish(ins, outs, sems):
        pos = _mesh_pos()

        @pl.when(pos[2] == layer)
        def _():
            for a in range(na):
                for j, m in enumerate(CHIP_MASKS):
                    copy(ins, outs, sems, pos, a, j, m, False).wait_recv()
            for a in range(na):
                for j, m in enumerate(CHIP_MASKS):
                    copy(ins, outs, sems, pos, a, j, m, True).wait_send()

    return _Comm(arrs, [S(s.shape, s.dtype) for s in arrs], [dma((na, 3)), dma((na, 3))], start, finish)


def _swap_layer(arrs, layer, tag):
    na = len(arrs)

    def body(*refs):
        ins, outs = refs[:na], refs[na:2 * na]
        ssem, rsem = refs[2 * na:]
        pos = _mesh_pos()
        sib = _peer(pos, SIBLING)
        cps = [_rcopy(ins[a], outs[a], ssem.at[a], rsem.at[a], sib) for a in range(na)]

        @pl.when(pos[2] != layer)
        def _():
            for cp in cps:
                cp.start()
            for cp in cps:
                cp.wait_send()

        @pl.when(pos[2] == layer)
        def _():
            for cp in cps:
                cp.wait_recv()

    dma = pltpu.SemaphoreType.DMA
    return pl.pallas_call(
        body, in_specs=[ANY] * na, out_specs=[ANY] * na, out_shape=[S(s.shape, s.dtype) for s in arrs],
        scratch_shapes=[dma((na,)), dma((na,))], name=f"swap_layer{tag}")(*arrs)


def _share_sibling(arrs):
    na = len(arrs)

    def body(*refs):
        bufs = refs[na:2 * na]
        ssem, rsem = refs[2 * na:]
        pos = _mesh_pos()
        c = pos[2]
        sib = _peer(pos, SIBLING)
        sends = [_rcopy(bufs[a].at[c], bufs[a].at[c], ssem.at[a], rsem.at[a], sib) for a in range(na)]
        for cp in sends:
            cp.start()
        for a in range(na):
            _rcopy(bufs[a].at[c], bufs[a].at[1 - c], ssem.at[a], rsem.at[a], sib).wait_recv()
        for cp in sends:
            cp.wait_send()

    dma = pltpu.SemaphoreType.DMA
    return pl.pallas_call(
        body, in_specs=[ANY] * na, out_specs=[ANY] * na, out_shape=[S(s.shape, s.dtype) for s in arrs],
        input_output_aliases={a: a for a in range(na)},
        scratch_shapes=[dma((na,)), dma((na,))], name="share_sibling")(*arrs)


def _small_allreduce(vec):
    r = vec.shape[0]

    def body(v_ref, sum_ref, all_ref, ssem, rsem):
        pos = _mesh_pos()
        me = _lin(pos)
        all_ref[me] = v_ref[...]
        cps = [_rcopy(v_ref, all_ref.at[me], ssem.at[m - 1], rsem.at[m - 1], _peer(pos, m)) for m in range(1, N_DEV)]
        for cp in cps:
            cp.start()
        for m in range(1, N_DEV):
            src = _peer(pos, m)
            _rcopy(v_ref, all_ref.at[_lin(src)], ssem.at[m - 1], rsem.at[m - 1], src).wait_recv()
        acc = all_ref[0]
        for d in range(1, N_DEV):
            acc = acc + all_ref[d]
        sum_ref[...] = acc
        for cp in cps:
            cp.wait_send()

    dma = pltpu.SemaphoreType.DMA
    return pl.pallas_call(
        body, in_specs=[VMEM], out_specs=[VMEM, VMEM],
        out_shape=[S((r, 128), F32), S((N_DEV, r, 128), F32)],
        scratch_shapes=[dma((N_DEV - 1,)), dma((N_DEV - 1,))],
        compiler_params=pltpu.CompilerParams(vmem_limit_bytes=VMEM_LIMIT_V7X), name="small_allreduce")(vec)


ROW_BYTES_TARGET = 1 << 20


def _row_tile(rows, cols, itemsize=4, align=16, target=ROW_BYTES_TARGET):
    fits = [t for t in range(align, rows + 1, align) if rows % t == 0]
    under = [t for t in fits if t * cols * itemsize <= target]
    return max(under) if under else (min(fits) if fits else rows)


def _pair_add(g, other):
    r, cdim = g.shape
    tr = _row_tile(r, cdim, 1)

    def body(g_ref, o_ref, out_ref):
        out_ref[...] = (g_ref[...].astype(F32) + o_ref[...].astype(F32)).astype(out_ref.dtype)

    blk = pl.BlockSpec((tr, cdim), lambda i: (i, 0))
    return pl.pallas_call(body, grid=(r // tr,), in_specs=[blk, blk], out_specs=blk, out_shape=S((r, cdim), BF),
                          compiler_params=_cparams("parallel"), name="pair_add")(g, other)


def _sum4(parts, pairs, chip_core):
    _, r, cdim = parts[0].shape
    tr = _row_tile(r, cdim)

    def body(kc_ref, q0_ref, q1_ref, own0_ref, own1_ref, out_ref):
        first = kc_ref[1] == 0
        mine = jnp.where(first, own0_ref[...], own1_ref[...]).astype(F32)
        terms = [jnp.where(kc_ref[0] == j, mine, jnp.where(first, q0_ref[j], q1_ref[j]).astype(F32))
                 for j in range(N_CHIPS)]
        out_ref[...] = ((terms[0] + terms[1]) + terms[2]) + terms[3]

    allc = pl.BlockSpec((N_CHIPS, tr, cdim), lambda i, kc: (0, i, 0))
    own = pl.BlockSpec((None, tr, cdim), lambda i, kc: (kc[0], i, 0))
    return pl.pallas_call(
        body,
        grid_spec=pltpu.PrefetchScalarGridSpec(
            num_scalar_prefetch=1, grid=(r // tr,), in_specs=[allc, allc, own, own],
            out_specs=pl.BlockSpec((None, tr, cdim), lambda i, kc: (kc[1], i, 0))),
        out_shape=S((DEPTH, r, cdim), F32), compiler_params=_cparams("parallel"), name="sum4")(
            chip_core, parts[0], parts[1], pairs[0], pairs[1])


def _ada_w_grad(silu_c, dmod_cols, chip):
    def body(k_ref, sc_ref, dm_ref, o_ref):
        del k_ref
        o_ref[...] = lax.dot_general(sc_ref[...], dm_ref[...], TN, precision=HIGHEST, preferred_element_type=F32)

    return pl.pallas_call(
        body,
        grid_spec=pltpu.PrefetchScalarGridSpec(
            num_scalar_prefetch=1, grid=(4,),
            in_specs=[pl.BlockSpec((N_DEV, D), lambda j, k: (0, 0)),
                      pl.BlockSpec((None, N_DEV, ADA_COLS), lambda j, k: (4 * j + k[0], 0, 0))],
            out_specs=pl.BlockSpec((None, D, ADA_COLS), lambda j, k: (j, 0, 0))),
        out_shape=S((4, D, ADA_COLS), F32), compiler_params=_cparams("parallel"), name="ada_w_grad")(chip, silu_c, dmod_cols)


def _adamw(w, g, m, v, *, name):
    r = w.shape[0]
    rest = w.shape[1:]
    row_elems = 1
    for d in rest:
        row_elems *= d
    tr = _row_tile(r, row_elems, align=8 if len(rest) == 1 else 1, target=5 * ROW_BYTES_TARGET // 2)
    c1 = 1.0 / (1.0 - ADAM_B1 ** ADAM_STEP)
    c2 = 1.0 / (1.0 - ADAM_B2 ** ADAM_STEP)

    def body(w_ref, g_ref, m_ref, v_ref, d_ref, mo_ref, vo_ref):
        gv = g_ref[...]
        mn = ADAM_B1 * m_ref[...] + (1.0 - ADAM_B1) * gv
        vn = ADAM_B2 * v_ref[...] + (1.0 - ADAM_B2) * (gv * gv)
        mo_ref[...] = mn
        vo_ref[...] = vn
        d_ref[...] = -ADAM_LR * ((mn * c1) / (jnp.sqrt(vn * c2) + ADAM_EPS) + ADAM_WD * w_ref[...])

    zeros = (0,) * len(rest)
    blk = pl.BlockSpec((tr,) + rest, lambda i: (i,) + zeros)
    return pl.pallas_call(
        body, grid=(r // tr,), in_specs=[blk] * 4, out_specs=[blk] * 3, out_shape=[S(w.shape, F32)] * 3,
        compiler_params=_cparams("parallel"), name=name)(w, g, m, v)


def _adamw_many(ws, gs, ms, vs):
    n = len(ws)
    c1 = 1.0 / (1.0 - ADAM_B1 ** ADAM_STEP)
    c2 = 1.0 / (1.0 - ADAM_B2 ** ADAM_STEP)

    def body(*refs):
        for i in range(n):
            w_ref, g_ref, m_ref, v_ref, d_ref, mo_ref, vo_ref = (refs[k * n + i] for k in range(7))
            gv = g_ref[...]
            mn = ADAM_B1 * m_ref[...] + (1.0 - ADAM_B1) * gv
            vn = ADAM_B2 * v_ref[...] + (1.0 - ADAM_B2) * (gv * gv)
            mo_ref[...] = mn
            vo_ref[...] = vn
            d_ref[...] = -ADAM_LR * ((mn * c1) / (jnp.sqrt(vn * c2) + ADAM_EPS) + ADAM_WD * w_ref[...])

    shapes = [S(a.shape, F32) for a in ws]
    outs = pl.pallas_call(
        body, in_specs=[VMEM] * (4 * n), out_specs=[VMEM] * (3 * n), out_shape=shapes * 3,
        compiler_params=pltpu.CompilerParams(vmem_limit_bytes=VMEM_LIMIT_V7X), name="adamw_small")(*ws, *gs, *ms, *vs)
    return outs[0:n], outs[n:2 * n], outs[2 * n:3 * n]


WEIGHTS = ['ada_mix_w', 'ada_mix_b', 'norm_mix_g', 'w_in', 'b_gate', 'ssd_conv_w', 'ssd_conv_b', 'ssd_dt_bias',
           'ssd_a_log', 'ssd_d', 'ssd_norm_g', 'w_ssd_out', 'conf_conv_w', 'conf_conv_b', 'conf_ln_g', 'conf_ln_b',
           'w_conf_out', 'sc_conv_w', 'w_sc_out', 'w_o', 'ada_ffn_w', 'ada_ffn_b', 'norm_ffn_g', 'w_up', 'ffn_conv_w',
           'ffn_conv_b', 'w_down', 'final_norm_g']
SMALL = ['ada_mix_b', 'norm_mix_g', 'b_gate', 'ssd_conv_b', 'ssd_dt_bias', 'ssd_a_log', 'ssd_d', 'ssd_norm_g', 'conf_conv_b',
         'conf_ln_g', 'conf_ln_b', 'ada_ffn_b', 'norm_ffn_g', 'ffn_conv_b']
CONVS = ['ssd_conv_w', 'conf_conv_w', 'sc_conv_w', 'ffn_conv_w']
BIG = ['ada_mix_w', 'ada_ffn_w', 'w_in', 'w_up', 'w_conf_out', 'w_sc_out', 'w_ssd_out', 'w_o', 'w_down']


def _pack_rows(pieces):
    flat = [p.reshape(-1) for p in pieces]
    offs, o = [], 0
    for f in flat:
        offs.append(o)
        o += f.shape[0]
    total = -(-o // 1024) * 1024
    vec = jnp.concatenate(flat + [jnp.zeros((total - o,), F32)])
    return vec.reshape(total // 128, 128), offs


def _by_chip(a, axis):
    shp = a.shape
    a = a.reshape(shp[:axis] + (N_CHIPS, shp[axis] // N_CHIPS) + shp[axis + 1:])
    return jnp.moveaxis(a, axis, 0)


def _from_chips(a, axis):
    a = jnp.moveaxis(a, 0, axis)
    shp = a.shape
    return a.reshape(shp[:axis] + (shp[axis] * shp[axis + 1],) + shp[axis + 2:])


def kernel(x, c, ada_mix_w, ada_mix_b, norm_mix_g, w_in, b_gate, ssd_conv_w, ssd_conv_b, ssd_dt_bias, ssd_a_log, ssd_d, ssd_norm_g, w_ssd_out, conf_conv_w, conf_conv_b, conf_ln_g, conf_ln_b, w_conf_out, sc_conv_w, w_sc_out, w_o, ada_ffn_w, ada_ffn_b, norm_ffn_g, w_up, ffn_conv_w, ffn_conv_b, w_down, final_norm_g, loss_target, m_ada_mix_w, m_ada_mix_b, m_norm_mix_g, m_w_in, m_b_gate, m_ssd_conv_w, m_ssd_conv_b, m_ssd_dt_bias, m_ssd_a_log, m_ssd_d, m_ssd_norm_g, m_w_ssd_out, m_conf_conv_w, m_conf_conv_b, m_conf_ln_g, m_conf_ln_b, m_w_conf_out, m_sc_conv_w, m_w_sc_out, m_w_o, m_ada_ffn_w, m_ada_ffn_b, m_norm_ffn_g, m_w_up, m_ffn_conv_w, m_ffn_conv_b, m_w_down, m_final_norm_g, v_ada_mix_w, v_ada_mix_b, v_norm_mix_g, v_w_in, v_b_gate, v_ssd_conv_w, v_ssd_conv_b, v_ssd_dt_bias, v_ssd_a_log, v_ssd_d, v_ssd_norm_g, v_w_ssd_out, v_conf_conv_w, v_conf_conv_b, v_conf_ln_g, v_conf_ln_b, v_w_conf_out, v_sc_conv_w, v_w_sc_out, v_w_o, v_ada_ffn_w, v_ada_ffn_b, v_norm_ffn_g, v_w_up, v_ffn_conv_w, v_ffn_conv_b, v_w_down, v_final_norm_g):
    args = locals()
    w = {n: args[n] for n in WEIGHTS}
    mom = {n: args["m_" + n] for n in WEIGHTS}
    var = {n: args["v_" + n] for n in WEIGHTS}
    pos = _mesh_pos()
    chip = _chip(pos)
    core = pos[2]

    conv_pack = jnp.zeros((DEPTH, CONV_ROWS, CONV_COLS), F32)
    for n, (r0, taps, width) in CONV_PACK.items():
        conv_pack = conv_pack.at[:, r0:r0 + taps, 0:width].set(w[n])
    c_blk = jnp.pad(c, ((0, 7), (0, 0)))
    mods_raw, silu_c, conv_all = _ada_exchange(c_blk, ada_mix_w, ada_ffn_w, conv_pack)
    ada_b = jnp.concatenate([ada_mix_b, ada_ffn_b], axis=0)
    mod_all = mods_raw.reshape(N_CHIPS, 4, ADA_COLS).transpose(1, 0, 2).reshape(4, 3 * D) + ada_b
    mod_all = mod_all.reshape(4, 3, D)
    mods = [(mod_all[i], mod_all[2 + i]) for i in range(DEPTH)]
    conv_full = {n: _from_chips(conv_all[:, :, r0:r0 + taps, 0:width], 2) for n, (r0, taps, width) in CONV_PACK.items()}

    cast = lambda a: a.astype(BF)
    shards = [cast(w_in), jnp.concatenate([cast(w_conf_out), cast(w_sc_out)], axis=1),
              jnp.concatenate([cast(w_ssd_out), cast(w_o)], axis=1), cast(w_up), cast(w_down)]
    FIRST, MID, LATE = (0,), (1, 2), (3, 4)
    EARLY = FIRST + MID
    pick = lambda arrs, idx: [arrs[k] for k in idx]

    def own_block(l, gathered, idx):
        return [lax.dynamic_update_slice(g, shards[k][l][None], (chip, 0, 0)) for g, k in zip(gathered, idx)]

    def first_params(l, gathered):
        (g_in,) = own_block(l, gathered, FIRST)
        p = {n: w[n][l] for n in SMALL if not n.startswith("ada_")}
        p.update({n: conv_full[n][l] for n in CONVS})
        p["w_in_pad"] = _pad_w_in_chips(g_in)
        return p

    def mid_params(l, gathered):
        g_cs, g_row = own_block(l, gathered, MID)
        return {"w_conf_out": _from_chips(g_cs[:, 0:CONF_W], 1), "w_sc_out": _from_chips(g_cs[:, CONF_W:], 1),
                "w_ssd_out": _from_chips(g_row[:, 0:INNER // N_CHIPS], 0), "w_o": _from_chips(g_row[:, INNER // N_CHIPS:], 0)}

    def late_params(l, gathered):
        g_up, g_down = own_block(l, gathered, LATE)
        return {"w_up4": g_up, "w_down": _from_chips(g_down, 0)}

    def big_grads(g, idx):
        makers = (lambda: _unpad_w_in_chips(g["w_in_pad"]),
                  lambda: _by_chip(jnp.concatenate([g["w_conf_out"], g["w_sc_out"]], axis=0), 1),
                  lambda: jnp.concatenate([_by_chip(g["w_ssd_out"], 0), _by_chip(g["w_o"], 0)], axis=1),
                  lambda: g["w_up4"], lambda: _by_chip(g["w_down"], 0))
        return [makers[k]().astype(BF) for k in idx]

    def pair_sums(big, l, tag):
        theirs = _swap_layer(big, l, tag)
        out = []
        for g4, t4 in zip(big, theirs):
            k, r, cdim = g4.shape
            out.append(_pair_add(g4.reshape(k * r, cdim), t4.reshape(k * r, cdim)).reshape(k, r, cdim))
        return out

    seq = x.shape[1]
    xs = x.reshape(seq, D)
    p0 = first_params(0, _run_comm(_gather_comm(pick(shards, FIRST), 0), "gather_weights_l0"))
    xs, sv0, gathered1 = _layer_fwd(xs, p0, mods[0][0], mods[0][1], "_l0",
                                    comm_proj=_gather_comm(pick(shards, MID), 0),
                                    mid_params=lambda got: mid_params(0, got),
                                    comm_ssd=_gather_comm(pick(shards, LATE), 0),
                                    late_params=lambda got: late_params(0, got),
                                    comm_up=_gather_comm(pick(shards, EARLY), 1),
                                    comm=_gather_comm(pick(shards, LATE), 1))
    p1 = first_params(1, gathered1[0:1])
    p1.update(mid_params(1, gathered1[1:3]))
    p1.update(late_params(1, gathered1[3:5]))
    xs, sv1, _ = _layer_fwd(xs, p1, mods[1][0], mods[1][1], "_l1")
    dx, loss, dfg = _final_loss(xs, _row(final_norm_g), loss_target.reshape(seq, D), name="final_loss")
    dfinal = dfg[0]

    every = EARLY + LATE
    grads, dmods = [None, None], [None, None]
    dx, grads[1], dmm, dmf, _ = _layer_bwd(dx, p1, sv1, mods[1][0], mods[1][1], "_l1")
    dmods[1] = (dmm, dmf)
    pair1 = pair_sums(big_grads(grads[1], every), 1, "_l1")
    late0 = {}

    def scatter_late0(g):
        late0["pair"] = pair_sums(big_grads(g, MID + LATE), 0, "_l0_ffn")
        return _scatter_comm(late0["pair"], 0)

    dx, grads[0], dmm, dmf, (parts1, parts0_late) = _layer_bwd(
        dx, p0, sv0, mods[0][0], mods[0][1], "_l0", comm=_scatter_comm(pair1, 1), make_comm_conf=scatter_late0)
    dmods[0] = (dmm, dmf)
    pair0_early = pair_sums(big_grads(grads[0], FIRST), 0, "_l0")
    parts0_early = _run_comm(_scatter_comm(pair0_early, 0), "scatter_chips_l0")
    pair0 = list(pair0_early) + list(late0["pair"])
    parts0 = list(parts0_early) + list(parts0_late)
    chip_core = jnp.stack([chip, core]).astype(jnp.int32)
    red = _share_sibling([_sum4((q0, q1), (o0, o1), chip_core)
                          for q0, q1, o0, o1 in zip(parts0, parts1, pair0, pair1)])
    reduced = {"w_in": red[0], "cs": red[1], "row": red[2], "w_up": red[3], "w_down": red[4]}
    gw = {"w_in": reduced["w_in"], "w_up": reduced["w_up"], "w_down": reduced["w_down"],
          "w_conf_out": reduced["cs"][:, 0:CONF_W], "w_sc_out": reduced["cs"][:, CONF_W:],
          "w_ssd_out": reduced["row"][:, 0:INNER // N_CHIPS], "w_o": reduced["row"][:, INNER // N_CHIPS:]}

    dmod = jnp.stack([dmods[0][0], dmods[1][0], dmods[0][1], dmods[1][1]])
    small_local = {n: jnp.stack([grads[l][n] for l in range(DEPTH)]) for n in SMALL if not n.startswith("ada_")}
    pieces = [loss[0]] + [small_local[n] for n in SMALL if not n.startswith("ada_")]
    pieces += [jnp.stack([grads[l][n] for l in range(DEPTH)]) for n in CONVS] + [dfinal, dmod]
    vec, offs = _pack_rows(pieces)
    vsum, vall = _small_allreduce(vec)
    flat = vsum.reshape(-1)

    def piece(k, like):
        return flat[offs[k]:offs[k] + like.size].reshape(like.shape)

    loss_out = flat[0]
    k = 1
    for n in SMALL:
        if not n.startswith("ada_"):
            gw[n] = piece(k, small_local[n])
            k += 1
    for n in CONVS:
        full = piece(k, conv_full[n])
        width = CONV_PACK[n][2]
        gw[n] = lax.dynamic_slice_in_dim(full, chip * width, width, axis=2)
        k += 1
    gw["final_norm_g"] = piece(k, dfinal)
    k += 1
    dmod_sum = piece(k, dmod)
    gw["ada_mix_b"], gw["ada_ffn_b"] = dmod_sum[0:2], dmod_sum[2:4]
    dmod_all = vall.reshape(N_DEV, -1)[:, offs[k]:offs[k] + dmod.size]
    dmod_cols = dmod_all.reshape(N_DEV, 4 * N_CHIPS, ADA_COLS).transpose(1, 0, 2)
    ada_g = _ada_w_grad(silu_c, dmod_cols, jnp.reshape(chip, (1,)).astype(jnp.int32))
    gw["ada_mix_w"], gw["ada_ffn_w"] = ada_g[0:2], ada_g[2:4]

    delta, new_m, new_v = {}, {}, {}
    for n in BIG:
        shp = w[n].shape
        if shp[2] % 128:
            two_d = lambda a: jnp.transpose(a, (2, 0, 1))
            back = lambda a: jnp.transpose(a, (1, 2, 0))
        else:
            two_d = lambda a: a.reshape(shp[0] * shp[1], shp[2])
            back = lambda a: a.reshape(shp)
        d_, m_, v_ = _adamw(two_d(w[n]), two_d(gw[n]), two_d(mom[n]), two_d(var[n]), name=f"adamw_{n}")
        delta[n], new_m[n], new_v[n] = back(d_), back(m_), back(v_)
    rest = [n for n in WEIGHTS if n not in BIG]
    two_d = lambda a: a.reshape(1, -1) if a.ndim == 1 else a
    outs = _adamw_many(*[[two_d(src[n]) for n in rest] for src in (w, gw, mom, var)])
    for dst, group in zip((delta, new_m, new_v), outs):
        for n, o in zip(rest, group):
            dst[n] = o.reshape(w[n].shape)

    return (loss_out, dx[None], *[gw[n] for n in WEIGHTS], *[delta[n] for n in WEIGHTS],
            *[new_m[n] for n in WEIGHTS], *[new_v[n] for n in WEIGHTS])
```

```python
import jax
import jax.numpy as jnp
from jax import lax
from jax.experimental import pallas as pl
from jax.experimental.pallas import tpu as pltpu

F32 = jnp.float32
BF = jnp.bfloat16
S = jax.ShapeDtypeStruct

D = 1024
HEADS = 16
HEAD_DIM = 64
INNER = HEADS * HEAD_DIM
GROUPS = 2
NSTATE = 64
Q = 128
SSD_K = 4
XBC = INNER + 2 * GROUPS * NSTATE
CONF_W = 512
CONF_K = 31
SC_W = 512
SC_K = 3
DFF = 2816
FFN_K = 3
EPS = 1e-6
DEPTH = 2
R_Z, R_XBC, R_DT, R_CONF, R_SC, R_GATES, N_IN = 0, 1024, 2304, 2320, 3344, 4880, 7952
P_GATES, P_SC, P_XBC, P_Z, P_CONF, PW = 0, 3072, 4608, 6144, 7168, 8192
XBC_PAD = 1536
DT_PAD = 128
P_DT = P_XBC + XBC
N_CHIPS = 4
N_DEV = 8

ADAM_LR, ADAM_B1, ADAM_B2, ADAM_EPS, ADAM_WD, ADAM_STEP = 0.001, 0.9, 0.999, 1e-08, 0.01, 10

VMEM_LIMIT_V7X = 56 * 1024 * 1024
HIGHEST = lax.Precision.HIGHEST


def _cparams(*sem):
    return pltpu.CompilerParams(dimension_semantics=sem, vmem_limit_bytes=VMEM_LIMIT_V7X)


def _full(shape):
    n = len(shape)
    return pl.BlockSpec(shape, lambda *_: (0,) * n)


def _rows(tm, w, cb=0):
    return pl.BlockSpec((tm, w), lambda i: (i, cb))


def _prev_rows(tm, halo, w, cb=0):
    r = tm // halo
    return pl.BlockSpec((halo, w), lambda i: (jnp.maximum(i * r - 1, 0), cb))


def _next_rows(tm, halo, w, nrows, cb=0):
    r = tm // halo
    last = nrows // halo - 1
    return pl.BlockSpec((halo, w), lambda i: (jnp.minimum((i + 1) * r, last), cb))


def _sigmoid(v):
    return 1.0 / (1.0 + jnp.exp(-v))


def _softplus(v):
    return jnp.maximum(v, 0.0) + jnp.log(1.0 + jnp.exp(-jnp.abs(v)))


def _colsum(v):
    return jnp.sum(v, axis=0, keepdims=True)


def _tile(n, want):
    t = min(n, want)
    assert n % t == 0, (n, want)
    return t


NN = (((1,), (0,)), ((), ()))
NT = (((1,), (1,)), ((), ()))
TN = (((0,), (0,)), ((), ()))


def _dot(a, b, dims=NN):
    return lax.dot_general(a, b, dims, preferred_element_type=F32)


def _mm(a, b, *, dims, grid, a_spec, b_spec, o_spec, out_shape, acc_shape, name, comm=None):
    nk = grid[2]

    def body(a_ref, b_ref, o_ref, acc_ref):
        k = pl.program_id(2)
        part = _dot(a_ref[...], b_ref[...], dims)
        if nk == 1:
            o_ref[...] = part.astype(o_ref.dtype)
        else:
            @pl.when(k == 0)
            def _():
                acc_ref[...] = part

            @pl.when(k > 0)
            def _():
                acc_ref[...] += part

            @pl.when(k == nk - 1)
            def _():
                o_ref[...] = acc_ref[...].astype(o_ref.dtype)

    scratch = [pltpu.VMEM(acc_shape if nk > 1 else (8, 128), F32)]
    if comm is not None:
        (out,), comm_out = _call_with_comm(body, tuple(grid), in_specs=[a_spec, b_spec], out_specs=[o_spec],
                                           out_shape=[out_shape], scratch_shapes=scratch, args=(a, b), comm=comm, name=name)
        return out, comm_out
    return pl.pallas_call(
        body, grid=grid, in_specs=[a_spec, b_spec], out_specs=o_spec, out_shape=out_shape, scratch_shapes=scratch,
        compiler_params=_cparams("parallel", "parallel", "arbitrary"), name=name)(a, b)


def _mm_nn(a, b, *, out_dtype, tm, tn, name, comm=None):
    m, k = a.shape
    n = b.shape[1]
    tm, tn = _tile(m, tm), _tile(n, tn)
    return _mm(a, b, dims=NN, grid=(m // tm, n // tn, 1),
               a_spec=pl.BlockSpec((tm, k), lambda i, j, kk: (i, 0)),
               b_spec=pl.BlockSpec((k, tn), lambda i, j, kk: (0, j)),
               o_spec=pl.BlockSpec((tm, tn), lambda i, j, kk: (i, j)),
               out_shape=S((m, n), out_dtype), acc_shape=(tm, tn), name=name, comm=comm)


def _mm_nt(a, b, *, out_dtype, tm, tk, name):
    m, kc = a.shape
    n = b.shape[0]
    tm, tk = _tile(m, tm), _tile(kc, tk)
    return _mm(a, b, dims=NT, grid=(m // tm, 1, kc // tk),
               a_spec=pl.BlockSpec((tm, tk), lambda i, j, kk: (i, kk)),
               b_spec=pl.BlockSpec((n, tk), lambda i, j, kk: (0, kk)),
               o_spec=pl.BlockSpec((tm, n), lambda i, j, kk: (i, 0)),
               out_shape=S((m, n), out_dtype), acc_shape=(tm, n), name=name)


def _mm_tn(a, b, *, tn, tk, name, out_dtype=BF, by_chip=False):
    kc, m = a.shape
    n = b.shape[1]
    tn, tk = _tile(n, tn), _tile(kc, tk)
    if by_chip:
        assert n == N_CHIPS * tn
        o_spec, out_shape = pl.BlockSpec((None, m, tn), lambda i, j, kk: (j, 0, 0)), S((N_CHIPS, m, tn), out_dtype)
    else:
        o_spec, out_shape = pl.BlockSpec((m, tn), lambda i, j, kk: (0, j)), S((m, n), out_dtype)
    return _mm(a, b, dims=TN, grid=(1, n // tn, kc // tk),
               a_spec=pl.BlockSpec((tk, m), lambda i, j, kk: (kk, 0)),
               b_spec=pl.BlockSpec((tk, tn), lambda i, j, kk: (kk, j)),
               o_spec=o_spec, out_shape=out_shape, acc_shape=(m, tn), name=name)


def _mm_nn_chips(a, b4, *, out_dtype, tm, name, comm=None):
    m, k = a.shape
    n4 = b4.shape[2]
    tm = _tile(m, tm)
    return _mm(a, b4, dims=NN, grid=(m // tm, N_CHIPS, 1),
               a_spec=pl.BlockSpec((tm, k), lambda i, j, kk: (i, 0)),
               b_spec=pl.BlockSpec((None, k, n4), lambda i, j, kk: (j, 0, 0)),
               o_spec=pl.BlockSpec((tm, n4), lambda i, j, kk: (i, j)),
               out_shape=S((m, N_CHIPS * n4), out_dtype), acc_shape=(tm, n4), name=name, comm=comm)


def _mm_nt_chips(a, b4, *, out_dtype, tm, name):
    m = a.shape[0]
    n, n4 = b4.shape[1], b4.shape[2]
    tm = _tile(m, tm)
    return _mm(a, b4, dims=NT, grid=(m // tm, 1, N_CHIPS),
               a_spec=pl.BlockSpec((tm, n4), lambda i, j, kk: (i, kk)),
               b_spec=pl.BlockSpec((None, n, n4), lambda i, j, kk: (kk, 0, 0)),
               o_spec=pl.BlockSpec((tm, n), lambda i, j, kk: (i, 0)),
               out_shape=S((m, n), out_dtype), acc_shape=(tm, n), name=name)


def _mm_resid(a, b, x, mod, *, tm, name):
    m, k = a.shape
    n = b.shape[1]
    tm = _tile(m, tm)

    def body(a_ref, b_ref, x_ref, mod_ref, o_ref, xn_ref):
        o = _dot(a_ref[...], b_ref[...])
        o_ref[...] = o.astype(o_ref.dtype)
        xn_ref[...] = x_ref[...] + mod_ref[2:3, :] * o

    return pl.pallas_call(
        body, grid=(m // tm,),
        in_specs=[_rows(tm, k), _full((k, n)), _rows(tm, n), _full((3, n))],
        out_specs=[_rows(tm, n), _rows(tm, n)],
        out_shape=[S((m, n), BF), S((m, n), F32)],
        compiler_params=_cparams("parallel"), name=name)(a, b, x, mod)


def _modnorm_fwd(x, gain, mod, *, name):
    n = x.shape[0]
    tm = _tile(n, 512)

    def body(x_ref, g_ref, mod_ref, h_ref):
        xv = x_ref[...]
        r = lax.rsqrt(jnp.mean(xv * xv, axis=-1, keepdims=True) + EPS)
        y = xv * r * g_ref[...]
        h_ref[...] = (y * (1.0 + mod_ref[1:2, :]) + mod_ref[0:1, :]).astype(h_ref.dtype)

    return pl.pallas_call(
        body, grid=(n // tm,), in_specs=[_rows(tm, D), _full((1, D)), _full((3, D))],
        out_specs=_rows(tm, D), out_shape=S((n, D), BF), compiler_params=_cparams("parallel"), name=name)(x, gain, mod)


def _modnorm_bwd(dh, x, dres, gain, mod, *, name):
    n = x.shape[0]
    tm = _tile(n, 512)

    def body(dh_ref, x_ref, dres_ref, g_ref, mod_ref, dx_ref, dg_ref, dsh_ref, dsc_ref):
        i = pl.program_id(0)
        xv = x_ref[...]
        r = lax.rsqrt(jnp.mean(xv * xv, axis=-1, keepdims=True) + EPS)
        xh = xv * r
        dhv = dh_ref[...]
        g = g_ref[...]
        dy = dhv * (1.0 + mod_ref[1:2, :])
        dxh = dy * g
        dx = r * (dxh - xh * jnp.mean(dxh * xh, axis=-1, keepdims=True))
        dx_ref[...] = dres_ref[...] + dx

        @pl.when(i == 0)
        def _():
            dg_ref[...] = jnp.zeros_like(dg_ref)
            dsh_ref[...] = jnp.zeros_like(dsh_ref)
            dsc_ref[...] = jnp.zeros_like(dsc_ref)

        dg_ref[...] += _colsum(dy * xh)
        dsh_ref[...] += _colsum(dhv)
        dsc_ref[...] += _colsum(dhv * xh * g)

    vec = S((1, D), F32)
    return pl.pallas_call(
        body, grid=(n // tm,),
        in_specs=[_rows(tm, D), _rows(tm, D), _rows(tm, D), _full((1, D)), _full((3, D))],
        out_specs=[_rows(tm, D), _full((1, D)), _full((1, D)), _full((1, D))],
        out_shape=[S((n, D), F32), vec, vec, vec],
        compiler_params=_cparams("arbitrary"), name=name)(dh, x, dres, gain, mod)


def _final_loss(x, gain, target, *, name):
    n = x.shape[0]
    tm = _tile(n, 512)

    def body(x_ref, g_ref, t_ref, dx_ref, loss_ref, dg_ref):
        i = pl.program_id(0)
        xv = x_ref[...]
        g = g_ref[...]
        r = lax.rsqrt(jnp.mean(xv * xv, axis=-1, keepdims=True) + EPS)
        xh = xv * r
        err = xh * g - t_ref[...]
        dy = err * (1.0 / D)
        dxh = dy * g
        dx_ref[...] = r * (dxh - xh * jnp.mean(dxh * xh, axis=-1, keepdims=True))

        @pl.when(i == 0)
        def _():
            loss_ref[...] = jnp.zeros_like(loss_ref)
            dg_ref[...] = jnp.zeros_like(dg_ref)

        part = _colsum(jnp.sum(err * err, axis=-1, keepdims=True)) * (0.5 / D)
        loss_ref[...] += jnp.broadcast_to(part, loss_ref.shape)
        dg_ref[...] += _colsum(dy * xh)

    return pl.pallas_call(
        body, grid=(n // tm,),
        in_specs=[_rows(tm, D), _full((1, D)), _rows(tm, D)],
        out_specs=[_rows(tm, D), _full((1, 128)), _full((1, D))],
        out_shape=[S((n, D), F32), S((1, 128), F32), S((1, D), F32)],
        compiler_params=_cparams("arbitrary"), name=name)(x, gain, target)


def _gate_bwd(dx, o, mod, *, name):
    n = dx.shape[0]
    tm = _tile(n, 512)

    def body(dx_ref, o_ref, mod_ref, do_ref, dgt_ref):
        i = pl.program_id(0)
        dxv = dx_ref[...]
        do_ref[...] = (dxv * mod_ref[2:3, :]).astype(do_ref.dtype)

        @pl.when(i == 0)
        def _():
            dgt_ref[...] = jnp.zeros_like(dgt_ref)

        dgt_ref[...] += _colsum(dxv * o_ref[...].astype(F32))

    return pl.pallas_call(
        body, grid=(n // tm,), in_specs=[_rows(tm, D), _rows(tm, D), _full((3, D))],
        out_specs=[_rows(tm, D), _full((1, D))], out_shape=[S((n, D), BF), S((1, D), F32)],
        compiler_params=_cparams("arbitrary"), name=name)(dx, o, mod)


def _conv(buf, w_ref, taps, start, rows, ch):
    acc = None
    for k in range(taps):
        term = buf[pl.ds(start - (taps - 1) + k, rows), 0:ch] * w_ref[k:k + 1, :]
        acc = term if acc is None else acc + term
    return acc


def _conv_t(buf, w_ref, taps, start, rows, ch):
    acc = None
    for k in range(taps):
        term = buf[pl.ds(start + (taps - 1) - k, rows), 0:ch] * w_ref[k:k + 1, :]
        acc = term if acc is None else acc + term
    return acc


def _conv_dw(dw_ref, dy, xbuf, taps, xstart, rows, ch):
    for k in range(taps):
        dw_ref[k:k + 1, :] += _colsum(dy * xbuf[pl.ds(xstart - (taps - 1) + k, rows), 0:ch])


HALO = 16
CONF_HALO = 32
CHUNK = 32
STRIP = 256


def _blocks8(v):
    return [v[8 * i:8 * (i + 1)] for i in range(v.shape[0] // 8)]


def _delay_rows(blocks, s):
    sub = lax.broadcasted_iota(jnp.int32, blocks[0].shape, 0)
    rolled = [pltpu.roll(b, s, 0) for b in blocks]
    return [jnp.where(sub < s, rolled[i - 1], rolled[i]) for i in range(1, len(blocks))]


def _advance_rows(blocks, s):
    sub = lax.broadcasted_iota(jnp.int32, blocks[0].shape, 0)
    rolled = [pltpu.roll(b, 8 - s, 0) for b in blocks]
    return [jnp.where(sub < 8 - s, rolled[i], rolled[i + 1]) for i in range(len(blocks) - 1)]


def _conv3_chunk(tail, xv, wk):
    blocks = [tail] + _blocks8(xv)
    x1 = jnp.concatenate(_delay_rows(blocks, 1), axis=0)
    x2 = jnp.concatenate(_delay_rows(blocks, 2), axis=0)
    return wk[0] * x2 + wk[1] * x1 + wk[2] * xv


def _ssd_pre_fwd(proj, w, b, *, name):
    n = proj.shape[0]
    tm = _tile(n, 512)
    cb = P_XBC // XBC_PAD

    def body(prev_ref, cur_ref, w_ref, b_ref, o_ref, c_ref, buf):
        i = pl.program_id(0)
        buf[0:HALO, :] = jnp.where(i == 0, 0.0, prev_ref[:, 0:XBC].astype(F32))
        buf[HALO:HALO + tm, :] = cur_ref[:, 0:XBC].astype(F32)
        c = _conv(buf, w_ref, SSD_K, HALO, tm, XBC) + b_ref[...]
        c_ref[...] = c.astype(c_ref.dtype)
        o_ref[...] = (c * _sigmoid(c)).astype(o_ref.dtype)

    return pl.pallas_call(
        body, grid=(n // tm,),
        in_specs=[_prev_rows(tm, HALO, XBC_PAD, cb), _rows(tm, XBC_PAD, cb), _full((SSD_K, XBC)), _full((1, XBC))],
        out_specs=[_rows(tm, XBC), _rows(tm, XBC)], out_shape=[S((n, XBC), BF), S((n, XBC), BF)],
        scratch_shapes=[pltpu.VMEM((HALO + tm, XBC), F32)],
        compiler_params=_cparams("parallel"), name=name)(proj, proj, w, b)


def _ssd_pre_bwd(proj, cpre, dact, ddt, dproj, w, *, name):
    n = proj.shape[0]
    tm = _tile(n, 512)
    nt = n // tm
    cb = P_XBC // XBC_PAD

    def body(x_ref, cc_ref, cn_ref, dc_ref, dn_ref, ddt_ref, w_ref, dproj_in, o_ref, dw_ref, db_ref, dbuf, acc):
        del dproj_in
        i = pl.program_id(0)
        last = i == nt - 1

        @pl.when(i == 0)
        def _():
            acc[...] = jnp.zeros_like(acc)

        def silu_bwd(cv, dav):
            sg = _sigmoid(cv)
            return dav * (sg * (1.0 + cv * (1.0 - sg)))

        for s in range(XBC // STRIP):
            c = pl.ds(s * STRIP, STRIP)
            wk = [w_ref[k:k + 1, c] for k in range(SSD_K)]

            def step1(j, carry):
                rows = pl.ds(pl.multiple_of(j * CHUNK, CHUNK), CHUNK)
                dbuf[rows, c] = silu_bwd(cc_ref[rows, c].astype(F32), dc_ref[rows, c].astype(F32))
                return carry

            lax.fori_loop(0, tm // CHUNK, step1, 0, unroll=2)
            dbuf[tm:tm + HALO, c] = silu_bwd(cn_ref[:, c].astype(F32), jnp.where(last, 0.0, dn_ref[:, c].astype(F32)))

            def step2(j, carry):
                r0 = pl.multiple_of(j * CHUNK, CHUNK)
                rows = pl.ds(r0, CHUNK)
                win = dbuf[pl.ds(r0, CHUNK + 8), c]
                blocks = _blocks8(win)
                xv = x_ref[rows, c].astype(F32)
                d0 = win[0:CHUNK]
                dx = wk[SSD_K - 1] * d0
                acc[SSD_K - 1, :, c] += d0 * xv
                acc[SSD_K, :, c] += d0
                for adv in range(1, SSD_K):
                    dk = jnp.concatenate(_advance_rows(blocks, adv), axis=0)
                    dx = dx + wk[SSD_K - 1 - adv] * dk
                    acc[SSD_K - 1 - adv, :, c] += dk * xv
                o_ref[rows, c] = dx.astype(o_ref.dtype)
                return carry

            lax.fori_loop(0, tm // CHUNK, step2, 0)

        o_ref[:, XBC:XBC + DT_PAD] = ddt_ref[...]
        o_ref[:, XBC + DT_PAD:XBC_PAD] = jnp.zeros((tm, XBC_PAD - XBC - DT_PAD), o_ref.dtype)

        @pl.when(last)
        def _():
            for k in range(SSD_K):
                dw_ref[k:k + 1, :] = _colsum(acc[k])
            db_ref[...] = _colsum(acc[SSD_K])

    return pl.pallas_call(
        body, grid=(nt,),
        in_specs=[_rows(tm, XBC_PAD, cb), _rows(tm, XBC), _next_rows(tm, HALO, XBC, n),
                  _rows(tm, XBC), _next_rows(tm, HALO, XBC, n), _rows(tm, DT_PAD),
                  _full((SSD_K, XBC)), pl.BlockSpec(memory_space=pl.ANY)],
        out_specs=[_rows(tm, XBC_PAD, cb), _full((SSD_K, XBC)), _full((1, XBC))],
        out_shape=[S(dproj.shape, dproj.dtype), S((SSD_K, XBC), F32), S((1, XBC), F32)],
        scratch_shapes=[pltpu.VMEM((tm + HALO, XBC), F32), pltpu.VMEM((SSD_K + 1, CHUNK, XBC), F32)],
        input_output_aliases={7: 0},
        compiler_params=_cparams("arbitrary"), name=name)(proj, cpre, cpre, dact, dact, ddt, w, dproj)


def _sc_fwd(proj, w, *, name):
    n = proj.shape[0]
    tm = _tile(n, 512)
    cb = P_SC // (3 * SC_W)

    def body(prev_ref, cur_ref, w_ref, o_ref, buf):
        i = pl.program_id(0)
        pv = prev_ref[...].astype(F32)
        cv = cur_ref[...].astype(F32)
        buf[0:HALO, :] = jnp.where(i == 0, 0.0, pv[:, SC_W:2 * SC_W] * pv[:, 2 * SC_W:])
        buf[HALO:HALO + tm, :] = cv[:, SC_W:2 * SC_W] * cv[:, 2 * SC_W:]
        q = _conv(buf, w_ref, SC_K, HALO, tm, SC_W)
        o_ref[...] = (cv[:, 0:SC_W] * q).astype(o_ref.dtype)

    return pl.pallas_call(
        body, grid=(n // tm,),
        in_specs=[_prev_rows(tm, HALO, 3 * SC_W, cb), _rows(tm, 3 * SC_W, cb), _full((SC_K, SC_W))],
        out_specs=_rows(tm, SC_W), out_shape=S((n, SC_W), BF),
        scratch_shapes=[pltpu.VMEM((HALO + tm, SC_W), F32)],
        compiler_params=_cparams("parallel"), name=name)(proj, proj, w)


def _sc_bwd(proj, da, dproj, w, *, name):
    n = proj.shape[0]
    tm = _tile(n, 512)
    nt = n // tm
    cb = P_SC // (3 * SC_W)

    def body(xp_ref, xc_ref, xn_ref, dc_ref, dn_ref, w_ref, dproj_in, o_ref, dw_ref, pbuf, dbuf):
        del dproj_in
        i = pl.program_id(0)
        pv = xp_ref[...].astype(F32)
        cv = xc_ref[...].astype(F32)
        nv = xn_ref[...].astype(F32)
        gb, gc, xv = cv[:, 0:SC_W], cv[:, SC_W:2 * SC_W], cv[:, 2 * SC_W:]
        pbuf[0:HALO, :] = jnp.where(i == 0, 0.0, pv[:, SC_W:2 * SC_W] * pv[:, 2 * SC_W:])
        pbuf[HALO:HALO + tm, :] = gc * xv
        q = _conv(pbuf, w_ref, SC_K, HALO, tm, SC_W)
        dav = dc_ref[...].astype(F32)
        dbuf[0:tm, :] = dav * gb
        dbuf[tm:tm + HALO, :] = jnp.where(i == nt - 1, 0.0, dn_ref[...].astype(F32) * nv[:, 0:SC_W])
        dp = _conv_t(dbuf, w_ref, SC_K, 0, tm, SC_W)
        o_ref[:, 0:SC_W] = (dav * q).astype(o_ref.dtype)
        o_ref[:, SC_W:2 * SC_W] = (dp * xv).astype(o_ref.dtype)
        o_ref[:, 2 * SC_W:] = (dp * gc).astype(o_ref.dtype)

        @pl.when(i == 0)
        def _():
            dw_ref[...] = jnp.zeros_like(dw_ref)

        _conv_dw(dw_ref, dbuf[0:tm, :], pbuf, SC_K, HALO, tm, SC_W)

    return pl.pallas_call(
        body, grid=(nt,),
        in_specs=[_prev_rows(tm, HALO, 3 * SC_W, cb), _rows(tm, 3 * SC_W, cb), _next_rows(tm, HALO, 3 * SC_W, n, cb),
                  _rows(tm, SC_W), _next_rows(tm, HALO, SC_W, n), _full((SC_K, SC_W)),
                  pl.BlockSpec(memory_space=pl.ANY)],
        out_specs=[_rows(tm, 3 * SC_W, cb), _full((SC_K, SC_W))],
        out_shape=[S(dproj.shape, dproj.dtype), S((SC_K, SC_W), F32)],
        scratch_shapes=[pltpu.VMEM((HALO + tm, SC_W), F32), pltpu.VMEM((tm + HALO, SC_W), F32)],
        input_output_aliases={6: 0},
        compiler_params=_cparams("arbitrary"), name=name)(proj, proj, proj, da, da, w, dproj)


CONF_ROWS = 32
CONF_PHASES = 8


def _fill_advanced(src, dst, nblk, ncol):
    for s in range(ncol // STRIP):
        c = pl.ds(s * STRIP, STRIP)
        sub = lax.broadcasted_iota(jnp.int32, (8, STRIP), 0)
        first = src[0:8, c]
        carry0 = tuple(pltpu.roll(first, 8 - b, 0) for b in range(1, CONF_PHASES))

        def step(i, prev):
            r0 = pl.multiple_of(i * 8, 8)
            blk = src[pl.ds(r0, 8), c]
            cur = []
            for b in range(1, CONF_PHASES):
                rolled = pltpu.roll(blk, 8 - b, 0)
                cur.append(rolled)
                dst[b - 1, pl.ds(r0 - 8, 8), c] = jnp.where(sub < 8 - b, prev[b - 1], rolled)
            return tuple(cur)

        lax.fori_loop(1, nblk, step, carry0)


def _conf_fwd(proj, w, b, ln_g, ln_b, *, name):
    n = proj.shape[0]
    tm = _tile(n, 512)
    cb = P_CONF // (2 * CONF_W)
    h = CONF_HALO

    def body(prev_ref, cur_ref, w_ref, b_ref, g_ref, be_ref, a_ref, uc_ref, buf):
        i = pl.program_id(0)
        pv = prev_ref[...].astype(F32)
        cv = cur_ref[...].astype(F32)
        buf[0:h, :] = jnp.where(i == 0, 0.0, pv[:, 0:CONF_W] * _sigmoid(pv[:, CONF_W:]))
        buf[h:h + tm, :] = cv[:, 0:CONF_W] * _sigmoid(cv[:, CONF_W:])
        uc = _conv(buf, w_ref, CONF_K, h, tm, CONF_W) + b_ref[...]
        uc_ref[...] = uc.astype(uc_ref.dtype)
        mu = jnp.mean(uc, axis=-1, keepdims=True)
        xc = uc - mu
        v = xc * lax.rsqrt(jnp.mean(xc * xc, axis=-1, keepdims=True) + EPS) * g_ref[...] + be_ref[...]
        a_ref[...] = (v * _sigmoid(v)).astype(a_ref.dtype)

    vec = _full((1, CONF_W))
    return pl.pallas_call(
        body, grid=(n // tm,),
        in_specs=[_prev_rows(tm, h, 2 * CONF_W, cb), _rows(tm, 2 * CONF_W, cb), _full((CONF_K, CONF_W)), vec, vec, vec],
        out_specs=[_rows(tm, CONF_W), _rows(tm, CONF_W)],
        out_shape=[S((n, CONF_W), BF), S((n, CONF_W), BF)],
        scratch_shapes=[pltpu.VMEM((h + tm, CONF_W), F32)],
        compiler_params=_cparams("parallel"), name=name)(proj, proj, w, b, ln_g, ln_b)


def _conf_bwd(proj, uc, da, dproj, w, ln_g, ln_b, *, name, comm=None):
    n = proj.shape[0]
    tm = _tile(n, 512)
    nt = n // tm
    cb = P_CONF // (2 * CONF_W)
    h = CONF_HALO

    def body(xp_ref, xc_ref, ucc_ref, ucn_ref, dac_ref, dan_ref, w_ref, g_ref, be_ref, dproj_in,
             o_ref, dw_ref, db_ref, dg_ref, dbe_ref, ubuf, dbuf, dsh, wb):
        del dproj_in
        i = pl.program_id(0)
        pv = xp_ref[...].astype(F32)
        cv = xc_ref[...].astype(F32)
        val, gt = cv[:, 0:CONF_W], cv[:, CONF_W:]
        sg = _sigmoid(gt)
        ubuf[0:h, :] = jnp.where(i == 0, 0.0, pv[:, 0:CONF_W] * _sigmoid(pv[:, CONF_W:]))
        ubuf[h:h + tm, :] = val * sg

        def ln_silu_bwd(ucv, dav):
            mu = jnp.mean(ucv, axis=-1, keepdims=True)
            xc = ucv - mu
            r = lax.rsqrt(jnp.mean(xc * xc, axis=-1, keepdims=True) + EPS)
            xh = xc * r
            v = xh * g_ref[...] + be_ref[...]
            s = _sigmoid(v)
            dv = dav * (s * (1.0 + v * (1.0 - s)))
            dxh = dv * g_ref[...]
            duc = r * (dxh - jnp.mean(dxh, axis=-1, keepdims=True) - xh * jnp.mean(dxh * xh, axis=-1, keepdims=True))
            return duc, dv, xh

        duc, dv, xh = ln_silu_bwd(ucc_ref[...].astype(F32), dac_ref[...].astype(F32))
        dbuf[0:tm, :] = duc
        ducn, _, _ = ln_silu_bwd(ucn_ref[...].astype(F32), dan_ref[...].astype(F32))
        dbuf[tm:tm + h, :] = jnp.where(i == nt - 1, 0.0, ducn)
        for k in range(CONF_K):
            wb[k] = jnp.broadcast_to(w_ref[k:k + 1, :], (8, CONF_W))
        _fill_advanced(dbuf, dsh, (tm + h) // 8, CONF_W)
        nb = CONF_ROWS // 8
        for s_ in range(CONF_W // STRIP):
            c = pl.ds(s_ * STRIP, STRIP)
            cg = pl.ds(CONF_W + s_ * STRIP, STRIP)

            def dconv(j, carry):
                r0 = pl.multiple_of(j * CONF_ROWS, CONF_ROWS)
                acc = [jnp.zeros((8, STRIP), F32) for _ in range(nb)]
                for t in range(CONF_K):
                    wv = wb[CONF_K - 1 - t, :, c]
                    for q in range(nb):
                        rows8 = pl.ds(r0 + 8 * q + 8 * (t // 8), 8)
                        dv = dbuf[rows8, c] if t % 8 == 0 else dsh[t % 8 - 1, rows8, c]
                        acc[q] = acc[q] + wv * dv
                du = jnp.concatenate(acc, axis=0)
                rows = pl.ds(r0, CONF_ROWS)
                vl = xc_ref[rows, c].astype(F32)
                sgv = _sigmoid(xc_ref[rows, cg].astype(F32))
                o_ref[rows, c] = (du * sgv).astype(o_ref.dtype)
                o_ref[rows, cg] = (du * vl * sgv * (1.0 - sgv)).astype(o_ref.dtype)
                return carry

            lax.fori_loop(0, tm // CONF_ROWS, dconv, 0)

        @pl.when(i == 0)
        def _():
            dw_ref[...] = jnp.zeros_like(dw_ref)
            db_ref[...] = jnp.zeros_like(db_ref)
            dg_ref[...] = jnp.zeros_like(dg_ref)
            dbe_ref[...] = jnp.zeros_like(dbe_ref)

        dg_ref[...] += _colsum(dv * xh)
        dbe_ref[...] += _colsum(dv)
        db_ref[...] += _colsum(duc)
        _conv_dw(dw_ref, duc, ubuf, CONF_K, h, tm, CONF_W)

    vec = _full((1, CONF_W))
    vshape = S((1, CONF_W), F32)
    return _call_with_comm(
        body, nt,
        in_specs=[_prev_rows(tm, h, 2 * CONF_W, cb), _rows(tm, 2 * CONF_W, cb),
                  _rows(tm, CONF_W), _next_rows(tm, h, CONF_W, n), _rows(tm, CONF_W), _next_rows(tm, h, CONF_W, n),
                  _full((CONF_K, CONF_W)), vec, vec, pl.BlockSpec(memory_space=pl.ANY)],
        out_specs=[_rows(tm, 2 * CONF_W, cb), _full((CONF_K, CONF_W)), vec, vec, vec],
        out_shape=[S(dproj.shape, dproj.dtype), S((CONF_K, CONF_W), F32), vshape, vshape, vshape],
        scratch_shapes=[pltpu.VMEM((h + tm, CONF_W), F32), pltpu.VMEM((tm + h, CONF_W), F32),
                        pltpu.VMEM((CONF_PHASES - 1, tm + h, CONF_W), F32), pltpu.VMEM((CONF_K, 8, CONF_W), F32)],
        args=(proj, proj, uc, uc, da, da, w, ln_g, ln_b, dproj), comm=comm, name=name, aliases={9: 0})


def _ffn_act_fwd(up, w, b, *, name, comm=None):
    n = up.shape[0]
    tm = _tile(n, 512)
    c2 = 2 * DFF

    def body(prev_ref, cur_ref, w_ref, b_ref, o_ref, u_ref):
        first = pl.program_id(0) == 0
        for s in range(DFF // STRIP):
            cols = (pl.ds(s * STRIP, STRIP), pl.ds(DFF + s * STRIP, STRIP))
            wk = [[w_ref[k:k + 1, c] for k in range(FFN_K)] for c in cols]
            bk = [b_ref[:, c] for c in cols]
            tails = tuple(jnp.where(first, 0.0, prev_ref[:, c].astype(F32)[HALO - 8:HALO]) for c in cols)

            def step(j, tails):
                r0 = pl.multiple_of(j * CHUNK, CHUNK)
                us, new_tails = [], []
                for h in range(2):
                    xv = cur_ref[pl.ds(r0, CHUNK), cols[h]].astype(F32)
                    us.append(_conv3_chunk(tails[h], xv, wk[h]) + bk[h])
                    u_ref[pl.ds(r0, CHUNK), cols[h]] = us[h].astype(u_ref.dtype)
                    new_tails.append(xv[CHUNK - 8:CHUNK])
                o_ref[pl.ds(r0, CHUNK), cols[0]] = (us[0] * _sigmoid(us[0]) * us[1]).astype(o_ref.dtype)
                return tuple(new_tails)

            lax.fori_loop(0, tm // CHUNK, step, tails, unroll=2)

    return _call_with_comm(
        body, n // tm,
        in_specs=[_prev_rows(tm, HALO, c2), _rows(tm, c2), _full((FFN_K, c2)), _full((1, c2))],
        out_specs=[_rows(tm, DFF), _rows(tm, c2)], out_shape=[S((n, DFF), BF), S((n, c2), BF)],
        scratch_shapes=[], args=(up, up, w, b), comm=comm, name=name)


def _ffn_act_bwd(up, u, dact, w, *, name, comm=None):
    n = up.shape[0]
    tm = _tile(n, 512)
    nt = n // tm
    c2 = 2 * DFF

    def body(x_ref, uc_ref, un_ref, dc_ref, dn_ref, w_ref, o_ref, dw_ref, db_ref, dbuf, acc):
        i = pl.program_id(0)
        last = i == nt - 1

        @pl.when(i == 0)
        def _():
            acc[...] = jnp.zeros_like(acc)

        def swiglu_bwd(gate, val, dav):
            sg = _sigmoid(gate)
            return dav * val * (sg * (1.0 + gate * (1.0 - sg))), dav * gate * sg

        for s in range(DFF // STRIP):
            cols = (pl.ds(s * STRIP, STRIP), pl.ds(DFF + s * STRIP, STRIP))
            wk = [[w_ref[k:k + 1, c] for k in range(FFN_K)] for c in cols]

            def step1(j, carry):
                r0 = pl.multiple_of(j * CHUNK, CHUNK)
                rows = pl.ds(r0, CHUNK)
                dus = swiglu_bwd(uc_ref[rows, cols[0]].astype(F32), uc_ref[rows, cols[1]].astype(F32),
                                 dc_ref[rows, cols[0]].astype(F32))
                for h in range(2):
                    dbuf[rows, cols[h]] = dus[h]
                return carry

            lax.fori_loop(0, tm // CHUNK, step1, 0, unroll=2)
            dus = swiglu_bwd(un_ref[:, cols[0]].astype(F32), un_ref[:, cols[1]].astype(F32),
                             jnp.where(last, 0.0, dn_ref[:, cols[0]].astype(F32)))
            for h in range(2):
                dbuf[tm:tm + HALO, cols[h]] = dus[h]

            def step2(j, carry):
                r0 = pl.multiple_of(j * CHUNK, CHUNK)
                rows = pl.ds(r0, CHUNK)
                for h in range(2):
                    win = dbuf[pl.ds(r0, CHUNK + 8), cols[h]]
                    blocks = _blocks8(win)
                    d0 = win[0:CHUNK]
                    d1 = jnp.concatenate(_advance_rows(blocks, 1), axis=0)
                    d2 = jnp.concatenate(_advance_rows(blocks, 2), axis=0)
                    o_ref[rows, cols[h]] = (wk[h][2] * d0 + wk[h][1] * d1 + wk[h][0] * d2).astype(o_ref.dtype)
                    xv = x_ref[rows, cols[h]].astype(F32)
                    acc[2, :, cols[h]] += d0 * xv
                    acc[1, :, cols[h]] += d1 * xv
                    acc[0, :, cols[h]] += d2 * xv
                    acc[FFN_K, :, cols[h]] += d0
                return carry

            lax.fori_loop(0, tm // CHUNK, step2, 0)

        @pl.when(last)
        def _():
            for k in range(FFN_K):
                dw_ref[k:k + 1, :] = _colsum(acc[k])
            db_ref[...] = _colsum(acc[FFN_K])

    return _call_with_comm(
        body, nt,
        in_specs=[_rows(tm, c2), _rows(tm, c2), _next_rows(tm, HALO, c2, n),
                  _rows(tm, DFF), _next_rows(tm, HALO, DFF, n), _full((FFN_K, c2))],
        out_specs=[_rows(tm, c2), _full((FFN_K, c2)), _full((1, c2))],
        out_shape=[S((n, c2), BF), S((FFN_K, c2), F32), S((1, c2), F32)],
        scratch_shapes=[pltpu.VMEM((tm + HALO, c2), F32), pltpu.VMEM((FFN_K + 1, CHUNK, c2), F32)],
        args=(up, u, u, dact, dact, w), comm=comm, name=name)


def _head_consts():
    lane = jnp.arange(INNER) // HEAD_DIM
    rep = (jnp.arange(128)[:, None] == lane[None, :]).astype(BF)
    return rep, rep.T


def _split_dot(v, m):
    hi = v.astype(BF)
    lo = (v - hi.astype(F32)).astype(BF)
    return _dot(hi, m) + _dot(lo, m)


def _split3(v):
    hi = v.astype(BF)
    r1 = v - hi.astype(F32)
    mid = r1.astype(BF)
    lo = (r1 - mid.astype(F32)).astype(BF)
    return hi, mid, lo


def _mask_dot(mask, v, dims, mask_first):
    m = mask.astype(BF)
    parts = [_dot(m, t, dims) if mask_first else _dot(t, m, dims) for t in _split3(v)]
    return (parts[0] + parts[1]) + parts[2]


def _chunk_decay_terms(dt_raw, dtb, alog, rep):
    row = lax.broadcasted_iota(jnp.int32, (Q, Q), 0)
    col = lax.broadcasted_iota(jnp.int32, (Q, Q), 1)
    lower = row >= col
    upper = col >= row
    dt = _softplus(dt_raw + dtb)
    a = -jnp.exp(alog)
    adt = dt * a
    acum = _mask_dot(lower, adt, NN, True)
    acum_t = _mask_dot(upper, adt, TN, False)
    alast = acum[Q - 1:Q, :]
    e = jnp.exp(acum)
    f = jnp.exp(alast - acum)
    ex = _split_dot(jnp.concatenate([dt, e, f, jnp.broadcast_to(jnp.exp(alast), (8, 128))], axis=0), rep)
    return dict(lower=lower, upper=upper, dt=dt, a=a, acum=acum, acum_t=acum_t, alast=alast,
                dt_x=ex[0:Q], e_x=ex[Q:2 * Q], f_x=ex[2 * Q:3 * Q], cd_x=ex[3 * Q:3 * Q + 1])


def _block_diag2(v, lo):
    return jnp.concatenate([jnp.where(lo, v, 0.0), jnp.where(lo, 0.0, v)], axis=0).astype(BF)


def _ssd_fwd(xbc_act, proj, dt_bias, a_log, d_x, norm_g, *, name, comm=None):
    n = xbc_act.shape[0]
    nc = n // Q
    rep, _ = _head_consts()

    def body(xs_ref, bc_ref, dt_ref, z_ref, dtb_ref, alog_ref, dx_ref, ng_ref, rep_ref, y_ref, yn_ref, hp_ref,
             h_scr, y_scr):
        i = pl.program_id(0)

        @pl.when(i == 0)
        def _():
            h_scr[...] = jnp.zeros_like(h_scr)

        hp_ref[...] = h_scr[...]
        t = _chunk_decay_terms(dt_ref[...].astype(F32), dtb_ref[...], alog_ref[...], rep_ref[...])
        xs = xs_ref[...].astype(F32)
        xt = xs * t["dt_x"]
        lo = lax.broadcasted_iota(jnp.int32, (Q, 128), 1) < HEAD_DIM
        gw = INNER // GROUPS
        for g in range(GROUPS):
            bm = bc_ref[:, g * NSTATE:(g + 1) * NSTATE]
            cm = bc_ref[:, GROUPS * NSTATE + g * NSTATE:GROUPS * NSTATE + (g + 1) * NSTATE]
            cb = _dot(cm, bm, NT)
            hg = h_scr[:, g * gw:(g + 1) * gw]
            yoff = _dot(cm, hg.astype(BF))
            for jj in range(gw // 128):
                p = g * (gw // 128) + jj
                sl = slice(p * 128, (p + 1) * 128)
                ws = []
                for hd in (2 * p, 2 * p + 1):
                    seg = t["acum"][:, hd:hd + 1] - t["acum_t"][hd:hd + 1, :]
                    ws.append((cb * jnp.exp(jnp.where(t["lower"], seg, -jnp.inf))).astype(BF))
                ydiag = _dot(jnp.concatenate(ws, axis=1), _block_diag2(xt[:, sl], lo))
                y_scr[:, sl] = ydiag + yoff[:, jj * 128:(jj + 1) * 128] * t["e_x"][:, sl] + dx_ref[:, sl] * xs[:, sl]
            xf = (xt[:, g * gw:(g + 1) * gw] * t["f_x"][:, g * gw:(g + 1) * gw]).astype(BF)
            h_scr[:, g * gw:(g + 1) * gw] = hg * t["cd_x"][:, g * gw:(g + 1) * gw] + _dot(bm, xf, TN)
        y = y_scr[...]
        y_ref[...] = y.astype(y_ref.dtype)
        z = z_ref[...].astype(F32)
        v = y * z * _sigmoid(z)
        for g in range(GROUPS):
            vg = v[:, g * gw:(g + 1) * gw]
            r = lax.rsqrt(jnp.mean(vg * vg, axis=-1, keepdims=True) + EPS)
            yn_ref[:, g * gw:(g + 1) * gw] = (vg * r * ng_ref[:, g * gw:(g + 1) * gw]).astype(yn_ref.dtype)

    vec = _full((1, INNER))
    hv = _full((1, 128))
    return _call_with_comm(
        body, nc,
        in_specs=[_rows(Q, INNER, 0), _rows(Q, 2 * GROUPS * NSTATE, INNER // (2 * GROUPS * NSTATE)),
                  _rows(Q, DT_PAD, P_DT // DT_PAD), _rows(Q, INNER, P_Z // INNER),
                  hv, hv, vec, vec, _full((128, INNER))],
        out_specs=[_rows(Q, INNER), _rows(Q, INNER), pl.BlockSpec((None, NSTATE, INNER), lambda i: (i, 0, 0))],
        out_shape=[S((n, INNER), BF), S((n, INNER), BF), S((nc, NSTATE, INNER), F32)],
        scratch_shapes=[pltpu.VMEM((NSTATE, INNER), F32), pltpu.VMEM((Q, INNER), F32)],
        args=(xbc_act, xbc_act, proj, proj, dt_bias, a_log, d_x, norm_g, rep), comm=comm, name=name)


def _ssd_bwd(xbc_act, proj, y, dyn, hprev, dproj, dt_bias, a_log, d_x, norm_g, *, name):
    n = xbc_act.shape[0]
    nc = n // Q
    rep, sel = _head_consts()
    gw = INNER // GROUPS

    def rev(w, cb=0):
        return pl.BlockSpec((Q, w), lambda i: (nc - 1 - i, cb))

    def body(xs_ref, bc_ref, dt_ref, z_ref, y_ref, dyn_ref, hp_ref, dtb_ref, alog_ref, dx_ref, ng_ref, rep_ref,
             sel_ref, dproj_in, dz_ref, ddt_ref, dxbc_ref, dng_ref, ddtb_ref, dalog_ref, dd_ref,
             dh_scr, dxt_scr, st_scr, off_scr, rs_scr, cs_scr, dng_acc, ddtb_acc, da_acc, dd_acc):
        del dproj_in
        i = pl.program_id(0)

        @pl.when(i == 0)
        def _():
            for r in (dh_scr, dng_acc, ddtb_acc, da_acc, dd_acc):
                r[...] = jnp.zeros_like(r)

        y = y_ref[...].astype(F32)
        z = z_ref[...].astype(F32)
        sz = _sigmoid(z)
        silu = z * sz
        v = y * silu
        dyn = dyn_ref[...].astype(F32)
        dvs = []
        for g in range(GROUPS):
            gs = slice(g * gw, (g + 1) * gw)
            vg = v[:, gs]
            r = lax.rsqrt(jnp.mean(vg * vg, axis=-1, keepdims=True) + EPS)
            vn = vg * r
            dvn = dyn[:, gs] * ng_ref[:, gs]
            dng_acc[:, gs] += _colsum(dyn[:, gs] * vn)
            dvs.append(r * (dvn - vn * jnp.mean(dvn * vn, axis=-1, keepdims=True)))
        dv = jnp.concatenate(dvs, axis=1)
        dy = dv * silu
        dz_ref[...] = (dv * y * (sz * (1.0 + z * (1.0 - sz)))).astype(dz_ref.dtype)

        dt_raw = dt_ref[...].astype(F32)
        t = _chunk_decay_terms(dt_raw, dtb_ref[...], alog_ref[...], rep_ref[...])
        xs = xs_ref[...].astype(F32)
        dsk = dx_ref[...]
        dd_acc[...] += _colsum(dy * xs)
        xt = xs * t["dt_x"]
        dye = dy * t["e_x"]
        xtf = xt * t["f_x"]
        hp = hp_ref[...]
        dh = dh_scr[...]
        lo = lax.broadcasted_iota(jnp.int32, (Q, 128), 1) < HEAD_DIM
        rs_scr[...] = jnp.zeros_like(rs_scr)
        cs_scr[...] = jnp.zeros_like(cs_scr)
        for g in range(GROUPS):
            gs = slice(g * gw, (g + 1) * gw)
            bm = bc_ref[:, g * NSTATE:(g + 1) * NSTATE]
            cm = bc_ref[:, GROUPS * NSTATE + g * NSTATE:GROUPS * NSTATE + (g + 1) * NSTATE]
            cbt = _dot(bm, cm, NT)
            dhg = dh[:, gs].astype(BF)
            hpg = hp[:, gs].astype(BF)
            dxt_state = _dot(bm, dhg) * t["f_x"][:, gs]
            st_scr[:, gs] = dxt_state
            dye_g = dye[:, gs]
            off_scr[:, gs] = dye_g * _dot(cm, hpg)
            dye_b = dye_g.astype(BF)
            db = _dot(xtf[:, gs].astype(BF), dhg, NT)
            dc = _dot(dye_b, hpg, NT)
            dh_scr[:, gs] = t["cd_x"][:, gs] * dh[:, gs] + _dot(cm, dye_b, TN)
            dcbt = jnp.zeros((Q, Q), F32)
            for jj in range(gw // 128):
                p = g * (gw // 128) + jj
                sl = slice(p * 128, (p + 1) * 128)
                lts, wfs = [], []
                for hd in (2 * p, 2 * p + 1):
                    seg_t = t["acum_t"][hd:hd + 1, :] - t["acum"][:, hd:hd + 1]
                    lt = jnp.exp(jnp.where(t["upper"], seg_t, -jnp.inf))
                    lts.append(lt)
                    wfs.append(cbt * lt)
                dyp = dy[:, sl]
                dxt_diag = _dot(jnp.concatenate([w.astype(BF) for w in wfs], axis=1), _block_diag2(dyp, lo))
                dwt2 = _dot(_block_diag2(xt[:, sl], lo), dyp.astype(BF), NT)
                for k, hd in enumerate((2 * p, 2 * p + 1)):
                    dwt = dwt2[k * Q:(k + 1) * Q]
                    dcbt = dcbt + dwt * lts[k]
                    mt = dwt * wfs[k]
                    rs_scr[hd:hd + 1, :] = _colsum(mt)
                    cs_scr[:, hd:hd + 1] = jnp.sum(mt, axis=1, keepdims=True)
                dxt_scr[:, sl] = dxt_diag + dxt_state[:, jj * 128:(jj + 1) * 128]
            dcbt_b = dcbt.astype(BF)
            db = db + _dot(dcbt_b, cm)
            dc = dc + _dot(dcbt_b, bm, TN)
            dxbc_ref[:, INNER + g * NSTATE:INNER + (g + 1) * NSTATE] = db.astype(dxbc_ref.dtype)
            dxbc_ref[:, INNER + (GROUPS + g) * NSTATE:INNER + (GROUPS + g + 1) * NSTATE] = dc.astype(dxbc_ref.dtype)
        dxt = dxt_scr[...]
        dst = st_scr[...]
        sel_m = sel_ref[...]
        sums = _split_dot(jnp.concatenate([off_scr[...], xs * dst, xs * dxt], axis=0), sel_m)
        r1_off, r3_state, r3 = sums[0:Q], sums[Q:2 * Q], sums[2 * Q:3 * Q]
        t1 = _colsum(xt * dst)
        t2 = _colsum(dh * hp)
        tails = _split_dot(jnp.concatenate([jnp.broadcast_to(t1, (8, INNER)), jnp.broadcast_to(t2, (8, INNER))], axis=0),
                           sel_m)
        extra = tails[0:1] + jnp.exp(t["alast"]) * tails[8:9]
        last_row = lax.broadcasted_iota(jnp.int32, (Q, 128), 0) == Q - 1
        da_cum = (rs_scr[...].T - cs_scr[...]) + r1_off - t["dt"] * r3_state + jnp.where(last_row, extra, 0.0)
        dadt = _mask_dot(t["upper"], da_cum, NN, True)
        ddt = r3 + t["a"] * dadt
        da_acc[...] += _colsum(dadt * t["dt"])
        real = lax.broadcasted_iota(jnp.int32, (Q, 128), 1) < HEADS
        ddraw = jnp.where(real, ddt * _sigmoid(dt_raw + dtb_ref[...]), 0.0)
        ddt_ref[...] = ddraw.astype(ddt_ref.dtype)
        ddtb_acc[...] += _colsum(ddraw)
        dxbc_ref[:, 0:INNER] = (dy * dsk + dxt * t["dt_x"]).astype(dxbc_ref.dtype)

        @pl.when(i == nc - 1)
        def _():
            dng_ref[...] = dng_acc[...]
            ddtb_ref[...] = ddtb_acc[...]
            dalog_ref[...] = da_acc[...] * t["a"]
            dd_ref[...] = _split_dot(jnp.broadcast_to(dd_acc[...], (8, INNER)), sel_m)[0:1]

    vec = _full((1, INNER))
    hv = _full((1, 128))
    return pl.pallas_call(
        body, grid=(nc,),
        in_specs=[rev(INNER, 0), rev(2 * GROUPS * NSTATE, INNER // (2 * GROUPS * NSTATE)),
                  rev(DT_PAD, P_DT // DT_PAD), rev(INNER, P_Z // INNER), rev(INNER), rev(INNER),
                  pl.BlockSpec((None, NSTATE, INNER), lambda i: (nc - 1 - i, 0, 0)),
                  hv, hv, vec, vec, _full((128, INNER)), _full((INNER, 128)), pl.BlockSpec(memory_space=pl.ANY)],
        out_specs=[rev(INNER, P_Z // INNER), rev(DT_PAD), rev(XBC), vec, hv, hv, hv],
        out_shape=[S(dproj.shape, dproj.dtype), S((n, DT_PAD), BF), S((n, XBC), BF),
                   S((1, INNER), F32), S((1, 128), F32), S((1, 128), F32), S((1, 128), F32)],
        scratch_shapes=[pltpu.VMEM((NSTATE, INNER), F32), pltpu.VMEM((Q, INNER), F32), pltpu.VMEM((Q, INNER), F32),
                        pltpu.VMEM((Q, INNER), F32), pltpu.VMEM((128, Q), F32), pltpu.VMEM((Q, 128), F32),
                        pltpu.VMEM((1, INNER), F32), pltpu.VMEM((1, 128), F32), pltpu.VMEM((1, 128), F32),
                        pltpu.VMEM((1, INNER), F32)],
        input_output_aliases={13: 0},
        compiler_params=_cparams("arbitrary"), name=name)(
            xbc_act, xbc_act, proj, proj, y, dyn, hprev, dt_bias, a_log, d_x, norm_g, rep, sel, dproj)


def _mixer_out_fwd(yn, a_conf, a_sc, proj, b_gate, w_ssd, w_conf, w_sc, w_o, x, mod, *, name):
    n = x.shape[0]
    tm = _tile(n, 256)

    def body(yn_ref, ac_ref, as_ref, gt_ref, bg_ref, wa_ref, wb_ref, wc_ref, wo_ref, x_ref, mod_ref,
             ya_ref, yb_ref, yc_ref, mg_ref, mix_ref, xn_ref):
        ya = _dot(yn_ref[...], wa_ref[...])
        yb = _dot(ac_ref[...], wb_ref[...])
        yc = _dot(as_ref[...], wc_ref[...])
        ya_ref[...] = ya.astype(ya_ref.dtype)
        yb_ref[...] = yb.astype(yb_ref.dtype)
        yc_ref[...] = yc.astype(yc_ref.dtype)
        g = _sigmoid(gt_ref[...].astype(F32) + bg_ref[...])
        merged = (g[:, 0:D] * ya + g[:, D:2 * D] * yb + g[:, 2 * D:] * yc).astype(mg_ref.dtype)
        mg_ref[...] = merged
        mix = _dot(merged, wo_ref[...])
        mix_ref[...] = mix.astype(mix_ref.dtype)
        xn_ref[...] = x_ref[...] + mod_ref[2:3, :] * mix

    act = S((n, D), BF)
    return pl.pallas_call(
        body, grid=(n // tm,),
        in_specs=[_rows(tm, INNER), _rows(tm, CONF_W), _rows(tm, SC_W), _rows(tm, 3 * D, P_GATES // (3 * D)),
                  _full((1, 3 * D)), _full((INNER, D)), _full((CONF_W, D)), _full((SC_W, D)), _full((D, D)),
                  _rows(tm, D), _full((3, D))],
        out_specs=[_rows(tm, D)] * 6,
        out_shape=[act, act, act, act, act, S((n, D), F32)],
        compiler_params=_cparams("parallel"), name=name)(yn, a_conf, a_sc, proj, b_gate, w_ssd, w_conf, w_sc, w_o, x, mod)


def _mixer_out_bwd(dx, mix, ya, yb, yc, proj, b_gate, w_ssd, w_conf, w_sc, w_o, mod, *, name):
    n = dx.shape[0]
    tm = _tile(n, 256)

    def body(dx_ref, mix_ref, ya_ref, yb_ref, yc_ref, gt_ref, bg_ref, wa_ref, wb_ref, wc_ref, wo_ref, mod_ref,
             do_ref, dya_ref, dyb_ref, dyc_ref, dgt_ref, dyn_ref, dac_ref, das_ref, dgm_ref, dbg_ref, dm_scr, acc):
        i = pl.program_id(0)
        last = i == pl.num_programs(0) - 1

        @pl.when(i == 0)
        def _():
            acc[...] = jnp.zeros_like(acc)

        do_ref[...] = (dx_ref[...] * mod_ref[2:3, :]).astype(BF)
        dm_scr[...] = _dot(do_ref[...], wo_ref[...], NT)
        branches = ((ya_ref, dya_ref), (yb_ref, dyb_ref), (yc_ref, dyc_ref))
        for s_ in range(D // STRIP):
            c = pl.ds(s_ * STRIP, STRIP)

            def step(j, carry):
                rows = pl.ds(pl.multiple_of(j * CHUNK, CHUNK), CHUNK)
                dmv = dm_scr[rows, c]
                acc[3, :, c] += dx_ref[rows, c] * mix_ref[rows, c].astype(F32)
                for jb, (y_ref, o_ref) in enumerate(branches):
                    cj = pl.ds(jb * D + s_ * STRIP, STRIP)
                    gj = _sigmoid(gt_ref[rows, cj].astype(F32) + bg_ref[:, cj])
                    o_ref[rows, c] = (dmv * gj).astype(o_ref.dtype)
                    dgpre = dmv * y_ref[rows, c].astype(F32) * gj * (1.0 - gj)
                    dgt_ref[rows, cj] = dgpre.astype(dgt_ref.dtype)
                    acc[jb, :, c] += dgpre
                return carry

            lax.fori_loop(0, tm // CHUNK, step, 0)
        dyn_ref[...] = _dot(dya_ref[...], wa_ref[...], NT).astype(dyn_ref.dtype)
        dac_ref[...] = _dot(dyb_ref[...], wb_ref[...], NT).astype(dac_ref.dtype)
        das_ref[...] = _dot(dyc_ref[...], wc_ref[...], NT).astype(das_ref.dtype)

        @pl.when(last)
        def _():
            for jb in range(3):
                dbg_ref[:, jb * D:(jb + 1) * D] = _colsum(acc[jb])
            dgm_ref[...] = _colsum(acc[3])

    act = S((n, D), BF)
    return pl.pallas_call(
        body, grid=(n // tm,),
        in_specs=[_rows(tm, D)] * 5 + [_rows(tm, 3 * D, P_GATES // (3 * D)), _full((1, 3 * D)), _full((INNER, D)),
                                       _full((CONF_W, D)), _full((SC_W, D)), _full((D, D)), _full((3, D))],
        out_specs=[_rows(tm, D)] * 4 + [_rows(tm, 3 * D, P_GATES // (3 * D)), _rows(tm, INNER), _rows(tm, CONF_W),
                                        _rows(tm, SC_W), _full((1, D)), _full((1, 3 * D))],
        out_shape=[act, act, act, act, S((n, PW), BF), S((n, INNER), BF), S((n, CONF_W), BF), S((n, SC_W), BF),
                   S((1, D), F32), S((1, 3 * D), F32)],
        scratch_shapes=[pltpu.VMEM((tm, D), F32), pltpu.VMEM((4, CHUNK, D), F32)],
        compiler_params=_cparams("arbitrary"), name=name)(dx, mix, ya, yb, yc, proj, b_gate, w_ssd, w_conf, w_sc, w_o, mod)


W_IN_SHARD = N_IN // N_CHIPS
W_IN_SEGMENTS = ((R_Z, R_XBC, P_Z), (R_XBC, R_DT, P_XBC), (R_DT, R_CONF, P_DT), (R_CONF, R_SC, P_CONF),
                 (R_SC, R_GATES, P_SC), (R_GATES, N_IN, P_GATES))


def _pad_w_in_chips(w4):
    parts = []
    for lo, hi, dst in sorted(W_IN_SEGMENTS, key=lambda sgm: sgm[2]):
        for k in range(N_CHIPS):
            a, b = max(lo, k * W_IN_SHARD), min(hi, (k + 1) * W_IN_SHARD)
            if a < b:
                parts.append((dst + a - lo, w4[k][:, a - k * W_IN_SHARD:b - k * W_IN_SHARD]))
    out, pos = [], 0
    for start, piece in parts:
        if start > pos:
            out.append(jnp.zeros((w4.shape[1], start - pos), w4.dtype))
        out.append(piece)
        pos = start + piece.shape[1]
    if pos < PW:
        out.append(jnp.zeros((w4.shape[1], PW - pos), w4.dtype))
    return jnp.concatenate(out, axis=1)


def _unpad_w_in_chips(wp):
    blocks = []
    for k in range(N_CHIPS):
        pieces = []
        for lo, hi, dst in W_IN_SEGMENTS:
            a, b = max(lo, k * W_IN_SHARD), min(hi, (k + 1) * W_IN_SHARD)
            if a < b:
                pieces.append(wp[:, dst + a - lo:dst + b - lo])
        blocks.append(jnp.concatenate(pieces, axis=1))
    return jnp.stack(blocks)


def _row(v):
    return v.reshape(1, -1)


def _head_row(v):
    return jnp.pad(v, (0, 128 - HEADS)).reshape(1, 128)


def _layer_fwd(x, p, mod_mix, mod_ffn, tag, comm=None, comm_ssd=None, late_params=None, comm_up=None, comm_proj=None,
               mid_params=None):
    sv = {"x0": x}
    h = _modnorm_fwd(x, _row(p["norm_mix_g"]), mod_mix, name=f"modnorm_mix_fwd{tag}")
    proj = _mm_nn(h, p["w_in_pad"], out_dtype=BF, tm=1024, tn=2048, name=f"proj_fwd{tag}", comm=comm_proj)
    if comm_proj is not None:
        proj, proj_comm_out = proj
        p.update(mid_params(proj_comm_out))
    xbc_act, cpre = _ssd_pre_fwd(proj, p["ssd_conv_w"], _row(p["ssd_conv_b"]), name=f"ssd_pre_fwd{tag}")
    d_x = _row(jnp.repeat(p["ssd_d"], HEAD_DIM))
    (y, yn, hprev), ssd_comm_out = _ssd_fwd(xbc_act, proj, _head_row(p["ssd_dt_bias"]), _head_row(p["ssd_a_log"]), d_x,
                                            _row(p["ssd_norm_g"]), name=f"ssd_fwd{tag}", comm=comm_ssd)
    if late_params is not None:
        p.update(late_params(ssd_comm_out))
    a_conf, uc = _conf_fwd(proj, p["conf_conv_w"], _row(p["conf_conv_b"]), _row(p["conf_ln_g"]), _row(p["conf_ln_b"]),
                           name=f"conf_fwd{tag}")
    a_sc = _sc_fwd(proj, p["sc_conv_w"], name=f"sc_fwd{tag}")
    ya, yb, yc, merged, mix, x1 = _mixer_out_fwd(yn, a_conf, a_sc, proj, _row(p["b_gate"]), p["w_ssd_out"],
                                                 p["w_conf_out"], p["w_sc_out"], p["w_o"], x, mod_mix,
                                                 name=f"mixer_out_fwd{tag}")
    h2 = _modnorm_fwd(x1, _row(p["norm_ffn_g"]), mod_ffn, name=f"modnorm_ffn_fwd{tag}")
    up = _mm_nn_chips(h2, p["w_up4"], out_dtype=BF, tm=1024, name=f"up_fwd{tag}", comm=comm_up)
    up_comm_out = ()
    if comm_up is not None:
        up, up_comm_out = up
    (act, u_ffn), comm_out = _ffn_act_fwd(up, p["ffn_conv_w"], _row(p["ffn_conv_b"]), name=f"ffn_act_fwd{tag}", comm=comm)
    o, x2 = _mm_resid(act, p["w_down"], x1, mod_ffn, tm=512, name=f"down_fwd{tag}")
    sv.update(h=h, proj=proj, xbc_act=xbc_act, cpre=cpre, d_x=d_x, y=y, yn=yn, hprev=hprev, a_conf=a_conf, uc=uc, a_sc=a_sc,
              ya=ya, yb=yb, yc=yc, merged=merged, mix=mix, x1=x1, h2=h2, up=up, u_ffn=u_ffn, act=act, o=o)
    return x2, sv, tuple(up_comm_out) + tuple(comm_out)


def _layer_bwd(dx, p, sv, mod_mix, mod_ffn, tag, comm=None, make_comm_conf=None):
    g = {}
    do2, dgate_ffn = _gate_bwd(dx, sv["o"], mod_ffn, name=f"gate_ffn_bwd{tag}")
    dact = _mm_nt(do2, p["w_down"], out_dtype=BF, tm=1024, tk=D, name=f"down_dx{tag}")
    g["w_down"] = _mm_tn(sv["act"], do2, tn=D, tk=1024, name=f"down_dw{tag}")
    (dup, g["ffn_conv_w"], dffn_b), comm_out = _ffn_act_bwd(sv["up"], sv["u_ffn"], dact, p["ffn_conv_w"],
                                                            name=f"ffn_act_bwd{tag}", comm=comm)
    g["ffn_conv_b"] = dffn_b[0]
    dh2 = _mm_nt_chips(dup, p["w_up4"], out_dtype=F32, tm=1024, name=f"up_dx{tag}")
    g["w_up4"] = _mm_tn(sv["h2"], dup, tn=2 * DFF // N_CHIPS, tk=2048, by_chip=True, name=f"up_dw{tag}")
    dx1, dgn, dsh, dsc = _modnorm_bwd(dh2, sv["x1"], dx, _row(p["norm_ffn_g"]), mod_ffn, name=f"modnorm_ffn_bwd{tag}")
    g["norm_ffn_g"] = dgn[0]
    dmod_ffn = jnp.concatenate([dsh[0], dsc[0], dgate_ffn[0]])

    (do1, dya, dyb, dyc, dproj, dyn, dac, dasc, dgate_mix, dbg) = _mixer_out_bwd(
        dx1, sv["mix"], sv["ya"], sv["yb"], sv["yc"], sv["proj"], _row(p["b_gate"]), p["w_ssd_out"], p["w_conf_out"],
        p["w_sc_out"], p["w_o"], mod_mix, name=f"mixer_out_bwd{tag}")
    g["b_gate"] = dbg[0]
    g["w_o"] = _mm_tn(sv["merged"], do1, tn=D, tk=2048, name=f"wo_dw{tag}")
    g["w_ssd_out"] = _mm_tn(sv["yn"], dya, tn=D, tk=2048, name=f"wssd_dw{tag}")
    g["w_conf_out"] = _mm_tn(sv["a_conf"], dyb, tn=D, tk=2048, name=f"wconf_dw{tag}")
    g["w_sc_out"] = _mm_tn(sv["a_sc"], dyc, tn=D, tk=2048, name=f"wsc_dw{tag}")
    comm_conf = make_comm_conf(g) if make_comm_conf is not None else None
    (dproj, g["conf_conv_w"], dcb, dlg, dlb), conf_comm_out = _conf_bwd(
        sv["proj"], sv["uc"], dac, dproj, p["conf_conv_w"], _row(p["conf_ln_g"]), _row(p["conf_ln_b"]),
        name=f"conf_bwd{tag}", comm=comm_conf)
    g["conf_conv_b"], g["conf_ln_g"], g["conf_ln_b"] = dcb[0], dlg[0], dlb[0]
    dproj, g["sc_conv_w"] = _sc_bwd(sv["proj"], dasc, dproj, p["sc_conv_w"], name=f"sc_bwd{tag}")
    dproj, ddt, dxbc_act, dng, ddtb, dalog, ddd = _ssd_bwd(
        sv["xbc_act"], sv["proj"], sv["y"], dyn, sv["hprev"], dproj, _head_row(p["ssd_dt_bias"]),
        _head_row(p["ssd_a_log"]), sv["d_x"], _row(p["ssd_norm_g"]), name=f"ssd_bwd{tag}")
    g["ssd_norm_g"], g["ssd_dt_bias"], g["ssd_a_log"], g["ssd_d"] = dng[0], ddtb[0, :HEADS], dalog[0, :HEADS], ddd[0, :HEADS]
    dproj, g["ssd_conv_w"], dsb = _ssd_pre_bwd(sv["proj"], sv["cpre"], dxbc_act, ddt, dproj, p["ssd_conv_w"],
                                               name=f"ssd_pre_bwd{tag}")
    g["ssd_conv_b"] = dsb[0]
    dh = _mm_nt(dproj, p["w_in_pad"], out_dtype=F32, tm=1024, tk=4096, name=f"proj_dx{tag}")
    g["w_in_pad"] = _mm_tn(sv["h"], dproj, tn=2048, tk=2048, name=f"proj_dw{tag}")
    dx0, dgn, dsh, dsc = _modnorm_bwd(dh, sv["x0"], dx1, _row(p["norm_mix_g"]), mod_mix, name=f"modnorm_mix_bwd{tag}")
    g["norm_mix_g"] = dgn[0]
    dmod_mix = jnp.concatenate([dsh[0], dsc[0], dgate_mix[0]])
    return dx0, g, dmod_mix, dmod_ffn, (comm_out, conf_comm_out)


MESH = pl.DeviceIdType.MESH
ANY = pl.BlockSpec(memory_space=pl.ANY)
VMEM = pl.BlockSpec(memory_space=pltpu.VMEM)


def _mesh_pos():
    return lax.axis_index("x"), lax.axis_index("y"), lax.axis_index("c")


def _peer(pos, mask):
    return tuple(1 - v if (mask >> (2 - k)) & 1 else v for k, v in enumerate(pos))


def _lin(pos):
    return 4 * pos[0] + 2 * pos[1] + pos[2]


def _chip(pos):
    return 2 * pos[0] + pos[1]


def _rcopy(src, dst, send_sem, recv_sem, dev):
    return pltpu.make_async_remote_copy(src_ref=src, dst_ref=dst, send_sem=send_sem, recv_sem=recv_sem,
                                        device_id=dev, device_id_type=MESH)


CHIP_MASKS = (2, 4, 6)
SIBLING = 1
ADA_COLS = 3 * D // N_CHIPS
CONV_ROWS, CONV_COLS = 48, 2 * DFF // N_CHIPS
CONV_PACK = {"ffn_conv_w": (0, FFN_K, 2 * DFF // N_CHIPS), "ssd_conv_w": (3, SSD_K, XBC // N_CHIPS),
             "conf_conv_w": (8, CONF_K, CONF_W // N_CHIPS), "sc_conv_w": (40, SC_K, SC_W // N_CHIPS)}


def _ada_exchange(c_blk, ada_mix_w, ada_ffn_w, conv_pack):
    def body(c_ref, wm_ref, wf_ref, cw_ref, mods_ref, sc_ref, cwall_ref,
             call_scr, modp_scr, recv_scr, s1, r1, s3, r3, s4, r4):
        pos = _mesh_pos()
        me, km = _lin(pos), _chip(pos)
        call_scr[me] = c_ref[...]
        cwall_ref[km] = cw_ref[...]
        sends = []
        for m in range(1, N_DEV):
            sends.append(_rcopy(c_ref, call_scr.at[me], s1.at[m - 1], r1.at[m - 1], _peer(pos, m)))
        for j, m in enumerate(CHIP_MASKS):
            sends.append(_rcopy(cw_ref, cwall_ref.at[km], s4.at[j], r4.at[j], _peer(pos, m)))
        for cp in sends:
            cp.start()
        for m in range(1, N_DEV):
            src = _peer(pos, m)
            _rcopy(c_ref, call_scr.at[_lin(src)], s1.at[m - 1], r1.at[m - 1], src).wait_recv()
        cm = jnp.concatenate([call_scr[d, 0:1, :] for d in range(N_DEV)], axis=0)
        sc = cm * _sigmoid(cm)
        sc_ref[...] = sc
        for j, w in enumerate((wm_ref.at[0], wm_ref.at[1], wf_ref.at[0], wf_ref.at[1])):
            modp_scr[:, j * ADA_COLS:(j + 1) * ADA_COLS] = lax.dot_general(
                sc, w[...], NN, precision=HIGHEST, preferred_element_type=F32)
        recv_scr[km] = modp_scr[...]
        sends3 = [_rcopy(modp_scr, recv_scr.at[km], s3.at[j], r3.at[j], _peer(pos, m)) for j, m in enumerate(CHIP_MASKS)]
        for cp in sends3:
            cp.start()
        for j, m in enumerate(CHIP_MASKS):
            src = _peer(pos, m)
            _rcopy(modp_scr, recv_scr.at[_chip(src)], s3.at[j], r3.at[j], src).wait_recv()
            _rcopy(cw_ref, cwall_ref.at[_chip(src)], s4.at[j], r4.at[j], src).wait_recv()
        for k in range(N_CHIPS):
            mods_ref[k:k + 1, :] = recv_scr[k, pl.ds(me, 1), :]
        for cp in sends + sends3:
            cp.wait_send()

    dma = pltpu.SemaphoreType.DMA
    return pl.pallas_call(
        body, in_specs=[VMEM] * 4, out_specs=[VMEM] * 3,
        out_shape=[S((N_CHIPS, 4 * ADA_COLS), F32), S((N_DEV, D), F32), S((N_CHIPS,) + conv_pack.shape, F32)],
        scratch_shapes=[pltpu.VMEM((N_DEV, 8, D), F32), pltpu.VMEM((N_DEV, 4 * ADA_COLS), F32),
                        pltpu.VMEM((N_CHIPS, N_DEV, 4 * ADA_COLS), F32),
                        dma((N_DEV - 1,)), dma((N_DEV - 1,)), dma((3,)), dma((3,)), dma((3,)), dma((3,))],
        compiler_params=pltpu.CompilerParams(vmem_limit_bytes=VMEM_LIMIT_V7X), name="ada_exchange")(
            c_blk, ada_mix_w, ada_ffn_w, conv_pack)


class _Comm:
    def __init__(self, ins, out_shapes, scratch, start, finish, mid=None):
        self.ins, self.out_shapes, self.scratch, self.start, self.finish = list(ins), list(out_shapes), list(scratch), start, finish
        self.mid = mid


def _call_with_comm(body, nsteps, *, in_specs, out_specs, out_shape, scratch_shapes, args, comm, name, aliases=None):
    grid = nsteps if isinstance(nsteps, tuple) else (nsteps,)
    sem = ("arbitrary",) * len(grid)
    if comm is None:
        res = pl.pallas_call(body, grid=grid, in_specs=in_specs, out_specs=out_specs, out_shape=out_shape,
                             scratch_shapes=scratch_shapes, input_output_aliases=aliases or {},
                             compiler_params=_cparams(*sem), name=name)(*args)
        return tuple(res), ()
    ni, no, ns = len(in_specs), len(out_specs), len(scratch_shapes)
    ci, co = len(comm.ins), len(comm.out_shapes)

    def hosted(*refs):
        ins, cins = refs[:ni], refs[ni:ni + ci]
        outs, couts = refs[ni + ci:ni + ci + no], refs[ni + ci + no:ni + ci + no + co]
        scr, csem = refs[ni + ci + no + co:ni + ci + no + co + ns], refs[ni + ci + no + co + ns:]
        step, total = pl.program_id(0), grid[0]
        for d in range(1, len(grid)):
            step, total = step * grid[d] + pl.program_id(d), total * grid[d]

        @pl.when(step == 0)
        def _():
            comm.start(cins, couts, csem)

        body(*ins, *outs, *scr)
        if comm.mid is not None:
            @pl.when(step == max(0, total - 1 - max(2, total // 6)))
            def _():
                comm.mid(cins, couts, csem)

        @pl.when(step == total - 1)
        def _():
            comm.finish(cins, couts, csem)

    res = pl.pallas_call(
        hosted, grid=grid, in_specs=list(in_specs) + [ANY] * ci, out_specs=list(out_specs) + [ANY] * co,
        out_shape=list(out_shape) + comm.out_shapes, scratch_shapes=list(scratch_shapes) + comm.scratch,
        input_output_aliases=aliases or {}, compiler_params=_cparams(*sem), name=name)(*args, *comm.ins)
    return tuple(res[:no]), tuple(res[no:])


def _run_comm(comm, name):
    def body(*refs):
        ci, co = len(comm.ins), len(comm.out_shapes)
        comm.start(refs[:ci], refs[ci:ci + co], refs[ci + co:])
        if comm.mid is not None:
            comm.mid(refs[:ci], refs[ci:ci + co], refs[ci + co:])
        comm.finish(refs[:ci], refs[ci:ci + co], refs[ci + co:])

    return pl.pallas_call(body, in_specs=[ANY] * len(comm.ins), out_specs=[ANY] * len(comm.out_shapes),
                          out_shape=comm.out_shapes, scratch_shapes=comm.scratch, name=name)(*comm.ins)


def _gather_comm(shards, layer):
    na = len(shards)
    dma = pltpu.SemaphoreType.DMA

    def ici(ins, outs, sems, pos, a, j, m, sender):
        ssem, rsem = sems[0], sems[1]
        peer = _peer(pos, m)
        block = outs[a].at[_chip(pos) if sender else _chip(peer)]
        return _rcopy(ins[a].at[layer], block, ssem.at[a, j], rsem.at[a, j], peer)

    def start(ins, outs, sems):
        pos = _mesh_pos()

        @pl.when(pos[2] == layer)
        def _():
            for a in range(na):
                for j, m in enumerate(CHIP_MASKS):
                    ici(ins, outs, sems, pos, a, j, m, True).start()

    def forward(outs, sems, pos, a, j, m):
        blk = outs[a].at[_chip(_peer(pos, m))]
        return _rcopy(blk, blk, sems[2].at[a, j], sems[3].at[a, j], _peer(pos, SIBLING))

    def mid(ins, outs, sems):
        pos = _mesh_pos()

        @pl.when(pos[2] == layer)
        def _():
            for a in range(na):
                for j, m in enumerate(CHIP_MASKS):
                    ici(ins, outs, sems, pos, a, j, m, False).wait_recv()
                    forward(outs, sems, pos, a, j, m).start()

    def finish(ins, outs, sems):
        pos = _mesh_pos()

        @pl.when(pos[2] == layer)
        def _():
            for a in range(na):
                for j, m in enumerate(CHIP_MASKS):
                    ici(ins, outs, sems, pos, a, j, m, True).wait_send()
                    forward(outs, sems, pos, a, j, m).wait_send()

        @pl.when(pos[2] != layer)
        def _():
            for a in range(na):
                for j, m in enumerate(CHIP_MASKS):
                    forward(outs, sems, pos, a, j, m).wait_recv()

    return _Comm(shards, [S((N_CHIPS,) + s.shape[1:], s.dtype) for s in shards],
                 [dma((na, 3)), dma((na, 3)), dma((na, 3)), dma((na, 3))], start, finish, mid)


def _scatter_comm(arrs, layer):
    na = len(arrs)
    dma = pltpu.SemaphoreType.DMA

    def copy(ins, outs, sems, pos, a, j, m, sender):
        peer = _peer(pos, m)
        src = ins[a].at[_chip(peer) if sender else _chip(pos)]
        dst = outs[a].at[_chip(pos) if sender else _chip(peer)]
        return _rcopy(src, dst, sems[0].at[a, j], sems[1].at[a, j], peer)

    def start(ins, outs, sems):
        pos = _mesh_pos()

        @pl.when(pos[2] == layer)
        def _():
            for a in range(na):
                for j, m in enumerate(CHIP_MASKS):
                    copy(ins, outs, sems, pos, a, j, m, True).start()

    def finish(ins, outs, sems):
        pos = _mesh_pos()

        @pl.when(pos[2] == layer)
        def _():
            for a in range(na):
                for j, m in enumerate(CHIP_MASKS):
                    copy(ins, outs, sems, pos, a, j, m, False).wait_recv()
            for a in range(na):
                for j, m in enumerate(CHIP_MASKS):
                    copy(ins, outs, sems, pos, a, j, m, True).wait_send()

    return _Comm(arrs, [S(s.shape, s.dtype) for s in arrs], [dma((na, 3)), dma((na, 3))], start, finish)


def _swap_layer(arrs, layer, tag):
    na = len(arrs)

    def body(*refs):
        ins, outs = refs[:na], refs[na:2 * na]
        ssem, rsem = refs[2 * na:]
        pos = _mesh_pos()
        sib = _peer(pos, SIBLING)
        cps = [_rcopy(ins[a], outs[a], ssem.at[a], rsem.at[a], sib) for a in range(na)]

        @pl.when(pos[2] != layer)
        def _():
            for cp in cps:
                cp.start()
            for cp in cps:
                cp.wait_send()

        @pl.when(pos[2] == layer)
        def _():
            for cp in cps:
                cp.wait_recv()

    dma = pltpu.SemaphoreType.DMA
    return pl.pallas_call(
        body, in_specs=[ANY] * na, out_specs=[ANY] * na, out_shape=[S(s.shape, s.dtype) for s in arrs],
        scratch_shapes=[dma((na,)), dma((na,))], name=f"swap_layer{tag}")(*arrs)


def _share_sibling(arrs):
    na = len(arrs)

    def body(*refs):
        bufs = refs[na:2 * na]
        ssem, rsem = refs[2 * na:]
        pos = _mesh_pos()
        c = pos[2]
        sib = _peer(pos, SIBLING)
        sends = [_rcopy(bufs[a].at[c], bufs[a].at[c], ssem.at[a], rsem.at[a], sib) for a in range(na)]
        for cp in sends:
            cp.start()
        for a in range(na):
            _rcopy(bufs[a].at[c], bufs[a].at[1 - c], ssem.at[a], rsem.at[a], sib).wait_recv()
        for cp in sends:
            cp.wait_send()

    dma = pltpu.SemaphoreType.DMA
    return pl.pallas_call(
        body, in_specs=[ANY] * na, out_specs=[ANY] * na, out_shape=[S(s.shape, s.dtype) for s in arrs],
        input_output_aliases={a: a for a in range(na)},
        scratch_shapes=[dma((na,)), dma((na,))], name="share_sibling")(*arrs)


def _small_allreduce(vec):
    r = vec.shape[0]

    def body(v_ref, sum_ref, all_ref, ssem, rsem):
        pos = _mesh_pos()
        me = _lin(pos)
        all_ref[me] = v_ref[...]
        cps = [_rcopy(v_ref, all_ref.at[me], ssem.at[m - 1], rsem.at[m - 1], _peer(pos, m)) for m in range(1, N_DEV)]
        for cp in cps:
            cp.start()
        for m in range(1, N_DEV):
            src = _peer(pos, m)
            _rcopy(v_ref, all_ref.at[_lin(src)], ssem.at[m - 1], rsem.at[m - 1], src).wait_recv()
        acc = all_ref[0]
        for d in range(1, N_DEV):
            acc = acc + all_ref[d]
        sum_ref[...] = acc
        for cp in cps:
            cp.wait_send()

    dma = pltpu.SemaphoreType.DMA
    return pl.pallas_call(
        body, in_specs=[VMEM], out_specs=[VMEM, VMEM],
        out_shape=[S((r, 128), F32), S((N_DEV, r, 128), F32)],
        scratch_shapes=[dma((N_DEV - 1,)), dma((N_DEV - 1,))],
        compiler_params=pltpu.CompilerParams(vmem_limit_bytes=VMEM_LIMIT_V7X), name="small_allreduce")(vec)


ROW_BYTES_TARGET = 1 << 20


def _row_tile(rows, cols, itemsize=4, align=16, target=ROW_BYTES_TARGET):
    fits = [t for t in range(align, rows + 1, align) if rows % t == 0]
    under = [t for t in fits if t * cols * itemsize <= target]
    return max(under) if under else (min(fits) if fits else rows)


def _pair_add(g, other):
    r, cdim = g.shape
    tr = _row_tile(r, cdim, 1)

    def body(g_ref, o_ref, out_ref):
        out_ref[...] = (g_ref[...].astype(F32) + o_ref[...].astype(F32)).astype(out_ref.dtype)

    blk = pl.BlockSpec((tr, cdim), lambda i: (i, 0))
    return pl.pallas_call(body, grid=(r // tr,), in_specs=[blk, blk], out_specs=blk, out_shape=S((r, cdim), BF),
                          compiler_params=_cparams("parallel"), name="pair_add")(g, other)


def _sum4(parts, pairs, chip_core):
    _, r, cdim = parts[0].shape
    tr = _row_tile(r, cdim)

    def body(kc_ref, q0_ref, q1_ref, own0_ref, own1_ref, out_ref):
        first = kc_ref[1] == 0
        mine = jnp.where(first, own0_ref[...], own1_ref[...]).astype(F32)
        terms = [jnp.where(kc_ref[0] == j, mine, jnp.where(first, q0_ref[j], q1_ref[j]).astype(F32))
                 for j in range(N_CHIPS)]
        out_ref[...] = ((terms[0] + terms[1]) + terms[2]) + terms[3]

    allc = pl.BlockSpec((N_CHIPS, tr, cdim), lambda i, kc: (0, i, 0))
    own = pl.BlockSpec((None, tr, cdim), lambda i, kc: (kc[0], i, 0))
    return pl.pallas_call(
        body,
        grid_spec=pltpu.PrefetchScalarGridSpec(
            num_scalar_prefetch=1, grid=(r // tr,), in_specs=[allc, allc, own, own],
            out_specs=pl.BlockSpec((None, tr, cdim), lambda i, kc: (kc[1], i, 0))),
        out_shape=S((DEPTH, r, cdim), F32), compiler_params=_cparams("parallel"), name="sum4")(
            chip_core, parts[0], parts[1], pairs[0], pairs[1])


def _ada_w_grad(silu_c, dmod_cols, chip):
    def body(k_ref, sc_ref, dm_ref, o_ref):
        del k_ref
        o_ref[...] = lax.dot_general(sc_ref[...], dm_ref[...], TN, precision=HIGHEST, preferred_element_type=F32)

    return pl.pallas_call(
        body,
        grid_spec=pltpu.PrefetchScalarGridSpec(
            num_scalar_prefetch=1, grid=(4,),
            in_specs=[pl.BlockSpec((N_DEV, D), lambda j, k: (0, 0)),
                      pl.BlockSpec((None, N_DEV, ADA_COLS), lambda j, k: (4 * j + k[0], 0, 0))],
            out_specs=pl.BlockSpec((None, D, ADA_COLS), lambda j, k: (j, 0, 0))),
        out_shape=S((4, D, ADA_COLS), F32), compiler_params=_cparams("parallel"), name="ada_w_grad")(chip, silu_c, dmod_cols)


def _adamw(w, g, m, v, *, name):
    r = w.shape[0]
    rest = w.shape[1:]
    row_elems = 1
    for d in rest:
        row_elems *= d
    tr = _row_tile(r, row_elems, align=8 if len(rest) == 1 else 1, target=5 * ROW_BYTES_TARGET // 2)
    c1 = 1.0 / (1.0 - ADAM_B1 ** ADAM_STEP)
    c2 = 1.0 / (1.0 - ADAM_B2 ** ADAM_STEP)

    def body(w_ref, g_ref, m_ref, v_ref, d_ref, mo_ref, vo_ref):
        gv = g_ref[...]
        mn = ADAM_B1 * m_ref[...] + (1.0 - ADAM_B1) * gv
        vn = ADAM_B2 * v_ref[...] + (1.0 - ADAM_B2) * (gv * gv)
        mo_ref[...] = mn
        vo_ref[...] = vn
        d_ref[...] = -ADAM_LR * ((mn * c1) / (jnp.sqrt(vn * c2) + ADAM_EPS) + ADAM_WD * w_ref[...])

    zeros = (0,) * len(rest)
    blk = pl.BlockSpec((tr,) + rest, lambda i: (i,) + zeros)
    return pl.pallas_call(
        body, grid=(r // tr,), in_specs=[blk] * 4, out_specs=[blk] * 3, out_shape=[S(w.shape, F32)] * 3,
        compiler_params=_cparams("parallel"), name=name)(w, g, m, v)


def _adamw_many(ws, gs, ms, vs):
    n = len(ws)
    c1 = 1.0 / (1.0 - ADAM_B1 ** ADAM_STEP)
    c2 = 1.0 / (1.0 - ADAM_B2 ** ADAM_STEP)

    def body(*refs):
        for i in range(n):
            w_ref, g_ref, m_ref, v_ref, d_ref, mo_ref, vo_ref = (refs[k * n + i] for k in range(7))
            gv = g_ref[...]
            mn = ADAM_B1 * m_ref[...] + (1.0 - ADAM_B1) * gv
            vn = ADAM_B2 * v_ref[...] + (1.0 - ADAM_B2) * (gv * gv)
            mo_ref[...] = mn
            vo_ref[...] = vn
            d_ref[...] = -ADAM_LR * ((mn * c1) / (jnp.sqrt(vn * c2) + ADAM_EPS) + ADAM_WD * w_ref[...])

    shapes = [S(a.shape, F32) for a in ws]
    outs = pl.pallas_call(
        body, in_specs=[VMEM] * (4 * n), out_specs=[VMEM] * (3 * n), out_shape=shapes * 3,
        compiler_params=pltpu.CompilerParams(vmem_limit_bytes=VMEM_LIMIT_V7X), name="adamw_small")(*ws, *gs, *ms, *vs)
    return outs[0:n], outs[n:2 * n], outs[2 * n:3 * n]


WEIGHTS = ['ada_mix_w', 'ada_mix_b', 'norm_mix_g', 'w_in', 'b_gate', 'ssd_conv_w', 'ssd_conv_b', 'ssd_dt_bias',
           'ssd_a_log', 'ssd_d', 'ssd_norm_g', 'w_ssd_out', 'conf_conv_w', 'conf_conv_b', 'conf_ln_g', 'conf_ln_b',
           'w_conf_out', 'sc_conv_w', 'w_sc_out', 'w_o', 'ada_ffn_w', 'ada_ffn_b', 'norm_ffn_g', 'w_up', 'ffn_conv_w',
           'ffn_conv_b', 'w_down', 'final_norm_g']
SMALL = ['ada_mix_b', 'norm_mix_g', 'b_gate', 'ssd_conv_b', 'ssd_dt_bias', 'ssd_a_log', 'ssd_d', 'ssd_norm_g', 'conf_conv_b',
         'conf_ln_g', 'conf_ln_b', 'ada_ffn_b', 'norm_ffn_g', 'ffn_conv_b']
CONVS = ['ssd_conv_w', 'conf_conv_w', 'sc_conv_w', 'ffn_conv_w']
BIG = ['ada_mix_w', 'ada_ffn_w', 'w_in', 'w_up', 'w_conf_out', 'w_sc_out', 'w_ssd_out', 'w_o', 'w_down']


def _pack_rows(pieces):
    flat = [p.reshape(-1) for p in pieces]
    offs, o = [], 0
    for f in flat:
        offs.append(o)
        o += f.shape[0]
    total = -(-o // 1024) * 1024
    vec = jnp.concatenate(flat + [jnp.zeros((total - o,), F32)])
    return vec.reshape(total // 128, 128), offs


def _by_chip(a, axis):
    shp = a.shape
    a = a.reshape(shp[:axis] + (N_CHIPS, shp[axis] // N_CHIPS) + shp[axis + 1:])
    return jnp.moveaxis(a, axis, 0)


def _from_chips(a, axis):
    a = jnp.moveaxis(a, 0, axis)
    shp = a.shape
    return a.reshape(shp[:axis] + (shp[axis] * shp[axis + 1],) + shp[axis + 2:])


def kernel(x, c, ada_mix_w, ada_mix_b, norm_mix_g, w_in, b_gate, ssd_conv_w, ssd_conv_b, ssd_dt_bias, ssd_a_log, ssd_d, ssd_norm_g, w_ssd_out, conf_conv_w, conf_conv_b, conf_ln_g, conf_ln_b, w_conf_out, sc_conv_w, w_sc_out, w_o, ada_ffn_w, ada_ffn_b, norm_ffn_g, w_up, ffn_conv_w, ffn_conv_b, w_down, final_norm_g, loss_target, m_ada_mix_w, m_ada_mix_b, m_norm_mix_g, m_w_in, m_b_gate, m_ssd_conv_w, m_ssd_conv_b, m_ssd_dt_bias, m_ssd_a_log, m_ssd_d, m_ssd_norm_g, m_w_ssd_out, m_conf_conv_w, m_conf_conv_b, m_conf_ln_g, m_conf_ln_b, m_w_conf_out, m_sc_conv_w, m_w_sc_out, m_w_o, m_ada_ffn_w, m_ada_ffn_b, m_norm_ffn_g, m_w_up, m_ffn_conv_w, m_ffn_conv_b, m_w_down, m_final_norm_g, v_ada_mix_w, v_ada_mix_b, v_norm_mix_g, v_w_in, v_b_gate, v_ssd_conv_w, v_ssd_conv_b, v_ssd_dt_bias, v_ssd_a_log, v_ssd_d, v_ssd_norm_g, v_w_ssd_out, v_conf_conv_w, v_conf_conv_b, v_conf_ln_g, v_conf_ln_b, v_w_conf_out, v_sc_conv_w, v_w_sc_out, v_w_o, v_ada_ffn_w, v_ada_ffn_b, v_norm_ffn_g, v_w_up, v_ffn_conv_w, v_ffn_conv_b, v_w_down, v_final_norm_g):
    args = locals()
    w = {n: args[n] for n in WEIGHTS}
    mom = {n: args["m_" + n] for n in WEIGHTS}
    var = {n: args["v_" + n] for n in WEIGHTS}
    pos = _mesh_pos()
    chip = _chip(pos)
    core = pos[2]

    conv_pack = jnp.zeros((DEPTH, CONV_ROWS, CONV_COLS), F32)
    for n, (r0, taps, width) in CONV_PACK.items():
        conv_pack = conv_pack.at[:, r0:r0 + taps, 0:width].set(w[n])
    c_blk = jnp.pad(c, ((0, 7), (0, 0)))
    mods_raw, silu_c, conv_all = _ada_exchange(c_blk, ada_mix_w, ada_ffn_w, conv_pack)
    ada_b = jnp.concatenate([ada_mix_b, ada_ffn_b], axis=0)
    mod_all = mods_raw.reshape(N_CHIPS, 4, ADA_COLS).transpose(1, 0, 2).reshape(4, 3 * D) + ada_b
    mod_all = mod_all.reshape(4, 3, D)
    mods = [(mod_all[i], mod_all[2 + i]) for i in range(DEPTH)]
    conv_full = {n: _from_chips(conv_all[:, :, r0:r0 + taps, 0:width], 2) for n, (r0, taps, width) in CONV_PACK.items()}

    cast = lambda a: a.astype(BF)
    shards = [cast(w_in), jnp.concatenate([cast(w_conf_out), cast(w_sc_out)], axis=1),
              jnp.concatenate([cast(w_ssd_out), cast(w_o)], axis=1), cast(w_up), cast(w_down)]
    FIRST, MID, LATE = (0,), (1, 2), (3, 4)
    EARLY = FIRST + MID
    pick = lambda arrs, idx: [arrs[k] for k in idx]

    def own_block(l, gathered, idx):
        return [lax.dynamic_update_slice(g, shards[k][l][None], (chip, 0, 0)) for g, k in zip(gathered, idx)]

    def first_params(l, gathered):
        (g_in,) = own_block(l, gathered, FIRST)
        p = {n: w[n][l] for n in SMALL if not n.startswith("ada_")}
        p.update({n: conv_full[n][l] for n in CONVS})
        p["w_in_pad"] = _pad_w_in_chips(g_in)
        return p

    def mid_params(l, gathered):
        g_cs, g_row = own_block(l, gathered, MID)
        return {"w_conf_out": _from_chips(g_cs[:, 0:CONF_W], 1), "w_sc_out": _from_chips(g_cs[:, CONF_W:], 1),
                "w_ssd_out": _from_chips(g_row[:, 0:INNER // N_CHIPS], 0), "w_o": _from_chips(g_row[:, INNER // N_CHIPS:], 0)}

    def late_params(l, gathered):
        g_up, g_down = own_block(l, gathered, LATE)
        return {"w_up4": g_up, "w_down": _from_chips(g_down, 0)}

    def big_grads(g, idx):
        makers = (lambda: _unpad_w_in_chips(g["w_in_pad"]),
                  lambda: _by_chip(jnp.concatenate([g["w_conf_out"], g["w_sc_out"]], axis=0), 1),
                  lambda: jnp.concatenate([_by_chip(g["w_ssd_out"], 0), _by_chip(g["w_o"], 0)], axis=1),
                  lambda: g["w_up4"], lambda: _by_chip(g["w_down"], 0))
        return [makers[k]().astype(BF) for k in idx]

    def pair_sums(big, l, tag):
        theirs = _swap_layer(big, l, tag)
        out = []
        for g4, t4 in zip(big, theirs):
            k, r, cdim = g4.shape
            out.append(_pair_add(g4.reshape(k * r, cdim), t4.reshape(k * r, cdim)).reshape(k, r, cdim))
        return out

    seq = x.shape[1]
    xs = x.reshape(seq, D)
    p0 = first_params(0, _run_comm(_gather_comm(pick(shards, FIRST), 0), "gather_weights_l0"))
    xs, sv0, gathered1 = _layer_fwd(xs, p0, mods[0][0], mods[0][1], "_l0",
                                    comm_proj=_gather_comm(pick(shards, MID), 0),
                                    mid_params=lambda got: mid_params(0, got),
                                    comm_ssd=_gather_comm(pick(shards, LATE), 0),
                                    late_params=lambda got: late_params(0, got),
                                    comm_up=_gather_comm(pick(shards, EARLY), 1),
                                    comm=_gather_comm(pick(shards, LATE), 1))
    p1 = first_params(1, gathered1[0:1])
    p1.update(mid_params(1, gathered1[1:3]))
    p1.update(late_params(1, gathered1[3:5]))
    xs, sv1, _ = _layer_fwd(xs, p1, mods[1][0], mods[1][1], "_l1")
    dx, loss, dfg = _final_loss(xs, _row(final_norm_g), loss_target.reshape(seq, D), name="final_loss")
    dfinal = dfg[0]

    every = EARLY + LATE
    grads, dmods = [None, None], [None, None]
    dx, grads[1], dmm, dmf, _ = _layer_bwd(dx, p1, sv1, mods[1][0], mods[1][1], "_l1")
    dmods[1] = (dmm, dmf)
    pair1 = pair_sums(big_grads(grads[1], every), 1, "_l1")
    late0 = {}

    def scatter_late0(g):
        late0["pair"] = pair_sums(big_grads(g, MID + LATE), 0, "_l0_ffn")
        return _scatter_comm(late0["pair"], 0)

    dx, grads[0], dmm, dmf, (parts1, parts0_late) = _layer_bwd(
        dx, p0, sv0, mods[0][0], mods[0][1], "_l0", comm=_scatter_comm(pair1, 1), make_comm_conf=scatter_late0)
    dmods[0] = (dmm, dmf)
    pair0_early = pair_sums(big_grads(grads[0], FIRST), 0, "_l0")
    parts0_early = _run_comm(_scatter_comm(pair0_early, 0), "scatter_chips_l0")
    pair0 = list(pair0_early) + list(late0["pair"])
    parts0 = list(parts0_early) + list(parts0_late)
    chip_core = jnp.stack([chip, core]).astype(jnp.int32)
    red = _share_sibling([_sum4((q0, q1), (o0, o1), chip_core)
                          for q0, q1, o0, o1 in zip(parts0, parts1, pair0, pair1)])
    reduced = {"w_in": red[0], "cs": red[1], "row": red[2], "w_up": red[3], "w_down": red[4]}
    gw = {"w_in": reduced["w_in"], "w_up": reduced["w_up"], "w_down": reduced["w_down"],
          "w_conf_out": reduced["cs"][:, 0:CONF_W], "w_sc_out": reduced["cs"][:, CONF_W:],
          "w_ssd_out": reduced["row"][:, 0:INNER // N_CHIPS], "w_o": reduced["row"][:, INNER // N_CHIPS:]}

    dmod = jnp.stack([dmods[0][0], dmods[1][0], dmods[0][1], dmods[1][1]])
    small_local = {n: jnp.stack([grads[l][n] for l in range(DEPTH)]) for n in SMALL if not n.startswith("ada_")}
    pieces = [loss[0]] + [small_local[n] for n in SMALL if not n.startswith("ada_")]
    pieces += [jnp.stack([grads[l][n] for l in range(DEPTH)]) for n in CONVS] + [dfinal, dmod]
    vec, offs = _pack_rows(pieces)
    vsum, vall = _small_allreduce(vec)
    flat = vsum.reshape(-1)

    def piece(k, like):
        return flat[offs[k]:offs[k] + like.size].reshape(like.shape)

    loss_out = flat[0]
    k = 1
    for n in SMALL:
        if not n.startswith("ada_"):
            gw[n] = piece(k, small_local[n])
            k += 1
    for n in CONVS:
        full = piece(k, conv_full[n])
        width = CONV_PACK[n][2]
        gw[n] = lax.dynamic_slice_in_dim(full, chip * width, width, axis=2)
        k += 1
    gw["final_norm_g"] = piece(k, dfinal)
    k += 1
    dmod_sum = piece(k, dmod)
    gw["ada_mix_b"], gw["ada_ffn_b"] = dmod_sum[0:2], dmod_sum[2:4]
    dmod_all = vall.reshape(N_DEV, -1)[:, offs[k]:offs[k] + dmod.size]
    dmod_cols = dmod_all.reshape(N_DEV, 4 * N_CHIPS, ADA_COLS).transpose(1, 0, 2)
    ada_g = _ada_w_grad(silu_c, dmod_cols, jnp.reshape(chip, (1,)).astype(jnp.int32))
    gw["ada_mix_w"], gw["ada_ffn_w"] = ada_g[0:2], ada_g[2:4]

    delta, new_m, new_v = {}, {}, {}
    for n in BIG:
        shp = w[n].shape
        if shp[2] % 128:
            two_d = lambda a: jnp.transpose(a, (2, 0, 1))
            back = lambda a: jnp.transpose(a, (1, 2, 0))
        else:
            two_d = lambda a: a.reshape(shp[0] * shp[1], shp[2])
            back = lambda a: a.reshape(shp)
        d_, m_, v_ = _adamw(two_d(w[n]), two_d(gw[n]), two_d(mom[n]), two_d(var[n]), name=f"adamw_{n}")
        delta[n], new_m[n], new_v[n] = back(d_), back(m_), back(v_)
    rest = [n for n in WEIGHTS if n not in BIG]
    two_d = lambda a: a.reshape(1, -1) if a.ndim == 1 else a
    outs = _adamw_many(*[[two_d(src[n]) for n in rest] for src in (w, gw, mom, var)])
    for dst, group in zip((delta, new_m, new_v), outs):
        for n, o in zip(rest, group):
            dst[n] = o.reshape(w[n].shape)

    return (loss_out, dx[None], *[gw[n] for n in WEIGHTS], *[delta[n] for n in WEIGHTS],
            *[new_m[n] for n in WEIGHTS], *[new_v[n] for n in WEIGHTS])
```

```python
import jax
import jax.numpy as jnp
from jax import lax
from jax.experimental import pallas as pl
from jax.experimental.pallas import tpu as pltpu

F32 = jnp.float32
BF = jnp.bfloat16
S = jax.ShapeDtypeStruct

D = 1024
HEADS = 16
HEAD_DIM = 64
INNER = HEADS * HEAD_DIM
GROUPS = 2
NSTATE = 64
Q = 128
SSD_K = 4
XBC = INNER + 2 * GROUPS * NSTATE
CONF_W = 512
CONF_K = 31
SC_W = 512
SC_K = 3
DFF = 2816
FFN_K = 3
EPS = 1e-6
DEPTH = 2
R_Z, R_XBC, R_DT, R_CONF, R_SC, R_GATES, N_IN = 0, 1024, 2304, 2320, 3344, 4880, 7952
P_GATES, P_SC, P_XBC, P_Z, P_CONF, PW = 0, 3072, 4608, 6144, 7168, 8192
XBC_PAD = 1536
DT_PAD = 128
P_DT = P_XBC + XBC
N_CHIPS = 4
N_DEV = 8

ADAM_LR, ADAM_B1, ADAM_B2, ADAM_EPS, ADAM_WD, ADAM_STEP = 0.001, 0.9, 0.999, 1e-08, 0.01, 10

VMEM_LIMIT_V7X = 56 * 1024 * 1024
HIGHEST = lax.Precision.HIGHEST


def _cparams(*sem):
    return pltpu.CompilerParams(dimension_semantics=sem, vmem_limit_bytes=VMEM_LIMIT_V7X)


def _full(shape):
    n = len(shape)
    return pl.BlockSpec(shape, lambda *_: (0,) * n)


def _rows(tm, w, cb=0):
    return pl.BlockSpec((tm, w), lambda i: (i, cb))


def _prev_rows(tm, halo, w, cb=0):
    r = tm // halo
    return pl.BlockSpec((halo, w), lambda i: (jnp.maximum(i * r - 1, 0), cb))


def _next_rows(tm, halo, w, nrows, cb=0):
    r = tm // halo
    last = nrows // halo - 1
    return pl.BlockSpec((halo, w), lambda i: (jnp.minimum((i + 1) * r, last), cb))


def _sigmoid(v):
    return 1.0 / (1.0 + jnp.exp(-v))


def _softplus(v):
    return jnp.maximum(v, 0.0) + jnp.log(1.0 + jnp.exp(-jnp.abs(v)))


def _colsum(v):
    return jnp.sum(v, axis=0, keepdims=True)


def _tile(n, want):
    t = min(n, want)
    assert n % t == 0, (n, want)
    return t


NN = (((1,), (0,)), ((), ()))
NT = (((1,), (1,)), ((), ()))
TN = (((0,), (0,)), ((), ()))


def _dot(a, b, dims=NN):
    return lax.dot_general(a, b, dims, preferred_element_type=F32)


def _mm(a, b, *, dims, grid, a_spec, b_spec, o_spec, out_shape, acc_shape, name, comm=None):
    nk = grid[2]

    def body(a_ref, b_ref, o_ref, acc_ref):
        k = pl.program_id(2)
        part = _dot(a_ref[...], b_ref[...], dims)
        if nk == 1:
            o_ref[...] = part.astype(o_ref.dtype)
        else:
            @pl.when(k == 0)
            def _():
                acc_ref[...] = part

            @pl.when(k > 0)
            def _():
                acc_ref[...] += part

            @pl.when(k == nk - 1)
            def _():
                o_ref[...] = acc_ref[...].astype(o_ref.dtype)

    scratch = [pltpu.VMEM(acc_shape if nk > 1 else (8, 128), F32)]
    if comm is not None:
        (out,), comm_out = _call_with_comm(body, tuple(grid), in_specs=[a_spec, b_spec], out_specs=[o_spec],
                                           out_shape=[out_shape], scratch_shapes=scratch, args=(a, b), comm=comm, name=name)
        return out, comm_out
    return pl.pallas_call(
        body, grid=grid, in_specs=[a_spec, b_spec], out_specs=o_spec, out_shape=out_shape, scratch_shapes=scratch,
        compiler_params=_cparams("parallel", "parallel", "arbitrary"), name=name)(a, b)


def _mm_nn(a, b, *, out_dtype, tm, tn, name, comm=None):
    m, k = a.shape
    n = b.shape[1]
    tm, tn = _tile(m, tm), _tile(n, tn)
    return _mm(a, b, dims=NN, grid=(m // tm, n // tn, 1),
               a_spec=pl.BlockSpec((tm, k), lambda i, j, kk: (i, 0)),
               b_spec=pl.BlockSpec((k, tn), lambda i, j, kk: (0, j)),
               o_spec=pl.BlockSpec((tm, tn), lambda i, j, kk: (i, j)),
               out_shape=S((m, n), out_dtype), acc_shape=(tm, tn), name=name, comm=comm)


def _mm_nt(a, b, *, out_dtype, tm, tk, name):
    m, kc = a.shape
    n = b.shape[0]
    tm, tk = _tile(m, tm), _tile(kc, tk)
    return _mm(a, b, dims=NT, grid=(m // tm, 1, kc // tk),
               a_spec=pl.BlockSpec((tm, tk), lambda i, j, kk: (i, kk)),
               b_spec=pl.BlockSpec((n, tk), lambda i, j, kk: (0, kk)),
               o_spec=pl.BlockSpec((tm, n), lambda i, j, kk: (i, 0)),
               out_shape=S((m, n), out_dtype), acc_shape=(tm, n), name=name)


def _mm_tn(a, b, *, tn, tk, name, out_dtype=BF, by_chip=False):
    kc, m = a.shape
    n = b.shape[1]
    tn, tk = _tile(n, tn), _tile(kc, tk)
    if by_chip:
        assert n == N_CHIPS * tn
        o_spec, out_shape = pl.BlockSpec((None, m, tn), lambda i, j, kk: (j, 0, 0)), S((N_CHIPS, m, tn), out_dtype)
    else:
        o_spec, out_shape = pl.BlockSpec((m, tn), lambda i, j, kk: (0, j)), S((m, n), out_dtype)
    return _mm(a, b, dims=TN, grid=(1, n // tn, kc // tk),
               a_spec=pl.BlockSpec((tk, m), lambda i, j, kk: (kk, 0)),
               b_spec=pl.BlockSpec((tk, tn), lambda i, j, kk: (kk, j)),
               o_spec=o_spec, out_shape=out_shape, acc_shape=(m, tn), name=name)


def _mm_nn_chips(a, b4, *, out_dtype, tm, name, comm=None):
    m, k = a.shape
    n4 = b4.shape[2]
    tm = _tile(m, tm)
    return _mm(a, b4, dims=NN, grid=(m // tm, N_CHIPS, 1),
               a_spec=pl.BlockSpec((tm, k), lambda i, j, kk: (i, 0)),
               b_spec=pl.BlockSpec((None, k, n4), lambda i, j, kk: (j, 0, 0)),
               o_spec=pl.BlockSpec((tm, n4), lambda i, j, kk: (i, j)),
               out_shape=S((m, N_CHIPS * n4), out_dtype), acc_shape=(tm, n4), name=name, comm=comm)


def _mm_nt_chips(a, b4, *, out_dtype, tm, name):
    m = a.shape[0]
    n, n4 = b4.shape[1], b4.shape[2]
    tm = _tile(m, tm)
    return _mm(a, b4, dims=NT, grid=(m // tm, 1, N_CHIPS),
               a_spec=pl.BlockSpec((tm, n4), lambda i, j, kk: (i, kk)),
               b_spec=pl.BlockSpec((None, n, n4), lambda i, j, kk: (kk, 0, 0)),
               o_spec=pl.BlockSpec((tm, n), lambda i, j, kk: (i, 0)),
               out_shape=S((m, n), out_dtype), acc_shape=(tm, n), name=name)


def _mm_resid(a, b, x, mod, *, tm, name):
    m, k = a.shape
    n = b.shape[1]
    tm = _tile(m, tm)

    def body(a_ref, b_ref, x_ref, mod_ref, o_ref, xn_ref):
        o = _dot(a_ref[...], b_ref[...])
        o_ref[...] = o.astype(o_ref.dtype)
        xn_ref[...] = x_ref[...] + mod_ref[2:3, :] * o

    return pl.pallas_call(
        body, grid=(m // tm,),
        in_specs=[_rows(tm, k), _full((k, n)), _rows(tm, n), _full((3, n))],
        out_specs=[_rows(tm, n), _rows(tm, n)],
        out_shape=[S((m, n), BF), S((m, n), F32)],
        compiler_params=_cparams("parallel"), name=name)(a, b, x, mod)


def _modnorm_fwd(x, gain, mod, *, name):
    n = x.shape[0]
    tm = _tile(n, 512)

    def body(x_ref, g_ref, mod_ref, h_ref):
        xv = x_ref[...]
        r = lax.rsqrt(jnp.mean(xv * xv, axis=-1, keepdims=True) + EPS)
        y = xv * r * g_ref[...]
        h_ref[...] = (y * (1.0 + mod_ref[1:2, :]) + mod_ref[0:1, :]).astype(h_ref.dtype)

    return pl.pallas_call(
        body, grid=(n // tm,), in_specs=[_rows(tm, D), _full((1, D)), _full((3, D))],
        out_specs=_rows(tm, D), out_shape=S((n, D), BF), compiler_params=_cparams("parallel"), name=name)(x, gain, mod)


def _modnorm_bwd(dh, x, dres, gain, mod, *, name):
    n = x.shape[0]
    tm = _tile(n, 512)

    def body(dh_ref, x_ref, dres_ref, g_ref, mod_ref, dx_ref, dg_ref, dsh_ref, dsc_ref):
        i = pl.program_id(0)
        xv = x_ref[...]
        r = lax.rsqrt(jnp.mean(xv * xv, axis=-1, keepdims=True) + EPS)
        xh = xv * r
        dhv = dh_ref[...].astype(F32)
        g = g_ref[...]
        dy = dhv * (1.0 + mod_ref[1:2, :])
        dxh = dy * g
        dx = r * (dxh - xh * jnp.mean(dxh * xh, axis=-1, keepdims=True))
        dx_ref[...] = dres_ref[...] + dx

        @pl.when(i == 0)
        def _():
            dg_ref[...] = jnp.zeros_like(dg_ref)
            dsh_ref[...] = jnp.zeros_like(dsh_ref)
            dsc_ref[...] = jnp.zeros_like(dsc_ref)

        dg_ref[...] += _colsum(dy * xh)
        dsh_ref[...] += _colsum(dhv)
        dsc_ref[...] += _colsum(dhv * xh * g)

    vec = S((1, D), F32)
    return pl.pallas_call(
        body, grid=(n // tm,),
        in_specs=[_rows(tm, D), _rows(tm, D), _rows(tm, D), _full((1, D)), _full((3, D))],
        out_specs=[_rows(tm, D), _full((1, D)), _full((1, D)), _full((1, D))],
        out_shape=[S((n, D), F32), vec, vec, vec],
        compiler_params=_cparams("arbitrary"), name=name)(dh, x, dres, gain, mod)


def _final_loss(x, gain, target, *, name):
    n = x.shape[0]
    tm = _tile(n, 512)

    def body(x_ref, g_ref, t_ref, dx_ref, loss_ref, dg_ref):
        i = pl.program_id(0)
        xv = x_ref[...]
        g = g_ref[...]
        r = lax.rsqrt(jnp.mean(xv * xv, axis=-1, keepdims=True) + EPS)
        xh = xv * r
        err = xh * g - t_ref[...]
        dy = err * (1.0 / D)
        dxh = dy * g
        dx_ref[...] = r * (dxh - xh * jnp.mean(dxh * xh, axis=-1, keepdims=True))

        @pl.when(i == 0)
        def _():
            loss_ref[...] = jnp.zeros_like(loss_ref)
            dg_ref[...] = jnp.zeros_like(dg_ref)

        part = _colsum(jnp.sum(err * err, axis=-1, keepdims=True)) * (0.5 / D)
        loss_ref[...] += jnp.broadcast_to(part, loss_ref.shape)
        dg_ref[...] += _colsum(dy * xh)

    return pl.pallas_call(
        body, grid=(n // tm,),
        in_specs=[_rows(tm, D), _full((1, D)), _rows(tm, D)],
        out_specs=[_rows(tm, D), _full((1, 128)), _full((1, D))],
        out_shape=[S((n, D), F32), S((1, 128), F32), S((1, D), F32)],
        compiler_params=_cparams("arbitrary"), name=name)(x, gain, target)


def _gate_bwd(dx, o, mod, *, name):
    n = dx.shape[0]
    tm = _tile(n, 512)

    def body(dx_ref, o_ref, mod_ref, do_ref, dgt_ref):
        i = pl.program_id(0)
        dxv = dx_ref[...]
        do_ref[...] = (dxv * mod_ref[2:3, :]).astype(do_ref.dtype)

        @pl.when(i == 0)
        def _():
            dgt_ref[...] = jnp.zeros_like(dgt_ref)

        dgt_ref[...] += _colsum(dxv * o_ref[...].astype(F32))

    return pl.pallas_call(
        body, grid=(n // tm,), in_specs=[_rows(tm, D), _rows(tm, D), _full((3, D))],
        out_specs=[_rows(tm, D), _full((1, D))], out_shape=[S((n, D), BF), S((1, D), F32)],
        compiler_params=_cparams("arbitrary"), name=name)(dx, o, mod)


def _conv(buf, w_ref, taps, start, rows, ch):
    acc = None
    for k in range(taps):
        term = buf[pl.ds(start - (taps - 1) + k, rows), 0:ch] * w_ref[k:k + 1, :]
        acc = term if acc is None else acc + term
    return acc


def _conv_t(buf, w_ref, taps, start, rows, ch):
    acc = None
    for k in range(taps):
        term = buf[pl.ds(start + (taps - 1) - k, rows), 0:ch] * w_ref[k:k + 1, :]
        acc = term if acc is None else acc + term
    return acc


def _conv_dw(dw_ref, dy, xbuf, taps, xstart, rows, ch):
    for k in range(taps):
        dw_ref[k:k + 1, :] += _colsum(dy * xbuf[pl.ds(xstart - (taps - 1) + k, rows), 0:ch])


HALO = 16
CONF_HALO = 32
CHUNK = 32
STRIP = 256


def _blocks8(v):
    return [v[8 * i:8 * (i + 1)] for i in range(v.shape[0] // 8)]


def _delay_rows(blocks, s):
    sub = lax.broadcasted_iota(jnp.int32, blocks[0].shape, 0)
    rolled = [pltpu.roll(b, s, 0) for b in blocks]
    return [jnp.where(sub < s, rolled[i - 1], rolled[i]) for i in range(1, len(blocks))]


def _advance_rows(blocks, s):
    sub = lax.broadcasted_iota(jnp.int32, blocks[0].shape, 0)
    rolled = [pltpu.roll(b, 8 - s, 0) for b in blocks]
    return [jnp.where(sub < 8 - s, rolled[i], rolled[i + 1]) for i in range(len(blocks) - 1)]


def _conv3_chunk(tail, xv, wk):
    blocks = [tail] + _blocks8(xv)
    x1 = jnp.concatenate(_delay_rows(blocks, 1), axis=0)
    x2 = jnp.concatenate(_delay_rows(blocks, 2), axis=0)
    return wk[0] * x2 + wk[1] * x1 + wk[2] * xv


def _ssd_pre_fwd(proj, w, b, *, name):
    n = proj.shape[0]
    tm = _tile(n, 512)
    cb = P_XBC // XBC_PAD

    def body(prev_ref, cur_ref, w_ref, b_ref, o_ref, c_ref, buf):
        i = pl.program_id(0)
        buf[0:HALO, :] = jnp.where(i == 0, 0.0, prev_ref[:, 0:XBC].astype(F32))
        buf[HALO:HALO + tm, :] = cur_ref[:, 0:XBC].astype(F32)
        c = _conv(buf, w_ref, SSD_K, HALO, tm, XBC) + b_ref[...]
        c_ref[...] = c.astype(c_ref.dtype)
        o_ref[...] = (c * _sigmoid(c)).astype(o_ref.dtype)

    return pl.pallas_call(
        body, grid=(n // tm,),
        in_specs=[_prev_rows(tm, HALO, XBC_PAD, cb), _rows(tm, XBC_PAD, cb), _full((SSD_K, XBC)), _full((1, XBC))],
        out_specs=[_rows(tm, XBC), _rows(tm, XBC)], out_shape=[S((n, XBC), BF), S((n, XBC), BF)],
        scratch_shapes=[pltpu.VMEM((HALO + tm, XBC), F32)],
        compiler_params=_cparams("parallel"), name=name)(proj, proj, w, b)


def _ssd_pre_bwd(proj, cpre, dact, ddt, dproj, w, *, name):
    n = proj.shape[0]
    tm = _tile(n, 512)
    nt = n // tm
    cb = P_XBC // XBC_PAD

    def body(x_ref, cc_ref, cn_ref, dc_ref, dn_ref, ddt_ref, w_ref, dproj_in, o_ref, dw_ref, db_ref, dbuf, acc):
        del dproj_in
        i = pl.program_id(0)
        last = i == nt - 1

        @pl.when(i == 0)
        def _():
            acc[...] = jnp.zeros_like(acc)

        def silu_bwd(cv, dav):
            sg = _sigmoid(cv)
            return dav * (sg * (1.0 + cv * (1.0 - sg)))

        for s in range(XBC // STRIP):
            c = pl.ds(s * STRIP, STRIP)
            wk = [w_ref[k:k + 1, c] for k in range(SSD_K)]

            def step1(j, carry):
                rows = pl.ds(pl.multiple_of(j * CHUNK, CHUNK), CHUNK)
                dbuf[rows, c] = silu_bwd(cc_ref[rows, c].astype(F32), dc_ref[rows, c].astype(F32))
                return carry

            lax.fori_loop(0, tm // CHUNK, step1, 0, unroll=2)
            dbuf[tm:tm + HALO, c] = silu_bwd(cn_ref[:, c].astype(F32), jnp.where(last, 0.0, dn_ref[:, c].astype(F32)))

            def step2(j, carry):
                r0 = pl.multiple_of(j * CHUNK, CHUNK)
                rows = pl.ds(r0, CHUNK)
                win = dbuf[pl.ds(r0, CHUNK + 8), c]
                blocks = _blocks8(win)
                xv = x_ref[rows, c].astype(F32)
                d0 = win[0:CHUNK]
                dx = wk[SSD_K - 1] * d0
                acc[SSD_K - 1, :, c] += d0 * xv
                acc[SSD_K, :, c] += d0
                for adv in range(1, SSD_K):
                    dk = jnp.concatenate(_advance_rows(blocks, adv), axis=0)
                    dx = dx + wk[SSD_K - 1 - adv] * dk
                    acc[SSD_K - 1 - adv, :, c] += dk * xv
                o_ref[rows, c] = dx.astype(o_ref.dtype)
                return carry

            lax.fori_loop(0, tm // CHUNK, step2, 0)

        o_ref[:, XBC:XBC + DT_PAD] = ddt_ref[...]
        o_ref[:, XBC + DT_PAD:XBC_PAD] = jnp.zeros((tm, XBC_PAD - XBC - DT_PAD), o_ref.dtype)

        @pl.when(last)
        def _():
            for k in range(SSD_K):
                dw_ref[k:k + 1, :] = _colsum(acc[k])
            db_ref[...] = _colsum(acc[SSD_K])

    return pl.pallas_call(
        body, grid=(nt,),
        in_specs=[_rows(tm, XBC_PAD, cb), _rows(tm, XBC), _next_rows(tm, HALO, XBC, n),
                  _rows(tm, XBC), _next_rows(tm, HALO, XBC, n), _rows(tm, DT_PAD),
                  _full((SSD_K, XBC)), pl.BlockSpec(memory_space=pl.ANY)],
        out_specs=[_rows(tm, XBC_PAD, cb), _full((SSD_K, XBC)), _full((1, XBC))],
        out_shape=[S(dproj.shape, dproj.dtype), S((SSD_K, XBC), F32), S((1, XBC), F32)],
        scratch_shapes=[pltpu.VMEM((tm + HALO, XBC), F32), pltpu.VMEM((SSD_K + 1, CHUNK, XBC), F32)],
        input_output_aliases={7: 0},
        compiler_params=_cparams("arbitrary"), name=name)(proj, cpre, cpre, dact, dact, ddt, w, dproj)


def _sc_fwd(proj, w, *, name):
    n = proj.shape[0]
    tm = _tile(n, 512)
    cb = P_SC // (3 * SC_W)

    def body(prev_ref, cur_ref, w_ref, o_ref, buf):
        i = pl.program_id(0)
        pv = prev_ref[...].astype(F32)
        cv = cur_ref[...].astype(F32)
        buf[0:HALO, :] = jnp.where(i == 0, 0.0, pv[:, SC_W:2 * SC_W] * pv[:, 2 * SC_W:])
        buf[HALO:HALO + tm, :] = cv[:, SC_W:2 * SC_W] * cv[:, 2 * SC_W:]
        q = _conv(buf, w_ref, SC_K, HALO, tm, SC_W)
        o_ref[...] = (cv[:, 0:SC_W] * q).astype(o_ref.dtype)

    return pl.pallas_call(
        body, grid=(n // tm,),
        in_specs=[_prev_rows(tm, HALO, 3 * SC_W, cb), _rows(tm, 3 * SC_W, cb), _full((SC_K, SC_W))],
        out_specs=_rows(tm, SC_W), out_shape=S((n, SC_W), BF),
        scratch_shapes=[pltpu.VMEM((HALO + tm, SC_W), F32)],
        compiler_params=_cparams("parallel"), name=name)(proj, proj, w)


def _sc_bwd(proj, da, dproj, w, *, name):
    n = proj.shape[0]
    tm = _tile(n, 512)
    nt = n // tm
    cb = P_SC // (3 * SC_W)

    def body(xp_ref, xc_ref, xn_ref, dc_ref, dn_ref, w_ref, dproj_in, o_ref, dw_ref, pbuf, dbuf):
        del dproj_in
        i = pl.program_id(0)
        pv = xp_ref[...].astype(F32)
        cv = xc_ref[...].astype(F32)
        nv = xn_ref[...].astype(F32)
        gb, gc, xv = cv[:, 0:SC_W], cv[:, SC_W:2 * SC_W], cv[:, 2 * SC_W:]
        pbuf[0:HALO, :] = jnp.where(i == 0, 0.0, pv[:, SC_W:2 * SC_W] * pv[:, 2 * SC_W:])
        pbuf[HALO:HALO + tm, :] = gc * xv
        q = _conv(pbuf, w_ref, SC_K, HALO, tm, SC_W)
        dav = dc_ref[...].astype(F32)
        dbuf[0:tm, :] = dav * gb
        dbuf[tm:tm + HALO, :] = jnp.where(i == nt - 1, 0.0, dn_ref[...].astype(F32) * nv[:, 0:SC_W])
        dp = _conv_t(dbuf, w_ref, SC_K, 0, tm, SC_W)
        o_ref[:, 0:SC_W] = (dav * q).astype(o_ref.dtype)
        o_ref[:, SC_W:2 * SC_W] = (dp * xv).astype(o_ref.dtype)
        o_ref[:, 2 * SC_W:] = (dp * gc).astype(o_ref.dtype)

        @pl.when(i == 0)
        def _():
            dw_ref[...] = jnp.zeros_like(dw_ref)

        _conv_dw(dw_ref, dbuf[0:tm, :], pbuf, SC_K, HALO, tm, SC_W)

    return pl.pallas_call(
        body, grid=(nt,),
        in_specs=[_prev_rows(tm, HALO, 3 * SC_W, cb), _rows(tm, 3 * SC_W, cb), _next_rows(tm, HALO, 3 * SC_W, n, cb),
                  _rows(tm, SC_W), _next_rows(tm, HALO, SC_W, n), _full((SC_K, SC_W)),
                  pl.BlockSpec(memory_space=pl.ANY)],
        out_specs=[_rows(tm, 3 * SC_W, cb), _full((SC_K, SC_W))],
        out_shape=[S(dproj.shape, dproj.dtype), S((SC_K, SC_W), F32)],
        scratch_shapes=[pltpu.VMEM((HALO + tm, SC_W), F32), pltpu.VMEM((tm + HALO, SC_W), F32)],
        input_output_aliases={6: 0},
        compiler_params=_cparams("arbitrary"), name=name)(proj, proj, proj, da, da, w, dproj)


CONF_ROWS = 32
CONF_PHASES = 8


def _fill_advanced(src, dst, nblk, ncol):
    for s in range(ncol // STRIP):
        c = pl.ds(s * STRIP, STRIP)
        sub = lax.broadcasted_iota(jnp.int32, (8, STRIP), 0)
        first = src[0:8, c]
        carry0 = tuple(pltpu.roll(first, 8 - b, 0) for b in range(1, CONF_PHASES))

        def step(i, prev):
            r0 = pl.multiple_of(i * 8, 8)
            blk = src[pl.ds(r0, 8), c]
            cur = []
            for b in range(1, CONF_PHASES):
                rolled = pltpu.roll(blk, 8 - b, 0)
                cur.append(rolled)
                dst[b - 1, pl.ds(r0 - 8, 8), c] = jnp.where(sub < 8 - b, prev[b - 1], rolled)
            return tuple(cur)

        lax.fori_loop(1, nblk, step, carry0)


def _conf_fwd(proj, w, b, ln_g, ln_b, *, name):
    n = proj.shape[0]
    tm = _tile(n, 512)
    cb = P_CONF // (2 * CONF_W)
    h = CONF_HALO

    def body(prev_ref, cur_ref, w_ref, b_ref, g_ref, be_ref, a_ref, uc_ref, buf):
        i = pl.program_id(0)
        pv = prev_ref[...].astype(F32)
        cv = cur_ref[...].astype(F32)
        buf[0:h, :] = jnp.where(i == 0, 0.0, pv[:, 0:CONF_W] * _sigmoid(pv[:, CONF_W:]))
        buf[h:h + tm, :] = cv[:, 0:CONF_W] * _sigmoid(cv[:, CONF_W:])
        uc = _conv(buf, w_ref, CONF_K, h, tm, CONF_W) + b_ref[...]
        uc_ref[...] = uc.astype(uc_ref.dtype)
        mu = jnp.mean(uc, axis=-1, keepdims=True)
        xc = uc - mu
        v = xc * lax.rsqrt(jnp.mean(xc * xc, axis=-1, keepdims=True) + EPS) * g_ref[...] + be_ref[...]
        a_ref[...] = (v * _sigmoid(v)).astype(a_ref.dtype)

    vec = _full((1, CONF_W))
    return pl.pallas_call(
        body, grid=(n // tm,),
        in_specs=[_prev_rows(tm, h, 2 * CONF_W, cb), _rows(tm, 2 * CONF_W, cb), _full((CONF_K, CONF_W)), vec, vec, vec],
        out_specs=[_rows(tm, CONF_W), _rows(tm, CONF_W)],
        out_shape=[S((n, CONF_W), BF), S((n, CONF_W), BF)],
        scratch_shapes=[pltpu.VMEM((h + tm, CONF_W), F32)],
        compiler_params=_cparams("parallel"), name=name)(proj, proj, w, b, ln_g, ln_b)


def _conf_bwd(proj, uc, da, dproj, w, ln_g, ln_b, *, name, comm=None):
    n = proj.shape[0]
    tm = _tile(n, 512)
    nt = n // tm
    cb = P_CONF // (2 * CONF_W)
    h = CONF_HALO

    def body(xp_ref, xc_ref, ucc_ref, ucn_ref, dac_ref, dan_ref, w_ref, g_ref, be_ref, dproj_in,
             o_ref, dw_ref, db_ref, dg_ref, dbe_ref, ubuf, dbuf, dsh, wb):
        del dproj_in
        i = pl.program_id(0)
        pv = xp_ref[...].astype(F32)
        cv = xc_ref[...].astype(F32)
        val, gt = cv[:, 0:CONF_W], cv[:, CONF_W:]
        sg = _sigmoid(gt)
        ubuf[0:h, :] = jnp.where(i == 0, 0.0, pv[:, 0:CONF_W] * _sigmoid(pv[:, CONF_W:]))
        ubuf[h:h + tm, :] = val * sg

        def ln_silu_bwd(ucv, dav):
            mu = jnp.mean(ucv, axis=-1, keepdims=True)
            xc = ucv - mu
            r = lax.rsqrt(jnp.mean(xc * xc, axis=-1, keepdims=True) + EPS)
            xh = xc * r
            v = xh * g_ref[...] + be_ref[...]
            s = _sigmoid(v)
            dv = dav * (s * (1.0 + v * (1.0 - s)))
            dxh = dv * g_ref[...]
            duc = r * (dxh - jnp.mean(dxh, axis=-1, keepdims=True) - xh * jnp.mean(dxh * xh, axis=-1, keepdims=True))
            return duc, dv, xh

        duc, dv, xh = ln_silu_bwd(ucc_ref[...].astype(F32), dac_ref[...].astype(F32))
        dbuf[0:tm, :] = duc
        ducn, _, _ = ln_silu_bwd(ucn_ref[...].astype(F32), dan_ref[...].astype(F32))
        dbuf[tm:tm + h, :] = jnp.where(i == nt - 1, 0.0, ducn)
        for k in range(CONF_K):
            wb[k] = jnp.broadcast_to(w_ref[k:k + 1, :], (8, CONF_W))
        _fill_advanced(dbuf, dsh, (tm + h) // 8, CONF_W)
        nb = CONF_ROWS // 8
        for s_ in range(CONF_W // STRIP):
            c = pl.ds(s_ * STRIP, STRIP)
            cg = pl.ds(CONF_W + s_ * STRIP, STRIP)

            def dconv(j, carry):
                r0 = pl.multiple_of(j * CONF_ROWS, CONF_ROWS)
                acc = [jnp.zeros((8, STRIP), F32) for _ in range(nb)]
                for t in range(CONF_K):
                    wv = wb[CONF_K - 1 - t, :, c]
                    for q in range(nb):
                        rows8 = pl.ds(r0 + 8 * q + 8 * (t // 8), 8)
                        dv = dbuf[rows8, c] if t % 8 == 0 else dsh[t % 8 - 1, rows8, c]
                        acc[q] = acc[q] + wv * dv
                du = jnp.concatenate(acc, axis=0)
                rows = pl.ds(r0, CONF_ROWS)
                vl = xc_ref[rows, c].astype(F32)
                sgv = _sigmoid(xc_ref[rows, cg].astype(F32))
                o_ref[rows, c] = (du * sgv).astype(o_ref.dtype)
                o_ref[rows, cg] = (du * vl * sgv * (1.0 - sgv)).astype(o_ref.dtype)
                return carry

            lax.fori_loop(0, tm // CONF_ROWS, dconv, 0)

        @pl.when(i == 0)
        def _():
            dw_ref[...] = jnp.zeros_like(dw_ref)
            db_ref[...] = jnp.zeros_like(db_ref)
            dg_ref[...] = jnp.zeros_like(dg_ref)
            dbe_ref[...] = jnp.zeros_like(dbe_ref)

        dg_ref[...] += _colsum(dv * xh)
        dbe_ref[...] += _colsum(dv)
        db_ref[...] += _colsum(duc)
        _conv_dw(dw_ref, duc, ubuf, CONF_K, h, tm, CONF_W)

    vec = _full((1, CONF_W))
    vshape = S((1, CONF_W), F32)
    return _call_with_comm(
        body, nt,
        in_specs=[_prev_rows(tm, h, 2 * CONF_W, cb), _rows(tm, 2 * CONF_W, cb),
                  _rows(tm, CONF_W), _next_rows(tm, h, CONF_W, n), _rows(tm, CONF_W), _next_rows(tm, h, CONF_W, n),
                  _full((CONF_K, CONF_W)), vec, vec, pl.BlockSpec(memory_space=pl.ANY)],
        out_specs=[_rows(tm, 2 * CONF_W, cb), _full((CONF_K, CONF_W)), vec, vec, vec],
        out_shape=[S(dproj.shape, dproj.dtype), S((CONF_K, CONF_W), F32), vshape, vshape, vshape],
        scratch_shapes=[pltpu.VMEM((h + tm, CONF_W), F32), pltpu.VMEM((tm + h, CONF_W), F32),
                        pltpu.VMEM((CONF_PHASES - 1, tm + h, CONF_W), F32), pltpu.VMEM((CONF_K, 8, CONF_W), F32)],
        args=(proj, proj, uc, uc, da, da, w, ln_g, ln_b, dproj), comm=comm, name=name, aliases={9: 0})


def _ffn_act_fwd(up, w, b, *, name, comm=None):
    n = up.shape[0]
    tm = _tile(n, 512)
    c2 = 2 * DFF

    def body(prev_ref, cur_ref, w_ref, b_ref, o_ref, u_ref):
        first = pl.program_id(0) == 0
        for s in range(DFF // STRIP):
            cols = (pl.ds(s * STRIP, STRIP), pl.ds(DFF + s * STRIP, STRIP))
            wk = [[w_ref[k:k + 1, c] for k in range(FFN_K)] for c in cols]
            bk = [b_ref[:, c] for c in cols]
            tails = tuple(jnp.where(first, 0.0, prev_ref[:, c].astype(F32)[HALO - 8:HALO]) for c in cols)

            def step(j, tails):
                r0 = pl.multiple_of(j * CHUNK, CHUNK)
                us, new_tails = [], []
                for h in range(2):
                    xv = cur_ref[pl.ds(r0, CHUNK), cols[h]].astype(F32)
                    us.append(_conv3_chunk(tails[h], xv, wk[h]) + bk[h])
                    u_ref[pl.ds(r0, CHUNK), cols[h]] = us[h].astype(u_ref.dtype)
                    new_tails.append(xv[CHUNK - 8:CHUNK])
                o_ref[pl.ds(r0, CHUNK), cols[0]] = (us[0] * _sigmoid(us[0]) * us[1]).astype(o_ref.dtype)
                return tuple(new_tails)

            lax.fori_loop(0, tm // CHUNK, step, tails, unroll=2)

    return _call_with_comm(
        body, n // tm,
        in_specs=[_prev_rows(tm, HALO, c2), _rows(tm, c2), _full((FFN_K, c2)), _full((1, c2))],
        out_specs=[_rows(tm, DFF), _rows(tm, c2)], out_shape=[S((n, DFF), BF), S((n, c2), BF)],
        scratch_shapes=[], args=(up, up, w, b), comm=comm, name=name)


def _ffn_act_bwd(up, u, dact, w, *, name, comm=None):
    n = up.shape[0]
    tm = _tile(n, 512)
    nt = n // tm
    c2 = 2 * DFF

    def body(x_ref, uc_ref, un_ref, dc_ref, dn_ref, w_ref, o_ref, dw_ref, db_ref, dbuf, acc):
        i = pl.program_id(0)
        last = i == nt - 1

        @pl.when(i == 0)
        def _():
            acc[...] = jnp.zeros_like(acc)

        def swiglu_bwd(gate, val, dav):
            sg = _sigmoid(gate)
            return dav * val * (sg * (1.0 + gate * (1.0 - sg))), dav * gate * sg

        for s in range(DFF // STRIP):
            cols = (pl.ds(s * STRIP, STRIP), pl.ds(DFF + s * STRIP, STRIP))
            wk = [[w_ref[k:k + 1, c] for k in range(FFN_K)] for c in cols]

            def step1(j, carry):
                r0 = pl.multiple_of(j * CHUNK, CHUNK)
                rows = pl.ds(r0, CHUNK)
                dus = swiglu_bwd(uc_ref[rows, cols[0]].astype(F32), uc_ref[rows, cols[1]].astype(F32),
                                 dc_ref[rows, cols[0]].astype(F32))
                for h in range(2):
                    dbuf[rows, cols[h]] = dus[h]
                return carry

            lax.fori_loop(0, tm // CHUNK, step1, 0, unroll=2)
            dus = swiglu_bwd(un_ref[:, cols[0]].astype(F32), un_ref[:, cols[1]].astype(F32),
                             jnp.where(last, 0.0, dn_ref[:, cols[0]].astype(F32)))
            for h in range(2):
                dbuf[tm:tm + HALO, cols[h]] = dus[h]

            def step2(j, carry):
                r0 = pl.multiple_of(j * CHUNK, CHUNK)
                rows = pl.ds(r0, CHUNK)
                for h in range(2):
                    win = dbuf[pl.ds(r0, CHUNK + 8), cols[h]]
                    blocks = _blocks8(win)
                    d0 = win[0:CHUNK]
                    d1 = jnp.concatenate(_advance_rows(blocks, 1), axis=0)
                    d2 = jnp.concatenate(_advance_rows(blocks, 2), axis=0)
                    o_ref[rows, cols[h]] = (wk[h][2] * d0 + wk[h][1] * d1 + wk[h][0] * d2).astype(o_ref.dtype)
                    xv = x_ref[rows, cols[h]].astype(F32)
                    acc[2, :, cols[h]] += d0 * xv
                    acc[1, :, cols[h]] += d1 * xv
                    acc[0, :, cols[h]] += d2 * xv
                    acc[FFN_K, :, cols[h]] += d0
                return carry

            lax.fori_loop(0, tm // CHUNK, step2, 0)

        @pl.when(last)
        def _():
            for k in range(FFN_K):
                dw_ref[k:k + 1, :] = _colsum(acc[k])
            db_ref[...] = _colsum(acc[FFN_K])

    return _call_with_comm(
        body, nt,
        in_specs=[_rows(tm, c2), _rows(tm, c2), _next_rows(tm, HALO, c2, n),
                  _rows(tm, DFF), _next_rows(tm, HALO, DFF, n), _full((FFN_K, c2))],
        out_specs=[_rows(tm, c2), _full((FFN_K, c2)), _full((1, c2))],
        out_shape=[S((n, c2), BF), S((FFN_K, c2), F32), S((1, c2), F32)],
        scratch_shapes=[pltpu.VMEM((tm + HALO, c2), F32), pltpu.VMEM((FFN_K + 1, CHUNK, c2), F32)],
        args=(up, u, u, dact, dact, w), comm=comm, name=name)


def _head_consts():
    lane = jnp.arange(INNER) // HEAD_DIM
    rep = (jnp.arange(128)[:, None] == lane[None, :]).astype(BF)
    return rep, rep.T


def _split_dot(v, m):
    hi = v.astype(BF)
    lo = (v - hi.astype(F32)).astype(BF)
    return _dot(hi, m) + _dot(lo, m)


def _split3(v):
    hi = v.astype(BF)
    r1 = v - hi.astype(F32)
    mid = r1.astype(BF)
    lo = (r1 - mid.astype(F32)).astype(BF)
    return hi, mid, lo


def _mask_dot(mask, v, dims, mask_first):
    m = mask.astype(BF)
    parts = [_dot(m, t, dims) if mask_first else _dot(t, m, dims) for t in _split3(v)]
    return (parts[0] + parts[1]) + parts[2]


def _chunk_decay_terms(dt_raw, dtb, alog, rep):
    row = lax.broadcasted_iota(jnp.int32, (Q, Q), 0)
    col = lax.broadcasted_iota(jnp.int32, (Q, Q), 1)
    lower = row >= col
    upper = col >= row
    dt = _softplus(dt_raw + dtb)
    a = -jnp.exp(alog)
    adt = dt * a
    acum = _mask_dot(lower, adt, NN, True)
    acum_t = _mask_dot(upper, adt, TN, False)
    alast = acum[Q - 1:Q, :]
    e = jnp.exp(acum)
    f = jnp.exp(alast - acum)
    ex = _split_dot(jnp.concatenate([dt, e, f, jnp.broadcast_to(jnp.exp(alast), (8, 128))], axis=0), rep)
    return dict(lower=lower, upper=upper, dt=dt, a=a, acum=acum, acum_t=acum_t, alast=alast,
                dt_x=ex[0:Q], e_x=ex[Q:2 * Q], f_x=ex[2 * Q:3 * Q], cd_x=ex[3 * Q:3 * Q + 1])


def _block_diag2(v, lo):
    return jnp.concatenate([jnp.where(lo, v, 0.0), jnp.where(lo, 0.0, v)], axis=0).astype(BF)


def _ssd_fwd(xbc_act, proj, dt_bias, a_log, d_x, norm_g, *, name, comm=None):
    n = xbc_act.shape[0]
    nc = n // Q
    rep, _ = _head_consts()

    def body(xs_ref, bc_ref, dt_ref, z_ref, dtb_ref, alog_ref, dx_ref, ng_ref, rep_ref, y_ref, yn_ref, hp_ref,
             h_scr, y_scr):
        i = pl.program_id(0)

        @pl.when(i == 0)
        def _():
            h_scr[...] = jnp.zeros_like(h_scr)

        hp_ref[...] = h_scr[...]
        t = _chunk_decay_terms(dt_ref[...].astype(F32), dtb_ref[...], alog_ref[...], rep_ref[...])
        xs = xs_ref[...].astype(F32)
        xt = xs * t["dt_x"]
        lo = lax.broadcasted_iota(jnp.int32, (Q, 128), 1) < HEAD_DIM
        gw = INNER // GROUPS
        for g in range(GROUPS):
            bm = bc_ref[:, g * NSTATE:(g + 1) * NSTATE]
            cm = bc_ref[:, GROUPS * NSTATE + g * NSTATE:GROUPS * NSTATE + (g + 1) * NSTATE]
            cb = _dot(cm, bm, NT)
            hg = h_scr[:, g * gw:(g + 1) * gw]
            yoff = _dot(cm, hg.astype(BF))
            for jj in range(gw // 128):
                p = g * (gw // 128) + jj
                sl = slice(p * 128, (p + 1) * 128)
                ws = []
                for hd in (2 * p, 2 * p + 1):
                    seg = t["acum"][:, hd:hd + 1] - t["acum_t"][hd:hd + 1, :]
                    ws.append((cb * jnp.exp(jnp.where(t["lower"], seg, -jnp.inf))).astype(BF))
                ydiag = _dot(jnp.concatenate(ws, axis=1), _block_diag2(xt[:, sl], lo))
                y_scr[:, sl] = ydiag + yoff[:, jj * 128:(jj + 1) * 128] * t["e_x"][:, sl] + dx_ref[:, sl] * xs[:, sl]
            xf = (xt[:, g * gw:(g + 1) * gw] * t["f_x"][:, g * gw:(g + 1) * gw]).astype(BF)
            h_scr[:, g * gw:(g + 1) * gw] = hg * t["cd_x"][:, g * gw:(g + 1) * gw] + _dot(bm, xf, TN)
        y = y_scr[...]
        y_ref[...] = y.astype(y_ref.dtype)
        z = z_ref[...].astype(F32)
        v = y * z * _sigmoid(z)
        for g in range(GROUPS):
            vg = v[:, g * gw:(g + 1) * gw]
            r = lax.rsqrt(jnp.mean(vg * vg, axis=-1, keepdims=True) + EPS)
            yn_ref[:, g * gw:(g + 1) * gw] = (vg * r * ng_ref[:, g * gw:(g + 1) * gw]).astype(yn_ref.dtype)

    vec = _full((1, INNER))
    hv = _full((1, 128))
    return _call_with_comm(
        body, nc,
        in_specs=[_rows(Q, INNER, 0), _rows(Q, 2 * GROUPS * NSTATE, INNER // (2 * GROUPS * NSTATE)),
                  _rows(Q, DT_PAD, P_DT // DT_PAD), _rows(Q, INNER, P_Z // INNER),
                  hv, hv, vec, vec, _full((128, INNER))],
        out_specs=[_rows(Q, INNER), _rows(Q, INNER), pl.BlockSpec((None, NSTATE, INNER), lambda i: (i, 0, 0))],
        out_shape=[S((n, INNER), BF), S((n, INNER), BF), S((nc, NSTATE, INNER), F32)],
        scratch_shapes=[pltpu.VMEM((NSTATE, INNER), F32), pltpu.VMEM((Q, INNER), F32)],
        args=(xbc_act, xbc_act, proj, proj, dt_bias, a_log, d_x, norm_g, rep), comm=comm, name=name)


def _ssd_bwd(xbc_act, proj, y, dyn, hprev, dproj, dt_bias, a_log, d_x, norm_g, *, name):
    n = xbc_act.shape[0]
    nc = n // Q
    rep, sel = _head_consts()
    gw = INNER // GROUPS

    def rev(w, cb=0):
        return pl.BlockSpec((Q, w), lambda i: (nc - 1 - i, cb))

    def body(xs_ref, bc_ref, dt_ref, z_ref, y_ref, dyn_ref, hp_ref, dtb_ref, alog_ref, dx_ref, ng_ref, rep_ref,
             sel_ref, dproj_in, dz_ref, ddt_ref, dxbc_ref, dng_ref, ddtb_ref, dalog_ref, dd_ref,
             dh_scr, dxt_scr, st_scr, off_scr, rs_scr, cs_scr, dng_acc, ddtb_acc, da_acc, dd_acc):
        del dproj_in
        i = pl.program_id(0)

        @pl.when(i == 0)
        def _():
            for r in (dh_scr, dng_acc, ddtb_acc, da_acc, dd_acc):
                r[...] = jnp.zeros_like(r)

        y = y_ref[...].astype(F32)
        z = z_ref[...].astype(F32)
        sz = _sigmoid(z)
        silu = z * sz
        v = y * silu
        dyn = dyn_ref[...].astype(F32)
        dvs = []
        for g in range(GROUPS):
            gs = slice(g * gw, (g + 1) * gw)
            vg = v[:, gs]
            r = lax.rsqrt(jnp.mean(vg * vg, axis=-1, keepdims=True) + EPS)
            vn = vg * r
            dvn = dyn[:, gs] * ng_ref[:, gs]
            dng_acc[:, gs] += _colsum(dyn[:, gs] * vn)
            dvs.append(r * (dvn - vn * jnp.mean(dvn * vn, axis=-1, keepdims=True)))
        dv = jnp.concatenate(dvs, axis=1)
        dy = dv * silu
        dz_ref[...] = (dv * y * (sz * (1.0 + z * (1.0 - sz)))).astype(dz_ref.dtype)

        dt_raw = dt_ref[...].astype(F32)
        t = _chunk_decay_terms(dt_raw, dtb_ref[...], alog_ref[...], rep_ref[...])
        xs = xs_ref[...].astype(F32)
        dsk = dx_ref[...]
        dd_acc[...] += _colsum(dy * xs)
        xt = xs * t["dt_x"]
        dye = dy * t["e_x"]
        xtf = xt * t["f_x"]
        hp = hp_ref[...]
        dh = dh_scr[...]
        lo = lax.broadcasted_iota(jnp.int32, (Q, 128), 1) < HEAD_DIM
        rs_scr[...] = jnp.zeros_like(rs_scr)
        cs_scr[...] = jnp.zeros_like(cs_scr)
        for g in range(GROUPS):
            gs = slice(g * gw, (g + 1) * gw)
            bm = bc_ref[:, g * NSTATE:(g + 1) * NSTATE]
            cm = bc_ref[:, GROUPS * NSTATE + g * NSTATE:GROUPS * NSTATE + (g + 1) * NSTATE]
            cbt = _dot(bm, cm, NT)
            dhg = dh[:, gs].astype(BF)
            hpg = hp[:, gs].astype(BF)
            dxt_state = _dot(bm, dhg) * t["f_x"][:, gs]
            st_scr[:, gs] = dxt_state
            dye_g = dye[:, gs]
            off_scr[:, gs] = dye_g * _dot(cm, hpg)
            dye_b = dye_g.astype(BF)
            db = _dot(xtf[:, gs].astype(BF), dhg, NT)
            dc = _dot(dye_b, hpg, NT)
            dh_scr[:, gs] = t["cd_x"][:, gs] * dh[:, gs] + _dot(cm, dye_b, TN)
            dcbt = jnp.zeros((Q, Q), F32)
            for jj in range(gw // 128):
                p = g * (gw // 128) + jj
                sl = slice(p * 128, (p + 1) * 128)
                lts, wfs = [], []
                for hd in (2 * p, 2 * p + 1):
                    seg_t = t["acum_t"][hd:hd + 1, :] - t["acum"][:, hd:hd + 1]
                    lt = jnp.exp(jnp.where(t["upper"], seg_t, -jnp.inf))
                    lts.append(lt)
                    wfs.append(cbt * lt)
                dyp = dy[:, sl]
                dxt_diag = _dot(jnp.concatenate([w.astype(BF) for w in wfs], axis=1), _block_diag2(dyp, lo))
                dwt2 = _dot(_block_diag2(xt[:, sl], lo), dyp.astype(BF), NT)
                for k, hd in enumerate((2 * p, 2 * p + 1)):
                    dwt = dwt2[k * Q:(k + 1) * Q]
                    dcbt = dcbt + dwt * lts[k]
                    mt = dwt * wfs[k]
                    rs_scr[hd:hd + 1, :] = _colsum(mt)
                    cs_scr[:, hd:hd + 1] = jnp.sum(mt, axis=1, keepdims=True)
                dxt_scr[:, sl] = dxt_diag + dxt_state[:, jj * 128:(jj + 1) * 128]
            dcbt_b = dcbt.astype(BF)
            db = db + _dot(dcbt_b, cm)
            dc = dc + _dot(dcbt_b, bm, TN)
            dxbc_ref[:, INNER + g * NSTATE:INNER + (g + 1) * NSTATE] = db.astype(dxbc_ref.dtype)
            dxbc_ref[:, INNER + (GROUPS + g) * NSTATE:INNER + (GROUPS + g + 1) * NSTATE] = dc.astype(dxbc_ref.dtype)
        dxt = dxt_scr[...]
        dst = st_scr[...]
        sel_m = sel_ref[...]
        sums = _split_dot(jnp.concatenate([off_scr[...], xs * dst, xs * dxt], axis=0), sel_m)
        r1_off, r3_state, r3 = sums[0:Q], sums[Q:2 * Q], sums[2 * Q:3 * Q]
        t1 = _colsum(xt * dst)
        t2 = _colsum(dh * hp)
        tails = _split_dot(jnp.concatenate([jnp.broadcast_to(t1, (8, INNER)), jnp.broadcast_to(t2, (8, INNER))], axis=0),
                           sel_m)
        extra = tails[0:1] + jnp.exp(t["alast"]) * tails[8:9]
        last_row = lax.broadcasted_iota(jnp.int32, (Q, 128), 0) == Q - 1
        da_cum = (rs_scr[...].T - cs_scr[...]) + r1_off - t["dt"] * r3_state + jnp.where(last_row, extra, 0.0)
        dadt = _mask_dot(t["upper"], da_cum, NN, True)
        ddt = r3 + t["a"] * dadt
        da_acc[...] += _colsum(dadt * t["dt"])
        real = lax.broadcasted_iota(jnp.int32, (Q, 128), 1) < HEADS
        ddraw = jnp.where(real, ddt * _sigmoid(dt_raw + dtb_ref[...]), 0.0)
        ddt_ref[...] = ddraw.astype(ddt_ref.dtype)
        ddtb_acc[...] += _colsum(ddraw)
        dxbc_ref[:, 0:INNER] = (dy * dsk + dxt * t["dt_x"]).astype(dxbc_ref.dtype)

        @pl.when(i == nc - 1)
        def _():
            dng_ref[...] = dng_acc[...]
            ddtb_ref[...] = ddtb_acc[...]
            dalog_ref[...] = da_acc[...] * t["a"]
            dd_ref[...] = _split_dot(jnp.broadcast_to(dd_acc[...], (8, INNER)), sel_m)[0:1]

    vec = _full((1, INNER))
    hv = _full((1, 128))
    return pl.pallas_call(
        body, grid=(nc,),
        in_specs=[rev(INNER, 0), rev(2 * GROUPS * NSTATE, INNER // (2 * GROUPS * NSTATE)),
                  rev(DT_PAD, P_DT // DT_PAD), rev(INNER, P_Z // INNER), rev(INNER), rev(INNER),
                  pl.BlockSpec((None, NSTATE, INNER), lambda i: (nc - 1 - i, 0, 0)),
                  hv, hv, vec, vec, _full((128, INNER)), _full((INNER, 128)), pl.BlockSpec(memory_space=pl.ANY)],
        out_specs=[rev(INNER, P_Z // INNER), rev(DT_PAD), rev(XBC), vec, hv, hv, hv],
        out_shape=[S(dproj.shape, dproj.dtype), S((n, DT_PAD), BF), S((n, XBC), BF),
                   S((1, INNER), F32), S((1, 128), F32), S((1, 128), F32), S((1, 128), F32)],
        scratch_shapes=[pltpu.VMEM((NSTATE, INNER), F32), pltpu.VMEM((Q, INNER), F32), pltpu.VMEM((Q, INNER), F32),
                        pltpu.VMEM((Q, INNER), F32), pltpu.VMEM((128, Q), F32), pltpu.VMEM((Q, 128), F32),
                        pltpu.VMEM((1, INNER), F32), pltpu.VMEM((1, 128), F32), pltpu.VMEM((1, 128), F32),
                        pltpu.VMEM((1, INNER), F32)],
        input_output_aliases={13: 0},
        compiler_params=_cparams("arbitrary"), name=name)(
            xbc_act, xbc_act, proj, proj, y, dyn, hprev, dt_bias, a_log, d_x, norm_g, rep, sel, dproj)


def _mixer_out_fwd(yn, a_conf, a_sc, proj, b_gate, w_ssd, w_conf, w_sc, w_o, x, mod, *, name):
    n = x.shape[0]
    tm = _tile(n, 256)

    def body(yn_ref, ac_ref, as_ref, gt_ref, bg_ref, wa_ref, wb_ref, wc_ref, wo_ref, x_ref, mod_ref,
             ya_ref, yb_ref, yc_ref, mg_ref, mix_ref, xn_ref):
        ya = _dot(yn_ref[...], wa_ref[...])
        yb = _dot(ac_ref[...], wb_ref[...])
        yc = _dot(as_ref[...], wc_ref[...])
        ya_ref[...] = ya.astype(ya_ref.dtype)
        yb_ref[...] = yb.astype(yb_ref.dtype)
        yc_ref[...] = yc.astype(yc_ref.dtype)
        g = _sigmoid(gt_ref[...].astype(F32) + bg_ref[...])
        merged = (g[:, 0:D] * ya + g[:, D:2 * D] * yb + g[:, 2 * D:] * yc).astype(mg_ref.dtype)
        mg_ref[...] = merged
        mix = _dot(merged, wo_ref[...])
        mix_ref[...] = mix.astype(mix_ref.dtype)
        xn_ref[...] = x_ref[...] + mod_ref[2:3, :] * mix

    act = S((n, D), BF)
    return pl.pallas_call(
        body, grid=(n // tm,),
        in_specs=[_rows(tm, INNER), _rows(tm, CONF_W), _rows(tm, SC_W), _rows(tm, 3 * D, P_GATES // (3 * D)),
                  _full((1, 3 * D)), _full((INNER, D)), _full((CONF_W, D)), _full((SC_W, D)), _full((D, D)),
                  _rows(tm, D), _full((3, D))],
        out_specs=[_rows(tm, D)] * 6,
        out_shape=[act, act, act, act, act, S((n, D), F32)],
        compiler_params=_cparams("parallel"), name=name)(yn, a_conf, a_sc, proj, b_gate, w_ssd, w_conf, w_sc, w_o, x, mod)


def _mixer_out_bwd(dx, mix, ya, yb, yc, proj, b_gate, w_ssd, w_conf, w_sc, w_o, mod, *, name):
    n = dx.shape[0]
    tm = _tile(n, 256)

    def body(dx_ref, mix_ref, ya_ref, yb_ref, yc_ref, gt_ref, bg_ref, wa_ref, wb_ref, wc_ref, wo_ref, mod_ref,
             do_ref, dya_ref, dyb_ref, dyc_ref, dgt_ref, dyn_ref, dac_ref, das_ref, dgm_ref, dbg_ref, dm_scr, acc):
        i = pl.program_id(0)
        last = i == pl.num_programs(0) - 1

        @pl.when(i == 0)
        def _():
            acc[...] = jnp.zeros_like(acc)

        do_ref[...] = (dx_ref[...] * mod_ref[2:3, :]).astype(BF)
        dm_scr[...] = _dot(do_ref[...], wo_ref[...], NT)
        branches = ((ya_ref, dya_ref), (yb_ref, dyb_ref), (yc_ref, dyc_ref))
        for s_ in range(D // STRIP):
            c = pl.ds(s_ * STRIP, STRIP)

            def step(j, carry):
                rows = pl.ds(pl.multiple_of(j * CHUNK, CHUNK), CHUNK)
                dmv = dm_scr[rows, c]
                acc[3, :, c] += dx_ref[rows, c] * mix_ref[rows, c].astype(F32)
                for jb, (y_ref, o_ref) in enumerate(branches):
                    cj = pl.ds(jb * D + s_ * STRIP, STRIP)
                    gj = _sigmoid(gt_ref[rows, cj].astype(F32) + bg_ref[:, cj])
                    o_ref[rows, c] = (dmv * gj).astype(o_ref.dtype)
                    dgpre = dmv * y_ref[rows, c].astype(F32) * gj * (1.0 - gj)
                    dgt_ref[rows, cj] = dgpre.astype(dgt_ref.dtype)
                    acc[jb, :, c] += dgpre
                return carry

            lax.fori_loop(0, tm // CHUNK, step, 0)
        dyn_ref[...] = _dot(dya_ref[...], wa_ref[...], NT).astype(dyn_ref.dtype)
        dac_ref[...] = _dot(dyb_ref[...], wb_ref[...], NT).astype(dac_ref.dtype)
        das_ref[...] = _dot(dyc_ref[...], wc_ref[...], NT).astype(das_ref.dtype)

        @pl.when(last)
        def _():
            for jb in range(3):
                dbg_ref[:, jb * D:(jb + 1) * D] = _colsum(acc[jb])
            dgm_ref[...] = _colsum(acc[3])

    act = S((n, D), BF)
    return pl.pallas_call(
        body, grid=(n // tm,),
        in_specs=[_rows(tm, D)] * 5 + [_rows(tm, 3 * D, P_GATES // (3 * D)), _full((1, 3 * D)), _full((INNER, D)),
                                       _full((CONF_W, D)), _full((SC_W, D)), _full((D, D)), _full((3, D))],
        out_specs=[_rows(tm, D)] * 4 + [_rows(tm, 3 * D, P_GATES // (3 * D)), _rows(tm, INNER), _rows(tm, CONF_W),
                                        _rows(tm, SC_W), _full((1, D)), _full((1, 3 * D))],
        out_shape=[act, act, act, act, S((n, PW), BF), S((n, INNER), BF), S((n, CONF_W), BF), S((n, SC_W), BF),
                   S((1, D), F32), S((1, 3 * D), F32)],
        scratch_shapes=[pltpu.VMEM((tm, D), F32), pltpu.VMEM((4, CHUNK, D), F32)],
        compiler_params=_cparams("arbitrary"), name=name)(dx, mix, ya, yb, yc, proj, b_gate, w_ssd, w_conf, w_sc, w_o, mod)


W_IN_SHARD = N_IN // N_CHIPS
W_IN_SEGMENTS = ((R_Z, R_XBC, P_Z), (R_XBC, R_DT, P_XBC), (R_DT, R_CONF, P_DT), (R_CONF, R_SC, P_CONF),
                 (R_SC, R_GATES, P_SC), (R_GATES, N_IN, P_GATES))


def _pad_w_in_chips(w4):
    parts = []
    for lo, hi, dst in sorted(W_IN_SEGMENTS, key=lambda sgm: sgm[2]):
        for k in range(N_CHIPS):
            a, b = max(lo, k * W_IN_SHARD), min(hi, (k + 1) * W_IN_SHARD)
            if a < b:
                parts.append((dst + a - lo, w4[k][:, a - k * W_IN_SHARD:b - k * W_IN_SHARD]))
    out, pos = [], 0
    for start, piece in parts:
        if start > pos:
            out.append(jnp.zeros((w4.shape[1], start - pos), w4.dtype))
        out.append(piece)
        pos = start + piece.shape[1]
    if pos < PW:
        out.append(jnp.zeros((w4.shape[1], PW - pos), w4.dtype))
    return jnp.concatenate(out, axis=1)


def _unpad_w_in_chips(wp):
    blocks = []
    for k in range(N_CHIPS):
        pieces = []
        for lo, hi, dst in W_IN_SEGMENTS:
            a, b = max(lo, k * W_IN_SHARD), min(hi, (k + 1) * W_IN_SHARD)
            if a < b:
                pieces.append(wp[:, dst + a - lo:dst + b - lo])
        blocks.append(jnp.concatenate(pieces, axis=1))
    return jnp.stack(blocks)


def _row(v):
    return v.reshape(1, -1)


def _head_row(v):
    return jnp.pad(v, (0, 128 - HEADS)).reshape(1, 128)


def _layer_fwd(x, p, mod_mix, mod_ffn, tag, comm=None, comm_ssd=None, late_params=None, comm_up=None, comm_proj=None,
               mid_params=None):
    sv = {"x0": x}
    h = _modnorm_fwd(x, _row(p["norm_mix_g"]), mod_mix, name=f"modnorm_mix_fwd{tag}")
    proj = _mm_nn(h, p["w_in_pad"], out_dtype=BF, tm=1024, tn=2048, name=f"proj_fwd{tag}", comm=comm_proj)
    if comm_proj is not None:
        proj, proj_comm_out = proj
        p.update(mid_params(proj_comm_out))
    xbc_act, cpre = _ssd_pre_fwd(proj, p["ssd_conv_w"], _row(p["ssd_conv_b"]), name=f"ssd_pre_fwd{tag}")
    d_x = _row(jnp.repeat(p["ssd_d"], HEAD_DIM))
    (y, yn, hprev), ssd_comm_out = _ssd_fwd(xbc_act, proj, _head_row(p["ssd_dt_bias"]), _head_row(p["ssd_a_log"]), d_x,
                                            _row(p["ssd_norm_g"]), name=f"ssd_fwd{tag}", comm=comm_ssd)
    if late_params is not None:
        p.update(late_params(ssd_comm_out))
    a_conf, uc = _conf_fwd(proj, p["conf_conv_w"], _row(p["conf_conv_b"]), _row(p["conf_ln_g"]), _row(p["conf_ln_b"]),
                           name=f"conf_fwd{tag}")
    a_sc = _sc_fwd(proj, p["sc_conv_w"], name=f"sc_fwd{tag}")
    ya, yb, yc, merged, mix, x1 = _mixer_out_fwd(yn, a_conf, a_sc, proj, _row(p["b_gate"]), p["w_ssd_out"],
                                                 p["w_conf_out"], p["w_sc_out"], p["w_o"], x, mod_mix,
                                                 name=f"mixer_out_fwd{tag}")
    h2 = _modnorm_fwd(x1, _row(p["norm_ffn_g"]), mod_ffn, name=f"modnorm_ffn_fwd{tag}")
    up = _mm_nn_chips(h2, p["w_up4"], out_dtype=BF, tm=1024, name=f"up_fwd{tag}", comm=comm_up)
    up_comm_out = ()
    if comm_up is not None:
        up, up_comm_out = up
    (act, u_ffn), comm_out = _ffn_act_fwd(up, p["ffn_conv_w"], _row(p["ffn_conv_b"]), name=f"ffn_act_fwd{tag}", comm=comm)
    o, x2 = _mm_resid(act, p["w_down"], x1, mod_ffn, tm=512, name=f"down_fwd{tag}")
    sv.update(h=h, proj=proj, xbc_act=xbc_act, cpre=cpre, d_x=d_x, y=y, yn=yn, hprev=hprev, a_conf=a_conf, uc=uc, a_sc=a_sc,
              ya=ya, yb=yb, yc=yc, merged=merged, mix=mix, x1=x1, h2=h2, up=up, u_ffn=u_ffn, act=act, o=o)
    return x2, sv, tuple(up_comm_out) + tuple(comm_out)


def _layer_bwd(dx, p, sv, mod_mix, mod_ffn, tag, comm=None, make_comm_conf=None):
    g = {}
    do2, dgate_ffn = _gate_bwd(dx, sv["o"], mod_ffn, name=f"gate_ffn_bwd{tag}")
    dact = _mm_nt(do2, p["w_down"], out_dtype=BF, tm=1024, tk=D, name=f"down_dx{tag}")
    g["w_down"] = _mm_tn(sv["act"], do2, tn=D, tk=1024, name=f"down_dw{tag}")
    (dup, g["ffn_conv_w"], dffn_b), comm_out = _ffn_act_bwd(sv["up"], sv["u_ffn"], dact, p["ffn_conv_w"],
                                                            name=f"ffn_act_bwd{tag}", comm=comm)
    g["ffn_conv_b"] = dffn_b[0]
    dh2 = _mm_nt_chips(dup, p["w_up4"], out_dtype=BF, tm=1024, name=f"up_dx{tag}")
    g["w_up4"] = _mm_tn(sv["h2"], dup, tn=2 * DFF // N_CHIPS, tk=2048, by_chip=True, name=f"up_dw{tag}")
    dx1, dgn, dsh, dsc = _modnorm_bwd(dh2, sv["x1"], dx, _row(p["norm_ffn_g"]), mod_ffn, name=f"modnorm_ffn_bwd{tag}")
    g["norm_ffn_g"] = dgn[0]
    dmod_ffn = jnp.concatenate([dsh[0], dsc[0], dgate_ffn[0]])

    (do1, dya, dyb, dyc, dproj, dyn, dac, dasc, dgate_mix, dbg) = _mixer_out_bwd(
        dx1, sv["mix"], sv["ya"], sv["yb"], sv["yc"], sv["proj"], _row(p["b_gate"]), p["w_ssd_out"], p["w_conf_out"],
        p["w_sc_out"], p["w_o"], mod_mix, name=f"mixer_out_bwd{tag}")
    g["b_gate"] = dbg[0]
    g["w_o"] = _mm_tn(sv["merged"], do1, tn=D, tk=2048, name=f"wo_dw{tag}")
    g["w_ssd_out"] = _mm_tn(sv["yn"], dya, tn=D, tk=2048, name=f"wssd_dw{tag}")
    g["w_conf_out"] = _mm_tn(sv["a_conf"], dyb, tn=D, tk=2048, name=f"wconf_dw{tag}")
    g["w_sc_out"] = _mm_tn(sv["a_sc"], dyc, tn=D, tk=2048, name=f"wsc_dw{tag}")
    comm_conf = make_comm_conf(g) if make_comm_conf is not None else None
    (dproj, g["conf_conv_w"], dcb, dlg, dlb), conf_comm_out = _conf_bwd(
        sv["proj"], sv["uc"], dac, dproj, p["conf_conv_w"], _row(p["conf_ln_g"]), _row(p["conf_ln_b"]),
        name=f"conf_bwd{tag}", comm=comm_conf)
    g["conf_conv_b"], g["conf_ln_g"], g["conf_ln_b"] = dcb[0], dlg[0], dlb[0]
    dproj, g["sc_conv_w"] = _sc_bwd(sv["proj"], dasc, dproj, p["sc_conv_w"], name=f"sc_bwd{tag}")
    dproj, ddt, dxbc_act, dng, ddtb, dalog, ddd = _ssd_bwd(
        sv["xbc_act"], sv["proj"], sv["y"], dyn, sv["hprev"], dproj, _head_row(p["ssd_dt_bias"]),
        _head_row(p["ssd_a_log"]), sv["d_x"], _row(p["ssd_norm_g"]), name=f"ssd_bwd{tag}")
    g["ssd_norm_g"], g["ssd_dt_bias"], g["ssd_a_log"], g["ssd_d"] = dng[0], ddtb[0, :HEADS], dalog[0, :HEADS], ddd[0, :HEADS]
    dproj, g["ssd_conv_w"], dsb = _ssd_pre_bwd(sv["proj"], sv["cpre"], dxbc_act, ddt, dproj, p["ssd_conv_w"],
                                               name=f"ssd_pre_bwd{tag}")
    g["ssd_conv_b"] = dsb[0]
    dh = _mm_nt(dproj, p["w_in_pad"], out_dtype=BF, tm=1024, tk=4096, name=f"proj_dx{tag}")
    g["w_in_pad"] = _mm_tn(sv["h"], dproj, tn=2048, tk=2048, name=f"proj_dw{tag}")
    dx0, dgn, dsh, dsc = _modnorm_bwd(dh, sv["x0"], dx1, _row(p["norm_mix_g"]), mod_mix, name=f"modnorm_mix_bwd{tag}")
    g["norm_mix_g"] = dgn[0]
    dmod_mix = jnp.concatenate([dsh[0], dsc[0], dgate_mix[0]])
    return dx0, g, dmod_mix, dmod_ffn, (comm_out, conf_comm_out)


MESH = pl.DeviceIdType.MESH
ANY = pl.BlockSpec(memory_space=pl.ANY)
VMEM = pl.BlockSpec(memory_space=pltpu.VMEM)


def _mesh_pos():
    return lax.axis_index("x"), lax.axis_index("y"), lax.axis_index("c")


def _peer(pos, mask):
    return tuple(1 - v if (mask >> (2 - k)) & 1 else v for k, v in enumerate(pos))


def _lin(pos):
    return 4 * pos[0] + 2 * pos[1] + pos[2]


def _chip(pos):
    return 2 * pos[0] + pos[1]


def _rcopy(src, dst, send_sem, recv_sem, dev):
    return pltpu.make_async_remote_copy(src_ref=src, dst_ref=dst, send_sem=send_sem, recv_sem=recv_sem,
                                        device_id=dev, device_id_type=MESH)


CHIP_MASKS = (2, 4, 6)
SIBLING = 1
ADA_COLS = 3 * D // N_CHIPS
CONV_ROWS, CONV_COLS = 48, 2 * DFF // N_CHIPS
CONV_PACK = {"ffn_conv_w": (0, FFN_K, 2 * DFF // N_CHIPS), "ssd_conv_w": (3, SSD_K, XBC // N_CHIPS),
             "conf_conv_w": (8, CONF_K, CONF_W // N_CHIPS), "sc_conv_w": (40, SC_K, SC_W // N_CHIPS)}


def _ada_exchange(c_blk, ada_mix_w, ada_ffn_w, conv_pack):
    def body(c_ref, wm_ref, wf_ref, cw_ref, mods_ref, sc_ref, cwall_ref,
             call_scr, modp_scr, recv_scr, s1, r1, s3, r3, s4, r4):
        pos = _mesh_pos()
        me, km = _lin(pos), _chip(pos)
        call_scr[me] = c_ref[...]
        cwall_ref[km] = cw_ref[...]
        sends = []
        for m in range(1, N_DEV):
            sends.append(_rcopy(c_ref, call_scr.at[me], s1.at[m - 1], r1.at[m - 1], _peer(pos, m)))
        for j, m in enumerate(CHIP_MASKS):
            sends.append(_rcopy(cw_ref, cwall_ref.at[km], s4.at[j], r4.at[j], _peer(pos, m)))
        for cp in sends:
            cp.start()
        for m in range(1, N_DEV):
            src = _peer(pos, m)
            _rcopy(c_ref, call_scr.at[_lin(src)], s1.at[m - 1], r1.at[m - 1], src).wait_recv()
        cm = jnp.concatenate([call_scr[d, 0:1, :] for d in range(N_DEV)], axis=0)
        sc = cm * _sigmoid(cm)
        sc_ref[...] = sc
        for j, w in enumerate((wm_ref.at[0], wm_ref.at[1], wf_ref.at[0], wf_ref.at[1])):
            modp_scr[:, j * ADA_COLS:(j + 1) * ADA_COLS] = lax.dot_general(
                sc, w[...], NN, precision=HIGHEST, preferred_element_type=F32)
        recv_scr[km] = modp_scr[...]
        sends3 = [_rcopy(modp_scr, recv_scr.at[km], s3.at[j], r3.at[j], _peer(pos, m)) for j, m in enumerate(CHIP_MASKS)]
        for cp in sends3:
            cp.start()
        for j, m in enumerate(CHIP_MASKS):
            src = _peer(pos, m)
            _rcopy(modp_scr, recv_scr.at[_chip(src)], s3.at[j], r3.at[j], src).wait_recv()
            _rcopy(cw_ref, cwall_ref.at[_chip(src)], s4.at[j], r4.at[j], src).wait_recv()
        for k in range(N_CHIPS):
            mods_ref[k:k + 1, :] = recv_scr[k, pl.ds(me, 1), :]
        for cp in sends + sends3:
            cp.wait_send()

    dma = pltpu.SemaphoreType.DMA
    return pl.pallas_call(
        body, in_specs=[VMEM] * 4, out_specs=[VMEM] * 3,
        out_shape=[S((N_CHIPS, 4 * ADA_COLS), F32), S((N_DEV, D), F32), S((N_CHIPS,) + conv_pack.shape, F32)],
        scratch_shapes=[pltpu.VMEM((N_DEV, 8, D), F32), pltpu.VMEM((N_DEV, 4 * ADA_COLS), F32),
                        pltpu.VMEM((N_CHIPS, N_DEV, 4 * ADA_COLS), F32),
                        dma((N_DEV - 1,)), dma((N_DEV - 1,)), dma((3,)), dma((3,)), dma((3,)), dma((3,))],
        compiler_params=pltpu.CompilerParams(vmem_limit_bytes=VMEM_LIMIT_V7X), name="ada_exchange")(
            c_blk, ada_mix_w, ada_ffn_w, conv_pack)


class _Comm:
    def __init__(self, ins, out_shapes, scratch, start, finish, mid=None):
        self.ins, self.out_shapes, self.scratch, self.start, self.finish = list(ins), list(out_shapes), list(scratch), start, finish
        self.mid = mid


def _call_with_comm(body, nsteps, *, in_specs, out_specs, out_shape, scratch_shapes, args, comm, name, aliases=None):
    grid = nsteps if isinstance(nsteps, tuple) else (nsteps,)
    sem = ("arbitrary",) * len(grid)
    if comm is None:
        res = pl.pallas_call(body, grid=grid, in_specs=in_specs, out_specs=out_specs, out_shape=out_shape,
                             scratch_shapes=scratch_shapes, input_output_aliases=aliases or {},
                             compiler_params=_cparams(*sem), name=name)(*args)
        return tuple(res), ()
    ni, no, ns = len(in_specs), len(out_specs), len(scratch_shapes)
    ci, co = len(comm.ins), len(comm.out_shapes)

    def hosted(*refs):
        ins, cins = refs[:ni], refs[ni:ni + ci]
        outs, couts = refs[ni + ci:ni + ci + no], refs[ni + ci + no:ni + ci + no + co]
        scr, csem = refs[ni + ci + no + co:ni + ci + no + co + ns], refs[ni + ci + no + co + ns:]
        step, total = pl.program_id(0), grid[0]
        for d in range(1, len(grid)):
            step, total = step * grid[d] + pl.program_id(d), total * grid[d]

        @pl.when(step == 0)
        def _():
            comm.start(cins, couts, csem)

        body(*ins, *outs, *scr)
        if comm.mid is not None:
            @pl.when(step == max(0, total - 1 - max(2, total // 6)))
            def _():
                comm.mid(cins, couts, csem)

        @pl.when(step == total - 1)
        def _():
            comm.finish(cins, couts, csem)

    res = pl.pallas_call(
        hosted, grid=grid, in_specs=list(in_specs) + [ANY] * ci, out_specs=list(out_specs) + [ANY] * co,
        out_shape=list(out_shape) + comm.out_shapes, scratch_shapes=list(scratch_shapes) + comm.scratch,
        input_output_aliases=aliases or {}, compiler_params=_cparams(*sem), name=name)(*args, *comm.ins)
    return tuple(res[:no]), tuple(res[no:])


def _run_comm(comm, name):
    def body(*refs):
        ci, co = len(comm.ins), len(comm.out_shapes)
        comm.start(refs[:ci], refs[ci:ci + co], refs[ci + co:])
        if comm.mid is not None:
            comm.mid(refs[:ci], refs[ci:ci + co], refs[ci + co:])
        comm.finish(refs[:ci], refs[ci:ci + co], refs[ci + co:])

    return pl.pallas_call(body, in_specs=[ANY] * len(comm.ins), out_specs=[ANY] * len(comm.out_shapes),
                          out_shape=comm.out_shapes, scratch_shapes=comm.scratch, name=name)(*comm.ins)


def _gather_comm(shards, layer):
    na = len(shards)
    dma = pltpu.SemaphoreType.DMA

    def ici(ins, outs, sems, pos, a, j, m, sender):
        ssem, rsem = sems[0], sems[1]
        peer = _peer(pos, m)
        block = outs[a].at[_chip(pos) if sender else _chip(peer)]
        return _rcopy(ins[a].at[layer], block, ssem.at[a, j], rsem.at[a, j], peer)

    def start(ins, outs, sems):
        pos = _mesh_pos()

        @pl.when(pos[2] == layer)
        def _():
            for a in range(na):
                for j, m in enumerate(CHIP_MASKS):
                    ici(ins, outs, sems, pos, a, j, m, True).start()

    def forward(outs, sems, pos, a, j, m):
        blk = outs[a].at[_chip(_peer(pos, m))]
        return _rcopy(blk, blk, sems[2].at[a, j], sems[3].at[a, j], _peer(pos, SIBLING))

    def mid(ins, outs, sems):
        pos = _mesh_pos()

        @pl.when(pos[2] == layer)
        def _():
            for a in range(na):
                for j, m in enumerate(CHIP_MASKS):
                    ici(ins, outs, sems, pos, a, j, m, False).wait_recv()
                    forward(outs, sems, pos, a, j, m).start()

    def finish(ins, outs, sems):
        pos = _mesh_pos()

        @pl.when(pos[2] == layer)
        def _():
            for a in range(na):
                for j, m in enumerate(CHIP_MASKS):
                    ici(ins, outs, sems, pos, a, j, m, True).wait_send()
                    forward(outs, sems, pos, a, j, m).wait_send()

        @pl.when(pos[2] != layer)
        def _():
            for a in range(na):
                for j, m in enumerate(CHIP_MASKS):
                    forward(outs, sems, pos, a, j, m).wait_recv()

    return _Comm(shards, [S((N_CHIPS,) + s.shape[1:], s.dtype) for s in shards],
                 [dma((na, 3)), dma((na, 3)), dma((na, 3)), dma((na, 3))], start, finish, mid)


def _scatter_comm(arrs, layer):
    na = len(arrs)
    dma = pltpu.SemaphoreType.DMA

    def copy(ins, outs, sems, pos, a, j, m, sender):
        peer = _peer(pos, m)
        src = ins[a].at[_chip(peer) if sender else _chip(pos)]
        dst = outs[a].at[_chip(pos) if sender else _chip(peer)]
        return _rcopy(src, dst, sems[0].at[a, j], sems[1].at[a, j], peer)

    def start(ins, outs, sems):
        pos = _mesh_pos()

        @pl.when(pos[2] == layer)
        def _():
            for a in range(na):
                for j, m in enumerate(CHIP_MASKS):
                    copy(ins, outs, sems, pos, a, j, m, True).start()

    def finish(ins, outs, sems):
        pos = _mesh_pos()

        @pl.when(pos[2] == layer)
        def _():
            for a in range(na):
                for j, m in enumerate(CHIP_MASKS):
                    copy(ins, outs, sems, pos, a, j, m, False).wait_recv()
            for a in range(na):
                for j, m in enumerate(CHIP_MASKS):
                    copy(ins, outs, sems, pos, a, j, m, True).wait_send()

    return _Comm(arrs, [S(s.shape, s.dtype) for s in arrs], [dma((na, 3)), dma((na, 3))], start, finish)


def _swap_layer(arrs, layer, tag):
    na = len(arrs)

    def body(*refs):
        ins, outs = refs[:na], refs[na:2 * na]
        ssem, rsem = refs[2 * na:]
        pos = _mesh_pos()
        sib = _peer(pos, SIBLING)
        cps = [_rcopy(ins[a], outs[a], ssem.at[a], rsem.at[a], sib) for a in range(na)]

        @pl.when(pos[2] != layer)
        def _():
            for cp in cps:
                cp.start()
            for cp in cps:
                cp.wait_send()

        @pl.when(pos[2] == layer)
        def _():
            for cp in cps:
                cp.wait_recv()

    dma = pltpu.SemaphoreType.DMA
    return pl.pallas_call(
        body, in_specs=[ANY] * na, out_specs=[ANY] * na, out_shape=[S(s.shape, s.dtype) for s in arrs],
        scratch_shapes=[dma((na,)), dma((na,))], name=f"swap_layer{tag}")(*arrs)


def _share_sibling(arrs):
    na = len(arrs)

    def body(*refs):
        bufs = refs[na:2 * na]
        ssem, rsem = refs[2 * na:]
        pos = _mesh_pos()
        c = pos[2]
        sib = _peer(pos, SIBLING)
        sends = [_rcopy(bufs[a].at[c], bufs[a].at[c], ssem.at[a], rsem.at[a], sib) for a in range(na)]
        for cp in sends:
            cp.start()
        for a in range(na):
            _rcopy(bufs[a].at[c], bufs[a].at[1 - c], ssem.at[a], rsem.at[a], sib).wait_recv()
        for cp in sends:
            cp.wait_send()

    dma = pltpu.SemaphoreType.DMA
    return pl.pallas_call(
        body, in_specs=[ANY] * na, out_specs=[ANY] * na, out_shape=[S(s.shape, s.dtype) for s in arrs],
        input_output_aliases={a: a for a in range(na)},
        scratch_shapes=[dma((na,)), dma((na,))], name="share_sibling")(*arrs)


def _small_allreduce(vec):
    r = vec.shape[0]

    def body(v_ref, sum_ref, all_ref, ssem, rsem):
        pos = _mesh_pos()
        me = _lin(pos)
        all_ref[me] = v_ref[...]
        cps = [_rcopy(v_ref, all_ref.at[me], ssem.at[m - 1], rsem.at[m - 1], _peer(pos, m)) for m in range(1, N_DEV)]
        for cp in cps:
            cp.start()
        for m in range(1, N_DEV):
            src = _peer(pos, m)
            _rcopy(v_ref, all_ref.at[_lin(src)], ssem.at[m - 1], rsem.at[m - 1], src).wait_recv()
        acc = all_ref[0]
        for d in range(1, N_DEV):
            acc = acc + all_ref[d]
        sum_ref[...] = acc
        for cp in cps:
            cp.wait_send()

    dma = pltpu.SemaphoreType.DMA
    return pl.pallas_call(
        body, in_specs=[VMEM], out_specs=[VMEM, VMEM],
        out_shape=[S((r, 128), F32), S((N_DEV, r, 128), F32)],
        scratch_shapes=[dma((N_DEV - 1,)), dma((N_DEV - 1,))],
        compiler_params=pltpu.CompilerParams(vmem_limit_bytes=VMEM_LIMIT_V7X), name="small_allreduce")(vec)


ROW_BYTES_TARGET = 1 << 20


def _row_tile(rows, cols, itemsize=4, align=16, target=ROW_BYTES_TARGET):
    fits = [t for t in range(align, rows + 1, align) if rows % t == 0]
    under = [t for t in fits if t * cols * itemsize <= target]
    return max(under) if under else (min(fits) if fits else rows)


def _pair_add(g, other):
    r, cdim = g.shape
    tr = _row_tile(r, cdim, 1)

    def body(g_ref, o_ref, out_ref):
        out_ref[...] = (g_ref[...].astype(F32) + o_ref[...].astype(F32)).astype(out_ref.dtype)

    blk = pl.BlockSpec((tr, cdim), lambda i: (i, 0))
    return pl.pallas_call(body, grid=(r // tr,), in_specs=[blk, blk], out_specs=blk, out_shape=S((r, cdim), BF),
                          compiler_params=_cparams("parallel"), name="pair_add")(g, other)


def _sum4(parts, pairs, chip_core):
    _, r, cdim = parts[0].shape
    tr = _row_tile(r, cdim)

    def body(kc_ref, q0_ref, q1_ref, own0_ref, own1_ref, out_ref):
        first = kc_ref[1] == 0
        mine = jnp.where(first, own0_ref[...], own1_ref[...]).astype(F32)
        terms = [jnp.where(kc_ref[0] == j, mine, jnp.where(first, q0_ref[j], q1_ref[j]).astype(F32))
                 for j in range(N_CHIPS)]
        out_ref[...] = ((terms[0] + terms[1]) + terms[2]) + terms[3]

    allc = pl.BlockSpec((N_CHIPS, tr, cdim), lambda i, kc: (0, i, 0))
    own = pl.BlockSpec((None, tr, cdim), lambda i, kc: (kc[0], i, 0))
    return pl.pallas_call(
        body,
        grid_spec=pltpu.PrefetchScalarGridSpec(
            num_scalar_prefetch=1, grid=(r // tr,), in_specs=[allc, allc, own, own],
            out_specs=pl.BlockSpec((None, tr, cdim), lambda i, kc: (kc[1], i, 0))),
        out_shape=S((DEPTH, r, cdim), F32), compiler_params=_cparams("parallel"), name="sum4")(
            chip_core, parts[0], parts[1], pairs[0], pairs[1])


def _ada_w_grad(silu_c, dmod_cols, chip):
    def body(k_ref, sc_ref, dm_ref, o_ref):
        del k_ref
        o_ref[...] = lax.dot_general(sc_ref[...], dm_ref[...], TN, precision=HIGHEST, preferred_element_type=F32)

    return pl.pallas_call(
        body,
        grid_spec=pltpu.PrefetchScalarGridSpec(
            num_scalar_prefetch=1, grid=(4,),
            in_specs=[pl.BlockSpec((N_DEV, D), lambda j, k: (0, 0)),
                      pl.BlockSpec((None, N_DEV, ADA_COLS), lambda j, k: (4 * j + k[0], 0, 0))],
            out_specs=pl.BlockSpec((None, D, ADA_COLS), lambda j, k: (j, 0, 0))),
        out_shape=S((4, D, ADA_COLS), F32), compiler_params=_cparams("parallel"), name="ada_w_grad")(chip, silu_c, dmod_cols)


def _adamw(w, g, m, v, *, name):
    r = w.shape[0]
    rest = w.shape[1:]
    row_elems = 1
    for d in rest:
        row_elems *= d
    tr = _row_tile(r, row_elems, align=8 if len(rest) == 1 else 1, target=5 * ROW_BYTES_TARGET // 2)
    c1 = 1.0 / (1.0 - ADAM_B1 ** ADAM_STEP)
    c2 = 1.0 / (1.0 - ADAM_B2 ** ADAM_STEP)

    def body(w_ref, g_ref, m_ref, v_ref, d_ref, mo_ref, vo_ref):
        gv = g_ref[...]
        mn = ADAM_B1 * m_ref[...] + (1.0 - ADAM_B1) * gv
        vn = ADAM_B2 * v_ref[...] + (1.0 - ADAM_B2) * (gv * gv)
        mo_ref[...] = mn
        vo_ref[...] = vn
        d_ref[...] = -ADAM_LR * ((mn * c1) / (jnp.sqrt(vn * c2) + ADAM_EPS) + ADAM_WD * w_ref[...])

    zeros = (0,) * len(rest)
    blk = pl.BlockSpec((tr,) + rest, lambda i: (i,) + zeros)
    return pl.pallas_call(
        body, grid=(r // tr,), in_specs=[blk] * 4, out_specs=[blk] * 3, out_shape=[S(w.shape, F32)] * 3,
        compiler_params=_cparams("parallel"), name=name)(w, g, m, v)


def _adamw_many(ws, gs, ms, vs):
    n = len(ws)
    c1 = 1.0 / (1.0 - ADAM_B1 ** ADAM_STEP)
    c2 = 1.0 / (1.0 - ADAM_B2 ** ADAM_STEP)

    def body(*refs):
        for i in range(n):
            w_ref, g_ref, m_ref, v_ref, d_ref, mo_ref, vo_ref = (refs[k * n + i] for k in range(7))
            gv = g_ref[...]
            mn = ADAM_B1 * m_ref[...] + (1.0 - ADAM_B1) * gv
            vn = ADAM_B2 * v_ref[...] + (1.0 - ADAM_B2) * (gv * gv)
            mo_ref[...] = mn
            vo_ref[...] = vn
            d_ref[...] = -ADAM_LR * ((mn * c1) / (jnp.sqrt(vn * c2) + ADAM_EPS) + ADAM_WD * w_ref[...])

    shapes = [S(a.shape, F32) for a in ws]
    outs = pl.pallas_call(
        body, in_specs=[VMEM] * (4 * n), out_specs=[VMEM] * (3 * n), out_shape=shapes * 3,
        compiler_params=pltpu.CompilerParams(vmem_limit_bytes=VMEM_LIMIT_V7X), name="adamw_small")(*ws, *gs, *ms, *vs)
    return outs[0:n], outs[n:2 * n], outs[2 * n:3 * n]


WEIGHTS = ['ada_mix_w', 'ada_mix_b', 'norm_mix_g', 'w_in', 'b_gate', 'ssd_conv_w', 'ssd_conv_b', 'ssd_dt_bias',
           'ssd_a_log', 'ssd_d', 'ssd_norm_g', 'w_ssd_out', 'conf_conv_w', 'conf_conv_b', 'conf_ln_g', 'conf_ln_b',
           'w_conf_out', 'sc_conv_w', 'w_sc_out', 'w_o', 'ada_ffn_w', 'ada_ffn_b', 'norm_ffn_g', 'w_up', 'ffn_conv_w',
           'ffn_conv_b', 'w_down', 'final_norm_g']
SMALL = ['ada_mix_b', 'norm_mix_g', 'b_gate', 'ssd_conv_b', 'ssd_dt_bias', 'ssd_a_log', 'ssd_d', 'ssd_norm_g', 'conf_conv_b',
         'conf_ln_g', 'conf_ln_b', 'ada_ffn_b', 'norm_ffn_g', 'ffn_conv_b']
CONVS = ['ssd_conv_w', 'conf_conv_w', 'sc_conv_w', 'ffn_conv_w']
BIG = ['ada_mix_w', 'ada_ffn_w', 'w_in', 'w_up', 'w_conf_out', 'w_sc_out', 'w_ssd_out', 'w_o', 'w_down']


def _pack_rows(pieces):
    flat = [p.reshape(-1) for p in pieces]
    offs, o = [], 0
    for f in flat:
        offs.append(o)
        o += f.shape[0]
    total = -(-o // 1024) * 1024
    vec = jnp.concatenate(flat + [jnp.zeros((total - o,), F32)])
    return vec.reshape(total // 128, 128), offs


def _by_chip(a, axis):
    shp = a.shape
    a = a.reshape(shp[:axis] + (N_CHIPS, shp[axis] // N_CHIPS) + shp[axis + 1:])
    return jnp.moveaxis(a, axis, 0)


def _from_chips(a, axis):
    a = jnp.moveaxis(a, 0, axis)
    shp = a.shape
    return a.reshape(shp[:axis] + (shp[axis] * shp[axis + 1],) + shp[axis + 2:])


def kernel(x, c, ada_mix_w, ada_mix_b, norm_mix_g, w_in, b_gate, ssd_conv_w, ssd_conv_b, ssd_dt_bias, ssd_a_log, ssd_d, ssd_norm_g, w_ssd_out, conf_conv_w, conf_conv_b, conf_ln_g, conf_ln_b, w_conf_out, sc_conv_w, w_sc_out, w_o, ada_ffn_w, ada_ffn_b, norm_ffn_g, w_up, ffn_conv_w, ffn_conv_b, w_down, final_norm_g, loss_target, m_ada_mix_w, m_ada_mix_b, m_norm_mix_g, m_w_in, m_b_gate, m_ssd_conv_w, m_ssd_conv_b, m_ssd_dt_bias, m_ssd_a_log, m_ssd_d, m_ssd_norm_g, m_w_ssd_out, m_conf_conv_w, m_conf_conv_b, m_conf_ln_g, m_conf_ln_b, m_w_conf_out, m_sc_conv_w, m_w_sc_out, m_w_o, m_ada_ffn_w, m_ada_ffn_b, m_norm_ffn_g, m_w_up, m_ffn_conv_w, m_ffn_conv_b, m_w_down, m_final_norm_g, v_ada_mix_w, v_ada_mix_b, v_norm_mix_g, v_w_in, v_b_gate, v_ssd_conv_w, v_ssd_conv_b, v_ssd_dt_bias, v_ssd_a_log, v_ssd_d, v_ssd_norm_g, v_w_ssd_out, v_conf_conv_w, v_conf_conv_b, v_conf_ln_g, v_conf_ln_b, v_w_conf_out, v_sc_conv_w, v_w_sc_out, v_w_o, v_ada_ffn_w, v_ada_ffn_b, v_norm_ffn_g, v_w_up, v_ffn_conv_w, v_ffn_conv_b, v_w_down, v_final_norm_g):
    args = locals()
    w = {n: args[n] for n in WEIGHTS}
    mom = {n: args["m_" + n] for n in WEIGHTS}
    var = {n: args["v_" + n] for n in WEIGHTS}
    pos = _mesh_pos()
    chip = _chip(pos)
    core = pos[2]

    conv_pack = jnp.zeros((DEPTH, CONV_ROWS, CONV_COLS), F32)
    for n, (r0, taps, width) in CONV_PACK.items():
        conv_pack = conv_pack.at[:, r0:r0 + taps, 0:width].set(w[n])
    c_blk = jnp.pad(c, ((0, 7), (0, 0)))
    mods_raw, silu_c, conv_all = _ada_exchange(c_blk, ada_mix_w, ada_ffn_w, conv_pack)
    ada_b = jnp.concatenate([ada_mix_b, ada_ffn_b], axis=0)
    mod_all = mods_raw.reshape(N_CHIPS, 4, ADA_COLS).transpose(1, 0, 2).reshape(4, 3 * D) + ada_b
    mod_all = mod_all.reshape(4, 3, D)
    mods = [(mod_all[i], mod_all[2 + i]) for i in range(DEPTH)]
    conv_full = {n: _from_chips(conv_all[:, :, r0:r0 + taps, 0:width], 2) for n, (r0, taps, width) in CONV_PACK.items()}

    cast = lambda a: a.astype(BF)
    shards = [cast(w_in), jnp.concatenate([cast(w_conf_out), cast(w_sc_out)], axis=1),
              jnp.concatenate([cast(w_ssd_out), cast(w_o)], axis=1), cast(w_up), cast(w_down)]
    FIRST, MID, LATE = (0,), (1, 2), (3, 4)
    EARLY = FIRST + MID
    pick = lambda arrs, idx: [arrs[k] for k in idx]

    def own_block(l, gathered, idx):
        return [lax.dynamic_update_slice(g, shards[k][l][None], (chip, 0, 0)) for g, k in zip(gathered, idx)]

    def first_params(l, gathered):
        (g_in,) = own_block(l, gathered, FIRST)
        p = {n: w[n][l] for n in SMALL if not n.startswith("ada_")}
        p.update({n: conv_full[n][l] for n in CONVS})
        p["w_in_pad"] = _pad_w_in_chips(g_in)
        return p

    def mid_params(l, gathered):
        g_cs, g_row = own_block(l, gathered, MID)
        return {"w_conf_out": _from_chips(g_cs[:, 0:CONF_W], 1), "w_sc_out": _from_chips(g_cs[:, CONF_W:], 1),
                "w_ssd_out": _from_chips(g_row[:, 0:INNER // N_CHIPS], 0), "w_o": _from_chips(g_row[:, INNER // N_CHIPS:], 0)}

    def late_params(l, gathered):
        g_up, g_down = own_block(l, gathered, LATE)
        return {"w_up4": g_up, "w_down": _from_chips(g_down, 0)}

    def big_grads(g, idx):
        makers = (lambda: _unpad_w_in_chips(g["w_in_pad"]),
                  lambda: _by_chip(jnp.concatenate([g["w_conf_out"], g["w_sc_out"]], axis=0), 1),
                  lambda: jnp.concatenate([_by_chip(g["w_ssd_out"], 0), _by_chip(g["w_o"], 0)], axis=1),
                  lambda: g["w_up4"], lambda: _by_chip(g["w_down"], 0))
        return [makers[k]().astype(BF) for k in idx]

    def pair_sums(big, l, tag):
        theirs = _swap_layer(big, l, tag)
        out = []
        for g4, t4 in zip(big, theirs):
            k, r, cdim = g4.shape
            out.append(_pair_add(g4.reshape(k * r, cdim), t4.reshape(k * r, cdim)).reshape(k, r, cdim))
        return out

    seq = x.shape[1]
    xs = x.reshape(seq, D)
    p0 = first_params(0, _run_comm(_gather_comm(pick(shards, FIRST), 0), "gather_weights_l0"))
    xs, sv0, gathered1 = _layer_fwd(xs, p0, mods[0][0], mods[0][1], "_l0",
                                    comm_proj=_gather_comm(pick(shards, MID), 0),
                                    mid_params=lambda got: mid_params(0, got),
                                    comm_ssd=_gather_comm(pick(shards, LATE), 0),
                                    late_params=lambda got: late_params(0, got),
                                    comm_up=_gather_comm(pick(shards, EARLY), 1),
                                    comm=_gather_comm(pick(shards, LATE), 1))
    p1 = first_params(1, gathered1[0:1])
    p1.update(mid_params(1, gathered1[1:3]))
    p1.update(late_params(1, gathered1[3:5]))
    xs, sv1, _ = _layer_fwd(xs, p1, mods[1][0], mods[1][1], "_l1")
    dx, loss, dfg = _final_loss(xs, _row(final_norm_g), loss_target.reshape(seq, D), name="final_loss")
    dfinal = dfg[0]

    every = EARLY + LATE
    grads, dmods = [None, None], [None, None]
    dx, grads[1], dmm, dmf, _ = _layer_bwd(dx, p1, sv1, mods[1][0], mods[1][1], "_l1")
    dmods[1] = (dmm, dmf)
    pair1 = pair_sums(big_grads(grads[1], every), 1, "_l1")
    late0 = {}

    def scatter_late0(g):
        late0["pair"] = pair_sums(big_grads(g, MID + LATE), 0, "_l0_ffn")
        return _scatter_comm(late0["pair"], 0)

    dx, grads[0], dmm, dmf, (parts1, parts0_late) = _layer_bwd(
        dx, p0, sv0, mods[0][0], mods[0][1], "_l0", comm=_scatter_comm(pair1, 1), make_comm_conf=scatter_late0)
    dmods[0] = (dmm, dmf)
    pair0_early = pair_sums(big_grads(grads[0], FIRST), 0, "_l0")
    parts0_early = _run_comm(_scatter_comm(pair0_early, 0), "scatter_chips_l0")
    pair0 = list(pair0_early) + list(late0["pair"])
    parts0 = list(parts0_early) + list(parts0_late)
    chip_core = jnp.stack([chip, core]).astype(jnp.int32)
    red = _share_sibling([_sum4((q0, q1), (o0, o1), chip_core)
                          for q0, q1, o0, o1 in zip(parts0, parts1, pair0, pair1)])
    reduced = {"w_in": red[0], "cs": red[1], "row": red[2], "w_up": red[3], "w_down": red[4]}
    gw = {"w_in": reduced["w_in"], "w_up": reduced["w_up"], "w_down": reduced["w_down"],
          "w_conf_out": reduced["cs"][:, 0:CONF_W], "w_sc_out": reduced["cs"][:, CONF_W:],
          "w_ssd_out": reduced["row"][:, 0:INNER // N_CHIPS], "w_o": reduced["row"][:, INNER // N_CHIPS:]}

    dmod = jnp.stack([dmods[0][0], dmods[1][0], dmods[0][1], dmods[1][1]])
    small_local = {n: jnp.stack([grads[l][n] for l in range(DEPTH)]) for n in SMALL if not n.startswith("ada_")}
    pieces = [loss[0]] + [small_local[n] for n in SMALL if not n.startswith("ada_")]
    pieces += [jnp.stack([grads[l][n] for l in range(DEPTH)]) for n in CONVS] + [dfinal, dmod]
    vec, offs = _pack_rows(pieces)
    vsum, vall = _small_allreduce(vec)
    flat = vsum.reshape(-1)

    def piece(k, like):
        return flat[offs[k]:offs[k] + like.size].reshape(like.shape)

    loss_out = flat[0]
    k = 1
    for n in SMALL:
        if not n.startswith("ada_"):
            gw[n] = piece(k, small_local[n])
            k += 1
    for n in CONVS:
        full = piece(k, conv_full[n])
        width = CONV_PACK[n][2]
        gw[n] = lax.dynamic_slice_in_dim(full, chip * width, width, axis=2)
        k += 1
    gw["final_norm_g"] = piece(k, dfinal)
    k += 1
    dmod_sum = piece(k, dmod)
    gw["ada_mix_b"], gw["ada_ffn_b"] = dmod_sum[0:2], dmod_sum[2:4]
    dmod_all = vall.reshape(N_DEV, -1)[:, offs[k]:offs[k] + dmod.size]
    dmod_cols = dmod_all.reshape(N_DEV, 4 * N_CHIPS, ADA_COLS).transpose(1, 0, 2)
    ada_g = _ada_w_grad(silu_c, dmod_cols, jnp.reshape(chip, (1,)).astype(jnp.int32))
    gw["ada_mix_w"], gw["ada_ffn_w"] = ada_g[0:2], ada_g[2:4]

    delta, new_m, new_v = {}, {}, {}
    for n in BIG:
        shp = w[n].shape
        if shp[2] % 128:
            two_d = lambda a: jnp.transpose(a, (2, 0, 1))
            back = lambda a: jnp.transpose(a, (1, 2, 0))
        else:
            two_d = lambda a: a.reshape(shp[0] * shp[1], shp[2])
            back = lambda a: a.reshape(shp)
        d_, m_, v_ = _adamw(two_d(w[n]), two_d(gw[n]), two_d(mom[n]), two_d(var[n]), name=f"adamw_{n}")
        delta[n], new_m[n], new_v[n] = back(d_), back(m_), back(v_)
    rest = [n for n in WEIGHTS if n not in BIG]
    two_d = lambda a: a.reshape(1, -1) if a.ndim == 1 else a
    outs = _adamw_many(*[[two_d(src[n]) for n in rest] for src in (w, gw, mom, var)])
    for dst, group in zip((delta, new_m, new_v), outs):
        for n, o in zip(rest, group):
            dst[n] = o.reshape(w[n].shape)

    return (loss_out, dx[None], *[gw[n] for n in WEIGHTS], *[delta[n] for n in WEIGHTS],
            *[new_m[n] for n in WEIGHTS], *[new_v[n] for n in WEIGHTS])
```
